```python
import math
import jax, jax.numpy as jnp
from jax import lax
import numpy as np

D_MODEL = 1024
BATCH = 8
SEQ = 8192
DEPTH = 2

D_MIX = D_MODEL
HEAD = 64
C_HY = D_MIX // 2
C_RW = D_MIX - C_HY
N_HY_GROUPS = C_HY // HEAD
N_RW_HEADS = C_RW // HEAD
HY_ORDER = 2
HY_PROJ = (HY_ORDER + 1) * C_HY
HY_EMB = 33
HY_FILTER_WIDTH = 64
HY_TARGET = 1e-2
HY_FAST_PCT = 0.3
HY_SLOW_PCT = 1.5
RW_DECAY_LORA = 64
RW_AAA_LORA = 64
RW_GATE_LORA = 128
RW_MV_LORA = 32
RW_PROJ = 3 * C_RW + 2 * RW_DECAY_LORA + 2 * RW_AAA_LORA + RW_GATE_LORA
P_IN = HY_PROJ + RW_PROJ
D_FF = 4 * D_MODEL
RMS_EPS = 1e-5
GN_EPS = HEAD * 1e-5

kernel_name = 'hymba_hyena_rwkv7_bidir_encoder'


def f32(a):
    return a.astype(jnp.float32)


def rmsnorm(x, g):
    xf = f32(x)
    y = xf * lax.rsqrt(jnp.mean(xf * xf, axis=-1, keepdims=True) + RMS_EPS)
    return (y * f32(g)).astype(x.dtype)


def shift_prev(u):
    return jnp.pad(u, ((0, 0), (1, 0), (0, 0)))[:, :-1]


def shift_next(u):
    return jnp.pad(u, ((0, 0), (0, 1), (0, 0)))[:, 1:]


def hyena_filters(T, w1, b1, w2, b2, w3, b3, freq, w_out):
    pos = jnp.arange(T, dtype=jnp.float32)
    t = pos / max(T - 1, 1)
    bands = (HY_EMB - 1) // 2
    fr = jnp.linspace(1e-4, bands - 1, bands, dtype=jnp.float32)
    ang = (2.0 * math.pi / T) * pos[:, None] * fr[None, :]
    z = jnp.concatenate([t[:, None], jnp.cos(ang), -jnp.sin(ang)], axis=-1)
    h = jnp.sin(f32(freq[0]) * (z @ f32(w1) + f32(b1)))
    h = jnp.sin(f32(freq[1]) * (h @ f32(w2) + f32(b2)))
    h = jnp.sin(f32(freq[2]) * (h @ f32(w3) + f32(b3)))
    h = (h @ f32(w_out)).reshape(T, HY_ORDER, 2, C_HY)
    max_decay = math.log(HY_TARGET) / HY_FAST_PCT
    min_decay = math.log(HY_TARGET) / HY_SLOW_PCT
    delta = jnp.abs(jnp.linspace(min_decay, max_decay, C_HY, dtype=jnp.float32))
    window = jnp.exp(-t[:, None] * delta[None, :])
    h = h * window[:, None, None, :]
    fwd = h[:, :, 0]
    bwd = h[:, :, 1]
    two = jnp.concatenate([fwd, jnp.zeros_like(fwd[:1]), bwd[1:][::-1]], axis=0)
    two = two * lax.rsqrt(jnp.sum(two * two, axis=0, keepdims=True) + 1e-6)
    return jnp.fft.rfft(two, axis=0)


def long_conv(u, h_f, skip):
    T = u.shape[1]
    y = jnp.fft.irfft(jnp.fft.rfft(u, n=2 * T, axis=1) * h_f[None], n=2 * T, axis=1)[:, :T]
    return y + u * f32(skip)


def hyena_mixer(p, conv_w, conv_b, fw1, fb1, fw2, fb2, fw3, fb3, ffreq, fwout, skip, norm_g):
    B, T, _ = p.shape
    pf = f32(p)
    cw = f32(conv_w)
    u = cw[0] * shift_prev(pf) + cw[1] * pf + cw[2] * shift_next(pf) + f32(conv_b)
    x1, x2, v = jnp.split(u, 3, axis=-1)
    h_f = hyena_filters(T, fw1, fb1, fw2, fb2, fw3, fb3, ffreq, fwout)
    z = x1 * long_conv(v, h_f[:, 0], skip[0])
    z = x2 * long_conv(z, h_f[:, 1], skip[1])
    zg = z.reshape(B, T, N_HY_GROUPS, HEAD)
    zg = zg * lax.rsqrt(jnp.mean(zg * zg, axis=-1, keepdims=True) + RMS_EPS)
    return (zg.reshape(B, T, C_HY) * f32(norm_g)).astype(p.dtype)


def wkv7_scan(r, decay, k, v, kk, a, reverse):
    B, T, H, N = r.shape

    def step(S, inp):
        r_t, w_t, k_t, v_t, kk_t, a_t = inp
        sa = jnp.einsum('bhvk,bhk->bhv', S, -kk_t)
        S = (S * w_t[:, :, None, :] + sa[..., None] * (kk_t * a_t)[:, :, None, :]
             + v_t[..., None] * k_t[:, :, None, :])
        return S, jnp.einsum('bhvk,bhk->bhv', S, r_t)

    S0 = jnp.zeros((B, H, N, N), jnp.float32)
    xs = tuple(jnp.moveaxis(z, 1, 0) for z in (r, decay, k, v, kk, a))
    _, ys = lax.scan(step, S0, xs, reverse=reverse)
    return jnp.moveaxis(ys, 0, 1)


def rwkv7_mixer(p, hn, v_first, shift, w0, w2, a0, a2, g2, k_k, k_a, r_k, lnx_g, lnx_b, vres):
    B, T, _ = p.shape
    out_dtype = p.dtype
    pf = f32(p)
    mu = f32(shift)
    pf = pf + mu[0] * (shift_prev(pf) - pf) + mu[1] * (shift_next(pf) - pf)
    cuts = [C_RW, 2 * C_RW, 3 * C_RW, 3 * C_RW + 2 * RW_DECAY_LORA,
            3 * C_RW + 2 * RW_DECAY_LORA + 2 * RW_AAA_LORA]
    r, k, v, wd, ad, gd = jnp.split(pf, cuts, axis=-1)
    if vres is None:
        v_first = v
    else:
        v0, v1, v2 = vres
        v = v + (v_first - v) * jax.nn.sigmoid(f32(v0) + (f32(hn) @ f32(v1)) @ f32(v2))
    g = jax.nn.sigmoid(gd) @ f32(g2)

    def heads(z):
        return z.reshape(B, T, N_RW_HEADS, HEAD)

    kk = heads(k * f32(k_k))
    kk = kk / jnp.maximum(jnp.linalg.norm(kk, axis=-1, keepdims=True), 1e-12)
    vh = heads(v)
    states = []
    bonuses = []
    for d in range(2):
        wd_d = wd[..., d * RW_DECAY_LORA:(d + 1) * RW_DECAY_LORA]
        ad_d = ad[..., d * RW_AAA_LORA:(d + 1) * RW_AAA_LORA]
        w = -jax.nn.softplus(-(f32(w0[d]) + jnp.tanh(wd_d) @ f32(w2[d]))) - 0.5
        decay = jnp.exp(-jnp.exp(w))
        a = jax.nn.sigmoid(f32(a0[d]) + ad_d @ f32(a2[d]))
        k_d = k * (1.0 + (a - 1.0) * f32(k_a))
        states.append(wkv7_scan(heads(r), heads(decay), heads(k_d), vh, kk, heads(a), d == 1))
        bonuses.append(jnp.sum(heads(r * k_d * f32(r_k)), axis=-1, keepdims=True) * vh)
    s = states[0] + states[1]
    mean = jnp.mean(s, axis=-1, keepdims=True)
    var = jnp.mean(jnp.square(s - mean), axis=-1, keepdims=True)
    s = (s - mean) * lax.rsqrt(var + GN_EPS)
    y = s.reshape(B, T, C_RW) * f32(lnx_g) + f32(lnx_b)
    y = y + (bonuses[0] + bonuses[1]).reshape(B, T, C_RW)
    return (y * g).astype(out_dtype), v_first


def setup_inputs(seed: int = 0) -> dict:
    key = jax.random.key(seed)
    ks = iter(jax.random.split(key, 48))

    def nrm(shape, scale):
        return scale * jax.random.normal(next(ks), shape, jnp.float32)

    def uni(shape, lo, hi):
        return jax.random.uniform(next(ks), shape, jnp.float32, lo, hi)

    L = DEPTH
    ratio = jnp.arange(C_RW, dtype=jnp.float32) / (C_RW - 1)
    decay_base = -6.0 + 5.0 * ratio ** 0.9
    return {
        'x': nrm((BATCH, SEQ, D_MODEL), 1.0),
        'norm1_g': 1.0 + nrm((L, D_MODEL), 0.02),
        'w_in': nrm((L, D_MODEL, P_IN), D_MODEL ** -0.5),
        'hy_conv_w': nrm((L, 3, HY_PROJ), 3 ** -0.5),
        'hy_conv_b': nrm((L, HY_PROJ), 0.02),
        'hy_f_w1': nrm((L, HY_EMB, HY_FILTER_WIDTH), HY_EMB ** -0.5),
        'hy_f_b1': nrm((L, HY_FILTER_WIDTH), 0.1),
        'hy_f_w2': nrm((L, HY_FILTER_WIDTH, HY_FILTER_WIDTH), HY_FILTER_WIDTH ** -0.5),
        'hy_f_b2': nrm((L, HY_FILTER_WIDTH), 0.1),
        'hy_f_w3': nrm((L, HY_FILTER_WIDTH, HY_FILTER_WIDTH), HY_FILTER_WIDTH ** -0.5),
        'hy_f_b3': nrm((L, HY_FILTER_WIDTH), 0.1),
        'hy_f_freq': 1.0 + nrm((L, 3, HY_FILTER_WIDTH), 0.1),
        'hy_f_wout': nrm((L, HY_FILTER_WIDTH, HY_ORDER * 2 * C_HY), HY_FILTER_WIDTH ** -0.5),
        'hy_skip': nrm((L, HY_ORDER, C_HY), 0.5),
        'hy_norm_g': 1.0 + nrm((L, C_HY), 0.02),
        'rw_shift': uni((L, 2, RW_PROJ), 0.0, 0.5),
        'rw_w0': decay_base + nrm((L, 2, C_RW), 0.1),
        'rw_w2': nrm((L, 2, RW_DECAY_LORA, C_RW), 0.1),
        'rw_a0': nrm((L, 2, C_RW), 0.1),
        'rw_a2': nrm((L, 2, RW_AAA_LORA, C_RW), 0.1),
        'rw_g2': nrm((L, RW_GATE_LORA, C_RW), RW_GATE_LORA ** -0.5),
        'rw_k_k': 0.85 + nrm((L, C_RW), 0.02),
        'rw_k_a': 1.0 + nrm((L, C_RW), 0.02),
        'rw_r_k': nrm((L, C_RW), 0.1),
        'rw_lnx_g': 1.0 + nrm((L, C_RW), 0.02),
        'rw_lnx_b': nrm((L, C_RW), 0.02),
        'rw_v0': 1.0 + nrm((L - 1, C_RW), 0.1),
        'rw_v1': nrm((L - 1, D_MODEL, RW_MV_LORA), D_MODEL ** -0.5),
        'rw_v2': nrm((L - 1, RW_MV_LORA, C_RW), 0.1),
        'w_out': nrm((L, D_MIX, D_MODEL), D_MIX ** -0.5),
        'norm2_g': 1.0 + nrm((L, D_MODEL), 0.02),
        'mlp_w1': nrm((L, D_MODEL, D_FF), D_MODEL ** -0.5),
        'mlp_w2': nrm((L, D_FF, D_MODEL), D_FF ** -0.5),
        'final_g': 1.0 + nrm((D_MODEL,), 0.02),
    }


def reference(x, norm1_g, w_in, hy_conv_w, hy_conv_b, hy_f_w1, hy_f_b1, hy_f_w2, hy_f_b2,
              hy_f_w3, hy_f_b3, hy_f_freq, hy_f_wout, hy_skip, hy_norm_g, rw_shift, rw_w0, rw_w2,
              rw_a0, rw_a2, rw_g2, rw_k_k, rw_k_a, rw_r_k, rw_lnx_g, rw_lnx_b, rw_v0, rw_v1, rw_v2,
              w_out, norm2_g, mlp_w1, mlp_w2, final_g):
    v_first = None
    for l in range(DEPTH):
        hn = rmsnorm(x, norm1_g[l])
        p = hn @ w_in[l]
        p_hy, p_rw = p[..., :HY_PROJ], p[..., HY_PROJ:]
        y_hy = hyena_mixer(p_hy, hy_conv_w[l], hy_conv_b[l], hy_f_w1[l], hy_f_b1[l], hy_f_w2[l],
                           hy_f_b2[l], hy_f_w3[l], hy_f_b3[l], hy_f_freq[l], hy_f_wout[l],
                           hy_skip[l], hy_norm_g[l])
        vres = None if l == 0 else (rw_v0[l - 1], rw_v1[l - 1], rw_v2[l - 1])
        y_rw, v_first = rwkv7_mixer(p_rw, hn, v_first, rw_shift[l], rw_w0[l], rw_w2[l], rw_a0[l],
                                    rw_a2[l], rw_g2[l], rw_k_k[l], rw_k_a[l], rw_r_k[l],
                                    rw_lnx_g[l], rw_lnx_b[l], vres)
        x = x + (jnp.concatenate([y_hy, y_rw], axis=-1) @ w_out[l]).astype(x.dtype)
        hn = rmsnorm(x, norm2_g[l])
        x = x + (jnp.square(jax.nn.relu(hn @ mlp_w1[l])) @ mlp_w2[l]).astype(x.dtype)
    return rmsnorm(x, final_g)
```

```python
import functools
import math

import jax
import jax.numpy as jnp
from jax import lax
from jax.experimental import pallas as pl
from jax.experimental.pallas import tpu as pltpu

F32 = jnp.float32
BF16 = jnp.bfloat16
HP = lax.Precision.HIGHEST

HEAD = 64
HEADS_PER_GROUP = 4
GW = HEAD * HEADS_PER_GROUP
CHUNK = 64
RMS_EPS = 1e-5
GN_EPS = HEAD * 1e-5
HY_TARGET = 1e-2
HY_FAST_PCT = 0.3
HY_SLOW_PCT = 1.5
VMEM_LIMIT = 56 * 1024 * 1024


def _params(sem, vmem=VMEM_LIMIT):
    return pltpu.CompilerParams(dimension_semantics=sem, vmem_limit_bytes=vmem)


def _dot(a, b, precision=None):
    return jnp.dot(a, b, preferred_element_type=F32, precision=precision)


def _dot_nt(a, b, precision=None):
    return lax.dot_general(a, b, (((1,), (1,)), ((), ())), preferred_element_type=F32,
                           precision=precision)


def _dot_tn(a, b, precision=None):
    return lax.dot_general(a, b, (((0,), (0,)), ((), ())), preferred_element_type=F32,
                           precision=precision)


def _rmsnorm_kernel(x_ref, g_ref, o_ref):
    x = x_ref[...]
    y = x * lax.rsqrt(jnp.mean(x * x, axis=-1, keepdims=True) + RMS_EPS)
    o_ref[...] = (y * g_ref[...]).astype(o_ref.dtype)


def _rmsnorm(x2d, g, out_dtype, tm=1024):
    m, d = x2d.shape
    tm = min(tm, m)
    return pl.pallas_call(
        _rmsnorm_kernel,
        out_shape=jax.ShapeDtypeStruct((m, d), out_dtype),
        grid=(m // tm,),
        in_specs=[pl.BlockSpec((tm, d), lambda i: (i, 0)), pl.BlockSpec((1, d), lambda i: (0, 0))],
        out_specs=pl.BlockSpec((tm, d), lambda i: (i, 0)),
        compiler_params=_params(("parallel",)),
        name="rmsnorm",
    )(x2d, g.reshape(1, d))


def _matmul_kernel(*refs, act, has_res):
    if has_res:
        x_ref, w_ref, r_ref, o_ref = refs
    else:
        x_ref, w_ref, o_ref = refs
    acc = _dot(x_ref[...], w_ref[...])
    if act == "relu2":
        acc = jnp.square(jnp.maximum(acc, 0.0))
    if has_res:
        acc = acc + r_ref[...]
    o_ref[...] = acc.astype(o_ref.dtype)


def _matmul(x, w, out_dtype, *, act=None, residual=None, tm=512, tn=512):
    m, k = x.shape
    n = w.shape[1]
    tm = min(tm, m)
    tn = max(c for c in range(128, min(tn, n) + 1, 128) if n % c == 0)
    in_specs = [pl.BlockSpec((tm, k), lambda i, j: (i, 0)), pl.BlockSpec((k, tn), lambda i, j: (0, j))]
    args = [x, w]
    if residual is not None:
        in_specs.append(pl.BlockSpec((tm, tn), lambda i, j: (i, j)))
        args.append(residual)
    return pl.pallas_call(
        functools.partial(_matmul_kernel, act=act, has_res=residual is not None),
        out_shape=jax.ShapeDtypeStruct((m, n), out_dtype),
        grid=(m // tm, n // tn),
        in_specs=in_specs,
        out_specs=pl.BlockSpec((tm, tn), lambda i, j: (i, j)),
        compiler_params=_params(("parallel", "parallel")),
        name="matmul",
    )(*args)


def _proj_perm_in_kernel(x_ref, w_ref, o_ref, *, g, d):
    lhs = jnp.concatenate([x_ref[0, :, j * d:(j + 1) * d] for j in range(g)], axis=0)
    o_ref[0] = _dot(lhs, w_ref[...]).astype(o_ref.dtype)


def _proj_perm_in(hn, w, n1h, n2, g, out_dtype):
    b, t, d = hn.shape
    p = w.shape[1]
    xv = hn.reshape(b, n1h, n2 * d)
    return pl.pallas_call(
        functools.partial(_proj_perm_in_kernel, g=g, d=d),
        out_shape=jax.ShapeDtypeStruct((b, t, p), out_dtype),
        grid=(b, n2 // g),
        in_specs=[pl.BlockSpec((1, n1h, g * d), lambda i, j: (i, 0, j)),
                  pl.BlockSpec((d, p), lambda i, j: (0, 0))],
        out_specs=pl.BlockSpec((1, g * n1h, p), lambda i, j: (i, j, 0)),
        compiler_params=_params(("parallel", "parallel")),
        name="proj_perm_in",
    )(xv, w)


def _proj_perm_out_kernel(y_ref, w_ref, r_ref, o_ref, *, g, d, n1h):
    acc = _dot(y_ref[0], w_ref[...])
    for j in range(g):
        o_ref[0, :, j * d:(j + 1) * d] = acc[j * n1h:(j + 1) * n1h] + r_ref[0, :, j * d:(j + 1) * d]


def _proj_perm_out(y, w, res, n1h, n2, g):
    b, t, c = y.shape
    d = w.shape[1]
    rv = res.reshape(b, n1h, n2 * d)
    out = pl.pallas_call(
        functools.partial(_proj_perm_out_kernel, g=g, d=d, n1h=n1h),
        out_shape=jax.ShapeDtypeStruct((b, n1h, n2 * d), F32),
        grid=(b, n2 // g),
        in_specs=[pl.BlockSpec((1, g * n1h, c), lambda i, j: (i, j, 0)),
                  pl.BlockSpec((c, d), lambda i, j: (0, 0)),
                  pl.BlockSpec((1, n1h, g * d), lambda i, j: (i, 0, j))],
        out_specs=pl.BlockSpec((1, n1h, g * d), lambda i, j: (i, 0, j)),
        compiler_params=_params(("parallel", "parallel")),
        name="proj_perm_out",
    )(y, w, rv)
    return out.reshape(b, t, d)


def _shortconv_kernel(p_ref, w_ref, b_ref, o_ref, *, n1h, n2):
    w0 = w_ref[0:1, :]
    w1 = w_ref[1:2, :]
    w2 = w_ref[2:3, :]
    bias = b_ref[...]
    rows = lax.broadcasted_iota(jnp.int32, (n1h, 1), 0)

    def blk(i):
        return p_ref[0, pl.ds(pl.multiple_of(i * n1h, n1h), n1h), :]

    def body(i, carry):
        o_ref[0, pl.ds(pl.multiple_of(i * n1h, n1h), n1h), :] = (
            w0 * blk(i - 1) + w1 * blk(i) + w2 * blk(i + 1) + bias).astype(o_ref.dtype)
        return carry

    lax.fori_loop(1, n2 - 1, body, 0)
    last = p_ref[0, (n2 - 1) * n1h:, :]
    first = p_ref[0, :n1h, :]
    prev0 = jnp.where(rows == 0, 0.0, pltpu.roll(last, 1, 0))
    o_ref[0, :n1h, :] = (w0 * prev0 + w1 * first + w2 * p_ref[0, n1h:2 * n1h, :] + bias).astype(o_ref.dtype)
    nxt = jnp.where(rows == n1h - 1, 0.0, pltpu.roll(first, n1h - 1, 0))
    o_ref[0, (n2 - 1) * n1h:, :] = (w0 * p_ref[0, (n2 - 2) * n1h:(n2 - 1) * n1h, :] + w1 * last
                                    + w2 * nxt + bias).astype(o_ref.dtype)


def _shortconv(p, w, bias, n1h, n2, cb=128):
    b, t, c = p.shape
    return pl.pallas_call(
        functools.partial(_shortconv_kernel, n1h=n1h, n2=n2),
        out_shape=jax.ShapeDtypeStruct((b, t, c), F32),
        grid=(b, c // cb),
        in_specs=[pl.BlockSpec((1, t, cb), lambda i, j: (i, 0, j)),
                  pl.BlockSpec((3, cb), lambda i, j: (0, j)),
                  pl.BlockSpec((1, cb), lambda i, j: (0, j))],
        out_specs=pl.BlockSpec((1, t, cb), lambda i, j: (i, 0, j)),
        compiler_params=_params(("parallel", "parallel")),
        name="hy_shortconv",
    )(p, w, bias.reshape(1, c))


def _dft_tables(t):
    n = 2 * t
    n1 = 1 << ((n.bit_length() - 1) // 2)
    n2 = n // n1
    n1h = n1 // 2
    two_pi = 2.0 * math.pi
    k1 = jnp.arange(n1, dtype=jnp.int32)
    n2i = jnp.arange(n2, dtype=jnp.int32)

    def cs(prod, mod):
        ang = (two_pi / mod) * (prod % mod).astype(F32)
        return jnp.cos(ang), jnp.sin(ang)

    tpos = n2i[:, None] + n2 * jnp.arange(n1h, dtype=jnp.int32)[None, :]
    c, s = cs(k1[None, :, None] * tpos[:, None, :], n)
    g1 = jnp.concatenate([jnp.concatenate([c, s], axis=2), jnp.concatenate([-s, c], axis=2)], axis=1)
    ct, st = jnp.swapaxes(c, 1, 2) / n, jnp.swapaxes(s, 1, 2) / n
    g4a = jnp.concatenate([ct, st], axis=1)
    g4b = jnp.concatenate([-st, ct], axis=1)
    tposf = n2i[:, None] + n2 * jnp.arange(n1, dtype=jnp.int32)[None, :]
    cf, sf = cs(k1[None, :, None] * tposf[:, None, :], n)
    g1f = jnp.concatenate([cf, -sf], axis=1)
    c2, s2 = cs(n2i[:, None] * n2i[None, :], n2)
    f2a = jnp.concatenate([c2, -s2], axis=0)
    f2b = jnp.concatenate([s2, c2], axis=0)
    f3 = jnp.concatenate([jnp.concatenate([c2, -s2], axis=1), jnp.concatenate([s2, c2], axis=1)], axis=0)
    return dict(n1=n1, n2=n2, n1h=n1h, g1=g1.astype(BF16), g4a=g4a.astype(BF16), g4b=g4b.astype(BF16),
                g1f=g1f, f2a=f2a, f2b=f2b, f3=f3.astype(BF16))


def _filter_mlp_kernel(z_ref, w1_ref, b1_ref, w2_ref, b2_ref, w3_ref, b3_ref, fr_ref, wo_ref, dl_ref,
                       o_ref):
    z = z_ref[...]
    h = jnp.sin(fr_ref[0:1, :] * (_dot(z, w1_ref[...], HP) + b1_ref[...]))
    h = jnp.sin(fr_ref[1:2, :] * (_dot(h, w2_ref[...], HP) + b2_ref[...]))
    h = jnp.sin(fr_ref[2:3, :] * (_dot(h, w3_ref[...], HP) + b3_ref[...]))
    window = jnp.exp(-z[:, 0:1] * dl_ref[...])
    o_ref[...] = _dot(h, wo_ref[...], HP) * window


def _filter_mlp(t, w1, b1, w2, b2, w3, b3, freq, w_out, c_hy, tr=512):
    emb, width = w1.shape
    bands = (emb - 1) // 2
    pos = jnp.arange(t, dtype=F32)
    tt = pos / max(t - 1, 1)
    fr = jnp.linspace(1e-4, bands - 1, bands, dtype=F32)
    ang = (2.0 * math.pi / t) * pos[:, None] * fr[None, :]
    z = jnp.concatenate([tt[:, None], jnp.cos(ang), -jnp.sin(ang)], axis=-1)
    z = jnp.pad(z, ((0, 0), (0, 128 - emb)))
    w1 = jnp.pad(w1, ((0, 128 - emb), (0, 0)))
    emb = 128
    max_decay = math.log(HY_TARGET) / HY_FAST_PCT
    min_decay = math.log(HY_TARGET) / HY_SLOW_PCT
    delta = jnp.abs(jnp.linspace(min_decay, max_decay, c_hy, dtype=F32))
    nout = w_out.shape[1]
    delta_full = jnp.tile(delta, nout // c_hy).reshape(1, nout)
    tr = min(tr, t)
    full = lambda a: pl.BlockSpec(a.shape, lambda i: (0,) * a.ndim)
    args = [z, w1, b1.reshape(1, -1), w2, b2.reshape(1, -1), w3, b3.reshape(1, -1), freq, w_out, delta_full]
    return pl.pallas_call(
        _filter_mlp_kernel,
        out_shape=jax.ShapeDtypeStruct((t, nout), F32),
        grid=(t // tr,),
        in_specs=[pl.BlockSpec((tr, emb), lambda i: (i, 0))] + [full(a) for a in args[1:]],
        out_specs=pl.BlockSpec((tr, nout), lambda i: (i, 0)),
        compiler_params=_params(("parallel",)),
        name="hy_filter_mlp",
    )(*args)


def _filter_s1_kernel(x_ref, g_ref, o_ref, ss_ref, *, g, oc):
    @pl.when(pl.program_id(0) == 0)
    def _():
        ss_ref[...] = jnp.zeros_like(ss_ref)

    ss = ss_ref[...]
    for j in range(g):
        x = x_ref[:, j * oc:(j + 1) * oc]
        o_ref[j] = _dot(g_ref[j], x, HP)
        ss = ss + jnp.sum(x * x, axis=0, keepdims=True)
    ss_ref[...] = ss


def _filter_s2_kernel(sr_ref, si_ref, fa_ref, fb_ref, ss_ref, o_ref, *, g, oc):
    scale = lax.rsqrt(ss_ref[...] + 1e-6)
    for j in range(g):
        x = _dot(fa_ref[...], sr_ref[:, j * oc:(j + 1) * oc], HP) + _dot(fb_ref[...], si_ref[:, j * oc:(j + 1) * oc], HP)
        o_ref[j] = x * scale


def _filter_spectrum(two, tabs, g=4):
    n, oc = two.shape
    n1, n2 = tabs["n1"], tabs["n2"]
    g = min(g, n1, n2)
    xv = two.reshape(n1, n2 * oc)
    s1, ss = pl.pallas_call(
        functools.partial(_filter_s1_kernel, g=g, oc=oc),
        out_shape=(jax.ShapeDtypeStruct((n2, 2 * n1, oc), F32), jax.ShapeDtypeStruct((1, oc), F32)),
        grid=(n2 // g,),
        in_specs=[pl.BlockSpec((n1, g * oc), lambda j: (0, j)),
                  pl.BlockSpec((g, 2 * n1, n1), lambda j: (j, 0, 0))],
        out_specs=(pl.BlockSpec((g, 2 * n1, oc), lambda j: (j, 0, 0)),
                   pl.BlockSpec((1, oc), lambda j: (0, 0))),
        compiler_params=_params(("arbitrary",)),
        name="hy_filter_dft1",
    )(xv, tabs["g1f"])
    sv = s1.reshape(n2, 2 * n1 * oc)
    return pl.pallas_call(
        functools.partial(_filter_s2_kernel, g=g, oc=oc),
        out_shape=jax.ShapeDtypeStruct((n1, 2 * n2, oc), F32),
        grid=(n1 // g,),
        in_specs=[pl.BlockSpec((n2, g * oc), lambda k: (0, k)),
                  pl.BlockSpec((n2, g * oc), lambda k: (0, n1 // g + k)),
                  pl.BlockSpec((2 * n2, n2), lambda k: (0, 0)),
                  pl.BlockSpec((2 * n2, n2), lambda k: (0, 0)),
                  pl.BlockSpec((1, oc), lambda k: (0, 0))],
        out_specs=pl.BlockSpec((g, 2 * n2, oc), lambda k: (k, 0, 0)),
        compiler_params=_params(("parallel",)),
        name="hy_filter_dft2",
    )(sv, sv, tabs["f2a"], tabs["f2b"], ss)


def _conv_s1_kernel(u_ref, g_ref, o_ref, *, g, n1h):
    for j in range(g):
        rows = slice(j * n1h, (j + 1) * n1h)
        rhs = jnp.concatenate([u_ref[0, 0, rows, :], u_ref[0, 1, rows, :]], axis=0).astype(BF16)
        o_ref[0, j] = _dot(g_ref[j], rhs).astype(o_ref.dtype)


def _conv_s23_kernel(sr_ref, si_ref, fa_ref, fb_ref, f3_ref, h_ref, o_ref, *, g, c, n2):
    for j in range(g):
        cols = slice(j * c, (j + 1) * c)
        x = _dot(fa_ref[...], sr_ref[0, :, cols]) + _dot(fb_ref[...], si_ref[0, :, cols])
        xr, xi = x[:n2], x[n2:]
        hr, hi = h_ref[j, :n2, :], h_ref[j, n2:, :]
        y = jnp.concatenate([xr * hr - xi * hi, xr * hi + xi * hr], axis=0).astype(BF16)
        o_ref[0, j] = _dot(f3_ref[...], y).astype(o_ref.dtype)


def _conv_s4_kernel(rr_ref, ri_ref, ga_ref, gb_ref, u_ref, gate_ref, skip_ref, *rest, g, c, n1h, norm):
    if norm:
        ng_ref, avg_ref, o_ref = rest
    else:
        (o_ref,) = rest
    skip = skip_ref[...]
    for j in range(g):
        cols = slice(j * c, (j + 1) * c)
        rows = slice(j * n1h, (j + 1) * n1h)
        y = _dot(ga_ref[j], rr_ref[0, :, cols]) + _dot(gb_ref[j], ri_ref[0, :, cols])
        for q in range(2):
            z = gate_ref[0, q, rows, :] * (y[q * n1h:(q + 1) * n1h] + u_ref[0, q, rows, :] * skip)
            if norm:
                ms = _dot(z * z, avg_ref[...], HP)
                z = z * lax.rsqrt(ms + RMS_EPS) * ng_ref[...]
            o_ref[0, q, rows, :] = z.astype(o_ref.dtype)


def _long_conv_gate(u_arr, u_col, gate_arr, gate_col, h_spec, h_col, skip, tabs, c, out_dtype,
                    norm_g=None, g=8):
    b, t, _ = u_arr.shape
    n1, n2, n1h = tabs["n1"], tabs["n2"], tabs["n1h"]
    g = min(g, n1, n2)
    npair = b // 2
    u4 = u_arr.reshape(npair, 2, t, u_arr.shape[2])
    gate4 = gate_arr.reshape(npair, 2, t, gate_arr.shape[2])
    s1 = pl.pallas_call(
        functools.partial(_conv_s1_kernel, g=g, n1h=n1h),
        out_shape=jax.ShapeDtypeStruct((npair, n2, 2 * n1, c), BF16),
        grid=(npair, n2 // g),
        in_specs=[pl.BlockSpec((1, 2, g * n1h, c), lambda p, j: (p, 0, j, u_col)),
                  pl.BlockSpec((g, 2 * n1, n1), lambda p, j: (j, 0, 0))],
        out_specs=pl.BlockSpec((1, g, 2 * n1, c), lambda p, j: (p, j, 0, 0)),
        compiler_params=_params(("parallel", "parallel")),
        name="hy_conv_dft1",
    )(u4, tabs["g1"])
    sv = s1.reshape(npair, n2, 2 * n1 * c)
    r = pl.pallas_call(
        functools.partial(_conv_s23_kernel, g=g, c=c, n2=n2),
        out_shape=jax.ShapeDtypeStruct((npair, n1, 2 * n2, c), BF16),
        grid=(npair, n1 // g),
        in_specs=[pl.BlockSpec((1, n2, g * c), lambda p, k: (p, 0, k)),
                  pl.BlockSpec((1, n2, g * c), lambda p, k: (p, 0, n1 // g + k)),
                  pl.BlockSpec((2 * n2, n2), lambda p, k: (0, 0)),
                  pl.BlockSpec((2 * n2, n2), lambda p, k: (0, 0)),
                  pl.BlockSpec((2 * n2, 2 * n2), lambda p, k: (0, 0)),
                  pl.BlockSpec((g, 2 * n2, c), lambda p, k: (k, 0, h_col))],
        out_specs=pl.BlockSpec((1, g, 2 * n2, c), lambda p, k: (p, k, 0, 0)),
        compiler_params=_params(("parallel", "parallel")),
        name="hy_conv_dft23",
    )(sv, sv, tabs["f2a"].astype(BF16), tabs["f2b"].astype(BF16), tabs["f3"], h_spec)
    rv = r.reshape(npair, n1, 2 * n2 * c)
    norm = norm_g is not None
    in_specs = [pl.BlockSpec((1, n1, g * c), lambda p, j: (p, 0, j)),
                pl.BlockSpec((1, n1, g * c), lambda p, j: (p, 0, n2 // g + j)),
                pl.BlockSpec((g, n1, n1), lambda p, j: (j, 0, 0)),
                pl.BlockSpec((g, n1, n1), lambda p, j: (j, 0, 0)),
                pl.BlockSpec((1, 2, g * n1h, c), lambda p, j: (p, 0, j, u_col)),
                pl.BlockSpec((1, 2, g * n1h, c), lambda p, j: (p, 0, j, gate_col)),
                pl.BlockSpec((1, c), lambda p, j: (0, 0))]
    args = [rv, rv, tabs["g4a"], tabs["g4b"], u4, gate4, skip.reshape(1, c)]
    if norm:
        ch = jnp.arange(c) // HEAD
        avg = (ch[:, None] == ch[None, :]).astype(F32) / HEAD
        in_specs += [pl.BlockSpec((1, c), lambda p, j: (0, 0)), pl.BlockSpec((c, c), lambda p, j: (0, 0))]
        args += [norm_g.reshape(1, c), avg]
    out = pl.pallas_call(
        functools.partial(_conv_s4_kernel, g=g, c=c, n1h=n1h, norm=norm),
        out_shape=jax.ShapeDtypeStruct((npair, 2, t, c), out_dtype),
        grid=(npair, n2 // g),
        in_specs=in_specs,
        out_specs=pl.BlockSpec((1, 2, g * n1h, c), lambda p, j: (p, 0, j, 0)),
        compiler_params=_params(("parallel", "parallel")),
        name="hy_conv_dft4",
    )(*args)
    return out.reshape(b, t, c)


def _hyena_branch(hn, w_hy, conv_w, conv_b, fw1, fb1, fw2, fb2, fw3, fb3, ffreq, fwout, skip, norm_g, tabs):
    b, t, d = hn.shape
    c = norm_g.shape[0]
    n1, n2, n1h = tabs["n1"], tabs["n2"], tabs["n1h"]
    gp = max(1, min(n2, 512 // n1h))
    p = _proj_perm_in(hn, w_hy, n1h, n2, gp, F32)
    u = _shortconv(p, conv_w, conv_b, n1h, n2)
    h = _filter_mlp(t, fw1, fb1, fw2, fb2, fw3, fb3, ffreq, fwout, c)
    order = skip.shape[0]
    h = h.reshape(t, order, 2, c)
    fwd, bwd = h[:, :, 0], h[:, :, 1]
    two = jnp.concatenate([fwd, jnp.zeros_like(fwd[:1]), bwd[1:][::-1]], axis=0).reshape(2 * t, order * c)
    hspec = _filter_spectrum(two, tabs)
    z = _long_conv_gate(u, 2, u, 0, hspec, 0, skip[0], tabs, c, F32)
    return _long_conv_gate(z, 0, u, 1, hspec, 1, skip[1], tabs, c, BF16, norm_g=norm_g)


def _rw_prep_kernel(*refs, has_vres, c):
    if has_vres:
        (p_ref, pp_ref, pn_ref, mu_ref, w0_ref, w2_ref, a0_ref, a2_ref, g2_ref, kk_ref, ka_ref, rk_ref,
         sum_ref, hn_ref, vf_ref, v0_ref, v1_ref, v2_ref,
         r_o, v_o, kk_o, lwf_o, lwb_o, kf_o, kb_o, af_o, ab_o, g_o, bon_o) = refs
    else:
        (p_ref, pp_ref, pn_ref, mu_ref, w0_ref, w2_ref, a0_ref, a2_ref, g2_ref, kk_ref, ka_ref, rk_ref,
         sum_ref,
         r_o, v_o, kk_o, lwf_o, lwb_o, kf_o, kb_o, af_o, ab_o, g_o, bon_o) = refs
    i = pl.program_id(1)
    last = pl.num_programs(1) - 1
    p = p_ref[0]
    tt = p.shape[0]
    rows = lax.broadcasted_iota(jnp.int32, (tt, 1), 0)
    prev_row = jnp.where(i == 0, 0.0, pp_ref[0, 7:8, :])
    next_row = jnp.where(i == last, 0.0, pn_ref[0, 0:1, :])
    prev = jnp.where(rows == 0, prev_row, pltpu.roll(p, 1, 0))
    nxt = jnp.where(rows == tt - 1, next_row, pltpu.roll(p, tt - 1, 0))
    pf = p + mu_ref[0:1, :] * (prev - p) + mu_ref[1:2, :] * (nxt - p)
    r = pf[:, :c]
    k = pf[:, c:2 * c]
    v = pf[:, 2 * c:3 * c]
    lw = 3 * c
    nd = w2_ref.shape[0]
    na = a2_ref.shape[0]
    wd = pf[:, lw:lw + nd]
    ad = pf[:, lw + nd:lw + nd + na]
    gd = pf[:, lw + nd + na:]
    if has_vres:
        lora = _dot(_dot(hn_ref[0], v1_ref[...]).astype(BF16), v2_ref[...])
        v = v + (vf_ref[0] - v) * jax.nn.sigmoid(v0_ref[...] + lora)
    g = _dot(jax.nn.sigmoid(gd).astype(BF16), g2_ref[...])
    kk = k * kk_ref[...]
    nrm = jnp.sqrt(_dot(kk * kk, sum_ref[...], HP))
    kk = kk / jnp.maximum(nrm, 1e-12)
    wl = w0_ref[...] + _dot(jnp.tanh(wd).astype(BF16), w2_ref[...])
    w = -jax.nn.softplus(-wl) - 0.5
    logw = -jnp.exp(w)
    a = jax.nn.sigmoid(a0_ref[...] + _dot(ad.astype(BF16), a2_ref[...]))
    ka = ka_ref[...]
    rk = rk_ref[...]
    bon = jnp.zeros_like(v)
    k_d = []
    for d in range(2):
        a_d = a[:, d * c:(d + 1) * c]
        kd = k * (1.0 + (a_d - 1.0) * ka)
        k_d.append(kd)
        bon = bon + _dot(r * kd * rk, sum_ref[...], HP) * v
    r_o[0] = r
    v_o[0] = v
    kk_o[0] = kk
    lwf_o[0] = logw[:, :c]
    lwb_o[0] = logw[:, c:]
    kf_o[0] = k_d[0]
    kb_o[0] = k_d[1]
    af_o[0] = a[:, :c]
    ab_o[0] = a[:, c:]
    g_o[0] = g
    bon_o[0] = bon


def _blockdiag2(m):
    k, c = m.shape[1], m.shape[2]
    z = jnp.zeros((k, c), m.dtype)
    return jnp.concatenate([jnp.concatenate([m[0], z], axis=1), jnp.concatenate([z, m[1]], axis=1)], axis=0)


def _rw_prep(p, hn, v_first, shift, w0, w2, a0, a2, g2, k_k, k_a, r_k, vres, c, tt=256):
    b, t, pw = p.shape
    tt = min(tt, t)
    has_vres = vres is not None
    ch = jnp.arange(c) // HEAD
    summ = (ch[:, None] == ch[None, :]).astype(F32)
    row = lambda a: a.reshape(1, -1)
    args = [p, p, p, shift, row(w0), _blockdiag2(w2).astype(BF16), row(a0), _blockdiag2(a2).astype(BF16),
            g2.astype(BF16), row(k_k), row(k_a), row(r_k), summ]
    full = lambda a: pl.BlockSpec(a.shape, lambda i, j: (0,) * a.ndim)
    nb8 = t // 8
    in_specs = [pl.BlockSpec((1, tt, pw), lambda i, j: (i, j, 0)),
                pl.BlockSpec((1, 8, pw), lambda i, j: (i, jnp.maximum(j * (tt // 8) - 1, 0), 0)),
                pl.BlockSpec((1, 8, pw), lambda i, j: (i, jnp.minimum((j + 1) * (tt // 8), nb8 - 1), 0))]
    in_specs += [full(a) for a in args[3:]]
    if has_vres:
        v0, v1, v2 = vres
        extra = [hn, v_first, row(v0), v1.astype(BF16), v2.astype(BF16)]
        in_specs += [pl.BlockSpec((1, tt, hn.shape[2]), lambda i, j: (i, j, 0)),
                     pl.BlockSpec((1, tt, c), lambda i, j: (i, j, 0))] + [full(a) for a in extra[2:]]
        args += extra
    outs = tuple(jax.ShapeDtypeStruct((b, t, c), F32) for _ in range(11))
    return pl.pallas_call(
        functools.partial(_rw_prep_kernel, has_vres=has_vres, c=c),
        out_shape=outs,
        grid=(b, t // tt),
        in_specs=in_specs,
        out_specs=tuple(pl.BlockSpec((1, tt, c), lambda i, j: (i, j, 0)) for _ in range(11)),
        compiler_params=_params(("parallel", "parallel")),
        name="rw_prep",
    )(*args)


def _scan_kernel(r_ref, k_ref, v_ref, kk_ref, a_ref, lw_ref, o_ref, s_ref, *, reverse):
    L = CHUNK

    @pl.when(pl.program_id(2) == 0)
    def _():
        s_ref[...] = jnp.zeros_like(s_ref)

    r, k, v, kk, a, lw = r_ref[0], k_ref[0], v_ref[0], kk_ref[0], a_ref[0], lw_ref[0]
    ti = lax.broadcasted_iota(jnp.int32, (L, L), 0)
    si = lax.broadcasted_iota(jnp.int32, (L, L), 1)
    tri = ((si >= ti) if reverse else (si <= ti)).astype(F32)
    cum = _dot(tri, lw, HP)
    tot = cum[0:1, :] if reverse else cum[L - 1:L, :]
    p_in = jnp.exp(cum)
    p_inv = jnp.exp(-cum)
    p_ex = jnp.exp(cum - lw)
    p_rem = jnp.exp(tot - cum)
    beta = kk * a
    a_t = -kk * p_ex
    r_t = r * p_in
    k_t = k * p_inv
    b_t = beta * p_inv
    k_h = k * p_rem
    b_h = beta * p_rem

    gi = lax.broadcasted_iota(jnp.int32, (GW, GW), 0) // HEAD
    gj = lax.broadcasted_iota(jnp.int32, (GW, GW), 1) // HEAD
    bdm = gi == gj

    def bd(x):
        return jnp.where(bdm, jnp.concatenate([x] * HEADS_PER_GROUP, axis=0), 0.0)

    def rcmul(x, y):
        return _dot(x, bd(y), HP)

    tt = lax.broadcasted_iota(jnp.int32, (L, GW), 0)
    ss = lax.broadcasted_iota(jnp.int32, (L, GW), 1) % L
    strict = (ss > tt) if reverse else (ss < tt)
    incl = (ss >= tt) if reverse else (ss <= tt)
    eye = (ss == tt).astype(F32)

    lhs = jnp.concatenate([a_t, r_t], axis=0)
    akk = _dot_nt(lhs, bd(k_t), HP)
    abb = _dot_nt(lhs, bd(b_t), HP)
    a_ak = jnp.where(strict, akk[:L], 0.0)
    a_rk = jnp.where(incl, akk[L:], 0.0)
    a_ab = jnp.where(strict, abb[:L], 0.0)
    a_rb = jnp.where(incl, abb[L:], 0.0)

    pw = a_ab
    tinv = eye + pw
    for _ in range(5):
        pw = rcmul(pw, pw)
        tinv = tinv + rcmul(tinv, pw)

    av = rcmul(jnp.concatenate([a_ak, a_rk], axis=0), v)
    wu = _dot(tinv, jnp.concatenate([bd(a_t), bd(av[:L])], axis=1), HP)
    wm, um = wu[:, :GW], wu[:, GW:]
    rbwu = _dot(a_rb, jnp.concatenate([bd(wm), bd(um)], axis=1), HP)
    rq = r_t + rbwu[:, :GW]
    o_in = av[L:] + rbwu[:, GW:]

    s0 = s_ref[...]
    o_ref[0] = _dot_nt(rq, bd(s0), HP) + o_in

    d256 = lax.broadcasted_iota(jnp.int32, (GW, GW), 0) == lax.broadcasted_iota(jnp.int32, (GW, GW), 1)
    m_bd = jnp.where(bdm, _dot_tn(wm, b_h, HP), 0.0) + jnp.where(d256, jnp.exp(tot), 0.0)
    n_full = jnp.where(bdm, _dot_tn(v, k_h, HP) + _dot_tn(um, b_h, HP), 0.0)
    n_rc = n_full[0:HEAD]
    for h in range(1, HEADS_PER_GROUP):
        n_rc = n_rc + n_full[h * HEAD:(h + 1) * HEAD]
    s_ref[...] = _dot(s0, m_bd, HP) + n_rc


def _wkv_scan(r, k, v, kk, a, lw, reverse):
    b, t, c = r.shape
    nc = t // CHUNK
    ng = c // GW
    if reverse:
        idx = lambda i, h, j: (i, nc - 1 - j, h)
    else:
        idx = lambda i, h, j: (i, j, h)
    spec = pl.BlockSpec((1, CHUNK, GW), idx)
    return pl.pallas_call(
        functools.partial(_scan_kernel, reverse=reverse),
        out_shape=jax.ShapeDtypeStruct((b, t, c), F32),
        grid=(b, ng, nc),
        in_specs=[spec] * 6,
        out_specs=spec,
        scratch_shapes=[pltpu.VMEM((HEAD, GW), F32)],
        compiler_params=_params(("parallel", "parallel", "arbitrary")),
        name="rw_scan_bwd" if reverse else "rw_scan_fwd",
    )(r, k, v, kk, a, lw)


def _rw_post_kernel(sf_ref, sb_ref, bon_ref, g_ref, lg_ref, lb_ref, avg_ref, o_ref):
    s = sf_ref[...] + sb_ref[...]
    mean = _dot(s, avg_ref[...], HP)
    d = s - mean
    var = _dot(d * d, avg_ref[...], HP)
    y = d * lax.rsqrt(var + GN_EPS) * lg_ref[...] + lb_ref[...] + bon_ref[...]
    o_ref[...] = (y * g_ref[...]).astype(o_ref.dtype)


def _rw_post(sf, sb, bon, g, lnx_g, lnx_b, tm=512):
    b, t, c = sf.shape
    m = b * t
    tm = min(tm, m)
    ch = jnp.arange(c) // HEAD
    avg = (ch[:, None] == ch[None, :]).astype(F32) / HEAD
    big = pl.BlockSpec((tm, c), lambda i: (i, 0))
    small = pl.BlockSpec((1, c), lambda i: (0, 0))
    out = pl.pallas_call(
        _rw_post_kernel,
        out_shape=jax.ShapeDtypeStruct((m, c), BF16),
        grid=(m // tm,),
        in_specs=[big, big, big, big, small, small, pl.BlockSpec((c, c), lambda i: (0, 0))],
        out_specs=big,
        compiler_params=_params(("parallel",)),
        name="rw_post",
    )(sf.reshape(m, c), sb.reshape(m, c), bon.reshape(m, c), g.reshape(m, c),
      lnx_g.reshape(1, c), lnx_b.reshape(1, c), avg)
    return out.reshape(b, t, c)


def _rwkv_branch(hn, w_rw, v_first, shift, w0, w2, a0, a2, g2, k_k, k_a, r_k, lnx_g, lnx_b, vres, c):
    b, t, d = hn.shape
    p = _matmul(hn.reshape(b * t, d), w_rw, F32).reshape(b, t, -1)
    r, v, kk, lwf, lwb, kf, kb, af, ab, g, bon = _rw_prep(
        p, hn, v_first, shift, w0, w2, a0, a2, g2, k_k, k_a, r_k, vres, c)
    sf = _wkv_scan(r, kf, v, kk, af, lwf, False)
    sb = _wkv_scan(r, kb, v, kk, ab, lwb, True)
    y = _rw_post(sf, sb, bon, g, lnx_g, lnx_b)
    return y, (v if vres is None else v_first)


def _forward(x, norm1_g, w_in, hy_conv_w, hy_conv_b, hy_f_w1, hy_f_b1, hy_f_w2, hy_f_b2, hy_f_w3, hy_f_b3,
             hy_f_freq, hy_f_wout, hy_skip, hy_norm_g, rw_shift, rw_w0, rw_w2, rw_a0, rw_a2, rw_g2, rw_k_k,
             rw_k_a, rw_r_k, rw_lnx_g, rw_lnx_b, rw_v0, rw_v1, rw_v2, w_out, norm2_g, mlp_w1, mlp_w2,
             final_g):
    b, t, d = x.shape
    depth = w_in.shape[0]
    c_hy = hy_norm_g.shape[1]
    c_rw = rw_lnx_g.shape[1]
    hy_proj = hy_conv_b.shape[1]
    assert b % 2 == 0 and t % CHUNK == 0 and CHUNK == HEAD
    tabs = _dft_tables(t)
    n1h, n2 = tabs["n1h"], tabs["n2"]
    gp = max(1, min(n2, 512 // n1h))
    v_first = None
    for l in range(depth):
        hn = _rmsnorm(x.reshape(b * t, d), norm1_g[l], BF16).reshape(b, t, d)
        w_l = w_in[l].astype(BF16)
        y_hy = _hyena_branch(hn, w_l[:, :hy_proj], hy_conv_w[l], hy_conv_b[l], hy_f_w1[l], hy_f_b1[l],
                             hy_f_w2[l], hy_f_b2[l], hy_f_w3[l], hy_f_b3[l], hy_f_freq[l], hy_f_wout[l],
                             hy_skip[l], hy_norm_g[l], tabs)
        vres = None if l == 0 else (rw_v0[l - 1], rw_v1[l - 1], rw_v2[l - 1])
        y_rw, v_first = _rwkv_branch(hn, w_l[:, hy_proj:], v_first, rw_shift[l], rw_w0[l], rw_w2[l], rw_a0[l],
                                     rw_a2[l], rw_g2[l], rw_k_k[l], rw_k_a[l], rw_r_k[l], rw_lnx_g[l],
                                     rw_lnx_b[l], vres, c_rw)
        wo = w_out[l].astype(BF16)
        x = _proj_perm_out(y_hy, wo[:c_hy], x, n1h, n2, gp)
        x = _matmul(y_rw.reshape(b * t, c_rw), wo[c_hy:], F32, residual=x.reshape(b * t, d)).reshape(b, t, d)
        hn2 = _rmsnorm(x.reshape(b * t, d), norm2_g[l], BF16)
        hmid = _matmul(hn2, mlp_w1[l].astype(BF16), BF16, act="relu2")
        x = _matmul(hmid, mlp_w2[l].astype(BF16), F32, residual=x.reshape(b * t, d)).reshape(b, t, d)
    return _rmsnorm(x.reshape(b * t, d), final_g, F32).reshape(b, t, d)


def kernel(x, norm1_g, w_in, hy_conv_w, hy_conv_b, hy_f_w1, hy_f_b1, hy_f_w2, hy_f_b2, hy_f_w3, hy_f_b3,
           hy_f_freq, hy_f_wout, hy_skip, hy_norm_g, rw_shift, rw_w0, rw_w2, rw_a0, rw_a2, rw_g2, rw_k_k,
           rw_k_a, rw_r_k, rw_lnx_g, rw_lnx_b, rw_v0, rw_v1, rw_v2, w_out, norm2_g, mlp_w1, mlp_w2, final_g):
    return _forward(x, norm1_g, w_in, hy_conv_w, hy_conv_b, hy_f_w1, hy_f_b1, hy_f_w2, hy_f_b2, hy_f_w3,
                    hy_f_b3, hy_f_freq, hy_f_wout, hy_skip, hy_norm_g, rw_shift, rw_w0, rw_w2, rw_a0, rw_a2,
                    rw_g2, rw_k_k, rw_k_a, rw_r_k, rw_lnx_g, rw_lnx_b, rw_v0, rw_v1, rw_v2, w_out, norm2_g,
                    mlp_w1, mlp_w2, final_g)
```

```python
import functools
import math

import jax
import jax.numpy as jnp
from jax import lax
from jax.experimental import pallas as pl
from jax.experimental.pallas import tpu as pltpu

F32 = jnp.float32
BF16 = jnp.bfloat16
HP = lax.Precision.HIGHEST

HEAD = 64
HEADS_PER_GROUP = 4
GW = HEAD * HEADS_PER_GROUP
CHUNK = 64
RMS_EPS = 1e-5
GN_EPS = HEAD * 1e-5
HY_TARGET = 1e-2
HY_FAST_PCT = 0.3
HY_SLOW_PCT = 1.5
VMEM_LIMIT = 56 * 1024 * 1024


def _params(sem, vmem=VMEM_LIMIT):
    return pltpu.CompilerParams(dimension_semantics=sem, vmem_limit_bytes=vmem)


def _dot(a, b, precision=None):
    return jnp.dot(a, b, preferred_element_type=F32, precision=precision)


def _dot_nt(a, b, precision=None):
    return lax.dot_general(a, b, (((1,), (1,)), ((), ())), preferred_element_type=F32,
                           precision=precision)


def _dot_tn(a, b, precision=None):
    return lax.dot_general(a, b, (((0,), (0,)), ((), ())), preferred_element_type=F32,
                           precision=precision)


def _rmsnorm_kernel(x_ref, g_ref, o_ref):
    x = x_ref[...]
    y = x * lax.rsqrt(jnp.mean(x * x, axis=-1, keepdims=True) + RMS_EPS)
    o_ref[...] = (y * g_ref[...]).astype(o_ref.dtype)


def _rmsnorm(x2d, g, out_dtype, tm=1024):
    m, d = x2d.shape
    tm = min(tm, m)
    return pl.pallas_call(
        _rmsnorm_kernel,
        out_shape=jax.ShapeDtypeStruct((m, d), out_dtype),
        grid=(m // tm,),
        in_specs=[pl.BlockSpec((tm, d), lambda i: (i, 0)), pl.BlockSpec((1, d), lambda i: (0, 0))],
        out_specs=pl.BlockSpec((tm, d), lambda i: (i, 0)),
        compiler_params=_params(("parallel",)),
        name="rmsnorm",
    )(x2d, g.reshape(1, d))


def _matmul_kernel(*refs, act, has_res):
    if has_res:
        x_ref, w_ref, r_ref, o_ref = refs
    else:
        x_ref, w_ref, o_ref = refs
    acc = _dot(x_ref[...], w_ref[...])
    if act == "relu2":
        acc = jnp.square(jnp.maximum(acc, 0.0))
    if has_res:
        acc = acc + r_ref[...]
    o_ref[...] = acc.astype(o_ref.dtype)


def _matmul(x, w, out_dtype, *, act=None, residual=None, tm=512, tn=512):
    m, k = x.shape
    n = w.shape[1]
    tm = min(tm, m)
    tn = max(c for c in range(128, min(tn, n) + 1, 128) if n % c == 0)
    in_specs = [pl.BlockSpec((tm, k), lambda i, j: (i, 0)), pl.BlockSpec((k, tn), lambda i, j: (0, j))]
    args = [x, w]
    if residual is not None:
        in_specs.append(pl.BlockSpec((tm, tn), lambda i, j: (i, j)))
        args.append(residual)
    return pl.pallas_call(
        functools.partial(_matmul_kernel, act=act, has_res=residual is not None),
        out_shape=jax.ShapeDtypeStruct((m, n), out_dtype),
        grid=(m // tm, n // tn),
        in_specs=in_specs,
        out_specs=pl.BlockSpec((tm, tn), lambda i, j: (i, j)),
        compiler_params=_params(("parallel", "parallel")),
        name="matmul",
    )(*args)


def _proj_perm_in_kernel(x_ref, w_ref, o_ref, *, g, d):
    lhs = jnp.concatenate([x_ref[0, :, j * d:(j + 1) * d] for j in range(g)], axis=0)
    o_ref[0] = _dot(lhs, w_ref[...]).astype(o_ref.dtype)


def _proj_perm_in(hn, w, n1h, n2, g, out_dtype):
    b, t, d = hn.shape
    p = w.shape[1]
    xv = hn.reshape(b, n1h, n2 * d)
    return pl.pallas_call(
        functools.partial(_proj_perm_in_kernel, g=g, d=d),
        out_shape=jax.ShapeDtypeStruct((b, t, p), out_dtype),
        grid=(b, n2 // g),
        in_specs=[pl.BlockSpec((1, n1h, g * d), lambda i, j: (i, 0, j)),
                  pl.BlockSpec((d, p), lambda i, j: (0, 0))],
        out_specs=pl.BlockSpec((1, g * n1h, p), lambda i, j: (i, j, 0)),
        compiler_params=_params(("parallel", "parallel")),
        name="proj_perm_in",
    )(xv, w)


def _proj_perm_out_kernel(y_ref, w_ref, r_ref, o_ref, *, g, d, n1h):
    acc = _dot(y_ref[0], w_ref[...])
    for j in range(g):
        o_ref[0, :, j * d:(j + 1) * d] = acc[j * n1h:(j + 1) * n1h] + r_ref[0, :, j * d:(j + 1) * d]


def _proj_perm_out(y, w, res, n1h, n2, g):
    b, t, c = y.shape
    d = w.shape[1]
    rv = res.reshape(b, n1h, n2 * d)
    out = pl.pallas_call(
        functools.partial(_proj_perm_out_kernel, g=g, d=d, n1h=n1h),
        out_shape=jax.ShapeDtypeStruct((b, n1h, n2 * d), F32),
        grid=(b, n2 // g),
        in_specs=[pl.BlockSpec((1, g * n1h, c), lambda i, j: (i, j, 0)),
                  pl.BlockSpec((c, d), lambda i, j: (0, 0)),
                  pl.BlockSpec((1, n1h, g * d), lambda i, j: (i, 0, j))],
        out_specs=pl.BlockSpec((1, n1h, g * d), lambda i, j: (i, 0, j)),
        compiler_params=_params(("parallel", "parallel")),
        name="proj_perm_out",
    )(y, w, rv)
    return out.reshape(b, t, d)


def _shortconv_kernel(p_ref, w_ref, b_ref, o_ref, *, n1h, n2):
    w0 = w_ref[0:1, :]
    w1 = w_ref[1:2, :]
    w2 = w_ref[2:3, :]
    bias = b_ref[...]
    rows = lax.broadcasted_iota(jnp.int32, (n1h, 1), 0)

    def blk(i):
        return p_ref[0, pl.ds(pl.multiple_of(i * n1h, n1h), n1h), :]

    def body(i, carry):
        o_ref[0, pl.ds(pl.multiple_of(i * n1h, n1h), n1h), :] = (
            w0 * blk(i - 1) + w1 * blk(i) + w2 * blk(i + 1) + bias).astype(o_ref.dtype)
        return carry

    lax.fori_loop(1, n2 - 1, body, 0)
    last = p_ref[0, (n2 - 1) * n1h:, :]
    first = p_ref[0, :n1h, :]
    prev0 = jnp.where(rows == 0, 0.0, pltpu.roll(last, 1, 0))
    o_ref[0, :n1h, :] = (w0 * prev0 + w1 * first + w2 * p_ref[0, n1h:2 * n1h, :] + bias).astype(o_ref.dtype)
    nxt = jnp.where(rows == n1h - 1, 0.0, pltpu.roll(first, n1h - 1, 0))
    o_ref[0, (n2 - 1) * n1h:, :] = (w0 * p_ref[0, (n2 - 2) * n1h:(n2 - 1) * n1h, :] + w1 * last
                                    + w2 * nxt + bias).astype(o_ref.dtype)


def _shortconv(p, w, bias, n1h, n2, cb=128):
    b, t, c = p.shape
    return pl.pallas_call(
        functools.partial(_shortconv_kernel, n1h=n1h, n2=n2),
        out_shape=jax.ShapeDtypeStruct((b, t, c), F32),
        grid=(b, c // cb),
        in_specs=[pl.BlockSpec((1, t, cb), lambda i, j: (i, 0, j)),
                  pl.BlockSpec((3, cb), lambda i, j: (0, j)),
                  pl.BlockSpec((1, cb), lambda i, j: (0, j))],
        out_specs=pl.BlockSpec((1, t, cb), lambda i, j: (i, 0, j)),
        compiler_params=_params(("parallel", "parallel")),
        name="hy_shortconv",
    )(p, w, bias.reshape(1, c))


def _dft_tables(t):
    n = 2 * t
    n1 = 1 << ((n.bit_length() - 1) // 2)
    n2 = n // n1
    n1h = n1 // 2
    two_pi = 2.0 * math.pi
    k1 = jnp.arange(n1, dtype=jnp.int32)
    n2i = jnp.arange(n2, dtype=jnp.int32)

    def cs(prod, mod):
        ang = (two_pi / mod) * (prod % mod).astype(F32)
        return jnp.cos(ang), jnp.sin(ang)

    tpos = n2i[:, None] + n2 * jnp.arange(n1h, dtype=jnp.int32)[None, :]
    c, s = cs(k1[None, :, None] * tpos[:, None, :], n)
    g1 = jnp.concatenate([jnp.concatenate([c, s], axis=2), jnp.concatenate([-s, c], axis=2)], axis=1)
    ct, st = jnp.swapaxes(c, 1, 2) / n, jnp.swapaxes(s, 1, 2) / n
    g4a = jnp.concatenate([ct, st], axis=1)
    g4b = jnp.concatenate([-st, ct], axis=1)
    tposf = n2i[:, None] + n2 * jnp.arange(n1, dtype=jnp.int32)[None, :]
    cf, sf = cs(k1[None, :, None] * tposf[:, None, :], n)
    g1f = jnp.concatenate([cf, -sf], axis=1)
    c2, s2 = cs(n2i[:, None] * n2i[None, :], n2)
    f2a = jnp.concatenate([c2, -s2], axis=0)
    f2b = jnp.concatenate([s2, c2], axis=0)
    f3 = jnp.concatenate([jnp.concatenate([c2, -s2], axis=1), jnp.concatenate([s2, c2], axis=1)], axis=0)
    return dict(n1=n1, n2=n2, n1h=n1h, g1=g1.astype(BF16), g4a=g4a.astype(BF16), g4b=g4b.astype(BF16),
                g1f=g1f, f2a=f2a, f2b=f2b, f3=f3.astype(BF16))


def _filter_mlp_kernel(z_ref, w1_ref, b1_ref, w2_ref, b2_ref, w3_ref, b3_ref, fr_ref, wo_ref, dl_ref,
                       o_ref):
    z = z_ref[...]
    h = jnp.sin(fr_ref[0:1, :] * (_dot(z, w1_ref[...], HP) + b1_ref[...]))
    h = jnp.sin(fr_ref[1:2, :] * (_dot(h, w2_ref[...], HP) + b2_ref[...]))
    h = jnp.sin(fr_ref[2:3, :] * (_dot(h, w3_ref[...], HP) + b3_ref[...]))
    window = jnp.exp(-z[:, 0:1] * dl_ref[...])
    o_ref[...] = _dot(h, wo_ref[...], HP) * window


def _filter_mlp(t, w1, b1, w2, b2, w3, b3, freq, w_out, c_hy, tr=512):
    emb, width = w1.shape
    bands = (emb - 1) // 2
    pos = jnp.arange(t, dtype=F32)
    tt = pos / max(t - 1, 1)
    fr = jnp.linspace(1e-4, bands - 1, bands, dtype=F32)
    ang = (2.0 * math.pi / t) * pos[:, None] * fr[None, :]
    z = jnp.concatenate([tt[:, None], jnp.cos(ang), -jnp.sin(ang)], axis=-1)
    z = jnp.pad(z, ((0, 0), (0, 128 - emb)))
    w1 = jnp.pad(w1, ((0, 128 - emb), (0, 0)))
    emb = 128
    max_decay = math.log(HY_TARGET) / HY_FAST_PCT
    min_decay = math.log(HY_TARGET) / HY_SLOW_PCT
    delta = jnp.abs(jnp.linspace(min_decay, max_decay, c_hy, dtype=F32))
    nout = w_out.shape[1]
    delta_full = jnp.tile(delta, nout // c_hy).reshape(1, nout)
    tr = min(tr, t)
    full = lambda a: pl.BlockSpec(a.shape, lambda i: (0,) * a.ndim)
    args = [z, w1, b1.reshape(1, -1), w2, b2.reshape(1, -1), w3, b3.reshape(1, -1), freq, w_out, delta_full]
    return pl.pallas_call(
        _filter_mlp_kernel,
        out_shape=jax.ShapeDtypeStruct((t, nout), F32),
        grid=(t // tr,),
        in_specs=[pl.BlockSpec((tr, emb), lambda i: (i, 0))] + [full(a) for a in args[1:]],
        out_specs=pl.BlockSpec((tr, nout), lambda i: (i, 0)),
        compiler_params=_params(("parallel",)),
        name="hy_filter_mlp",
    )(*args)


def _filter_s1_kernel(x_ref, g_ref, o_ref, ss_ref, *, g, oc):
    @pl.when(pl.program_id(0) == 0)
    def _():
        ss_ref[...] = jnp.zeros_like(ss_ref)

    ss = ss_ref[...]
    for j in range(g):
        x = x_ref[:, j * oc:(j + 1) * oc]
        o_ref[j] = _dot(g_ref[j], x, HP)
        ss = ss + jnp.sum(x * x, axis=0, keepdims=True)
    ss_ref[...] = ss


def _filter_s2_kernel(sr_ref, si_ref, fa_ref, fb_ref, ss_ref, o_ref, *, g, oc):
    scale = lax.rsqrt(ss_ref[...] + 1e-6)
    for j in range(g):
        x = _dot(fa_ref[...], sr_ref[:, j * oc:(j + 1) * oc], HP) + _dot(fb_ref[...], si_ref[:, j * oc:(j + 1) * oc], HP)
        o_ref[j] = x * scale


def _filter_spectrum(two, tabs, g=4):
    n, oc = two.shape
    n1, n2 = tabs["n1"], tabs["n2"]
    g = min(g, n1, n2)
    xv = two.reshape(n1, n2 * oc)
    s1, ss = pl.pallas_call(
        functools.partial(_filter_s1_kernel, g=g, oc=oc),
        out_shape=(jax.ShapeDtypeStruct((n2, 2 * n1, oc), F32), jax.ShapeDtypeStruct((1, oc), F32)),
        grid=(n2 // g,),
        in_specs=[pl.BlockSpec((n1, g * oc), lambda j: (0, j)),
                  pl.BlockSpec((g, 2 * n1, n1), lambda j: (j, 0, 0))],
        out_specs=(pl.BlockSpec((g, 2 * n1, oc), lambda j: (j, 0, 0)),
                   pl.BlockSpec((1, oc), lambda j: (0, 0))),
        compiler_params=_params(("arbitrary",)),
        name="hy_filter_dft1",
    )(xv, tabs["g1f"])
    sv = s1.reshape(n2, 2 * n1 * oc)
    return pl.pallas_call(
        functools.partial(_filter_s2_kernel, g=g, oc=oc),
        out_shape=jax.ShapeDtypeStruct((n1, 2 * n2, oc), F32),
        grid=(n1 // g,),
        in_specs=[pl.BlockSpec((n2, g * oc), lambda k: (0, k)),
                  pl.BlockSpec((n2, g * oc), lambda k: (0, n1 // g + k)),
                  pl.BlockSpec((2 * n2, n2), lambda k: (0, 0)),
                  pl.BlockSpec((2 * n2, n2), lambda k: (0, 0)),
                  pl.BlockSpec((1, oc), lambda k: (0, 0))],
        out_specs=pl.BlockSpec((g, 2 * n2, oc), lambda k: (k, 0, 0)),
        compiler_params=_params(("parallel",)),
        name="hy_filter_dft2",
    )(sv, sv, tabs["f2a"], tabs["f2b"], ss)


def _conv_s1_kernel(u_ref, g_ref, o_ref, *, g, n1h):
    for j in range(g):
        rows = slice(j * n1h, (j + 1) * n1h)
        rhs = jnp.concatenate([u_ref[0, 0, rows, :], u_ref[0, 1, rows, :]], axis=0).astype(BF16)
        o_ref[0, j] = _dot(g_ref[j], rhs).astype(o_ref.dtype)


def _conv_s23_kernel(sr_ref, si_ref, fa_ref, fb_ref, f3_ref, h_ref, o_ref, *, g, c, n2):
    for j in range(g):
        cols = slice(j * c, (j + 1) * c)
        x = _dot(fa_ref[...], sr_ref[0, :, cols]) + _dot(fb_ref[...], si_ref[0, :, cols])
        xr, xi = x[:n2], x[n2:]
        hr, hi = h_ref[j, :n2, :], h_ref[j, n2:, :]
        y = jnp.concatenate([xr * hr - xi * hi, xr * hi + xi * hr], axis=0).astype(BF16)
        o_ref[0, j] = _dot(f3_ref[...], y).astype(o_ref.dtype)


def _conv_s4_kernel(rr_ref, ri_ref, ga_ref, gb_ref, u_ref, gate_ref, skip_ref, *rest, g, c, n1h, norm):
    if norm:
        ng_ref, avg_ref, o_ref = rest
    else:
        (o_ref,) = rest
    skip = skip_ref[...]
    for j in range(g):
        cols = slice(j * c, (j + 1) * c)
        rows = slice(j * n1h, (j + 1) * n1h)
        y = _dot(ga_ref[j], rr_ref[0, :, cols]) + _dot(gb_ref[j], ri_ref[0, :, cols])
        for q in range(2):
            z = gate_ref[0, q, rows, :] * (y[q * n1h:(q + 1) * n1h] + u_ref[0, q, rows, :] * skip)
            if norm:
                ms = _dot(z * z, avg_ref[...], HP)
                z = z * lax.rsqrt(ms + RMS_EPS) * ng_ref[...]
            o_ref[0, q, rows, :] = z.astype(o_ref.dtype)


def _long_conv_gate(u_arr, u_col, gate_arr, gate_col, h_spec, h_col, skip, tabs, c, out_dtype,
                    norm_g=None, g=8):
    b, t, _ = u_arr.shape
    n1, n2, n1h = tabs["n1"], tabs["n2"], tabs["n1h"]
    g = min(g, n1, n2)
    npair = b // 2
    u4 = u_arr.reshape(npair, 2, t, u_arr.shape[2])
    gate4 = gate_arr.reshape(npair, 2, t, gate_arr.shape[2])
    s1 = pl.pallas_call(
        functools.partial(_conv_s1_kernel, g=g, n1h=n1h),
        out_shape=jax.ShapeDtypeStruct((npair, n2, 2 * n1, c), BF16),
        grid=(npair, n2 // g),
        in_specs=[pl.BlockSpec((1, 2, g * n1h, c), lambda p, j: (p, 0, j, u_col)),
                  pl.BlockSpec((g, 2 * n1, n1), lambda p, j: (j, 0, 0))],
        out_specs=pl.BlockSpec((1, g, 2 * n1, c), lambda p, j: (p, j, 0, 0)),
        compiler_params=_params(("parallel", "parallel")),
        name="hy_conv_dft1",
    )(u4, tabs["g1"])
    sv = s1.reshape(npair, n2, 2 * n1 * c)
    r = pl.pallas_call(
        functools.partial(_conv_s23_kernel, g=g, c=c, n2=n2),
        out_shape=jax.ShapeDtypeStruct((npair, n1, 2 * n2, c), BF16),
        grid=(npair, n1 // g),
        in_specs=[pl.BlockSpec((1, n2, g * c), lambda p, k: (p, 0, k)),
                  pl.BlockSpec((1, n2, g * c), lambda p, k: (p, 0, n1 // g + k)),
                  pl.BlockSpec((2 * n2, n2), lambda p, k: (0, 0)),
                  pl.BlockSpec((2 * n2, n2), lambda p, k: (0, 0)),
                  pl.BlockSpec((2 * n2, 2 * n2), lambda p, k: (0, 0)),
                  pl.BlockSpec((g, 2 * n2, c), lambda p, k: (k, 0, h_col))],
        out_specs=pl.BlockSpec((1, g, 2 * n2, c), lambda p, k: (p, k, 0, 0)),
        compiler_params=_params(("parallel", "parallel")),
        name="hy_conv_dft23",
    )(sv, sv, tabs["f2a"].astype(BF16), tabs["f2b"].astype(BF16), tabs["f3"], h_spec)
    rv = r.reshape(npair, n1, 2 * n2 * c)
    norm = norm_g is not None
    in_specs = [pl.BlockSpec((1, n1, g * c), lambda p, j: (p, 0, j)),
                pl.BlockSpec((1, n1, g * c), lambda p, j: (p, 0, n2 // g + j)),
                pl.BlockSpec((g, n1, n1), lambda p, j: (j, 0, 0)),
                pl.BlockSpec((g, n1, n1), lambda p, j: (j, 0, 0)),
                pl.BlockSpec((1, 2, g * n1h, c), lambda p, j: (p, 0, j, u_col)),
                pl.BlockSpec((1, 2, g * n1h, c), lambda p, j: (p, 0, j, gate_col)),
                pl.BlockSpec((1, c), lambda p, j: (0, 0))]
    args = [rv, rv, tabs["g4a"], tabs["g4b"], u4, gate4, skip.reshape(1, c)]
    if norm:
        ch = jnp.arange(c) // HEAD
        avg = (ch[:, None] == ch[None, :]).astype(F32) / HEAD
        in_specs += [pl.BlockSpec((1, c), lambda p, j: (0, 0)), pl.BlockSpec((c, c), lambda p, j: (0, 0))]
        args += [norm_g.reshape(1, c), avg]
    out = pl.pallas_call(
        functools.partial(_conv_s4_kernel, g=g, c=c, n1h=n1h, norm=norm),
        out_shape=jax.ShapeDtypeStruct((npair, 2, t, c), out_dtype),
        grid=(npair, n2 // g),
        in_specs=in_specs,
        out_specs=pl.BlockSpec((1, 2, g * n1h, c), lambda p, j: (p, 0, j, 0)),
        compiler_params=_params(("parallel", "parallel")),
        name="hy_conv_dft4",
    )(*args)
    return out.reshape(b, t, c)


def _hyena_branch(hn, w_hy, conv_w, conv_b, fw1, fb1, fw2, fb2, fw3, fb3, ffreq, fwout, skip, norm_g, tabs):
    b, t, d = hn.shape
    c = norm_g.shape[0]
    n1, n2, n1h = tabs["n1"], tabs["n2"], tabs["n1h"]
    gp = max(1, min(n2, 512 // n1h))
    p = _proj_perm_in(hn, w_hy, n1h, n2, gp, F32)
    u = _shortconv(p, conv_w, conv_b, n1h, n2)
    h = _filter_mlp(t, fw1, fb1, fw2, fb2, fw3, fb3, ffreq, fwout, c)
    order = skip.shape[0]
    h = h.reshape(t, order, 2, c)
    fwd, bwd = h[:, :, 0], h[:, :, 1]
    two = jnp.concatenate([fwd, jnp.zeros_like(fwd[:1]), bwd[1:][::-1]], axis=0).reshape(2 * t, order * c)
    hspec = _filter_spectrum(two, tabs)
    z = _long_conv_gate(u, 2, u, 0, hspec, 0, skip[0], tabs, c, F32)
    return _long_conv_gate(z, 0, u, 1, hspec, 1, skip[1], tabs, c, BF16, norm_g=norm_g)


def _rw_prep_kernel(*refs, has_vres, c):
    if has_vres:
        (p_ref, pp_ref, pn_ref, mu_ref, w0_ref, w2_ref, a0_ref, a2_ref, g2_ref, kk_ref, ka_ref, rk_ref,
         sum_ref, hn_ref, vf_ref, v0_ref, v1_ref, v2_ref,
         r_o, v_o, kk_o, lwf_o, lwb_o, kf_o, kb_o, af_o, ab_o, g_o, bon_o) = refs
    else:
        (p_ref, pp_ref, pn_ref, mu_ref, w0_ref, w2_ref, a0_ref, a2_ref, g2_ref, kk_ref, ka_ref, rk_ref,
         sum_ref,
         r_o, v_o, kk_o, lwf_o, lwb_o, kf_o, kb_o, af_o, ab_o, g_o, bon_o) = refs
    i = pl.program_id(1)
    last = pl.num_programs(1) - 1
    p = p_ref[0]
    tt = p.shape[0]
    rows = lax.broadcasted_iota(jnp.int32, (tt, 1), 0)
    prev_row = jnp.where(i == 0, 0.0, pp_ref[0, 7:8, :])
    next_row = jnp.where(i == last, 0.0, pn_ref[0, 0:1, :])
    prev = jnp.where(rows == 0, prev_row, pltpu.roll(p, 1, 0))
    nxt = jnp.where(rows == tt - 1, next_row, pltpu.roll(p, tt - 1, 0))
    pf = p + mu_ref[0:1, :] * (prev - p) + mu_ref[1:2, :] * (nxt - p)
    r = pf[:, :c]
    k = pf[:, c:2 * c]
    v = pf[:, 2 * c:3 * c]
    lw = 3 * c
    nd = w2_ref.shape[0]
    na = a2_ref.shape[0]
    wd = pf[:, lw:lw + nd]
    ad = pf[:, lw + nd:lw + nd + na]
    gd = pf[:, lw + nd + na:]
    if has_vres:
        lora = _dot(_dot(hn_ref[0], v1_ref[...]).astype(BF16), v2_ref[...])
        v = v + (vf_ref[0] - v) * jax.nn.sigmoid(v0_ref[...] + lora)
    g = _dot(jax.nn.sigmoid(gd).astype(BF16), g2_ref[...])
    kk = k * kk_ref[...]
    nrm = jnp.sqrt(_dot(kk * kk, sum_ref[...], HP))
    kk = kk / jnp.maximum(nrm, 1e-12)
    wl = w0_ref[...] + _dot(jnp.tanh(wd).astype(BF16), w2_ref[...])
    w = -jax.nn.softplus(-wl) - 0.5
    logw = -jnp.exp(w)
    a = jax.nn.sigmoid(a0_ref[...] + _dot(ad.astype(BF16), a2_ref[...]))
    ka = ka_ref[...]
    rk = rk_ref[...]
    bon = jnp.zeros_like(v)
    k_d = []
    for d in range(2):
        a_d = a[:, d * c:(d + 1) * c]
        kd = k * (1.0 + (a_d - 1.0) * ka)
        k_d.append(kd)
        bon = bon + _dot(r * kd * rk, sum_ref[...], HP) * v
    r_o[0] = r
    v_o[0] = v
    kk_o[0] = kk
    lwf_o[0] = logw[:, :c]
    lwb_o[0] = logw[:, c:]
    kf_o[0] = k_d[0]
    kb_o[0] = k_d[1]
    af_o[0] = a[:, :c]
    ab_o[0] = a[:, c:]
    g_o[0] = g
    bon_o[0] = bon


def _blockdiag2(m):
    k, c = m.shape[1], m.shape[2]
    z = jnp.zeros((k, c), m.dtype)
    return jnp.concatenate([jnp.concatenate([m[0], z], axis=1), jnp.concatenate([z, m[1]], axis=1)], axis=0)


def _rw_prep(p, hn, v_first, shift, w0, w2, a0, a2, g2, k_k, k_a, r_k, vres, c, tt=256):
    b, t, pw = p.shape
    tt = min(tt, t)
    has_vres = vres is not None
    ch = jnp.arange(c) // HEAD
    summ = (ch[:, None] == ch[None, :]).astype(F32)
    row = lambda a: a.reshape(1, -1)
    args = [p, p, p, shift, row(w0), _blockdiag2(w2).astype(BF16), row(a0), _blockdiag2(a2).astype(BF16),
            g2.astype(BF16), row(k_k), row(k_a), row(r_k), summ]
    full = lambda a: pl.BlockSpec(a.shape, lambda i, j: (0,) * a.ndim)
    nb8 = t // 8
    in_specs = [pl.BlockSpec((1, tt, pw), lambda i, j: (i, j, 0)),
                pl.BlockSpec((1, 8, pw), lambda i, j: (i, jnp.maximum(j * (tt // 8) - 1, 0), 0)),
                pl.BlockSpec((1, 8, pw), lambda i, j: (i, jnp.minimum((j + 1) * (tt // 8), nb8 - 1), 0))]
    in_specs += [full(a) for a in args[3:]]
    if has_vres:
        v0, v1, v2 = vres
        extra = [hn, v_first, row(v0), v1.astype(BF16), v2.astype(BF16)]
        in_specs += [pl.BlockSpec((1, tt, hn.shape[2]), lambda i, j: (i, j, 0)),
                     pl.BlockSpec((1, tt, c), lambda i, j: (i, j, 0))] + [full(a) for a in extra[2:]]
        args += extra
    outs = tuple(jax.ShapeDtypeStruct((b, t, c), F32) for _ in range(11))
    return pl.pallas_call(
        functools.partial(_rw_prep_kernel, has_vres=has_vres, c=c),
        out_shape=outs,
        grid=(b, t // tt),
        in_specs=in_specs,
        out_specs=tuple(pl.BlockSpec((1, tt, c), lambda i, j: (i, j, 0)) for _ in range(11)),
        compiler_params=_params(("parallel", "parallel")),
        name="rw_prep",
    )(*args)


def _split3(x):
    h1 = x.astype(BF16)
    r1 = x - h1.astype(F32)
    h2 = r1.astype(BF16)
    h3 = (r1 - h2.astype(F32)).astype(BF16)
    return h1, h2, h3


def _scan_chunks(s0, r, k, v, kk, a, lw, rev, bdm):
    L = CHUNK
    n = len(r)
    each = lambda f, *ls: [f(*xs) for xs in zip(*ls)]
    ti = lax.broadcasted_iota(jnp.int32, (L, L), 0)
    si = lax.broadcasted_iota(jnp.int32, (L, L), 1)
    tt = lax.broadcasted_iota(jnp.int32, (L, GW), 0)
    ss = lax.broadcasted_iota(jnp.int32, (L, GW), 1) % L
    eye = (ss == tt).astype(F32)
    tri = [((si >= ti) if q else (si <= ti)).astype(F32).astype(BF16) for q in rev]
    strict = [(ss > tt) if q else (ss < tt) for q in rev]
    incl = [(ss >= tt) if q else (ss <= tt) for q in rev]
    bdf = bdm.astype(F32)

    def bd(x):
        xb = x.astype(BF16)
        return jnp.concatenate([xb] * HEADS_PER_GROUP, axis=0) * bdm

    def rcmul(x, y):
        return _dot(x.astype(BF16), bd(y))

    parts = each(_split3, lw)
    cum = [_dot(t, p[0]) + _dot(t, p[1]) + _dot(t, p[2]) for t, p in zip(tri, parts)]
    tot = [c[0:1, :] if q else c[L - 1:L, :] for c, q in zip(cum, rev)]
    p_in = each(jnp.exp, cum)
    p_inv = each(lambda c: jnp.exp(-c), cum)
    p_ex = each(lambda c, w: jnp.exp(c - w), cum, lw)
    p_rem = each(lambda t, c: jnp.exp(t - c), tot, cum)
    beta = each(lambda x, y: x * y, kk, a)
    a_t = each(lambda x, p: -x * p, kk, p_ex)
    r_t = each(lambda x, p: x * p, r, p_in)
    k_t = each(lambda x, p: x * p, k, p_inv)
    b_t = each(lambda x, p: x * p, beta, p_inv)
    k_h = each(lambda x, p: x * p, k, p_rem)
    b_h = each(lambda x, p: x * p, beta, p_rem)

    lhs = each(lambda x, y: jnp.concatenate([x, y], axis=0).astype(BF16), a_t, r_t)
    akk = each(lambda x, y: _dot_nt(x, bd(y)), lhs, k_t)
    abb = each(lambda x, y: _dot_nt(x, bd(y)), lhs, b_t)
    a_ak = each(lambda m, x: jnp.where(m, x[:L], 0.0), strict, akk)
    a_rk = each(lambda m, x: jnp.where(m, x[L:], 0.0), incl, akk)
    a_ab = each(lambda m, x: jnp.where(m, x[:L], 0.0), strict, abb)
    a_rb = each(lambda m, x: jnp.where(m, x[L:], 0.0), incl, abb)

    pw = a_ab
    tinv = each(lambda x: eye + x, pw)
    for _ in range(5):
        pw = each(rcmul, pw, pw)
        tinv = each(lambda t, p: t + rcmul(t, p), tinv, pw)

    av = each(lambda x, y, z: rcmul(jnp.concatenate([x, y], axis=0), z), a_ak, a_rk, v)
    wu = each(lambda t, x, y: _dot(t.astype(BF16), jnp.concatenate([bd(x), bd(y[:L])], axis=1)), tinv, a_t, av)
    rbwu = each(lambda x, w: _dot(x.astype(BF16), jnp.concatenate([bd(w[:, :GW]), bd(w[:, GW:])], axis=1)),
                a_rb, wu)
    o = each(lambda x, w, s, y: _dot_nt((x + w[:, :GW]).astype(BF16), bd(s)) + y[L:] + w[:, GW:],
             r_t, rbwu, s0, av)

    m_bd = each(lambda w, x: _dot_tn(w[:, :GW].astype(BF16), x.astype(BF16)) * bdf, wu, b_h)
    n_full = each(lambda x, w, y, z: _dot_tn(jnp.concatenate([x, w[:, GW:]], axis=0).astype(BF16),
                                             jnp.concatenate([y, z], axis=0).astype(BF16)) * bdf,
                  v, wu, k_h, b_h)
    s1 = []
    for i in range(n):
        n_rc = n_full[i][0:HEAD]
        for h in range(1, HEADS_PER_GROUP):
            n_rc = n_rc + n_full[i][h * HEAD:(h + 1) * HEAD]
        s1.append(_dot(s0[i].astype(BF16), m_bd[i].astype(BF16)) + s0[i] * jnp.exp(tot[i]) + n_rc)
    return s1, o


def _scan_kernel(rf_ref, kf_ref, vf_ref, kkf_ref, af_ref, lwf_ref,
                 rb_ref, kb_ref, vb_ref, kkb_ref, ab_ref, lwb_ref, of_ref, ob_ref, s_ref, *, ng):
    @pl.when(pl.program_id(1) == 0)
    def _():
        s_ref[...] = jnp.zeros_like(s_ref)

    gi = lax.broadcasted_iota(jnp.int32, (GW, GW), 0) // HEAD
    gj = lax.broadcasted_iota(jnp.int32, (GW, GW), 1) // HEAD
    bdm = (gi == gj).astype(F32).astype(BF16)
    dirs = ((rf_ref, kf_ref, vf_ref, kkf_ref, af_ref, lwf_ref), (rb_ref, kb_ref, vb_ref, kkb_ref, ab_ref, lwb_ref))
    chains = [(d, h) for d in range(2) for h in range(ng)]
    ins = [[ref[0, :, h * GW:(h + 1) * GW] for d, h in chains for ref in (dirs[d][i],)] for i in range(6)]
    s1, o = _scan_chunks([s_ref[d, h] for d, h in chains], *ins, [d == 1 for d, _ in chains], bdm)
    for (d, h), s_new, o_new in zip(chains, s1, o):
        (of_ref, ob_ref)[d][0, :, h * GW:(h + 1) * GW] = o_new
        s_ref[d, h] = s_new


def _wkv_scan(r, v, kk, kf, af, lwf, kb, ab, lwb):
    b, t, c = r.shape
    nc = t // CHUNK
    ng = c // GW
    fspec = pl.BlockSpec((1, CHUNK, c), lambda i, j: (i, j, 0))
    bspec = pl.BlockSpec((1, CHUNK, c), lambda i, j: (i, nc - 1 - j, 0))
    return pl.pallas_call(
        functools.partial(_scan_kernel, ng=ng),
        out_shape=(jax.ShapeDtypeStruct((b, t, c), F32), jax.ShapeDtypeStruct((b, t, c), F32)),
        grid=(b, nc),
        in_specs=[fspec] * 6 + [bspec] * 6,
        out_specs=(fspec, bspec),
        scratch_shapes=[pltpu.VMEM((2, ng, HEAD, GW), F32)],
        compiler_params=_params(("parallel", "arbitrary")),
        name="rw_scan",
    )(r, kf, v, kk, af, lwf, r, kb, v, kk, ab, lwb)


def _rw_post_kernel(sf_ref, sb_ref, bon_ref, g_ref, lg_ref, lb_ref, avg_ref, o_ref):
    s = sf_ref[...] + sb_ref[...]
    mean = _dot(s, avg_ref[...], HP)
    d = s - mean
    var = _dot(d * d, avg_ref[...], HP)
    y = d * lax.rsqrt(var + GN_EPS) * lg_ref[...] + lb_ref[...] + bon_ref[...]
    o_ref[...] = (y * g_ref[...]).astype(o_ref.dtype)


def _rw_post(sf, sb, bon, g, lnx_g, lnx_b, tm=512):
    b, t, c = sf.shape
    m = b * t
    tm = min(tm, m)
    ch = jnp.arange(c) // HEAD
    avg = (ch[:, None] == ch[None, :]).astype(F32) / HEAD
    big = pl.BlockSpec((tm, c), lambda i: (i, 0))
    small = pl.BlockSpec((1, c), lambda i: (0, 0))
    out = pl.pallas_call(
        _rw_post_kernel,
        out_shape=jax.ShapeDtypeStruct((m, c), BF16),
        grid=(m // tm,),
        in_specs=[big, big, big, big, small, small, pl.BlockSpec((c, c), lambda i: (0, 0))],
        out_specs=big,
        compiler_params=_params(("parallel",)),
        name="rw_post",
    )(sf.reshape(m, c), sb.reshape(m, c), bon.reshape(m, c), g.reshape(m, c),
      lnx_g.reshape(1, c), lnx_b.reshape(1, c), avg)
    return out.reshape(b, t, c)


def _rwkv_branch(hn, w_rw, v_first, shift, w0, w2, a0, a2, g2, k_k, k_a, r_k, lnx_g, lnx_b, vres, c):
    b, t, d = hn.shape
    p = _matmul(hn.reshape(b * t, d), w_rw, F32).reshape(b, t, -1)
    r, v, kk, lwf, lwb, kf, kb, af, ab, g, bon = _rw_prep(
        p, hn, v_first, shift, w0, w2, a0, a2, g2, k_k, k_a, r_k, vres, c)
    sf, sb = _wkv_scan(r, v, kk, kf, af, lwf, kb, ab, lwb)
    y = _rw_post(sf, sb, bon, g, lnx_g, lnx_b)
    return y, (v if vres is None else v_first)


def _forward(x, norm1_g, w_in, hy_conv_w, hy_conv_b, hy_f_w1, hy_f_b1, hy_f_w2, hy_f_b2, hy_f_w3, hy_f_b3,
             hy_f_freq, hy_f_wout, hy_skip, hy_norm_g, rw_shift, rw_w0, rw_w2, rw_a0, rw_a2, rw_g2, rw_k_k,
             rw_k_a, rw_r_k, rw_lnx_g, rw_lnx_b, rw_v0, rw_v1, rw_v2, w_out, norm2_g, mlp_w1, mlp_w2,
             final_g):
    b, t, d = x.shape
    depth = w_in.shape[0]
    c_hy = hy_norm_g.shape[1]
    c_rw = rw_lnx_g.shape[1]
    hy_proj = hy_conv_b.shape[1]
    assert b % 2 == 0 and t % CHUNK == 0 and CHUNK == HEAD
    tabs = _dft_tables(t)
    n1h, n2 = tabs["n1h"], tabs["n2"]
    gp = max(1, min(n2, 512 // n1h))
    v_first = None
    for l in range(depth):
        hn = _rmsnorm(x.reshape(b * t, d), norm1_g[l], BF16).reshape(b, t, d)
        w_l = w_in[l].astype(BF16)
        y_hy = _hyena_branch(hn, w_l[:, :hy_proj], hy_conv_w[l], hy_conv_b[l], hy_f_w1[l], hy_f_b1[l],
                             hy_f_w2[l], hy_f_b2[l], hy_f_w3[l], hy_f_b3[l], hy_f_freq[l], hy_f_wout[l],
                             hy_skip[l], hy_norm_g[l], tabs)
        vres = None if l == 0 else (rw_v0[l - 1], rw_v1[l - 1], rw_v2[l - 1])
        y_rw, v_first = _rwkv_branch(hn, w_l[:, hy_proj:], v_first, rw_shift[l], rw_w0[l], rw_w2[l], rw_a0[l],
                                     rw_a2[l], rw_g2[l], rw_k_k[l], rw_k_a[l], rw_r_k[l], rw_lnx_g[l],
                                     rw_lnx_b[l], vres, c_rw)
        wo = w_out[l].astype(BF16)
        x = _proj_perm_out(y_hy, wo[:c_hy], x, n1h, n2, gp)
        x = _matmul(y_rw.reshape(b * t, c_rw), wo[c_hy:], F32, residual=x.reshape(b * t, d)).reshape(b, t, d)
        hn2 = _rmsnorm(x.reshape(b * t, d), norm2_g[l], BF16)
        hmid = _matmul(hn2, mlp_w1[l].astype(BF16), BF16, act="relu2")
        x = _matmul(hmid, mlp_w2[l].astype(BF16), F32, residual=x.reshape(b * t, d)).reshape(b, t, d)
    return _rmsnorm(x.reshape(b * t, d), final_g, F32).reshape(b, t, d)


def kernel(x, norm1_g, w_in, hy_conv_w, hy_conv_b, hy_f_w1, hy_f_b1, hy_f_w2, hy_f_b2, hy_f_w3, hy_f_b3,
           hy_f_freq, hy_f_wout, hy_skip, hy_norm_g, rw_shift, rw_w0, rw_w2, rw_a0, rw_a2, rw_g2, rw_k_k,
           rw_k_a, rw_r_k, rw_lnx_g, rw_lnx_b, rw_v0, rw_v1, rw_v2, w_out, norm2_g, mlp_w1, mlp_w2, final_g):
    return _forward(x, norm1_g, w_in, hy_conv_w, hy_conv_b, hy_f_w1, hy_f_b1, hy_f_w2, hy_f_b2, hy_f_w3,
                    hy_f_b3, hy_f_freq, hy_f_wout, hy_skip, hy_norm_g, rw_shift, rw_w0, rw_w2, rw_a0, rw_a2,
                    rw_g2, rw_k_k, rw_k_a, rw_r_k, rw_lnx_g, rw_lnx_b, rw_v0, rw_v1, rw_v2, w_out, norm2_g,
                    mlp_w1, mlp_w2, final_g)
```

```python
import functools
import math

import jax
import jax.numpy as jnp
from jax import lax
from jax.experimental import pallas as pl
from jax.experimental.pallas import tpu as pltpu

F32 = jnp.float32
BF16 = jnp.bfloat16
HP = lax.Precision.HIGHEST

HEAD = 64
HEADS_PER_GROUP = 4
GW = HEAD * HEADS_PER_GROUP
CHUNK = 64
RMS_EPS = 1e-5
GN_EPS = HEAD * 1e-5
HY_TARGET = 1e-2
HY_FAST_PCT = 0.3
HY_SLOW_PCT = 1.5
VMEM_LIMIT = 56 * 1024 * 1024


def _params(sem, vmem=VMEM_LIMIT):
    return pltpu.CompilerParams(dimension_semantics=sem, vmem_limit_bytes=vmem)


def _dot(a, b, precision=None):
    return jnp.dot(a, b, preferred_element_type=F32, precision=precision)


def _dot_nt(a, b, precision=None):
    return lax.dot_general(a, b, (((1,), (1,)), ((), ())), preferred_element_type=F32,
                           precision=precision)


def _dot_tn(a, b, precision=None):
    return lax.dot_general(a, b, (((0,), (0,)), ((), ())), preferred_element_type=F32,
                           precision=precision)


def _rmsnorm_kernel(x_ref, g_ref, o_ref):
    x = x_ref[...]
    y = x * lax.rsqrt(jnp.mean(x * x, axis=-1, keepdims=True) + RMS_EPS)
    o_ref[...] = (y * g_ref[...]).astype(o_ref.dtype)


def _rmsnorm(x2d, g, out_dtype, tm=1024):
    m, d = x2d.shape
    tm = min(tm, m)
    return pl.pallas_call(
        _rmsnorm_kernel,
        out_shape=jax.ShapeDtypeStruct((m, d), out_dtype),
        grid=(m // tm,),
        in_specs=[pl.BlockSpec((tm, d), lambda i: (i, 0)), pl.BlockSpec((1, d), lambda i: (0, 0))],
        out_specs=pl.BlockSpec((tm, d), lambda i: (i, 0)),
        compiler_params=_params(("parallel",)),
        name="rmsnorm",
    )(x2d, g.reshape(1, d))


def _matmul_kernel(*refs, act, has_res):
    if has_res:
        x_ref, w_ref, r_ref, o_ref = refs
    else:
        x_ref, w_ref, o_ref = refs
    acc = _dot(x_ref[...], w_ref[...])
    if act == "relu2":
        acc = jnp.square(jnp.maximum(acc, 0.0))
    if has_res:
        acc = acc + r_ref[...]
    o_ref[...] = acc.astype(o_ref.dtype)


def _matmul(x, w, out_dtype, *, act=None, residual=None, tm=512):
    m, k = x.shape
    n = w.shape[1]
    tm = min(tm, m)
    in_specs = [pl.BlockSpec((tm, k), lambda i: (i, 0)), _const_spec((k, n))]
    args = [x, w]
    if residual is not None:
        in_specs.append(pl.BlockSpec((tm, n), lambda i: (i, 0)))
        args.append(residual)
    return pl.pallas_call(
        functools.partial(_matmul_kernel, act=act, has_res=residual is not None),
        out_shape=jax.ShapeDtypeStruct((m, n), out_dtype),
        grid=(m // tm,),
        in_specs=in_specs,
        out_specs=pl.BlockSpec((tm, n), lambda i: (i, 0)),
        compiler_params=_params(("parallel",)),
        name="matmul",
    )(*args)


def _const_spec(shape):
    return pl.BlockSpec(shape, lambda *_: (0,) * len(shape), pipeline_mode=pl.Buffered(1))


def _dot2(x, m):
    hi = x.astype(BF16)
    lo = (x - hi.astype(F32)).astype(BF16)
    return _dot(hi, m) + _dot(lo, m)


def _mlp_kernel(x_ref, g_ref, w1_ref, w2_ref, o_ref, *, nchunk):
    x = x_ref[...]
    hn = (x * lax.rsqrt(jnp.mean(x * x, axis=-1, keepdims=True) + RMS_EPS) * g_ref[...]).astype(BF16)
    ff = w1_ref.shape[1]
    cw = ff // nchunk
    acc = x
    for j in range(nchunk):
        h = _dot(hn, w1_ref[:, j * cw:(j + 1) * cw])
        h = jnp.square(jnp.maximum(h, 0.0)).astype(BF16)
        acc = acc + _dot(h, w2_ref[j * cw:(j + 1) * cw, :])
    o_ref[...] = acc


def _mlp(x2d, g, w1, w2, tm=512, nchunk=4):
    m, d = x2d.shape
    ff = w1.shape[1]
    tm = min(tm, m)
    return pl.pallas_call(
        functools.partial(_mlp_kernel, nchunk=nchunk),
        out_shape=jax.ShapeDtypeStruct((m, d), F32),
        grid=(m // tm,),
        in_specs=[pl.BlockSpec((tm, d), lambda i: (i, 0)), _const_spec((1, d)),
                  _const_spec((d, ff)), _const_spec((ff, d))],
        out_specs=pl.BlockSpec((tm, d), lambda i: (i, 0)),
        compiler_params=_params(("parallel",)),
        name="mlp",
    )(x2d, g.reshape(1, d), w1, w2)


def _proj_perm_in_kernel(x_ref, w_ref, o_ref, *, g, d):
    lhs = jnp.concatenate([x_ref[0, :, j * d:(j + 1) * d] for j in range(g)], axis=0)
    o_ref[0] = _dot(lhs, w_ref[...]).astype(o_ref.dtype)


def _proj_perm_in(hn, w, n1h, n2, g, out_dtype):
    b, t, d = hn.shape
    p = w.shape[1]
    xv = hn.reshape(b, n1h, n2 * d)
    return pl.pallas_call(
        functools.partial(_proj_perm_in_kernel, g=g, d=d),
        out_shape=jax.ShapeDtypeStruct((b, t, p), out_dtype),
        grid=(b, n2 // g),
        in_specs=[pl.BlockSpec((1, n1h, g * d), lambda i, j: (i, 0, j)),
                  pl.BlockSpec((d, p), lambda i, j: (0, 0))],
        out_specs=pl.BlockSpec((1, g * n1h, p), lambda i, j: (i, j, 0)),
        compiler_params=_params(("parallel", "parallel")),
        name="proj_perm_in",
    )(xv, w)


def _proj_perm_out_kernel(y_ref, w_ref, r_ref, o_ref, *, g, d, n1h):
    acc = _dot(y_ref[0], w_ref[...])
    for j in range(g):
        o_ref[0, :, j * d:(j + 1) * d] = acc[j * n1h:(j + 1) * n1h] + r_ref[0, :, j * d:(j + 1) * d]


def _proj_perm_out(y, w, res, n1h, n2, g):
    b, t, c = y.shape
    d = w.shape[1]
    rv = res.reshape(b, n1h, n2 * d)
    out = pl.pallas_call(
        functools.partial(_proj_perm_out_kernel, g=g, d=d, n1h=n1h),
        out_shape=jax.ShapeDtypeStruct((b, n1h, n2 * d), F32),
        grid=(b, n2 // g),
        in_specs=[pl.BlockSpec((1, g * n1h, c), lambda i, j: (i, j, 0)),
                  pl.BlockSpec((c, d), lambda i, j: (0, 0)),
                  pl.BlockSpec((1, n1h, g * d), lambda i, j: (i, 0, j))],
        out_specs=pl.BlockSpec((1, n1h, g * d), lambda i, j: (i, 0, j)),
        compiler_params=_params(("parallel", "parallel")),
        name="proj_perm_out",
    )(y, w, rv)
    return out.reshape(b, t, d)


def _shortconv_kernel(p_ref, w_ref, b_ref, o_ref, *, n1h, n2):
    w0 = w_ref[0:1, :]
    w1 = w_ref[1:2, :]
    w2 = w_ref[2:3, :]
    bias = b_ref[...]
    rows = lax.broadcasted_iota(jnp.int32, (n1h, 1), 0)

    def blk(i):
        return p_ref[0, pl.ds(pl.multiple_of(i * n1h, n1h), n1h), :]

    def body(i, carry):
        o_ref[0, pl.ds(pl.multiple_of(i * n1h, n1h), n1h), :] = (
            w0 * blk(i - 1) + w1 * blk(i) + w2 * blk(i + 1) + bias).astype(o_ref.dtype)
        return carry

    lax.fori_loop(1, n2 - 1, body, 0)
    last = p_ref[0, (n2 - 1) * n1h:, :]
    first = p_ref[0, :n1h, :]
    prev0 = jnp.where(rows == 0, 0.0, pltpu.roll(last, 1, 0))
    o_ref[0, :n1h, :] = (w0 * prev0 + w1 * first + w2 * p_ref[0, n1h:2 * n1h, :] + bias).astype(o_ref.dtype)
    nxt = jnp.where(rows == n1h - 1, 0.0, pltpu.roll(first, n1h - 1, 0))
    o_ref[0, (n2 - 1) * n1h:, :] = (w0 * p_ref[0, (n2 - 2) * n1h:(n2 - 1) * n1h, :] + w1 * last
                                    + w2 * nxt + bias).astype(o_ref.dtype)


def _shortconv(p, w, bias, n1h, n2, cb=128):
    b, t, c = p.shape
    return pl.pallas_call(
        functools.partial(_shortconv_kernel, n1h=n1h, n2=n2),
        out_shape=jax.ShapeDtypeStruct((b, t, c), F32),
        grid=(b, c // cb),
        in_specs=[pl.BlockSpec((1, t, cb), lambda i, j: (i, 0, j)),
                  pl.BlockSpec((3, cb), lambda i, j: (0, j)),
                  pl.BlockSpec((1, cb), lambda i, j: (0, j))],
        out_specs=pl.BlockSpec((1, t, cb), lambda i, j: (i, 0, j)),
        compiler_params=_params(("parallel", "parallel")),
        name="hy_shortconv",
    )(p, w, bias.reshape(1, c))


def _dft_tables(t):
    n = 2 * t
    n1 = 1 << ((n.bit_length() - 1) // 2)
    n2 = n // n1
    n1h = n1 // 2
    two_pi = 2.0 * math.pi
    k1 = jnp.arange(n1, dtype=jnp.int32)
    n2i = jnp.arange(n2, dtype=jnp.int32)

    def cs(prod, mod):
        ang = (two_pi / mod) * (prod % mod).astype(F32)
        return jnp.cos(ang), jnp.sin(ang)

    tpos = n2i[:, None] + n2 * jnp.arange(n1h, dtype=jnp.int32)[None, :]
    c, s = cs(k1[None, :, None] * tpos[:, None, :], n)
    g1 = jnp.concatenate([jnp.concatenate([c, s], axis=2), jnp.concatenate([-s, c], axis=2)], axis=1)
    ct, st = jnp.swapaxes(c, 1, 2) / n, jnp.swapaxes(s, 1, 2) / n
    g4a = jnp.concatenate([ct, st], axis=1)
    g4b = jnp.concatenate([-st, ct], axis=1)
    tposf = n2i[:, None] + n2 * jnp.arange(n1, dtype=jnp.int32)[None, :]
    cf, sf = cs(k1[None, :, None] * tposf[:, None, :], n)
    g1f = jnp.concatenate([cf, -sf], axis=1)
    c2, s2 = cs(n2i[:, None] * n2i[None, :], n2)
    f2a = jnp.concatenate([c2, -s2], axis=0)
    f2b = jnp.concatenate([s2, c2], axis=0)
    f3 = jnp.concatenate([jnp.concatenate([c2, -s2], axis=1), jnp.concatenate([s2, c2], axis=1)], axis=0)
    return dict(n1=n1, n2=n2, n1h=n1h, g1=g1.astype(BF16), g4a=g4a.astype(BF16), g4b=g4b.astype(BF16),
                g1f=g1f, f2a=f2a, f2b=f2b, f3=f3.astype(BF16))


def _filter_mlp_kernel(z_ref, w1_ref, b1_ref, w2_ref, b2_ref, w3_ref, b3_ref, fr_ref, wo_ref, dl_ref,
                       o_ref):
    z = z_ref[...]
    h = jnp.sin(fr_ref[0:1, :] * (_dot(z, w1_ref[...], HP) + b1_ref[...]))
    h = jnp.sin(fr_ref[1:2, :] * (_dot(h, w2_ref[...], HP) + b2_ref[...]))
    h = jnp.sin(fr_ref[2:3, :] * (_dot(h, w3_ref[...], HP) + b3_ref[...]))
    window = jnp.exp(-z[:, 0:1] * dl_ref[...])
    o_ref[...] = _dot(h, wo_ref[...], HP) * window


def _filter_mlp(t, w1, b1, w2, b2, w3, b3, freq, w_out, c_hy, tr=512):
    emb, width = w1.shape
    bands = (emb - 1) // 2
    pos = jnp.arange(t, dtype=F32)
    tt = pos / max(t - 1, 1)
    fr = jnp.linspace(1e-4, bands - 1, bands, dtype=F32)
    ang = (2.0 * math.pi / t) * pos[:, None] * fr[None, :]
    z = jnp.concatenate([tt[:, None], jnp.cos(ang), -jnp.sin(ang)], axis=-1)
    z = jnp.pad(z, ((0, 0), (0, 128 - emb)))
    w1 = jnp.pad(w1, ((0, 128 - emb), (0, 0)))
    emb = 128
    max_decay = math.log(HY_TARGET) / HY_FAST_PCT
    min_decay = math.log(HY_TARGET) / HY_SLOW_PCT
    delta = jnp.abs(jnp.linspace(min_decay, max_decay, c_hy, dtype=F32))
    nout = w_out.shape[1]
    delta_full = jnp.tile(delta, nout // c_hy).reshape(1, nout)
    tr = min(tr, t)
    full = lambda a: pl.BlockSpec(a.shape, lambda i: (0,) * a.ndim)
    args = [z, w1, b1.reshape(1, -1), w2, b2.reshape(1, -1), w3, b3.reshape(1, -1), freq, w_out, delta_full]
    return pl.pallas_call(
        _filter_mlp_kernel,
        out_shape=jax.ShapeDtypeStruct((t, nout), F32),
        grid=(t // tr,),
        in_specs=[pl.BlockSpec((tr, emb), lambda i: (i, 0))] + [full(a) for a in args[1:]],
        out_specs=pl.BlockSpec((tr, nout), lambda i: (i, 0)),
        compiler_params=_params(("parallel",)),
        name="hy_filter_mlp",
    )(*args)


def _filter_s1_kernel(x_ref, g_ref, o_ref, ss_ref, *, g, oc):
    @pl.when(pl.program_id(0) == 0)
    def _():
        ss_ref[...] = jnp.zeros_like(ss_ref)

    ss = ss_ref[...]
    for j in range(g):
        x = x_ref[:, j * oc:(j + 1) * oc]
        o_ref[j] = _dot(g_ref[j], x, HP)
        ss = ss + jnp.sum(x * x, axis=0, keepdims=True)
    ss_ref[...] = ss


def _filter_s2_kernel(sr_ref, si_ref, fa_ref, fb_ref, ss_ref, o_ref, *, g, oc):
    scale = lax.rsqrt(ss_ref[...] + 1e-6)
    for j in range(g):
        x = _dot(fa_ref[...], sr_ref[:, j * oc:(j + 1) * oc], HP) + _dot(fb_ref[...], si_ref[:, j * oc:(j + 1) * oc], HP)
        o_ref[j] = x * scale


def _filter_spectrum(two, tabs, g=4):
    n, oc = two.shape
    n1, n2 = tabs["n1"], tabs["n2"]
    g = min(g, n1, n2)
    xv = two.reshape(n1, n2 * oc)
    s1, ss = pl.pallas_call(
        functools.partial(_filter_s1_kernel, g=g, oc=oc),
        out_shape=(jax.ShapeDtypeStruct((n2, 2 * n1, oc), F32), jax.ShapeDtypeStruct((1, oc), F32)),
        grid=(n2 // g,),
        in_specs=[pl.BlockSpec((n1, g * oc), lambda j: (0, j)),
                  pl.BlockSpec((g, 2 * n1, n1), lambda j: (j, 0, 0))],
        out_specs=(pl.BlockSpec((g, 2 * n1, oc), lambda j: (j, 0, 0)),
                   pl.BlockSpec((1, oc), lambda j: (0, 0))),
        compiler_params=_params(("arbitrary",)),
        name="hy_filter_dft1",
    )(xv, tabs["g1f"])
    sv = s1.reshape(n2, 2 * n1 * oc)
    return pl.pallas_call(
        functools.partial(_filter_s2_kernel, g=g, oc=oc),
        out_shape=jax.ShapeDtypeStruct((n1, 2 * n2, oc), F32),
        grid=(n1 // g,),
        in_specs=[pl.BlockSpec((n2, g * oc), lambda k: (0, k)),
                  pl.BlockSpec((n2, g * oc), lambda k: (0, n1 // g + k)),
                  pl.BlockSpec((2 * n2, n2), lambda k: (0, 0)),
                  pl.BlockSpec((2 * n2, n2), lambda k: (0, 0)),
                  pl.BlockSpec((1, oc), lambda k: (0, 0))],
        out_specs=pl.BlockSpec((g, 2 * n2, oc), lambda k: (k, 0, 0)),
        compiler_params=_params(("parallel",)),
        name="hy_filter_dft2",
    )(sv, sv, tabs["f2a"], tabs["f2b"], ss)


def _conv_s1_kernel(u_ref, g_ref, o_ref, *, g, n1h):
    for j in range(g):
        rows = slice(j * n1h, (j + 1) * n1h)
        rhs = jnp.concatenate([u_ref[0, 0, rows, :], u_ref[0, 1, rows, :]], axis=0).astype(BF16)
        o_ref[0, j] = _dot(g_ref[j], rhs).astype(o_ref.dtype)


def _conv_s23_kernel(sr_ref, si_ref, fa_ref, fb_ref, f3_ref, h_ref, o_ref, *, g, c, n2):
    for j in range(g):
        cols = slice(j * c, (j + 1) * c)
        x = _dot(fa_ref[...], sr_ref[0, :, cols]) + _dot(fb_ref[...], si_ref[0, :, cols])
        xr, xi = x[:n2], x[n2:]
        hr, hi = h_ref[j, :n2, :], h_ref[j, n2:, :]
        y = jnp.concatenate([xr * hr - xi * hi, xr * hi + xi * hr], axis=0).astype(BF16)
        o_ref[0, j] = _dot(f3_ref[...], y).astype(o_ref.dtype)


def _conv_s4_kernel(rr_ref, ri_ref, ga_ref, gb_ref, u_ref, gate_ref, skip_ref, *rest, g, c, n1h, norm):
    if norm:
        ng_ref, avg_ref, o_ref = rest
    else:
        (o_ref,) = rest
    skip = skip_ref[...]
    for j in range(g):
        cols = slice(j * c, (j + 1) * c)
        rows = slice(j * n1h, (j + 1) * n1h)
        y = _dot(ga_ref[j], rr_ref[0, :, cols]) + _dot(gb_ref[j], ri_ref[0, :, cols])
        for q in range(2):
            z = gate_ref[0, q, rows, :] * (y[q * n1h:(q + 1) * n1h] + u_ref[0, q, rows, :] * skip)
            if norm:
                ms = _dot2(z * z, avg_ref[...])
                z = z * lax.rsqrt(ms + RMS_EPS) * ng_ref[...]
            o_ref[0, q, rows, :] = z.astype(o_ref.dtype)


def _long_conv_gate(u_arr, u_col, gate_arr, gate_col, h_spec, h_col, skip, tabs, c, out_dtype,
                    norm_g=None, g=8):
    b, t, _ = u_arr.shape
    n1, n2, n1h = tabs["n1"], tabs["n2"], tabs["n1h"]
    g = min(g, n1, n2)
    npair = b // 2
    u4 = u_arr.reshape(npair, 2, t, u_arr.shape[2])
    gate4 = gate_arr.reshape(npair, 2, t, gate_arr.shape[2])
    s1 = pl.pallas_call(
        functools.partial(_conv_s1_kernel, g=g, n1h=n1h),
        out_shape=jax.ShapeDtypeStruct((npair, n2, 2 * n1, c), BF16),
        grid=(npair, n2 // g),
        in_specs=[pl.BlockSpec((1, 2, g * n1h, c), lambda p, j: (p, 0, j, u_col)),
                  pl.BlockSpec((g, 2 * n1, n1), lambda p, j: (j, 0, 0))],
        out_specs=pl.BlockSpec((1, g, 2 * n1, c), lambda p, j: (p, j, 0, 0)),
        compiler_params=_params(("parallel", "parallel")),
        name="hy_conv_dft1",
    )(u4, tabs["g1"])
    sv = s1.reshape(npair, n2, 2 * n1 * c)
    r = pl.pallas_call(
        functools.partial(_conv_s23_kernel, g=g, c=c, n2=n2),
        out_shape=jax.ShapeDtypeStruct((npair, n1, 2 * n2, c), BF16),
        grid=(npair, n1 // g),
        in_specs=[pl.BlockSpec((1, n2, g * c), lambda p, k: (p, 0, k)),
                  pl.BlockSpec((1, n2, g * c), lambda p, k: (p, 0, n1 // g + k)),
                  pl.BlockSpec((2 * n2, n2), lambda p, k: (0, 0)),
                  pl.BlockSpec((2 * n2, n2), lambda p, k: (0, 0)),
                  pl.BlockSpec((2 * n2, 2 * n2), lambda p, k: (0, 0)),
                  pl.BlockSpec((g, 2 * n2, c), lambda p, k: (k, 0, h_col))],
        out_specs=pl.BlockSpec((1, g, 2 * n2, c), lambda p, k: (p, k, 0, 0)),
        compiler_params=_params(("parallel", "parallel")),
        name="hy_conv_dft23",
    )(sv, sv, tabs["f2a"].astype(BF16), tabs["f2b"].astype(BF16), tabs["f3"], h_spec)
    rv = r.reshape(npair, n1, 2 * n2 * c)
    norm = norm_g is not None
    in_specs = [pl.BlockSpec((1, n1, g * c), lambda p, j: (p, 0, j)),
                pl.BlockSpec((1, n1, g * c), lambda p, j: (p, 0, n2 // g + j)),
                pl.BlockSpec((g, n1, n1), lambda p, j: (j, 0, 0)),
                pl.BlockSpec((g, n1, n1), lambda p, j: (j, 0, 0)),
                pl.BlockSpec((1, 2, g * n1h, c), lambda p, j: (p, 0, j, u_col)),
                pl.BlockSpec((1, 2, g * n1h, c), lambda p, j: (p, 0, j, gate_col)),
                pl.BlockSpec((1, c), lambda p, j: (0, 0))]
    args = [rv, rv, tabs["g4a"], tabs["g4b"], u4, gate4, skip.reshape(1, c)]
    if norm:
        ch = jnp.arange(c) // HEAD
        avg = ((ch[:, None] == ch[None, :]).astype(F32) / HEAD).astype(BF16)
        in_specs += [pl.BlockSpec((1, c), lambda p, j: (0, 0)), pl.BlockSpec((c, c), lambda p, j: (0, 0))]
        args += [norm_g.reshape(1, c), avg]
    out = pl.pallas_call(
        functools.partial(_conv_s4_kernel, g=g, c=c, n1h=n1h, norm=norm),
        out_shape=jax.ShapeDtypeStruct((npair, 2, t, c), out_dtype),
        grid=(npair, n2 // g),
        in_specs=in_specs,
        out_specs=pl.BlockSpec((1, 2, g * n1h, c), lambda p, j: (p, 0, j, 0)),
        compiler_params=_params(("parallel", "parallel")),
        name="hy_conv_dft4",
    )(*args)
    return out.reshape(b, t, c)


def _hyena_branch(hn, w_hy, conv_w, conv_b, fw1, fb1, fw2, fb2, fw3, fb3, ffreq, fwout, skip, norm_g, tabs):
    b, t, d = hn.shape
    c = norm_g.shape[0]
    n1, n2, n1h = tabs["n1"], tabs["n2"], tabs["n1h"]
    gp = max(1, min(n2, 512 // n1h))
    p = _proj_perm_in(hn, w_hy, n1h, n2, gp, F32)
    u = _shortconv(p, conv_w, conv_b, n1h, n2)
    h = _filter_mlp(t, fw1, fb1, fw2, fb2, fw3, fb3, ffreq, fwout, c)
    order = skip.shape[0]
    h = h.reshape(t, order, 2, c)
    fwd, bwd = h[:, :, 0], h[:, :, 1]
    two = jnp.concatenate([fwd, jnp.zeros_like(fwd[:1]), bwd[1:][::-1]], axis=0).reshape(2 * t, order * c)
    hspec = _filter_spectrum(two, tabs)
    z = _long_conv_gate(u, 2, u, 0, hspec, 0, skip[0], tabs, c, F32)
    return _long_conv_gate(z, 0, u, 1, hspec, 1, skip[1], tabs, c, BF16, norm_g=norm_g)


def _rw_prep_kernel(*refs, has_vres, c):
    if has_vres:
        (p_ref, pp_ref, pn_ref, mu_ref, w0_ref, w2_ref, a0_ref, a2_ref, g2_ref, kk_ref, ka_ref, rk_ref,
         sum_ref, hn_ref, vf_ref, v0_ref, v1_ref, v2_ref,
         r_o, v_o, kk_o, lwf_o, lwb_o, kf_o, kb_o, af_o, ab_o, g_o, bon_o) = refs
    else:
        (p_ref, pp_ref, pn_ref, mu_ref, w0_ref, w2_ref, a0_ref, a2_ref, g2_ref, kk_ref, ka_ref, rk_ref,
         sum_ref,
         r_o, v_o, kk_o, lwf_o, lwb_o, kf_o, kb_o, af_o, ab_o, g_o, bon_o) = refs
    i = pl.program_id(1)
    last = pl.num_programs(1) - 1
    p = p_ref[0]
    tt = p.shape[0]
    rows = lax.broadcasted_iota(jnp.int32, (tt, 1), 0)
    prev_row = jnp.where(i == 0, 0.0, pp_ref[0, 7:8, :])
    next_row = jnp.where(i == last, 0.0, pn_ref[0, 0:1, :])
    prev = jnp.where(rows == 0, prev_row, pltpu.roll(p, 1, 0))
    nxt = jnp.where(rows == tt - 1, next_row, pltpu.roll(p, tt - 1, 0))
    pf = p + mu_ref[0:1, :] * (prev - p) + mu_ref[1:2, :] * (nxt - p)
    r = pf[:, :c]
    k = pf[:, c:2 * c]
    v = pf[:, 2 * c:3 * c]
    lw = 3 * c
    nd = w2_ref.shape[0]
    na = a2_ref.shape[0]
    wd = pf[:, lw:lw + nd]
    ad = pf[:, lw + nd:lw + nd + na]
    gd = pf[:, lw + nd + na:]
    if has_vres:
        lora = _dot(_dot(hn_ref[0], v1_ref[...]).astype(BF16), v2_ref[...])
        v = v + (vf_ref[0] - v) * jax.nn.sigmoid(v0_ref[...] + lora)
    g = _dot(jax.nn.sigmoid(gd).astype(BF16), g2_ref[...])
    kk = k * kk_ref[...]
    nrm = jnp.sqrt(_dot2(kk * kk, sum_ref[...]))
    kk = kk / jnp.maximum(nrm, 1e-12)
    wl = w0_ref[...] + _dot(jnp.tanh(wd).astype(BF16), w2_ref[...])
    w = -jax.nn.softplus(-wl) - 0.5
    logw = -jnp.exp(w)
    a = jax.nn.sigmoid(a0_ref[...] + _dot(ad.astype(BF16), a2_ref[...]))
    ka = ka_ref[...]
    rk = rk_ref[...]
    k_d = [k * (1.0 + (a[:, d * c:(d + 1) * c] - 1.0) * ka) for d in range(2)]
    bon = _dot2(r * (k_d[0] + k_d[1]) * rk, sum_ref[...]) * v
    r_o[0] = r
    v_o[0] = v
    kk_o[0] = kk
    lwf_o[0] = logw[:, :c]
    lwb_o[0] = logw[:, c:]
    kf_o[0] = k_d[0]
    kb_o[0] = k_d[1]
    af_o[0] = a[:, :c]
    ab_o[0] = a[:, c:]
    g_o[0] = g
    bon_o[0] = bon


def _blockdiag2(m):
    k, c = m.shape[1], m.shape[2]
    z = jnp.zeros((k, c), m.dtype)
    return jnp.concatenate([jnp.concatenate([m[0], z], axis=1), jnp.concatenate([z, m[1]], axis=1)], axis=0)


def _rw_prep(p, hn, v_first, shift, w0, w2, a0, a2, g2, k_k, k_a, r_k, vres, c, tt=256):
    b, t, pw = p.shape
    tt = min(tt, t)
    has_vres = vres is not None
    ch = jnp.arange(c) // HEAD
    summ = (ch[:, None] == ch[None, :]).astype(BF16)
    row = lambda a: a.reshape(1, -1)
    args = [p, p, p, shift, row(w0), _blockdiag2(w2).astype(BF16), row(a0), _blockdiag2(a2).astype(BF16),
            g2.astype(BF16), row(k_k), row(k_a), row(r_k), summ]
    full = lambda a: pl.BlockSpec(a.shape, lambda i, j: (0,) * a.ndim)
    nb8 = t // 8
    in_specs = [pl.BlockSpec((1, tt, pw), lambda i, j: (i, j, 0)),
                pl.BlockSpec((1, 8, pw), lambda i, j: (i, jnp.maximum(j * (tt // 8) - 1, 0), 0)),
                pl.BlockSpec((1, 8, pw), lambda i, j: (i, jnp.minimum((j + 1) * (tt // 8), nb8 - 1), 0))]
    in_specs += [full(a) for a in args[3:]]
    if has_vres:
        v0, v1, v2 = vres
        extra = [hn, v_first, row(v0), v1.astype(BF16), v2.astype(BF16)]
        in_specs += [pl.BlockSpec((1, tt, hn.shape[2]), lambda i, j: (i, j, 0)),
                     pl.BlockSpec((1, tt, c), lambda i, j: (i, j, 0))] + [full(a) for a in extra[2:]]
        args += extra
    outs = tuple(jax.ShapeDtypeStruct((b, t, c), F32) for _ in range(11))
    return pl.pallas_call(
        functools.partial(_rw_prep_kernel, has_vres=has_vres, c=c),
        out_shape=outs,
        grid=(b, t // tt),
        in_specs=in_specs,
        out_specs=tuple(pl.BlockSpec((1, tt, c), lambda i, j: (i, j, 0)) for _ in range(11)),
        compiler_params=_params(("parallel", "parallel")),
        name="rw_prep",
    )(*args)


def _split3(x):
    h1 = x.astype(BF16)
    r1 = x - h1.astype(F32)
    h2 = r1.astype(BF16)
    h3 = (r1 - h2.astype(F32)).astype(BF16)
    return h1, h2, h3


def _scan_chunks(s0, r, k, v, kk, a, lw, rev, bdm):
    L = CHUNK
    n = len(r)
    each = lambda f, *ls: [f(*xs) for xs in zip(*ls)]
    ti = lax.broadcasted_iota(jnp.int32, (L, L), 0)
    si = lax.broadcasted_iota(jnp.int32, (L, L), 1)
    tt = lax.broadcasted_iota(jnp.int32, (L, GW), 0)
    ss = lax.broadcasted_iota(jnp.int32, (L, GW), 1) % L
    eye = (ss == tt).astype(F32)
    tri = [((si >= ti) if q else (si <= ti)).astype(F32).astype(BF16) for q in rev]
    strict = [(ss > tt) if q else (ss < tt) for q in rev]
    incl = [(ss >= tt) if q else (ss <= tt) for q in rev]
    bdf = bdm.astype(F32)

    def bd(x):
        xb = x.astype(BF16)
        return jnp.concatenate([xb] * HEADS_PER_GROUP, axis=0) * bdm

    def rcmul(x, y):
        return _dot(x.astype(BF16), bd(y))

    parts = each(_split3, lw)
    cum = [_dot(t, p[0]) + _dot(t, p[1]) + _dot(t, p[2]) for t, p in zip(tri, parts)]
    tot = [c[0:1, :] if q else c[L - 1:L, :] for c, q in zip(cum, rev)]
    p_in = each(jnp.exp, cum)
    p_inv = each(lambda c: jnp.exp(-c), cum)
    p_ex = each(lambda c, w: jnp.exp(c - w), cum, lw)
    p_rem = each(lambda t, c: jnp.exp(t - c), tot, cum)
    beta = each(lambda x, y: x * y, kk, a)
    a_t = each(lambda x, p: -x * p, kk, p_ex)
    r_t = each(lambda x, p: x * p, r, p_in)
    k_t = each(lambda x, p: x * p, k, p_inv)
    b_t = each(lambda x, p: x * p, beta, p_inv)
    k_h = each(lambda x, p: x * p, k, p_rem)
    b_h = each(lambda x, p: x * p, beta, p_rem)

    lhs = each(lambda x, y: jnp.concatenate([x, y], axis=0).astype(BF16), a_t, r_t)
    akk = each(lambda x, y: _dot_nt(x, bd(y)), lhs, k_t)
    abb = each(lambda x, y: _dot_nt(x, bd(y)), lhs, b_t)
    a_ak = each(lambda m, x: jnp.where(m, x[:L], 0.0), strict, akk)
    a_rk = each(lambda m, x: jnp.where(m, x[L:], 0.0), incl, akk)
    a_ab = each(lambda m, x: jnp.where(m, x[:L], 0.0), strict, abb)
    a_rb = each(lambda m, x: jnp.where(m, x[L:], 0.0), incl, abb)

    pw = a_ab
    tinv = each(lambda x: eye + x, pw)
    for _ in range(5):
        pw = each(rcmul, pw, pw)
        tinv = each(lambda t, p: t + rcmul(t, p), tinv, pw)

    av = each(lambda x, y, z: rcmul(jnp.concatenate([x, y], axis=0), z), a_ak, a_rk, v)
    wu = each(lambda t, x, y: _dot(t.astype(BF16), jnp.concatenate([bd(x), bd(y[:L])], axis=1)), tinv, a_t, av)
    rbwu = each(lambda x, w: _dot(x.astype(BF16), jnp.concatenate([bd(w[:, :GW]), bd(w[:, GW:])], axis=1)),
                a_rb, wu)
    o = each(lambda x, w, s, y: _dot_nt((x + w[:, :GW]).astype(BF16), bd(s)) + y[L:] + w[:, GW:],
             r_t, rbwu, s0, av)

    m_bd = each(lambda w, x: _dot_tn(w[:, :GW].astype(BF16), x.astype(BF16)) * bdf, wu, b_h)
    n_full = each(lambda x, w, y, z: _dot_tn(jnp.concatenate([x, w[:, GW:]], axis=0).astype(BF16),
                                             jnp.concatenate([y, z], axis=0).astype(BF16)) * bdf,
                  v, wu, k_h, b_h)
    s1 = []
    for i in range(n):
        n_rc = n_full[i][0:HEAD]
        for h in range(1, HEADS_PER_GROUP):
            n_rc = n_rc + n_full[i][h * HEAD:(h + 1) * HEAD]
        s1.append(_dot(s0[i].astype(BF16), m_bd[i].astype(BF16)) + s0[i] * jnp.exp(tot[i]) + n_rc)
    return s1, o


def _scan_kernel(rf_ref, kf_ref, vf_ref, kkf_ref, af_ref, lwf_ref,
                 rb_ref, kb_ref, vb_ref, kkb_ref, ab_ref, lwb_ref, of_ref, ob_ref, s_ref, *, ng):
    @pl.when(pl.program_id(1) == 0)
    def _():
        s_ref[...] = jnp.zeros_like(s_ref)

    gi = lax.broadcasted_iota(jnp.int32, (GW, GW), 0) // HEAD
    gj = lax.broadcasted_iota(jnp.int32, (GW, GW), 1) // HEAD
    bdm = (gi == gj).astype(F32).astype(BF16)
    dirs = ((rf_ref, kf_ref, vf_ref, kkf_ref, af_ref, lwf_ref), (rb_ref, kb_ref, vb_ref, kkb_ref, ab_ref, lwb_ref))
    chains = [(d, h) for d in range(2) for h in range(ng)]
    ins = [[ref[0, :, h * GW:(h + 1) * GW] for d, h in chains for ref in (dirs[d][i],)] for i in range(6)]
    s1, o = _scan_chunks([s_ref[d, h] for d, h in chains], *ins, [d == 1 for d, _ in chains], bdm)
    for (d, h), s_new, o_new in zip(chains, s1, o):
        (of_ref, ob_ref)[d][0, :, h * GW:(h + 1) * GW] = o_new
        s_ref[d, h] = s_new


def _wkv_scan(r, v, kk, kf, af, lwf, kb, ab, lwb):
    b, t, c = r.shape
    nc = t // CHUNK
    ng = c // GW
    fspec = pl.BlockSpec((1, CHUNK, c), lambda i, j: (i, j, 0))
    bspec = pl.BlockSpec((1, CHUNK, c), lambda i, j: (i, nc - 1 - j, 0))
    return pl.pallas_call(
        functools.partial(_scan_kernel, ng=ng),
        out_shape=(jax.ShapeDtypeStruct((b, t, c), F32), jax.ShapeDtypeStruct((b, t, c), F32)),
        grid=(b, nc),
        in_specs=[fspec] * 6 + [bspec] * 6,
        out_specs=(fspec, bspec),
        scratch_shapes=[pltpu.VMEM((2, ng, HEAD, GW), F32)],
        compiler_params=_params(("parallel", "arbitrary")),
        name="rw_scan",
    )(r, kf, v, kk, af, lwf, r, kb, v, kk, ab, lwb)


def _rw_post_kernel(sf_ref, sb_ref, bon_ref, g_ref, lg_ref, lb_ref, avg_ref, o_ref):
    s = sf_ref[...] + sb_ref[...]
    mean = _dot2(s, avg_ref[...])
    d = s - mean
    var = _dot2(d * d, avg_ref[...])
    y = d * lax.rsqrt(var + GN_EPS) * lg_ref[...] + lb_ref[...] + bon_ref[...]
    o_ref[...] = (y * g_ref[...]).astype(o_ref.dtype)


def _rw_post(sf, sb, bon, g, lnx_g, lnx_b, tm=512):
    b, t, c = sf.shape
    m = b * t
    tm = min(tm, m)
    ch = jnp.arange(c) // HEAD
    avg = ((ch[:, None] == ch[None, :]).astype(F32) / HEAD).astype(BF16)
    big = pl.BlockSpec((tm, c), lambda i: (i, 0))
    small = pl.BlockSpec((1, c), lambda i: (0, 0))
    out = pl.pallas_call(
        _rw_post_kernel,
        out_shape=jax.ShapeDtypeStruct((m, c), BF16),
        grid=(m // tm,),
        in_specs=[big, big, big, big, small, small, pl.BlockSpec((c, c), lambda i: (0, 0))],
        out_specs=big,
        compiler_params=_params(("parallel",)),
        name="rw_post",
    )(sf.reshape(m, c), sb.reshape(m, c), bon.reshape(m, c), g.reshape(m, c),
      lnx_g.reshape(1, c), lnx_b.reshape(1, c), avg)
    return out.reshape(b, t, c)


def _rwkv_branch(hn, w_rw, v_first, shift, w0, w2, a0, a2, g2, k_k, k_a, r_k, lnx_g, lnx_b, vres, c):
    b, t, d = hn.shape
    p = _matmul(hn.reshape(b * t, d), w_rw, F32).reshape(b, t, -1)
    r, v, kk, lwf, lwb, kf, kb, af, ab, g, bon = _rw_prep(
        p, hn, v_first, shift, w0, w2, a0, a2, g2, k_k, k_a, r_k, vres, c)
    sf, sb = _wkv_scan(r, v, kk, kf, af, lwf, kb, ab, lwb)
    y = _rw_post(sf, sb, bon, g, lnx_g, lnx_b)
    return y, (v if vres is None else v_first)


def _forward(x, norm1_g, w_in, hy_conv_w, hy_conv_b, hy_f_w1, hy_f_b1, hy_f_w2, hy_f_b2, hy_f_w3, hy_f_b3,
             hy_f_freq, hy_f_wout, hy_skip, hy_norm_g, rw_shift, rw_w0, rw_w2, rw_a0, rw_a2, rw_g2, rw_k_k,
             rw_k_a, rw_r_k, rw_lnx_g, rw_lnx_b, rw_v0, rw_v1, rw_v2, w_out, norm2_g, mlp_w1, mlp_w2,
             final_g):
    b, t, d = x.shape
    depth = w_in.shape[0]
    c_hy = hy_norm_g.shape[1]
    c_rw = rw_lnx_g.shape[1]
    hy_proj = hy_conv_b.shape[1]
    assert b % 2 == 0 and t % CHUNK == 0 and CHUNK == HEAD
    tabs = _dft_tables(t)
    n1h, n2 = tabs["n1h"], tabs["n2"]
    gp = max(1, min(n2, 512 // n1h))
    v_first = None
    for l in range(depth):
        hn = _rmsnorm(x.reshape(b * t, d), norm1_g[l], BF16).reshape(b, t, d)
        w_l = w_in[l].astype(BF16)
        y_hy = _hyena_branch(hn, w_l[:, :hy_proj], hy_conv_w[l], hy_conv_b[l], hy_f_w1[l], hy_f_b1[l],
                             hy_f_w2[l], hy_f_b2[l], hy_f_w3[l], hy_f_b3[l], hy_f_freq[l], hy_f_wout[l],
                             hy_skip[l], hy_norm_g[l], tabs)
        vres = None if l == 0 else (rw_v0[l - 1], rw_v1[l - 1], rw_v2[l - 1])
        y_rw, v_first = _rwkv_branch(hn, w_l[:, hy_proj:], v_first, rw_shift[l], rw_w0[l], rw_w2[l], rw_a0[l],
                                     rw_a2[l], rw_g2[l], rw_k_k[l], rw_k_a[l], rw_r_k[l], rw_lnx_g[l],
                                     rw_lnx_b[l], vres, c_rw)
        wo = w_out[l].astype(BF16)
        x = _proj_perm_out(y_hy, wo[:c_hy], x, n1h, n2, gp)
        x = _matmul(y_rw.reshape(b * t, c_rw), wo[c_hy:], F32, residual=x.reshape(b * t, d),
                    tm=1024).reshape(b, t, d)
        x = _mlp(x.reshape(b * t, d), norm2_g[l], mlp_w1[l].astype(BF16), mlp_w2[l].astype(BF16)).reshape(b, t, d)
    return _rmsnorm(x.reshape(b * t, d), final_g, F32).reshape(b, t, d)


def kernel(x, norm1_g, w_in, hy_conv_w, hy_conv_b, hy_f_w1, hy_f_b1, hy_f_w2, hy_f_b2, hy_f_w3, hy_f_b3,
           hy_f_freq, hy_f_wout, hy_skip, hy_norm_g, rw_shift, rw_w0, rw_w2, rw_a0, rw_a2, rw_g2, rw_k_k,
           rw_k_a, rw_r_k, rw_lnx_g, rw_lnx_b, rw_v0, rw_v1, rw_v2, w_out, norm2_g, mlp_w1, mlp_w2, final_g):
    return _forward(x, norm1_g, w_in, hy_conv_w, hy_conv_b, hy_f_w1, hy_f_b1, hy_f_w2, hy_f_b2, hy_f_w3,
                    hy_f_b3, hy_f_freq, hy_f_wout, hy_skip, hy_norm_g, rw_shift, rw_w0, rw_w2, rw_a0, rw_a2,
                    rw_g2, rw_k_k, rw_k_a, rw_r_k, rw_lnx_g, rw_lnx_b, rw_v0, rw_v1, rw_v2, w_out, norm2_g,
                    mlp_w1, mlp_w2, final_g)
```

```python
import functools
import math

import jax
import jax.numpy as jnp
from jax import lax
from jax.experimental import pallas as pl
from jax.experimental.pallas import tpu as pltpu

F32 = jnp.float32
BF16 = jnp.bfloat16
HP = lax.Precision.HIGHEST

HEAD = 64
HEADS_PER_GROUP = 4
GW = HEAD * HEADS_PER_GROUP
CHUNK = 64
RMS_EPS = 1e-5
GN_EPS = HEAD * 1e-5
HY_TARGET = 1e-2
HY_FAST_PCT = 0.3
HY_SLOW_PCT = 1.5
VMEM_LIMIT = 56 * 1024 * 1024


def _params(sem, vmem=VMEM_LIMIT):
    return pltpu.CompilerParams(dimension_semantics=sem, vmem_limit_bytes=vmem)


def _dot(a, b, precision=None):
    return jnp.dot(a, b, preferred_element_type=F32, precision=precision)


def _dot_nt(a, b, precision=None):
    return lax.dot_general(a, b, (((1,), (1,)), ((), ())), preferred_element_type=F32,
                           precision=precision)


def _dot_tn(a, b, precision=None):
    return lax.dot_general(a, b, (((0,), (0,)), ((), ())), preferred_element_type=F32,
                           precision=precision)


def _rmsnorm_kernel(x_ref, g_ref, o_ref):
    x = x_ref[...]
    y = x * lax.rsqrt(jnp.mean(x * x, axis=-1, keepdims=True) + RMS_EPS)
    o_ref[...] = (y * g_ref[...]).astype(o_ref.dtype)


def _rmsnorm(x2d, g, out_dtype, tm=1024):
    m, d = x2d.shape
    tm = min(tm, m)
    return pl.pallas_call(
        _rmsnorm_kernel,
        out_shape=jax.ShapeDtypeStruct((m, d), out_dtype),
        grid=(m // tm,),
        in_specs=[pl.BlockSpec((tm, d), lambda i: (i, 0)), pl.BlockSpec((1, d), lambda i: (0, 0))],
        out_specs=pl.BlockSpec((tm, d), lambda i: (i, 0)),
        compiler_params=_params(("parallel",)),
        name="rmsnorm",
    )(x2d, g.reshape(1, d))


def _const_spec(shape):
    return pl.BlockSpec(shape, lambda *_: (0,) * len(shape), pipeline_mode=pl.Buffered(1))


def _dot2(x, m):
    hi = x.astype(BF16)
    lo = (x - hi.astype(F32)).astype(BF16)
    return _dot(hi, m) + _dot(lo, m)


def _mlp_kernel(x_ref, g_ref, w1_ref, w2_ref, o_ref, *, nchunk):
    x = x_ref[...]
    hn = (x * lax.rsqrt(jnp.mean(x * x, axis=-1, keepdims=True) + RMS_EPS) * g_ref[...]).astype(BF16)
    ff = w1_ref.shape[1]
    cw = ff // nchunk
    acc = x
    for j in range(nchunk):
        h = _dot(hn, w1_ref[:, j * cw:(j + 1) * cw])
        h = jnp.square(jnp.maximum(h, 0.0)).astype(BF16)
        acc = acc + _dot(h, w2_ref[j * cw:(j + 1) * cw, :])
    o_ref[...] = acc


def _mlp(x2d, g, w1, w2, tm=512, nchunk=4):
    m, d = x2d.shape
    ff = w1.shape[1]
    tm = min(tm, m)
    return pl.pallas_call(
        functools.partial(_mlp_kernel, nchunk=nchunk),
        out_shape=jax.ShapeDtypeStruct((m, d), F32),
        grid=(m // tm,),
        in_specs=[pl.BlockSpec((tm, d), lambda i: (i, 0)), _const_spec((1, d)),
                  _const_spec((d, ff)), _const_spec((ff, d))],
        out_specs=pl.BlockSpec((tm, d), lambda i: (i, 0)),
        compiler_params=_params(("parallel",)),
        name="mlp",
    )(x2d, g.reshape(1, d), w1, w2)


SUB = 8


def _rms(x, g):
    return (x * lax.rsqrt(jnp.mean(x * x, axis=-1, keepdims=True) + RMS_EPS) * g).astype(BF16)


def _norm_matmul_kernel(x_ref, g_ref, w_ref, o_ref):
    o_ref[...] = _dot(_rms(x_ref[...], g_ref[...]), w_ref[...]).astype(o_ref.dtype)


def _norm_matmul(x2d, g, w, out_dtype, tm=512):
    m, d = x2d.shape
    n = w.shape[1]
    tm = min(tm, m)
    return pl.pallas_call(
        _norm_matmul_kernel,
        out_shape=jax.ShapeDtypeStruct((m, n), out_dtype),
        grid=(m // tm,),
        in_specs=[pl.BlockSpec((tm, d), lambda i: (i, 0)), _const_spec((1, d)), _const_spec((d, n))],
        out_specs=pl.BlockSpec((tm, n), lambda i: (i, 0)),
        compiler_params=_params(("parallel",)),
        name="norm_matmul",
    )(x2d, g.reshape(1, d), w)


def _proj_perm_in_kernel(x_ref, g_ref, w_ref, o_ref):
    lhs = jnp.concatenate([_rms(x_ref[0, :, j, :], g_ref[...]) for j in range(SUB)], axis=0)
    o_ref[0] = _dot(lhs, w_ref[...]).astype(o_ref.dtype)


def _proj_perm_in(x, g, w, n1h, n2, out_dtype):
    b, t, d = x.shape
    p = w.shape[1]
    xv = x.reshape(b, n1h, n2, d)
    return pl.pallas_call(
        _proj_perm_in_kernel,
        out_shape=jax.ShapeDtypeStruct((b, t, p), out_dtype),
        grid=(b, n2 // SUB),
        in_specs=[pl.BlockSpec((1, n1h, SUB, d), lambda i, j: (i, 0, j, 0)), _const_spec((1, d)),
                  _const_spec((d, p))],
        out_specs=pl.BlockSpec((1, SUB * n1h, p), lambda i, j: (i, j, 0)),
        compiler_params=_params(("parallel", "parallel")),
        name="proj_perm_in",
    )(xv, g.reshape(1, d), w)


def _out_proj_kernel(yh_ref, yr_ref, wh_ref, wr_ref, r_ref, o_ref):
    o_ref[...] = (_dot(yh_ref[...].astype(BF16), wh_ref[...]) + _dot(yr_ref[...].astype(BF16), wr_ref[...])
                  + r_ref[...])


def _out_proj(yh, yr, w, res, tm=1024):
    m, ch = yh.shape
    cr = yr.shape[1]
    d = w.shape[1]
    tm = min(tm, m)
    row = lambda c: pl.BlockSpec((tm, c), lambda i: (i, 0))
    return pl.pallas_call(
        _out_proj_kernel,
        out_shape=jax.ShapeDtypeStruct((m, d), F32),
        grid=(m // tm,),
        in_specs=[row(ch), row(cr), _const_spec((ch, d)), _const_spec((cr, d)), row(d)],
        out_specs=row(d),
        compiler_params=_params(("parallel",)),
        name="out_proj",
    )(yh, yr, w[:ch], w[ch:], res)


def _shortconv_kernel(p_ref, w_ref, b_ref, o_ref, *, n1h, n2):
    w0 = w_ref[0:1, :]
    w1 = w_ref[1:2, :]
    w2 = w_ref[2:3, :]
    bias = b_ref[...]
    rows = lax.broadcasted_iota(jnp.int32, (n1h, 1), 0)

    def blk(i):
        return p_ref[0, pl.ds(pl.multiple_of(i * n1h, n1h), n1h), :].astype(F32)

    def body(i, carry):
        o_ref[0, pl.ds(pl.multiple_of(i * n1h, n1h), n1h), :] = (
            w0 * blk(i - 1) + w1 * blk(i) + w2 * blk(i + 1) + bias).astype(o_ref.dtype)
        return carry

    lax.fori_loop(1, n2 - 1, body, 0)
    last = p_ref[0, (n2 - 1) * n1h:, :].astype(F32)
    first = p_ref[0, :n1h, :].astype(F32)
    prev0 = jnp.where(rows == 0, 0.0, pltpu.roll(last, 1, 0))
    o_ref[0, :n1h, :] = (w0 * prev0 + w1 * first + w2 * p_ref[0, n1h:2 * n1h, :].astype(F32)
                         + bias).astype(o_ref.dtype)
    nxt = jnp.where(rows == n1h - 1, 0.0, pltpu.roll(first, n1h - 1, 0))
    o_ref[0, (n2 - 1) * n1h:, :] = (w0 * p_ref[0, (n2 - 2) * n1h:(n2 - 1) * n1h, :].astype(F32) + w1 * last
                                    + w2 * nxt + bias).astype(o_ref.dtype)


def _shortconv(p, w, bias, n1h, n2, cb=128):
    b, t, c = p.shape
    return pl.pallas_call(
        functools.partial(_shortconv_kernel, n1h=n1h, n2=n2),
        out_shape=jax.ShapeDtypeStruct((b, t, c), BF16),
        grid=(b, c // cb),
        in_specs=[pl.BlockSpec((1, t, cb), lambda i, j: (i, 0, j)),
                  pl.BlockSpec((3, cb), lambda i, j: (0, j)),
                  pl.BlockSpec((1, cb), lambda i, j: (0, j))],
        out_specs=pl.BlockSpec((1, t, cb), lambda i, j: (i, 0, j)),
        compiler_params=_params(("parallel", "parallel")),
        name="hy_shortconv",
    )(p, w, bias.reshape(1, c))


def _dft_tables(t):
    n = 2 * t
    n1 = 1 << ((n.bit_length() - 1) // 2)
    n2 = n // n1
    n1h = n1 // 2
    two_pi = 2.0 * math.pi
    k1 = jnp.arange(n1, dtype=jnp.int32)
    n2i = jnp.arange(n2, dtype=jnp.int32)

    def cs(prod, mod):
        ang = (two_pi / mod) * (prod % mod).astype(F32)
        return jnp.cos(ang), jnp.sin(ang)

    tpos = n2i[:, None] + n2 * jnp.arange(n1h, dtype=jnp.int32)[None, :]
    c, s = cs(k1[None, :, None] * tpos[:, None, :], n)
    g1 = jnp.concatenate([jnp.concatenate([c, s], axis=2), jnp.concatenate([-s, c], axis=2)], axis=1)
    ct, st = jnp.swapaxes(c, 1, 2) / n, jnp.swapaxes(s, 1, 2) / n
    g4a = jnp.concatenate([ct, st], axis=1)
    g4b = jnp.concatenate([-st, ct], axis=1)
    tposf = n2i[:, None] + n2 * jnp.arange(n1, dtype=jnp.int32)[None, :]
    cf, sf = cs(k1[None, :, None] * tposf[:, None, :], n)
    g1f = jnp.concatenate([cf, -sf], axis=1)
    c2, s2 = cs(n2i[:, None] * n2i[None, :], n2)
    f2a = jnp.concatenate([c2, -s2], axis=0)
    f2b = jnp.concatenate([s2, c2], axis=0)
    f3 = jnp.concatenate([jnp.concatenate([c2, -s2], axis=1), jnp.concatenate([s2, c2], axis=1)], axis=0)
    return dict(n1=n1, n2=n2, n1h=n1h, g1=g1.astype(BF16), g4a=g4a.astype(BF16), g4b=g4b.astype(BF16),
                g1f=g1f, f2a=f2a, f2b=f2b, f3=f3.astype(BF16))


def _filter_mlp_kernel(z_ref, w1_ref, b1_ref, w2_ref, b2_ref, w3_ref, b3_ref, fr_ref, wo_ref, dl_ref,
                       o_ref):
    z = z_ref[...]
    h = jnp.sin(fr_ref[0:1, :] * (_dot(z, w1_ref[...], HP) + b1_ref[...]))
    h = jnp.sin(fr_ref[1:2, :] * (_dot(h, w2_ref[...], HP) + b2_ref[...]))
    h = jnp.sin(fr_ref[2:3, :] * (_dot(h, w3_ref[...], HP) + b3_ref[...]))
    window = jnp.exp(-z[:, 0:1] * dl_ref[...])
    o_ref[...] = _dot(h, wo_ref[...], HP) * window


def _filter_mlp(t, w1, b1, w2, b2, w3, b3, freq, w_out, c_hy, tr=512):
    emb, width = w1.shape
    bands = (emb - 1) // 2
    pos = jnp.arange(t, dtype=F32)
    tt = pos / max(t - 1, 1)
    fr = jnp.linspace(1e-4, bands - 1, bands, dtype=F32)
    ang = (2.0 * math.pi / t) * pos[:, None] * fr[None, :]
    z = jnp.concatenate([tt[:, None], jnp.cos(ang), -jnp.sin(ang)], axis=-1)
    z = jnp.pad(z, ((0, 0), (0, 128 - emb)))
    w1 = jnp.pad(w1, ((0, 128 - emb), (0, 0)))
    emb = 128
    max_decay = math.log(HY_TARGET) / HY_FAST_PCT
    min_decay = math.log(HY_TARGET) / HY_SLOW_PCT
    delta = jnp.abs(jnp.linspace(min_decay, max_decay, c_hy, dtype=F32))
    nout = w_out.shape[1]
    delta_full = jnp.tile(delta, nout // c_hy).reshape(1, nout)
    tr = min(tr, t)
    full = lambda a: pl.BlockSpec(a.shape, lambda i: (0,) * a.ndim)
    args = [z, w1, b1.reshape(1, -1), w2, b2.reshape(1, -1), w3, b3.reshape(1, -1), freq, w_out, delta_full]
    return pl.pallas_call(
        _filter_mlp_kernel,
        out_shape=jax.ShapeDtypeStruct((t, nout), F32),
        grid=(t // tr,),
        in_specs=[pl.BlockSpec((tr, emb), lambda i: (i, 0))] + [full(a) for a in args[1:]],
        out_specs=pl.BlockSpec((tr, nout), lambda i: (i, 0)),
        compiler_params=_params(("parallel",)),
        name="hy_filter_mlp",
    )(*args)


def _filter_s1_kernel(x_ref, g_ref, o_ref, ss_ref, *, g, oc):
    @pl.when(pl.program_id(0) == 0)
    def _():
        ss_ref[...] = jnp.zeros_like(ss_ref)

    ss = ss_ref[...]
    for j in range(g):
        x = x_ref[:, j, :]
        o_ref[j] = _dot(g_ref[j], x, HP)
        ss = ss + jnp.sum(x * x, axis=0, keepdims=True)
    ss_ref[...] = ss


def _filter_s2_kernel(sr_ref, si_ref, fa_ref, fb_ref, ss_ref, o_ref, *, g, oc):
    scale = lax.rsqrt(ss_ref[...] + 1e-6)
    for j in range(g):
        x = _dot(fa_ref[...], sr_ref[:, j, :], HP) + _dot(fb_ref[...], si_ref[:, j, :], HP)
        o_ref[j] = x * scale


def _filter_spectrum(two, tabs):
    n, oc = two.shape
    n1, n2 = tabs["n1"], tabs["n2"]
    g = SUB
    s1, ss = pl.pallas_call(
        functools.partial(_filter_s1_kernel, g=g, oc=oc),
        out_shape=(jax.ShapeDtypeStruct((n2, 2 * n1, oc), F32), jax.ShapeDtypeStruct((1, oc), F32)),
        grid=(n2 // g,),
        in_specs=[pl.BlockSpec((n1, g, oc), lambda j: (0, j, 0)),
                  pl.BlockSpec((g, 2 * n1, n1), lambda j: (j, 0, 0))],
        out_specs=(pl.BlockSpec((g, 2 * n1, oc), lambda j: (j, 0, 0)),
                   pl.BlockSpec((1, oc), lambda j: (0, 0))),
        compiler_params=_params(("arbitrary",)),
        name="hy_filter_dft1",
    )(two.reshape(n1, n2, oc), tabs["g1f"])
    return pl.pallas_call(
        functools.partial(_filter_s2_kernel, g=g, oc=oc),
        out_shape=jax.ShapeDtypeStruct((n1, 2 * n2, oc), F32),
        grid=(n1 // g,),
        in_specs=[pl.BlockSpec((n2, g, oc), lambda k: (0, k, 0)),
                  pl.BlockSpec((n2, g, oc), lambda k: (0, n1 // g + k, 0)),
                  pl.BlockSpec((2 * n2, n2), lambda k: (0, 0)),
                  pl.BlockSpec((2 * n2, n2), lambda k: (0, 0)),
                  pl.BlockSpec((1, oc), lambda k: (0, 0))],
        out_specs=pl.BlockSpec((g, 2 * n2, oc), lambda k: (k, 0, 0)),
        compiler_params=_params(("parallel",)),
        name="hy_filter_dft2",
    )(s1, s1, tabs["f2a"], tabs["f2b"], ss)


def _conv_s1_kernel(u_ref, g_ref, o_ref, *, g, n1h):
    for j in range(g):
        rows = slice(j * n1h, (j + 1) * n1h)
        rhs = jnp.concatenate([u_ref[0, 0, rows, :], u_ref[0, 1, rows, :]], axis=0)
        o_ref[0, j] = _dot(g_ref[j], rhs).astype(o_ref.dtype)


def _conv_s23_kernel(sr_ref, si_ref, fa_ref, fb_ref, f3_ref, h_ref, o_ref, *, g, c, n2):
    for j in range(g):
        cols = slice(j * c, (j + 1) * c)
        x = _dot(fa_ref[...], sr_ref[0, :, cols]) + _dot(fb_ref[...], si_ref[0, :, cols])
        xr, xi = x[:n2], x[n2:]
        hr, hi = h_ref[j, :n2, :], h_ref[j, n2:, :]
        y = jnp.concatenate([xr * hr - xi * hi, xr * hi + xi * hr], axis=0).astype(BF16)
        o_ref[0, j] = _dot(f3_ref[...], y).astype(o_ref.dtype)


def _conv_s4_kernel(rr_ref, ri_ref, ga_ref, gb_ref, u_ref, gate_ref, skip_ref, *rest, g, c, n1h, norm):
    if norm:
        ng_ref, avg_ref, o_ref = rest
    else:
        (o_ref,) = rest
    skip = skip_ref[...]
    for j in range(g):
        cols = slice(j * c, (j + 1) * c)
        rows = slice(j * n1h, (j + 1) * n1h)
        y = _dot(ga_ref[j], rr_ref[0, :, cols]) + _dot(gb_ref[j], ri_ref[0, :, cols])
        for q in range(2):
            z = gate_ref[0, q, rows, :].astype(F32) * (y[q * n1h:(q + 1) * n1h]
                                                       + u_ref[0, q, rows, :].astype(F32) * skip)
            if norm:
                ms = _dot2(z * z, avg_ref[...])
                o_ref[0, q, :, j, :] = z * lax.rsqrt(ms + RMS_EPS) * ng_ref[...]
            else:
                o_ref[0, q, rows, :] = z.astype(o_ref.dtype)


def _long_conv_gate(u_arr, u_col, gate_arr, gate_col, h_spec, h_col, skip, tabs, c, norm_g=None):
    b, t, _ = u_arr.shape
    n1, n2, n1h = tabs["n1"], tabs["n2"], tabs["n1h"]
    g = SUB
    npair = b // 2
    u4 = u_arr.reshape(npair, 2, t, u_arr.shape[2])
    gate4 = gate_arr.reshape(npair, 2, t, gate_arr.shape[2])
    s1 = pl.pallas_call(
        functools.partial(_conv_s1_kernel, g=g, n1h=n1h),
        out_shape=jax.ShapeDtypeStruct((npair, n2, 2 * n1, c), BF16),
        grid=(npair, n2 // g),
        in_specs=[pl.BlockSpec((1, 2, g * n1h, c), lambda p, j: (p, 0, j, u_col)),
                  pl.BlockSpec((g, 2 * n1, n1), lambda p, j: (j, 0, 0))],
        out_specs=pl.BlockSpec((1, g, 2 * n1, c), lambda p, j: (p, j, 0, 0)),
        compiler_params=_params(("parallel", "parallel")),
        name="hy_conv_dft1",
    )(u4, tabs["g1"])
    sv = s1.reshape(npair, n2, 2 * n1 * c)
    r = pl.pallas_call(
        functools.partial(_conv_s23_kernel, g=g, c=c, n2=n2),
        out_shape=jax.ShapeDtypeStruct((npair, n1, 2 * n2, c), BF16),
        grid=(npair, n1 // g),
        in_specs=[pl.BlockSpec((1, n2, g * c), lambda p, k: (p, 0, k)),
                  pl.BlockSpec((1, n2, g * c), lambda p, k: (p, 0, n1 // g + k)),
                  pl.BlockSpec((2 * n2, n2), lambda p, k: (0, 0)),
                  pl.BlockSpec((2 * n2, n2), lambda p, k: (0, 0)),
                  pl.BlockSpec((2 * n2, 2 * n2), lambda p, k: (0, 0)),
                  pl.BlockSpec((g, 2 * n2, c), lambda p, k: (k, 0, h_col))],
        out_specs=pl.BlockSpec((1, g, 2 * n2, c), lambda p, k: (p, k, 0, 0)),
        compiler_params=_params(("parallel", "parallel")),
        name="hy_conv_dft23",
    )(sv, sv, tabs["f2a"].astype(BF16), tabs["f2b"].astype(BF16), tabs["f3"], h_spec)
    rv = r.reshape(npair, n1, 2 * n2 * c)
    norm = norm_g is not None
    in_specs = [pl.BlockSpec((1, n1, g * c), lambda p, j: (p, 0, j)),
                pl.BlockSpec((1, n1, g * c), lambda p, j: (p, 0, n2 // g + j)),
                pl.BlockSpec((g, n1, n1), lambda p, j: (j, 0, 0)),
                pl.BlockSpec((g, n1, n1), lambda p, j: (j, 0, 0)),
                pl.BlockSpec((1, 2, g * n1h, c), lambda p, j: (p, 0, j, u_col)),
                pl.BlockSpec((1, 2, g * n1h, c), lambda p, j: (p, 0, j, gate_col)),
                pl.BlockSpec((1, c), lambda p, j: (0, 0))]
    args = [rv, rv, tabs["g4a"], tabs["g4b"], u4, gate4, skip.reshape(1, c)]
    if norm:
        ch = jnp.arange(c) // HEAD
        avg = ((ch[:, None] == ch[None, :]).astype(F32) / HEAD).astype(BF16)
        in_specs += [pl.BlockSpec((1, c), lambda p, j: (0, 0)), pl.BlockSpec((c, c), lambda p, j: (0, 0))]
        args += [norm_g.reshape(1, c), avg]
        out_shape = jax.ShapeDtypeStruct((npair, 2, n1h, n2, c), F32)
        out_spec = pl.BlockSpec((1, 2, n1h, g, c), lambda p, j: (p, 0, 0, j, 0))
    else:
        out_shape = jax.ShapeDtypeStruct((npair, 2, t, c), BF16)
        out_spec = pl.BlockSpec((1, 2, g * n1h, c), lambda p, j: (p, 0, j, 0))
    out = pl.pallas_call(
        functools.partial(_conv_s4_kernel, g=g, c=c, n1h=n1h, norm=norm),
        out_shape=out_shape,
        grid=(npair, n2 // g),
        in_specs=in_specs,
        out_specs=out_spec,
        compiler_params=_params(("parallel", "parallel")),
        name="hy_conv_dft4",
    )(*args)
    return out.reshape(b, t, c)


def _hyena_branch(x, norm_g1, w_hy, conv_w, conv_b, fw1, fb1, fw2, fb2, fw3, fb3, ffreq, fwout, skip, norm_g,
                  tabs):
    b, t, d = x.shape
    c = norm_g.shape[0]
    n1, n2, n1h = tabs["n1"], tabs["n2"], tabs["n1h"]
    p = _proj_perm_in(x, norm_g1, w_hy, n1h, n2, BF16)
    u = _shortconv(p, conv_w, conv_b, n1h, n2)
    h = _filter_mlp(t, fw1, fb1, fw2, fb2, fw3, fb3, ffreq, fwout, c)
    order = skip.shape[0]
    h = h.reshape(t, order, 2, c)
    fwd, bwd = h[:, :, 0], h[:, :, 1]
    two = jnp.concatenate([fwd, jnp.zeros_like(fwd[:1]), bwd[1:][::-1]], axis=0).reshape(2 * t, order * c)
    hspec = _filter_spectrum(two, tabs)
    z = _long_conv_gate(u, 2, u, 0, hspec, 0, skip[0], tabs, c)
    return _long_conv_gate(z, 0, u, 1, hspec, 1, skip[1], tabs, c, norm_g=norm_g)


def _rw_prep_kernel(*refs, has_vres, c):
    if has_vres:
        (p_ref, pp_ref, pn_ref, mu_ref, w0_ref, w2_ref, a0_ref, a2_ref, g2_ref, kk_ref, ka_ref, rk_ref,
         sum_ref, vf_ref, v0_ref, v2_ref,
         r_o, v_o, kk_o, lwf_o, lwb_o, kf_o, kb_o, af_o, ab_o, g_o, bon_o) = refs
    else:
        (p_ref, pp_ref, pn_ref, mu_ref, w0_ref, w2_ref, a0_ref, a2_ref, g2_ref, kk_ref, ka_ref, rk_ref,
         sum_ref,
         r_o, v_o, kk_o, lwf_o, lwb_o, kf_o, kb_o, af_o, ab_o, g_o, bon_o) = refs
    i = pl.program_id(1)
    last = pl.num_programs(1) - 1
    nsh = mu_ref.shape[1]
    p = p_ref[0, :, :nsh]
    tt = p.shape[0]
    rows = lax.broadcasted_iota(jnp.int32, (tt, 1), 0)
    prev_row = jnp.where(i == 0, 0.0, pp_ref[0, 7:8, :nsh])
    next_row = jnp.where(i == last, 0.0, pn_ref[0, 0:1, :nsh])
    prev = jnp.where(rows == 0, prev_row, pltpu.roll(p, 1, 0))
    nxt = jnp.where(rows == tt - 1, next_row, pltpu.roll(p, tt - 1, 0))
    pf = p + mu_ref[0:1, :] * (prev - p) + mu_ref[1:2, :] * (nxt - p)
    r = pf[:, :c]
    k = pf[:, c:2 * c]
    v = pf[:, 2 * c:3 * c]
    lw = 3 * c
    nd = w2_ref.shape[0]
    na = a2_ref.shape[0]
    wd = pf[:, lw:lw + nd]
    ad = pf[:, lw + nd:lw + nd + na]
    gd = pf[:, lw + nd + na:]
    if has_vres:
        lora = _dot(p_ref[0, :, nsh:].astype(BF16), v2_ref[...])
        v = v + (vf_ref[0] - v) * jax.nn.sigmoid(v0_ref[...] + lora)
    g = _dot(jax.nn.sigmoid(gd).astype(BF16), g2_ref[...])
    kk = k * kk_ref[...]
    nrm = jnp.sqrt(_dot2(kk * kk, sum_ref[...]))
    kk = kk / jnp.maximum(nrm, 1e-12)
    wl = w0_ref[...] + _dot(jnp.tanh(wd).astype(BF16), w2_ref[...])
    w = -jax.nn.softplus(-wl) - 0.5
    logw = -jnp.exp(w)
    a = jax.nn.sigmoid(a0_ref[...] + _dot(ad.astype(BF16), a2_ref[...]))
    ka = ka_ref[...]
    rk = rk_ref[...]
    k_d = [k * (1.0 + (a[:, d * c:(d + 1) * c] - 1.0) * ka) for d in range(2)]
    bon = _dot2(r * (k_d[0] + k_d[1]) * rk, sum_ref[...]) * v
    r_o[0] = r
    v_o[0] = v
    kk_o[0] = kk
    lwf_o[0] = logw[:, :c]
    lwb_o[0] = logw[:, c:]
    kf_o[0] = k_d[0]
    kb_o[0] = k_d[1]
    af_o[0] = a[:, :c]
    ab_o[0] = a[:, c:]
    g_o[0] = g
    bon_o[0] = bon


def _blockdiag2(m):
    k, c = m.shape[1], m.shape[2]
    z = jnp.zeros((k, c), m.dtype)
    return jnp.concatenate([jnp.concatenate([m[0], z], axis=1), jnp.concatenate([z, m[1]], axis=1)], axis=0)


def _rw_prep(p, v_first, shift, w0, w2, a0, a2, g2, k_k, k_a, r_k, vres, c, tt=256):
    b, t, pw = p.shape
    tt = min(tt, t)
    has_vres = vres is not None
    ch = jnp.arange(c) // HEAD
    summ = (ch[:, None] == ch[None, :]).astype(BF16)
    row = lambda a: a.reshape(1, -1)
    args = [p, p, p, shift, row(w0), _blockdiag2(w2).astype(BF16), row(a0), _blockdiag2(a2).astype(BF16),
            g2.astype(BF16), row(k_k), row(k_a), row(r_k), summ]
    full = lambda a: pl.BlockSpec(a.shape, lambda i, j: (0,) * a.ndim)
    nb8 = t // 8
    in_specs = [pl.BlockSpec((1, tt, pw), lambda i, j: (i, j, 0)),
                pl.BlockSpec((1, 8, pw), lambda i, j: (i, jnp.maximum(j * (tt // 8) - 1, 0), 0)),
                pl.BlockSpec((1, 8, pw), lambda i, j: (i, jnp.minimum((j + 1) * (tt // 8), nb8 - 1), 0))]
    in_specs += [full(a) for a in args[3:]]
    if has_vres:
        v0, _, v2 = vres
        v2p = jnp.pad(v2, ((0, pw - shift.shape[1] - v2.shape[0]), (0, 0))).astype(BF16)
        extra = [v_first, row(v0), v2p]
        in_specs += [pl.BlockSpec((1, tt, c), lambda i, j: (i, j, 0))] + [full(a) for a in extra[1:]]
        args += extra
    outs = tuple(jax.ShapeDtypeStruct((b, t, c), F32) for _ in range(11))
    return pl.pallas_call(
        functools.partial(_rw_prep_kernel, has_vres=has_vres, c=c),
        out_shape=outs,
        grid=(b, t // tt),
        in_specs=in_specs,
        out_specs=tuple(pl.BlockSpec((1, tt, c), lambda i, j: (i, j, 0)) for _ in range(11)),
        compiler_params=_params(("parallel", "parallel")),
        name="rw_prep",
    )(*args)


def _split3(x):
    h1 = x.astype(BF16)
    r1 = x - h1.astype(F32)
    h2 = r1.astype(BF16)
    h3 = (r1 - h2.astype(F32)).astype(BF16)
    return h1, h2, h3


def _scan_chunks(s0, r, k, v, kk, a, lw, rev, bdm):
    L = CHUNK
    n = len(r)
    each = lambda f, *ls: [f(*xs) for xs in zip(*ls)]
    ti = lax.broadcasted_iota(jnp.int32, (L, L), 0)
    si = lax.broadcasted_iota(jnp.int32, (L, L), 1)
    tt = lax.broadcasted_iota(jnp.int32, (L, GW), 0)
    ss = lax.broadcasted_iota(jnp.int32, (L, GW), 1) % L
    eye = (ss == tt).astype(F32)
    tri = [((si >= ti) if q else (si <= ti)).astype(F32).astype(BF16) for q in rev]
    strict = [(ss > tt) if q else (ss < tt) for q in rev]
    incl = [(ss >= tt) if q else (ss <= tt) for q in rev]
    bdf = bdm.astype(F32)

    def bd(x):
        xb = x.astype(BF16)
        return jnp.concatenate([xb] * HEADS_PER_GROUP, axis=0) * bdm

    def rcmul(x, y):
        return _dot(x.astype(BF16), bd(y))

    parts = each(_split3, lw)
    cum = [_dot(t, p[0]) + _dot(t, p[1]) + _dot(t, p[2]) for t, p in zip(tri, parts)]
    tot = [c[0:1, :] if q else c[L - 1:L, :] for c, q in zip(cum, rev)]
    p_in = each(jnp.exp, cum)
    p_inv = each(lambda c: jnp.exp(-c), cum)
    p_ex = each(lambda c, w: jnp.exp(c - w), cum, lw)
    p_rem = each(lambda t, c: jnp.exp(t - c), tot, cum)
    beta = each(lambda x, y: x * y, kk, a)
    a_t = each(lambda x, p: -x * p, kk, p_ex)
    r_t = each(lambda x, p: x * p, r, p_in)
    k_t = each(lambda x, p: x * p, k, p_inv)
    b_t = each(lambda x, p: x * p, beta, p_inv)
    k_h = each(lambda x, p: x * p, k, p_rem)
    b_h = each(lambda x, p: x * p, beta, p_rem)

    lhs = each(lambda x, y: jnp.concatenate([x, y], axis=0).astype(BF16), a_t, r_t)
    akk = each(lambda x, y: _dot_nt(x, bd(y)), lhs, k_t)
    abb = each(lambda x, y: _dot_nt(x, bd(y)), lhs, b_t)
    a_ak = each(lambda m, x: jnp.where(m, x[:L], 0.0), strict, akk)
    a_rk = each(lambda m, x: jnp.where(m, x[L:], 0.0), incl, akk)
    a_ab = each(lambda m, x: jnp.where(m, x[:L], 0.0), strict, abb)
    a_rb = each(lambda m, x: jnp.where(m, x[L:], 0.0), incl, abb)

    pw = a_ab
    tinv = each(lambda x: eye + x, pw)
    for _ in range(5):
        pw = each(rcmul, pw, pw)
        tinv = each(lambda t, p: t + rcmul(t, p), tinv, pw)

    av = each(lambda x, y, z: rcmul(jnp.concatenate([x, y], axis=0), z), a_ak, a_rk, v)
    wu = each(lambda t, x, y: _dot(t.astype(BF16), jnp.concatenate([bd(x), bd(y[:L])], axis=1)), tinv, a_t, av)
    rbwu = each(lambda x, w: _dot(x.astype(BF16), jnp.concatenate([bd(w[:, :GW]), bd(w[:, GW:])], axis=1)),
                a_rb, wu)
    o = each(lambda x, w, s, y: _dot_nt((x + w[:, :GW]).astype(BF16), bd(s)) + y[L:] + w[:, GW:],
             r_t, rbwu, s0, av)

    m_bd = each(lambda w, x: _dot_tn(w[:, :GW].astype(BF16), x.astype(BF16)) * bdf, wu, b_h)
    n_full = each(lambda x, w, y, z: _dot_tn(jnp.concatenate([x, w[:, GW:]], axis=0).astype(BF16),
                                             jnp.concatenate([y, z], axis=0).astype(BF16)) * bdf,
                  v, wu, k_h, b_h)
    s1 = []
    for i in range(n):
        n_rc = n_full[i][0:HEAD]
        for h in range(1, HEADS_PER_GROUP):
            n_rc = n_rc + n_full[i][h * HEAD:(h + 1) * HEAD]
        s1.append(_dot(s0[i].astype(BF16), m_bd[i].astype(BF16)) + s0[i] * jnp.exp(tot[i]) + n_rc)
    return s1, o


def _scan_kernel(rf_ref, kf_ref, vf_ref, kkf_ref, af_ref, lwf_ref,
                 rb_ref, kb_ref, vb_ref, kkb_ref, ab_ref, lwb_ref, of_ref, ob_ref, s_ref, *, ng):
    @pl.when(pl.program_id(1) == 0)
    def _():
        s_ref[...] = jnp.zeros_like(s_ref)

    gi = lax.broadcasted_iota(jnp.int32, (GW, GW), 0) // HEAD
    gj = lax.broadcasted_iota(jnp.int32, (GW, GW), 1) // HEAD
    bdm = (gi == gj).astype(F32).astype(BF16)
    dirs = ((rf_ref, kf_ref, vf_ref, kkf_ref, af_ref, lwf_ref), (rb_ref, kb_ref, vb_ref, kkb_ref, ab_ref, lwb_ref))
    chains = [(d, h) for d in range(2) for h in range(ng)]
    ins = [[ref[0, :, h * GW:(h + 1) * GW] for d, h in chains for ref in (dirs[d][i],)] for i in range(6)]
    s1, o = _scan_chunks([s_ref[d, h] for d, h in chains], *ins, [d == 1 for d, _ in chains], bdm)
    for (d, h), s_new, o_new in zip(chains, s1, o):
        (of_ref, ob_ref)[d][0, :, h * GW:(h + 1) * GW] = o_new
        s_ref[d, h] = s_new


def _wkv_scan(r, v, kk, kf, af, lwf, kb, ab, lwb):
    b, t, c = r.shape
    nc = t // CHUNK
    ng = c // GW
    fspec = pl.BlockSpec((1, CHUNK, c), lambda i, j: (i, j, 0))
    bspec = pl.BlockSpec((1, CHUNK, c), lambda i, j: (i, nc - 1 - j, 0))
    return pl.pallas_call(
        functools.partial(_scan_kernel, ng=ng),
        out_shape=(jax.ShapeDtypeStruct((b, t, c), F32), jax.ShapeDtypeStruct((b, t, c), F32)),
        grid=(b, nc),
        in_specs=[fspec] * 6 + [bspec] * 6,
        out_specs=(fspec, bspec),
        scratch_shapes=[pltpu.VMEM((2, ng, HEAD, GW), F32)],
        compiler_params=_params(("parallel", "arbitrary")),
        name="rw_scan",
    )(r, kf, v, kk, af, lwf, r, kb, v, kk, ab, lwb)


def _rw_post_kernel(sf_ref, sb_ref, bon_ref, g_ref, lg_ref, lb_ref, avg_ref, o_ref):
    s = sf_ref[...] + sb_ref[...]
    mean = _dot2(s, avg_ref[...])
    d = s - mean
    var = _dot2(d * d, avg_ref[...])
    y = d * lax.rsqrt(var + GN_EPS) * lg_ref[...] + lb_ref[...] + bon_ref[...]
    o_ref[...] = (y * g_ref[...]).astype(o_ref.dtype)


def _rw_post(sf, sb, bon, g, lnx_g, lnx_b, tm=512):
    b, t, c = sf.shape
    m = b * t
    tm = min(tm, m)
    ch = jnp.arange(c) // HEAD
    avg = ((ch[:, None] == ch[None, :]).astype(F32) / HEAD).astype(BF16)
    big = pl.BlockSpec((tm, c), lambda i: (i, 0))
    small = pl.BlockSpec((1, c), lambda i: (0, 0))
    out = pl.pallas_call(
        _rw_post_kernel,
        out_shape=jax.ShapeDtypeStruct((m, c), BF16),
        grid=(m // tm,),
        in_specs=[big, big, big, big, small, small, pl.BlockSpec((c, c), lambda i: (0, 0))],
        out_specs=big,
        compiler_params=_params(("parallel",)),
        name="rw_post",
    )(sf.reshape(m, c), sb.reshape(m, c), bon.reshape(m, c), g.reshape(m, c),
      lnx_g.reshape(1, c), lnx_b.reshape(1, c), avg)
    return out.reshape(b, t, c)


def _rwkv_branch(x, norm_g, w_rw, v_first, shift, w0, w2, a0, a2, g2, k_k, k_a, r_k, lnx_g, lnx_b, vres, c):
    b, t, d = x.shape
    if vres is not None:
        v1 = vres[1].astype(BF16)
        w_rw = jnp.concatenate([w_rw, jnp.pad(v1, ((0, 0), (0, -v1.shape[1] % 128)))], axis=1)
    p = _norm_matmul(x.reshape(b * t, d), norm_g, w_rw, F32).reshape(b, t, -1)
    r, v, kk, lwf, lwb, kf, kb, af, ab, g, bon = _rw_prep(
        p, v_first, shift, w0, w2, a0, a2, g2, k_k, k_a, r_k, vres, c)
    sf, sb = _wkv_scan(r, v, kk, kf, af, lwf, kb, ab, lwb)
    y = _rw_post(sf, sb, bon, g, lnx_g, lnx_b)
    return y, (v if vres is None else v_first)


def _forward(x, norm1_g, w_in, hy_conv_w, hy_conv_b, hy_f_w1, hy_f_b1, hy_f_w2, hy_f_b2, hy_f_w3, hy_f_b3,
             hy_f_freq, hy_f_wout, hy_skip, hy_norm_g, rw_shift, rw_w0, rw_w2, rw_a0, rw_a2, rw_g2, rw_k_k,
             rw_k_a, rw_r_k, rw_lnx_g, rw_lnx_b, rw_v0, rw_v1, rw_v2, w_out, norm2_g, mlp_w1, mlp_w2,
             final_g):
    b, t, d = x.shape
    depth = w_in.shape[0]
    c_hy = hy_norm_g.shape[1]
    c_rw = rw_lnx_g.shape[1]
    hy_proj = hy_conv_b.shape[1]
    assert b % 2 == 0 and t % CHUNK == 0 and CHUNK == HEAD
    tabs = _dft_tables(t)
    assert tabs["n2"] % SUB == 0 and tabs["n1"] % SUB == 0
    v_first = None
    for l in range(depth):
        w_l = w_in[l].astype(BF16)
        y_hy = _hyena_branch(x, norm1_g[l], w_l[:, :hy_proj], hy_conv_w[l], hy_conv_b[l], hy_f_w1[l],
                             hy_f_b1[l], hy_f_w2[l], hy_f_b2[l], hy_f_w3[l], hy_f_b3[l], hy_f_freq[l],
                             hy_f_wout[l], hy_skip[l], hy_norm_g[l], tabs)
        vres = None if l == 0 else (rw_v0[l - 1], rw_v1[l - 1], rw_v2[l - 1])
        y_rw, v_first = _rwkv_branch(x, norm1_g[l], w_l[:, hy_proj:], v_first, rw_shift[l], rw_w0[l], rw_w2[l],
                                     rw_a0[l], rw_a2[l], rw_g2[l], rw_k_k[l], rw_k_a[l], rw_r_k[l],
                                     rw_lnx_g[l], rw_lnx_b[l], vres, c_rw)
        x = _out_proj(y_hy.reshape(b * t, c_hy), y_rw.reshape(b * t, c_rw), w_out[l].astype(BF16),
                      x.reshape(b * t, d)).reshape(b, t, d)
        x = _mlp(x.reshape(b * t, d), norm2_g[l], mlp_w1[l].astype(BF16), mlp_w2[l].astype(BF16)).reshape(b, t, d)
    return _rmsnorm(x.reshape(b * t, d), final_g, F32).reshape(b, t, d)


def kernel(x, norm1_g, w_in, hy_conv_w, hy_conv_b, hy_f_w1, hy_f_b1, hy_f_w2, hy_f_b2, hy_f_w3, hy_f_b3,
           hy_f_freq, hy_f_wout, hy_skip, hy_norm_g, rw_shift, rw_w0, rw_w2, rw_a0, rw_a2, rw_g2, rw_k_k,
           rw_k_a, rw_r_k, rw_lnx_g, rw_lnx_b, rw_v0, rw_v1, rw_v2, w_out, norm2_g, mlp_w1, mlp_w2, final_g):
    return _forward(x, norm1_g, w_in, hy_conv_w, hy_conv_b, hy_f_w1, hy_f_b1, hy_f_w2, hy_f_b2, hy_f_w3,
                    hy_f_b3, hy_f_freq, hy_f_wout, hy_skip, hy_norm_g, rw_shift, rw_w0, rw_w2, rw_a0, rw_a2,
                    rw_g2, rw_k_k, rw_k_a, rw_r_k, rw_lnx_g, rw_lnx_b, rw_v0, rw_v1, rw_v2, w_out, norm2_g,
                    mlp_w1, mlp_w2, final_g)
```

```python
import functools
import math

import jax
import jax.numpy as jnp
from jax import lax
from jax.experimental import pallas as pl
from jax.experimental.pallas import tpu as pltpu

F32 = jnp.float32
BF16 = jnp.bfloat16
HP = lax.Precision.HIGHEST

HEAD = 64
HEADS_PER_GROUP = 4
GW = HEAD * HEADS_PER_GROUP
CHUNK = 64
RMS_EPS = 1e-5
GN_EPS = HEAD * 1e-5
HY_TARGET = 1e-2
HY_FAST_PCT = 0.3
HY_SLOW_PCT = 1.5
VMEM_LIMIT = 56 * 1024 * 1024


def _params(sem, vmem=VMEM_LIMIT):
    return pltpu.CompilerParams(dimension_semantics=sem, vmem_limit_bytes=vmem)


def _dot(a, b, precision=None):
    return jnp.dot(a, b, preferred_element_type=F32, precision=precision)


def _dot_nt(a, b, precision=None):
    return lax.dot_general(a, b, (((1,), (1,)), ((), ())), preferred_element_type=F32,
                           precision=precision)


def _dot_tn(a, b, precision=None):
    return lax.dot_general(a, b, (((0,), (0,)), ((), ())), preferred_element_type=F32,
                           precision=precision)


def _rmsnorm_kernel(x_ref, g_ref, o_ref):
    x = x_ref[...]
    y = x * lax.rsqrt(jnp.mean(x * x, axis=-1, keepdims=True) + RMS_EPS)
    o_ref[...] = (y * g_ref[...]).astype(o_ref.dtype)


def _rmsnorm(x2d, g, out_dtype, tm=1024):
    m, d = x2d.shape
    tm = min(tm, m)
    return pl.pallas_call(
        _rmsnorm_kernel,
        out_shape=jax.ShapeDtypeStruct((m, d), out_dtype),
        grid=(m // tm,),
        in_specs=[pl.BlockSpec((tm, d), lambda i: (i, 0)), pl.BlockSpec((1, d), lambda i: (0, 0))],
        out_specs=pl.BlockSpec((tm, d), lambda i: (i, 0)),
        compiler_params=_params(("parallel",)),
        name="rmsnorm",
    )(x2d, g.reshape(1, d))


def _const_spec(shape):
    return pl.BlockSpec(shape, lambda *_: (0,) * len(shape), pipeline_mode=pl.Buffered(1))


def _dot2(x, m):
    hi = x.astype(BF16)
    lo = (x - hi.astype(F32)).astype(BF16)
    return _dot(hi, m) + _dot(lo, m)


def _mlp_kernel(x_ref, g_ref, w1_ref, w2_ref, o_ref, *, nchunk):
    x = x_ref[...]
    hn = (x * lax.rsqrt(jnp.mean(x * x, axis=-1, keepdims=True) + RMS_EPS) * g_ref[...]).astype(BF16)
    ff = w1_ref.shape[1]
    cw = ff // nchunk
    acc = x
    for j in range(nchunk):
        h = _dot(hn, w1_ref[:, j * cw:(j + 1) * cw])
        h = jnp.square(jnp.maximum(h, 0.0)).astype(BF16)
        acc = acc + _dot(h, w2_ref[j * cw:(j + 1) * cw, :])
    o_ref[...] = acc


def _mlp(x2d, g, w1, w2, tm=512, nchunk=4):
    m, d = x2d.shape
    ff = w1.shape[1]
    tm = min(tm, m)
    return pl.pallas_call(
        functools.partial(_mlp_kernel, nchunk=nchunk),
        out_shape=jax.ShapeDtypeStruct((m, d), F32),
        grid=(m // tm,),
        in_specs=[pl.BlockSpec((tm, d), lambda i: (i, 0)), _const_spec((1, d)),
                  _const_spec((d, ff)), _const_spec((ff, d))],
        out_specs=pl.BlockSpec((tm, d), lambda i: (i, 0)),
        compiler_params=_params(("parallel",)),
        name="mlp",
    )(x2d, g.reshape(1, d), w1, w2)


SUB = 8


def _rms(x, g):
    return (x * lax.rsqrt(jnp.mean(x * x, axis=-1, keepdims=True) + RMS_EPS) * g).astype(BF16)


def _norm_matmul_kernel(x_ref, g_ref, w_ref, o_ref):
    o_ref[...] = _dot(_rms(x_ref[...], g_ref[...]), w_ref[...]).astype(o_ref.dtype)


def _norm_matmul(x2d, g, w, out_dtype, tm=512):
    m, d = x2d.shape
    n = w.shape[1]
    tm = min(tm, m)
    return pl.pallas_call(
        _norm_matmul_kernel,
        out_shape=jax.ShapeDtypeStruct((m, n), out_dtype),
        grid=(m // tm,),
        in_specs=[pl.BlockSpec((tm, d), lambda i: (i, 0)), _const_spec((1, d)), _const_spec((d, n))],
        out_specs=pl.BlockSpec((tm, n), lambda i: (i, 0)),
        compiler_params=_params(("parallel",)),
        name="norm_matmul",
    )(x2d, g.reshape(1, d), w)


def _proj_perm_in_kernel(x_ref, g_ref, w_ref, o_ref):
    lhs = jnp.concatenate([_rms(x_ref[0, :, j, :], g_ref[...]) for j in range(SUB)], axis=0)
    o_ref[0] = _dot(lhs, w_ref[...]).astype(o_ref.dtype)


def _proj_perm_in(x, g, w, n1h, n2, out_dtype):
    b, t, d = x.shape
    p = w.shape[1]
    xv = x.reshape(b, n1h, n2, d)
    return pl.pallas_call(
        _proj_perm_in_kernel,
        out_shape=jax.ShapeDtypeStruct((b, t, p), out_dtype),
        grid=(b, n2 // SUB),
        in_specs=[pl.BlockSpec((1, n1h, SUB, d), lambda i, j: (i, 0, j, 0)), _const_spec((1, d)),
                  _const_spec((d, p))],
        out_specs=pl.BlockSpec((1, SUB * n1h, p), lambda i, j: (i, j, 0)),
        compiler_params=_params(("parallel", "parallel")),
        name="proj_perm_in",
    )(xv, g.reshape(1, d), w)


def _out_proj_kernel(yh_ref, yr_ref, wh_ref, wr_ref, r_ref, o_ref):
    o_ref[...] = (_dot(yh_ref[...].astype(BF16), wh_ref[...]) + _dot(yr_ref[...].astype(BF16), wr_ref[...])
                  + r_ref[...])


def _out_proj(yh, yr, w, res, tm=1024):
    m, ch = yh.shape
    cr = yr.shape[1]
    d = w.shape[1]
    tm = min(tm, m)
    row = lambda c: pl.BlockSpec((tm, c), lambda i: (i, 0))
    return pl.pallas_call(
        _out_proj_kernel,
        out_shape=jax.ShapeDtypeStruct((m, d), F32),
        grid=(m // tm,),
        in_specs=[row(ch), row(cr), _const_spec((ch, d)), _const_spec((cr, d)), row(d)],
        out_specs=row(d),
        compiler_params=_params(("parallel",)),
        name="out_proj",
    )(yh, yr, w[:ch], w[ch:], res)


def _shortconv_kernel(p_ref, w_ref, b_ref, o_ref, *, n1h, n2):
    w0 = w_ref[0:1, :]
    w1 = w_ref[1:2, :]
    w2 = w_ref[2:3, :]
    bias = b_ref[...]
    rows = lax.broadcasted_iota(jnp.int32, (n1h, 1), 0)

    def blk(i):
        return p_ref[0, pl.ds(pl.multiple_of(i * n1h, n1h), n1h), :].astype(F32)

    def body(i, carry):
        o_ref[0, pl.ds(pl.multiple_of(i * n1h, n1h), n1h), :] = (
            w0 * blk(i - 1) + w1 * blk(i) + w2 * blk(i + 1) + bias).astype(o_ref.dtype)
        return carry

    lax.fori_loop(1, n2 - 1, body, 0)
    last = p_ref[0, (n2 - 1) * n1h:, :].astype(F32)
    first = p_ref[0, :n1h, :].astype(F32)
    prev0 = jnp.where(rows == 0, 0.0, pltpu.roll(last, 1, 0))
    o_ref[0, :n1h, :] = (w0 * prev0 + w1 * first + w2 * p_ref[0, n1h:2 * n1h, :].astype(F32)
                         + bias).astype(o_ref.dtype)
    nxt = jnp.where(rows == n1h - 1, 0.0, pltpu.roll(first, n1h - 1, 0))
    o_ref[0, (n2 - 1) * n1h:, :] = (w0 * p_ref[0, (n2 - 2) * n1h:(n2 - 1) * n1h, :].astype(F32) + w1 * last
                                    + w2 * nxt + bias).astype(o_ref.dtype)


def _shortconv(p, w, bias, n1h, n2, cb=128):
    b, t, c = p.shape
    return pl.pallas_call(
        functools.partial(_shortconv_kernel, n1h=n1h, n2=n2),
        out_shape=jax.ShapeDtypeStruct((b, t, c), BF16),
        grid=(b, c // cb),
        in_specs=[pl.BlockSpec((1, t, cb), lambda i, j: (i, 0, j)),
                  pl.BlockSpec((3, cb), lambda i, j: (0, j)),
                  pl.BlockSpec((1, cb), lambda i, j: (0, j))],
        out_specs=pl.BlockSpec((1, t, cb), lambda i, j: (i, 0, j)),
        compiler_params=_params(("parallel", "parallel")),
        name="hy_shortconv",
    )(p, w, bias.reshape(1, c))


def _dft_tables(t):
    n = 2 * t
    n1 = 1 << ((n.bit_length() - 1) // 2)
    n2 = n // n1
    n1h = n1 // 2
    two_pi = 2.0 * math.pi
    k1 = jnp.arange(n1, dtype=jnp.int32)
    n2i = jnp.arange(n2, dtype=jnp.int32)

    def cs(prod, mod):
        ang = (two_pi / mod) * (prod % mod).astype(F32)
        return jnp.cos(ang), jnp.sin(ang)

    tpos = n2i[:, None] + n2 * jnp.arange(n1h, dtype=jnp.int32)[None, :]
    c, s = cs(k1[None, :, None] * tpos[:, None, :], n)
    g1 = jnp.concatenate([jnp.concatenate([c, s], axis=2), jnp.concatenate([-s, c], axis=2)], axis=1)
    ct, st = jnp.swapaxes(c, 1, 2) / n, jnp.swapaxes(s, 1, 2) / n
    g4a = jnp.concatenate([ct, st], axis=1)
    g4b = jnp.concatenate([-st, ct], axis=1)
    tposf = n2i[:, None] + n2 * jnp.arange(n1, dtype=jnp.int32)[None, :]
    cf, sf = cs(k1[None, :, None] * tposf[:, None, :], n)
    g1f = jnp.concatenate([cf, -sf], axis=1)
    c2, s2 = cs(n2i[:, None] * n2i[None, :], n2)
    f2a = jnp.concatenate([c2, -s2], axis=0)
    f2b = jnp.concatenate([s2, c2], axis=0)
    f3 = jnp.concatenate([jnp.concatenate([c2, -s2], axis=1), jnp.concatenate([s2, c2], axis=1)], axis=0)
    return dict(n1=n1, n2=n2, n1h=n1h, g1=g1.astype(BF16), g4a=g4a.astype(BF16), g4b=g4b.astype(BF16),
                g1f=g1f, f2a=f2a, f2b=f2b, f3=f3.astype(BF16))


def _filter_mlp_kernel(z_ref, w1_ref, b1_ref, w2_ref, b2_ref, w3_ref, b3_ref, fr_ref, wo_ref, dl_ref,
                       o_ref):
    z = z_ref[...]
    h = jnp.sin(fr_ref[0:1, :] * (_dot(z, w1_ref[...], HP) + b1_ref[...]))
    h = jnp.sin(fr_ref[1:2, :] * (_dot(h, w2_ref[...], HP) + b2_ref[...]))
    h = jnp.sin(fr_ref[2:3, :] * (_dot(h, w3_ref[...], HP) + b3_ref[...]))
    window = jnp.exp(-z[:, 0:1] * dl_ref[...])
    o_ref[...] = _dot(h, wo_ref[...], HP) * window


def _filter_mlp(t, w1, b1, w2, b2, w3, b3, freq, w_out, c_hy, tr=512):
    emb, width = w1.shape
    bands = (emb - 1) // 2
    pos = jnp.arange(t, dtype=F32)
    tt = pos / max(t - 1, 1)
    fr = jnp.linspace(1e-4, bands - 1, bands, dtype=F32)
    ang = (2.0 * math.pi / t) * pos[:, None] * fr[None, :]
    z = jnp.concatenate([tt[:, None], jnp.cos(ang), -jnp.sin(ang)], axis=-1)
    z = jnp.pad(z, ((0, 0), (0, 128 - emb)))
    w1 = jnp.pad(w1, ((0, 128 - emb), (0, 0)))
    emb = 128
    max_decay = math.log(HY_TARGET) / HY_FAST_PCT
    min_decay = math.log(HY_TARGET) / HY_SLOW_PCT
    delta = jnp.abs(jnp.linspace(min_decay, max_decay, c_hy, dtype=F32))
    nout = w_out.shape[1]
    delta_full = jnp.tile(delta, nout // c_hy).reshape(1, nout)
    tr = min(tr, t)
    full = lambda a: pl.BlockSpec(a.shape, lambda i: (0,) * a.ndim)
    args = [z, w1, b1.reshape(1, -1), w2, b2.reshape(1, -1), w3, b3.reshape(1, -1), freq, w_out, delta_full]
    return pl.pallas_call(
        _filter_mlp_kernel,
        out_shape=jax.ShapeDtypeStruct((t, nout), F32),
        grid=(t // tr,),
        in_specs=[pl.BlockSpec((tr, emb), lambda i: (i, 0))] + [full(a) for a in args[1:]],
        out_specs=pl.BlockSpec((tr, nout), lambda i: (i, 0)),
        compiler_params=_params(("parallel",)),
        name="hy_filter_mlp",
    )(*args)


def _filter_s1_kernel(x_ref, g_ref, o_ref, ss_ref, *, g, oc):
    @pl.when(pl.program_id(0) == 0)
    def _():
        ss_ref[...] = jnp.zeros_like(ss_ref)

    ss = ss_ref[...]
    for j in range(g):
        x = x_ref[:, j, :]
        o_ref[j] = _dot(g_ref[j], x, HP)
        ss = ss + jnp.sum(x * x, axis=0, keepdims=True)
    ss_ref[...] = ss


def _filter_s2_kernel(sr_ref, si_ref, fa_ref, fb_ref, ss_ref, o_ref, *, g, oc):
    scale = lax.rsqrt(ss_ref[...] + 1e-6)
    for j in range(g):
        x = _dot(fa_ref[...], sr_ref[:, j, :], HP) + _dot(fb_ref[...], si_ref[:, j, :], HP)
        o_ref[j] = x * scale


def _filter_spectrum(two, tabs):
    n, oc = two.shape
    n1, n2 = tabs["n1"], tabs["n2"]
    g = SUB
    s1, ss = pl.pallas_call(
        functools.partial(_filter_s1_kernel, g=g, oc=oc),
        out_shape=(jax.ShapeDtypeStruct((n2, 2 * n1, oc), F32), jax.ShapeDtypeStruct((1, oc), F32)),
        grid=(n2 // g,),
        in_specs=[pl.BlockSpec((n1, g, oc), lambda j: (0, j, 0)),
                  pl.BlockSpec((g, 2 * n1, n1), lambda j: (j, 0, 0))],
        out_specs=(pl.BlockSpec((g, 2 * n1, oc), lambda j: (j, 0, 0)),
                   pl.BlockSpec((1, oc), lambda j: (0, 0))),
        compiler_params=_params(("arbitrary",)),
        name="hy_filter_dft1",
    )(two.reshape(n1, n2, oc), tabs["g1f"])
    return pl.pallas_call(
        functools.partial(_filter_s2_kernel, g=g, oc=oc),
        out_shape=jax.ShapeDtypeStruct((n1, 2 * n2, oc), F32),
        grid=(n1 // g,),
        in_specs=[pl.BlockSpec((n2, g, oc), lambda k: (0, k, 0)),
                  pl.BlockSpec((n2, g, oc), lambda k: (0, n1 // g + k, 0)),
                  pl.BlockSpec((2 * n2, n2), lambda k: (0, 0)),
                  pl.BlockSpec((2 * n2, n2), lambda k: (0, 0)),
                  pl.BlockSpec((1, oc), lambda k: (0, 0))],
        out_specs=pl.BlockSpec((g, 2 * n2, oc), lambda k: (k, 0, 0)),
        compiler_params=_params(("parallel",)),
        name="hy_filter_dft2",
    )(s1, s1, tabs["f2a"], tabs["f2b"], ss)


def _conv_s1_kernel(u_ref, g_ref, o_ref, *, g, n1h):
    for j in range(g):
        rows = slice(j * n1h, (j + 1) * n1h)
        rhs = jnp.concatenate([u_ref[0, 0, rows, :], u_ref[0, 1, rows, :]], axis=0)
        o_ref[0, j] = _dot(g_ref[j], rhs).astype(o_ref.dtype)


def _conv_s23_kernel(sr_ref, si_ref, fa_ref, fb_ref, f3_ref, h_ref, o_ref, *, g, c, n2):
    for j in range(g):
        cols = slice(j * c, (j + 1) * c)
        x = _dot(fa_ref[...], sr_ref[0, :, cols]) + _dot(fb_ref[...], si_ref[0, :, cols])
        xr, xi = x[:n2], x[n2:]
        hr, hi = h_ref[j, :n2, :], h_ref[j, n2:, :]
        y = jnp.concatenate([xr * hr - xi * hi, xr * hi + xi * hr], axis=0).astype(BF16)
        o_ref[0, j] = _dot(f3_ref[...], y).astype(o_ref.dtype)


def _conv_s4_kernel(rr_ref, ri_ref, ga_ref, gb_ref, u_ref, gate_ref, skip_ref, *rest, g, c, n1h, norm):
    if norm:
        ng_ref, avg_ref, o_ref = rest
    else:
        (o_ref,) = rest
    skip = skip_ref[...]
    for j in range(g):
        cols = slice(j * c, (j + 1) * c)
        rows = slice(j * n1h, (j + 1) * n1h)
        y = _dot(ga_ref[j], rr_ref[0, :, cols]) + _dot(gb_ref[j], ri_ref[0, :, cols])
        for q in range(2):
            z = gate_ref[0, q, rows, :].astype(F32) * (y[q * n1h:(q + 1) * n1h]
                                                       + u_ref[0, q, rows, :].astype(F32) * skip)
            if norm:
                ms = _dot2(z * z, avg_ref[...])
                o_ref[0, q, :, j, :] = z * lax.rsqrt(ms + RMS_EPS) * ng_ref[...]
            else:
                o_ref[0, q, rows, :] = z.astype(o_ref.dtype)


def _long_conv_gate(u_arr, u_col, gate_arr, gate_col, h_spec, h_col, skip, tabs, c, norm_g=None):
    b, t, _ = u_arr.shape
    n1, n2, n1h = tabs["n1"], tabs["n2"], tabs["n1h"]
    g = SUB
    npair = b // 2
    u4 = u_arr.reshape(npair, 2, t, u_arr.shape[2])
    gate4 = gate_arr.reshape(npair, 2, t, gate_arr.shape[2])
    s1 = pl.pallas_call(
        functools.partial(_conv_s1_kernel, g=g, n1h=n1h),
        out_shape=jax.ShapeDtypeStruct((npair, n2, 2 * n1, c), BF16),
        grid=(npair, n2 // g),
        in_specs=[pl.BlockSpec((1, 2, g * n1h, c), lambda p, j: (p, 0, j, u_col)),
                  pl.BlockSpec((g, 2 * n1, n1), lambda p, j: (j, 0, 0))],
        out_specs=pl.BlockSpec((1, g, 2 * n1, c), lambda p, j: (p, j, 0, 0)),
        compiler_params=_params(("parallel", "parallel")),
        name="hy_conv_dft1",
    )(u4, tabs["g1"])
    sv = s1.reshape(npair, n2, 2 * n1 * c)
    r = pl.pallas_call(
        functools.partial(_conv_s23_kernel, g=g, c=c, n2=n2),
        out_shape=jax.ShapeDtypeStruct((npair, n1, 2 * n2, c), BF16),
        grid=(npair, n1 // g),
        in_specs=[pl.BlockSpec((1, n2, g * c), lambda p, k: (p, 0, k)),
                  pl.BlockSpec((1, n2, g * c), lambda p, k: (p, 0, n1 // g + k)),
                  pl.BlockSpec((2 * n2, n2), lambda p, k: (0, 0)),
                  pl.BlockSpec((2 * n2, n2), lambda p, k: (0, 0)),
                  pl.BlockSpec((2 * n2, 2 * n2), lambda p, k: (0, 0)),
                  pl.BlockSpec((g, 2 * n2, c), lambda p, k: (k, 0, h_col))],
        out_specs=pl.BlockSpec((1, g, 2 * n2, c), lambda p, k: (p, k, 0, 0)),
        compiler_params=_params(("parallel", "parallel")),
        name="hy_conv_dft23",
    )(sv, sv, tabs["f2a"].astype(BF16), tabs["f2b"].astype(BF16), tabs["f3"], h_spec)
    rv = r.reshape(npair, n1, 2 * n2 * c)
    norm = norm_g is not None
    in_specs = [pl.BlockSpec((1, n1, g * c), lambda p, j: (p, 0, j)),
                pl.BlockSpec((1, n1, g * c), lambda p, j: (p, 0, n2 // g + j)),
                pl.BlockSpec((g, n1, n1), lambda p, j: (j, 0, 0)),
                pl.BlockSpec((g, n1, n1), lambda p, j: (j, 0, 0)),
                pl.BlockSpec((1, 2, g * n1h, c), lambda p, j: (p, 0, j, u_col)),
                pl.BlockSpec((1, 2, g * n1h, c), lambda p, j: (p, 0, j, gate_col)),
                pl.BlockSpec((1, c), lambda p, j: (0, 0))]
    args = [rv, rv, tabs["g4a"], tabs["g4b"], u4, gate4, skip.reshape(1, c)]
    if norm:
        ch = jnp.arange(c) // HEAD
        avg = ((ch[:, None] == ch[None, :]).astype(F32) / HEAD).astype(BF16)
        in_specs += [pl.BlockSpec((1, c), lambda p, j: (0, 0)), pl.BlockSpec((c, c), lambda p, j: (0, 0))]
        args += [norm_g.reshape(1, c), avg]
        out_shape = jax.ShapeDtypeStruct((npair, 2, n1h, n2, c), F32)
        out_spec = pl.BlockSpec((1, 2, n1h, g, c), lambda p, j: (p, 0, 0, j, 0))
    else:
        out_shape = jax.ShapeDtypeStruct((npair, 2, t, c), BF16)
        out_spec = pl.BlockSpec((1, 2, g * n1h, c), lambda p, j: (p, 0, j, 0))
    out = pl.pallas_call(
        functools.partial(_conv_s4_kernel, g=g, c=c, n1h=n1h, norm=norm),
        out_shape=out_shape,
        grid=(npair, n2 // g),
        in_specs=in_specs,
        out_specs=out_spec,
        compiler_params=_params(("parallel", "parallel")),
        name="hy_conv_dft4",
    )(*args)
    return out.reshape(b, t, c)


def _hyena_branch(x, norm_g1, w_hy, conv_w, conv_b, fw1, fb1, fw2, fb2, fw3, fb3, ffreq, fwout, skip, norm_g,
                  tabs):
    b, t, d = x.shape
    c = norm_g.shape[0]
    n1, n2, n1h = tabs["n1"], tabs["n2"], tabs["n1h"]
    p = _proj_perm_in(x, norm_g1, w_hy, n1h, n2, BF16)
    u = _shortconv(p, conv_w, conv_b, n1h, n2)
    h = _filter_mlp(t, fw1, fb1, fw2, fb2, fw3, fb3, ffreq, fwout, c)
    order = skip.shape[0]
    h = h.reshape(t, order, 2, c)
    fwd, bwd = h[:, :, 0], h[:, :, 1]
    two = jnp.concatenate([fwd, jnp.zeros_like(fwd[:1]), bwd[1:][::-1]], axis=0).reshape(2 * t, order * c)
    hspec = _filter_spectrum(two, tabs)
    z = _long_conv_gate(u, 2, u, 0, hspec, 0, skip[0], tabs, c)
    return _long_conv_gate(z, 0, u, 1, hspec, 1, skip[1], tabs, c, norm_g=norm_g)


def _rw_prep_kernel(*refs, has_vres, c):
    if has_vres:
        (p_ref, pp_ref, pn_ref, mu_ref, w0_ref, w2_ref, a0_ref, a2_ref, g2_ref, kk_ref, ka_ref, rk_ref,
         sum_ref, vf_ref, v0_ref, v2_ref,
         r_o, v_o, kk_o, lwf_o, lwb_o, kf_o, kb_o, af_o, ab_o, g_o, bon_o) = refs
    else:
        (p_ref, pp_ref, pn_ref, mu_ref, w0_ref, w2_ref, a0_ref, a2_ref, g2_ref, kk_ref, ka_ref, rk_ref,
         sum_ref,
         r_o, v_o, kk_o, lwf_o, lwb_o, kf_o, kb_o, af_o, ab_o, g_o, bon_o) = refs
    i = pl.program_id(1)
    last = pl.num_programs(1) - 1
    nsh = mu_ref.shape[1]
    p = p_ref[0, :, :nsh]
    tt = p.shape[0]
    rows = lax.broadcasted_iota(jnp.int32, (tt, 1), 0)
    prev_row = jnp.where(i == 0, 0.0, pp_ref[0, 7:8, :nsh])
    next_row = jnp.where(i == last, 0.0, pn_ref[0, 0:1, :nsh])
    prev = jnp.where(rows == 0, prev_row, pltpu.roll(p, 1, 0))
    nxt = jnp.where(rows == tt - 1, next_row, pltpu.roll(p, tt - 1, 0))
    pf = p + mu_ref[0:1, :] * (prev - p) + mu_ref[1:2, :] * (nxt - p)
    r = pf[:, :c]
    k = pf[:, c:2 * c]
    v = pf[:, 2 * c:3 * c]
    lw = 3 * c
    nd = w2_ref.shape[0]
    na = a2_ref.shape[0]
    wd = pf[:, lw:lw + nd]
    ad = pf[:, lw + nd:lw + nd + na]
    gd = pf[:, lw + nd + na:]
    if has_vres:
        lora = _dot(p_ref[0, :, nsh:].astype(BF16), v2_ref[...])
        v = v + (vf_ref[0] - v) * jax.nn.sigmoid(v0_ref[...] + lora)
    g = _dot(jax.nn.sigmoid(gd).astype(BF16), g2_ref[...])
    kk = k * kk_ref[...]
    nrm = jnp.sqrt(_dot2(kk * kk, sum_ref[...]))
    kk = kk / jnp.maximum(nrm, 1e-12)
    wl = w0_ref[...] + _dot(jnp.tanh(wd).astype(BF16), w2_ref[...])
    w = -jax.nn.softplus(-wl) - 0.5
    logw = -jnp.exp(w)
    a = jax.nn.sigmoid(a0_ref[...] + _dot(ad.astype(BF16), a2_ref[...]))
    ka = ka_ref[...]
    rk = rk_ref[...]
    k_d = [k * (1.0 + (a[:, d * c:(d + 1) * c] - 1.0) * ka) for d in range(2)]
    bon = _dot2(r * (k_d[0] + k_d[1]) * rk, sum_ref[...]) * v
    r_o[0] = r
    v_o[0] = v
    kk_o[0] = kk
    lwf_o[0] = logw[:, :c]
    lwb_o[0] = logw[:, c:]
    kf_o[0] = k_d[0]
    kb_o[0] = k_d[1]
    af_o[0] = a[:, :c]
    ab_o[0] = a[:, c:]
    g_o[0] = g
    bon_o[0] = bon


def _blockdiag2(m):
    k, c = m.shape[1], m.shape[2]
    z = jnp.zeros((k, c), m.dtype)
    return jnp.concatenate([jnp.concatenate([m[0], z], axis=1), jnp.concatenate([z, m[1]], axis=1)], axis=0)


def _rw_prep(p, v_first, shift, w0, w2, a0, a2, g2, k_k, k_a, r_k, vres, c, tt=256):
    b, t, pw = p.shape
    tt = min(tt, t)
    has_vres = vres is not None
    ch = jnp.arange(c) // HEAD
    summ = (ch[:, None] == ch[None, :]).astype(BF16)
    row = lambda a: a.reshape(1, -1)
    args = [p, p, p, shift, row(w0), _blockdiag2(w2).astype(BF16), row(a0), _blockdiag2(a2).astype(BF16),
            g2.astype(BF16), row(k_k), row(k_a), row(r_k), summ]
    full = lambda a: pl.BlockSpec(a.shape, lambda i, j: (0,) * a.ndim)
    nb8 = t // 8
    in_specs = [pl.BlockSpec((1, tt, pw), lambda i, j: (i, j, 0)),
                pl.BlockSpec((1, 8, pw), lambda i, j: (i, jnp.maximum(j * (tt // 8) - 1, 0), 0)),
                pl.BlockSpec((1, 8, pw), lambda i, j: (i, jnp.minimum((j + 1) * (tt // 8), nb8 - 1), 0))]
    in_specs += [full(a) for a in args[3:]]
    if has_vres:
        v0, _, v2 = vres
        v2p = jnp.pad(v2, ((0, pw - shift.shape[1] - v2.shape[0]), (0, 0))).astype(BF16)
        extra = [v_first, row(v0), v2p]
        in_specs += [pl.BlockSpec((1, tt, c), lambda i, j: (i, j, 0))] + [full(a) for a in extra[1:]]
        args += extra
    outs = tuple(jax.ShapeDtypeStruct((b, t, c), F32) for _ in range(11))
    return pl.pallas_call(
        functools.partial(_rw_prep_kernel, has_vres=has_vres, c=c),
        out_shape=outs,
        grid=(b, t // tt),
        in_specs=in_specs,
        out_specs=tuple(pl.BlockSpec((1, tt, c), lambda i, j: (i, j, 0)) for _ in range(11)),
        compiler_params=_params(("parallel", "parallel")),
        name="rw_prep",
    )(*args)


def _split3(x):
    h1 = x.astype(BF16)
    r1 = x - h1.astype(F32)
    h2 = r1.astype(BF16)
    h3 = (r1 - h2.astype(F32)).astype(BF16)
    return h1, h2, h3


def _scan_chunks(s0, r, k, v, kk, a, lw, rev, bdm):
    L = CHUNK
    n = len(r)
    each = lambda f, *ls: [f(*xs) for xs in zip(*ls)]
    ti = lax.broadcasted_iota(jnp.int32, (L, L), 0)
    si = lax.broadcasted_iota(jnp.int32, (L, L), 1)
    tt = lax.broadcasted_iota(jnp.int32, (L, GW), 0)
    ss = lax.broadcasted_iota(jnp.int32, (L, GW), 1) % L
    eye = (ss == tt).astype(F32)
    tri = [((si >= ti) if q else (si <= ti)).astype(F32).astype(BF16) for q in rev]
    strict = [(ss > tt) if q else (ss < tt) for q in rev]
    incl = [(ss >= tt) if q else (ss <= tt) for q in rev]
    bdf = bdm.astype(F32)

    def bd(x):
        xb = x.astype(BF16)
        return jnp.concatenate([xb] * HEADS_PER_GROUP, axis=0) * bdm

    def bd_t(x):
        xt = x.T.astype(BF16)
        return jnp.concatenate([xt] * HEADS_PER_GROUP, axis=1) * bdm

    def rcmul(x, y):
        return _dot(x.astype(BF16), bd(y))

    parts = each(_split3, lw)
    cum = [_dot(t, p[0]) + _dot(t, p[1]) + _dot(t, p[2]) for t, p in zip(tri, parts)]
    tot = [c[0:1, :] if q else c[L - 1:L, :] for c, q in zip(cum, rev)]
    p_in = each(jnp.exp, cum)
    p_inv = each(lambda c: jnp.exp(-c), cum)
    p_ex = each(lambda c, w: jnp.exp(c - w), cum, lw)
    p_rem = each(lambda t, c: jnp.exp(t - c), tot, cum)
    beta = each(lambda x, y: x * y, kk, a)
    a_t = each(lambda x, p: -x * p, kk, p_ex)
    r_t = each(lambda x, p: x * p, r, p_in)
    k_t = each(lambda x, p: x * p, k, p_inv)
    b_t = each(lambda x, p: x * p, beta, p_inv)
    k_h = each(lambda x, p: x * p, k, p_rem)
    b_h = each(lambda x, p: x * p, beta, p_rem)

    lhs = each(lambda x, y: jnp.concatenate([x, y], axis=0).astype(BF16), a_t, r_t)
    akk = each(lambda x, y: _dot(x, bd_t(y)), lhs, k_t)
    abb = each(lambda x, y: _dot(x, bd_t(y)), lhs, b_t)
    a_ak = each(lambda m, x: jnp.where(m, x[:L], 0.0), strict, akk)
    a_rk = each(lambda m, x: jnp.where(m, x[L:], 0.0), incl, akk)
    a_ab = each(lambda m, x: jnp.where(m, x[:L], 0.0), strict, abb)
    a_rb = each(lambda m, x: jnp.where(m, x[L:], 0.0), incl, abb)

    pw = a_ab
    tinv = each(lambda x: eye + x, pw)
    pw = each(rcmul, pw, pw)
    for _ in range(4):
        both = each(lambda p, t: rcmul(jnp.concatenate([p, t], axis=0), p), pw, tinv)
        pw = [x[:L] for x in both]
        tinv = each(lambda t, x: t + x[L:], tinv, both)
    tinv = each(lambda t, p: t + rcmul(t, p), tinv, pw)

    av = each(lambda x, y, z: rcmul(jnp.concatenate([x, y], axis=0), z), a_ak, a_rk, v)
    ct = each(rcmul, a_rb, tinv)
    wu2 = each(lambda t, c, x, y: _dot(jnp.concatenate([t, c], axis=0).astype(BF16),
                                       jnp.concatenate([bd(x), bd(y[:L])], axis=1)), tinv, ct, a_t, av)
    wu = [x[:L] for x in wu2]
    rbwu = [x[L:] for x in wu2]
    o = each(lambda x, w, s, y: _dot((x + w[:, :GW]).astype(BF16), bd_t(s)) + y[L:] + w[:, GW:],
             r_t, rbwu, s0, av)

    m_bd = each(lambda w, x: _dot_tn(w[:, :GW].astype(BF16), x.astype(BF16)) * bdf, wu, b_h)
    n_full = each(lambda x, w, y, z: _dot_tn(jnp.concatenate([x, w[:, GW:]], axis=0).astype(BF16),
                                             jnp.concatenate([y, z], axis=0).astype(BF16)) * bdf,
                  v, wu, k_h, b_h)
    s1 = []
    for i in range(n):
        n_rc = n_full[i][0:HEAD]
        for h in range(1, HEADS_PER_GROUP):
            n_rc = n_rc + n_full[i][h * HEAD:(h + 1) * HEAD]
        s1.append(_dot(s0[i].astype(BF16), m_bd[i].astype(BF16)) + s0[i] * jnp.exp(tot[i]) + n_rc)
    return s1, o


def _scan_kernel(rf_ref, kf_ref, vf_ref, kkf_ref, af_ref, lwf_ref,
                 rb_ref, kb_ref, vb_ref, kkb_ref, ab_ref, lwb_ref, of_ref, ob_ref, s_ref, *, ng, nb):
    @pl.when(pl.program_id(1) == 0)
    def _():
        s_ref[...] = jnp.zeros_like(s_ref)

    gi = lax.broadcasted_iota(jnp.int32, (GW, GW), 0) // HEAD
    gj = lax.broadcasted_iota(jnp.int32, (GW, GW), 1) // HEAD
    bdm = (gi == gj).astype(F32).astype(BF16)
    dirs = ((rf_ref, kf_ref, vf_ref, kkf_ref, af_ref, lwf_ref), (rb_ref, kb_ref, vb_ref, kkb_ref, ab_ref, lwb_ref))
    chains = [(i, d, h) for i in range(nb) for d in range(2) for h in range(ng)]
    ins = [[dirs[d][q][i, :, h * GW:(h + 1) * GW] for i, d, h in chains] for q in range(6)]
    s1, o = _scan_chunks([s_ref[i, d, h] for i, d, h in chains], *ins, [d == 1 for _, d, _ in chains], bdm)
    for (i, d, h), s_new, o_new in zip(chains, s1, o):
        (of_ref, ob_ref)[d][i, :, h * GW:(h + 1) * GW] = o_new
        s_ref[i, d, h] = s_new


def _wkv_scan(r, v, kk, kf, af, lwf, kb, ab, lwb, nb=4):
    b, t, c = r.shape
    nc = t // CHUNK
    ng = c // GW
    fspec = pl.BlockSpec((nb, CHUNK, c), lambda i, j: (i, j, 0))
    bspec = pl.BlockSpec((nb, CHUNK, c), lambda i, j: (i, nc - 1 - j, 0))
    return pl.pallas_call(
        functools.partial(_scan_kernel, ng=ng, nb=nb),
        out_shape=(jax.ShapeDtypeStruct((b, t, c), F32), jax.ShapeDtypeStruct((b, t, c), F32)),
        grid=(b // nb, nc),
        in_specs=[fspec] * 6 + [bspec] * 6,
        out_specs=(fspec, bspec),
        scratch_shapes=[pltpu.VMEM((nb, 2, ng, HEAD, GW), F32)],
        compiler_params=_params(("parallel", "arbitrary")),
        name="rw_scan",
    )(r, kf, v, kk, af, lwf, r, kb, v, kk, ab, lwb)


def _rw_post_kernel(sf_ref, sb_ref, bon_ref, g_ref, lg_ref, lb_ref, avg_ref, o_ref):
    s = sf_ref[...] + sb_ref[...]
    mean = _dot2(s, avg_ref[...])
    d = s - mean
    var = _dot2(d * d, avg_ref[...])
    y = d * lax.rsqrt(var + GN_EPS) * lg_ref[...] + lb_ref[...] + bon_ref[...]
    o_ref[...] = (y * g_ref[...]).astype(o_ref.dtype)


def _rw_post(sf, sb, bon, g, lnx_g, lnx_b, tm=512):
    b, t, c = sf.shape
    m = b * t
    tm = min(tm, m)
    ch = jnp.arange(c) // HEAD
    avg = ((ch[:, None] == ch[None, :]).astype(F32) / HEAD).astype(BF16)
    big = pl.BlockSpec((tm, c), lambda i: (i, 0))
    small = pl.BlockSpec((1, c), lambda i: (0, 0))
    out = pl.pallas_call(
        _rw_post_kernel,
        out_shape=jax.ShapeDtypeStruct((m, c), BF16),
        grid=(m // tm,),
        in_specs=[big, big, big, big, small, small, pl.BlockSpec((c, c), lambda i: (0, 0))],
        out_specs=big,
        compiler_params=_params(("parallel",)),
        name="rw_post",
    )(sf.reshape(m, c), sb.reshape(m, c), bon.reshape(m, c), g.reshape(m, c),
      lnx_g.reshape(1, c), lnx_b.reshape(1, c), avg)
    return out.reshape(b, t, c)


def _rwkv_branch(x, norm_g, w_rw, v_first, shift, w0, w2, a0, a2, g2, k_k, k_a, r_k, lnx_g, lnx_b, vres, c):
    b, t, d = x.shape
    if vres is not None:
        v1 = vres[1].astype(BF16)
        w_rw = jnp.concatenate([w_rw, jnp.pad(v1, ((0, 0), (0, -v1.shape[1] % 128)))], axis=1)
    p = _norm_matmul(x.reshape(b * t, d), norm_g, w_rw, F32).reshape(b, t, -1)
    r, v, kk, lwf, lwb, kf, kb, af, ab, g, bon = _rw_prep(
        p, v_first, shift, w0, w2, a0, a2, g2, k_k, k_a, r_k, vres, c)
    sf, sb = _wkv_scan(r, v, kk, kf, af, lwf, kb, ab, lwb)
    y = _rw_post(sf, sb, bon, g, lnx_g, lnx_b)
    return y, (v if vres is None else v_first)


def _forward(x, norm1_g, w_in, hy_conv_w, hy_conv_b, hy_f_w1, hy_f_b1, hy_f_w2, hy_f_b2, hy_f_w3, hy_f_b3,
             hy_f_freq, hy_f_wout, hy_skip, hy_norm_g, rw_shift, rw_w0, rw_w2, rw_a0, rw_a2, rw_g2, rw_k_k,
             rw_k_a, rw_r_k, rw_lnx_g, rw_lnx_b, rw_v0, rw_v1, rw_v2, w_out, norm2_g, mlp_w1, mlp_w2,
             final_g):
    b, t, d = x.shape
    depth = w_in.shape[0]
    c_hy = hy_norm_g.shape[1]
    c_rw = rw_lnx_g.shape[1]
    hy_proj = hy_conv_b.shape[1]
    assert b % 2 == 0 and t % CHUNK == 0 and CHUNK == HEAD
    tabs = _dft_tables(t)
    assert tabs["n2"] % SUB == 0 and tabs["n1"] % SUB == 0
    v_first = None
    for l in range(depth):
        w_l = w_in[l].astype(BF16)
        y_hy = _hyena_branch(x, norm1_g[l], w_l[:, :hy_proj], hy_conv_w[l], hy_conv_b[l], hy_f_w1[l],
                             hy_f_b1[l], hy_f_w2[l], hy_f_b2[l], hy_f_w3[l], hy_f_b3[l], hy_f_freq[l],
                             hy_f_wout[l], hy_skip[l], hy_norm_g[l], tabs)
        vres = None if l == 0 else (rw_v0[l - 1], rw_v1[l - 1], rw_v2[l - 1])
        y_rw, v_first = _rwkv_branch(x, norm1_g[l], w_l[:, hy_proj:], v_first, rw_shift[l], rw_w0[l], rw_w2[l],
                                     rw_a0[l], rw_a2[l], rw_g2[l], rw_k_k[l], rw_k_a[l], rw_r_k[l],
                                     rw_lnx_g[l], rw_lnx_b[l], vres, c_rw)
        x = _out_proj(y_hy.reshape(b * t, c_hy), y_rw.reshape(b * t, c_rw), w_out[l].astype(BF16),
                      x.reshape(b * t, d)).reshape(b, t, d)
        x = _mlp(x.reshape(b * t, d), norm2_g[l], mlp_w1[l].astype(BF16), mlp_w2[l].astype(BF16)).reshape(b, t, d)
    return _rmsnorm(x.reshape(b * t, d), final_g, F32).reshape(b, t, d)


def kernel(x, norm1_g, w_in, hy_conv_w, hy_conv_b, hy_f_w1, hy_f_b1, hy_f_w2, hy_f_b2, hy_f_w3, hy_f_b3,
           hy_f_freq, hy_f_wout, hy_skip, hy_norm_g, rw_shift, rw_w0, rw_w2, rw_a0, rw_a2, rw_g2, rw_k_k,
           rw_k_a, rw_r_k, rw_lnx_g, rw_lnx_b, rw_v0, rw_v1, rw_v2, w_out, norm2_g, mlp_w1, mlp_w2, final_g):
    return _forward(x, norm1_g, w_in, hy_conv_w, hy_conv_b, hy_f_w1, hy_f_b1, hy_f_w2, hy_f_b2, hy_f_w3,
                    hy_f_b3, hy_f_freq, hy_f_wout, hy_skip, hy_norm_g, rw_shift, rw_w0, rw_w2, rw_a0, rw_a2,
                    rw_g2, rw_k_k, rw_k_a, rw_r_k, rw_lnx_g, rw_lnx_b, rw_v0, rw_v1, rw_v2, w_out, norm2_g,
                    mlp_w1, mlp_w2, final_g)
```

```python
import functools
import math

import jax
import jax.numpy as jnp
from jax import lax
from jax.experimental import pallas as pl
from jax.experimental.pallas import tpu as pltpu

F32 = jnp.float32
BF16 = jnp.bfloat16
HP = lax.Precision.HIGHEST

HEAD = 64
HEADS_PER_GROUP = 4
GW = HEAD * HEADS_PER_GROUP
CHUNK = 64
RMS_EPS = 1e-5
GN_EPS = HEAD * 1e-5
HY_TARGET = 1e-2
HY_FAST_PCT = 0.3
HY_SLOW_PCT = 1.5
VMEM_LIMIT = 56 * 1024 * 1024


def _params(sem, vmem=VMEM_LIMIT):
    return pltpu.CompilerParams(dimension_semantics=sem, vmem_limit_bytes=vmem)


def _dot(a, b, precision=None):
    return jnp.dot(a, b, preferred_element_type=F32, precision=precision)


def _dot_nt(a, b, precision=None):
    return lax.dot_general(a, b, (((1,), (1,)), ((), ())), preferred_element_type=F32,
                           precision=precision)


def _dot_tn(a, b, precision=None):
    return lax.dot_general(a, b, (((0,), (0,)), ((), ())), preferred_element_type=F32,
                           precision=precision)


def _rmsnorm_kernel(x_ref, g_ref, o_ref):
    x = x_ref[...]
    y = x * lax.rsqrt(jnp.mean(x * x, axis=-1, keepdims=True) + RMS_EPS)
    o_ref[...] = (y * g_ref[...]).astype(o_ref.dtype)


def _rmsnorm(x2d, g, out_dtype, tm=1024):
    m, d = x2d.shape
    tm = min(tm, m)
    return pl.pallas_call(
        _rmsnorm_kernel,
        out_shape=jax.ShapeDtypeStruct((m, d), out_dtype),
        grid=(m // tm,),
        in_specs=[pl.BlockSpec((tm, d), lambda i: (i, 0)), pl.BlockSpec((1, d), lambda i: (0, 0))],
        out_specs=pl.BlockSpec((tm, d), lambda i: (i, 0)),
        compiler_params=_params(("parallel",)),
        name="rmsnorm",
    )(x2d, g.reshape(1, d))


def _const_spec(shape):
    return pl.BlockSpec(shape, lambda *_: (0,) * len(shape), pipeline_mode=pl.Buffered(1))


def _dot2(x, m):
    hi = x.astype(BF16)
    lo = (x - hi.astype(F32)).astype(BF16)
    return _dot(hi, m) + _dot(lo, m)


def _dot_hl(a, b):
    ah = a.astype(BF16)
    al = (a - ah.astype(F32)).astype(BF16)
    bh = b.astype(BF16)
    bl = (b - bh.astype(F32)).astype(BF16)
    return _dot(ah, bh) + _dot(ah, bl) + _dot(al, bh)


def _mlp_kernel(x_ref, g_ref, w1_ref, w2_ref, o_ref, *, nchunk):
    x = x_ref[...]
    hn = (x * lax.rsqrt(jnp.mean(x * x, axis=-1, keepdims=True) + RMS_EPS) * g_ref[...]).astype(BF16)
    ff = w1_ref.shape[1]
    cw = ff // nchunk
    acc = x
    for j in range(nchunk):
        h = _dot(hn, w1_ref[:, j * cw:(j + 1) * cw])
        h = jnp.square(jnp.maximum(h, 0.0)).astype(BF16)
        acc = acc + _dot(h, w2_ref[j * cw:(j + 1) * cw, :])
    o_ref[...] = acc


def _mlp(x2d, g, w1, w2, tm=512, nchunk=4):
    m, d = x2d.shape
    ff = w1.shape[1]
    tm = min(tm, m)
    return pl.pallas_call(
        functools.partial(_mlp_kernel, nchunk=nchunk),
        out_shape=jax.ShapeDtypeStruct((m, d), F32),
        grid=(m // tm,),
        in_specs=[pl.BlockSpec((tm, d), lambda i: (i, 0)), _const_spec((1, d)),
                  _const_spec((d, ff)), _const_spec((ff, d))],
        out_specs=pl.BlockSpec((tm, d), lambda i: (i, 0)),
        compiler_params=_params(("parallel",)),
        name="mlp",
    )(x2d, g.reshape(1, d), w1, w2)


SUB = 8
SUB16 = 16


def _rms(x, g):
    return (x * lax.rsqrt(jnp.mean(x * x, axis=-1, keepdims=True) + RMS_EPS) * g).astype(BF16)


def _in_proj_kernel(x_ref, g_ref, wh_ref, wr_ref, oh_ref, or_ref, *, n2):
    hn = _rms(x_ref[0], g_ref[...])
    or_ref[0] = _dot(hn, wr_ref[...]).astype(or_ref.dtype)
    ph = _dot(hn, wh_ref[...])
    for j in range(SUB):
        oh_ref[0, :, j, :] = ph[j * n2:(j + 1) * n2]


def _in_proj(x, g, w_hy, w_rw, n1h, n2):
    b, t, d = x.shape
    ph, pr = w_hy.shape[1], w_rw.shape[1]
    tm = SUB * n2
    p_hy, p_rw = pl.pallas_call(
        functools.partial(_in_proj_kernel, n2=n2),
        out_shape=(jax.ShapeDtypeStruct((b, n2, n1h, ph), F32), jax.ShapeDtypeStruct((b, t, pr), BF16)),
        grid=(b, t // tm),
        in_specs=[pl.BlockSpec((1, tm, d), lambda i, j: (i, j, 0)), _const_spec((1, d)),
                  _const_spec((d, ph)), _const_spec((d, pr))],
        out_specs=(pl.BlockSpec((1, n2, SUB, ph), lambda i, j: (i, 0, j, 0)),
                   pl.BlockSpec((1, tm, pr), lambda i, j: (i, j, 0))),
        compiler_params=_params(("parallel", "parallel")),
        name="in_proj",
    )(x, g.reshape(1, d), w_hy, w_rw)
    return p_hy.reshape(b, t, ph), p_rw


def _out_proj_kernel(yh_ref, yr_ref, wh_ref, wr_ref, r_ref, o_ref):
    o_ref[...] = _dot(yh_ref[...].astype(BF16), wh_ref[...]) + _dot(yr_ref[...], wr_ref[...]) + r_ref[...]


def _out_proj(yh, yr, w, res, tm=1024):
    m, ch = yh.shape
    cr = yr.shape[1]
    d = w.shape[1]
    tm = min(tm, m)
    row = lambda c: pl.BlockSpec((tm, c), lambda i: (i, 0))
    return pl.pallas_call(
        _out_proj_kernel,
        out_shape=jax.ShapeDtypeStruct((m, d), F32),
        grid=(m // tm,),
        in_specs=[row(ch), row(cr), _const_spec((ch, d)), _const_spec((cr, d)), row(d)],
        out_specs=row(d),
        compiler_params=_params(("parallel",)),
        name="out_proj",
    )(yh, yr, w[:ch], w[ch:], res)


def _shortconv_kernel(p_ref, w_ref, b_ref, o_ref, *, n1h, n2):
    w0 = w_ref[0:1, :]
    w1 = w_ref[1:2, :]
    w2 = w_ref[2:3, :]
    bias = b_ref[...]
    rows = lax.broadcasted_iota(jnp.int32, (n1h, 1), 0)

    def blk(i):
        return p_ref[0, pl.ds(pl.multiple_of(i * n1h, n1h), n1h), :].astype(F32)

    def body(i, carry):
        o_ref[0, pl.ds(pl.multiple_of(i * n1h, n1h), n1h), :] = (
            w0 * blk(i - 1) + w1 * blk(i) + w2 * blk(i + 1) + bias).astype(o_ref.dtype)
        return carry

    lax.fori_loop(1, n2 - 1, body, 0)
    last = p_ref[0, (n2 - 1) * n1h:, :].astype(F32)
    first = p_ref[0, :n1h, :].astype(F32)
    prev0 = jnp.where(rows == 0, 0.0, pltpu.roll(last, 1, 0))
    o_ref[0, :n1h, :] = (w0 * prev0 + w1 * first + w2 * p_ref[0, n1h:2 * n1h, :].astype(F32)
                         + bias).astype(o_ref.dtype)
    nxt = jnp.where(rows == n1h - 1, 0.0, pltpu.roll(first, n1h - 1, 0))
    o_ref[0, (n2 - 1) * n1h:, :] = (w0 * p_ref[0, (n2 - 2) * n1h:(n2 - 1) * n1h, :].astype(F32) + w1 * last
                                    + w2 * nxt + bias).astype(o_ref.dtype)


def _shortconv(p, w, bias, n1h, n2, cb=128):
    b, t, c = p.shape
    return pl.pallas_call(
        functools.partial(_shortconv_kernel, n1h=n1h, n2=n2),
        out_shape=jax.ShapeDtypeStruct((b, t, c), BF16),
        grid=(b, c // cb),
        in_specs=[pl.BlockSpec((1, t, cb), lambda i, j: (i, 0, j)),
                  pl.BlockSpec((3, cb), lambda i, j: (0, j)),
                  pl.BlockSpec((1, cb), lambda i, j: (0, j))],
        out_specs=pl.BlockSpec((1, t, cb), lambda i, j: (i, 0, j)),
        compiler_params=_params(("parallel", "parallel")),
        name="hy_shortconv",
    )(p, w, bias.reshape(1, c))


def _dft_tables(t):
    n = 2 * t
    n1 = 1 << ((n.bit_length() - 1) // 2)
    n2 = n // n1
    n1h = n1 // 2
    two_pi = 2.0 * math.pi
    k1 = jnp.arange(n1, dtype=jnp.int32)
    n2i = jnp.arange(n2, dtype=jnp.int32)

    def cs(prod, mod):
        ang = (two_pi / mod) * (prod % mod).astype(F32)
        return jnp.cos(ang), jnp.sin(ang)

    tpos = n2i[:, None] + n2 * jnp.arange(n1h, dtype=jnp.int32)[None, :]
    c, s = cs(k1[None, :, None] * tpos[:, None, :], n)
    g1 = jnp.concatenate([jnp.concatenate([c, s], axis=2), jnp.concatenate([-s, c], axis=2)], axis=1)
    ct, st = jnp.swapaxes(c, 1, 2) / n, jnp.swapaxes(s, 1, 2) / n
    g4a = jnp.concatenate([ct, st], axis=1)
    g4b = jnp.concatenate([-st, ct], axis=1)
    tposf = n2i[:, None] + n2 * jnp.arange(n1, dtype=jnp.int32)[None, :]
    cf, sf = cs(k1[None, :, None] * tposf[:, None, :], n)
    g1f = jnp.concatenate([cf, -sf], axis=1)
    c2, s2 = cs(n2i[:, None] * n2i[None, :], n2)
    f2a = jnp.concatenate([c2, -s2], axis=0)
    f2b = jnp.concatenate([s2, c2], axis=0)
    f3 = jnp.concatenate([jnp.concatenate([c2, -s2], axis=1), jnp.concatenate([s2, c2], axis=1)], axis=0)
    return dict(n1=n1, n2=n2, n1h=n1h, g1=g1.astype(BF16), g4a=g4a.astype(BF16), g4b=g4b.astype(BF16),
                g1f=g1f, f2a=f2a, f2b=f2b, f3=f3.astype(BF16))


def _filter_mlp_kernel(z_ref, w1_ref, b1_ref, w2_ref, b2_ref, w3_ref, b3_ref, fr_ref, wo_ref, dl_ref,
                       o_ref):
    z = z_ref[...]
    h = jnp.sin(fr_ref[0:1, :] * (_dot(z, w1_ref[...], HP) + b1_ref[...]))
    h = jnp.sin(fr_ref[1:2, :] * (_dot(h, w2_ref[...], HP) + b2_ref[...]))
    h = jnp.sin(fr_ref[2:3, :] * (_dot(h, w3_ref[...], HP) + b3_ref[...]))
    window = jnp.exp(-z[:, 0:1] * dl_ref[...])
    o_ref[...] = _dot(h, wo_ref[...], HP) * window


def _filter_mlp(t, w1, b1, w2, b2, w3, b3, freq, w_out, c_hy, tr=512):
    emb, width = w1.shape
    bands = (emb - 1) // 2
    pos = jnp.arange(t, dtype=F32)
    tt = pos / max(t - 1, 1)
    fr = jnp.linspace(1e-4, bands - 1, bands, dtype=F32)
    ang = (2.0 * math.pi / t) * pos[:, None] * fr[None, :]
    z = jnp.concatenate([tt[:, None], jnp.cos(ang), -jnp.sin(ang)], axis=-1)
    z = jnp.pad(z, ((0, 0), (0, 128 - emb)))
    w1 = jnp.pad(w1, ((0, 128 - emb), (0, 0)))
    emb = 128
    max_decay = math.log(HY_TARGET) / HY_FAST_PCT
    min_decay = math.log(HY_TARGET) / HY_SLOW_PCT
    delta = jnp.abs(jnp.linspace(min_decay, max_decay, c_hy, dtype=F32))
    nout = w_out.shape[1]
    delta_full = jnp.tile(delta, nout // c_hy).reshape(1, nout)
    tr = min(tr, t)
    full = lambda a: pl.BlockSpec(a.shape, lambda i: (0,) * a.ndim)
    args = [z, w1, b1.reshape(1, -1), w2, b2.reshape(1, -1), w3, b3.reshape(1, -1), freq, w_out, delta_full]
    return pl.pallas_call(
        _filter_mlp_kernel,
        out_shape=jax.ShapeDtypeStruct((t, nout), F32),
        grid=(t // tr,),
        in_specs=[pl.BlockSpec((tr, emb), lambda i: (i, 0))] + [full(a) for a in args[1:]],
        out_specs=pl.BlockSpec((tr, nout), lambda i: (i, 0)),
        compiler_params=_params(("parallel",)),
        name="hy_filter_mlp",
    )(*args)


def _filter_s1_kernel(x_ref, g_ref, o_ref, ss_ref, *, g, oc):
    @pl.when(pl.program_id(0) == 0)
    def _():
        ss_ref[...] = jnp.zeros_like(ss_ref)

    ss = ss_ref[...]
    for j in range(g):
        x = x_ref[:, j, :]
        o_ref[j] = _dot_hl(g_ref[j], x)
        ss = ss + jnp.sum(x * x, axis=0, keepdims=True)
    ss_ref[...] = ss


def _filter_s2_kernel(sr_ref, si_ref, fa_ref, fb_ref, ss_ref, o_ref, *, g, oc):
    scale = lax.rsqrt(ss_ref[...] + 1e-6)
    for j in range(g):
        x = _dot_hl(fa_ref[...], sr_ref[:, j, :]) + _dot_hl(fb_ref[...], si_ref[:, j, :])
        o_ref[j] = x * scale


def _filter_spectrum(two, tabs):
    n, oc = two.shape
    n1, n2 = tabs["n1"], tabs["n2"]
    g = SUB
    s1, ss = pl.pallas_call(
        functools.partial(_filter_s1_kernel, g=g, oc=oc),
        out_shape=(jax.ShapeDtypeStruct((n2, 2 * n1, oc), F32), jax.ShapeDtypeStruct((1, oc), F32)),
        grid=(n2 // g,),
        in_specs=[pl.BlockSpec((n1, g, oc), lambda j: (0, j, 0)),
                  pl.BlockSpec((g, 2 * n1, n1), lambda j: (j, 0, 0))],
        out_specs=(pl.BlockSpec((g, 2 * n1, oc), lambda j: (j, 0, 0)),
                   pl.BlockSpec((1, oc), lambda j: (0, 0))),
        compiler_params=_params(("arbitrary",)),
        name="hy_filter_dft1",
    )(two.reshape(n1, n2, oc), tabs["g1f"])
    return pl.pallas_call(
        functools.partial(_filter_s2_kernel, g=g, oc=oc),
        out_shape=jax.ShapeDtypeStruct((n1, 2 * n2, oc), F32),
        grid=(n1 // g,),
        in_specs=[pl.BlockSpec((n2, g, oc), lambda k: (0, k, 0)),
                  pl.BlockSpec((n2, g, oc), lambda k: (0, n1 // g + k, 0)),
                  pl.BlockSpec((2 * n2, n2), lambda k: (0, 0)),
                  pl.BlockSpec((2 * n2, n2), lambda k: (0, 0)),
                  pl.BlockSpec((1, oc), lambda k: (0, 0))],
        out_specs=pl.BlockSpec((g, 2 * n2, oc), lambda k: (k, 0, 0)),
        compiler_params=_params(("parallel",)),
        name="hy_filter_dft2",
    )(s1, s1, tabs["f2a"], tabs["f2b"], ss)


def _conv_s1_kernel(u_ref, g_ref, o_ref, *, g, n1h):
    for j in range(g):
        rows = slice(j * n1h, (j + 1) * n1h)
        rhs = jnp.concatenate([u_ref[0, 0, rows, :], u_ref[0, 1, rows, :]], axis=0)
        o_ref[0, j] = _dot(g_ref[j], rhs).astype(o_ref.dtype)


def _conv_s23_kernel(sr_ref, si_ref, fa_ref, fb_ref, f3_ref, h_ref, o_ref, *, g, c, n2):
    for j in range(g):
        cols = slice(j * c, (j + 1) * c)
        x = _dot(fa_ref[...], sr_ref[0, :, cols]) + _dot(fb_ref[...], si_ref[0, :, cols])
        xr, xi = x[:n2], x[n2:]
        hr, hi = h_ref[j, :n2, :], h_ref[j, n2:, :]
        y = jnp.concatenate([xr * hr - xi * hi, xr * hi + xi * hr], axis=0).astype(BF16)
        o_ref[0, j] = _dot(f3_ref[...], y).astype(o_ref.dtype)


def _conv_s4_kernel(rr_ref, ri_ref, ga_ref, gb_ref, u_ref, gate_ref, skip_ref, *rest, g, c, n1h, norm):
    if norm:
        ng_ref, avg_ref, o_ref = rest
    else:
        (o_ref,) = rest
    skip = skip_ref[...]
    for j in range(g):
        cols = slice(j * c, (j + 1) * c)
        rows = slice(j * n1h, (j + 1) * n1h)
        y = _dot(ga_ref[j], rr_ref[0, :, cols]) + _dot(gb_ref[j], ri_ref[0, :, cols])
        for q in range(2):
            z = gate_ref[0, q, rows, :].astype(F32) * (y[q * n1h:(q + 1) * n1h]
                                                       + u_ref[0, q, rows, :].astype(F32) * skip)
            if norm:
                ms = _dot2(z * z, avg_ref[...])
                o_ref[0, q, :, j, :] = z * lax.rsqrt(ms + RMS_EPS) * ng_ref[...]
            else:
                o_ref[0, q, rows, :] = z.astype(o_ref.dtype)


def _long_conv_gate(u_arr, u_col, gate_arr, gate_col, h_spec, h_col, skip, tabs, c, norm_g=None):
    b, t, _ = u_arr.shape
    n1, n2, n1h = tabs["n1"], tabs["n2"], tabs["n1h"]
    g = SUB
    npair = b // 2
    u4 = u_arr.reshape(npair, 2, t, u_arr.shape[2])
    gate4 = gate_arr.reshape(npair, 2, t, gate_arr.shape[2])
    s1 = pl.pallas_call(
        functools.partial(_conv_s1_kernel, g=g, n1h=n1h),
        out_shape=jax.ShapeDtypeStruct((npair, n2, 2 * n1, c), BF16),
        grid=(npair, n2 // g),
        in_specs=[pl.BlockSpec((1, 2, g * n1h, c), lambda p, j: (p, 0, j, u_col)),
                  pl.BlockSpec((g, 2 * n1, n1), lambda p, j: (j, 0, 0))],
        out_specs=pl.BlockSpec((1, g, 2 * n1, c), lambda p, j: (p, j, 0, 0)),
        compiler_params=_params(("parallel", "parallel")),
        name="hy_conv_dft1",
    )(u4, tabs["g1"])
    sv = s1.reshape(npair, n2, 2 * n1 * c)
    r = pl.pallas_call(
        functools.partial(_conv_s23_kernel, g=g, c=c, n2=n2),
        out_shape=jax.ShapeDtypeStruct((npair, n1, 2 * n2, c), BF16),
        grid=(npair, n1 // g),
        in_specs=[pl.BlockSpec((1, n2, g * c), lambda p, k: (p, 0, k)),
                  pl.BlockSpec((1, n2, g * c), lambda p, k: (p, 0, n1 // g + k)),
                  pl.BlockSpec((2 * n2, n2), lambda p, k: (0, 0)),
                  pl.BlockSpec((2 * n2, n2), lambda p, k: (0, 0)),
                  pl.BlockSpec((2 * n2, 2 * n2), lambda p, k: (0, 0)),
                  pl.BlockSpec((g, 2 * n2, c), lambda p, k: (k, 0, h_col))],
        out_specs=pl.BlockSpec((1, g, 2 * n2, c), lambda p, k: (p, k, 0, 0)),
        compiler_params=_params(("parallel", "parallel")),
        name="hy_conv_dft23",
    )(sv, sv, tabs["f2a"].astype(BF16), tabs["f2b"].astype(BF16), tabs["f3"], h_spec)
    rv = r.reshape(npair, n1, 2 * n2 * c)
    norm = norm_g is not None
    in_specs = [pl.BlockSpec((1, n1, g * c), lambda p, j: (p, 0, j)),
                pl.BlockSpec((1, n1, g * c), lambda p, j: (p, 0, n2 // g + j)),
                pl.BlockSpec((g, n1, n1), lambda p, j: (j, 0, 0)),
                pl.BlockSpec((g, n1, n1), lambda p, j: (j, 0, 0)),
                pl.BlockSpec((1, 2, g * n1h, c), lambda p, j: (p, 0, j, u_col)),
                pl.BlockSpec((1, 2, g * n1h, c), lambda p, j: (p, 0, j, gate_col)),
                pl.BlockSpec((1, c), lambda p, j: (0, 0))]
    args = [rv, rv, tabs["g4a"], tabs["g4b"], u4, gate4, skip.reshape(1, c)]
    if norm:
        ch = jnp.arange(c) // HEAD
        avg = ((ch[:, None] == ch[None, :]).astype(F32) / HEAD).astype(BF16)
        in_specs += [pl.BlockSpec((1, c), lambda p, j: (0, 0)), pl.BlockSpec((c, c), lambda p, j: (0, 0))]
        args += [norm_g.reshape(1, c), avg]
        out_shape = jax.ShapeDtypeStruct((npair, 2, n1h, n2, c), F32)
        out_spec = pl.BlockSpec((1, 2, n1h, g, c), lambda p, j: (p, 0, 0, j, 0))
    else:
        out_shape = jax.ShapeDtypeStruct((npair, 2, t, c), BF16)
        out_spec = pl.BlockSpec((1, 2, g * n1h, c), lambda p, j: (p, 0, j, 0))
    out = pl.pallas_call(
        functools.partial(_conv_s4_kernel, g=g, c=c, n1h=n1h, norm=norm),
        out_shape=out_shape,
        grid=(npair, n2 // g),
        in_specs=in_specs,
        out_specs=out_spec,
        compiler_params=_params(("parallel", "parallel")),
        name="hy_conv_dft4",
    )(*args)
    return out.reshape(b, t, c)


def _hyena_branch(p, conv_w, conv_b, fw1, fb1, fw2, fb2, fw3, fb3, ffreq, fwout, skip, norm_g, tabs):
    b, t, _ = p.shape
    c = norm_g.shape[0]
    n1, n2, n1h = tabs["n1"], tabs["n2"], tabs["n1h"]
    u = _shortconv(p, conv_w, conv_b, n1h, n2)
    h = _filter_mlp(t, fw1, fb1, fw2, fb2, fw3, fb3, ffreq, fwout, c)
    order = skip.shape[0]
    h = h.reshape(t, order, 2, c)
    fwd, bwd = h[:, :, 0], h[:, :, 1]
    two = jnp.concatenate([fwd, jnp.zeros_like(fwd[:1]), bwd[1:][::-1]], axis=0).reshape(2 * t, order * c)
    hspec = _filter_spectrum(two, tabs)
    z = _long_conv_gate(u, 2, u, 0, hspec, 0, skip[0], tabs, c)
    return _long_conv_gate(z, 0, u, 1, hspec, 1, skip[1], tabs, c, norm_g=norm_g)


def _rw_prep_kernel(*refs, has_vres, c):
    if has_vres:
        (p_ref, pp_ref, pn_ref, mu_ref, w0_ref, w2_ref, a0_ref, a2_ref, g2_ref, kk_ref, ka_ref, rk_ref,
         sum_ref, vf_ref, v0_ref, v2_ref,
         r_o, v_o, kk_o, lwf_o, lwb_o, kf_o, kb_o, af_o, ab_o, g_o, bon_o) = refs
    else:
        (p_ref, pp_ref, pn_ref, mu_ref, w0_ref, w2_ref, a0_ref, a2_ref, g2_ref, kk_ref, ka_ref, rk_ref,
         sum_ref,
         r_o, v_o, kk_o, lwf_o, lwb_o, kf_o, kb_o, af_o, ab_o, g_o, bon_o) = refs
    i = pl.program_id(1)
    last = pl.num_programs(1) - 1
    nsh = mu_ref.shape[1]
    p = p_ref[0, :, :nsh].astype(F32)
    tt = p.shape[0]
    rows = lax.broadcasted_iota(jnp.int32, (tt, 1), 0)
    prev_row = jnp.where(i == 0, 0.0, pp_ref[0, SUB16 - 1:SUB16, :nsh].astype(F32))
    next_row = jnp.where(i == last, 0.0, pn_ref[0, 0:1, :nsh].astype(F32))
    prev = jnp.where(rows == 0, prev_row, pltpu.roll(p, 1, 0))
    nxt = jnp.where(rows == tt - 1, next_row, pltpu.roll(p, tt - 1, 0))
    pf = p + mu_ref[0:1, :] * (prev - p) + mu_ref[1:2, :] * (nxt - p)
    r = pf[:, :c]
    k = pf[:, c:2 * c]
    v = pf[:, 2 * c:3 * c]
    lw = 3 * c
    nd = w2_ref.shape[0]
    na = a2_ref.shape[0]
    wd = pf[:, lw:lw + nd]
    ad = pf[:, lw + nd:lw + nd + na]
    gd = pf[:, lw + nd + na:]
    if has_vres:
        lora = _dot(p_ref[0, :, nsh:], v2_ref[...])
        v = v + (vf_ref[0].astype(F32) - v) * jax.nn.sigmoid(v0_ref[...] + lora)
    g = _dot(jax.nn.sigmoid(gd).astype(BF16), g2_ref[...])
    kk = k * kk_ref[...]
    nrm = jnp.sqrt(_dot2(kk * kk, sum_ref[...]))
    kk = kk / jnp.maximum(nrm, 1e-12)
    wl = w0_ref[...] + _dot(jnp.tanh(wd).astype(BF16), w2_ref[...])
    w = -jax.nn.softplus(-wl) - 0.5
    logw = -jnp.exp(w)
    a = jax.nn.sigmoid(a0_ref[...] + _dot(ad.astype(BF16), a2_ref[...]))
    ka = ka_ref[...]
    rk = rk_ref[...]
    k_d = [k * (1.0 + (a[:, d * c:(d + 1) * c] - 1.0) * ka) for d in range(2)]
    bon = _dot2(r * (k_d[0] + k_d[1]) * rk, sum_ref[...]) * v
    r_o[0] = r.astype(BF16)
    v_o[0] = v.astype(BF16)
    kk_o[0] = kk.astype(BF16)
    lwf_o[0] = logw[:, :c]
    lwb_o[0] = logw[:, c:]
    kf_o[0] = k_d[0].astype(BF16)
    kb_o[0] = k_d[1].astype(BF16)
    af_o[0] = a[:, :c].astype(BF16)
    ab_o[0] = a[:, c:].astype(BF16)
    g_o[0] = g.astype(BF16)
    bon_o[0] = bon.astype(BF16)


def _blockdiag2(m):
    k, c = m.shape[1], m.shape[2]
    z = jnp.zeros((k, c), m.dtype)
    return jnp.concatenate([jnp.concatenate([m[0], z], axis=1), jnp.concatenate([z, m[1]], axis=1)], axis=0)


def _rw_prep(p, v_first, shift, w0, w2, a0, a2, g2, k_k, k_a, r_k, vres, c, tt=256):
    b, t, pw = p.shape
    tt = min(tt, t)
    has_vres = vres is not None
    ch = jnp.arange(c) // HEAD
    summ = (ch[:, None] == ch[None, :]).astype(BF16)
    row = lambda a: a.reshape(1, -1)
    args = [p, p, p, shift, row(w0), _blockdiag2(w2).astype(BF16), row(a0), _blockdiag2(a2).astype(BF16),
            g2.astype(BF16), row(k_k), row(k_a), row(r_k), summ]
    full = lambda a: pl.BlockSpec(a.shape, lambda i, j: (0,) * a.ndim)
    nhb = t // SUB16
    in_specs = [pl.BlockSpec((1, tt, pw), lambda i, j: (i, j, 0)),
                pl.BlockSpec((1, SUB16, pw), lambda i, j: (i, jnp.maximum(j * (tt // SUB16) - 1, 0), 0)),
                pl.BlockSpec((1, SUB16, pw), lambda i, j: (i, jnp.minimum((j + 1) * (tt // SUB16), nhb - 1), 0))]
    in_specs += [full(a) for a in args[3:]]
    if has_vres:
        v0, _, v2 = vres
        v2p = jnp.pad(v2, ((0, pw - shift.shape[1] - v2.shape[0]), (0, 0))).astype(BF16)
        extra = [v_first, row(v0), v2p]
        in_specs += [pl.BlockSpec((1, tt, c), lambda i, j: (i, j, 0))] + [full(a) for a in extra[1:]]
        args += extra
    outs = tuple(jax.ShapeDtypeStruct((b, t, c), F32 if i in (3, 4) else BF16) for i in range(11))
    return pl.pallas_call(
        functools.partial(_rw_prep_kernel, has_vres=has_vres, c=c),
        out_shape=outs,
        grid=(b, t // tt),
        in_specs=in_specs,
        out_specs=tuple(pl.BlockSpec((1, tt, c), lambda i, j: (i, j, 0)) for _ in range(11)),
        compiler_params=_params(("parallel", "parallel")),
        name="rw_prep",
    )(*args)


def _split3(x):
    h1 = x.astype(BF16)
    r1 = x - h1.astype(F32)
    h2 = r1.astype(BF16)
    h3 = (r1 - h2.astype(F32)).astype(BF16)
    return h1, h2, h3


def _scan_chunks(s0, r, k, v, kk, a, lw, rev, bdm):
    L = CHUNK
    n = len(r)
    each = lambda f, *ls: [f(*xs) for xs in zip(*ls)]
    ti = lax.broadcasted_iota(jnp.int32, (L, L), 0)
    si = lax.broadcasted_iota(jnp.int32, (L, L), 1)
    tt = lax.broadcasted_iota(jnp.int32, (L, GW), 0)
    ss = lax.broadcasted_iota(jnp.int32, (L, GW), 1) % L
    eye = (ss == tt).astype(F32)
    tri = [((si >= ti) if q else (si <= ti)).astype(F32).astype(BF16) for q in rev]
    strict = [(ss > tt) if q else (ss < tt) for q in rev]
    incl = [(ss >= tt) if q else (ss <= tt) for q in rev]
    bdf = bdm.astype(F32)

    def bd(x):
        xb = x.astype(BF16)
        return jnp.concatenate([xb] * HEADS_PER_GROUP, axis=0) * bdm

    def bd_t(x):
        xt = x.T.astype(BF16)
        return jnp.concatenate([xt] * HEADS_PER_GROUP, axis=1) * bdm

    def rcmul(x, y):
        return _dot(x.astype(BF16), bd(y))

    parts = each(_split3, lw)
    cum = [_dot(t, p[0]) + _dot(t, p[1]) + _dot(t, p[2]) for t, p in zip(tri, parts)]
    tot = [c[0:1, :] if q else c[L - 1:L, :] for c, q in zip(cum, rev)]
    p_in = each(jnp.exp, cum)
    p_inv = each(lambda c: jnp.exp(-c), cum)
    p_ex = each(lambda c, w: jnp.exp(c - w), cum, lw)
    p_rem = each(lambda t, c: jnp.exp(t - c), tot, cum)
    beta = each(lambda x, y: x * y, kk, a)
    a_t = each(lambda x, p: -x * p, kk, p_ex)
    r_t = each(lambda x, p: x * p, r, p_in)
    k_t = each(lambda x, p: x * p, k, p_inv)
    b_t = each(lambda x, p: x * p, beta, p_inv)
    k_h = each(lambda x, p: x * p, k, p_rem)
    b_h = each(lambda x, p: x * p, beta, p_rem)

    lhs = each(lambda x, y: jnp.concatenate([x, y], axis=0).astype(BF16), a_t, r_t)
    akk = each(lambda x, y: _dot(x, bd_t(y)), lhs, k_t)
    abb = each(lambda x, y: _dot(x, bd_t(y)), lhs, b_t)
    a_ak = each(lambda m, x: jnp.where(m, x[:L], 0.0), strict, akk)
    a_rk = each(lambda m, x: jnp.where(m, x[L:], 0.0), incl, akk)
    a_ab = each(lambda m, x: jnp.where(m, x[:L], 0.0), strict, abb)
    a_rb = each(lambda m, x: jnp.where(m, x[L:], 0.0), incl, abb)

    pw = a_ab
    tinv = each(lambda x: eye + x, pw)
    pw = each(rcmul, pw, pw)
    for _ in range(4):
        both = each(lambda p, t: rcmul(jnp.concatenate([p, t], axis=0), p), pw, tinv)
        pw = [x[:L] for x in both]
        tinv = each(lambda t, x: t + x[L:], tinv, both)
    tinv = each(lambda t, p: t + rcmul(t, p), tinv, pw)

    av = each(lambda x, y, z: rcmul(jnp.concatenate([x, y], axis=0), z), a_ak, a_rk, v)
    ct = each(rcmul, a_rb, tinv)
    wu2 = each(lambda t, c, x, y: _dot(jnp.concatenate([t, c], axis=0).astype(BF16),
                                       jnp.concatenate([bd(x), bd(y[:L])], axis=1)), tinv, ct, a_t, av)
    wu = [x[:L] for x in wu2]
    rbwu = [x[L:] for x in wu2]
    o = each(lambda x, w, s, y: _dot((x + w[:, :GW]).astype(BF16), bd_t(s)) + y[L:] + w[:, GW:],
             r_t, rbwu, s0, av)

    m_bd = each(lambda w, x: _dot_tn(w[:, :GW].astype(BF16), x.astype(BF16)) * bdf, wu, b_h)
    n_full = each(lambda x, w, y, z: _dot_tn(jnp.concatenate([x, w[:, GW:]], axis=0).astype(BF16),
                                             jnp.concatenate([y, z], axis=0).astype(BF16)) * bdf,
                  v, wu, k_h, b_h)
    s1 = []
    for i in range(n):
        n_rc = n_full[i][0:HEAD]
        for h in range(1, HEADS_PER_GROUP):
            n_rc = n_rc + n_full[i][h * HEAD:(h + 1) * HEAD]
        s1.append(_dot(s0[i].astype(BF16), m_bd[i].astype(BF16)) + s0[i] * jnp.exp(tot[i]) + n_rc)
    return s1, o


def _scan_kernel(rf_ref, kf_ref, vf_ref, kkf_ref, af_ref, lwf_ref,
                 rb_ref, kb_ref, vb_ref, kkb_ref, ab_ref, lwb_ref, of_ref, ob_ref, s_ref, *, ng, nb):
    @pl.when(pl.program_id(1) == 0)
    def _():
        s_ref[...] = jnp.zeros_like(s_ref)

    gi = lax.broadcasted_iota(jnp.int32, (GW, GW), 0) // HEAD
    gj = lax.broadcasted_iota(jnp.int32, (GW, GW), 1) // HEAD
    bdm = (gi == gj).astype(F32).astype(BF16)
    dirs = ((rf_ref, kf_ref, vf_ref, kkf_ref, af_ref, lwf_ref), (rb_ref, kb_ref, vb_ref, kkb_ref, ab_ref, lwb_ref))
    chains = [(i, d, h) for i in range(nb) for d in range(2) for h in range(ng)]
    ins = [[dirs[d][q][i, :, h * GW:(h + 1) * GW].astype(F32) for i, d, h in chains] for q in range(6)]
    s1, o = _scan_chunks([s_ref[i, d, h] for i, d, h in chains], *ins, [d == 1 for _, d, _ in chains], bdm)
    for (i, d, h), s_new, o_new in zip(chains, s1, o):
        (of_ref, ob_ref)[d][i, :, h * GW:(h + 1) * GW] = o_new
        s_ref[i, d, h] = s_new


SCAN_BATCH_ROWS = 4


def _wkv_scan(r, v, kk, kf, af, lwf, kb, ab, lwb):
    b, t, c = r.shape
    nc = t // CHUNK
    ng = c // GW
    nb = math.gcd(b, SCAN_BATCH_ROWS)
    fspec = pl.BlockSpec((nb, CHUNK, c), lambda i, j: (i, j, 0))
    bspec = pl.BlockSpec((nb, CHUNK, c), lambda i, j: (i, nc - 1 - j, 0))
    return pl.pallas_call(
        functools.partial(_scan_kernel, ng=ng, nb=nb),
        out_shape=(jax.ShapeDtypeStruct((b, t, c), F32), jax.ShapeDtypeStruct((b, t, c), F32)),
        grid=(b // nb, nc),
        in_specs=[fspec] * 6 + [bspec] * 6,
        out_specs=(fspec, bspec),
        scratch_shapes=[pltpu.VMEM((nb, 2, ng, HEAD, GW), F32)],
        compiler_params=_params(("parallel", "arbitrary")),
        name="rw_scan",
    )(r, kf, v, kk, af, lwf, r, kb, v, kk, ab, lwb)


def _rw_post_kernel(sf_ref, sb_ref, bon_ref, g_ref, lg_ref, lb_ref, avg_ref, o_ref):
    s = sf_ref[...] + sb_ref[...]
    mean = _dot2(s, avg_ref[...])
    d = s - mean
    var = _dot2(d * d, avg_ref[...])
    y = d * lax.rsqrt(var + GN_EPS) * lg_ref[...] + lb_ref[...] + bon_ref[...].astype(F32)
    o_ref[...] = (y * g_ref[...].astype(F32)).astype(o_ref.dtype)


def _rw_post(sf, sb, bon, g, lnx_g, lnx_b, tm=512):
    b, t, c = sf.shape
    m = b * t
    tm = min(tm, m)
    ch = jnp.arange(c) // HEAD
    avg = ((ch[:, None] == ch[None, :]).astype(F32) / HEAD).astype(BF16)
    big = pl.BlockSpec((tm, c), lambda i: (i, 0))
    small = pl.BlockSpec((1, c), lambda i: (0, 0))
    out = pl.pallas_call(
        _rw_post_kernel,
        out_shape=jax.ShapeDtypeStruct((m, c), BF16),
        grid=(m // tm,),
        in_specs=[big, big, big, big, small, small, pl.BlockSpec((c, c), lambda i: (0, 0))],
        out_specs=big,
        compiler_params=_params(("parallel",)),
        name="rw_post",
    )(sf.reshape(m, c), sb.reshape(m, c), bon.reshape(m, c), g.reshape(m, c),
      lnx_g.reshape(1, c), lnx_b.reshape(1, c), avg)
    return out.reshape(b, t, c)


def _rwkv_branch(p, v_first, shift, w0, w2, a0, a2, g2, k_k, k_a, r_k, lnx_g, lnx_b, vres, c):
    r, v, kk, lwf, lwb, kf, kb, af, ab, g, bon = _rw_prep(
        p, v_first, shift, w0, w2, a0, a2, g2, k_k, k_a, r_k, vres, c)
    sf, sb = _wkv_scan(r, v, kk, kf, af, lwf, kb, ab, lwb)
    y = _rw_post(sf, sb, bon, g, lnx_g, lnx_b)
    return y, (v if vres is None else v_first)


def _forward(x, norm1_g, w_in, hy_conv_w, hy_conv_b, hy_f_w1, hy_f_b1, hy_f_w2, hy_f_b2, hy_f_w3, hy_f_b3,
             hy_f_freq, hy_f_wout, hy_skip, hy_norm_g, rw_shift, rw_w0, rw_w2, rw_a0, rw_a2, rw_g2, rw_k_k,
             rw_k_a, rw_r_k, rw_lnx_g, rw_lnx_b, rw_v0, rw_v1, rw_v2, w_out, norm2_g, mlp_w1, mlp_w2,
             final_g):
    b, t, d = x.shape
    depth = w_in.shape[0]
    c_hy = hy_norm_g.shape[1]
    c_rw = rw_lnx_g.shape[1]
    hy_proj = hy_conv_b.shape[1]
    assert b % 2 == 0 and t % CHUNK == 0 and CHUNK == HEAD
    tabs = _dft_tables(t)
    assert tabs["n2"] % SUB == 0 and tabs["n1"] % SUB == 0
    v_first = None
    for l in range(depth):
        w_l = w_in[l].astype(BF16)
        w_rw = w_l[:, hy_proj:]
        vres = None if l == 0 else (rw_v0[l - 1], rw_v1[l - 1], rw_v2[l - 1])
        if vres is not None:
            v1 = vres[1].astype(BF16)
            w_rw = jnp.concatenate([w_rw, jnp.pad(v1, ((0, 0), (0, -v1.shape[1] % 128)))], axis=1)
        p_hy, p_rw = _in_proj(x, norm1_g[l], w_l[:, :hy_proj], w_rw, tabs["n1h"], tabs["n2"])
        y_hy = _hyena_branch(p_hy, hy_conv_w[l], hy_conv_b[l], hy_f_w1[l], hy_f_b1[l], hy_f_w2[l], hy_f_b2[l],
                             hy_f_w3[l], hy_f_b3[l], hy_f_freq[l], hy_f_wout[l], hy_skip[l], hy_norm_g[l], tabs)
        y_rw, v_first = _rwkv_branch(p_rw, v_first, rw_shift[l], rw_w0[l], rw_w2[l], rw_a0[l], rw_a2[l],
                                     rw_g2[l], rw_k_k[l], rw_k_a[l], rw_r_k[l], rw_lnx_g[l], rw_lnx_b[l],
                                     vres, c_rw)
        x = _out_proj(y_hy.reshape(b * t, c_hy), y_rw.reshape(b * t, c_rw), w_out[l].astype(BF16),
                      x.reshape(b * t, d)).reshape(b, t, d)
        x = _mlp(x.reshape(b * t, d), norm2_g[l], mlp_w1[l].astype(BF16), mlp_w2[l].astype(BF16)).reshape(b, t, d)
    return _rmsnorm(x.reshape(b * t, d), final_g, F32).reshape(b, t, d)


def kernel(x, norm1_g, w_in, hy_conv_w, hy_conv_b, hy_f_w1, hy_f_b1, hy_f_w2, hy_f_b2, hy_f_w3, hy_f_b3,
           hy_f_freq, hy_f_wout, hy_skip, hy_norm_g, rw_shift, rw_w0, rw_w2, rw_a0, rw_a2, rw_g2, rw_k_k,
           rw_k_a, rw_r_k, rw_lnx_g, rw_lnx_b, rw_v0, rw_v1, rw_v2, w_out, norm2_g, mlp_w1, mlp_w2, final_g):
    return _forward(x, norm1_g, w_in, hy_conv_w, hy_conv_b, hy_f_w1, hy_f_b1, hy_f_w2, hy_f_b2, hy_f_w3,
                    hy_f_b3, hy_f_freq, hy_f_wout, hy_skip, hy_norm_g, rw_shift, rw_w0, rw_w2, rw_a0, rw_a2,
                    rw_g2, rw_k_k, rw_k_a, rw_r_k, rw_lnx_g, rw_lnx_b, rw_v0, rw_v1, rw_v2, w_out, norm2_g,
                    mlp_w1, mlp_w2, final_g)
```

```python
import functools
import math

import jax
import jax.numpy as jnp
from jax import lax
from jax.experimental import pallas as pl
from jax.experimental.pallas import tpu as pltpu

F32 = jnp.float32
BF16 = jnp.bfloat16
HP = lax.Precision.HIGHEST

HEAD = 64
HEADS_PER_GROUP = 4
GW = HEAD * HEADS_PER_GROUP
CHUNK = 64
RMS_EPS = 1e-5
GN_EPS = HEAD * 1e-5
HY_TARGET = 1e-2
HY_FAST_PCT = 0.3
HY_SLOW_PCT = 1.5
VMEM_LIMIT = 56 * 1024 * 1024


def _params(sem, vmem=VMEM_LIMIT):
    return pltpu.CompilerParams(dimension_semantics=sem, vmem_limit_bytes=vmem)


def _dot(a, b, precision=None):
    return jnp.dot(a, b, preferred_element_type=F32, precision=precision)


def _dot_nt(a, b, precision=None):
    return lax.dot_general(a, b, (((1,), (1,)), ((), ())), preferred_element_type=F32,
                           precision=precision)


def _dot_tn(a, b, precision=None):
    return lax.dot_general(a, b, (((0,), (0,)), ((), ())), preferred_element_type=F32,
                           precision=precision)


def _rmsnorm_kernel(x_ref, g_ref, o_ref):
    x = x_ref[...]
    y = x * lax.rsqrt(jnp.mean(x * x, axis=-1, keepdims=True) + RMS_EPS)
    o_ref[...] = (y * g_ref[...]).astype(o_ref.dtype)


def _rmsnorm(x2d, g, out_dtype, tm=1024):
    m, d = x2d.shape
    tm = min(tm, m)
    return pl.pallas_call(
        _rmsnorm_kernel,
        out_shape=jax.ShapeDtypeStruct((m, d), out_dtype),
        grid=(m // tm,),
        in_specs=[pl.BlockSpec((tm, d), lambda i: (i, 0)), pl.BlockSpec((1, d), lambda i: (0, 0))],
        out_specs=pl.BlockSpec((tm, d), lambda i: (i, 0)),
        compiler_params=_params(("parallel",)),
        name="rmsnorm",
    )(x2d, g.reshape(1, d))


def _const_spec(shape):
    return pl.BlockSpec(shape, lambda *_: (0,) * len(shape), pipeline_mode=pl.Buffered(1))


def _dot2(x, m):
    hi = x.astype(BF16)
    lo = (x - hi.astype(F32)).astype(BF16)
    return _dot(hi, m) + _dot(lo, m)


def _mlp_kernel(x_ref, g_ref, w1_ref, w2_ref, o_ref, *, nchunk):
    x = x_ref[...]
    hn = (x * lax.rsqrt(jnp.mean(x * x, axis=-1, keepdims=True) + RMS_EPS) * g_ref[...]).astype(BF16)
    ff = w1_ref.shape[1]
    cw = ff // nchunk
    acc = x
    for j in range(nchunk):
        h = _dot(hn, w1_ref[:, j * cw:(j + 1) * cw])
        h = jnp.square(jnp.maximum(h, 0.0)).astype(BF16)
        acc = acc + _dot(h, w2_ref[j * cw:(j + 1) * cw, :])
    o_ref[...] = acc


def _mlp(x2d, g, w1, w2, tm=512, nchunk=4):
    m, d = x2d.shape
    ff = w1.shape[1]
    tm = min(tm, m)
    return pl.pallas_call(
        functools.partial(_mlp_kernel, nchunk=nchunk),
        out_shape=jax.ShapeDtypeStruct((m, d), F32),
        grid=(m // tm,),
        in_specs=[pl.BlockSpec((tm, d), lambda i: (i, 0)), _const_spec((1, d)),
                  _const_spec((d, ff)), _const_spec((ff, d))],
        out_specs=pl.BlockSpec((tm, d), lambda i: (i, 0)),
        compiler_params=_params(("parallel",)),
        name="mlp",
    )(x2d, g.reshape(1, d), w1, w2)


SUB = 8
SUB16 = 16


def _rms(x, g):
    return (x * lax.rsqrt(jnp.mean(x * x, axis=-1, keepdims=True) + RMS_EPS) * g).astype(BF16)


def _in_proj_kernel(x_ref, g_ref, wh_ref, wr_ref, oh_ref, or_ref, *, n2):
    hn = _rms(x_ref[0], g_ref[...])
    or_ref[0] = _dot(hn, wr_ref[...]).astype(or_ref.dtype)
    ph = _dot(hn, wh_ref[...])
    for j in range(SUB):
        oh_ref[0, :, j, :] = ph[j * n2:(j + 1) * n2]


def _in_proj(x, g, w_hy, w_rw, n1h, n2):
    b, t, d = x.shape
    ph, pr = w_hy.shape[1], w_rw.shape[1]
    tm = SUB * n2
    p_hy, p_rw = pl.pallas_call(
        functools.partial(_in_proj_kernel, n2=n2),
        out_shape=(jax.ShapeDtypeStruct((b, n2, n1h, ph), F32), jax.ShapeDtypeStruct((b, t, pr), BF16)),
        grid=(b, t // tm),
        in_specs=[pl.BlockSpec((1, tm, d), lambda i, j: (i, j, 0)), _const_spec((1, d)),
                  _const_spec((d, ph)), _const_spec((d, pr))],
        out_specs=(pl.BlockSpec((1, n2, SUB, ph), lambda i, j: (i, 0, j, 0)),
                   pl.BlockSpec((1, tm, pr), lambda i, j: (i, j, 0))),
        compiler_params=_params(("parallel", "parallel")),
        name="in_proj",
    )(x, g.reshape(1, d), w_hy, w_rw)
    return p_hy.reshape(b, t, ph), p_rw


def _out_proj_kernel(yh_ref, yr_ref, wh_ref, wr_ref, r_ref, o_ref):
    o_ref[...] = _dot(yh_ref[...].astype(BF16), wh_ref[...]) + _dot(yr_ref[...], wr_ref[...]) + r_ref[...]


def _out_proj(yh, yr, w, res, tm=1024):
    m, ch = yh.shape
    cr = yr.shape[1]
    d = w.shape[1]
    tm = min(tm, m)
    row = lambda c: pl.BlockSpec((tm, c), lambda i: (i, 0))
    return pl.pallas_call(
        _out_proj_kernel,
        out_shape=jax.ShapeDtypeStruct((m, d), F32),
        grid=(m // tm,),
        in_specs=[row(ch), row(cr), _const_spec((ch, d)), _const_spec((cr, d)), row(d)],
        out_specs=row(d),
        compiler_params=_params(("parallel",)),
        name="out_proj",
    )(yh, yr, w[:ch], w[ch:], res)


def _shortconv_kernel(p_ref, w_ref, b_ref, o_ref, *, n1h, n2):
    w0 = w_ref[0:1, :]
    w1 = w_ref[1:2, :]
    w2 = w_ref[2:3, :]
    bias = b_ref[...]
    rows = lax.broadcasted_iota(jnp.int32, (n1h, 1), 0)

    def blk(i):
        return p_ref[0, pl.ds(pl.multiple_of(i * n1h, n1h), n1h), :].astype(F32)

    def body(i, carry):
        o_ref[0, pl.ds(pl.multiple_of(i * n1h, n1h), n1h), :] = (
            w0 * blk(i - 1) + w1 * blk(i) + w2 * blk(i + 1) + bias).astype(o_ref.dtype)
        return carry

    lax.fori_loop(1, n2 - 1, body, 0)
    last = p_ref[0, (n2 - 1) * n1h:, :].astype(F32)
    first = p_ref[0, :n1h, :].astype(F32)
    prev0 = jnp.where(rows == 0, 0.0, pltpu.roll(last, 1, 0))
    o_ref[0, :n1h, :] = (w0 * prev0 + w1 * first + w2 * p_ref[0, n1h:2 * n1h, :].astype(F32)
                         + bias).astype(o_ref.dtype)
    nxt = jnp.where(rows == n1h - 1, 0.0, pltpu.roll(first, n1h - 1, 0))
    o_ref[0, (n2 - 1) * n1h:, :] = (w0 * p_ref[0, (n2 - 2) * n1h:(n2 - 1) * n1h, :].astype(F32) + w1 * last
                                    + w2 * nxt + bias).astype(o_ref.dtype)


def _shortconv(p, w, bias, n1h, n2, cb=128):
    b, t, c = p.shape
    return pl.pallas_call(
        functools.partial(_shortconv_kernel, n1h=n1h, n2=n2),
        out_shape=jax.ShapeDtypeStruct((b, t, c), BF16),
        grid=(b, c // cb),
        in_specs=[pl.BlockSpec((1, t, cb), lambda i, j: (i, 0, j)),
                  pl.BlockSpec((3, cb), lambda i, j: (0, j)),
                  pl.BlockSpec((1, cb), lambda i, j: (0, j))],
        out_specs=pl.BlockSpec((1, t, cb), lambda i, j: (i, 0, j)),
        compiler_params=_params(("parallel", "parallel")),
        name="hy_shortconv",
    )(p, w, bias.reshape(1, c))


def _dft_tables(t):
    n = 2 * t
    n1 = 1 << ((n.bit_length() - 1) // 2)
    n2 = n // n1
    n1h = n1 // 2
    two_pi = 2.0 * math.pi
    k1 = jnp.arange(n1, dtype=jnp.int32)
    n2i = jnp.arange(n2, dtype=jnp.int32)

    def cs(prod, mod):
        ang = (two_pi / mod) * (prod % mod).astype(F32)
        return jnp.cos(ang), jnp.sin(ang)

    tpos = n2i[:, None] + n2 * jnp.arange(n1h, dtype=jnp.int32)[None, :]
    c, s = cs(k1[None, :, None] * tpos[:, None, :], n)
    g1 = jnp.concatenate([jnp.concatenate([c, s], axis=2), jnp.concatenate([-s, c], axis=2)], axis=1)
    ct, st = jnp.swapaxes(c, 1, 2) / n, jnp.swapaxes(s, 1, 2) / n
    g4a = jnp.concatenate([ct, st], axis=1)
    g4b = jnp.concatenate([-st, ct], axis=1)
    tposf = n2i[:, None] + n2 * jnp.arange(n1, dtype=jnp.int32)[None, :]
    cf, sf = cs(k1[None, :, None] * tposf[:, None, :], n)
    g1f = jnp.concatenate([cf, -sf], axis=1)
    c2, s2 = cs(n2i[:, None] * n2i[None, :], n2)
    f2a = jnp.concatenate([c2, -s2], axis=0)
    f2b = jnp.concatenate([s2, c2], axis=0)
    f3 = jnp.concatenate([jnp.concatenate([c2, -s2], axis=1), jnp.concatenate([s2, c2], axis=1)], axis=0)
    return dict(n1=n1, n2=n2, n1h=n1h, g1=g1.astype(BF16), g4a=g4a.astype(BF16), g4b=g4b.astype(BF16),
                g1f=g1f, f2a=f2a, f2b=f2b, f3=f3.astype(BF16))


def _filter_mlp_kernel(z_ref, w1_ref, b1_ref, w2_ref, b2_ref, w3_ref, b3_ref, fr_ref, wo_ref, dl_ref,
                       o_ref):
    z = z_ref[...]
    h = jnp.sin(fr_ref[0:1, :] * (_dot(z, w1_ref[...], HP) + b1_ref[...]))
    h = jnp.sin(fr_ref[1:2, :] * (_dot(h, w2_ref[...], HP) + b2_ref[...]))
    h = jnp.sin(fr_ref[2:3, :] * (_dot(h, w3_ref[...], HP) + b3_ref[...]))
    window = jnp.exp(-z[:, 0:1] * dl_ref[...])
    o_ref[...] = _dot(h, wo_ref[...], HP) * window


def _filter_mlp(t, w1, b1, w2, b2, w3, b3, freq, w_out, c_hy, tr=512):
    emb, width = w1.shape
    bands = (emb - 1) // 2
    pos = jnp.arange(t, dtype=F32)
    tt = pos / max(t - 1, 1)
    fr = jnp.linspace(1e-4, bands - 1, bands, dtype=F32)
    ang = (2.0 * math.pi / t) * pos[:, None] * fr[None, :]
    z = jnp.concatenate([tt[:, None], jnp.cos(ang), -jnp.sin(ang)], axis=-1)
    z = jnp.pad(z, ((0, 0), (0, 128 - emb)))
    w1 = jnp.pad(w1, ((0, 128 - emb), (0, 0)))
    emb = 128
    max_decay = math.log(HY_TARGET) / HY_FAST_PCT
    min_decay = math.log(HY_TARGET) / HY_SLOW_PCT
    delta = jnp.abs(jnp.linspace(min_decay, max_decay, c_hy, dtype=F32))
    nout = w_out.shape[1]
    delta_full = jnp.tile(delta, nout // c_hy).reshape(1, nout)
    tr = min(tr, t)
    full = lambda a: pl.BlockSpec(a.shape, lambda i: (0,) * a.ndim)
    args = [z, w1, b1.reshape(1, -1), w2, b2.reshape(1, -1), w3, b3.reshape(1, -1), freq, w_out, delta_full]
    return pl.pallas_call(
        _filter_mlp_kernel,
        out_shape=jax.ShapeDtypeStruct((t, nout), F32),
        grid=(t // tr,),
        in_specs=[pl.BlockSpec((tr, emb), lambda i: (i, 0))] + [full(a) for a in args[1:]],
        out_specs=pl.BlockSpec((tr, nout), lambda i: (i, 0)),
        compiler_params=_params(("parallel",)),
        name="hy_filter_mlp",
    )(*args)


def _filter_s1_kernel(x_ref, g_ref, o_ref, ss_ref, *, g, oc):
    @pl.when(pl.program_id(0) == 0)
    def _():
        ss_ref[...] = jnp.zeros_like(ss_ref)

    ss = ss_ref[...]
    for j in range(g):
        x = x_ref[:, j, :]
        o_ref[j] = _dot(g_ref[j], x, HP)
        ss = ss + jnp.sum(x * x, axis=0, keepdims=True)
    ss_ref[...] = ss


def _filter_s2_kernel(sr_ref, si_ref, fa_ref, fb_ref, ss_ref, o_ref, *, g, oc):
    scale = lax.rsqrt(ss_ref[...] + 1e-6)
    for j in range(g):
        x = _dot(fa_ref[...], sr_ref[:, j, :], HP) + _dot(fb_ref[...], si_ref[:, j, :], HP)
        o_ref[j] = x * scale


def _filter_spectrum(two, tabs):
    n, oc = two.shape
    n1, n2 = tabs["n1"], tabs["n2"]
    g = SUB
    s1, ss = pl.pallas_call(
        functools.partial(_filter_s1_kernel, g=g, oc=oc),
        out_shape=(jax.ShapeDtypeStruct((n2, 2 * n1, oc), F32), jax.ShapeDtypeStruct((1, oc), F32)),
        grid=(n2 // g,),
        in_specs=[pl.BlockSpec((n1, g, oc), lambda j: (0, j, 0)),
                  pl.BlockSpec((g, 2 * n1, n1), lambda j: (j, 0, 0))],
        out_specs=(pl.BlockSpec((g, 2 * n1, oc), lambda j: (j, 0, 0)),
                   pl.BlockSpec((1, oc), lambda j: (0, 0))),
        compiler_params=_params(("arbitrary",)),
        name="hy_filter_dft1",
    )(two.reshape(n1, n2, oc), tabs["g1f"])
    return pl.pallas_call(
        functools.partial(_filter_s2_kernel, g=g, oc=oc),
        out_shape=jax.ShapeDtypeStruct((n1, 2 * n2, oc), F32),
        grid=(n1 // g,),
        in_specs=[pl.BlockSpec((n2, g, oc), lambda k: (0, k, 0)),
                  pl.BlockSpec((n2, g, oc), lambda k: (0, n1 // g + k, 0)),
                  pl.BlockSpec((2 * n2, n2), lambda k: (0, 0)),
                  pl.BlockSpec((2 * n2, n2), lambda k: (0, 0)),
                  pl.BlockSpec((1, oc), lambda k: (0, 0))],
        out_specs=pl.BlockSpec((g, 2 * n2, oc), lambda k: (k, 0, 0)),
        compiler_params=_params(("parallel",)),
        name="hy_filter_dft2",
    )(s1, s1, tabs["f2a"], tabs["f2b"], ss)


def _conv_s1_kernel(u_ref, g_ref, o_ref, *, g, n1h):
    for j in range(g):
        rows = slice(j * n1h, (j + 1) * n1h)
        rhs = jnp.concatenate([u_ref[0, 0, rows, :], u_ref[0, 1, rows, :]], axis=0)
        o_ref[0, j] = _dot(g_ref[j], rhs)


def _conv_s23_kernel(sr_ref, si_ref, fa_ref, fb_ref, f3_ref, h_ref, o_ref, *, g, n2):
    for j in range(g):
        x = (_dot(fa_ref[...], sr_ref[0, :, j, :].astype(BF16))
             + _dot(fb_ref[...], si_ref[0, :, j, :].astype(BF16)))
        xr, xi = x[:n2], x[n2:]
        hr, hi = h_ref[j, :n2, :], h_ref[j, n2:, :]
        y = jnp.concatenate([xr * hr - xi * hi, xr * hi + xi * hr], axis=0).astype(BF16)
        o_ref[0, j] = _dot(f3_ref[...], y)


def _conv_s4_kernel(rr_ref, ri_ref, ga_ref, gb_ref, u_ref, gate_ref, skip_ref, *rest, g, n1h, norm):
    if norm:
        ng_ref, avg_ref, o_ref = rest
    else:
        (o_ref,) = rest
    skip = skip_ref[...]
    for j in range(g):
        rows = slice(j * n1h, (j + 1) * n1h)
        y = (_dot(ga_ref[j], rr_ref[0, :, j, :].astype(BF16))
             + _dot(gb_ref[j], ri_ref[0, :, j, :].astype(BF16)))
        for q in range(2):
            z = gate_ref[0, q, rows, :].astype(F32) * (y[q * n1h:(q + 1) * n1h]
                                                       + u_ref[0, q, rows, :].astype(F32) * skip)
            if norm:
                ms = _dot((z * z).astype(BF16), avg_ref[...])
                o_ref[0, q, :, j, :] = z * lax.rsqrt(ms + RMS_EPS) * ng_ref[...]
            else:
                o_ref[0, q, rows, :] = z.astype(o_ref.dtype)


def _long_conv_gate(u_arr, u_col, gate_arr, gate_col, h_spec, h_col, skip, tabs, c, norm_g=None):
    b, t, _ = u_arr.shape
    n1, n2, n1h = tabs["n1"], tabs["n2"], tabs["n1h"]
    g = SUB
    npair = b // 2
    u4 = u_arr.reshape(npair, 2, t, u_arr.shape[2])
    gate4 = gate_arr.reshape(npair, 2, t, gate_arr.shape[2])
    s1 = pl.pallas_call(
        functools.partial(_conv_s1_kernel, g=g, n1h=n1h),
        out_shape=jax.ShapeDtypeStruct((npair, n2, 2 * n1, c), F32),
        grid=(npair, n2 // g),
        in_specs=[pl.BlockSpec((1, 2, g * n1h, c), lambda p, j: (p, 0, j, u_col)),
                  pl.BlockSpec((g, 2 * n1, n1), lambda p, j: (j, 0, 0))],
        out_specs=pl.BlockSpec((1, g, 2 * n1, c), lambda p, j: (p, j, 0, 0)),
        compiler_params=_params(("parallel", "parallel")),
        name="hy_conv_dft1",
    )(u4, tabs["g1"])
    r = pl.pallas_call(
        functools.partial(_conv_s23_kernel, g=g, n2=n2),
        out_shape=jax.ShapeDtypeStruct((npair, n1, 2 * n2, c), F32),
        grid=(npair, n1 // g),
        in_specs=[pl.BlockSpec((1, n2, g, c), lambda p, k: (p, 0, k, 0)),
                  pl.BlockSpec((1, n2, g, c), lambda p, k: (p, 0, n1 // g + k, 0)),
                  pl.BlockSpec((2 * n2, n2), lambda p, k: (0, 0)),
                  pl.BlockSpec((2 * n2, n2), lambda p, k: (0, 0)),
                  pl.BlockSpec((2 * n2, 2 * n2), lambda p, k: (0, 0)),
                  pl.BlockSpec((g, 2 * n2, c), lambda p, k: (k, 0, h_col))],
        out_specs=pl.BlockSpec((1, g, 2 * n2, c), lambda p, k: (p, k, 0, 0)),
        compiler_params=_params(("parallel", "parallel")),
        name="hy_conv_dft23",
    )(s1, s1, tabs["f2a"].astype(BF16), tabs["f2b"].astype(BF16), tabs["f3"], h_spec)
    norm = norm_g is not None
    in_specs = [pl.BlockSpec((1, n1, g, c), lambda p, j: (p, 0, j, 0)),
                pl.BlockSpec((1, n1, g, c), lambda p, j: (p, 0, n2 // g + j, 0)),
                pl.BlockSpec((g, n1, n1), lambda p, j: (j, 0, 0)),
                pl.BlockSpec((g, n1, n1), lambda p, j: (j, 0, 0)),
                pl.BlockSpec((1, 2, g * n1h, c), lambda p, j: (p, 0, j, u_col)),
                pl.BlockSpec((1, 2, g * n1h, c), lambda p, j: (p, 0, j, gate_col)),
                pl.BlockSpec((1, c), lambda p, j: (0, 0))]
    args = [r, r, tabs["g4a"], tabs["g4b"], u4, gate4, skip.reshape(1, c)]
    if norm:
        ch = jnp.arange(c) // HEAD
        avg = ((ch[:, None] == ch[None, :]).astype(F32) / HEAD).astype(BF16)
        in_specs += [pl.BlockSpec((1, c), lambda p, j: (0, 0)), pl.BlockSpec((c, c), lambda p, j: (0, 0))]
        args += [norm_g.reshape(1, c), avg]
        out_shape = jax.ShapeDtypeStruct((npair, 2, n1h, n2, c), F32)
        out_spec = pl.BlockSpec((1, 2, n1h, g, c), lambda p, j: (p, 0, 0, j, 0))
    else:
        out_shape = jax.ShapeDtypeStruct((npair, 2, t, c), BF16)
        out_spec = pl.BlockSpec((1, 2, g * n1h, c), lambda p, j: (p, 0, j, 0))
    out = pl.pallas_call(
        functools.partial(_conv_s4_kernel, g=g, n1h=n1h, norm=norm),
        out_shape=out_shape,
        grid=(npair, n2 // g),
        in_specs=in_specs,
        out_specs=out_spec,
        compiler_params=_params(("parallel", "parallel")),
        name="hy_conv_dft4",
    )(*args)
    return out.reshape(b, t, c)


def _hyena_branch(p, conv_w, conv_b, fw1, fb1, fw2, fb2, fw3, fb3, ffreq, fwout, skip, norm_g, tabs):
    b, t, _ = p.shape
    c = norm_g.shape[0]
    n1, n2, n1h = tabs["n1"], tabs["n2"], tabs["n1h"]
    u = _shortconv(p, conv_w, conv_b, n1h, n2)
    h = _filter_mlp(t, fw1, fb1, fw2, fb2, fw3, fb3, ffreq, fwout, c)
    order = skip.shape[0]
    h = h.reshape(t, order, 2, c)
    fwd, bwd = h[:, :, 0], h[:, :, 1]
    two = jnp.concatenate([fwd, jnp.zeros_like(fwd[:1]), bwd[1:][::-1]], axis=0).reshape(2 * t, order * c)
    hspec = _filter_spectrum(two, tabs)
    z = _long_conv_gate(u, 2, u, 0, hspec, 0, skip[0], tabs, c)
    return _long_conv_gate(z, 0, u, 1, hspec, 1, skip[1], tabs, c, norm_g=norm_g)


def _rw_prep_kernel(*refs, has_vres, c):
    if has_vres:
        (p_ref, pp_ref, pn_ref, mu_ref, w0_ref, w2_ref, a0_ref, a2_ref, g2_ref, kk_ref, ka_ref, rk_ref,
         sum_ref, vf_ref, v0_ref, v2_ref,
         r_o, v_o, kk_o, lwf_o, lwb_o, kf_o, kb_o, af_o, ab_o, g_o, bon_o) = refs
    else:
        (p_ref, pp_ref, pn_ref, mu_ref, w0_ref, w2_ref, a0_ref, a2_ref, g2_ref, kk_ref, ka_ref, rk_ref,
         sum_ref,
         r_o, v_o, kk_o, lwf_o, lwb_o, kf_o, kb_o, af_o, ab_o, g_o, bon_o) = refs
    i = pl.program_id(1)
    last = pl.num_programs(1) - 1
    nsh = mu_ref.shape[1]
    p = p_ref[0, :, :nsh].astype(F32)
    tt = p.shape[0]
    rows = lax.broadcasted_iota(jnp.int32, (tt, 1), 0)
    prev_row = jnp.where(i == 0, 0.0, pp_ref[0, SUB16 - 1:SUB16, :nsh].astype(F32))
    next_row = jnp.where(i == last, 0.0, pn_ref[0, 0:1, :nsh].astype(F32))
    prev = jnp.where(rows == 0, prev_row, pltpu.roll(p, 1, 0))
    nxt = jnp.where(rows == tt - 1, next_row, pltpu.roll(p, tt - 1, 0))
    pf = p + mu_ref[0:1, :] * (prev - p) + mu_ref[1:2, :] * (nxt - p)
    r = pf[:, :c]
    k = pf[:, c:2 * c]
    v = pf[:, 2 * c:3 * c]
    lw = 3 * c
    nd = w2_ref.shape[0]
    na = a2_ref.shape[0]
    wd = pf[:, lw:lw + nd]
    ad = pf[:, lw + nd:lw + nd + na]
    gd = pf[:, lw + nd + na:]
    if has_vres:
        lora = _dot(p_ref[0, :, nsh:], v2_ref[...])
        v = v + (vf_ref[0].astype(F32) - v) * jax.nn.sigmoid(v0_ref[...] + lora)
    g = _dot(jax.nn.sigmoid(gd).astype(BF16), g2_ref[...])
    kk = k * kk_ref[...]
    nrm = jnp.sqrt(_dot((kk * kk).astype(BF16), sum_ref[...]))
    kk = kk / jnp.maximum(nrm, 1e-12)
    wl = w0_ref[...] + _dot(jnp.tanh(wd).astype(BF16), w2_ref[...])
    w = -jax.nn.softplus(-wl) - 0.5
    logw = -jnp.exp(w)
    a = jax.nn.sigmoid(a0_ref[...] + _dot(ad.astype(BF16), a2_ref[...]))
    ka = ka_ref[...]
    rk = rk_ref[...]
    k_d = [k * (1.0 + (a[:, d * c:(d + 1) * c] - 1.0) * ka) for d in range(2)]
    bon = _dot((r * (k_d[0] + k_d[1]) * rk).astype(BF16), sum_ref[...]) * v
    r_o[0] = r.astype(BF16)
    v_o[0] = v.astype(BF16)
    kk_o[0] = kk.astype(BF16)
    lwf_o[0] = logw[:, :c]
    lwb_o[0] = logw[:, c:]
    kf_o[0] = k_d[0].astype(BF16)
    kb_o[0] = k_d[1].astype(BF16)
    af_o[0] = a[:, :c].astype(BF16)
    ab_o[0] = a[:, c:].astype(BF16)
    g_o[0] = g.astype(BF16)
    bon_o[0] = bon.astype(BF16)


def _blockdiag2(m):
    k, c = m.shape[1], m.shape[2]
    z = jnp.zeros((k, c), m.dtype)
    return jnp.concatenate([jnp.concatenate([m[0], z], axis=1), jnp.concatenate([z, m[1]], axis=1)], axis=0)


def _rw_prep(p, v_first, shift, w0, w2, a0, a2, g2, k_k, k_a, r_k, vres, c, tt=256):
    b, t, pw = p.shape
    tt = min(tt, t)
    has_vres = vres is not None
    ch = jnp.arange(c) // HEAD
    summ = (ch[:, None] == ch[None, :]).astype(BF16)
    row = lambda a: a.reshape(1, -1)
    args = [p, p, p, shift, row(w0), _blockdiag2(w2).astype(BF16), row(a0), _blockdiag2(a2).astype(BF16),
            g2.astype(BF16), row(k_k), row(k_a), row(r_k), summ]
    full = lambda a: pl.BlockSpec(a.shape, lambda i, j: (0,) * a.ndim)
    nhb = t // SUB16
    in_specs = [pl.BlockSpec((1, tt, pw), lambda i, j: (i, j, 0)),
                pl.BlockSpec((1, SUB16, pw), lambda i, j: (i, jnp.maximum(j * (tt // SUB16) - 1, 0), 0)),
                pl.BlockSpec((1, SUB16, pw), lambda i, j: (i, jnp.minimum((j + 1) * (tt // SUB16), nhb - 1), 0))]
    in_specs += [full(a) for a in args[3:]]
    if has_vres:
        v0, _, v2 = vres
        v2p = jnp.pad(v2, ((0, pw - shift.shape[1] - v2.shape[0]), (0, 0))).astype(BF16)
        extra = [v_first, row(v0), v2p]
        in_specs += [pl.BlockSpec((1, tt, c), lambda i, j: (i, j, 0))] + [full(a) for a in extra[1:]]
        args += extra
    outs = tuple(jax.ShapeDtypeStruct((b, t, c), F32 if i in (3, 4) else BF16) for i in range(11))
    return pl.pallas_call(
        functools.partial(_rw_prep_kernel, has_vres=has_vres, c=c),
        out_shape=outs,
        grid=(b, t // tt),
        in_specs=in_specs,
        out_specs=tuple(pl.BlockSpec((1, tt, c), lambda i, j: (i, j, 0)) for _ in range(11)),
        compiler_params=_params(("parallel", "parallel")),
        name="rw_prep",
    )(*args)


def _split3(x):
    h1 = x.astype(BF16)
    r1 = x - h1.astype(F32)
    h2 = r1.astype(BF16)
    h3 = (r1 - h2.astype(F32)).astype(BF16)
    return h1, h2, h3


def _scan_chunks(s0, r, k, v, kk, a, lw, rev, bdm):
    L = CHUNK
    n = len(r)
    each = lambda f, *ls: [f(*xs) for xs in zip(*ls)]
    ti = lax.broadcasted_iota(jnp.int32, (L, L), 0)
    si = lax.broadcasted_iota(jnp.int32, (L, L), 1)
    tt = lax.broadcasted_iota(jnp.int32, (L, GW), 0)
    ss = lax.broadcasted_iota(jnp.int32, (L, GW), 1) % L
    eye = (ss == tt).astype(F32)
    tri = [((si >= ti) if q else (si <= ti)).astype(F32).astype(BF16) for q in rev]
    strict = [(ss > tt) if q else (ss < tt) for q in rev]
    incl = [(ss >= tt) if q else (ss <= tt) for q in rev]
    bdf = bdm.astype(F32)

    def bd(x):
        xb = x.astype(BF16)
        return jnp.concatenate([xb] * HEADS_PER_GROUP, axis=0) * bdm

    def bd_t(x):
        xt = x.T.astype(BF16)
        return jnp.concatenate([xt] * HEADS_PER_GROUP, axis=1) * bdm

    def rcmul(x, y):
        return _dot(x.astype(BF16), bd(y))

    parts = each(_split3, lw)
    cum = [_dot(t, p[0]) + _dot(t, p[1]) + _dot(t, p[2]) for t, p in zip(tri, parts)]
    tot = [c[0:1, :] if q else c[L - 1:L, :] for c, q in zip(cum, rev)]
    p_in = each(jnp.exp, cum)
    p_inv = each(lambda c: jnp.exp(-c), cum)
    p_ex = each(lambda c, w: jnp.exp(c - w), cum, lw)
    p_rem = each(lambda t, c: jnp.exp(t - c), tot, cum)
    beta = each(lambda x, y: x * y, kk, a)
    a_t = each(lambda x, p: -x * p, kk, p_ex)
    r_t = each(lambda x, p: x * p, r, p_in)
    k_t = each(lambda x, p: x * p, k, p_inv)
    b_t = each(lambda x, p: x * p, beta, p_inv)
    k_h = each(lambda x, p: x * p, k, p_rem)
    b_h = each(lambda x, p: x * p, beta, p_rem)

    lhs = each(lambda x, y: jnp.concatenate([x, y], axis=0).astype(BF16), a_t, r_t)
    akk = each(lambda x, y: _dot(x, bd_t(y)), lhs, k_t)
    abb = each(lambda x, y: _dot(x, bd_t(y)), lhs, b_t)
    a_ak = each(lambda m, x: jnp.where(m, x[:L], 0.0), strict, akk)
    a_rk = each(lambda m, x: jnp.where(m, x[L:], 0.0), incl, akk)
    a_ab = each(lambda m, x: jnp.where(m, x[:L], 0.0), strict, abb)
    a_rb = each(lambda m, x: jnp.where(m, x[L:], 0.0), incl, abb)

    pw = a_ab
    tinv = each(lambda x: eye + x, pw)
    pw = each(rcmul, pw, pw)
    for _ in range(4):
        both = each(lambda p, t: rcmul(jnp.concatenate([p, t], axis=0), p), pw, tinv)
        pw = [x[:L] for x in both]
        tinv = each(lambda t, x: t + x[L:], tinv, both)
    tinv = each(lambda t, p: t + rcmul(t, p), tinv, pw)

    av = each(lambda x, y, z: rcmul(jnp.concatenate([x, y], axis=0), z), a_ak, a_rk, v)
    ct = each(rcmul, a_rb, tinv)
    wu2 = each(lambda t, c, x, y: _dot(jnp.concatenate([t, c], axis=0).astype(BF16),
                                       jnp.concatenate([bd(x), bd(y[:L])], axis=1)), tinv, ct, a_t, av)
    wu = [x[:L] for x in wu2]
    rbwu = [x[L:] for x in wu2]
    o = each(lambda x, w, s, y: _dot((x + w[:, :GW]).astype(BF16), bd_t(s)) + y[L:] + w[:, GW:],
             r_t, rbwu, s0, av)

    m_bd = each(lambda w, x: _dot_tn(w[:, :GW].astype(BF16), x.astype(BF16)) * bdf, wu, b_h)
    n_full = each(lambda x, w, y, z: _dot_tn(jnp.concatenate([x, w[:, GW:]], axis=0).astype(BF16),
                                             jnp.concatenate([y, z], axis=0).astype(BF16)) * bdf,
                  v, wu, k_h, b_h)
    s1 = []
    for i in range(n):
        n_rc = n_full[i][0:HEAD]
        for h in range(1, HEADS_PER_GROUP):
            n_rc = n_rc + n_full[i][h * HEAD:(h + 1) * HEAD]
        s1.append(_dot(s0[i].astype(BF16), m_bd[i].astype(BF16)) + s0[i] * jnp.exp(tot[i]) + n_rc)
    return s1, o


def _scan_kernel(rf_ref, kf_ref, vf_ref, kkf_ref, af_ref, lwf_ref,
                 rb_ref, kb_ref, vb_ref, kkb_ref, ab_ref, lwb_ref, of_ref, ob_ref, s_ref, *, ng, nb):
    @pl.when(pl.program_id(1) == 0)
    def _():
        s_ref[...] = jnp.zeros_like(s_ref)

    gi = lax.broadcasted_iota(jnp.int32, (GW, GW), 0) // HEAD
    gj = lax.broadcasted_iota(jnp.int32, (GW, GW), 1) // HEAD
    bdm = (gi == gj).astype(F32).astype(BF16)
    dirs = ((rf_ref, kf_ref, vf_ref, kkf_ref, af_ref, lwf_ref), (rb_ref, kb_ref, vb_ref, kkb_ref, ab_ref, lwb_ref))
    chains = [(i, d, h) for i in range(nb) for d in range(2) for h in range(ng)]
    ins = [[dirs[d][q][i, :, h * GW:(h + 1) * GW].astype(F32) for i, d, h in chains] for q in range(6)]
    s1, o = _scan_chunks([s_ref[i, d, h] for i, d, h in chains], *ins, [d == 1 for _, d, _ in chains], bdm)
    for (i, d, h), s_new, o_new in zip(chains, s1, o):
        (of_ref, ob_ref)[d][i, :, h * GW:(h + 1) * GW] = o_new
        s_ref[i, d, h] = s_new


SCAN_BATCH_ROWS = 4


def _wkv_scan(r, v, kk, kf, af, lwf, kb, ab, lwb):
    b, t, c = r.shape
    nc = t // CHUNK
    ng = c // GW
    nb = math.gcd(b, SCAN_BATCH_ROWS)
    fspec = pl.BlockSpec((nb, CHUNK, c), lambda i, j: (i, j, 0))
    bspec = pl.BlockSpec((nb, CHUNK, c), lambda i, j: (i, nc - 1 - j, 0))
    return pl.pallas_call(
        functools.partial(_scan_kernel, ng=ng, nb=nb),
        out_shape=(jax.ShapeDtypeStruct((b, t, c), F32), jax.ShapeDtypeStruct((b, t, c), F32)),
        grid=(b // nb, nc),
        in_specs=[fspec] * 6 + [bspec] * 6,
        out_specs=(fspec, bspec),
        scratch_shapes=[pltpu.VMEM((nb, 2, ng, HEAD, GW), F32)],
        compiler_params=_params(("parallel", "arbitrary")),
        name="rw_scan",
    )(r, kf, v, kk, af, lwf, r, kb, v, kk, ab, lwb)


def _rw_post_kernel(sf_ref, sb_ref, bon_ref, g_ref, lg_ref, lb_ref, avg_ref, o_ref):
    s = sf_ref[...] + sb_ref[...]
    mean = _dot2(s, avg_ref[...])
    d = s - mean
    var = _dot((d * d).astype(BF16), avg_ref[...])
    y = d * lax.rsqrt(var + GN_EPS) * lg_ref[...] + lb_ref[...] + bon_ref[...].astype(F32)
    o_ref[...] = (y * g_ref[...].astype(F32)).astype(o_ref.dtype)


def _rw_post(sf, sb, bon, g, lnx_g, lnx_b, tm=512):
    b, t, c = sf.shape
    m = b * t
    tm = min(tm, m)
    ch = jnp.arange(c) // HEAD
    avg = ((ch[:, None] == ch[None, :]).astype(F32) / HEAD).astype(BF16)
    big = pl.BlockSpec((tm, c), lambda i: (i, 0))
    small = pl.BlockSpec((1, c), lambda i: (0, 0))
    out = pl.pallas_call(
        _rw_post_kernel,
        out_shape=jax.ShapeDtypeStruct((m, c), BF16),
        grid=(m // tm,),
        in_specs=[big, big, big, big, small, small, pl.BlockSpec((c, c), lambda i: (0, 0))],
        out_specs=big,
        compiler_params=_params(("parallel",)),
        name="rw_post",
    )(sf.reshape(m, c), sb.reshape(m, c), bon.reshape(m, c), g.reshape(m, c),
      lnx_g.reshape(1, c), lnx_b.reshape(1, c), avg)
    return out.reshape(b, t, c)


def _rwkv_branch(p, v_first, shift, w0, w2, a0, a2, g2, k_k, k_a, r_k, lnx_g, lnx_b, vres, c):
    r, v, kk, lwf, lwb, kf, kb, af, ab, g, bon = _rw_prep(
        p, v_first, shift, w0, w2, a0, a2, g2, k_k, k_a, r_k, vres, c)
    sf, sb = _wkv_scan(r, v, kk, kf, af, lwf, kb, ab, lwb)
    y = _rw_post(sf, sb, bon, g, lnx_g, lnx_b)
    return y, (v if vres is None else v_first)


def _forward(x, norm1_g, w_in, hy_conv_w, hy_conv_b, hy_f_w1, hy_f_b1, hy_f_w2, hy_f_b2, hy_f_w3, hy_f_b3,
             hy_f_freq, hy_f_wout, hy_skip, hy_norm_g, rw_shift, rw_w0, rw_w2, rw_a0, rw_a2, rw_g2, rw_k_k,
             rw_k_a, rw_r_k, rw_lnx_g, rw_lnx_b, rw_v0, rw_v1, rw_v2, w_out, norm2_g, mlp_w1, mlp_w2,
             final_g):
    b, t, d = x.shape
    depth = w_in.shape[0]
    c_hy = hy_norm_g.shape[1]
    c_rw = rw_lnx_g.shape[1]
    hy_proj = hy_conv_b.shape[1]
    assert b % 2 == 0 and t % CHUNK == 0 and CHUNK == HEAD
    tabs = _dft_tables(t)
    assert tabs["n2"] % SUB == 0 and tabs["n1"] % SUB == 0
    v_first = None
    for l in range(depth):
        w_l = w_in[l].astype(BF16)
        w_rw = w_l[:, hy_proj:]
        vres = None if l == 0 else (rw_v0[l - 1], rw_v1[l - 1], rw_v2[l - 1])
        if vres is not None:
            v1 = vres[1].astype(BF16)
            w_rw = jnp.concatenate([w_rw, jnp.pad(v1, ((0, 0), (0, -v1.shape[1] % 128)))], axis=1)
        p_hy, p_rw = _in_proj(x, norm1_g[l], w_l[:, :hy_proj], w_rw, tabs["n1h"], tabs["n2"])
        y_hy = _hyena_branch(p_hy, hy_conv_w[l], hy_conv_b[l], hy_f_w1[l], hy_f_b1[l], hy_f_w2[l], hy_f_b2[l],
                             hy_f_w3[l], hy_f_b3[l], hy_f_freq[l], hy_f_wout[l], hy_skip[l], hy_norm_g[l], tabs)
        y_rw, v_first = _rwkv_branch(p_rw, v_first, rw_shift[l], rw_w0[l], rw_w2[l], rw_a0[l], rw_a2[l],
                                     rw_g2[l], rw_k_k[l], rw_k_a[l], rw_r_k[l], rw_lnx_g[l], rw_lnx_b[l],
                                     vres, c_rw)
        x = _out_proj(y_hy.reshape(b * t, c_hy), y_rw.reshape(b * t, c_rw), w_out[l].astype(BF16),
                      x.reshape(b * t, d)).reshape(b, t, d)
        x = _mlp(x.reshape(b * t, d), norm2_g[l], mlp_w1[l].astype(BF16), mlp_w2[l].astype(BF16)).reshape(b, t, d)
    return _rmsnorm(x.reshape(b * t, d), final_g, F32).reshape(b, t, d)


def kernel(x, norm1_g, w_in, hy_conv_w, hy_conv_b, hy_f_w1, hy_f_b1, hy_f_w2, hy_f_b2, hy_f_w3, hy_f_b3,
           hy_f_freq, hy_f_wout, hy_skip, hy_norm_g, rw_shift, rw_w0, rw_w2, rw_a0, rw_a2, rw_g2, rw_k_k,
           rw_k_a, rw_r_k, rw_lnx_g, rw_lnx_b, rw_v0, rw_v1, rw_v2, w_out, norm2_g, mlp_w1, mlp_w2, final_g):
    return _forward(x, norm1_g, w_in, hy_conv_w, hy_conv_b, hy_f_w1, hy_f_b1, hy_f_w2, hy_f_b2, hy_f_w3,
                    hy_f_b3, hy_f_freq, hy_f_wout, hy_skip, hy_norm_g, rw_shift, rw_w0, rw_w2, rw_a0, rw_a2,
                    rw_g2, rw_k_k, rw_k_a, rw_r_k, rw_lnx_g, rw_lnx_b, rw_v0, rw_v1, rw_v2, w_out, norm2_g,
                    mlp_w1, mlp_w2, final_g)
```

```python
import functools
import math

import jax
import jax.numpy as jnp
from jax import lax
from jax.experimental import pallas as pl
from jax.experimental.pallas import tpu as pltpu

F32 = jnp.float32
BF16 = jnp.bfloat16
HP = lax.Precision.HIGHEST

HEAD = 64
HEADS_PER_GROUP = 4
GW = HEAD * HEADS_PER_GROUP
CHUNK = 64
RMS_EPS = 1e-5
GN_EPS = HEAD * 1e-5
HY_TARGET = 1e-2
HY_FAST_PCT = 0.3
HY_SLOW_PCT = 1.5
VMEM_LIMIT = 56 * 1024 * 1024


def _params(sem, vmem=VMEM_LIMIT):
    return pltpu.CompilerParams(dimension_semantics=sem, vmem_limit_bytes=vmem)


def _dot(a, b, precision=None):
    return jnp.dot(a, b, preferred_element_type=F32, precision=precision)


def _dot_nt(a, b, precision=None):
    return lax.dot_general(a, b, (((1,), (1,)), ((), ())), preferred_element_type=F32,
                           precision=precision)


def _dot_tn(a, b, precision=None):
    return lax.dot_general(a, b, (((0,), (0,)), ((), ())), preferred_element_type=F32,
                           precision=precision)


def _rmsnorm_kernel(x_ref, g_ref, o_ref):
    x = x_ref[...]
    y = x * lax.rsqrt(jnp.mean(x * x, axis=-1, keepdims=True) + RMS_EPS)
    o_ref[...] = (y * g_ref[...]).astype(o_ref.dtype)


def _rmsnorm(x2d, g, out_dtype, tm=1024):
    m, d = x2d.shape
    tm = min(tm, m)
    return pl.pallas_call(
        _rmsnorm_kernel,
        out_shape=jax.ShapeDtypeStruct((m, d), out_dtype),
        grid=(m // tm,),
        in_specs=[pl.BlockSpec((tm, d), lambda i: (i, 0)), pl.BlockSpec((1, d), lambda i: (0, 0))],
        out_specs=pl.BlockSpec((tm, d), lambda i: (i, 0)),
        compiler_params=_params(("parallel",)),
        name="rmsnorm",
    )(x2d, g.reshape(1, d))


def _const_spec(shape):
    return pl.BlockSpec(shape, lambda *_: (0,) * len(shape), pipeline_mode=pl.Buffered(1))


def _dot2(x, m):
    hi = x.astype(BF16)
    lo = (x - hi.astype(F32)).astype(BF16)
    return _dot(hi, m) + _dot(lo, m)


def _mlp_kernel(x_ref, g_ref, w1_ref, w2_ref, o_ref, *, nchunk):
    x = x_ref[...]
    hn = (x * lax.rsqrt(jnp.mean(x * x, axis=-1, keepdims=True) + RMS_EPS) * g_ref[...]).astype(BF16)
    ff = w1_ref.shape[1]
    cw = ff // nchunk
    acc = x
    for j in range(nchunk):
        h = _dot(hn, w1_ref[:, j * cw:(j + 1) * cw])
        h = jnp.square(jnp.maximum(h, 0.0)).astype(BF16)
        acc = acc + _dot(h, w2_ref[j * cw:(j + 1) * cw, :])
    o_ref[...] = acc


def _mlp(x2d, g, w1, w2, tm=512, nchunk=4):
    m, d = x2d.shape
    ff = w1.shape[1]
    tm = min(tm, m)
    return pl.pallas_call(
        functools.partial(_mlp_kernel, nchunk=nchunk),
        out_shape=jax.ShapeDtypeStruct((m, d), F32),
        grid=(m // tm,),
        in_specs=[pl.BlockSpec((tm, d), lambda i: (i, 0)), _const_spec((1, d)),
                  _const_spec((d, ff)), _const_spec((ff, d))],
        out_specs=pl.BlockSpec((tm, d), lambda i: (i, 0)),
        compiler_params=_params(("parallel",)),
        name="mlp",
    )(x2d, g.reshape(1, d), w1, w2)


SUB = 8
SUB16 = 16


def _rms(x, g):
    return (x * lax.rsqrt(jnp.mean(x * x, axis=-1, keepdims=True) + RMS_EPS) * g).astype(BF16)


def _in_proj_kernel(x_ref, g_ref, wh_ref, wr_ref, oh_ref, or_ref, *, n2):
    hn = _rms(x_ref[0], g_ref[...])
    or_ref[0] = _dot(hn, wr_ref[...]).astype(or_ref.dtype)
    ph = _dot(hn, wh_ref[...])
    for j in range(SUB):
        oh_ref[0, :, j, :] = ph[j * n2:(j + 1) * n2]


def _in_proj(x, g, w_hy, w_rw, n1h, n2):
    b, t, d = x.shape
    ph, pr = w_hy.shape[1], w_rw.shape[1]
    tm = SUB * n2
    p_hy, p_rw = pl.pallas_call(
        functools.partial(_in_proj_kernel, n2=n2),
        out_shape=(jax.ShapeDtypeStruct((b, n2, n1h, ph), F32), jax.ShapeDtypeStruct((b, t, pr), BF16)),
        grid=(b, t // tm),
        in_specs=[pl.BlockSpec((1, tm, d), lambda i, j: (i, j, 0)), _const_spec((1, d)),
                  _const_spec((d, ph)), _const_spec((d, pr))],
        out_specs=(pl.BlockSpec((1, n2, SUB, ph), lambda i, j: (i, 0, j, 0)),
                   pl.BlockSpec((1, tm, pr), lambda i, j: (i, j, 0))),
        compiler_params=_params(("parallel", "parallel")),
        name="in_proj",
    )(x, g.reshape(1, d), w_hy, w_rw)
    return p_hy.reshape(b, t, ph), p_rw


def _out_proj_kernel(yh_ref, yr_ref, wh_ref, wr_ref, r_ref, o_ref):
    o_ref[...] = _dot(yh_ref[...].astype(BF16), wh_ref[...]) + _dot(yr_ref[...], wr_ref[...]) + r_ref[...]


def _out_proj(yh, yr, w, res, tm=1024):
    m, ch = yh.shape
    cr = yr.shape[1]
    d = w.shape[1]
    tm = min(tm, m)
    row = lambda c: pl.BlockSpec((tm, c), lambda i: (i, 0))
    return pl.pallas_call(
        _out_proj_kernel,
        out_shape=jax.ShapeDtypeStruct((m, d), F32),
        grid=(m // tm,),
        in_specs=[row(ch), row(cr), _const_spec((ch, d)), _const_spec((cr, d)), row(d)],
        out_specs=row(d),
        compiler_params=_params(("parallel",)),
        name="out_proj",
    )(yh, yr, w[:ch], w[ch:], res)


def _shortconv_kernel(p_ref, w_ref, b_ref, o_ref, *, n1h, n2):
    w0 = w_ref[0:1, :]
    w1 = w_ref[1:2, :]
    w2 = w_ref[2:3, :]
    bias = b_ref[...]
    rows = lax.broadcasted_iota(jnp.int32, (n1h, 1), 0)

    def blk(i):
        return p_ref[0, pl.ds(pl.multiple_of(i * n1h, n1h), n1h), :].astype(F32)

    def body(i, carry):
        o_ref[0, pl.ds(pl.multiple_of(i * n1h, n1h), n1h), :] = (
            w0 * blk(i - 1) + w1 * blk(i) + w2 * blk(i + 1) + bias).astype(o_ref.dtype)
        return carry

    lax.fori_loop(1, n2 - 1, body, 0)
    last = p_ref[0, (n2 - 1) * n1h:, :].astype(F32)
    first = p_ref[0, :n1h, :].astype(F32)
    prev0 = jnp.where(rows == 0, 0.0, pltpu.roll(last, 1, 0))
    o_ref[0, :n1h, :] = (w0 * prev0 + w1 * first + w2 * p_ref[0, n1h:2 * n1h, :].astype(F32)
                         + bias).astype(o_ref.dtype)
    nxt = jnp.where(rows == n1h - 1, 0.0, pltpu.roll(first, n1h - 1, 0))
    o_ref[0, (n2 - 1) * n1h:, :] = (w0 * p_ref[0, (n2 - 2) * n1h:(n2 - 1) * n1h, :].astype(F32) + w1 * last
                                    + w2 * nxt + bias).astype(o_ref.dtype)


def _shortconv(p, w, bias, n1h, n2, cb=128):
    b, t, c = p.shape
    return pl.pallas_call(
        functools.partial(_shortconv_kernel, n1h=n1h, n2=n2),
        out_shape=jax.ShapeDtypeStruct((b, t, c), BF16),
        grid=(b, c // cb),
        in_specs=[pl.BlockSpec((1, t, cb), lambda i, j: (i, 0, j)),
                  pl.BlockSpec((3, cb), lambda i, j: (0, j)),
                  pl.BlockSpec((1, cb), lambda i, j: (0, j))],
        out_specs=pl.BlockSpec((1, t, cb), lambda i, j: (i, 0, j)),
        compiler_params=_params(("parallel", "parallel")),
        name="hy_shortconv",
    )(p, w, bias.reshape(1, c))


def _dft_tables(t):
    n = 2 * t
    n1 = 1 << ((n.bit_length() - 1) // 2)
    n2 = n // n1
    n1h = n1 // 2
    two_pi = 2.0 * math.pi
    k1 = jnp.arange(n1, dtype=jnp.int32)
    n2i = jnp.arange(n2, dtype=jnp.int32)

    def cs(prod, mod):
        ang = (two_pi / mod) * (prod % mod).astype(F32)
        return jnp.cos(ang), jnp.sin(ang)

    tpos = n2i[:, None] + n2 * jnp.arange(n1h, dtype=jnp.int32)[None, :]
    c, s = cs(k1[None, :, None] * tpos[:, None, :], n)
    g1 = jnp.concatenate([jnp.concatenate([c, s], axis=2), jnp.concatenate([-s, c], axis=2)], axis=1)
    ct, st = jnp.swapaxes(c, 1, 2) / n, jnp.swapaxes(s, 1, 2) / n
    g4a = jnp.concatenate([ct, st], axis=1)
    g4b = jnp.concatenate([-st, ct], axis=1)
    tposf = n2i[:, None] + n2 * jnp.arange(n1, dtype=jnp.int32)[None, :]
    cf, sf = cs(k1[None, :, None] * tposf[:, None, :], n)
    g1f = jnp.concatenate([cf, -sf], axis=1)
    c2, s2 = cs(n2i[:, None] * n2i[None, :], n2)
    f2a = jnp.concatenate([c2, -s2], axis=0)
    f2b = jnp.concatenate([s2, c2], axis=0)
    f3 = jnp.concatenate([jnp.concatenate([c2, -s2], axis=1), jnp.concatenate([s2, c2], axis=1)], axis=0)
    return dict(n1=n1, n2=n2, n1h=n1h, g1=g1.astype(BF16), g4a=g4a.astype(BF16), g4b=g4b.astype(BF16),
                g1f=g1f, f2a=f2a, f2b=f2b, f3=f3.astype(BF16))


def _filter_mlp_kernel(z_ref, w1_ref, b1_ref, w2_ref, b2_ref, w3_ref, b3_ref, fr_ref, wo_ref, dl_ref,
                       o_ref):
    z = z_ref[...]
    h = jnp.sin(fr_ref[0:1, :] * (_dot(z, w1_ref[...], HP) + b1_ref[...]))
    h = jnp.sin(fr_ref[1:2, :] * (_dot(h, w2_ref[...], HP) + b2_ref[...]))
    h = jnp.sin(fr_ref[2:3, :] * (_dot(h, w3_ref[...], HP) + b3_ref[...]))
    window = jnp.exp(-z[:, 0:1] * dl_ref[...])
    o_ref[...] = _dot(h, wo_ref[...], HP) * window


def _filter_mlp(t, w1, b1, w2, b2, w3, b3, freq, w_out, c_hy, tr=512):
    emb, width = w1.shape
    bands = (emb - 1) // 2
    pos = jnp.arange(t, dtype=F32)
    tt = pos / max(t - 1, 1)
    fr = jnp.linspace(1e-4, bands - 1, bands, dtype=F32)
    ang = (2.0 * math.pi / t) * pos[:, None] * fr[None, :]
    z = jnp.concatenate([tt[:, None], jnp.cos(ang), -jnp.sin(ang)], axis=-1)
    z = jnp.pad(z, ((0, 0), (0, 128 - emb)))
    w1 = jnp.pad(w1, ((0, 128 - emb), (0, 0)))
    emb = 128
    max_decay = math.log(HY_TARGET) / HY_FAST_PCT
    min_decay = math.log(HY_TARGET) / HY_SLOW_PCT
    delta = jnp.abs(jnp.linspace(min_decay, max_decay, c_hy, dtype=F32))
    nout = w_out.shape[1]
    delta_full = jnp.tile(delta, nout // c_hy).reshape(1, nout)
    tr = min(tr, t)
    full = lambda a: pl.BlockSpec(a.shape, lambda i: (0,) * a.ndim)
    args = [z, w1, b1.reshape(1, -1), w2, b2.reshape(1, -1), w3, b3.reshape(1, -1), freq, w_out, delta_full]
    return pl.pallas_call(
        _filter_mlp_kernel,
        out_shape=jax.ShapeDtypeStruct((t, nout), F32),
        grid=(t // tr,),
        in_specs=[pl.BlockSpec((tr, emb), lambda i: (i, 0))] + [full(a) for a in args[1:]],
        out_specs=pl.BlockSpec((tr, nout), lambda i: (i, 0)),
        compiler_params=_params(("parallel",)),
        name="hy_filter_mlp",
    )(*args)


def _filter_s1_kernel(x_ref, g_ref, o_ref, ss_ref, *, g, oc):
    @pl.when(pl.program_id(0) == 0)
    def _():
        ss_ref[...] = jnp.zeros_like(ss_ref)

    ss = ss_ref[...]
    for j in range(g):
        x = x_ref[:, j, :]
        o_ref[j] = _dot(g_ref[j], x, HP)
        ss = ss + jnp.sum(x * x, axis=0, keepdims=True)
    ss_ref[...] = ss


def _filter_s2_kernel(sr_ref, si_ref, fa_ref, fb_ref, ss_ref, o_ref, *, g, oc):
    scale = lax.rsqrt(ss_ref[...] + 1e-6)
    for j in range(g):
        x = _dot(fa_ref[...], sr_ref[:, j, :], HP) + _dot(fb_ref[...], si_ref[:, j, :], HP)
        o_ref[j] = x * scale


def _filter_spectrum(two, tabs):
    n, oc = two.shape
    n1, n2 = tabs["n1"], tabs["n2"]
    g = SUB
    s1, ss = pl.pallas_call(
        functools.partial(_filter_s1_kernel, g=g, oc=oc),
        out_shape=(jax.ShapeDtypeStruct((n2, 2 * n1, oc), F32), jax.ShapeDtypeStruct((1, oc), F32)),
        grid=(n2 // g,),
        in_specs=[pl.BlockSpec((n1, g, oc), lambda j: (0, j, 0)),
                  pl.BlockSpec((g, 2 * n1, n1), lambda j: (j, 0, 0))],
        out_specs=(pl.BlockSpec((g, 2 * n1, oc), lambda j: (j, 0, 0)),
                   pl.BlockSpec((1, oc), lambda j: (0, 0))),
        compiler_params=_params(("arbitrary",)),
        name="hy_filter_dft1",
    )(two.reshape(n1, n2, oc), tabs["g1f"])
    return pl.pallas_call(
        functools.partial(_filter_s2_kernel, g=g, oc=oc),
        out_shape=jax.ShapeDtypeStruct((n1, 2 * n2, oc), F32),
        grid=(n1 // g,),
        in_specs=[pl.BlockSpec((n2, g, oc), lambda k: (0, k, 0)),
                  pl.BlockSpec((n2, g, oc), lambda k: (0, n1 // g + k, 0)),
                  pl.BlockSpec((2 * n2, n2), lambda k: (0, 0)),
                  pl.BlockSpec((2 * n2, n2), lambda k: (0, 0)),
                  pl.BlockSpec((1, oc), lambda k: (0, 0))],
        out_specs=pl.BlockSpec((g, 2 * n2, oc), lambda k: (k, 0, 0)),
        compiler_params=_params(("parallel",)),
        name="hy_filter_dft2",
    )(s1, s1, tabs["f2a"], tabs["f2b"], ss)


def _conv_s1_kernel(u_ref, g_ref, o_ref, *, g, n1h):
    for j in range(g):
        rows = slice(j * n1h, (j + 1) * n1h)
        rhs = jnp.concatenate([u_ref[0, 0, rows, :], u_ref[0, 1, rows, :]], axis=0)
        o_ref[0, j] = _dot(g_ref[j], rhs)


def _conv_s23_kernel(sr_ref, si_ref, fa_ref, fb_ref, f3_ref, h_ref, o_ref, *, g, n2):
    sr = jnp.concatenate([sr_ref[0, :, j, :] for j in range(g)], axis=0).astype(BF16)
    si = jnp.concatenate([si_ref[0, :, j, :] for j in range(g)], axis=0).astype(BF16)
    for j in range(g):
        x = _dot(fa_ref[...], sr[j * n2:(j + 1) * n2]) + _dot(fb_ref[...], si[j * n2:(j + 1) * n2])
        xr, xi = x[:n2], x[n2:]
        hr, hi = h_ref[j, :n2, :], h_ref[j, n2:, :]
        y = jnp.concatenate([xr * hr - xi * hi, xr * hi + xi * hr], axis=0).astype(BF16)
        o_ref[0, j] = _dot(f3_ref[...], y)


def _conv_s4_kernel(rr_ref, ri_ref, ga_ref, gb_ref, u_ref, gate_ref, skip_ref, *rest, g, n1h, norm):
    if norm:
        ng_ref, avg_ref, o_ref = rest
    else:
        (o_ref,) = rest
    skip = skip_ref[...]
    n1 = rr_ref.shape[1]
    rr = jnp.concatenate([rr_ref[0, :, j, :] for j in range(g)], axis=0).astype(BF16)
    ri = jnp.concatenate([ri_ref[0, :, j, :] for j in range(g)], axis=0).astype(BF16)
    for j in range(g):
        rows = slice(j * n1h, (j + 1) * n1h)
        y = _dot(ga_ref[j], rr[j * n1:(j + 1) * n1]) + _dot(gb_ref[j], ri[j * n1:(j + 1) * n1])
        for q in range(2):
            z = gate_ref[0, q, rows, :].astype(F32) * (y[q * n1h:(q + 1) * n1h]
                                                       + u_ref[0, q, rows, :].astype(F32) * skip)
            if norm:
                ms = _dot((z * z).astype(BF16), avg_ref[...])
                o_ref[0, q, :, j, :] = z * lax.rsqrt(ms + RMS_EPS) * ng_ref[...]
            else:
                o_ref[0, q, rows, :] = z.astype(o_ref.dtype)


def _long_conv_gate(u_arr, u_col, gate_arr, gate_col, h_spec, h_col, skip, tabs, c, norm_g=None):
    b, t, _ = u_arr.shape
    n1, n2, n1h = tabs["n1"], tabs["n2"], tabs["n1h"]
    g = SUB
    npair = b // 2
    u4 = u_arr.reshape(npair, 2, t, u_arr.shape[2])
    gate4 = gate_arr.reshape(npair, 2, t, gate_arr.shape[2])
    s1 = pl.pallas_call(
        functools.partial(_conv_s1_kernel, g=g, n1h=n1h),
        out_shape=jax.ShapeDtypeStruct((npair, n2, 2 * n1, c), F32),
        grid=(npair, n2 // g),
        in_specs=[pl.BlockSpec((1, 2, g * n1h, c), lambda p, j: (p, 0, j, u_col)),
                  pl.BlockSpec((g, 2 * n1, n1), lambda p, j: (j, 0, 0))],
        out_specs=pl.BlockSpec((1, g, 2 * n1, c), lambda p, j: (p, j, 0, 0)),
        compiler_params=_params(("parallel", "parallel")),
        name="hy_conv_dft1",
    )(u4, tabs["g1"])
    r = pl.pallas_call(
        functools.partial(_conv_s23_kernel, g=g, n2=n2),
        out_shape=jax.ShapeDtypeStruct((npair, n1, 2 * n2, c), F32),
        grid=(npair, n1 // g),
        in_specs=[pl.BlockSpec((1, n2, g, c), lambda p, k: (p, 0, k, 0)),
                  pl.BlockSpec((1, n2, g, c), lambda p, k: (p, 0, n1 // g + k, 0)),
                  pl.BlockSpec((2 * n2, n2), lambda p, k: (0, 0)),
                  pl.BlockSpec((2 * n2, n2), lambda p, k: (0, 0)),
                  pl.BlockSpec((2 * n2, 2 * n2), lambda p, k: (0, 0)),
                  pl.BlockSpec((g, 2 * n2, c), lambda p, k: (k, 0, h_col))],
        out_specs=pl.BlockSpec((1, g, 2 * n2, c), lambda p, k: (p, k, 0, 0)),
        compiler_params=_params(("parallel", "parallel")),
        name="hy_conv_dft23",
    )(s1, s1, tabs["f2a"].astype(BF16), tabs["f2b"].astype(BF16), tabs["f3"], h_spec)
    norm = norm_g is not None
    in_specs = [pl.BlockSpec((1, n1, g, c), lambda p, j: (p, 0, j, 0)),
                pl.BlockSpec((1, n1, g, c), lambda p, j: (p, 0, n2 // g + j, 0)),
                pl.BlockSpec((g, n1, n1), lambda p, j: (j, 0, 0)),
                pl.BlockSpec((g, n1, n1), lambda p, j: (j, 0, 0)),
                pl.BlockSpec((1, 2, g * n1h, c), lambda p, j: (p, 0, j, u_col)),
                pl.BlockSpec((1, 2, g * n1h, c), lambda p, j: (p, 0, j, gate_col)),
                pl.BlockSpec((1, c), lambda p, j: (0, 0))]
    args = [r, r, tabs["g4a"], tabs["g4b"], u4, gate4, skip.reshape(1, c)]
    if norm:
        ch = jnp.arange(c) // HEAD
        avg = ((ch[:, None] == ch[None, :]).astype(F32) / HEAD).astype(BF16)
        in_specs += [pl.BlockSpec((1, c), lambda p, j: (0, 0)), pl.BlockSpec((c, c), lambda p, j: (0, 0))]
        args += [norm_g.reshape(1, c), avg]
        out_shape = jax.ShapeDtypeStruct((npair, 2, n1h, n2, c), F32)
        out_spec = pl.BlockSpec((1, 2, n1h, g, c), lambda p, j: (p, 0, 0, j, 0))
    else:
        out_shape = jax.ShapeDtypeStruct((npair, 2, t, c), BF16)
        out_spec = pl.BlockSpec((1, 2, g * n1h, c), lambda p, j: (p, 0, j, 0))
    out = pl.pallas_call(
        functools.partial(_conv_s4_kernel, g=g, n1h=n1h, norm=norm),
        out_shape=out_shape,
        grid=(npair, n2 // g),
        in_specs=in_specs,
        out_specs=out_spec,
        compiler_params=_params(("parallel", "parallel")),
        name="hy_conv_dft4",
    )(*args)
    return out.reshape(b, t, c)


def _hyena_branch(p, conv_w, conv_b, fw1, fb1, fw2, fb2, fw3, fb3, ffreq, fwout, skip, norm_g, tabs):
    b, t, _ = p.shape
    c = norm_g.shape[0]
    n1, n2, n1h = tabs["n1"], tabs["n2"], tabs["n1h"]
    u = _shortconv(p, conv_w, conv_b, n1h, n2)
    h = _filter_mlp(t, fw1, fb1, fw2, fb2, fw3, fb3, ffreq, fwout, c)
    order = skip.shape[0]
    h = h.reshape(t, order, 2, c)
    fwd, bwd = h[:, :, 0], h[:, :, 1]
    two = jnp.concatenate([fwd, jnp.zeros_like(fwd[:1]), bwd[1:][::-1]], axis=0).reshape(2 * t, order * c)
    hspec = _filter_spectrum(two, tabs)
    z = _long_conv_gate(u, 2, u, 0, hspec, 0, skip[0], tabs, c)
    return _long_conv_gate(z, 0, u, 1, hspec, 1, skip[1], tabs, c, norm_g=norm_g)


def _rw_prep_kernel(*refs, has_vres, c):
    if has_vres:
        (p_ref, pp_ref, pn_ref, mu_ref, w0_ref, w2_ref, a0_ref, a2_ref, g2_ref, kk_ref, ka_ref, rk_ref,
         sum_ref, vf_ref, v0_ref, v2_ref,
         r_o, v_o, kk_o, lwf_o, lwb_o, kf_o, kb_o, af_o, ab_o, g_o, bon_o) = refs
    else:
        (p_ref, pp_ref, pn_ref, mu_ref, w0_ref, w2_ref, a0_ref, a2_ref, g2_ref, kk_ref, ka_ref, rk_ref,
         sum_ref,
         r_o, v_o, kk_o, lwf_o, lwb_o, kf_o, kb_o, af_o, ab_o, g_o, bon_o) = refs
    i = pl.program_id(1)
    last = pl.num_programs(1) - 1
    nsh = mu_ref.shape[1]
    p = p_ref[0, :, :nsh].astype(F32)
    tt = p.shape[0]
    rows = lax.broadcasted_iota(jnp.int32, (tt, 1), 0)
    prev_row = jnp.where(i == 0, 0.0, pp_ref[0, SUB16 - 1:SUB16, :nsh].astype(F32))
    next_row = jnp.where(i == last, 0.0, pn_ref[0, 0:1, :nsh].astype(F32))
    prev = jnp.where(rows == 0, prev_row, pltpu.roll(p, 1, 0))
    nxt = jnp.where(rows == tt - 1, next_row, pltpu.roll(p, tt - 1, 0))
    pf = p + mu_ref[0:1, :] * (prev - p) + mu_ref[1:2, :] * (nxt - p)
    r = pf[:, :c]
    k = pf[:, c:2 * c]
    v = pf[:, 2 * c:3 * c]
    lw = 3 * c
    nd = w2_ref.shape[0]
    na = a2_ref.shape[0]
    wd = pf[:, lw:lw + nd]
    ad = pf[:, lw + nd:lw + nd + na]
    gd = pf[:, lw + nd + na:]
    if has_vres:
        lora = _dot(p_ref[0, :, nsh:], v2_ref[...])
        v = v + (vf_ref[0].astype(F32) - v) * jax.nn.sigmoid(v0_ref[...] + lora)
    g = _dot(jax.nn.sigmoid(gd).astype(BF16), g2_ref[...])
    kk = k * kk_ref[...]
    nrm = jnp.sqrt(_dot((kk * kk).astype(BF16), sum_ref[...]))
    kk = kk / jnp.maximum(nrm, 1e-12)
    wl = w0_ref[...] + _dot(jnp.tanh(wd).astype(BF16), w2_ref[...])
    w = -jax.nn.softplus(-wl) - 0.5
    logw = -jnp.exp(w)
    a = jax.nn.sigmoid(a0_ref[...] + _dot(ad.astype(BF16), a2_ref[...]))
    ka = ka_ref[...]
    rk = rk_ref[...]
    k_d = [k * (1.0 + (a[:, d * c:(d + 1) * c] - 1.0) * ka) for d in range(2)]
    bon = _dot((r * (k_d[0] + k_d[1]) * rk).astype(BF16), sum_ref[...]) * v
    r_o[0] = r.astype(BF16)
    v_o[0] = v.astype(BF16)
    kk_o[0] = kk.astype(BF16)
    lwf_o[0] = logw[:, :c]
    lwb_o[0] = logw[:, c:]
    kf_o[0] = k_d[0].astype(BF16)
    kb_o[0] = k_d[1].astype(BF16)
    af_o[0] = a[:, :c].astype(BF16)
    ab_o[0] = a[:, c:].astype(BF16)
    g_o[0] = g.astype(BF16)
    bon_o[0] = bon.astype(BF16)


def _blockdiag2(m):
    k, c = m.shape[1], m.shape[2]
    z = jnp.zeros((k, c), m.dtype)
    return jnp.concatenate([jnp.concatenate([m[0], z], axis=1), jnp.concatenate([z, m[1]], axis=1)], axis=0)


def _rw_prep(p, v_first, shift, w0, w2, a0, a2, g2, k_k, k_a, r_k, vres, c, tt=256):
    b, t, pw = p.shape
    tt = min(tt, t)
    has_vres = vres is not None
    ch = jnp.arange(c) // HEAD
    summ = (ch[:, None] == ch[None, :]).astype(BF16)
    row = lambda a: a.reshape(1, -1)
    args = [p, p, p, shift, row(w0), _blockdiag2(w2).astype(BF16), row(a0), _blockdiag2(a2).astype(BF16),
            g2.astype(BF16), row(k_k), row(k_a), row(r_k), summ]
    full = lambda a: pl.BlockSpec(a.shape, lambda i, j: (0,) * a.ndim)
    nhb = t // SUB16
    in_specs = [pl.BlockSpec((1, tt, pw), lambda i, j: (i, j, 0)),
                pl.BlockSpec((1, SUB16, pw), lambda i, j: (i, jnp.maximum(j * (tt // SUB16) - 1, 0), 0)),
                pl.BlockSpec((1, SUB16, pw), lambda i, j: (i, jnp.minimum((j + 1) * (tt // SUB16), nhb - 1), 0))]
    in_specs += [full(a) for a in args[3:]]
    if has_vres:
        v0, _, v2 = vres
        v2p = jnp.pad(v2, ((0, pw - shift.shape[1] - v2.shape[0]), (0, 0))).astype(BF16)
        extra = [v_first, row(v0), v2p]
        in_specs += [pl.BlockSpec((1, tt, c), lambda i, j: (i, j, 0))] + [full(a) for a in extra[1:]]
        args += extra
    outs = tuple(jax.ShapeDtypeStruct((b, t, c), F32 if i in (3, 4) else BF16) for i in range(11))
    return pl.pallas_call(
        functools.partial(_rw_prep_kernel, has_vres=has_vres, c=c),
        out_shape=outs,
        grid=(b, t // tt),
        in_specs=in_specs,
        out_specs=tuple(pl.BlockSpec((1, tt, c), lambda i, j: (i, j, 0)) for _ in range(11)),
        compiler_params=_params(("parallel", "parallel")),
        name="rw_prep",
    )(*args)


def _split3(x):
    h1 = x.astype(BF16)
    r1 = x - h1.astype(F32)
    h2 = r1.astype(BF16)
    h3 = (r1 - h2.astype(F32)).astype(BF16)
    return h1, h2, h3


def _scan_chunks(s0, r, k, v, kk, a, lw, rev, bdm):
    L = CHUNK
    n = len(r)
    each = lambda f, *ls: [f(*xs) for xs in zip(*ls)]
    ti = lax.broadcasted_iota(jnp.int32, (L, L), 0)
    si = lax.broadcasted_iota(jnp.int32, (L, L), 1)
    tt = lax.broadcasted_iota(jnp.int32, (L, GW), 0)
    ss = lax.broadcasted_iota(jnp.int32, (L, GW), 1) % L
    eye = (ss == tt).astype(F32)
    tri = [((si >= ti) if q else (si <= ti)).astype(F32).astype(BF16) for q in rev]
    strict = [(ss > tt) if q else (ss < tt) for q in rev]
    incl = [(ss >= tt) if q else (ss <= tt) for q in rev]
    bdf = bdm.astype(F32)

    def bd(x):
        xb = x.astype(BF16)
        return jnp.concatenate([xb] * HEADS_PER_GROUP, axis=0) * bdm

    def bd_t(x):
        xt = x.T.astype(BF16)
        return jnp.concatenate([xt] * HEADS_PER_GROUP, axis=1) * bdm

    def rcmul(x, y):
        return _dot(x.astype(BF16), bd(y))

    parts = each(_split3, lw)
    cum = [_dot(t, p[0]) + _dot(t, p[1]) + _dot(t, p[2]) for t, p in zip(tri, parts)]
    tot = [c[0:1, :] if q else c[L - 1:L, :] for c, q in zip(cum, rev)]
    p_in = each(jnp.exp, cum)
    p_inv = each(lambda c: jnp.exp(-c), cum)
    p_ex = each(lambda c, w: jnp.exp(c - w), cum, lw)
    p_rem = each(lambda t, c: jnp.exp(t - c), tot, cum)
    beta = each(lambda x, y: x * y, kk, a)
    a_t = each(lambda x, p: -x * p, kk, p_ex)
    r_t = each(lambda x, p: x * p, r, p_in)
    k_t = each(lambda x, p: x * p, k, p_inv)
    b_t = each(lambda x, p: x * p, beta, p_inv)
    k_h = each(lambda x, p: x * p, k, p_rem)
    b_h = each(lambda x, p: x * p, beta, p_rem)

    lhs = each(lambda x, y: jnp.concatenate([x, y], axis=0).astype(BF16), a_t, r_t)
    akk = each(lambda x, y: _dot(x, bd_t(y)), lhs, k_t)
    abb = each(lambda x, y: _dot(x, bd_t(y)), lhs, b_t)
    a_ak = each(lambda m, x: jnp.where(m, x[:L], 0.0), strict, akk)
    a_rk = each(lambda m, x: jnp.where(m, x[L:], 0.0), incl, akk)
    a_ab = each(lambda m, x: jnp.where(m, x[:L], 0.0), strict, abb)
    a_rb = each(lambda m, x: jnp.where(m, x[L:], 0.0), incl, abb)

    pw = a_ab
    tinv = each(lambda x: eye + x, pw)
    pw = each(rcmul, pw, pw)
    for _ in range(4):
        both = each(lambda p, t: rcmul(jnp.concatenate([p, t], axis=0), p), pw, tinv)
        pw = [x[:L] for x in both]
        tinv = each(lambda t, x: t + x[L:], tinv, both)
    tinv = each(lambda t, p: t + rcmul(t, p), tinv, pw)

    av = each(lambda x, y, z: rcmul(jnp.concatenate([x, y], axis=0), z), a_ak, a_rk, v)
    ct = each(rcmul, a_rb, tinv)
    wu2 = each(lambda t, c, x, y: _dot(jnp.concatenate([t, c], axis=0).astype(BF16),
                                       jnp.concatenate([bd(x), bd(y[:L])], axis=1)), tinv, ct, a_t, av)
    wu = [x[:L] for x in wu2]
    rbwu = [x[L:] for x in wu2]
    o = each(lambda x, w, s, y: _dot((x + w[:, :GW]).astype(BF16), bd_t(s)) + y[L:] + w[:, GW:],
             r_t, rbwu, s0, av)

    m_bd = each(lambda w, x: _dot_tn(w[:, :GW].astype(BF16), x.astype(BF16)) * bdf, wu, b_h)
    n_full = each(lambda x, w, y, z: _dot_tn(jnp.concatenate([x, w[:, GW:]], axis=0).astype(BF16),
                                             jnp.concatenate([y, z], axis=0).astype(BF16)) * bdf,
                  v, wu, k_h, b_h)
    s1 = []
    for i in range(n):
        n_rc = n_full[i][0:HEAD]
        for h in range(1, HEADS_PER_GROUP):
            n_rc = n_rc + n_full[i][h * HEAD:(h + 1) * HEAD]
        s1.append(_dot(s0[i].astype(BF16), m_bd[i].astype(BF16)) + s0[i] * jnp.exp(tot[i]) + n_rc)
    return s1, o


def _scan_kernel(rf_ref, kf_ref, vf_ref, kkf_ref, af_ref, lwf_ref,
                 rb_ref, kb_ref, vb_ref, kkb_ref, ab_ref, lwb_ref, of_ref, ob_ref, s_ref, *, ng, nb):
    @pl.when(pl.program_id(1) == 0)
    def _():
        s_ref[...] = jnp.zeros_like(s_ref)

    gi = lax.broadcasted_iota(jnp.int32, (GW, GW), 0) // HEAD
    gj = lax.broadcasted_iota(jnp.int32, (GW, GW), 1) // HEAD
    bdm = (gi == gj).astype(F32).astype(BF16)
    dirs = ((rf_ref, kf_ref, vf_ref, kkf_ref, af_ref, lwf_ref), (rb_ref, kb_ref, vb_ref, kkb_ref, ab_ref, lwb_ref))
    chains = [(i, d, h) for i in range(nb) for d in range(2) for h in range(ng)]
    ins = [[dirs[d][q][i, :, h * GW:(h + 1) * GW].astype(F32) for i, d, h in chains] for q in range(6)]
    s1, o = _scan_chunks([s_ref[i, d, h] for i, d, h in chains], *ins, [d == 1 for _, d, _ in chains], bdm)
    for (i, d, h), s_new, o_new in zip(chains, s1, o):
        (of_ref, ob_ref)[d][i, :, h * GW:(h + 1) * GW] = o_new
        s_ref[i, d, h] = s_new


SCAN_BATCH_ROWS = 4


def _wkv_scan(r, v, kk, kf, af, lwf, kb, ab, lwb):
    b, t, c = r.shape
    nc = t // CHUNK
    ng = c // GW
    nb = math.gcd(b, SCAN_BATCH_ROWS)
    fspec = pl.BlockSpec((nb, CHUNK, c), lambda i, j: (i, j, 0))
    bspec = pl.BlockSpec((nb, CHUNK, c), lambda i, j: (i, nc - 1 - j, 0))
    return pl.pallas_call(
        functools.partial(_scan_kernel, ng=ng, nb=nb),
        out_shape=(jax.ShapeDtypeStruct((b, t, c), F32), jax.ShapeDtypeStruct((b, t, c), F32)),
        grid=(b // nb, nc),
        in_specs=[fspec] * 6 + [bspec] * 6,
        out_specs=(fspec, bspec),
        scratch_shapes=[pltpu.VMEM((nb, 2, ng, HEAD, GW), F32)],
        compiler_params=_params(("parallel", "arbitrary")),
        name="rw_scan",
    )(r, kf, v, kk, af, lwf, r, kb, v, kk, ab, lwb)


def _rw_post_kernel(sf_ref, sb_ref, bon_ref, g_ref, lg_ref, lb_ref, avg_ref, o_ref):
    s = sf_ref[...] + sb_ref[...]
    mean = _dot2(s, avg_ref[...])
    d = s - mean
    var = _dot((d * d).astype(BF16), avg_ref[...])
    y = d * lax.rsqrt(var + GN_EPS) * lg_ref[...] + lb_ref[...] + bon_ref[...].astype(F32)
    o_ref[...] = (y * g_ref[...].astype(F32)).astype(o_ref.dtype)


def _rw_post(sf, sb, bon, g, lnx_g, lnx_b, tm=512):
    b, t, c = sf.shape
    m = b * t
    tm = min(tm, m)
    ch = jnp.arange(c) // HEAD
    avg = ((ch[:, None] == ch[None, :]).astype(F32) / HEAD).astype(BF16)
    big = pl.BlockSpec((tm, c), lambda i: (i, 0))
    small = pl.BlockSpec((1, c), lambda i: (0, 0))
    out = pl.pallas_call(
        _rw_post_kernel,
        out_shape=jax.ShapeDtypeStruct((m, c), BF16),
        grid=(m // tm,),
        in_specs=[big, big, big, big, small, small, pl.BlockSpec((c, c), lambda i: (0, 0))],
        out_specs=big,
        compiler_params=_params(("parallel",)),
        name="rw_post",
    )(sf.reshape(m, c), sb.reshape(m, c), bon.reshape(m, c), g.reshape(m, c),
      lnx_g.reshape(1, c), lnx_b.reshape(1, c), avg)
    return out.reshape(b, t, c)


def _rwkv_branch(p, v_first, shift, w0, w2, a0, a2, g2, k_k, k_a, r_k, lnx_g, lnx_b, vres, c):
    r, v, kk, lwf, lwb, kf, kb, af, ab, g, bon = _rw_prep(
        p, v_first, shift, w0, w2, a0, a2, g2, k_k, k_a, r_k, vres, c)
    sf, sb = _wkv_scan(r, v, kk, kf, af, lwf, kb, ab, lwb)
    y = _rw_post(sf, sb, bon, g, lnx_g, lnx_b)
    return y, (v if vres is None else v_first)


def _forward(x, norm1_g, w_in, hy_conv_w, hy_conv_b, hy_f_w1, hy_f_b1, hy_f_w2, hy_f_b2, hy_f_w3, hy_f_b3,
             hy_f_freq, hy_f_wout, hy_skip, hy_norm_g, rw_shift, rw_w0, rw_w2, rw_a0, rw_a2, rw_g2, rw_k_k,
             rw_k_a, rw_r_k, rw_lnx_g, rw_lnx_b, rw_v0, rw_v1, rw_v2, w_out, norm2_g, mlp_w1, mlp_w2,
             final_g):
    b, t, d = x.shape
    depth = w_in.shape[0]
    c_hy = hy_norm_g.shape[1]
    c_rw = rw_lnx_g.shape[1]
    hy_proj = hy_conv_b.shape[1]
    assert b % 2 == 0 and t % CHUNK == 0 and CHUNK == HEAD
    tabs = _dft_tables(t)
    assert tabs["n2"] % SUB == 0 and tabs["n1"] % SUB == 0
    v_first = None
    for l in range(depth):
        w_l = w_in[l].astype(BF16)
        w_rw = w_l[:, hy_proj:]
        vres = None if l == 0 else (rw_v0[l - 1], rw_v1[l - 1], rw_v2[l - 1])
        if vres is not None:
            v1 = vres[1].astype(BF16)
            w_rw = jnp.concatenate([w_rw, jnp.pad(v1, ((0, 0), (0, -v1.shape[1] % 128)))], axis=1)
        p_hy, p_rw = _in_proj(x, norm1_g[l], w_l[:, :hy_proj], w_rw, tabs["n1h"], tabs["n2"])
        y_hy = _hyena_branch(p_hy, hy_conv_w[l], hy_conv_b[l], hy_f_w1[l], hy_f_b1[l], hy_f_w2[l], hy_f_b2[l],
                             hy_f_w3[l], hy_f_b3[l], hy_f_freq[l], hy_f_wout[l], hy_skip[l], hy_norm_g[l], tabs)
        y_rw, v_first = _rwkv_branch(p_rw, v_first, rw_shift[l], rw_w0[l], rw_w2[l], rw_a0[l], rw_a2[l],
                                     rw_g2[l], rw_k_k[l], rw_k_a[l], rw_r_k[l], rw_lnx_g[l], rw_lnx_b[l],
                                     vres, c_rw)
        x = _out_proj(y_hy.reshape(b * t, c_hy), y_rw.reshape(b * t, c_rw), w_out[l].astype(BF16),
                      x.reshape(b * t, d)).reshape(b, t, d)
        x = _mlp(x.reshape(b * t, d), norm2_g[l], mlp_w1[l].astype(BF16), mlp_w2[l].astype(BF16)).reshape(b, t, d)
    return _rmsnorm(x.reshape(b * t, d), final_g, F32).reshape(b, t, d)


def kernel(x, norm1_g, w_in, hy_conv_w, hy_conv_b, hy_f_w1, hy_f_b1, hy_f_w2, hy_f_b2, hy_f_w3, hy_f_b3,
           hy_f_freq, hy_f_wout, hy_skip, hy_norm_g, rw_shift, rw_w0, rw_w2, rw_a0, rw_a2, rw_g2, rw_k_k,
           rw_k_a, rw_r_k, rw_lnx_g, rw_lnx_b, rw_v0, rw_v1, rw_v2, w_out, norm2_g, mlp_w1, mlp_w2, final_g):
    return _forward(x, norm1_g, w_in, hy_conv_w, hy_conv_b, hy_f_w1, hy_f_b1, hy_f_w2, hy_f_b2, hy_f_w3,
                    hy_f_b3, hy_f_freq, hy_f_wout, hy_skip, hy_norm_g, rw_shift, rw_w0, rw_w2, rw_a0, rw_a2,
                    rw_g2, rw_k_k, rw_k_a, rw_r_k, rw_lnx_g, rw_lnx_b, rw_v0, rw_v1, rw_v2, w_out, norm2_g,
                    mlp_w1, mlp_w2, final_g)
```

```python
import functools
import math

import jax
import jax.numpy as jnp
from jax import lax
from jax.experimental import pallas as pl
from jax.experimental.pallas import tpu as pltpu

F32 = jnp.float32
BF16 = jnp.bfloat16
HP = lax.Precision.HIGHEST

HEAD = 64
HEADS_PER_GROUP = 4
GW = HEAD * HEADS_PER_GROUP
CHUNK = 64
RMS_EPS = 1e-5
GN_EPS = HEAD * 1e-5
HY_TARGET = 1e-2
HY_FAST_PCT = 0.3
HY_SLOW_PCT = 1.5
VMEM_LIMIT = 56 * 1024 * 1024


def _params(sem, vmem=VMEM_LIMIT):
    return pltpu.CompilerParams(dimension_semantics=sem, vmem_limit_bytes=vmem)


def _dot(a, b, precision=None):
    return jnp.dot(a, b, preferred_element_type=F32, precision=precision)


def _dot_nt(a, b, precision=None):
    return lax.dot_general(a, b, (((1,), (1,)), ((), ())), preferred_element_type=F32,
                           precision=precision)


def _dot_tn(a, b, precision=None):
    return lax.dot_general(a, b, (((0,), (0,)), ((), ())), preferred_element_type=F32,
                           precision=precision)


def _const_spec(shape):
    return pl.BlockSpec(shape, lambda *_: (0,) * len(shape), pipeline_mode=pl.Buffered(1))


def _dot2(x, m):
    hi = x.astype(BF16)
    lo = (x - hi.astype(F32)).astype(BF16)
    return _dot(hi, m) + _dot(lo, m)


def _mlp_kernel(x_ref, g_ref, w1_ref, w2_ref, *rest, nchunk):
    x = x_ref[...]
    hn = (x * lax.rsqrt(jnp.mean(x * x, axis=-1, keepdims=True) + RMS_EPS) * g_ref[...]).astype(BF16)
    ff = w1_ref.shape[1]
    cw = ff // nchunk
    acc = x
    for j in range(nchunk):
        h = _dot(hn, w1_ref[:, j * cw:(j + 1) * cw])
        h = jnp.square(jnp.maximum(h, 0.0)).astype(BF16)
        acc = acc + _dot(h, w2_ref[j * cw:(j + 1) * cw, :])
    if len(rest) == 2:
        gf_ref, o_ref = rest
        acc = acc * lax.rsqrt(jnp.mean(acc * acc, axis=-1, keepdims=True) + RMS_EPS) * gf_ref[...]
    else:
        (o_ref,) = rest
    o_ref[...] = acc


def _mlp(x2d, g, w1, w2, final_g=None, tm=512, nchunk=4):
    m, d = x2d.shape
    ff = w1.shape[1]
    tm = min(tm, m)
    in_specs = [pl.BlockSpec((tm, d), lambda i: (i, 0)), _const_spec((1, d)),
                _const_spec((d, ff)), _const_spec((ff, d))]
    args = [x2d, g.reshape(1, d), w1, w2]
    if final_g is not None:
        in_specs.append(_const_spec((1, d)))
        args.append(final_g.reshape(1, d))
    return pl.pallas_call(
        functools.partial(_mlp_kernel, nchunk=nchunk),
        out_shape=jax.ShapeDtypeStruct((m, d), F32),
        grid=(m // tm,),
        in_specs=in_specs,
        out_specs=pl.BlockSpec((tm, d), lambda i: (i, 0)),
        compiler_params=_params(("parallel",)),
        name="mlp",
    )(*args)


SUB = 8
SUB16 = 16


def _rms(x, g):
    return (x * lax.rsqrt(jnp.mean(x * x, axis=-1, keepdims=True) + RMS_EPS) * g).astype(BF16)


def _in_proj_kernel(x_ref, g_ref, wh_ref, wr_ref, oh_ref, or_ref, *, n2):
    hn = _rms(x_ref[0], g_ref[...])
    or_ref[0] = _dot(hn, wr_ref[...]).astype(or_ref.dtype)
    ph = _dot(hn, wh_ref[...])
    for j in range(SUB):
        oh_ref[0, :, j, :] = ph[j * n2:(j + 1) * n2]


def _in_proj(x, g, w_hy, w_rw, n1h, n2):
    b, t, d = x.shape
    ph, pr = w_hy.shape[1], w_rw.shape[1]
    tm = SUB * n2
    p_hy, p_rw = pl.pallas_call(
        functools.partial(_in_proj_kernel, n2=n2),
        out_shape=(jax.ShapeDtypeStruct((b, n2, n1h, ph), F32), jax.ShapeDtypeStruct((b, t, pr), BF16)),
        grid=(b, t // tm),
        in_specs=[pl.BlockSpec((1, tm, d), lambda i, j: (i, j, 0)), _const_spec((1, d)),
                  _const_spec((d, ph)), _const_spec((d, pr))],
        out_specs=(pl.BlockSpec((1, n2, SUB, ph), lambda i, j: (i, 0, j, 0)),
                   pl.BlockSpec((1, tm, pr), lambda i, j: (i, j, 0))),
        compiler_params=_params(("parallel", "parallel")),
        name="in_proj",
    )(x, g.reshape(1, d), w_hy, w_rw)
    return p_hy.reshape(b, t, ph), p_rw


def _shortconv_kernel(p_ref, w_ref, b_ref, o_ref, *, n1h, n2):
    w0 = w_ref[0:1, :]
    w1 = w_ref[1:2, :]
    w2 = w_ref[2:3, :]
    bias = b_ref[...]
    rows = lax.broadcasted_iota(jnp.int32, (n1h, 1), 0)

    def blk(i):
        return p_ref[0, pl.ds(pl.multiple_of(i * n1h, n1h), n1h), :].astype(F32)

    def body(i, carry):
        o_ref[0, pl.ds(pl.multiple_of(i * n1h, n1h), n1h), :] = (
            w0 * blk(i - 1) + w1 * blk(i) + w2 * blk(i + 1) + bias).astype(o_ref.dtype)
        return carry

    lax.fori_loop(1, n2 - 1, body, 0)
    last = p_ref[0, (n2 - 1) * n1h:, :].astype(F32)
    first = p_ref[0, :n1h, :].astype(F32)
    prev0 = jnp.where(rows == 0, 0.0, pltpu.roll(last, 1, 0))
    o_ref[0, :n1h, :] = (w0 * prev0 + w1 * first + w2 * p_ref[0, n1h:2 * n1h, :].astype(F32)
                         + bias).astype(o_ref.dtype)
    nxt = jnp.where(rows == n1h - 1, 0.0, pltpu.roll(first, n1h - 1, 0))
    o_ref[0, (n2 - 1) * n1h:, :] = (w0 * p_ref[0, (n2 - 2) * n1h:(n2 - 1) * n1h, :].astype(F32) + w1 * last
                                    + w2 * nxt + bias).astype(o_ref.dtype)


def _shortconv(p, w, bias, n1h, n2, cb=128):
    b, t, c = p.shape
    return pl.pallas_call(
        functools.partial(_shortconv_kernel, n1h=n1h, n2=n2),
        out_shape=jax.ShapeDtypeStruct((b, t, c), BF16),
        grid=(b, c // cb),
        in_specs=[pl.BlockSpec((1, t, cb), lambda i, j: (i, 0, j)),
                  pl.BlockSpec((3, cb), lambda i, j: (0, j)),
                  pl.BlockSpec((1, cb), lambda i, j: (0, j))],
        out_specs=pl.BlockSpec((1, t, cb), lambda i, j: (i, 0, j)),
        compiler_params=_params(("parallel", "parallel")),
        name="hy_shortconv",
    )(p, w, bias.reshape(1, c))


def _dft_tables(t):
    n = 2 * t
    n1 = 1 << ((n.bit_length() - 1) // 2)
    n2 = n // n1
    n1h = n1 // 2
    two_pi = 2.0 * math.pi
    k1 = jnp.arange(n1, dtype=jnp.int32)
    n2i = jnp.arange(n2, dtype=jnp.int32)

    def cs(prod, mod):
        ang = (two_pi / mod) * (prod % mod).astype(F32)
        return jnp.cos(ang), jnp.sin(ang)

    tpos = n2i[:, None] + n2 * jnp.arange(n1h, dtype=jnp.int32)[None, :]
    c, s = cs(k1[None, :, None] * tpos[:, None, :], n)
    g1 = jnp.concatenate([jnp.concatenate([c, s], axis=2), jnp.concatenate([-s, c], axis=2)], axis=1)
    ct, st = jnp.swapaxes(c, 1, 2) / n, jnp.swapaxes(s, 1, 2) / n
    g4a = jnp.concatenate([ct, st], axis=1)
    g4b = jnp.concatenate([-st, ct], axis=1)
    tposf = n2i[:, None] + n2 * jnp.arange(n1, dtype=jnp.int32)[None, :]
    cf, sf = cs(k1[None, :, None] * tposf[:, None, :], n)
    g1f = jnp.concatenate([cf, -sf], axis=1)
    c2, s2 = cs(n2i[:, None] * n2i[None, :], n2)
    f2a = jnp.concatenate([c2, -s2], axis=0)
    f2b = jnp.concatenate([s2, c2], axis=0)
    f3 = jnp.concatenate([jnp.concatenate([c2, -s2], axis=1), jnp.concatenate([s2, c2], axis=1)], axis=0)
    return dict(n1=n1, n2=n2, n1h=n1h, g1=g1.astype(BF16), g4a=g4a.astype(BF16), g4b=g4b.astype(BF16),
                g1f=g1f.astype(BF16), f2a=f2a.astype(BF16), f2b=f2b.astype(BF16), f3=f3.astype(BF16))


def _filter_mlp_kernel(z_ref, w1_ref, b1_ref, w2_ref, b2_ref, w3_ref, b3_ref, fr_ref, wo_ref, dl_ref,
                       o_ref):
    z = z_ref[...]
    h = jnp.sin(fr_ref[0:1, :] * (_dot(z, w1_ref[...], HP) + b1_ref[...]))
    h = jnp.sin(fr_ref[1:2, :] * (_dot(h, w2_ref[...], HP) + b2_ref[...]))
    h = jnp.sin(fr_ref[2:3, :] * (_dot(h, w3_ref[...], HP) + b3_ref[...]))
    window = jnp.exp(-z[:, 0:1] * dl_ref[...])
    o_ref[...] = _dot(h, wo_ref[...], HP) * window


def _filter_mlp(t, w1, b1, w2, b2, w3, b3, freq, w_out, c_hy, tr=512):
    emb, width = w1.shape
    bands = (emb - 1) // 2
    pos = jnp.arange(t, dtype=F32)
    tt = pos / max(t - 1, 1)
    fr = jnp.linspace(1e-4, bands - 1, bands, dtype=F32)
    ang = (2.0 * math.pi / t) * pos[:, None] * fr[None, :]
    z = jnp.concatenate([tt[:, None], jnp.cos(ang), -jnp.sin(ang)], axis=-1)
    z = jnp.pad(z, ((0, 0), (0, 128 - emb)))
    w1 = jnp.pad(w1, ((0, 128 - emb), (0, 0)))
    emb = 128
    max_decay = math.log(HY_TARGET) / HY_FAST_PCT
    min_decay = math.log(HY_TARGET) / HY_SLOW_PCT
    delta = jnp.abs(jnp.linspace(min_decay, max_decay, c_hy, dtype=F32))
    nout = w_out.shape[1]
    delta_full = jnp.tile(delta, nout // c_hy).reshape(1, nout)
    tr = min(tr, t)
    full = lambda a: pl.BlockSpec(a.shape, lambda i: (0,) * a.ndim)
    args = [z, w1, b1.reshape(1, -1), w2, b2.reshape(1, -1), w3, b3.reshape(1, -1), freq, w_out, delta_full]
    return pl.pallas_call(
        _filter_mlp_kernel,
        out_shape=jax.ShapeDtypeStruct((t, nout), F32),
        grid=(t // tr,),
        in_specs=[pl.BlockSpec((tr, emb), lambda i: (i, 0))] + [full(a) for a in args[1:]],
        out_specs=pl.BlockSpec((tr, nout), lambda i: (i, 0)),
        compiler_params=_params(("parallel",)),
        name="hy_filter_mlp",
    )(*args)


def _filter_s1_kernel(x_ref, g_ref, o_ref, ss_ref, *, g, oc):
    @pl.when(pl.program_id(0) == 0)
    def _():
        ss_ref[...] = jnp.zeros_like(ss_ref)

    n1 = x_ref.shape[0]
    x = jnp.concatenate([x_ref[:, j, :] for j in range(g)], axis=0)
    ss_ref[...] += jnp.sum(x * x, axis=0, keepdims=True)
    xb = x.astype(BF16)
    for j in range(g):
        o_ref[j] = _dot(g_ref[j], xb[j * n1:(j + 1) * n1])


def _filter_s2_kernel(sr_ref, si_ref, fa_ref, fb_ref, ss_ref, o_ref, *, g, oc):
    n2 = sr_ref.shape[0]
    scale = lax.rsqrt(ss_ref[...] + 1e-6)
    sr = jnp.concatenate([sr_ref[:, j, :] for j in range(g)], axis=0).astype(BF16)
    si = jnp.concatenate([si_ref[:, j, :] for j in range(g)], axis=0).astype(BF16)
    for j in range(g):
        x = _dot(fa_ref[...], sr[j * n2:(j + 1) * n2]) + _dot(fb_ref[...], si[j * n2:(j + 1) * n2])
        o_ref[j] = x * scale


def _filter_spectrum(two, tabs):
    n, oc = two.shape
    n1, n2 = tabs["n1"], tabs["n2"]
    g = SUB
    s1, ss = pl.pallas_call(
        functools.partial(_filter_s1_kernel, g=g, oc=oc),
        out_shape=(jax.ShapeDtypeStruct((n2, 2 * n1, oc), F32), jax.ShapeDtypeStruct((1, oc), F32)),
        grid=(n2 // g,),
        in_specs=[pl.BlockSpec((n1, g, oc), lambda j: (0, j, 0)),
                  pl.BlockSpec((g, 2 * n1, n1), lambda j: (j, 0, 0))],
        out_specs=(pl.BlockSpec((g, 2 * n1, oc), lambda j: (j, 0, 0)),
                   pl.BlockSpec((1, oc), lambda j: (0, 0))),
        compiler_params=_params(("arbitrary",)),
        name="hy_filter_dft1",
    )(two.reshape(n1, n2, oc), tabs["g1f"])
    return pl.pallas_call(
        functools.partial(_filter_s2_kernel, g=g, oc=oc),
        out_shape=jax.ShapeDtypeStruct((n1, 2 * n2, oc), F32),
        grid=(n1 // g,),
        in_specs=[pl.BlockSpec((n2, g, oc), lambda k: (0, k, 0)),
                  pl.BlockSpec((n2, g, oc), lambda k: (0, n1 // g + k, 0)),
                  pl.BlockSpec((2 * n2, n2), lambda k: (0, 0)),
                  pl.BlockSpec((2 * n2, n2), lambda k: (0, 0)),
                  pl.BlockSpec((1, oc), lambda k: (0, 0))],
        out_specs=pl.BlockSpec((g, 2 * n2, oc), lambda k: (k, 0, 0)),
        compiler_params=_params(("parallel",)),
        name="hy_filter_dft2",
    )(s1, s1, tabs["f2a"], tabs["f2b"], ss)


def _conv_s1_kernel(u_ref, g_ref, o_ref, *, g, n1h):
    for j in range(g):
        rows = slice(j * n1h, (j + 1) * n1h)
        rhs = jnp.concatenate([u_ref[0, 0, rows, :], u_ref[0, 1, rows, :]], axis=0)
        o_ref[0, j] = _dot(g_ref[j], rhs)


def _conv_s23_kernel(sr_ref, si_ref, fa_ref, fb_ref, f3_ref, h_ref, o_ref, *, g, n2):
    sr = jnp.concatenate([sr_ref[0, :, j, :] for j in range(g)], axis=0).astype(BF16)
    si = jnp.concatenate([si_ref[0, :, j, :] for j in range(g)], axis=0).astype(BF16)
    for j in range(g):
        x = _dot(fa_ref[...], sr[j * n2:(j + 1) * n2]) + _dot(fb_ref[...], si[j * n2:(j + 1) * n2])
        xr, xi = x[:n2], x[n2:]
        hr, hi = h_ref[j, :n2, :], h_ref[j, n2:, :]
        y = jnp.concatenate([xr * hr - xi * hi, xr * hi + xi * hr], axis=0).astype(BF16)
        o_ref[0, j] = _dot(f3_ref[...], y)


def _conv_s4_kernel(rr_ref, ri_ref, ga_ref, gb_ref, u_ref, gate_ref, skip_ref, *rest, g, n1h, norm):
    if norm:
        ng_ref, avg_ref, o_ref = rest
    else:
        (o_ref,) = rest
    skip = skip_ref[...]
    n1 = rr_ref.shape[1]
    rr = jnp.concatenate([rr_ref[0, :, j, :] for j in range(g)], axis=0).astype(BF16)
    ri = jnp.concatenate([ri_ref[0, :, j, :] for j in range(g)], axis=0).astype(BF16)
    for j in range(g):
        rows = slice(j * n1h, (j + 1) * n1h)
        y = _dot(ga_ref[j], rr[j * n1:(j + 1) * n1]) + _dot(gb_ref[j], ri[j * n1:(j + 1) * n1])
        for q in range(2):
            z = gate_ref[0, q, rows, :].astype(F32) * (y[q * n1h:(q + 1) * n1h]
                                                       + u_ref[0, q, rows, :].astype(F32) * skip)
            if norm:
                ms = _dot((z * z).astype(BF16), avg_ref[...])
                o_ref[0, q, :, j, :] = z * lax.rsqrt(ms + RMS_EPS) * ng_ref[...]
            else:
                o_ref[0, q, rows, :] = z.astype(o_ref.dtype)


def _long_conv_gate(u_arr, u_col, gate_arr, gate_col, h_spec, h_col, skip, tabs, c, norm_g=None):
    b, t, _ = u_arr.shape
    n1, n2, n1h = tabs["n1"], tabs["n2"], tabs["n1h"]
    g = SUB
    npair = b // 2
    u4 = u_arr.reshape(npair, 2, t, u_arr.shape[2])
    gate4 = gate_arr.reshape(npair, 2, t, gate_arr.shape[2])
    s1 = pl.pallas_call(
        functools.partial(_conv_s1_kernel, g=g, n1h=n1h),
        out_shape=jax.ShapeDtypeStruct((npair, n2, 2 * n1, c), F32),
        grid=(npair, n2 // g),
        in_specs=[pl.BlockSpec((1, 2, g * n1h, c), lambda p, j: (p, 0, j, u_col)),
                  pl.BlockSpec((g, 2 * n1, n1), lambda p, j: (j, 0, 0))],
        out_specs=pl.BlockSpec((1, g, 2 * n1, c), lambda p, j: (p, j, 0, 0)),
        compiler_params=_params(("parallel", "parallel")),
        name="hy_conv_dft1",
    )(u4, tabs["g1"])
    r = pl.pallas_call(
        functools.partial(_conv_s23_kernel, g=g, n2=n2),
        out_shape=jax.ShapeDtypeStruct((npair, n1, 2 * n2, c), F32),
        grid=(npair, n1 // g),
        in_specs=[pl.BlockSpec((1, n2, g, c), lambda p, k: (p, 0, k, 0)),
                  pl.BlockSpec((1, n2, g, c), lambda p, k: (p, 0, n1 // g + k, 0)),
                  pl.BlockSpec((2 * n2, n2), lambda p, k: (0, 0)),
                  pl.BlockSpec((2 * n2, n2), lambda p, k: (0, 0)),
                  pl.BlockSpec((2 * n2, 2 * n2), lambda p, k: (0, 0)),
                  pl.BlockSpec((g, 2 * n2, c), lambda p, k: (k, 0, h_col))],
        out_specs=pl.BlockSpec((1, g, 2 * n2, c), lambda p, k: (p, k, 0, 0)),
        compiler_params=_params(("parallel", "parallel")),
        name="hy_conv_dft23",
    )(s1, s1, tabs["f2a"], tabs["f2b"], tabs["f3"], h_spec)
    norm = norm_g is not None
    in_specs = [pl.BlockSpec((1, n1, g, c), lambda p, j: (p, 0, j, 0)),
                pl.BlockSpec((1, n1, g, c), lambda p, j: (p, 0, n2 // g + j, 0)),
                pl.BlockSpec((g, n1, n1), lambda p, j: (j, 0, 0)),
                pl.BlockSpec((g, n1, n1), lambda p, j: (j, 0, 0)),
                pl.BlockSpec((1, 2, g * n1h, c), lambda p, j: (p, 0, j, u_col)),
                pl.BlockSpec((1, 2, g * n1h, c), lambda p, j: (p, 0, j, gate_col)),
                pl.BlockSpec((1, c), lambda p, j: (0, 0))]
    args = [r, r, tabs["g4a"], tabs["g4b"], u4, gate4, skip.reshape(1, c)]
    if norm:
        ch = jnp.arange(c) // HEAD
        avg = ((ch[:, None] == ch[None, :]).astype(F32) / HEAD).astype(BF16)
        in_specs += [pl.BlockSpec((1, c), lambda p, j: (0, 0)), pl.BlockSpec((c, c), lambda p, j: (0, 0))]
        args += [norm_g.reshape(1, c), avg]
        out_shape = jax.ShapeDtypeStruct((npair, 2, n1h, n2, c), F32)
        out_spec = pl.BlockSpec((1, 2, n1h, g, c), lambda p, j: (p, 0, 0, j, 0))
    else:
        out_shape = jax.ShapeDtypeStruct((npair, 2, t, c), BF16)
        out_spec = pl.BlockSpec((1, 2, g * n1h, c), lambda p, j: (p, 0, j, 0))
    out = pl.pallas_call(
        functools.partial(_conv_s4_kernel, g=g, n1h=n1h, norm=norm),
        out_shape=out_shape,
        grid=(npair, n2 // g),
        in_specs=in_specs,
        out_specs=out_spec,
        compiler_params=_params(("parallel", "parallel")),
        name="hy_conv_dft4",
    )(*args)
    return out.reshape(b, t, c)


def _hyena_branch(p, conv_w, conv_b, fw1, fb1, fw2, fb2, fw3, fb3, ffreq, fwout, skip, norm_g, tabs):
    b, t, _ = p.shape
    c = norm_g.shape[0]
    n1, n2, n1h = tabs["n1"], tabs["n2"], tabs["n1h"]
    u = _shortconv(p, conv_w, conv_b, n1h, n2)
    h = _filter_mlp(t, fw1, fb1, fw2, fb2, fw3, fb3, ffreq, fwout, c)
    order = skip.shape[0]
    h = h.reshape(t, order, 2, c)
    fwd, bwd = h[:, :, 0], h[:, :, 1]
    two = jnp.concatenate([fwd, jnp.zeros_like(fwd[:1]), bwd[1:][::-1]], axis=0).reshape(2 * t, order * c)
    hspec = _filter_spectrum(two, tabs)
    z = _long_conv_gate(u, 2, u, 0, hspec, 0, skip[0], tabs, c)
    return _long_conv_gate(z, 0, u, 1, hspec, 1, skip[1], tabs, c, norm_g=norm_g)


def _rw_prep_kernel(*refs, has_vres, c):
    if has_vres:
        (p_ref, pp_ref, pn_ref, mu_ref, w0_ref, w2_ref, a0_ref, a2_ref, g2_ref, kk_ref, ka_ref, rk_ref,
         sum_ref, vf_ref, v0_ref, v2_ref,
         r_o, v_o, kk_o, lwf_o, lwb_o, kf_o, kb_o, af_o, ab_o, g_o, bon_o) = refs
    else:
        (p_ref, pp_ref, pn_ref, mu_ref, w0_ref, w2_ref, a0_ref, a2_ref, g2_ref, kk_ref, ka_ref, rk_ref,
         sum_ref,
         r_o, v_o, kk_o, lwf_o, lwb_o, kf_o, kb_o, af_o, ab_o, g_o, bon_o) = refs
    i = pl.program_id(1)
    last = pl.num_programs(1) - 1
    nsh = mu_ref.shape[1]
    p = p_ref[0, :, :nsh].astype(F32)
    tt = p.shape[0]
    rows = lax.broadcasted_iota(jnp.int32, (tt, 1), 0)
    prev_row = jnp.where(i == 0, 0.0, pp_ref[0, SUB16 - 1:SUB16, :nsh].astype(F32))
    next_row = jnp.where(i == last, 0.0, pn_ref[0, 0:1, :nsh].astype(F32))
    prev = jnp.where(rows == 0, prev_row, pltpu.roll(p, 1, 0))
    nxt = jnp.where(rows == tt - 1, next_row, pltpu.roll(p, tt - 1, 0))
    mu0, mu1 = mu_ref[0:1, :], mu_ref[1:2, :]
    pf = p * (1.0 - mu0 - mu1) + mu0 * prev + mu1 * nxt
    r = pf[:, :c]
    k = pf[:, c:2 * c]
    v = pf[:, 2 * c:3 * c]
    lw = 3 * c
    nd = w2_ref.shape[0]
    na = a2_ref.shape[0]
    wd = pf[:, lw:lw + nd]
    ad = pf[:, lw + nd:lw + nd + na]
    gd = pf[:, lw + nd + na:]
    if has_vres:
        lora = _dot(p_ref[0, :, nsh:], v2_ref[...])
        v = v + (vf_ref[0].astype(F32) - v) * jax.nn.sigmoid(v0_ref[...] + lora)
    g = _dot(jax.nn.sigmoid(gd).astype(BF16), g2_ref[...])
    kk = k * kk_ref[...]
    kk = kk * lax.rsqrt(jnp.maximum(_dot((kk * kk).astype(BF16), sum_ref[...]), 1e-24))
    wl = w0_ref[...] + _dot(jnp.tanh(wd).astype(BF16), w2_ref[...])
    logw = -math.exp(-0.5) * jax.nn.sigmoid(wl)
    a = jax.nn.sigmoid(a0_ref[...] + _dot(ad.astype(BF16), a2_ref[...]))
    ka = ka_ref[...]
    rk = rk_ref[...]
    k_d = [k * (1.0 + (a[:, d * c:(d + 1) * c] - 1.0) * ka) for d in range(2)]
    bon = _dot((r * (k_d[0] + k_d[1]) * rk).astype(BF16), sum_ref[...]) * v
    r_o[0] = r.astype(BF16)
    v_o[0] = v.astype(BF16)
    kk_o[0] = kk.astype(BF16)
    lwf_o[0] = logw[:, :c]
    lwb_o[0] = logw[:, c:]
    kf_o[0] = k_d[0].astype(BF16)
    kb_o[0] = k_d[1].astype(BF16)
    af_o[0] = a[:, :c].astype(BF16)
    ab_o[0] = a[:, c:].astype(BF16)
    g_o[0] = g.astype(BF16)
    bon_o[0] = bon.astype(BF16)


def _blockdiag2(m):
    k, c = m.shape[1], m.shape[2]
    z = jnp.zeros((k, c), m.dtype)
    return jnp.concatenate([jnp.concatenate([m[0], z], axis=1), jnp.concatenate([z, m[1]], axis=1)], axis=0)


def _rw_prep(p, v_first, shift, w0, w2, a0, a2, g2, k_k, k_a, r_k, vres, c, tt=256):
    b, t, pw = p.shape
    tt = min(tt, t)
    has_vres = vres is not None
    ch = jnp.arange(c) // HEAD
    summ = (ch[:, None] == ch[None, :]).astype(BF16)
    row = lambda a: a.reshape(1, -1)
    args = [p, p, p, shift, row(w0), _blockdiag2(w2).astype(BF16), row(a0), _blockdiag2(a2).astype(BF16),
            g2.astype(BF16), row(k_k), row(k_a), row(r_k), summ]
    full = lambda a: pl.BlockSpec(a.shape, lambda i, j: (0,) * a.ndim)
    nhb = t // SUB16
    in_specs = [pl.BlockSpec((1, tt, pw), lambda i, j: (i, j, 0)),
                pl.BlockSpec((1, SUB16, pw), lambda i, j: (i, jnp.maximum(j * (tt // SUB16) - 1, 0), 0)),
                pl.BlockSpec((1, SUB16, pw), lambda i, j: (i, jnp.minimum((j + 1) * (tt // SUB16), nhb - 1), 0))]
    in_specs += [full(a) for a in args[3:]]
    if has_vres:
        v0, _, v2 = vres
        v2p = jnp.pad(v2, ((0, pw - shift.shape[1] - v2.shape[0]), (0, 0))).astype(BF16)
        extra = [v_first, row(v0), v2p]
        in_specs += [pl.BlockSpec((1, tt, c), lambda i, j: (i, j, 0))] + [full(a) for a in extra[1:]]
        args += extra
    outs = tuple(jax.ShapeDtypeStruct((b, t, c), F32 if i in (3, 4) else BF16) for i in range(11))
    return pl.pallas_call(
        functools.partial(_rw_prep_kernel, has_vres=has_vres, c=c),
        out_shape=outs,
        grid=(b, t // tt),
        in_specs=in_specs,
        out_specs=tuple(pl.BlockSpec((1, tt, c), lambda i, j: (i, j, 0)) for _ in range(11)),
        compiler_params=_params(("parallel", "parallel")),
        name="rw_prep",
    )(*args)


def _split3(x):
    h1 = x.astype(BF16)
    r1 = x - h1.astype(F32)
    h2 = r1.astype(BF16)
    h3 = (r1 - h2.astype(F32)).astype(BF16)
    return h1, h2, h3


def _scan_chunks(s0, r, k, v, kk, a, lw, rev, bdm):
    L = CHUNK
    n = len(r)
    each = lambda f, *ls: [f(*xs) for xs in zip(*ls)]
    ti = lax.broadcasted_iota(jnp.int32, (L, L), 0)
    si = lax.broadcasted_iota(jnp.int32, (L, L), 1)
    tt = lax.broadcasted_iota(jnp.int32, (L, GW), 0)
    ss = lax.broadcasted_iota(jnp.int32, (L, GW), 1) % L
    eye = (ss == tt).astype(F32)
    tri = [((si >= ti) if q else (si <= ti)).astype(F32).astype(BF16) for q in rev]
    strict = [(ss > tt) if q else (ss < tt) for q in rev]
    incl = [(ss >= tt) if q else (ss <= tt) for q in rev]
    bdf = bdm.astype(F32)

    def bd(x):
        xb = x.astype(BF16)
        return jnp.concatenate([xb] * HEADS_PER_GROUP, axis=0) * bdm

    def bd_t(x):
        xt = x.T.astype(BF16)
        return jnp.concatenate([xt] * HEADS_PER_GROUP, axis=1) * bdm

    def rcmul(x, y):
        return _dot(x.astype(BF16), bd(y))

    parts = each(_split3, lw)
    cum = [_dot(t, p[0]) + _dot(t, p[1]) + _dot(t, p[2]) for t, p in zip(tri, parts)]
    tot = [c[0:1, :] if q else c[L - 1:L, :] for c, q in zip(cum, rev)]
    p_in = each(jnp.exp, cum)
    p_inv = each(lambda c: jnp.exp(-c), cum)
    p_ex = each(lambda c, w: jnp.exp(c - w), cum, lw)
    p_rem = each(lambda t, c: jnp.exp(t - c), tot, cum)
    beta = each(lambda x, y: x * y, kk, a)
    a_t = each(lambda x, p: -x * p, kk, p_ex)
    r_t = each(lambda x, p: x * p, r, p_in)
    k_t = each(lambda x, p: x * p, k, p_inv)
    b_t = each(lambda x, p: x * p, beta, p_inv)
    k_h = each(lambda x, p: x * p, k, p_rem)
    b_h = each(lambda x, p: x * p, beta, p_rem)

    lhs = each(lambda x, y: jnp.concatenate([x, y], axis=0).astype(BF16), a_t, r_t)
    akk = each(lambda x, y: _dot(x, bd_t(y)), lhs, k_t)
    abb = each(lambda x, y: _dot(x, bd_t(y)), lhs, b_t)
    a_ak = each(lambda m, x: jnp.where(m, x[:L], 0.0), strict, akk)
    a_rk = each(lambda m, x: jnp.where(m, x[L:], 0.0), incl, akk)
    a_ab = each(lambda m, x: jnp.where(m, x[:L], 0.0), strict, abb)
    a_rb = each(lambda m, x: jnp.where(m, x[L:], 0.0), incl, abb)

    pw = a_ab
    tinv = each(lambda x: eye + x, pw)
    pw = each(rcmul, pw, pw)
    for _ in range(4):
        both = each(lambda p, t: rcmul(jnp.concatenate([p, t], axis=0), p), pw, tinv)
        pw = [x[:L] for x in both]
        tinv = each(lambda t, x: t + x[L:], tinv, both)
    tinv = each(lambda t, p: t + rcmul(t, p), tinv, pw)

    av = each(lambda x, y, z: rcmul(jnp.concatenate([x, y], axis=0), z), a_ak, a_rk, v)
    ct = each(rcmul, a_rb, tinv)
    wu2 = each(lambda t, c, x, y: _dot(jnp.concatenate([t, c], axis=0).astype(BF16),
                                       jnp.concatenate([bd(x), bd(y[:L])], axis=1)), tinv, ct, a_t, av)
    wu = [x[:L] for x in wu2]
    rbwu = [x[L:] for x in wu2]
    o = each(lambda x, w, s, y: _dot((x + w[:, :GW]).astype(BF16), bd_t(s)) + y[L:] + w[:, GW:],
             r_t, rbwu, s0, av)

    m_bd = each(lambda w, x: _dot_tn(w[:, :GW].astype(BF16), x.astype(BF16)) * bdf, wu, b_h)
    n_full = each(lambda x, w, y, z: _dot_tn(jnp.concatenate([x, w[:, GW:]], axis=0).astype(BF16),
                                             jnp.concatenate([y, z], axis=0).astype(BF16)) * bdf,
                  v, wu, k_h, b_h)
    s1 = []
    for i in range(n):
        n_rc = n_full[i][0:HEAD]
        for h in range(1, HEADS_PER_GROUP):
            n_rc = n_rc + n_full[i][h * HEAD:(h + 1) * HEAD]
        s1.append(_dot(s0[i].astype(BF16), m_bd[i].astype(BF16)) + s0[i] * jnp.exp(tot[i]) + n_rc)
    return s1, o


def _scan_kernel(rf_ref, kf_ref, vf_ref, kkf_ref, af_ref, lwf_ref,
                 rb_ref, kb_ref, vb_ref, kkb_ref, ab_ref, lwb_ref, of_ref, ob_ref, s_ref, *, ng, nb):
    @pl.when(pl.program_id(1) == 0)
    def _():
        s_ref[...] = jnp.zeros_like(s_ref)

    gi = lax.broadcasted_iota(jnp.int32, (GW, GW), 0) // HEAD
    gj = lax.broadcasted_iota(jnp.int32, (GW, GW), 1) // HEAD
    bdm = (gi == gj).astype(F32).astype(BF16)
    dirs = ((rf_ref, kf_ref, vf_ref, kkf_ref, af_ref, lwf_ref), (rb_ref, kb_ref, vb_ref, kkb_ref, ab_ref, lwb_ref))
    chains = [(i, d, h) for i in range(nb) for d in range(2) for h in range(ng)]
    ins = [[dirs[d][q][i, :, h * GW:(h + 1) * GW].astype(F32) for i, d, h in chains] for q in range(6)]
    s1, o = _scan_chunks([s_ref[i, d, h] for i, d, h in chains], *ins, [d == 1 for _, d, _ in chains], bdm)
    for (i, d, h), s_new, o_new in zip(chains, s1, o):
        (of_ref, ob_ref)[d][i, :, h * GW:(h + 1) * GW] = o_new
        s_ref[i, d, h] = s_new


SCAN_BATCH_ROWS = 4


def _wkv_scan(r, v, kk, kf, af, lwf, kb, ab, lwb):
    b, t, c = r.shape
    nc = t // CHUNK
    ng = c // GW
    nb = math.gcd(b, SCAN_BATCH_ROWS)
    fspec = pl.BlockSpec((nb, CHUNK, c), lambda i, j: (i, j, 0))
    bspec = pl.BlockSpec((nb, CHUNK, c), lambda i, j: (i, nc - 1 - j, 0))
    return pl.pallas_call(
        functools.partial(_scan_kernel, ng=ng, nb=nb),
        out_shape=(jax.ShapeDtypeStruct((b, t, c), F32), jax.ShapeDtypeStruct((b, t, c), F32)),
        grid=(b // nb, nc),
        in_specs=[fspec] * 6 + [bspec] * 6,
        out_specs=(fspec, bspec),
        scratch_shapes=[pltpu.VMEM((nb, 2, ng, HEAD, GW), F32)],
        compiler_params=_params(("parallel", "arbitrary")),
        name="rw_scan",
    )(r, kf, v, kk, af, lwf, r, kb, v, kk, ab, lwb)


def _out_proj_kernel(yh_ref, sf_ref, sb_ref, bon_ref, g_ref, lg_ref, lb_ref, avg_ref, wh_ref, wr_ref, r_ref,
                     o_ref):
    s = sf_ref[...] + sb_ref[...]
    mean = _dot2(s, avg_ref[...])
    d = s - mean
    var = _dot((d * d).astype(BF16), avg_ref[...])
    y = d * lax.rsqrt(var + GN_EPS) * lg_ref[...] + lb_ref[...] + bon_ref[...].astype(F32)
    yr = (y * g_ref[...].astype(F32)).astype(BF16)
    o_ref[...] = _dot(yh_ref[...].astype(BF16), wh_ref[...]) + _dot(yr, wr_ref[...]) + r_ref[...]


def _out_proj(yh, sf, sb, bon, g, lnx_g, lnx_b, w, res, tm=512):
    m, ch = yh.shape
    cr = sf.shape[1]
    d = w.shape[1]
    tm = min(tm, m)
    head = jnp.arange(cr) // HEAD
    avg = ((head[:, None] == head[None, :]).astype(F32) / HEAD).astype(BF16)
    row = lambda c: pl.BlockSpec((tm, c), lambda i: (i, 0))
    return pl.pallas_call(
        _out_proj_kernel,
        out_shape=jax.ShapeDtypeStruct((m, d), F32),
        grid=(m // tm,),
        in_specs=[row(ch), row(cr), row(cr), row(cr), row(cr), _const_spec((1, cr)), _const_spec((1, cr)),
                  _const_spec((cr, cr)), _const_spec((ch, d)), _const_spec((cr, d)), row(d)],
        out_specs=row(d),
        compiler_params=_params(("parallel",)),
        name="out_proj",
    )(yh, sf, sb, bon, g, lnx_g.reshape(1, cr), lnx_b.reshape(1, cr), avg, w[:ch], w[ch:], res)


def _rwkv_branch(p, v_first, shift, w0, w2, a0, a2, g2, k_k, k_a, r_k, vres, c):
    r, v, kk, lwf, lwb, kf, kb, af, ab, g, bon = _rw_prep(
        p, v_first, shift, w0, w2, a0, a2, g2, k_k, k_a, r_k, vres, c)
    sf, sb = _wkv_scan(r, v, kk, kf, af, lwf, kb, ab, lwb)
    return sf, sb, bon, g, (v if vres is None else v_first)


def _forward(x, norm1_g, w_in, hy_conv_w, hy_conv_b, hy_f_w1, hy_f_b1, hy_f_w2, hy_f_b2, hy_f_w3, hy_f_b3,
             hy_f_freq, hy_f_wout, hy_skip, hy_norm_g, rw_shift, rw_w0, rw_w2, rw_a0, rw_a2, rw_g2, rw_k_k,
             rw_k_a, rw_r_k, rw_lnx_g, rw_lnx_b, rw_v0, rw_v1, rw_v2, w_out, norm2_g, mlp_w1, mlp_w2,
             final_g):
    b, t, d = x.shape
    depth = w_in.shape[0]
    c_hy = hy_norm_g.shape[1]
    c_rw = rw_lnx_g.shape[1]
    hy_proj = hy_conv_b.shape[1]
    assert b % 2 == 0 and t % CHUNK == 0 and CHUNK == HEAD
    tabs = _dft_tables(t)
    assert tabs["n2"] % SUB == 0 and tabs["n1"] % SUB == 0
    v_first = None
    for l in range(depth):
        w_l = w_in[l].astype(BF16)
        w_rw = w_l[:, hy_proj:]
        vres = None if l == 0 else (rw_v0[l - 1], rw_v1[l - 1], rw_v2[l - 1])
        if vres is not None:
            v1 = vres[1].astype(BF16)
            w_rw = jnp.concatenate([w_rw, jnp.pad(v1, ((0, 0), (0, -v1.shape[1] % 128)))], axis=1)
        p_hy, p_rw = _in_proj(x, norm1_g[l], w_l[:, :hy_proj], w_rw, tabs["n1h"], tabs["n2"])
        y_hy = _hyena_branch(p_hy, hy_conv_w[l], hy_conv_b[l], hy_f_w1[l], hy_f_b1[l], hy_f_w2[l], hy_f_b2[l],
                             hy_f_w3[l], hy_f_b3[l], hy_f_freq[l], hy_f_wout[l], hy_skip[l], hy_norm_g[l], tabs)
        sf, sb, bon, g, v_first = _rwkv_branch(p_rw, v_first, rw_shift[l], rw_w0[l], rw_w2[l], rw_a0[l], rw_a2[l],
                                               rw_g2[l], rw_k_k[l], rw_k_a[l], rw_r_k[l], vres, c_rw)
        flat = lambda a: a.reshape(b * t, a.shape[-1])
        x = _out_proj(flat(y_hy), flat(sf), flat(sb), flat(bon), flat(g), rw_lnx_g[l], rw_lnx_b[l],
                      w_out[l].astype(BF16), flat(x))
        x = _mlp(x, norm2_g[l], mlp_w1[l].astype(BF16), mlp_w2[l].astype(BF16),
                 final_g=final_g if l == depth - 1 else None).reshape(b, t, d)
    return x


def kernel(x, norm1_g, w_in, hy_conv_w, hy_conv_b, hy_f_w1, hy_f_b1, hy_f_w2, hy_f_b2, hy_f_w3, hy_f_b3,
           hy_f_freq, hy_f_wout, hy_skip, hy_norm_g, rw_shift, rw_w0, rw_w2, rw_a0, rw_a2, rw_g2, rw_k_k,
           rw_k_a, rw_r_k, rw_lnx_g, rw_lnx_b, rw_v0, rw_v1, rw_v2, w_out, norm2_g, mlp_w1, mlp_w2, final_g):
    return _forward(x, norm1_g, w_in, hy_conv_w, hy_conv_b, hy_f_w1, hy_f_b1, hy_f_w2, hy_f_b2, hy_f_w3,
                    hy_f_b3, hy_f_freq, hy_f_wout, hy_skip, hy_norm_g, rw_shift, rw_w0, rw_w2, rw_a0, rw_a2,
                    rw_g2, rw_k_k, rw_k_a, rw_r_k, rw_lnx_g, rw_lnx_b, rw_v0, rw_v1, rw_v2, w_out, norm2_g,
                    mlp_w1, mlp_w2, final_g)
```

```python
import functools
import math

import jax
import jax.numpy as jnp
from jax import lax
from jax.experimental import pallas as pl
from jax.experimental.pallas import tpu as pltpu

F32 = jnp.float32
BF16 = jnp.bfloat16
HP = lax.Precision.HIGHEST

HEAD = 64
HEADS_PER_GROUP = 4
GW = HEAD * HEADS_PER_GROUP
CHUNK = 64
RMS_EPS = 1e-5
GN_EPS = HEAD * 1e-5
HY_TARGET = 1e-2
HY_FAST_PCT = 0.3
HY_SLOW_PCT = 1.5
VMEM_LIMIT = 56 * 1024 * 1024


def _params(sem, vmem=VMEM_LIMIT):
    return pltpu.CompilerParams(dimension_semantics=sem, vmem_limit_bytes=vmem)


def _dot(a, b, precision=None):
    return jnp.dot(a, b, preferred_element_type=F32, precision=precision)


def _dot_nt(a, b, precision=None):
    return lax.dot_general(a, b, (((1,), (1,)), ((), ())), preferred_element_type=F32,
                           precision=precision)


def _dot_tn(a, b, precision=None):
    return lax.dot_general(a, b, (((0,), (0,)), ((), ())), preferred_element_type=F32,
                           precision=precision)


def _const_spec(shape):
    return pl.BlockSpec(shape, lambda *_: (0,) * len(shape), pipeline_mode=pl.Buffered(1))


def _dot2(x, m):
    hi = x.astype(BF16)
    lo = (x - hi.astype(F32)).astype(BF16)
    return _dot(hi, m) + _dot(lo, m)


def _mlp_kernel(x_ref, g_ref, w1_ref, w2_ref, *rest, nchunk):
    x = x_ref[...]
    hn = (x * lax.rsqrt(jnp.mean(x * x, axis=-1, keepdims=True) + RMS_EPS) * g_ref[...]).astype(BF16)
    ff = w1_ref.shape[1]
    cw = ff // nchunk
    acc = x
    for j in range(nchunk):
        h = _dot(hn, w1_ref[:, j * cw:(j + 1) * cw])
        h = jnp.square(jnp.maximum(h, 0.0)).astype(BF16)
        acc = acc + _dot(h, w2_ref[j * cw:(j + 1) * cw, :])
    if len(rest) == 2:
        gf_ref, o_ref = rest
        acc = acc * lax.rsqrt(jnp.mean(acc * acc, axis=-1, keepdims=True) + RMS_EPS) * gf_ref[...]
    else:
        (o_ref,) = rest
    o_ref[...] = acc


def _mlp(x2d, g, w1, w2, final_g=None, tm=512, nchunk=4):
    m, d = x2d.shape
    ff = w1.shape[1]
    tm = min(tm, m)
    in_specs = [pl.BlockSpec((tm, d), lambda i: (i, 0)), _const_spec((1, d)),
                _const_spec((d, ff)), _const_spec((ff, d))]
    args = [x2d, g.reshape(1, d), w1, w2]
    if final_g is not None:
        in_specs.append(_const_spec((1, d)))
        args.append(final_g.reshape(1, d))
    return pl.pallas_call(
        functools.partial(_mlp_kernel, nchunk=nchunk),
        out_shape=jax.ShapeDtypeStruct((m, d), F32),
        grid=(m // tm,),
        in_specs=in_specs,
        out_specs=pl.BlockSpec((tm, d), lambda i: (i, 0)),
        compiler_params=_params(("parallel",)),
        name="mlp",
    )(*args)


SUB = 8
SUB16 = 16


def _rms(x, g):
    return (x * lax.rsqrt(jnp.mean(x * x, axis=-1, keepdims=True) + RMS_EPS) * g).astype(BF16)


def _in_proj_kernel(x_ref, g_ref, wh_ref, wr_ref, oh_ref, or_ref, *, n2):
    hn = _rms(x_ref[0], g_ref[...])
    or_ref[0] = _dot(hn, wr_ref[...]).astype(or_ref.dtype)
    ph = _dot(hn, wh_ref[...])
    for j in range(SUB):
        oh_ref[0, :, j, :] = ph[j * n2:(j + 1) * n2]


def _in_proj(x, g, w_hy, w_rw, n1h, n2):
    b, t, d = x.shape
    ph, pr = w_hy.shape[1], w_rw.shape[1]
    tm = SUB * n2
    p_hy, p_rw = pl.pallas_call(
        functools.partial(_in_proj_kernel, n2=n2),
        out_shape=(jax.ShapeDtypeStruct((b, n2, n1h, ph), F32), jax.ShapeDtypeStruct((b, t, pr), BF16)),
        grid=(b, t // tm),
        in_specs=[pl.BlockSpec((1, tm, d), lambda i, j: (i, j, 0)), _const_spec((1, d)),
                  _const_spec((d, ph)), _const_spec((d, pr))],
        out_specs=(pl.BlockSpec((1, n2, SUB, ph), lambda i, j: (i, 0, j, 0)),
                   pl.BlockSpec((1, tm, pr), lambda i, j: (i, j, 0))),
        compiler_params=_params(("parallel", "parallel")),
        name="in_proj",
    )(x, g.reshape(1, d), w_hy, w_rw)
    return p_hy.reshape(b, t, ph), p_rw


def _shortconv_kernel(p_ref, w_ref, b_ref, o_ref, *, n1h, n2):
    w0 = w_ref[0:1, :]
    w1 = w_ref[1:2, :]
    w2 = w_ref[2:3, :]
    bias = b_ref[...]
    rows = lax.broadcasted_iota(jnp.int32, (n1h, 1), 0)

    def blk(i):
        return p_ref[0, pl.ds(pl.multiple_of(i * n1h, n1h), n1h), :].astype(F32)

    def body(i, carry):
        o_ref[0, pl.ds(pl.multiple_of(i * n1h, n1h), n1h), :] = (
            w0 * blk(i - 1) + w1 * blk(i) + w2 * blk(i + 1) + bias).astype(o_ref.dtype)
        return carry

    lax.fori_loop(1, n2 - 1, body, 0)
    last = p_ref[0, (n2 - 1) * n1h:, :].astype(F32)
    first = p_ref[0, :n1h, :].astype(F32)
    prev0 = jnp.where(rows == 0, 0.0, pltpu.roll(last, 1, 0))
    o_ref[0, :n1h, :] = (w0 * prev0 + w1 * first + w2 * p_ref[0, n1h:2 * n1h, :].astype(F32)
                         + bias).astype(o_ref.dtype)
    nxt = jnp.where(rows == n1h - 1, 0.0, pltpu.roll(first, n1h - 1, 0))
    o_ref[0, (n2 - 1) * n1h:, :] = (w0 * p_ref[0, (n2 - 2) * n1h:(n2 - 1) * n1h, :].astype(F32) + w1 * last
                                    + w2 * nxt + bias).astype(o_ref.dtype)


def _shortconv(p, w, bias, n1h, n2, cb=128):
    b, t, c = p.shape
    return pl.pallas_call(
        functools.partial(_shortconv_kernel, n1h=n1h, n2=n2),
        out_shape=jax.ShapeDtypeStruct((b, t, c), BF16),
        grid=(b, c // cb),
        in_specs=[pl.BlockSpec((1, t, cb), lambda i, j: (i, 0, j)),
                  pl.BlockSpec((3, cb), lambda i, j: (0, j)),
                  pl.BlockSpec((1, cb), lambda i, j: (0, j))],
        out_specs=pl.BlockSpec((1, t, cb), lambda i, j: (i, 0, j)),
        compiler_params=_params(("parallel", "parallel")),
        name="hy_shortconv",
    )(p, w, bias.reshape(1, c))


def _dft_tables(t):
    n = 2 * t
    n1 = 1 << ((n.bit_length() - 1) // 2)
    n2 = n // n1
    n1h = n1 // 2
    two_pi = 2.0 * math.pi
    k1 = jnp.arange(n1, dtype=jnp.int32)
    n2i = jnp.arange(n2, dtype=jnp.int32)

    def cs(prod, mod):
        ang = (two_pi / mod) * (prod % mod).astype(F32)
        return jnp.cos(ang), jnp.sin(ang)

    tpos = n2i[:, None] + n2 * jnp.arange(n1h, dtype=jnp.int32)[None, :]
    c, s = cs(k1[None, :, None] * tpos[:, None, :], n)
    g1 = jnp.concatenate([jnp.concatenate([c, s], axis=2), jnp.concatenate([-s, c], axis=2)], axis=1)
    ct, st = jnp.swapaxes(c, 1, 2) / n, jnp.swapaxes(s, 1, 2) / n
    g4a = jnp.concatenate([ct, st], axis=1)
    g4b = jnp.concatenate([-st, ct], axis=1)
    tposf = n2i[:, None] + n2 * jnp.arange(n1, dtype=jnp.int32)[None, :]
    cf, sf = cs(k1[None, :, None] * tposf[:, None, :], n)
    g1f = jnp.concatenate([cf, -sf], axis=1)
    c2, s2 = cs(n2i[:, None] * n2i[None, :], n2)
    f2a = jnp.concatenate([c2, -s2], axis=0)
    f2b = jnp.concatenate([s2, c2], axis=0)
    f3 = jnp.concatenate([jnp.concatenate([c2, -s2], axis=1), jnp.concatenate([s2, c2], axis=1)], axis=0)
    g4 = jnp.concatenate([g4a, g4b], axis=2)
    f2 = jnp.concatenate([f2a, f2b], axis=1)
    return dict(n1=n1, n2=n2, n1h=n1h, g1=g1.astype(BF16), g4=g4.astype(BF16), g1f=g1f.astype(BF16),
                f2a=f2a.astype(BF16), f2b=f2b.astype(BF16), f2=f2.astype(BF16), f3=f3.astype(BF16))


def _filter_mlp_kernel(z_ref, w1_ref, b1_ref, w2_ref, b2_ref, w3_ref, b3_ref, fr_ref, wo_ref, dl_ref,
                       o_ref, *, t):
    z = z_ref[...]
    h = jnp.sin(fr_ref[0:1, :] * (_dot(z, w1_ref[...], HP) + b1_ref[...]))
    h = jnp.sin(fr_ref[1:2, :] * (_dot(h, w2_ref[...], HP) + b2_ref[...]))
    h = jnp.sin(fr_ref[2:3, :] * (_dot(h, w3_ref[...], HP) + b3_ref[...]))
    window = jnp.exp(-z[:, 0:1] * dl_ref[...])
    tr = z.shape[0]
    row = pl.program_id(0) * tr + lax.broadcasted_iota(jnp.int32, (tr, 1), 0)
    o_ref[...] = jnp.where(row == t, 0.0, _dot(h, wo_ref[0], HP) * window)


def _filter_taps(t, w1, b1, w2, b2, w3, b3, freq, w_out, c_hy, tr=512):
    emb, width = w1.shape
    bands = (emb - 1) // 2
    pos = jnp.arange(t, dtype=F32)
    tt = pos / max(t - 1, 1)
    fr = jnp.linspace(1e-4, bands - 1, bands, dtype=F32)
    ang = (2.0 * math.pi / t) * pos[:, None] * fr[None, :]
    z = jnp.concatenate([tt[:, None], jnp.cos(ang), -jnp.sin(ang)], axis=-1)
    z = jnp.pad(z, ((0, 0), (0, 128 - emb)))
    w1 = jnp.pad(w1, ((0, 128 - emb), (0, 0)))
    emb = 128
    z2 = jnp.concatenate([z, z[:1], z[1:][::-1]], axis=0)
    max_decay = math.log(HY_TARGET) / HY_FAST_PCT
    min_decay = math.log(HY_TARGET) / HY_SLOW_PCT
    delta = jnp.abs(jnp.linspace(min_decay, max_decay, c_hy, dtype=F32))
    order = w_out.shape[1] // (2 * c_hy)
    nout = order * c_hy
    w_dir = w_out.reshape(width, order, 2, c_hy).transpose(2, 0, 1, 3).reshape(2, width, nout)
    delta_full = jnp.tile(delta, order).reshape(1, nout)
    tr = min(tr, t)
    nt = t // tr
    full = lambda a: pl.BlockSpec(a.shape, lambda i: (0,) * a.ndim)
    args = [z2, w1, b1.reshape(1, -1), w2, b2.reshape(1, -1), w3, b3.reshape(1, -1), freq, w_dir, delta_full]
    in_specs = [pl.BlockSpec((tr, emb), lambda i: (i, 0))] + [full(a) for a in args[1:]]
    in_specs[8] = pl.BlockSpec((1, width, nout), lambda i: (i // nt, 0, 0))
    return pl.pallas_call(
        functools.partial(_filter_mlp_kernel, t=t),
        out_shape=jax.ShapeDtypeStruct((2 * t, nout), F32),
        grid=(2 * nt,),
        in_specs=in_specs,
        out_specs=pl.BlockSpec((tr, nout), lambda i: (i, 0)),
        compiler_params=_params(("parallel",)),
        name="hy_filter_mlp",
    )(*args)


def _filter_s1_kernel(x_ref, g_ref, o_ref, ss_ref, *, g, oc):
    @pl.when(pl.program_id(0) == 0)
    def _():
        ss_ref[...] = jnp.zeros_like(ss_ref)

    n1 = x_ref.shape[0]
    x = jnp.concatenate([x_ref[:, j, :] for j in range(g)], axis=0)
    ss_ref[...] += jnp.sum(x * x, axis=0, keepdims=True)
    xb = x.astype(BF16)
    for j in range(g):
        o_ref[j] = _dot(g_ref[j], xb[j * n1:(j + 1) * n1])


def _filter_s2_kernel(sr_ref, si_ref, fa_ref, fb_ref, ss_ref, o_ref, *, g, oc):
    n2 = sr_ref.shape[0]
    scale = lax.rsqrt(ss_ref[...] + 1e-6)
    sr = jnp.concatenate([sr_ref[:, j, :] for j in range(g)], axis=0).astype(BF16)
    si = jnp.concatenate([si_ref[:, j, :] for j in range(g)], axis=0).astype(BF16)
    for j in range(g):
        x = _dot(fa_ref[...], sr[j * n2:(j + 1) * n2]) + _dot(fb_ref[...], si[j * n2:(j + 1) * n2])
        o_ref[j] = x * scale


def _filter_spectrum(two, tabs):
    n, oc = two.shape
    n1, n2 = tabs["n1"], tabs["n2"]
    g = SUB
    s1, ss = pl.pallas_call(
        functools.partial(_filter_s1_kernel, g=g, oc=oc),
        out_shape=(jax.ShapeDtypeStruct((n2, 2 * n1, oc), F32), jax.ShapeDtypeStruct((1, oc), F32)),
        grid=(n2 // g,),
        in_specs=[pl.BlockSpec((n1, g, oc), lambda j: (0, j, 0)),
                  pl.BlockSpec((g, 2 * n1, n1), lambda j: (j, 0, 0))],
        out_specs=(pl.BlockSpec((g, 2 * n1, oc), lambda j: (j, 0, 0)),
                   pl.BlockSpec((1, oc), lambda j: (0, 0))),
        compiler_params=_params(("arbitrary",)),
        name="hy_filter_dft1",
    )(two.reshape(n1, n2, oc), tabs["g1f"])
    return pl.pallas_call(
        functools.partial(_filter_s2_kernel, g=g, oc=oc),
        out_shape=jax.ShapeDtypeStruct((n1, 2 * n2, oc), F32),
        grid=(n1 // g,),
        in_specs=[pl.BlockSpec((n2, g, oc), lambda k: (0, k, 0)),
                  pl.BlockSpec((n2, g, oc), lambda k: (0, n1 // g + k, 0)),
                  pl.BlockSpec((2 * n2, n2), lambda k: (0, 0)),
                  pl.BlockSpec((2 * n2, n2), lambda k: (0, 0)),
                  pl.BlockSpec((1, oc), lambda k: (0, 0))],
        out_specs=pl.BlockSpec((g, 2 * n2, oc), lambda k: (k, 0, 0)),
        compiler_params=_params(("parallel",)),
        name="hy_filter_dft2",
    )(s1, s1, tabs["f2a"], tabs["f2b"], ss)


def _conv_s1_kernel(u_ref, g_ref, o_ref, *, g, n1h):
    for j in range(g):
        rows = slice(j * n1h, (j + 1) * n1h)
        rhs = jnp.concatenate([u_ref[0, 0, rows, :], u_ref[0, 1, rows, :]], axis=0)
        o_ref[0, j] = _dot(g_ref[j], rhs)


def _conv_s23_kernel(sr_ref, si_ref, f2_ref, f3_ref, h_ref, o_ref, *, g, n2):
    s = jnp.concatenate([ref[0, :, j, :] for j in range(g) for ref in (sr_ref, si_ref)], axis=0).astype(BF16)
    for j in range(g):
        x = _dot(f2_ref[...], s[2 * j * n2:2 * (j + 1) * n2])
        xr, xi = x[:n2], x[n2:]
        hr, hi = h_ref[j, :n2, :], h_ref[j, n2:, :]
        y = jnp.concatenate([xr * hr - xi * hi, xr * hi + xi * hr], axis=0).astype(BF16)
        o_ref[0, j] = _dot(f3_ref[...], y)


def _conv_s4_kernel(rr_ref, ri_ref, g4_ref, u_ref, gate_ref, skip_ref, *rest, g, n1h, norm):
    if norm:
        ng_ref, avg_ref, o_ref = rest
    else:
        (o_ref,) = rest
    skip = skip_ref[...]
    n1 = rr_ref.shape[1]
    r = jnp.concatenate([ref[0, :, j, :] for j in range(g) for ref in (rr_ref, ri_ref)], axis=0).astype(BF16)
    for j in range(g):
        rows = slice(j * n1h, (j + 1) * n1h)
        y = _dot(g4_ref[j], r[2 * j * n1:2 * (j + 1) * n1])
        for q in range(2):
            z = gate_ref[0, q, rows, :].astype(F32) * (y[q * n1h:(q + 1) * n1h]
                                                       + u_ref[0, q, rows, :].astype(F32) * skip)
            if norm:
                ms = _dot((z * z).astype(BF16), avg_ref[...])
                o_ref[0, q, :, j, :] = z * lax.rsqrt(ms + RMS_EPS) * ng_ref[...]
            else:
                o_ref[0, q, rows, :] = z.astype(o_ref.dtype)


def _long_conv_gate(u_arr, u_col, gate_arr, gate_col, h_spec, h_col, skip, tabs, c, norm_g=None):
    b, t, _ = u_arr.shape
    n1, n2, n1h = tabs["n1"], tabs["n2"], tabs["n1h"]
    g = SUB
    npair = b // 2
    u4 = u_arr.reshape(npair, 2, t, u_arr.shape[2])
    gate4 = gate_arr.reshape(npair, 2, t, gate_arr.shape[2])
    s1 = pl.pallas_call(
        functools.partial(_conv_s1_kernel, g=g, n1h=n1h),
        out_shape=jax.ShapeDtypeStruct((npair, n2, 2 * n1, c), F32),
        grid=(npair, n2 // g),
        in_specs=[pl.BlockSpec((1, 2, g * n1h, c), lambda p, j: (p, 0, j, u_col)),
                  pl.BlockSpec((g, 2 * n1, n1), lambda p, j: (j, 0, 0))],
        out_specs=pl.BlockSpec((1, g, 2 * n1, c), lambda p, j: (p, j, 0, 0)),
        compiler_params=_params(("parallel", "parallel")),
        name="hy_conv_dft1",
    )(u4, tabs["g1"])
    r = pl.pallas_call(
        functools.partial(_conv_s23_kernel, g=g, n2=n2),
        out_shape=jax.ShapeDtypeStruct((npair, n1, 2 * n2, c), F32),
        grid=(npair, n1 // g),
        in_specs=[pl.BlockSpec((1, n2, g, c), lambda p, k: (p, 0, k, 0)),
                  pl.BlockSpec((1, n2, g, c), lambda p, k: (p, 0, n1 // g + k, 0)),
                  pl.BlockSpec((2 * n2, 2 * n2), lambda p, k: (0, 0)),
                  pl.BlockSpec((2 * n2, 2 * n2), lambda p, k: (0, 0)),
                  pl.BlockSpec((g, 2 * n2, c), lambda p, k: (k, 0, h_col))],
        out_specs=pl.BlockSpec((1, g, 2 * n2, c), lambda p, k: (p, k, 0, 0)),
        compiler_params=_params(("parallel", "parallel")),
        name="hy_conv_dft23",
    )(s1, s1, tabs["f2"], tabs["f3"], h_spec)
    norm = norm_g is not None
    in_specs = [pl.BlockSpec((1, n1, g, c), lambda p, j: (p, 0, j, 0)),
                pl.BlockSpec((1, n1, g, c), lambda p, j: (p, 0, n2 // g + j, 0)),
                pl.BlockSpec((g, n1, 2 * n1), lambda p, j: (j, 0, 0)),
                pl.BlockSpec((1, 2, g * n1h, c), lambda p, j: (p, 0, j, u_col)),
                pl.BlockSpec((1, 2, g * n1h, c), lambda p, j: (p, 0, j, gate_col)),
                pl.BlockSpec((1, c), lambda p, j: (0, 0))]
    args = [r, r, tabs["g4"], u4, gate4, skip.reshape(1, c)]
    if norm:
        ch = jnp.arange(c) // HEAD
        avg = ((ch[:, None] == ch[None, :]).astype(F32) / HEAD).astype(BF16)
        in_specs += [pl.BlockSpec((1, c), lambda p, j: (0, 0)), pl.BlockSpec((c, c), lambda p, j: (0, 0))]
        args += [norm_g.reshape(1, c), avg]
        out_shape = jax.ShapeDtypeStruct((npair, 2, n1h, n2, c), F32)
        out_spec = pl.BlockSpec((1, 2, n1h, g, c), lambda p, j: (p, 0, 0, j, 0))
    else:
        out_shape = jax.ShapeDtypeStruct((npair, 2, t, c), BF16)
        out_spec = pl.BlockSpec((1, 2, g * n1h, c), lambda p, j: (p, 0, j, 0))
    out = pl.pallas_call(
        functools.partial(_conv_s4_kernel, g=g, n1h=n1h, norm=norm),
        out_shape=out_shape,
        grid=(npair, n2 // g),
        in_specs=in_specs,
        out_specs=out_spec,
        compiler_params=_params(("parallel", "parallel")),
        name="hy_conv_dft4",
    )(*args)
    return out.reshape(b, t, c)


def _hyena_branch(p, conv_w, conv_b, fw1, fb1, fw2, fb2, fw3, fb3, ffreq, fwout, skip, norm_g, tabs):
    b, t, _ = p.shape
    c = norm_g.shape[0]
    n1, n2, n1h = tabs["n1"], tabs["n2"], tabs["n1h"]
    u = _shortconv(p, conv_w, conv_b, n1h, n2)
    two = _filter_taps(t, fw1, fb1, fw2, fb2, fw3, fb3, ffreq, fwout, c)
    hspec = _filter_spectrum(two, tabs)
    z = _long_conv_gate(u, 2, u, 0, hspec, 0, skip[0], tabs, c)
    return _long_conv_gate(z, 0, u, 1, hspec, 1, skip[1], tabs, c, norm_g=norm_g)


def _rw_prep_kernel(*refs, has_vres, c):
    if has_vres:
        (p_ref, pp_ref, pn_ref, mu_ref, w0_ref, w2_ref, a0_ref, a2_ref, g2_ref, kk_ref, ka_ref, rk_ref,
         sum_ref, vf_ref, v0_ref, v2_ref,
         r_o, v_o, kk_o, lwf_o, lwb_o, kf_o, kb_o, af_o, ab_o, g_o, bon_o) = refs
    else:
        (p_ref, pp_ref, pn_ref, mu_ref, w0_ref, w2_ref, a0_ref, a2_ref, g2_ref, kk_ref, ka_ref, rk_ref,
         sum_ref,
         r_o, v_o, kk_o, lwf_o, lwb_o, kf_o, kb_o, af_o, ab_o, g_o, bon_o) = refs
    i = pl.program_id(1)
    last = pl.num_programs(1) - 1
    nsh = mu_ref.shape[1]
    p = p_ref[0, :, :nsh].astype(F32)
    tt = p.shape[0]
    rows = lax.broadcasted_iota(jnp.int32, (tt, 1), 0)
    prev_row = jnp.where(i == 0, 0.0, pp_ref[0, SUB16 - 1:SUB16, :nsh].astype(F32))
    next_row = jnp.where(i == last, 0.0, pn_ref[0, 0:1, :nsh].astype(F32))
    prev = jnp.where(rows == 0, prev_row, pltpu.roll(p, 1, 0))
    nxt = jnp.where(rows == tt - 1, next_row, pltpu.roll(p, tt - 1, 0))
    mu0, mu1 = mu_ref[0:1, :], mu_ref[1:2, :]
    pf = p * (1.0 - mu0 - mu1) + mu0 * prev + mu1 * nxt
    r = pf[:, :c]
    k = pf[:, c:2 * c]
    v = pf[:, 2 * c:3 * c]
    lw = 3 * c
    nd = w2_ref.shape[0]
    na = a2_ref.shape[0]
    wd = pf[:, lw:lw + nd]
    ad = pf[:, lw + nd:lw + nd + na]
    gd = pf[:, lw + nd + na:]
    if has_vres:
        lora = _dot(p_ref[0, :, nsh:], v2_ref[...])
        v = v + (vf_ref[0].astype(F32) - v) * jax.nn.sigmoid(v0_ref[...] + lora)
    g = _dot(jax.nn.sigmoid(gd).astype(BF16), g2_ref[...])
    kk = k * kk_ref[...]
    kk = kk * lax.rsqrt(jnp.maximum(_dot((kk * kk).astype(BF16), sum_ref[...]), 1e-24))
    wl = w0_ref[...] + _dot(jnp.tanh(wd).astype(BF16), w2_ref[...])
    logw = -math.exp(-0.5) * jax.nn.sigmoid(wl)
    a = jax.nn.sigmoid(a0_ref[...] + _dot(ad.astype(BF16), a2_ref[...]))
    ka = ka_ref[...]
    rk = rk_ref[...]
    k_d = [k * (1.0 + (a[:, d * c:(d + 1) * c] - 1.0) * ka) for d in range(2)]
    bon = _dot((r * (k_d[0] + k_d[1]) * rk).astype(BF16), sum_ref[...]) * v
    r_o[0] = r.astype(BF16)
    v_o[0] = v.astype(BF16)
    kk_o[0] = kk.astype(BF16)
    lwf_o[0] = logw[:, :c]
    lwb_o[0] = logw[:, c:]
    kf_o[0] = k_d[0].astype(BF16)
    kb_o[0] = k_d[1].astype(BF16)
    af_o[0] = a[:, :c].astype(BF16)
    ab_o[0] = a[:, c:].astype(BF16)
    g_o[0] = g.astype(BF16)
    bon_o[0] = bon.astype(BF16)


def _blockdiag2(m):
    k, c = m.shape[1], m.shape[2]
    z = jnp.zeros((k, c), m.dtype)
    return jnp.concatenate([jnp.concatenate([m[0], z], axis=1), jnp.concatenate([z, m[1]], axis=1)], axis=0)


def _rw_prep(p, v_first, shift, w0, w2, a0, a2, g2, k_k, k_a, r_k, vres, c, tt=256):
    b, t, pw = p.shape
    tt = min(tt, t)
    has_vres = vres is not None
    ch = jnp.arange(c) // HEAD
    summ = (ch[:, None] == ch[None, :]).astype(BF16)
    row = lambda a: a.reshape(1, -1)
    args = [p, p, p, shift, row(w0), _blockdiag2(w2).astype(BF16), row(a0), _blockdiag2(a2).astype(BF16),
            g2.astype(BF16), row(k_k), row(k_a), row(r_k), summ]
    full = lambda a: pl.BlockSpec(a.shape, lambda i, j: (0,) * a.ndim)
    nhb = t // SUB16
    in_specs = [pl.BlockSpec((1, tt, pw), lambda i, j: (i, j, 0)),
                pl.BlockSpec((1, SUB16, pw), lambda i, j: (i, jnp.maximum(j * (tt // SUB16) - 1, 0), 0)),
                pl.BlockSpec((1, SUB16, pw), lambda i, j: (i, jnp.minimum((j + 1) * (tt // SUB16), nhb - 1), 0))]
    in_specs += [full(a) for a in args[3:]]
    if has_vres:
        v0, _, v2 = vres
        v2p = jnp.pad(v2, ((0, pw - shift.shape[1] - v2.shape[0]), (0, 0))).astype(BF16)
        extra = [v_first, row(v0), v2p]
        in_specs += [pl.BlockSpec((1, tt, c), lambda i, j: (i, j, 0))] + [full(a) for a in extra[1:]]
        args += extra
    outs = tuple(jax.ShapeDtypeStruct((b, t, c), F32 if i in (3, 4) else BF16) for i in range(11))
    return pl.pallas_call(
        functools.partial(_rw_prep_kernel, has_vres=has_vres, c=c),
        out_shape=outs,
        grid=(b, t // tt),
        in_specs=in_specs,
        out_specs=tuple(pl.BlockSpec((1, tt, c), lambda i, j: (i, j, 0)) for _ in range(11)),
        compiler_params=_params(("parallel", "parallel")),
        name="rw_prep",
    )(*args)


def _split3(x):
    h1 = x.astype(BF16)
    r1 = x - h1.astype(F32)
    h2 = r1.astype(BF16)
    h3 = (r1 - h2.astype(F32)).astype(BF16)
    return h1, h2, h3


def _scan_chunks(s0, r, k, v, kk, a, lw, rev, bdm):
    L = CHUNK
    n = len(r)
    each = lambda f, *ls: [f(*xs) for xs in zip(*ls)]
    ti = lax.broadcasted_iota(jnp.int32, (L, L), 0)
    si = lax.broadcasted_iota(jnp.int32, (L, L), 1)
    tt = lax.broadcasted_iota(jnp.int32, (L, GW), 0)
    ss = lax.broadcasted_iota(jnp.int32, (L, GW), 1) % L
    eye = (ss == tt).astype(F32)
    tri = [((si >= ti) if q else (si <= ti)).astype(F32).astype(BF16) for q in rev]
    strict = [(ss > tt) if q else (ss < tt) for q in rev]
    incl = [(ss >= tt) if q else (ss <= tt) for q in rev]
    bdf = bdm.astype(F32)

    def bd(x):
        xb = x.astype(BF16)
        return jnp.concatenate([xb] * HEADS_PER_GROUP, axis=0) * bdm

    def bd_t(x):
        xt = x.T.astype(BF16)
        return jnp.concatenate([xt] * HEADS_PER_GROUP, axis=1) * bdm

    def rcmul(x, y):
        return _dot(x.astype(BF16), bd(y))

    parts = each(_split3, lw)
    cum = [_dot(t, p[0]) + _dot(t, p[1]) + _dot(t, p[2]) for t, p in zip(tri, parts)]
    tot = [c[0:1, :] if q else c[L - 1:L, :] for c, q in zip(cum, rev)]
    p_in = each(jnp.exp, cum)
    p_inv = each(lambda c: jnp.exp(-c), cum)
    p_ex = each(lambda c, w: jnp.exp(c - w), cum, lw)
    p_rem = each(lambda t, c: jnp.exp(t - c), tot, cum)
    beta = each(lambda x, y: x * y, kk, a)
    a_t = each(lambda x, p: -x * p, kk, p_ex)
    r_t = each(lambda x, p: x * p, r, p_in)
    k_t = each(lambda x, p: x * p, k, p_inv)
    b_t = each(lambda x, p: x * p, beta, p_inv)
    k_h = each(lambda x, p: x * p, k, p_rem)
    b_h = each(lambda x, p: x * p, beta, p_rem)

    lhs = each(lambda x, y: jnp.concatenate([x, y], axis=0).astype(BF16), a_t, r_t)
    akk = each(lambda x, y: _dot(x, bd_t(y)), lhs, k_t)
    abb = each(lambda x, y: _dot(x, bd_t(y)), lhs, b_t)
    a_ak = each(lambda m, x: jnp.where(m, x[:L], 0.0), strict, akk)
    a_rk = each(lambda m, x: jnp.where(m, x[L:], 0.0), incl, akk)
    a_ab = each(lambda m, x: jnp.where(m, x[:L], 0.0), strict, abb)
    a_rb = each(lambda m, x: jnp.where(m, x[L:], 0.0), incl, abb)

    pw = a_ab
    tinv = each(lambda x: eye + x, pw)
    pw = each(rcmul, pw, pw)
    for _ in range(4):
        both = each(lambda p, t: rcmul(jnp.concatenate([p, t], axis=0), p), pw, tinv)
        pw = [x[:L] for x in both]
        tinv = each(lambda t, x: t + x[L:], tinv, both)
    tinv = each(lambda t, p: t + rcmul(t, p), tinv, pw)

    av = each(lambda x, y, z: rcmul(jnp.concatenate([x, y], axis=0), z), a_ak, a_rk, v)
    ct = each(rcmul, a_rb, tinv)
    wu2 = each(lambda t, c, x, y: _dot(jnp.concatenate([t, c], axis=0).astype(BF16),
                                       jnp.concatenate([bd(x), bd(y[:L])], axis=1)), tinv, ct, a_t, av)
    wu = [x[:L] for x in wu2]
    rbwu = [x[L:] for x in wu2]
    o = each(lambda x, w, s, y: _dot((x + w[:, :GW]).astype(BF16), bd_t(s)) + y[L:] + w[:, GW:],
             r_t, rbwu, s0, av)

    m_bd = each(lambda w, x: _dot_tn(w[:, :GW].astype(BF16), x.astype(BF16)) * bdf, wu, b_h)
    n_full = each(lambda x, w, y, z: _dot_tn(jnp.concatenate([x, w[:, GW:]], axis=0).astype(BF16),
                                             jnp.concatenate([y, z], axis=0).astype(BF16)) * bdf,
                  v, wu, k_h, b_h)
    s1 = []
    for i in range(n):
        n_rc = n_full[i][0:HEAD]
        for h in range(1, HEADS_PER_GROUP):
            n_rc = n_rc + n_full[i][h * HEAD:(h + 1) * HEAD]
        s1.append(_dot(s0[i].astype(BF16), m_bd[i].astype(BF16)) + s0[i] * jnp.exp(tot[i]) + n_rc)
    return s1, o


def _scan_kernel(rf_ref, kf_ref, vf_ref, kkf_ref, af_ref, lwf_ref,
                 rb_ref, kb_ref, vb_ref, kkb_ref, ab_ref, lwb_ref, of_ref, ob_ref, s_ref, *, ng, nb):
    @pl.when(pl.program_id(1) == 0)
    def _():
        s_ref[...] = jnp.zeros_like(s_ref)

    gi = lax.broadcasted_iota(jnp.int32, (GW, GW), 0) // HEAD
    gj = lax.broadcasted_iota(jnp.int32, (GW, GW), 1) // HEAD
    bdm = (gi == gj).astype(F32).astype(BF16)
    dirs = ((rf_ref, kf_ref, vf_ref, kkf_ref, af_ref, lwf_ref), (rb_ref, kb_ref, vb_ref, kkb_ref, ab_ref, lwb_ref))
    chains = [(i, d, h) for i in range(nb) for d in range(2) for h in range(ng)]
    ins = [[dirs[d][q][i, :, h * GW:(h + 1) * GW].astype(F32) for i, d, h in chains] for q in range(6)]
    s1, o = _scan_chunks([s_ref[i, d, h] for i, d, h in chains], *ins, [d == 1 for _, d, _ in chains], bdm)
    for (i, d, h), s_new, o_new in zip(chains, s1, o):
        (of_ref, ob_ref)[d][i, :, h * GW:(h + 1) * GW] = o_new
        s_ref[i, d, h] = s_new


SCAN_BATCH_ROWS = 4


def _wkv_scan(r, v, kk, kf, af, lwf, kb, ab, lwb):
    b, t, c = r.shape
    nc = t // CHUNK
    ng = c // GW
    nb = math.gcd(b, SCAN_BATCH_ROWS)
    fspec = pl.BlockSpec((nb, CHUNK, c), lambda i, j: (i, j, 0))
    bspec = pl.BlockSpec((nb, CHUNK, c), lambda i, j: (i, nc - 1 - j, 0))
    return pl.pallas_call(
        functools.partial(_scan_kernel, ng=ng, nb=nb),
        out_shape=(jax.ShapeDtypeStruct((b, t, c), F32), jax.ShapeDtypeStruct((b, t, c), F32)),
        grid=(b // nb, nc),
        in_specs=[fspec] * 6 + [bspec] * 6,
        out_specs=(fspec, bspec),
        scratch_shapes=[pltpu.VMEM((nb, 2, ng, HEAD, GW), F32)],
        compiler_params=_params(("parallel", "arbitrary")),
        name="rw_scan",
    )(r, kf, v, kk, af, lwf, r, kb, v, kk, ab, lwb)


def _out_proj_kernel(yh_ref, sf_ref, sb_ref, bon_ref, g_ref, lg_ref, lb_ref, avg_ref, wh_ref, wr_ref, r_ref,
                     o_ref):
    s = sf_ref[...] + sb_ref[...]
    mean = _dot2(s, avg_ref[...])
    d = s - mean
    var = _dot((d * d).astype(BF16), avg_ref[...])
    y = d * lax.rsqrt(var + GN_EPS) * lg_ref[...] + lb_ref[...] + bon_ref[...].astype(F32)
    yr = (y * g_ref[...].astype(F32)).astype(BF16)
    o_ref[...] = _dot(yh_ref[...].astype(BF16), wh_ref[...]) + _dot(yr, wr_ref[...]) + r_ref[...]


def _out_proj(yh, sf, sb, bon, g, lnx_g, lnx_b, w, res, tm=512):
    m, ch = yh.shape
    cr = sf.shape[1]
    d = w.shape[1]
    tm = min(tm, m)
    head = jnp.arange(cr) // HEAD
    avg = ((head[:, None] == head[None, :]).astype(F32) / HEAD).astype(BF16)
    row = lambda c: pl.BlockSpec((tm, c), lambda i: (i, 0))
    return pl.pallas_call(
        _out_proj_kernel,
        out_shape=jax.ShapeDtypeStruct((m, d), F32),
        grid=(m // tm,),
        in_specs=[row(ch), row(cr), row(cr), row(cr), row(cr), _const_spec((1, cr)), _const_spec((1, cr)),
                  _const_spec((cr, cr)), _const_spec((ch, d)), _const_spec((cr, d)), row(d)],
        out_specs=row(d),
        compiler_params=_params(("parallel",)),
        name="out_proj",
    )(yh, sf, sb, bon, g, lnx_g.reshape(1, cr), lnx_b.reshape(1, cr), avg, w[:ch], w[ch:], res)


def _rwkv_branch(p, v_first, shift, w0, w2, a0, a2, g2, k_k, k_a, r_k, vres, c):
    r, v, kk, lwf, lwb, kf, kb, af, ab, g, bon = _rw_prep(
        p, v_first, shift, w0, w2, a0, a2, g2, k_k, k_a, r_k, vres, c)
    sf, sb = _wkv_scan(r, v, kk, kf, af, lwf, kb, ab, lwb)
    return sf, sb, bon, g, (v if vres is None else v_first)


def _forward(x, norm1_g, w_in, hy_conv_w, hy_conv_b, hy_f_w1, hy_f_b1, hy_f_w2, hy_f_b2, hy_f_w3, hy_f_b3,
             hy_f_freq, hy_f_wout, hy_skip, hy_norm_g, rw_shift, rw_w0, rw_w2, rw_a0, rw_a2, rw_g2, rw_k_k,
             rw_k_a, rw_r_k, rw_lnx_g, rw_lnx_b, rw_v0, rw_v1, rw_v2, w_out, norm2_g, mlp_w1, mlp_w2,
             final_g):
    b, t, d = x.shape
    depth = w_in.shape[0]
    c_hy = hy_norm_g.shape[1]
    c_rw = rw_lnx_g.shape[1]
    hy_proj = hy_conv_b.shape[1]
    assert b % 2 == 0 and t % CHUNK == 0 and CHUNK == HEAD
    tabs = _dft_tables(t)
    assert tabs["n2"] % SUB == 0 and tabs["n1"] % SUB == 0
    v_first = None
    for l in range(depth):
        w_l = w_in[l].astype(BF16)
        w_rw = w_l[:, hy_proj:]
        vres = None if l == 0 else (rw_v0[l - 1], rw_v1[l - 1], rw_v2[l - 1])
        if vres is not None:
            v1 = vres[1].astype(BF16)
            w_rw = jnp.concatenate([w_rw, jnp.pad(v1, ((0, 0), (0, -v1.shape[1] % 128)))], axis=1)
        p_hy, p_rw = _in_proj(x, norm1_g[l], w_l[:, :hy_proj], w_rw, tabs["n1h"], tabs["n2"])
        y_hy = _hyena_branch(p_hy, hy_conv_w[l], hy_conv_b[l], hy_f_w1[l], hy_f_b1[l], hy_f_w2[l], hy_f_b2[l],
                             hy_f_w3[l], hy_f_b3[l], hy_f_freq[l], hy_f_wout[l], hy_skip[l], hy_norm_g[l], tabs)
        sf, sb, bon, g, v_first = _rwkv_branch(p_rw, v_first, rw_shift[l], rw_w0[l], rw_w2[l], rw_a0[l], rw_a2[l],
                                               rw_g2[l], rw_k_k[l], rw_k_a[l], rw_r_k[l], vres, c_rw)
        flat = lambda a: a.reshape(b * t, a.shape[-1])
        x = _out_proj(flat(y_hy), flat(sf), flat(sb), flat(bon), flat(g), rw_lnx_g[l], rw_lnx_b[l],
                      w_out[l].astype(BF16), flat(x))
        x = _mlp(x, norm2_g[l], mlp_w1[l].astype(BF16), mlp_w2[l].astype(BF16),
                 final_g=final_g if l == depth - 1 else None).reshape(b, t, d)
    return x


def kernel(x, norm1_g, w_in, hy_conv_w, hy_conv_b, hy_f_w1, hy_f_b1, hy_f_w2, hy_f_b2, hy_f_w3, hy_f_b3,
           hy_f_freq, hy_f_wout, hy_skip, hy_norm_g, rw_shift, rw_w0, rw_w2, rw_a0, rw_a2, rw_g2, rw_k_k,
           rw_k_a, rw_r_k, rw_lnx_g, rw_lnx_b, rw_v0, rw_v1, rw_v2, w_out, norm2_g, mlp_w1, mlp_w2, final_g):
    return _forward(x, norm1_g, w_in, hy_conv_w, hy_conv_b, hy_f_w1, hy_f_b1, hy_f_w2, hy_f_b2, hy_f_w3,
                    hy_f_b3, hy_f_freq, hy_f_wout, hy_skip, hy_norm_g, rw_shift, rw_w0, rw_w2, rw_a0, rw_a2,
                    rw_g2, rw_k_k, rw_k_a, rw_r_k, rw_lnx_g, rw_lnx_b, rw_v0, rw_v1, rw_v2, w_out, norm2_g,
                    mlp_w1, mlp_w2, final_g)
```

```python
import functools
import math

import jax
import jax.numpy as jnp
from jax import lax
from jax.experimental import pallas as pl
from jax.experimental.pallas import tpu as pltpu

F32 = jnp.float32
BF16 = jnp.bfloat16
HP = lax.Precision.HIGHEST

HEAD = 64
HEADS_PER_GROUP = 4
GW = HEAD * HEADS_PER_GROUP
CHUNK = 64
RMS_EPS = 1e-5
GN_EPS = HEAD * 1e-5
HY_TARGET = 1e-2
HY_FAST_PCT = 0.3
HY_SLOW_PCT = 1.5
VMEM_LIMIT = 56 * 1024 * 1024


def _params(sem, vmem=VMEM_LIMIT):
    return pltpu.CompilerParams(dimension_semantics=sem, vmem_limit_bytes=vmem)


def _dot(a, b, precision=None):
    return jnp.dot(a, b, preferred_element_type=F32, precision=precision)


def _dot_nt(a, b, precision=None):
    return lax.dot_general(a, b, (((1,), (1,)), ((), ())), preferred_element_type=F32,
                           precision=precision)


def _dot_tn(a, b, precision=None):
    return lax.dot_general(a, b, (((0,), (0,)), ((), ())), preferred_element_type=F32,
                           precision=precision)


def _const_spec(shape):
    return pl.BlockSpec(shape, lambda *_: (0,) * len(shape), pipeline_mode=pl.Buffered(1))


def _dot2(x, m):
    hi = x.astype(BF16)
    lo = (x - hi.astype(F32)).astype(BF16)
    return _dot(hi, m) + _dot(lo, m)


def _mlp_kernel(x_ref, g_ref, w1_ref, w2_ref, *rest, nchunk):
    x = x_ref[...]
    hn = (x * lax.rsqrt(jnp.mean(x * x, axis=-1, keepdims=True) + RMS_EPS) * g_ref[...]).astype(BF16)
    ff = w1_ref.shape[1]
    cw = ff // nchunk
    acc = x
    for j in range(nchunk):
        h = _dot(hn, w1_ref[:, j * cw:(j + 1) * cw])
        h = jnp.square(jnp.maximum(h, 0.0)).astype(BF16)
        acc = acc + _dot(h, w2_ref[j * cw:(j + 1) * cw, :])
    if len(rest) == 2:
        gf_ref, o_ref = rest
        acc = acc * lax.rsqrt(jnp.mean(acc * acc, axis=-1, keepdims=True) + RMS_EPS) * gf_ref[...]
    else:
        (o_ref,) = rest
    o_ref[...] = acc


def _mlp(x2d, g, w1, w2, final_g=None, tm=512, nchunk=4):
    m, d = x2d.shape
    ff = w1.shape[1]
    tm = min(tm, m)
    in_specs = [pl.BlockSpec((tm, d), lambda i: (i, 0)), _const_spec((1, d)),
                _const_spec((d, ff)), _const_spec((ff, d))]
    args = [x2d, g.reshape(1, d), w1, w2]
    if final_g is not None:
        in_specs.append(_const_spec((1, d)))
        args.append(final_g.reshape(1, d))
    return pl.pallas_call(
        functools.partial(_mlp_kernel, nchunk=nchunk),
        out_shape=jax.ShapeDtypeStruct((m, d), F32),
        grid=(m // tm,),
        in_specs=in_specs,
        out_specs=pl.BlockSpec((tm, d), lambda i: (i, 0)),
        compiler_params=_params(("parallel",)),
        name="mlp",
    )(*args)


SUB = 8
SUB16 = 16


def _rms(x, g):
    return (x * lax.rsqrt(jnp.mean(x * x, axis=-1, keepdims=True) + RMS_EPS) * g).astype(BF16)


def _in_proj_kernel(x_ref, g_ref, wh_ref, wr_ref, oh_ref, or_ref, *, n2):
    hn = _rms(x_ref[0], g_ref[...])
    or_ref[0] = _dot(hn, wr_ref[...]).astype(or_ref.dtype)
    ph = _dot(hn, wh_ref[...])
    for j in range(SUB):
        oh_ref[0, :, j, :] = ph[j * n2:(j + 1) * n2]


def _in_proj(x, g, w_hy, w_rw, n1h, n2):
    b, t, d = x.shape
    ph, pr = w_hy.shape[1], w_rw.shape[1]
    tm = SUB * n2
    p_hy, p_rw = pl.pallas_call(
        functools.partial(_in_proj_kernel, n2=n2),
        out_shape=(jax.ShapeDtypeStruct((b, n2, n1h, ph), F32), jax.ShapeDtypeStruct((b, t, pr), BF16)),
        grid=(b, t // tm),
        in_specs=[pl.BlockSpec((1, tm, d), lambda i, j: (i, j, 0)), _const_spec((1, d)),
                  _const_spec((d, ph)), _const_spec((d, pr))],
        out_specs=(pl.BlockSpec((1, n2, SUB, ph), lambda i, j: (i, 0, j, 0)),
                   pl.BlockSpec((1, tm, pr), lambda i, j: (i, j, 0))),
        compiler_params=_params(("parallel", "parallel")),
        name="in_proj",
    )(x, g.reshape(1, d), w_hy, w_rw)
    return p_hy.reshape(b, t, ph), p_rw


def _shortconv_kernel(p_ref, w_ref, b_ref, o_ref, *, n1h, n2):
    w0 = w_ref[0:1, :]
    w1 = w_ref[1:2, :]
    w2 = w_ref[2:3, :]
    bias = b_ref[...]
    rows = lax.broadcasted_iota(jnp.int32, (n1h, 1), 0)

    def blk(i):
        return p_ref[0, pl.ds(pl.multiple_of(i * n1h, n1h), n1h), :].astype(F32)

    def body(i, carry):
        o_ref[0, pl.ds(pl.multiple_of(i * n1h, n1h), n1h), :] = (
            w0 * blk(i - 1) + w1 * blk(i) + w2 * blk(i + 1) + bias).astype(o_ref.dtype)
        return carry

    lax.fori_loop(1, n2 - 1, body, 0)
    last = p_ref[0, (n2 - 1) * n1h:, :].astype(F32)
    first = p_ref[0, :n1h, :].astype(F32)
    prev0 = jnp.where(rows == 0, 0.0, pltpu.roll(last, 1, 0))
    o_ref[0, :n1h, :] = (w0 * prev0 + w1 * first + w2 * p_ref[0, n1h:2 * n1h, :].astype(F32)
                         + bias).astype(o_ref.dtype)
    nxt = jnp.where(rows == n1h - 1, 0.0, pltpu.roll(first, n1h - 1, 0))
    o_ref[0, (n2 - 1) * n1h:, :] = (w0 * p_ref[0, (n2 - 2) * n1h:(n2 - 1) * n1h, :].astype(F32) + w1 * last
                                    + w2 * nxt + bias).astype(o_ref.dtype)


def _shortconv(p, w, bias, n1h, n2, cb=128):
    b, t, c = p.shape
    return pl.pallas_call(
        functools.partial(_shortconv_kernel, n1h=n1h, n2=n2),
        out_shape=jax.ShapeDtypeStruct((b, t, c), BF16),
        grid=(b, c // cb),
        in_specs=[pl.BlockSpec((1, t, cb), lambda i, j: (i, 0, j)),
                  pl.BlockSpec((3, cb), lambda i, j: (0, j)),
                  pl.BlockSpec((1, cb), lambda i, j: (0, j))],
        out_specs=pl.BlockSpec((1, t, cb), lambda i, j: (i, 0, j)),
        compiler_params=_params(("parallel", "parallel")),
        name="hy_shortconv",
    )(p, w, bias.reshape(1, c))


def _dft_tables(t):
    n = 2 * t
    n1 = 1 << ((n.bit_length() - 1) // 2)
    n2 = n // n1
    n1h = n1 // 2
    two_pi = 2.0 * math.pi
    k1 = jnp.arange(n1, dtype=jnp.int32)
    n2i = jnp.arange(n2, dtype=jnp.int32)

    def cs(prod, mod):
        ang = (two_pi / mod) * (prod % mod).astype(F32)
        return jnp.cos(ang), jnp.sin(ang)

    tpos = n2i[:, None] + n2 * jnp.arange(n1h, dtype=jnp.int32)[None, :]
    c, s = cs(k1[None, :, None] * tpos[:, None, :], n)
    g1 = jnp.concatenate([jnp.concatenate([c, s], axis=2), jnp.concatenate([-s, c], axis=2)], axis=1)
    ct, st = jnp.swapaxes(c, 1, 2) / n, jnp.swapaxes(s, 1, 2) / n
    g4a = jnp.concatenate([ct, st], axis=1)
    g4b = jnp.concatenate([-st, ct], axis=1)
    tposf = n2i[:, None] + n2 * jnp.arange(n1, dtype=jnp.int32)[None, :]
    cf, sf = cs(k1[None, :, None] * tposf[:, None, :], n)
    g1f = jnp.concatenate([cf, -sf], axis=1)
    c2, s2 = cs(n2i[:, None] * n2i[None, :], n2)
    f2a = jnp.concatenate([c2, -s2], axis=0)
    f2b = jnp.concatenate([s2, c2], axis=0)
    f3 = jnp.concatenate([jnp.concatenate([c2, -s2], axis=1), jnp.concatenate([s2, c2], axis=1)], axis=0)
    g4 = jnp.concatenate([g4a, g4b], axis=2)
    f2 = jnp.concatenate([f2a, f2b], axis=1)
    return dict(n1=n1, n2=n2, n1h=n1h, g1=g1.astype(BF16), g4=g4.astype(BF16), g1f=g1f.astype(BF16),
                f2a=f2a.astype(BF16), f2b=f2b.astype(BF16), f2=f2.astype(BF16), f3=f3.astype(BF16))


def _filter_mlp_kernel(z_ref, w1_ref, b1_ref, w2_ref, b2_ref, w3_ref, b3_ref, fr_ref, wo_ref, dl_ref,
                       o_ref, *, t):
    z = z_ref[...]
    h = jnp.sin(fr_ref[0:1, :] * (_dot(z, w1_ref[...], HP) + b1_ref[...]))
    h = jnp.sin(fr_ref[1:2, :] * (_dot(h, w2_ref[...], HP) + b2_ref[...]))
    h = jnp.sin(fr_ref[2:3, :] * (_dot(h, w3_ref[...], HP) + b3_ref[...]))
    window = jnp.exp(-z[:, 0:1] * dl_ref[...])
    tr = z.shape[0]
    row = pl.program_id(0) * tr + lax.broadcasted_iota(jnp.int32, (tr, 1), 0)
    o_ref[...] = jnp.where(row == t, 0.0, _dot(h, wo_ref[0], HP) * window)


def _filter_taps(t, w1, b1, w2, b2, w3, b3, freq, w_out, c_hy, tr=512):
    emb, width = w1.shape
    bands = (emb - 1) // 2
    pos = jnp.arange(t, dtype=F32)
    tt = pos / max(t - 1, 1)
    fr = jnp.linspace(1e-4, bands - 1, bands, dtype=F32)
    ang = (2.0 * math.pi / t) * pos[:, None] * fr[None, :]
    z = jnp.concatenate([tt[:, None], jnp.cos(ang), -jnp.sin(ang)], axis=-1)
    z = jnp.pad(z, ((0, 0), (0, 128 - emb)))
    w1 = jnp.pad(w1, ((0, 128 - emb), (0, 0)))
    emb = 128
    z2 = jnp.concatenate([z, z[:1], z[1:][::-1]], axis=0)
    max_decay = math.log(HY_TARGET) / HY_FAST_PCT
    min_decay = math.log(HY_TARGET) / HY_SLOW_PCT
    delta = jnp.abs(jnp.linspace(min_decay, max_decay, c_hy, dtype=F32))
    order = w_out.shape[1] // (2 * c_hy)
    nout = order * c_hy
    w_dir = w_out.reshape(width, order, 2, c_hy).transpose(2, 0, 1, 3).reshape(2, width, nout)
    delta_full = jnp.tile(delta, order).reshape(1, nout)
    tr = min(tr, t)
    nt = t // tr
    full = lambda a: pl.BlockSpec(a.shape, lambda i: (0,) * a.ndim)
    args = [z2, w1, b1.reshape(1, -1), w2, b2.reshape(1, -1), w3, b3.reshape(1, -1), freq, w_dir, delta_full]
    in_specs = [pl.BlockSpec((tr, emb), lambda i: (i, 0))] + [full(a) for a in args[1:]]
    in_specs[8] = pl.BlockSpec((1, width, nout), lambda i: (i // nt, 0, 0))
    return pl.pallas_call(
        functools.partial(_filter_mlp_kernel, t=t),
        out_shape=jax.ShapeDtypeStruct((2 * t, nout), F32),
        grid=(2 * nt,),
        in_specs=in_specs,
        out_specs=pl.BlockSpec((tr, nout), lambda i: (i, 0)),
        compiler_params=_params(("parallel",)),
        name="hy_filter_mlp",
    )(*args)


def _filter_s1_kernel(x_ref, g_ref, o_ref, ss_ref, *, g, oc):
    @pl.when(pl.program_id(0) == 0)
    def _():
        ss_ref[...] = jnp.zeros_like(ss_ref)

    n1 = x_ref.shape[0]
    x = jnp.concatenate([x_ref[:, j, :] for j in range(g)], axis=0)
    ss_ref[...] += jnp.sum(x * x, axis=0, keepdims=True)
    xb = x.astype(BF16)
    for j in range(g):
        o_ref[j] = _dot(g_ref[j], xb[j * n1:(j + 1) * n1])


def _filter_s2_kernel(sr_ref, si_ref, fa_ref, fb_ref, ss_ref, o_ref, *, g, oc):
    n2 = sr_ref.shape[0]
    scale = lax.rsqrt(ss_ref[...] + 1e-6)
    sr = jnp.concatenate([sr_ref[:, j, :] for j in range(g)], axis=0).astype(BF16)
    si = jnp.concatenate([si_ref[:, j, :] for j in range(g)], axis=0).astype(BF16)
    for j in range(g):
        x = _dot(fa_ref[...], sr[j * n2:(j + 1) * n2]) + _dot(fb_ref[...], si[j * n2:(j + 1) * n2])
        o_ref[j] = x * scale


def _filter_spectrum(two, tabs):
    n, oc = two.shape
    n1, n2 = tabs["n1"], tabs["n2"]
    g = SUB
    s1, ss = pl.pallas_call(
        functools.partial(_filter_s1_kernel, g=g, oc=oc),
        out_shape=(jax.ShapeDtypeStruct((n2, 2 * n1, oc), F32), jax.ShapeDtypeStruct((1, oc), F32)),
        grid=(n2 // g,),
        in_specs=[pl.BlockSpec((n1, g, oc), lambda j: (0, j, 0)),
                  pl.BlockSpec((g, 2 * n1, n1), lambda j: (j, 0, 0))],
        out_specs=(pl.BlockSpec((g, 2 * n1, oc), lambda j: (j, 0, 0)),
                   pl.BlockSpec((1, oc), lambda j: (0, 0))),
        compiler_params=_params(("arbitrary",)),
        name="hy_filter_dft1",
    )(two.reshape(n1, n2, oc), tabs["g1f"])
    return pl.pallas_call(
        functools.partial(_filter_s2_kernel, g=g, oc=oc),
        out_shape=jax.ShapeDtypeStruct((n1, 2 * n2, oc), F32),
        grid=(n1 // g,),
        in_specs=[pl.BlockSpec((n2, g, oc), lambda k: (0, k, 0)),
                  pl.BlockSpec((n2, g, oc), lambda k: (0, n1 // g + k, 0)),
                  pl.BlockSpec((2 * n2, n2), lambda k: (0, 0)),
                  pl.BlockSpec((2 * n2, n2), lambda k: (0, 0)),
                  pl.BlockSpec((1, oc), lambda k: (0, 0))],
        out_specs=pl.BlockSpec((g, 2 * n2, oc), lambda k: (k, 0, 0)),
        compiler_params=_params(("parallel",)),
        name="hy_filter_dft2",
    )(s1, s1, tabs["f2a"], tabs["f2b"], ss)


def _conv_s1_kernel(u_ref, g_ref, o_ref, t_ref, *, g, n1h):
    for j in range(g):
        rows = slice(j * n1h, (j + 1) * n1h)
        rhs = jnp.concatenate([u_ref[0, 0, rows, :], u_ref[0, 1, rows, :]], axis=0)
        t_ref[:, j, :] = _dot(g_ref[j], rhs)
    o_ref[0] = t_ref[...].astype(o_ref.dtype)


def _conv_s23_kernel(sr_ref, si_ref, f2_ref, f3_ref, h_ref, o_ref, t_ref, *, g, n2):
    for j in range(g):
        s = jnp.concatenate([sr_ref[0, j], si_ref[0, j]], axis=0)
        x = _dot(f2_ref[...], s)
        xr, xi = x[:n2], x[n2:]
        hr, hi = h_ref[j, :n2, :], h_ref[j, n2:, :]
        y = jnp.concatenate([xr * hr - xi * hi, xr * hi + xi * hr], axis=0).astype(BF16)
        t_ref[:, j, :] = _dot(f3_ref[...], y)
    o_ref[0] = t_ref[...].astype(o_ref.dtype)


def _conv_s4_kernel(rr_ref, ri_ref, g4_ref, u_ref, gate_ref, skip_ref, *rest, g, n1h, norm):
    if norm:
        ng_ref, avg_ref, o_ref = rest
    else:
        (o_ref,) = rest
    skip = skip_ref[...]
    for j in range(g):
        rows = slice(j * n1h, (j + 1) * n1h)
        y = _dot(g4_ref[j], jnp.concatenate([rr_ref[0, j], ri_ref[0, j]], axis=0))
        for q in range(2):
            z = gate_ref[0, q, rows, :].astype(F32) * (y[q * n1h:(q + 1) * n1h]
                                                       + u_ref[0, q, rows, :].astype(F32) * skip)
            if norm:
                ms = _dot((z * z).astype(BF16), avg_ref[...])
                o_ref[0, q, :, j, :] = z * lax.rsqrt(ms + RMS_EPS) * ng_ref[...]
            else:
                o_ref[0, q, rows, :] = z.astype(o_ref.dtype)


def _long_conv_gate(u_arr, u_col, gate_arr, gate_col, h_spec, h_col, skip, tabs, c, norm_g=None):
    b, t, _ = u_arr.shape
    n1, n2, n1h = tabs["n1"], tabs["n2"], tabs["n1h"]
    g = SUB
    npair = b // 2
    u4 = u_arr.reshape(npair, 2, t, u_arr.shape[2])
    gate4 = gate_arr.reshape(npair, 2, t, gate_arr.shape[2])
    gt = SUB16
    s1 = pl.pallas_call(
        functools.partial(_conv_s1_kernel, g=gt, n1h=n1h),
        out_shape=jax.ShapeDtypeStruct((npair, 2 * n1, n2, c), BF16),
        grid=(npair, n2 // gt),
        in_specs=[pl.BlockSpec((1, 2, gt * n1h, c), lambda p, j: (p, 0, j, u_col)),
                  pl.BlockSpec((gt, 2 * n1, n1), lambda p, j: (j, 0, 0))],
        out_specs=pl.BlockSpec((1, 2 * n1, gt, c), lambda p, j: (p, 0, j, 0)),
        scratch_shapes=[pltpu.VMEM((2 * n1, gt, c), F32)],
        compiler_params=_params(("parallel", "parallel")),
        name="hy_conv_dft1",
    )(u4, tabs["g1"])
    r = pl.pallas_call(
        functools.partial(_conv_s23_kernel, g=gt, n2=n2),
        out_shape=jax.ShapeDtypeStruct((npair, 2 * n2, n1, c), BF16),
        grid=(npair, n1 // gt),
        in_specs=[pl.BlockSpec((1, gt, n2, c), lambda p, k: (p, k, 0, 0)),
                  pl.BlockSpec((1, gt, n2, c), lambda p, k: (p, n1 // gt + k, 0, 0)),
                  pl.BlockSpec((2 * n2, 2 * n2), lambda p, k: (0, 0)),
                  pl.BlockSpec((2 * n2, 2 * n2), lambda p, k: (0, 0)),
                  pl.BlockSpec((gt, 2 * n2, c), lambda p, k: (k, 0, h_col))],
        out_specs=pl.BlockSpec((1, 2 * n2, gt, c), lambda p, k: (p, 0, k, 0)),
        scratch_shapes=[pltpu.VMEM((2 * n2, gt, c), F32)],
        compiler_params=_params(("parallel", "parallel")),
        name="hy_conv_dft23",
    )(s1, s1, tabs["f2"], tabs["f3"], h_spec)
    norm = norm_g is not None
    in_specs = [pl.BlockSpec((1, g, n1, c), lambda p, j: (p, j, 0, 0)),
                pl.BlockSpec((1, g, n1, c), lambda p, j: (p, n2 // g + j, 0, 0)),
                pl.BlockSpec((g, n1, 2 * n1), lambda p, j: (j, 0, 0)),
                pl.BlockSpec((1, 2, g * n1h, c), lambda p, j: (p, 0, j, u_col)),
                pl.BlockSpec((1, 2, g * n1h, c), lambda p, j: (p, 0, j, gate_col)),
                pl.BlockSpec((1, c), lambda p, j: (0, 0))]
    args = [r, r, tabs["g4"], u4, gate4, skip.reshape(1, c)]
    if norm:
        ch = jnp.arange(c) // HEAD
        avg = ((ch[:, None] == ch[None, :]).astype(F32) / HEAD).astype(BF16)
        in_specs += [pl.BlockSpec((1, c), lambda p, j: (0, 0)), pl.BlockSpec((c, c), lambda p, j: (0, 0))]
        args += [norm_g.reshape(1, c), avg]
        out_shape = jax.ShapeDtypeStruct((npair, 2, n1h, n2, c), F32)
        out_spec = pl.BlockSpec((1, 2, n1h, g, c), lambda p, j: (p, 0, 0, j, 0))
    else:
        out_shape = jax.ShapeDtypeStruct((npair, 2, t, c), BF16)
        out_spec = pl.BlockSpec((1, 2, g * n1h, c), lambda p, j: (p, 0, j, 0))
    out = pl.pallas_call(
        functools.partial(_conv_s4_kernel, g=g, n1h=n1h, norm=norm),
        out_shape=out_shape,
        grid=(npair, n2 // g),
        in_specs=in_specs,
        out_specs=out_spec,
        compiler_params=_params(("parallel", "parallel")),
        name="hy_conv_dft4",
    )(*args)
    return out.reshape(b, t, c)


def _hyena_branch(p, conv_w, conv_b, fw1, fb1, fw2, fb2, fw3, fb3, ffreq, fwout, skip, norm_g, tabs):
    b, t, _ = p.shape
    c = norm_g.shape[0]
    n1, n2, n1h = tabs["n1"], tabs["n2"], tabs["n1h"]
    u = _shortconv(p, conv_w, conv_b, n1h, n2)
    two = _filter_taps(t, fw1, fb1, fw2, fb2, fw3, fb3, ffreq, fwout, c)
    hspec = _filter_spectrum(two, tabs)
    z = _long_conv_gate(u, 2, u, 0, hspec, 0, skip[0], tabs, c)
    return _long_conv_gate(z, 0, u, 1, hspec, 1, skip[1], tabs, c, norm_g=norm_g)


def _rw_prep_kernel(*refs, has_vres, c):
    if has_vres:
        (p_ref, pp_ref, pn_ref, mu_ref, w0_ref, w2_ref, a0_ref, a2_ref, g2_ref, kk_ref, ka_ref, rk_ref,
         sum_ref, vf_ref, v0_ref, v2_ref,
         r_o, v_o, kk_o, lwf_o, lwb_o, kf_o, kb_o, af_o, ab_o, g_o, bon_o) = refs
    else:
        (p_ref, pp_ref, pn_ref, mu_ref, w0_ref, w2_ref, a0_ref, a2_ref, g2_ref, kk_ref, ka_ref, rk_ref,
         sum_ref,
         r_o, v_o, kk_o, lwf_o, lwb_o, kf_o, kb_o, af_o, ab_o, g_o, bon_o) = refs
    i = pl.program_id(1)
    last = pl.num_programs(1) - 1
    nsh = mu_ref.shape[1]
    p = p_ref[0, :, :nsh].astype(F32)
    tt = p.shape[0]
    rows = lax.broadcasted_iota(jnp.int32, (tt, 1), 0)
    prev_row = jnp.where(i == 0, 0.0, pp_ref[0, SUB16 - 1:SUB16, :nsh].astype(F32))
    next_row = jnp.where(i == last, 0.0, pn_ref[0, 0:1, :nsh].astype(F32))
    prev = jnp.where(rows == 0, prev_row, pltpu.roll(p, 1, 0))
    nxt = jnp.where(rows == tt - 1, next_row, pltpu.roll(p, tt - 1, 0))
    mu0, mu1 = mu_ref[0:1, :], mu_ref[1:2, :]
    pf = p * (1.0 - mu0 - mu1) + mu0 * prev + mu1 * nxt
    r = pf[:, :c]
    k = pf[:, c:2 * c]
    v = pf[:, 2 * c:3 * c]
    lw = 3 * c
    nd = w2_ref.shape[0]
    na = a2_ref.shape[0]
    wd = pf[:, lw:lw + nd]
    ad = pf[:, lw + nd:lw + nd + na]
    gd = pf[:, lw + nd + na:]
    if has_vres:
        lora = _dot(p_ref[0, :, nsh:], v2_ref[...])
        v = v + (vf_ref[0].astype(F32) - v) * jax.nn.sigmoid(v0_ref[...] + lora)
    g = _dot(jax.nn.sigmoid(gd).astype(BF16), g2_ref[...])
    kk = k * kk_ref[...]
    kk = kk * lax.rsqrt(jnp.maximum(_dot((kk * kk).astype(BF16), sum_ref[...]), 1e-24))
    wl = w0_ref[...] + _dot(jnp.tanh(wd).astype(BF16), w2_ref[...])
    logw = -math.exp(-0.5) * jax.nn.sigmoid(wl)
    a = jax.nn.sigmoid(a0_ref[...] + _dot(ad.astype(BF16), a2_ref[...]))
    ka = ka_ref[...]
    rk = rk_ref[...]
    k_d = [k * (1.0 + (a[:, d * c:(d + 1) * c] - 1.0) * ka) for d in range(2)]
    bon = _dot((r * (k_d[0] + k_d[1]) * rk).astype(BF16), sum_ref[...]) * v
    r_o[0] = r.astype(BF16)
    v_o[0] = v.astype(BF16)
    kk_o[0] = kk.astype(BF16)
    lwf_o[0] = logw[:, :c]
    lwb_o[0] = logw[:, c:]
    kf_o[0] = k_d[0].astype(BF16)
    kb_o[0] = k_d[1].astype(BF16)
    af_o[0] = a[:, :c].astype(BF16)
    ab_o[0] = a[:, c:].astype(BF16)
    g_o[0] = g.astype(BF16)
    bon_o[0] = bon.astype(BF16)


def _blockdiag2(m):
    k, c = m.shape[1], m.shape[2]
    z = jnp.zeros((k, c), m.dtype)
    return jnp.concatenate([jnp.concatenate([m[0], z], axis=1), jnp.concatenate([z, m[1]], axis=1)], axis=0)


def _rw_prep(p, v_first, shift, w0, w2, a0, a2, g2, k_k, k_a, r_k, vres, c, tt=256):
    b, t, pw = p.shape
    tt = min(tt, t)
    has_vres = vres is not None
    ch = jnp.arange(c) // HEAD
    summ = (ch[:, None] == ch[None, :]).astype(BF16)
    row = lambda a: a.reshape(1, -1)
    args = [p, p, p, shift, row(w0), _blockdiag2(w2).astype(BF16), row(a0), _blockdiag2(a2).astype(BF16),
            g2.astype(BF16), row(k_k), row(k_a), row(r_k), summ]
    full = lambda a: pl.BlockSpec(a.shape, lambda i, j: (0,) * a.ndim)
    nhb = t // SUB16
    in_specs = [pl.BlockSpec((1, tt, pw), lambda i, j: (i, j, 0)),
                pl.BlockSpec((1, SUB16, pw), lambda i, j: (i, jnp.maximum(j * (tt // SUB16) - 1, 0), 0)),
                pl.BlockSpec((1, SUB16, pw), lambda i, j: (i, jnp.minimum((j + 1) * (tt // SUB16), nhb - 1), 0))]
    in_specs += [full(a) for a in args[3:]]
    if has_vres:
        v0, _, v2 = vres
        v2p = jnp.pad(v2, ((0, pw - shift.shape[1] - v2.shape[0]), (0, 0))).astype(BF16)
        extra = [v_first, row(v0), v2p]
        in_specs += [pl.BlockSpec((1, tt, c), lambda i, j: (i, j, 0))] + [full(a) for a in extra[1:]]
        args += extra
    outs = tuple(jax.ShapeDtypeStruct((b, t, c), F32 if i in (3, 4) else BF16) for i in range(11))
    return pl.pallas_call(
        functools.partial(_rw_prep_kernel, has_vres=has_vres, c=c),
        out_shape=outs,
        grid=(b, t // tt),
        in_specs=in_specs,
        out_specs=tuple(pl.BlockSpec((1, tt, c), lambda i, j: (i, j, 0)) for _ in range(11)),
        compiler_params=_params(("parallel", "parallel")),
        name="rw_prep",
    )(*args)


def _split3(x):
    h1 = x.astype(BF16)
    r1 = x - h1.astype(F32)
    h2 = r1.astype(BF16)
    h3 = (r1 - h2.astype(F32)).astype(BF16)
    return h1, h2, h3


def _scan_chunks(s0, r, k, v, kk, a, lw, rev, bdm):
    L = CHUNK
    n = len(r)
    each = lambda f, *ls: [f(*xs) for xs in zip(*ls)]
    ti = lax.broadcasted_iota(jnp.int32, (L, L), 0)
    si = lax.broadcasted_iota(jnp.int32, (L, L), 1)
    tt = lax.broadcasted_iota(jnp.int32, (L, GW), 0)
    ss = lax.broadcasted_iota(jnp.int32, (L, GW), 1) % L
    eye = (ss == tt).astype(F32)
    tri = [((si >= ti) if q else (si <= ti)).astype(F32).astype(BF16) for q in rev]
    strict = [(ss > tt) if q else (ss < tt) for q in rev]
    incl = [(ss >= tt) if q else (ss <= tt) for q in rev]
    bdf = bdm.astype(F32)

    def bd(x):
        xb = x.astype(BF16)
        return jnp.concatenate([xb] * HEADS_PER_GROUP, axis=0) * bdm

    def bd_t(x):
        xt = x.T.astype(BF16)
        return jnp.concatenate([xt] * HEADS_PER_GROUP, axis=1) * bdm

    def rcmul(x, y):
        return _dot(x.astype(BF16), bd(y))

    parts = each(_split3, lw)
    cum = [_dot(t, p[0]) + _dot(t, p[1]) + _dot(t, p[2]) for t, p in zip(tri, parts)]
    tot = [c[0:1, :] if q else c[L - 1:L, :] for c, q in zip(cum, rev)]
    p_in = each(jnp.exp, cum)
    p_inv = each(lambda c: jnp.exp(-c), cum)
    p_ex = each(lambda c, w: jnp.exp(c - w), cum, lw)
    p_rem = each(lambda t, c: jnp.exp(t - c), tot, cum)
    beta = each(lambda x, y: x * y, kk, a)
    a_t = each(lambda x, p: -x * p, kk, p_ex)
    r_t = each(lambda x, p: x * p, r, p_in)
    k_t = each(lambda x, p: x * p, k, p_inv)
    b_t = each(lambda x, p: x * p, beta, p_inv)
    k_h = each(lambda x, p: x * p, k, p_rem)
    b_h = each(lambda x, p: x * p, beta, p_rem)

    lhs = each(lambda x, y: jnp.concatenate([x, y], axis=0).astype(BF16), a_t, r_t)
    akk = each(lambda x, y: _dot(x, bd_t(y)), lhs, k_t)
    abb = each(lambda x, y: _dot(x, bd_t(y)), lhs, b_t)
    a_ak = each(lambda m, x: jnp.where(m, x[:L], 0.0), strict, akk)
    a_rk = each(lambda m, x: jnp.where(m, x[L:], 0.0), incl, akk)
    a_ab = each(lambda m, x: jnp.where(m, x[:L], 0.0), strict, abb)
    a_rb = each(lambda m, x: jnp.where(m, x[L:], 0.0), incl, abb)

    pw = a_ab
    tinv = each(lambda x: eye + x, pw)
    pw = each(rcmul, pw, pw)
    for _ in range(4):
        both = each(lambda p, t: rcmul(jnp.concatenate([p, t], axis=0), p), pw, tinv)
        pw = [x[:L] for x in both]
        tinv = each(lambda t, x: t + x[L:], tinv, both)
    tinv = each(lambda t, p: t + rcmul(t, p), tinv, pw)

    av = each(lambda x, y, z: rcmul(jnp.concatenate([x, y], axis=0), z), a_ak, a_rk, v)
    ct = each(rcmul, a_rb, tinv)
    wu2 = each(lambda t, c, x, y: _dot(jnp.concatenate([t, c], axis=0).astype(BF16),
                                       jnp.concatenate([bd(x), bd(y[:L])], axis=1)), tinv, ct, a_t, av)
    wu = [x[:L] for x in wu2]
    rbwu = [x[L:] for x in wu2]
    o = each(lambda x, w, s, y: _dot((x + w[:, :GW]).astype(BF16), bd_t(s)) + y[L:] + w[:, GW:],
             r_t, rbwu, s0, av)

    m_bd = each(lambda w, x: _dot_tn(w[:, :GW].astype(BF16), x.astype(BF16)) * bdf, wu, b_h)
    n_full = each(lambda x, w, y, z: _dot_tn(jnp.concatenate([x, w[:, GW:]], axis=0).astype(BF16),
                                             jnp.concatenate([y, z], axis=0).astype(BF16)) * bdf,
                  v, wu, k_h, b_h)
    s1 = []
    for i in range(n):
        n_rc = n_full[i][0:HEAD]
        for h in range(1, HEADS_PER_GROUP):
            n_rc = n_rc + n_full[i][h * HEAD:(h + 1) * HEAD]
        s1.append(_dot(s0[i].astype(BF16), m_bd[i].astype(BF16)) + s0[i] * jnp.exp(tot[i]) + n_rc)
    return s1, o


def _scan_kernel(rf_ref, kf_ref, vf_ref, kkf_ref, af_ref, lwf_ref,
                 rb_ref, kb_ref, vb_ref, kkb_ref, ab_ref, lwb_ref, of_ref, ob_ref, s_ref, *, ng, nb):
    @pl.when(pl.program_id(1) == 0)
    def _():
        s_ref[...] = jnp.zeros_like(s_ref)

    gi = lax.broadcasted_iota(jnp.int32, (GW, GW), 0) // HEAD
    gj = lax.broadcasted_iota(jnp.int32, (GW, GW), 1) // HEAD
    bdm = (gi == gj).astype(F32).astype(BF16)
    dirs = ((rf_ref, kf_ref, vf_ref, kkf_ref, af_ref, lwf_ref), (rb_ref, kb_ref, vb_ref, kkb_ref, ab_ref, lwb_ref))
    chains = [(i, d, h) for i in range(nb) for d in range(2) for h in range(ng)]
    ins = [[dirs[d][q][i, :, h * GW:(h + 1) * GW].astype(F32) for i, d, h in chains] for q in range(6)]
    s1, o = _scan_chunks([s_ref[i, d, h] for i, d, h in chains], *ins, [d == 1 for _, d, _ in chains], bdm)
    for (i, d, h), s_new, o_new in zip(chains, s1, o):
        (of_ref, ob_ref)[d][i, :, h * GW:(h + 1) * GW] = o_new
        s_ref[i, d, h] = s_new


SCAN_BATCH_ROWS = 4


def _wkv_scan(r, v, kk, kf, af, lwf, kb, ab, lwb):
    b, t, c = r.shape
    nc = t // CHUNK
    ng = c // GW
    nb = math.gcd(b, SCAN_BATCH_ROWS)
    fspec = pl.BlockSpec((nb, CHUNK, c), lambda i, j: (i, j, 0))
    bspec = pl.BlockSpec((nb, CHUNK, c), lambda i, j: (i, nc - 1 - j, 0))
    return pl.pallas_call(
        functools.partial(_scan_kernel, ng=ng, nb=nb),
        out_shape=(jax.ShapeDtypeStruct((b, t, c), F32), jax.ShapeDtypeStruct((b, t, c), F32)),
        grid=(b // nb, nc),
        in_specs=[fspec] * 6 + [bspec] * 6,
        out_specs=(fspec, bspec),
        scratch_shapes=[pltpu.VMEM((nb, 2, ng, HEAD, GW), F32)],
        compiler_params=_params(("parallel", "arbitrary")),
        name="rw_scan",
    )(r, kf, v, kk, af, lwf, r, kb, v, kk, ab, lwb)


def _out_proj_kernel(yh_ref, sf_ref, sb_ref, bon_ref, g_ref, lg_ref, lb_ref, avg_ref, wh_ref, wr_ref, r_ref,
                     o_ref):
    s = sf_ref[...] + sb_ref[...]
    mean = _dot2(s, avg_ref[...])
    d = s - mean
    var = _dot((d * d).astype(BF16), avg_ref[...])
    y = d * lax.rsqrt(var + GN_EPS) * lg_ref[...] + lb_ref[...] + bon_ref[...].astype(F32)
    yr = (y * g_ref[...].astype(F32)).astype(BF16)
    o_ref[...] = _dot(yh_ref[...].astype(BF16), wh_ref[...]) + _dot(yr, wr_ref[...]) + r_ref[...]


def _out_proj(yh, sf, sb, bon, g, lnx_g, lnx_b, w, res, tm=512):
    m, ch = yh.shape
    cr = sf.shape[1]
    d = w.shape[1]
    tm = min(tm, m)
    head = jnp.arange(cr) // HEAD
    avg = ((head[:, None] == head[None, :]).astype(F32) / HEAD).astype(BF16)
    row = lambda c: pl.BlockSpec((tm, c), lambda i: (i, 0))
    return pl.pallas_call(
        _out_proj_kernel,
        out_shape=jax.ShapeDtypeStruct((m, d), F32),
        grid=(m // tm,),
        in_specs=[row(ch), row(cr), row(cr), row(cr), row(cr), _const_spec((1, cr)), _const_spec((1, cr)),
                  _const_spec((cr, cr)), _const_spec((ch, d)), _const_spec((cr, d)), row(d)],
        out_specs=row(d),
        compiler_params=_params(("parallel",)),
        name="out_proj",
    )(yh, sf, sb, bon, g, lnx_g.reshape(1, cr), lnx_b.reshape(1, cr), avg, w[:ch], w[ch:], res)


def _rwkv_branch(p, v_first, shift, w0, w2, a0, a2, g2, k_k, k_a, r_k, vres, c):
    r, v, kk, lwf, lwb, kf, kb, af, ab, g, bon = _rw_prep(
        p, v_first, shift, w0, w2, a0, a2, g2, k_k, k_a, r_k, vres, c)
    sf, sb = _wkv_scan(r, v, kk, kf, af, lwf, kb, ab, lwb)
    return sf, sb, bon, g, (v if vres is None else v_first)


def _forward(x, norm1_g, w_in, hy_conv_w, hy_conv_b, hy_f_w1, hy_f_b1, hy_f_w2, hy_f_b2, hy_f_w3, hy_f_b3,
             hy_f_freq, hy_f_wout, hy_skip, hy_norm_g, rw_shift, rw_w0, rw_w2, rw_a0, rw_a2, rw_g2, rw_k_k,
             rw_k_a, rw_r_k, rw_lnx_g, rw_lnx_b, rw_v0, rw_v1, rw_v2, w_out, norm2_g, mlp_w1, mlp_w2,
             final_g):
    b, t, d = x.shape
    depth = w_in.shape[0]
    c_hy = hy_norm_g.shape[1]
    c_rw = rw_lnx_g.shape[1]
    hy_proj = hy_conv_b.shape[1]
    assert b % 2 == 0 and t % CHUNK == 0 and CHUNK == HEAD
    tabs = _dft_tables(t)
    assert tabs["n2"] % SUB16 == 0 and tabs["n1"] % SUB16 == 0 and tabs["n1h"] % SUB == 0
    v_first = None
    for l in range(depth):
        w_l = w_in[l].astype(BF16)
        w_rw = w_l[:, hy_proj:]
        vres = None if l == 0 else (rw_v0[l - 1], rw_v1[l - 1], rw_v2[l - 1])
        if vres is not None:
            v1 = vres[1].astype(BF16)
            w_rw = jnp.concatenate([w_rw, jnp.pad(v1, ((0, 0), (0, -v1.shape[1] % 128)))], axis=1)
        p_hy, p_rw = _in_proj(x, norm1_g[l], w_l[:, :hy_proj], w_rw, tabs["n1h"], tabs["n2"])
        y_hy = _hyena_branch(p_hy, hy_conv_w[l], hy_conv_b[l], hy_f_w1[l], hy_f_b1[l], hy_f_w2[l], hy_f_b2[l],
                             hy_f_w3[l], hy_f_b3[l], hy_f_freq[l], hy_f_wout[l], hy_skip[l], hy_norm_g[l], tabs)
        sf, sb, bon, g, v_first = _rwkv_branch(p_rw, v_first, rw_shift[l], rw_w0[l], rw_w2[l], rw_a0[l], rw_a2[l],
                                               rw_g2[l], rw_k_k[l], rw_k_a[l], rw_r_k[l], vres, c_rw)
        flat = lambda a: a.reshape(b * t, a.shape[-1])
        x = _out_proj(flat(y_hy), flat(sf), flat(sb), flat(bon), flat(g), rw_lnx_g[l], rw_lnx_b[l],
                      w_out[l].astype(BF16), flat(x))
        x = _mlp(x, norm2_g[l], mlp_w1[l].astype(BF16), mlp_w2[l].astype(BF16),
                 final_g=final_g if l == depth - 1 else None).reshape(b, t, d)
    return x


def kernel(x, norm1_g, w_in, hy_conv_w, hy_conv_b, hy_f_w1, hy_f_b1, hy_f_w2, hy_f_b2, hy_f_w3, hy_f_b3,
           hy_f_freq, hy_f_wout, hy_skip, hy_norm_g, rw_shift, rw_w0, rw_w2, rw_a0, rw_a2, rw_g2, rw_k_k,
           rw_k_a, rw_r_k, rw_lnx_g, rw_lnx_b, rw_v0, rw_v1, rw_v2, w_out, norm2_g, mlp_w1, mlp_w2, final_g):
    return _forward(x, norm1_g, w_in, hy_conv_w, hy_conv_b, hy_f_w1, hy_f_b1, hy_f_w2, hy_f_b2, hy_f_w3,
                    hy_f_b3, hy_f_freq, hy_f_wout, hy_skip, hy_norm_g, rw_shift, rw_w0, rw_w2, rw_a0, rw_a2,
                    rw_g2, rw_k_k, rw_k_a, rw_r_k, rw_lnx_g, rw_lnx_b, rw_v0, rw_v1, rw_v2, w_out, norm2_g,
                    mlp_w1, mlp_w2, final_g)
```

```python
import functools
import math

import jax
import jax.numpy as jnp
from jax import lax
from jax.experimental import pallas as pl
from jax.experimental.pallas import tpu as pltpu

F32 = jnp.float32
BF16 = jnp.bfloat16
HP = lax.Precision.HIGHEST

HEAD = 64
HEADS_PER_GROUP = 4
GW = HEAD * HEADS_PER_GROUP
CHUNK = 64
INV_BASE = 8
RMS_EPS = 1e-5
GN_EPS = HEAD * 1e-5
HY_TARGET = 1e-2
HY_FAST_PCT = 0.3
HY_SLOW_PCT = 1.5
VMEM_LIMIT = 56 * 1024 * 1024


def _params(sem, vmem=VMEM_LIMIT):
    return pltpu.CompilerParams(dimension_semantics=sem, vmem_limit_bytes=vmem)


def _dot(a, b, precision=None):
    return jnp.dot(a, b, preferred_element_type=F32, precision=precision)


def _dot_nt(a, b, precision=None):
    return lax.dot_general(a, b, (((1,), (1,)), ((), ())), preferred_element_type=F32,
                           precision=precision)


def _dot_tn(a, b, precision=None):
    return lax.dot_general(a, b, (((0,), (0,)), ((), ())), preferred_element_type=F32,
                           precision=precision)


def _const_spec(shape):
    return pl.BlockSpec(shape, lambda *_: (0,) * len(shape), pipeline_mode=pl.Buffered(1))


def _split3(x):
    h1 = x.astype(BF16)
    r1 = x - h1.astype(F32)
    h2 = r1.astype(BF16)
    h3 = (r1 - h2.astype(F32)).astype(BF16)
    return h1, h2, h3


def _dot3(x, m):
    h1, h2, h3 = _split3(x)
    return _dot(h1, m) + _dot(h2, m) + _dot(h3, m)


def _mlp_kernel(x_ref, g_ref, w1_ref, w2_ref, *rest, nchunk):
    x = x_ref[...]
    hn = (x * lax.rsqrt(jnp.mean(x * x, axis=-1, keepdims=True) + RMS_EPS) * g_ref[...]).astype(BF16)
    ff = w1_ref.shape[1]
    cw = ff // nchunk
    acc = x
    for j in range(nchunk):
        h = _dot(hn, w1_ref[:, j * cw:(j + 1) * cw])
        h = jnp.square(jnp.maximum(h, 0.0)).astype(BF16)
        acc = acc + _dot(h, w2_ref[j * cw:(j + 1) * cw, :])
    if len(rest) == 2:
        gf_ref, o_ref = rest
        acc = acc * lax.rsqrt(jnp.mean(acc * acc, axis=-1, keepdims=True) + RMS_EPS) * gf_ref[...]
    else:
        (o_ref,) = rest
    o_ref[...] = acc


def _mlp(x2d, g, w1, w2, final_g=None, tm=512, nchunk=4):
    m, d = x2d.shape
    ff = w1.shape[1]
    tm = min(tm, m)
    in_specs = [pl.BlockSpec((tm, d), lambda i: (i, 0)), _const_spec((1, d)),
                _const_spec((d, ff)), _const_spec((ff, d))]
    args = [x2d, g.reshape(1, d), w1, w2]
    if final_g is not None:
        in_specs.append(_const_spec((1, d)))
        args.append(final_g.reshape(1, d))
    return pl.pallas_call(
        functools.partial(_mlp_kernel, nchunk=nchunk),
        out_shape=jax.ShapeDtypeStruct((m, d), F32),
        grid=(m // tm,),
        in_specs=in_specs,
        out_specs=pl.BlockSpec((tm, d), lambda i: (i, 0)),
        compiler_params=_params(("parallel",)),
        name="mlp",
    )(*args)


SUB = 8
SUB16 = 16


def _rms(x, g):
    return (x * lax.rsqrt(jnp.mean(x * x, axis=-1, keepdims=True) + RMS_EPS) * g).astype(BF16)


def _in_proj_kernel(x_ref, g_ref, wh_ref, wr_ref, oh_ref, or_ref, *, n2):
    hn = _rms(x_ref[0], g_ref[...])
    or_ref[0] = _dot(hn, wr_ref[...]).astype(or_ref.dtype)
    ph = _dot(hn, wh_ref[...])
    for j in range(SUB):
        oh_ref[0, :, j, :] = ph[j * n2:(j + 1) * n2]


def _in_proj(x, g, w_hy, w_rw, n1h, n2):
    b, t, d = x.shape
    ph, pr = w_hy.shape[1], w_rw.shape[1]
    tm = SUB * n2
    p_hy, p_rw = pl.pallas_call(
        functools.partial(_in_proj_kernel, n2=n2),
        out_shape=(jax.ShapeDtypeStruct((b, n2, n1h, ph), F32), jax.ShapeDtypeStruct((b, t, pr), BF16)),
        grid=(b, t // tm),
        in_specs=[pl.BlockSpec((1, tm, d), lambda i, j: (i, j, 0)), _const_spec((1, d)),
                  _const_spec((d, ph)), _const_spec((d, pr))],
        out_specs=(pl.BlockSpec((1, n2, SUB, ph), lambda i, j: (i, 0, j, 0)),
                   pl.BlockSpec((1, tm, pr), lambda i, j: (i, j, 0))),
        compiler_params=_params(("parallel", "parallel")),
        name="in_proj",
    )(x, g.reshape(1, d), w_hy, w_rw)
    return p_hy.reshape(b, t, ph), p_rw


def _shortconv_kernel(p_ref, w_ref, b_ref, o_ref, *, n1h, n2):
    w0 = w_ref[0:1, :]
    w1 = w_ref[1:2, :]
    w2 = w_ref[2:3, :]
    bias = b_ref[...]
    rows = lax.broadcasted_iota(jnp.int32, (n1h, 1), 0)

    def blk(i):
        return p_ref[0, pl.ds(pl.multiple_of(i * n1h, n1h), n1h), :].astype(F32)

    def body(i, carry):
        o_ref[0, pl.ds(pl.multiple_of(i * n1h, n1h), n1h), :] = (
            w0 * blk(i - 1) + w1 * blk(i) + w2 * blk(i + 1) + bias).astype(o_ref.dtype)
        return carry

    lax.fori_loop(1, n2 - 1, body, 0)
    last = p_ref[0, (n2 - 1) * n1h:, :].astype(F32)
    first = p_ref[0, :n1h, :].astype(F32)
    prev0 = jnp.where(rows == 0, 0.0, pltpu.roll(last, 1, 0))
    o_ref[0, :n1h, :] = (w0 * prev0 + w1 * first + w2 * p_ref[0, n1h:2 * n1h, :].astype(F32)
                         + bias).astype(o_ref.dtype)
    nxt = jnp.where(rows == n1h - 1, 0.0, pltpu.roll(first, n1h - 1, 0))
    o_ref[0, (n2 - 1) * n1h:, :] = (w0 * p_ref[0, (n2 - 2) * n1h:(n2 - 1) * n1h, :].astype(F32) + w1 * last
                                    + w2 * nxt + bias).astype(o_ref.dtype)


def _shortconv(p, w, bias, n1h, n2, cb=128):
    b, t, c = p.shape
    return pl.pallas_call(
        functools.partial(_shortconv_kernel, n1h=n1h, n2=n2),
        out_shape=jax.ShapeDtypeStruct((b, t, c), BF16),
        grid=(b, c // cb),
        in_specs=[pl.BlockSpec((1, t, cb), lambda i, j: (i, 0, j)),
                  pl.BlockSpec((3, cb), lambda i, j: (0, j)),
                  pl.BlockSpec((1, cb), lambda i, j: (0, j))],
        out_specs=pl.BlockSpec((1, t, cb), lambda i, j: (i, 0, j)),
        compiler_params=_params(("parallel", "parallel")),
        name="hy_shortconv",
    )(p, w, bias.reshape(1, c))


def _dft_tables(t):
    n = 2 * t
    n1 = 1 << ((n.bit_length() - 1) // 2)
    n2 = n // n1
    n1h = n1 // 2
    two_pi = 2.0 * math.pi
    k1 = jnp.arange(n1, dtype=jnp.int32)
    n2i = jnp.arange(n2, dtype=jnp.int32)

    def cs(prod, mod):
        ang = (two_pi / mod) * (prod % mod).astype(F32)
        return jnp.cos(ang), jnp.sin(ang)

    tpos = n2i[:, None] + n2 * jnp.arange(n1h, dtype=jnp.int32)[None, :]
    c, s = cs(k1[None, :, None] * tpos[:, None, :], n)
    g1 = jnp.concatenate([jnp.concatenate([c, s], axis=2), jnp.concatenate([-s, c], axis=2)], axis=1)
    ct, st = jnp.swapaxes(c, 1, 2) / n, jnp.swapaxes(s, 1, 2) / n
    g4a = jnp.concatenate([ct, st], axis=1)
    g4b = jnp.concatenate([-st, ct], axis=1)
    tposf = n2i[:, None] + n2 * jnp.arange(n1, dtype=jnp.int32)[None, :]
    cf, sf = cs(k1[None, :, None] * tposf[:, None, :], n)
    g1f = jnp.concatenate([cf, -sf], axis=1)
    c2, s2 = cs(n2i[:, None] * n2i[None, :], n2)
    f2a = jnp.concatenate([c2, -s2], axis=0)
    f2b = jnp.concatenate([s2, c2], axis=0)
    f3 = jnp.concatenate([jnp.concatenate([c2, -s2], axis=1), jnp.concatenate([s2, c2], axis=1)], axis=0)
    g4 = jnp.concatenate([g4a, g4b], axis=2)
    f2 = jnp.concatenate([f2a, f2b], axis=1)
    return dict(n1=n1, n2=n2, n1h=n1h, g1=g1.astype(BF16), g4=g4.astype(BF16), g1f=g1f.astype(BF16),
                f2a=f2a.astype(BF16), f2b=f2b.astype(BF16), f2=f2.astype(BF16), f3=f3.astype(BF16))


def _filter_mlp_kernel(z_ref, w1_ref, b1_ref, w2_ref, b2_ref, w3_ref, b3_ref, fr_ref, wo_ref, dl_ref,
                       o_ref, *, t):
    z = z_ref[...]
    h = jnp.sin(fr_ref[0:1, :] * (_dot(z, w1_ref[...], HP) + b1_ref[...]))
    h = jnp.sin(fr_ref[1:2, :] * (_dot(h, w2_ref[...], HP) + b2_ref[...]))
    h = jnp.sin(fr_ref[2:3, :] * (_dot(h, w3_ref[...], HP) + b3_ref[...]))
    window = jnp.exp(-z[:, 0:1] * dl_ref[...])
    tr = z.shape[0]
    row = pl.program_id(0) * tr + lax.broadcasted_iota(jnp.int32, (tr, 1), 0)
    o_ref[...] = jnp.where(row == t, 0.0, _dot(h, wo_ref[0], HP) * window)


def _filter_taps(t, w1, b1, w2, b2, w3, b3, freq, w_out, c_hy, tr=512):
    emb, width = w1.shape
    bands = (emb - 1) // 2
    pos = jnp.arange(t, dtype=F32)
    tt = pos / max(t - 1, 1)
    fr = jnp.linspace(1e-4, bands - 1, bands, dtype=F32)
    ang = (2.0 * math.pi / t) * pos[:, None] * fr[None, :]
    z = jnp.concatenate([tt[:, None], jnp.cos(ang), -jnp.sin(ang)], axis=-1)
    z = jnp.pad(z, ((0, 0), (0, 128 - emb)))
    w1 = jnp.pad(w1, ((0, 128 - emb), (0, 0)))
    emb = 128
    z2 = jnp.concatenate([z, z[:1], z[1:][::-1]], axis=0)
    max_decay = math.log(HY_TARGET) / HY_FAST_PCT
    min_decay = math.log(HY_TARGET) / HY_SLOW_PCT
    delta = jnp.abs(jnp.linspace(min_decay, max_decay, c_hy, dtype=F32))
    order = w_out.shape[1] // (2 * c_hy)
    nout = order * c_hy
    w_dir = w_out.reshape(width, order, 2, c_hy).transpose(2, 0, 1, 3).reshape(2, width, nout)
    delta_full = jnp.tile(delta, order).reshape(1, nout)
    tr = min(tr, t)
    nt = t // tr
    full = lambda a: pl.BlockSpec(a.shape, lambda i: (0,) * a.ndim)
    args = [z2, w1, b1.reshape(1, -1), w2, b2.reshape(1, -1), w3, b3.reshape(1, -1), freq, w_dir, delta_full]
    in_specs = [pl.BlockSpec((tr, emb), lambda i: (i, 0))] + [full(a) for a in args[1:]]
    in_specs[8] = pl.BlockSpec((1, width, nout), lambda i: (i // nt, 0, 0))
    return pl.pallas_call(
        functools.partial(_filter_mlp_kernel, t=t),
        out_shape=jax.ShapeDtypeStruct((2 * t, nout), F32),
        grid=(2 * nt,),
        in_specs=in_specs,
        out_specs=pl.BlockSpec((tr, nout), lambda i: (i, 0)),
        compiler_params=_params(("parallel",)),
        name="hy_filter_mlp",
    )(*args)


def _filter_s1_kernel(x_ref, g_ref, o_ref, ss_ref, *, g, oc):
    @pl.when(pl.program_id(0) == 0)
    def _():
        ss_ref[...] = jnp.zeros_like(ss_ref)

    n1 = x_ref.shape[0]
    x = jnp.concatenate([x_ref[:, j, :] for j in range(g)], axis=0)
    ss_ref[...] += jnp.sum(x * x, axis=0, keepdims=True)
    xb = x.astype(BF16)
    for j in range(g):
        o_ref[j] = _dot(g_ref[j], xb[j * n1:(j + 1) * n1])


def _filter_s2_kernel(sr_ref, si_ref, fa_ref, fb_ref, ss_ref, o_ref, *, g, oc):
    n2 = sr_ref.shape[0]
    scale = lax.rsqrt(ss_ref[...] + 1e-6)
    sr = jnp.concatenate([sr_ref[:, j, :] for j in range(g)], axis=0).astype(BF16)
    si = jnp.concatenate([si_ref[:, j, :] for j in range(g)], axis=0).astype(BF16)
    for j in range(g):
        x = _dot(fa_ref[...], sr[j * n2:(j + 1) * n2]) + _dot(fb_ref[...], si[j * n2:(j + 1) * n2])
        o_ref[j] = x * scale


def _filter_spectrum(two, tabs):
    n, oc = two.shape
    n1, n2 = tabs["n1"], tabs["n2"]
    g = SUB
    s1, ss = pl.pallas_call(
        functools.partial(_filter_s1_kernel, g=g, oc=oc),
        out_shape=(jax.ShapeDtypeStruct((n2, 2 * n1, oc), F32), jax.ShapeDtypeStruct((1, oc), F32)),
        grid=(n2 // g,),
        in_specs=[pl.BlockSpec((n1, g, oc), lambda j: (0, j, 0)),
                  pl.BlockSpec((g, 2 * n1, n1), lambda j: (j, 0, 0))],
        out_specs=(pl.BlockSpec((g, 2 * n1, oc), lambda j: (j, 0, 0)),
                   pl.BlockSpec((1, oc), lambda j: (0, 0))),
        compiler_params=_params(("arbitrary",)),
        name="hy_filter_dft1",
    )(two.reshape(n1, n2, oc), tabs["g1f"])
    return pl.pallas_call(
        functools.partial(_filter_s2_kernel, g=g, oc=oc),
        out_shape=jax.ShapeDtypeStruct((n1, 2 * n2, oc), F32),
        grid=(n1 // g,),
        in_specs=[pl.BlockSpec((n2, g, oc), lambda k: (0, k, 0)),
                  pl.BlockSpec((n2, g, oc), lambda k: (0, n1 // g + k, 0)),
                  pl.BlockSpec((2 * n2, n2), lambda k: (0, 0)),
                  pl.BlockSpec((2 * n2, n2), lambda k: (0, 0)),
                  pl.BlockSpec((1, oc), lambda k: (0, 0))],
        out_specs=pl.BlockSpec((g, 2 * n2, oc), lambda k: (k, 0, 0)),
        compiler_params=_params(("parallel",)),
        name="hy_filter_dft2",
    )(s1, s1, tabs["f2a"], tabs["f2b"], ss)


def _conv_s1_kernel(u_ref, g_ref, o_ref, t_ref, *, g, n1h):
    for j in range(g):
        rows = slice(j * n1h, (j + 1) * n1h)
        rhs = jnp.concatenate([u_ref[0, 0, rows, :], u_ref[0, 1, rows, :]], axis=0)
        t_ref[:, j, :] = _dot(g_ref[j], rhs)
    o_ref[0] = t_ref[...].astype(o_ref.dtype)


def _conv_s23_kernel(sr_ref, si_ref, f2_ref, f3_ref, h_ref, o_ref, t_ref, *, g, n2):
    for j in range(g):
        s = jnp.concatenate([sr_ref[0, j], si_ref[0, j]], axis=0)
        x = _dot(f2_ref[...], s)
        xr, xi = x[:n2], x[n2:]
        hr, hi = h_ref[j, :n2, :], h_ref[j, n2:, :]
        y = jnp.concatenate([xr * hr - xi * hi, xr * hi + xi * hr], axis=0).astype(BF16)
        t_ref[:, j, :] = _dot(f3_ref[...], y)
    o_ref[0] = t_ref[...].astype(o_ref.dtype)


def _conv_s4_kernel(rr_ref, ri_ref, g4_ref, u_ref, gate_ref, skip_ref, *rest, g, n1h, norm):
    if norm:
        ng_ref, avg_ref, o_ref = rest
    else:
        (o_ref,) = rest
    skip = skip_ref[...]
    for j in range(g):
        rows = slice(j * n1h, (j + 1) * n1h)
        y = _dot(g4_ref[j], jnp.concatenate([rr_ref[0, j], ri_ref[0, j]], axis=0))
        for q in range(2):
            z = gate_ref[0, q, rows, :].astype(F32) * (y[q * n1h:(q + 1) * n1h]
                                                       + u_ref[0, q, rows, :].astype(F32) * skip)
            if norm:
                ms = _dot((z * z).astype(BF16), avg_ref[...])
                o_ref[0, q, :, j, :] = z * lax.rsqrt(ms + RMS_EPS) * ng_ref[...]
            else:
                o_ref[0, q, rows, :] = z.astype(o_ref.dtype)


def _long_conv_gate(u_arr, u_col, gate_arr, gate_col, h_spec, h_col, skip, tabs, c, norm_g=None):
    b, t, _ = u_arr.shape
    n1, n2, n1h = tabs["n1"], tabs["n2"], tabs["n1h"]
    g = SUB
    npair = b // 2
    u4 = u_arr.reshape(npair, 2, t, u_arr.shape[2])
    gate4 = gate_arr.reshape(npair, 2, t, gate_arr.shape[2])
    gt = SUB16
    s1 = pl.pallas_call(
        functools.partial(_conv_s1_kernel, g=gt, n1h=n1h),
        out_shape=jax.ShapeDtypeStruct((npair, 2 * n1, n2, c), BF16),
        grid=(n2 // gt, npair),
        in_specs=[pl.BlockSpec((1, 2, gt * n1h, c), lambda j, p: (p, 0, j, u_col)),
                  pl.BlockSpec((gt, 2 * n1, n1), lambda j, p: (j, 0, 0))],
        out_specs=pl.BlockSpec((1, 2 * n1, gt, c), lambda j, p: (p, 0, j, 0)),
        scratch_shapes=[pltpu.VMEM((2 * n1, gt, c), F32)],
        compiler_params=_params(("parallel", "parallel")),
        name="hy_conv_dft1",
    )(u4, tabs["g1"])
    r = pl.pallas_call(
        functools.partial(_conv_s23_kernel, g=gt, n2=n2),
        out_shape=jax.ShapeDtypeStruct((npair, 2 * n2, n1, c), BF16),
        grid=(n1 // gt, npair),
        in_specs=[pl.BlockSpec((1, gt, n2, c), lambda k, p: (p, k, 0, 0)),
                  pl.BlockSpec((1, gt, n2, c), lambda k, p: (p, n1 // gt + k, 0, 0)),
                  pl.BlockSpec((2 * n2, 2 * n2), lambda k, p: (0, 0)),
                  pl.BlockSpec((2 * n2, 2 * n2), lambda k, p: (0, 0)),
                  pl.BlockSpec((gt, 2 * n2, c), lambda k, p: (k, 0, h_col))],
        out_specs=pl.BlockSpec((1, 2 * n2, gt, c), lambda k, p: (p, 0, k, 0)),
        scratch_shapes=[pltpu.VMEM((2 * n2, gt, c), F32)],
        compiler_params=_params(("parallel", "parallel")),
        name="hy_conv_dft23",
    )(s1, s1, tabs["f2"], tabs["f3"], h_spec)
    norm = norm_g is not None
    in_specs = [pl.BlockSpec((1, g, n1, c), lambda p, j: (p, j, 0, 0)),
                pl.BlockSpec((1, g, n1, c), lambda p, j: (p, n2 // g + j, 0, 0)),
                pl.BlockSpec((g, n1, 2 * n1), lambda p, j: (j, 0, 0)),
                pl.BlockSpec((1, 2, g * n1h, c), lambda p, j: (p, 0, j, u_col)),
                pl.BlockSpec((1, 2, g * n1h, c), lambda p, j: (p, 0, j, gate_col)),
                pl.BlockSpec((1, c), lambda p, j: (0, 0))]
    args = [r, r, tabs["g4"], u4, gate4, skip.reshape(1, c)]
    if norm:
        ch = jnp.arange(c) // HEAD
        avg = ((ch[:, None] == ch[None, :]).astype(F32) / HEAD).astype(BF16)
        in_specs += [pl.BlockSpec((1, c), lambda p, j: (0, 0)), pl.BlockSpec((c, c), lambda p, j: (0, 0))]
        args += [norm_g.reshape(1, c), avg]
        out_shape = jax.ShapeDtypeStruct((npair, 2, n1h, n2, c), F32)
        out_spec = pl.BlockSpec((1, 2, n1h, g, c), lambda p, j: (p, 0, 0, j, 0))
    else:
        out_shape = jax.ShapeDtypeStruct((npair, 2, t, c), BF16)
        out_spec = pl.BlockSpec((1, 2, g * n1h, c), lambda p, j: (p, 0, j, 0))
    out = pl.pallas_call(
        functools.partial(_conv_s4_kernel, g=g, n1h=n1h, norm=norm),
        out_shape=out_shape,
        grid=(npair, n2 // g),
        in_specs=in_specs,
        out_specs=out_spec,
        compiler_params=_params(("parallel", "parallel")),
        name="hy_conv_dft4",
    )(*args)
    return out.reshape(b, t, c)


def _hyena_branch(p, conv_w, conv_b, fw1, fb1, fw2, fb2, fw3, fb3, ffreq, fwout, skip, norm_g, tabs):
    b, t, _ = p.shape
    c = norm_g.shape[0]
    n1, n2, n1h = tabs["n1"], tabs["n2"], tabs["n1h"]
    u = _shortconv(p, conv_w, conv_b, n1h, n2)
    two = _filter_taps(t, fw1, fb1, fw2, fb2, fw3, fb3, ffreq, fwout, c)
    hspec = _filter_spectrum(two, tabs)
    z = _long_conv_gate(u, 2, u, 0, hspec, 0, skip[0], tabs, c)
    return _long_conv_gate(z, 0, u, 1, hspec, 1, skip[1], tabs, c, norm_g=norm_g)


def _rw_prep_kernel(*refs, has_vres, c):
    if has_vres:
        (p_ref, pp_ref, pn_ref, mu_ref, w0_ref, w2_ref, a0_ref, a2_ref, g2_ref, kk_ref, ka_ref, rk_ref,
         sum_ref, vf_ref, v0_ref, v2_ref,
         r_o, v_o, kk_o, lwf_o, lwb_o, kf_o, kb_o, af_o, ab_o, g_o, bon_o) = refs
    else:
        (p_ref, pp_ref, pn_ref, mu_ref, w0_ref, w2_ref, a0_ref, a2_ref, g2_ref, kk_ref, ka_ref, rk_ref,
         sum_ref,
         r_o, v_o, kk_o, lwf_o, lwb_o, kf_o, kb_o, af_o, ab_o, g_o, bon_o) = refs
    i = pl.program_id(1)
    last = pl.num_programs(1) - 1
    nsh = mu_ref.shape[1]
    p = p_ref[0, :, :nsh].astype(F32)
    tt = p.shape[0]
    rows = lax.broadcasted_iota(jnp.int32, (tt, 1), 0)
    prev_row = jnp.where(i == 0, 0.0, pp_ref[0, SUB16 - 1:SUB16, :nsh].astype(F32))
    next_row = jnp.where(i == last, 0.0, pn_ref[0, 0:1, :nsh].astype(F32))
    prev = jnp.where(rows == 0, prev_row, pltpu.roll(p, 1, 0))
    nxt = jnp.where(rows == tt - 1, next_row, pltpu.roll(p, tt - 1, 0))
    mu0, mu1 = mu_ref[0:1, :], mu_ref[1:2, :]
    pf = p * (1.0 - mu0 - mu1) + mu0 * prev + mu1 * nxt
    r = pf[:, :c]
    k = pf[:, c:2 * c]
    v = pf[:, 2 * c:3 * c]
    lw = 3 * c
    nd = w2_ref.shape[0]
    na = a2_ref.shape[0]
    wd = pf[:, lw:lw + nd]
    ad = pf[:, lw + nd:lw + nd + na]
    gd = pf[:, lw + nd + na:]
    if has_vres:
        lora = _dot(p_ref[0, :, nsh:], v2_ref[...])
        v = v + (vf_ref[0].astype(F32) - v) * jax.nn.sigmoid(v0_ref[...] + lora)
    g = _dot(jax.nn.sigmoid(gd).astype(BF16), g2_ref[...])
    kk = k * kk_ref[...]
    kk = kk * lax.rsqrt(jnp.maximum(_dot((kk * kk).astype(BF16), sum_ref[...]), 1e-24))
    wl = w0_ref[...] + _dot(jnp.tanh(wd).astype(BF16), w2_ref[...])
    logw = -math.exp(-0.5) * jax.nn.sigmoid(wl)
    a = jax.nn.sigmoid(a0_ref[...] + _dot(ad.astype(BF16), a2_ref[...]))
    ka = ka_ref[...]
    rk = rk_ref[...]
    k_d = [k * (1.0 + (a[:, d * c:(d + 1) * c] - 1.0) * ka) for d in range(2)]
    bon = _dot((r * (k_d[0] + k_d[1]) * rk).astype(BF16), sum_ref[...]) * v
    r_o[0] = r.astype(BF16)
    v_o[0] = v.astype(BF16)
    kk_o[0] = kk.astype(BF16)
    lwf_o[0] = logw[:, :c]
    lwb_o[0] = logw[:, c:]
    kf_o[0] = k_d[0].astype(BF16)
    kb_o[0] = k_d[1].astype(BF16)
    af_o[0] = a[:, :c].astype(BF16)
    ab_o[0] = a[:, c:].astype(BF16)
    g_o[0] = g.astype(BF16)
    bon_o[0] = bon.astype(BF16)


def _blockdiag2(m):
    k, c = m.shape[1], m.shape[2]
    z = jnp.zeros((k, c), m.dtype)
    return jnp.concatenate([jnp.concatenate([m[0], z], axis=1), jnp.concatenate([z, m[1]], axis=1)], axis=0)


def _rw_prep(p, v_first, shift, w0, w2, a0, a2, g2, k_k, k_a, r_k, vres, c, tt=256):
    b, t, pw = p.shape
    tt = min(tt, t)
    has_vres = vres is not None
    ch = jnp.arange(c) // HEAD
    summ = (ch[:, None] == ch[None, :]).astype(BF16)
    row = lambda a: a.reshape(1, -1)
    args = [p, p, p, shift, row(w0), _blockdiag2(w2).astype(BF16), row(a0), _blockdiag2(a2).astype(BF16),
            g2.astype(BF16), row(k_k), row(k_a), row(r_k), summ]
    full = lambda a: pl.BlockSpec(a.shape, lambda i, j: (0,) * a.ndim)
    nhb = t // SUB16
    in_specs = [pl.BlockSpec((1, tt, pw), lambda i, j: (i, j, 0)),
                pl.BlockSpec((1, SUB16, pw), lambda i, j: (i, jnp.maximum(j * (tt // SUB16) - 1, 0), 0)),
                pl.BlockSpec((1, SUB16, pw), lambda i, j: (i, jnp.minimum((j + 1) * (tt // SUB16), nhb - 1), 0))]
    in_specs += [full(a) for a in args[3:]]
    if has_vres:
        v0, _, v2 = vres
        v2p = jnp.pad(v2, ((0, pw - shift.shape[1] - v2.shape[0]), (0, 0))).astype(BF16)
        extra = [v_first, row(v0), v2p]
        in_specs += [pl.BlockSpec((1, tt, c), lambda i, j: (i, j, 0))] + [full(a) for a in extra[1:]]
        args += extra
    outs = tuple(jax.ShapeDtypeStruct((b, t, c), F32 if i in (3, 4) else BF16) for i in range(11))
    return pl.pallas_call(
        functools.partial(_rw_prep_kernel, has_vres=has_vres, c=c),
        out_shape=outs,
        grid=(b, t // tt),
        in_specs=in_specs,
        out_specs=tuple(pl.BlockSpec((1, tt, c), lambda i, j: (i, j, 0)) for _ in range(11)),
        compiler_params=_params(("parallel", "parallel")),
        name="rw_prep",
    )(*args)


def _scan_chunks(s0, r, k, v, kk, a, lw, rev, bdm):
    L = CHUNK
    n = len(r)
    each = lambda f, *ls: [f(*xs) for xs in zip(*ls)]
    ti = lax.broadcasted_iota(jnp.int32, (L, L), 0)
    si = lax.broadcasted_iota(jnp.int32, (L, L), 1)
    tt = lax.broadcasted_iota(jnp.int32, (L, GW), 0)
    ss = lax.broadcasted_iota(jnp.int32, (L, GW), 1) % L
    eye = (ss == tt).astype(F32)
    tri = [((si >= ti) if q else (si <= ti)).astype(F32).astype(BF16) for q in rev]
    strict = [(ss > tt) if q else (ss < tt) for q in rev]
    incl = [(ss >= tt) if q else (ss <= tt) for q in rev]
    bdf = bdm.astype(F32)

    def bd(x):
        xb = x.astype(BF16)
        return jnp.concatenate([xb] * HEADS_PER_GROUP, axis=0) * bdm

    def bd_t(x):
        xt = x.T.astype(BF16)
        return jnp.concatenate([xt] * HEADS_PER_GROUP, axis=1) * bdm

    def rcmul(x, y):
        return _dot(x.astype(BF16), bd(y))

    parts = each(_split3, lw)
    cum = [_dot(t, p[0]) + _dot(t, p[1]) + _dot(t, p[2]) for t, p in zip(tri, parts)]
    tot = [c[0:1, :] if q else c[L - 1:L, :] for c, q in zip(cum, rev)]
    p_in = each(jnp.exp, cum)
    p_inv = each(lambda c: jnp.exp(-c), cum)
    p_ex = each(lambda c, w: jnp.exp(c - w), cum, lw)
    p_rem = each(lambda t, c: jnp.exp(t - c), tot, cum)
    beta = each(lambda x, y: x * y, kk, a)
    a_t = each(lambda x, p: -x * p, kk, p_ex)
    r_t = each(lambda x, p: x * p, r, p_in)
    k_t = each(lambda x, p: x * p, k, p_inv)
    b_t = each(lambda x, p: x * p, beta, p_inv)
    k_h = each(lambda x, p: x * p, k, p_rem)
    b_h = each(lambda x, p: x * p, beta, p_rem)

    lhs = each(lambda x, y: jnp.concatenate([x, y], axis=0).astype(BF16), a_t, r_t)
    akk = each(lambda x, y: _dot(x, bd_t(y)), lhs, k_t)
    abb = each(lambda x, y: _dot(x, bd_t(y)), lhs, b_t)
    a_ak = each(lambda m, x: jnp.where(m, x[:L], 0.0), strict, akk)
    a_rk = each(lambda m, x: jnp.where(m, x[L:], 0.0), incl, akk)
    a_ab = each(lambda m, x: jnp.where(m, x[:L], 0.0), strict, abb)
    a_rb = each(lambda m, x: jnp.where(m, x[L:], 0.0), incl, abb)

    same = lambda size: (tt // size) == (ss // size)
    d = each(lambda x: jnp.where(same(INV_BASE), x, 0.0), a_ab)
    tinv = each(lambda x: eye + x, d)
    pw = each(rcmul, d, d)
    size = 4
    while size < INV_BASE:
        both = each(lambda p, t: rcmul(jnp.concatenate([p, t], axis=0), p), pw, tinv)
        pw = [x[:L] for x in both]
        tinv = each(lambda t, x: t + x[L:], tinv, both)
        size *= 2
    tinv = each(lambda t, p: t + rcmul(t, p), tinv, pw)
    size = INV_BASE
    while size < L:
        size *= 2
        couple = same(size) & ~same(size // 2)
        e = each(lambda x: jnp.where(couple, x, 0.0), a_ab)
        tinv = each(lambda t, y: t + rcmul(rcmul(t, y), t), tinv, e)

    av = each(lambda x, y, z: rcmul(jnp.concatenate([x, y], axis=0), z), a_ak, a_rk, v)
    ct = each(rcmul, a_rb, tinv)
    wu2 = each(lambda t, c, x, y: _dot(jnp.concatenate([t, c], axis=0).astype(BF16),
                                       jnp.concatenate([bd(x), bd(y[:L])], axis=1)), tinv, ct, a_t, av)
    wu = [x[:L] for x in wu2]
    rbwu = [x[L:] for x in wu2]
    o = each(lambda x, w, s, y: _dot((x + w[:, :GW]).astype(BF16), bd_t(s)) + y[L:] + w[:, GW:],
             r_t, rbwu, s0, av)

    m_bd = each(lambda w, x: _dot_tn(w[:, :GW].astype(BF16), x.astype(BF16)) * bdf, wu, b_h)
    n_full = each(lambda x, w, y, z: _dot_tn(jnp.concatenate([x, w[:, GW:]], axis=0).astype(BF16),
                                             jnp.concatenate([y, z], axis=0).astype(BF16)) * bdf,
                  v, wu, k_h, b_h)
    s1 = []
    for i in range(n):
        n_rc = n_full[i][0:HEAD]
        for h in range(1, HEADS_PER_GROUP):
            n_rc = n_rc + n_full[i][h * HEAD:(h + 1) * HEAD]
        s1.append(_dot(s0[i].astype(BF16), m_bd[i].astype(BF16)) + s0[i] * jnp.exp(tot[i]) + n_rc)
    return s1, o


def _scan_kernel(rf_ref, kf_ref, vf_ref, kkf_ref, af_ref, lwf_ref,
                 rb_ref, kb_ref, vb_ref, kkb_ref, ab_ref, lwb_ref, of_ref, ob_ref, s_ref, *, ng, nb):
    @pl.when(pl.program_id(1) == 0)
    def _():
        s_ref[...] = jnp.zeros_like(s_ref)

    gi = lax.broadcasted_iota(jnp.int32, (GW, GW), 0) // HEAD
    gj = lax.broadcasted_iota(jnp.int32, (GW, GW), 1) // HEAD
    bdm = (gi == gj).astype(F32).astype(BF16)
    dirs = ((rf_ref, kf_ref, vf_ref, kkf_ref, af_ref, lwf_ref), (rb_ref, kb_ref, vb_ref, kkb_ref, ab_ref, lwb_ref))
    chains = [(i, d, h) for i in range(nb) for d in range(2) for h in range(ng)]
    ins = [[dirs[d][q][i, :, h * GW:(h + 1) * GW].astype(F32) for i, d, h in chains] for q in range(6)]
    s1, o = _scan_chunks([s_ref[i, d, h] for i, d, h in chains], *ins, [d == 1 for _, d, _ in chains], bdm)
    for (i, d, h), s_new, o_new in zip(chains, s1, o):
        (of_ref, ob_ref)[d][i, :, h * GW:(h + 1) * GW] = o_new
        s_ref[i, d, h] = s_new


SCAN_BATCH_ROWS = 4


def _wkv_scan(r, v, kk, kf, af, lwf, kb, ab, lwb):
    b, t, c = r.shape
    nc = t // CHUNK
    ng = c // GW
    nb = math.gcd(b, SCAN_BATCH_ROWS)
    fspec = pl.BlockSpec((nb, CHUNK, c), lambda i, j: (i, j, 0))
    bspec = pl.BlockSpec((nb, CHUNK, c), lambda i, j: (i, nc - 1 - j, 0))
    return pl.pallas_call(
        functools.partial(_scan_kernel, ng=ng, nb=nb),
        out_shape=(jax.ShapeDtypeStruct((b, t, c), F32), jax.ShapeDtypeStruct((b, t, c), F32)),
        grid=(b // nb, nc),
        in_specs=[fspec] * 6 + [bspec] * 6,
        out_specs=(fspec, bspec),
        scratch_shapes=[pltpu.VMEM((nb, 2, ng, HEAD, GW), F32)],
        compiler_params=_params(("parallel", "arbitrary")),
        name="rw_scan",
    )(r, kf, v, kk, af, lwf, r, kb, v, kk, ab, lwb)


def _out_proj_kernel(yh_ref, sf_ref, sb_ref, bon_ref, g_ref, lg_ref, lb_ref, avg_ref, wh_ref, wr_ref, r_ref,
                     o_ref):
    s = sf_ref[...] + sb_ref[...]
    mean = _dot3(s, avg_ref[...])
    d = s - mean
    var = _dot((d * d).astype(BF16), avg_ref[...])
    y = d * lax.rsqrt(var + GN_EPS) * lg_ref[...] + lb_ref[...] + bon_ref[...].astype(F32)
    yr = (y * g_ref[...].astype(F32)).astype(BF16)
    o_ref[...] = _dot(yh_ref[...].astype(BF16), wh_ref[...]) + _dot(yr, wr_ref[...]) + r_ref[...]


def _out_proj(yh, sf, sb, bon, g, lnx_g, lnx_b, w, res, tm=512):
    m, ch = yh.shape
    cr = sf.shape[1]
    d = w.shape[1]
    tm = min(tm, m)
    head = jnp.arange(cr) // HEAD
    avg = ((head[:, None] == head[None, :]).astype(F32) / HEAD).astype(BF16)
    row = lambda c: pl.BlockSpec((tm, c), lambda i: (i, 0))
    return pl.pallas_call(
        _out_proj_kernel,
        out_shape=jax.ShapeDtypeStruct((m, d), F32),
        grid=(m // tm,),
        in_specs=[row(ch), row(cr), row(cr), row(cr), row(cr), _const_spec((1, cr)), _const_spec((1, cr)),
                  _const_spec((cr, cr)), _const_spec((ch, d)), _const_spec((cr, d)), row(d)],
        out_specs=row(d),
        compiler_params=_params(("parallel",)),
        name="out_proj",
    )(yh, sf, sb, bon, g, lnx_g.reshape(1, cr), lnx_b.reshape(1, cr), avg, w[:ch], w[ch:], res)


def _rwkv_branch(p, v_first, shift, w0, w2, a0, a2, g2, k_k, k_a, r_k, vres, c):
    r, v, kk, lwf, lwb, kf, kb, af, ab, g, bon = _rw_prep(
        p, v_first, shift, w0, w2, a0, a2, g2, k_k, k_a, r_k, vres, c)
    sf, sb = _wkv_scan(r, v, kk, kf, af, lwf, kb, ab, lwb)
    return sf, sb, bon, g, (v if vres is None else v_first)


def _forward(x, norm1_g, w_in, hy_conv_w, hy_conv_b, hy_f_w1, hy_f_b1, hy_f_w2, hy_f_b2, hy_f_w3, hy_f_b3,
             hy_f_freq, hy_f_wout, hy_skip, hy_norm_g, rw_shift, rw_w0, rw_w2, rw_a0, rw_a2, rw_g2, rw_k_k,
             rw_k_a, rw_r_k, rw_lnx_g, rw_lnx_b, rw_v0, rw_v1, rw_v2, w_out, norm2_g, mlp_w1, mlp_w2,
             final_g):
    b, t, d = x.shape
    depth = w_in.shape[0]
    c_hy = hy_norm_g.shape[1]
    c_rw = rw_lnx_g.shape[1]
    hy_proj = hy_conv_b.shape[1]
    assert b % 2 == 0 and t % CHUNK == 0 and CHUNK == HEAD
    tabs = _dft_tables(t)
    assert tabs["n2"] % SUB16 == 0 and tabs["n1"] % SUB16 == 0 and tabs["n1h"] % SUB == 0
    v_first = None
    for l in range(depth):
        w_l = w_in[l].astype(BF16)
        w_rw = w_l[:, hy_proj:]
        vres = None if l == 0 else (rw_v0[l - 1], rw_v1[l - 1], rw_v2[l - 1])
        if vres is not None:
            v1 = vres[1].astype(BF16)
            w_rw = jnp.concatenate([w_rw, jnp.pad(v1, ((0, 0), (0, -v1.shape[1] % 128)))], axis=1)
        p_hy, p_rw = _in_proj(x, norm1_g[l], w_l[:, :hy_proj], w_rw, tabs["n1h"], tabs["n2"])
        y_hy = _hyena_branch(p_hy, hy_conv_w[l], hy_conv_b[l], hy_f_w1[l], hy_f_b1[l], hy_f_w2[l], hy_f_b2[l],
                             hy_f_w3[l], hy_f_b3[l], hy_f_freq[l], hy_f_wout[l], hy_skip[l], hy_norm_g[l], tabs)
        sf, sb, bon, g, v_first = _rwkv_branch(p_rw, v_first, rw_shift[l], rw_w0[l], rw_w2[l], rw_a0[l], rw_a2[l],
                                               rw_g2[l], rw_k_k[l], rw_k_a[l], rw_r_k[l], vres, c_rw)
        flat = lambda a: a.reshape(b * t, a.shape[-1])
        x = _out_proj(flat(y_hy), flat(sf), flat(sb), flat(bon), flat(g), rw_lnx_g[l], rw_lnx_b[l],
                      w_out[l].astype(BF16), flat(x))
        x = _mlp(x, norm2_g[l], mlp_w1[l].astype(BF16), mlp_w2[l].astype(BF16),
                 final_g=final_g if l == depth - 1 else None).reshape(b, t, d)
    return x


def kernel(x, norm1_g, w_in, hy_conv_w, hy_conv_b, hy_f_w1, hy_f_b1, hy_f_w2, hy_f_b2, hy_f_w3, hy_f_b3,
           hy_f_freq, hy_f_wout, hy_skip, hy_norm_g, rw_shift, rw_w0, rw_w2, rw_a0, rw_a2, rw_g2, rw_k_k,
           rw_k_a, rw_r_k, rw_lnx_g, rw_lnx_b, rw_v0, rw_v1, rw_v2, w_out, norm2_g, mlp_w1, mlp_w2, final_g):
    return _forward(x, norm1_g, w_in, hy_conv_w, hy_conv_b, hy_f_w1, hy_f_b1, hy_f_w2, hy_f_b2, hy_f_w3,
                    hy_f_b3, hy_f_freq, hy_f_wout, hy_skip, hy_norm_g, rw_shift, rw_w0, rw_w2, rw_a0, rw_a2,
                    rw_g2, rw_k_k, rw_k_a, rw_r_k, rw_lnx_g, rw_lnx_b, rw_v0, rw_v1, rw_v2, w_out, norm2_g,
                    mlp_w1, mlp_w2, final_g)
```

```python
import functools
import math

import jax
import jax.numpy as jnp
from jax import lax
from jax.experimental import pallas as pl
from jax.experimental.pallas import tpu as pltpu

F32 = jnp.float32
BF16 = jnp.bfloat16
HP = lax.Precision.HIGHEST

HEAD = 64
HEADS_PER_GROUP = 4
GW = HEAD * HEADS_PER_GROUP
CHUNK = 64
INV_BASE = 8
RMS_EPS = 1e-5
GN_EPS = HEAD * 1e-5
HY_TARGET = 1e-2
HY_FAST_PCT = 0.3
HY_SLOW_PCT = 1.5
VMEM_LIMIT = 56 * 1024 * 1024


def _params(sem, vmem=VMEM_LIMIT):
    return pltpu.CompilerParams(dimension_semantics=sem, vmem_limit_bytes=vmem)


def _dot(a, b, precision=None):
    return jnp.dot(a, b, preferred_element_type=F32, precision=precision)


def _dot_tn(a, b, precision=None):
    return lax.dot_general(a, b, (((0,), (0,)), ((), ())), preferred_element_type=F32,
                           precision=precision)


def _const_spec(shape):
    return pl.BlockSpec(shape, lambda *_: (0,) * len(shape), pipeline_mode=pl.Buffered(1))


def _split3(x):
    h1 = x.astype(BF16)
    r1 = x - h1.astype(F32)
    h2 = r1.astype(BF16)
    h3 = (r1 - h2.astype(F32)).astype(BF16)
    return h1, h2, h3


def _dot3(x, m):
    h1, h2, h3 = _split3(x)
    return _dot(h1, m) + _dot(h2, m) + _dot(h3, m)


def _mlp_kernel(x_ref, g_ref, w1_ref, w2_ref, *rest, nchunk):
    x = x_ref[...]
    hn = (x * lax.rsqrt(jnp.mean(x * x, axis=-1, keepdims=True) + RMS_EPS) * g_ref[...]).astype(BF16)
    ff = w1_ref.shape[1]
    cw = ff // nchunk
    acc = x
    for j in range(nchunk):
        h = _dot(hn, w1_ref[:, j * cw:(j + 1) * cw])
        h = jnp.square(jnp.maximum(h, 0.0)).astype(BF16)
        acc = acc + _dot(h, w2_ref[j * cw:(j + 1) * cw, :])
    if len(rest) == 2:
        gf_ref, o_ref = rest
        acc = acc * lax.rsqrt(jnp.mean(acc * acc, axis=-1, keepdims=True) + RMS_EPS) * gf_ref[...]
    else:
        (o_ref,) = rest
    o_ref[...] = acc


def _mlp(x2d, g, w1, w2, final_g=None, tm=512, nchunk=4):
    m, d = x2d.shape
    ff = w1.shape[1]
    tm = min(tm, m)
    in_specs = [pl.BlockSpec((tm, d), lambda i: (i, 0)), _const_spec((1, d)),
                _const_spec((d, ff)), _const_spec((ff, d))]
    args = [x2d, g.reshape(1, d), w1, w2]
    if final_g is not None:
        in_specs.append(_const_spec((1, d)))
        args.append(final_g.reshape(1, d))
    return pl.pallas_call(
        functools.partial(_mlp_kernel, nchunk=nchunk),
        out_shape=jax.ShapeDtypeStruct((m, d), F32),
        grid=(m // tm,),
        in_specs=in_specs,
        out_specs=pl.BlockSpec((tm, d), lambda i: (i, 0)),
        compiler_params=_params(("parallel",)),
        name="mlp",
    )(*args)


SUB = 8
SUB16 = 16


def _rms(x, g):
    return (x * lax.rsqrt(jnp.mean(x * x, axis=-1, keepdims=True) + RMS_EPS) * g).astype(BF16)


def _in_proj_kernel(x_ref, g_ref, wh_ref, wr_ref, oh_ref, or_ref, *, n2):
    hn = _rms(x_ref[0], g_ref[...])
    or_ref[0] = _dot(hn, wr_ref[...]).astype(or_ref.dtype)
    ph = _dot(hn, wh_ref[...])
    for j in range(SUB):
        oh_ref[0, :, j, :] = ph[j * n2:(j + 1) * n2]


def _in_proj(x, g, w_hy, w_rw, n1h, n2):
    b, t, d = x.shape
    ph, pr = w_hy.shape[1], w_rw.shape[1]
    tm = SUB * n2
    p_hy, p_rw = pl.pallas_call(
        functools.partial(_in_proj_kernel, n2=n2),
        out_shape=(jax.ShapeDtypeStruct((b, n2, n1h, ph), F32), jax.ShapeDtypeStruct((b, t, pr), BF16)),
        grid=(b, t // tm),
        in_specs=[pl.BlockSpec((1, tm, d), lambda i, j: (i, j, 0)), _const_spec((1, d)),
                  _const_spec((d, ph)), _const_spec((d, pr))],
        out_specs=(pl.BlockSpec((1, n2, SUB, ph), lambda i, j: (i, 0, j, 0)),
                   pl.BlockSpec((1, tm, pr), lambda i, j: (i, j, 0))),
        compiler_params=_params(("parallel", "parallel")),
        name="in_proj",
    )(x, g.reshape(1, d), w_hy, w_rw)
    return p_hy.reshape(b, t, ph), p_rw


def _shortconv_kernel(p_ref, w_ref, b_ref, o_ref, *, n1h, n2):
    w0 = w_ref[0:1, :]
    w1 = w_ref[1:2, :]
    w2 = w_ref[2:3, :]
    bias = b_ref[...]
    rows = lax.broadcasted_iota(jnp.int32, (n1h, 1), 0)

    def blk(i):
        return p_ref[0, pl.ds(pl.multiple_of(i * n1h, n1h), n1h), :].astype(F32)

    def body(i, carry):
        o_ref[0, pl.ds(pl.multiple_of(i * n1h, n1h), n1h), :] = (
            w0 * blk(i - 1) + w1 * blk(i) + w2 * blk(i + 1) + bias).astype(o_ref.dtype)
        return carry

    lax.fori_loop(1, n2 - 1, body, 0)
    last = p_ref[0, (n2 - 1) * n1h:, :].astype(F32)
    first = p_ref[0, :n1h, :].astype(F32)
    prev0 = jnp.where(rows == 0, 0.0, pltpu.roll(last, 1, 0))
    o_ref[0, :n1h, :] = (w0 * prev0 + w1 * first + w2 * p_ref[0, n1h:2 * n1h, :].astype(F32)
                         + bias).astype(o_ref.dtype)
    nxt = jnp.where(rows == n1h - 1, 0.0, pltpu.roll(first, n1h - 1, 0))
    o_ref[0, (n2 - 1) * n1h:, :] = (w0 * p_ref[0, (n2 - 2) * n1h:(n2 - 1) * n1h, :].astype(F32) + w1 * last
                                    + w2 * nxt + bias).astype(o_ref.dtype)


def _shortconv(p, w, bias, n1h, n2, cb=128):
    b, t, c = p.shape
    return pl.pallas_call(
        functools.partial(_shortconv_kernel, n1h=n1h, n2=n2),
        out_shape=jax.ShapeDtypeStruct((b, t, c), BF16),
        grid=(b, c // cb),
        in_specs=[pl.BlockSpec((1, t, cb), lambda i, j: (i, 0, j)),
                  pl.BlockSpec((3, cb), lambda i, j: (0, j)),
                  pl.BlockSpec((1, cb), lambda i, j: (0, j))],
        out_specs=pl.BlockSpec((1, t, cb), lambda i, j: (i, 0, j)),
        compiler_params=_params(("parallel", "parallel")),
        name="hy_shortconv",
    )(p, w, bias.reshape(1, c))


def _dft_tables(t):
    n = 2 * t
    n1 = 1 << ((n.bit_length() - 1) // 2)
    n2 = n // n1
    n1h = n1 // 2
    two_pi = 2.0 * math.pi
    k1 = jnp.arange(n1, dtype=jnp.int32)
    n2i = jnp.arange(n2, dtype=jnp.int32)

    def cs(prod, mod):
        ang = (two_pi / mod) * (prod % mod).astype(F32)
        return jnp.cos(ang), jnp.sin(ang)

    tpos = n2i[:, None] + n2 * jnp.arange(n1h, dtype=jnp.int32)[None, :]
    c, s = cs(k1[None, :, None] * tpos[:, None, :], n)
    g1 = jnp.concatenate([jnp.concatenate([c, s], axis=2), jnp.concatenate([-s, c], axis=2)], axis=1)
    ct, st = jnp.swapaxes(c, 1, 2) / n, jnp.swapaxes(s, 1, 2) / n
    g4a = jnp.concatenate([ct, st], axis=1)
    g4b = jnp.concatenate([-st, ct], axis=1)
    tposf = n2i[:, None] + n2 * jnp.arange(n1, dtype=jnp.int32)[None, :]
    cf, sf = cs(k1[None, :, None] * tposf[:, None, :], n)
    g1f = jnp.concatenate([cf, -sf], axis=1)
    c2, s2 = cs(n2i[:, None] * n2i[None, :], n2)
    f2a = jnp.concatenate([c2, -s2], axis=0)
    f2b = jnp.concatenate([s2, c2], axis=0)
    f3 = jnp.concatenate([jnp.concatenate([c2, -s2], axis=1), jnp.concatenate([s2, c2], axis=1)], axis=0)
    g4 = jnp.concatenate([g4a, g4b], axis=2)
    f2 = jnp.concatenate([f2a, f2b], axis=1)
    return dict(n1=n1, n2=n2, n1h=n1h, g1=g1.astype(BF16), g4=g4.astype(BF16), g1f=g1f.astype(BF16),
                f2=f2.astype(BF16), f3=f3.astype(BF16))


def _filter_mlp_kernel(z_ref, w1_ref, b1_ref, w2_ref, b2_ref, w3_ref, b3_ref, fr_ref, wo_ref, dl_ref,
                       o_ref, *, t):
    z = z_ref[...]
    h = jnp.sin(fr_ref[0:1, :] * (_dot(z, w1_ref[...], HP) + b1_ref[...]))
    h = jnp.sin(fr_ref[1:2, :] * (_dot(h, w2_ref[...], HP) + b2_ref[...]))
    h = jnp.sin(fr_ref[2:3, :] * (_dot(h, w3_ref[...], HP) + b3_ref[...]))
    window = jnp.exp(-z[:, 0:1] * dl_ref[...])
    tr = z.shape[0]
    row = pl.program_id(0) * tr + lax.broadcasted_iota(jnp.int32, (tr, 1), 0)
    o_ref[...] = jnp.where(row == t, 0.0, _dot(h, wo_ref[0], HP) * window)


def _filter_taps(t, w1, b1, w2, b2, w3, b3, freq, w_out, c_hy, tr=512):
    emb, width = w1.shape
    bands = (emb - 1) // 2
    pos = jnp.arange(t, dtype=F32)
    tt = pos / max(t - 1, 1)
    fr = jnp.linspace(1e-4, bands - 1, bands, dtype=F32)
    ang = (2.0 * math.pi / t) * pos[:, None] * fr[None, :]
    z = jnp.concatenate([tt[:, None], jnp.cos(ang), -jnp.sin(ang)], axis=-1)
    z = jnp.pad(z, ((0, 0), (0, 128 - emb)))
    w1 = jnp.pad(w1, ((0, 128 - emb), (0, 0)))
    emb = 128
    z2 = jnp.concatenate([z, z[:1], z[1:][::-1]], axis=0)
    max_decay = math.log(HY_TARGET) / HY_FAST_PCT
    min_decay = math.log(HY_TARGET) / HY_SLOW_PCT
    delta = jnp.abs(jnp.linspace(min_decay, max_decay, c_hy, dtype=F32))
    order = w_out.shape[1] // (2 * c_hy)
    nout = order * c_hy
    w_dir = w_out.reshape(width, order, 2, c_hy).transpose(2, 0, 1, 3).reshape(2, width, nout)
    delta_full = jnp.tile(delta, order).reshape(1, nout)
    tr = min(tr, t)
    nt = t // tr
    full = lambda a: pl.BlockSpec(a.shape, lambda i: (0,) * a.ndim)
    args = [z2, w1, b1.reshape(1, -1), w2, b2.reshape(1, -1), w3, b3.reshape(1, -1), freq, w_dir, delta_full]
    in_specs = [pl.BlockSpec((tr, emb), lambda i: (i, 0))] + [full(a) for a in args[1:]]
    in_specs[8] = pl.BlockSpec((1, width, nout), lambda i: (i // nt, 0, 0))
    return pl.pallas_call(
        functools.partial(_filter_mlp_kernel, t=t),
        out_shape=jax.ShapeDtypeStruct((2 * t, nout), F32),
        grid=(2 * nt,),
        in_specs=in_specs,
        out_specs=pl.BlockSpec((tr, nout), lambda i: (i, 0)),
        compiler_params=_params(("parallel",)),
        name="hy_filter_mlp",
    )(*args)


def _filter_s1_kernel(x_ref, g_ref, o_ref, ss_ref, *, g, oc):
    @pl.when(pl.program_id(0) == 0)
    def _():
        ss_ref[...] = jnp.zeros_like(ss_ref)

    n1 = x_ref.shape[0]
    x = jnp.concatenate([x_ref[:, j, :] for j in range(g)], axis=0)
    ss_ref[...] += jnp.sum(x * x, axis=0, keepdims=True)
    xb = x.astype(BF16)
    for j in range(g):
        o_ref[j] = _dot(g_ref[j], xb[j * n1:(j + 1) * n1])


def _filter_s2_kernel(sr_ref, si_ref, f2_ref, ss_ref, o_ref, *, g, oc):
    n2 = sr_ref.shape[0]
    scale = lax.rsqrt(ss_ref[...] + 1e-6)
    s = jnp.concatenate([ref[:, j, :] for j in range(g) for ref in (sr_ref, si_ref)], axis=0).astype(BF16)
    for j in range(g):
        o_ref[j] = _dot(f2_ref[...], s[2 * j * n2:2 * (j + 1) * n2]) * scale


def _filter_spectrum(two, tabs):
    n, oc = two.shape
    n1, n2 = tabs["n1"], tabs["n2"]
    g = SUB
    s1, ss = pl.pallas_call(
        functools.partial(_filter_s1_kernel, g=g, oc=oc),
        out_shape=(jax.ShapeDtypeStruct((n2, 2 * n1, oc), F32), jax.ShapeDtypeStruct((1, oc), F32)),
        grid=(n2 // g,),
        in_specs=[pl.BlockSpec((n1, g, oc), lambda j: (0, j, 0)),
                  pl.BlockSpec((g, 2 * n1, n1), lambda j: (j, 0, 0))],
        out_specs=(pl.BlockSpec((g, 2 * n1, oc), lambda j: (j, 0, 0)),
                   pl.BlockSpec((1, oc), lambda j: (0, 0))),
        compiler_params=_params(("arbitrary",)),
        name="hy_filter_dft1",
    )(two.reshape(n1, n2, oc), tabs["g1f"])
    return pl.pallas_call(
        functools.partial(_filter_s2_kernel, g=g, oc=oc),
        out_shape=jax.ShapeDtypeStruct((n1, 2 * n2, oc), F32),
        grid=(n1 // g,),
        in_specs=[pl.BlockSpec((n2, g, oc), lambda k: (0, k, 0)),
                  pl.BlockSpec((n2, g, oc), lambda k: (0, n1 // g + k, 0)),
                  pl.BlockSpec((2 * n2, 2 * n2), lambda k: (0, 0)),
                  pl.BlockSpec((1, oc), lambda k: (0, 0))],
        out_specs=pl.BlockSpec((g, 2 * n2, oc), lambda k: (k, 0, 0)),
        compiler_params=_params(("parallel",)),
        name="hy_filter_dft2",
    )(s1, s1, tabs["f2"], ss)


def _conv_s1_kernel(u_ref, g_ref, o_ref, t_ref, *, g, n1h):
    for j in range(g):
        rows = slice(j * n1h, (j + 1) * n1h)
        rhs = jnp.concatenate([u_ref[0, 0, rows, :], u_ref[0, 1, rows, :]], axis=0)
        t_ref[:, j, :] = _dot(g_ref[j], rhs)
    o_ref[0] = t_ref[...].astype(o_ref.dtype)


def _conv_s23_kernel(sr_ref, si_ref, f2_ref, f3_ref, h_ref, o_ref, t_ref, *, g, n2):
    for j in range(g):
        s = jnp.concatenate([sr_ref[0, j], si_ref[0, j]], axis=0)
        x = _dot(f2_ref[...], s)
        xr, xi = x[:n2], x[n2:]
        hr, hi = h_ref[j, :n2, :], h_ref[j, n2:, :]
        y = jnp.concatenate([xr * hr - xi * hi, xr * hi + xi * hr], axis=0).astype(BF16)
        t_ref[:, j, :] = _dot(f3_ref[...], y)
    o_ref[0] = t_ref[...].astype(o_ref.dtype)


def _conv_s4_kernel(rr_ref, ri_ref, g4_ref, u_ref, gate_ref, skip_ref, *rest, g, n1h, norm):
    if norm:
        ng_ref, avg_ref, o_ref = rest
    else:
        (o_ref,) = rest
    skip = skip_ref[...]
    for j in range(g):
        rows = slice(j * n1h, (j + 1) * n1h)
        y = _dot(g4_ref[j], jnp.concatenate([rr_ref[0, j], ri_ref[0, j]], axis=0))
        for q in range(2):
            z = gate_ref[0, q, rows, :].astype(F32) * (y[q * n1h:(q + 1) * n1h]
                                                       + u_ref[0, q, rows, :].astype(F32) * skip)
            if norm:
                ms = _dot((z * z).astype(BF16), avg_ref[...])
                o_ref[0, q, :, j, :] = z * lax.rsqrt(ms + RMS_EPS) * ng_ref[...]
            else:
                o_ref[0, q, rows, :] = z.astype(o_ref.dtype)


def _long_conv_gate(u_arr, u_col, gate_arr, gate_col, h_spec, h_col, skip, tabs, c, norm_g=None):
    b, t, _ = u_arr.shape
    n1, n2, n1h = tabs["n1"], tabs["n2"], tabs["n1h"]
    g = SUB
    npair = b // 2
    u4 = u_arr.reshape(npair, 2, t, u_arr.shape[2])
    gate4 = gate_arr.reshape(npair, 2, t, gate_arr.shape[2])
    gt = SUB16
    s1 = pl.pallas_call(
        functools.partial(_conv_s1_kernel, g=gt, n1h=n1h),
        out_shape=jax.ShapeDtypeStruct((npair, 2 * n1, n2, c), BF16),
        grid=(n2 // gt, npair),
        in_specs=[pl.BlockSpec((1, 2, gt * n1h, c), lambda j, p: (p, 0, j, u_col)),
                  pl.BlockSpec((gt, 2 * n1, n1), lambda j, p: (j, 0, 0))],
        out_specs=pl.BlockSpec((1, 2 * n1, gt, c), lambda j, p: (p, 0, j, 0)),
        scratch_shapes=[pltpu.VMEM((2 * n1, gt, c), F32)],
        compiler_params=_params(("parallel", "parallel")),
        name="hy_conv_dft1",
    )(u4, tabs["g1"])
    r = pl.pallas_call(
        functools.partial(_conv_s23_kernel, g=gt, n2=n2),
        out_shape=jax.ShapeDtypeStruct((npair, 2 * n2, n1, c), BF16),
        grid=(n1 // gt, npair),
        in_specs=[pl.BlockSpec((1, gt, n2, c), lambda k, p: (p, k, 0, 0)),
                  pl.BlockSpec((1, gt, n2, c), lambda k, p: (p, n1 // gt + k, 0, 0)),
                  pl.BlockSpec((2 * n2, 2 * n2), lambda k, p: (0, 0)),
                  pl.BlockSpec((2 * n2, 2 * n2), lambda k, p: (0, 0)),
                  pl.BlockSpec((gt, 2 * n2, c), lambda k, p: (k, 0, h_col))],
        out_specs=pl.BlockSpec((1, 2 * n2, gt, c), lambda k, p: (p, 0, k, 0)),
        scratch_shapes=[pltpu.VMEM((2 * n2, gt, c), F32)],
        compiler_params=_params(("parallel", "parallel")),
        name="hy_conv_dft23",
    )(s1, s1, tabs["f2"], tabs["f3"], h_spec)
    norm = norm_g is not None
    in_specs = [pl.BlockSpec((1, g, n1, c), lambda p, j: (p, j, 0, 0)),
                pl.BlockSpec((1, g, n1, c), lambda p, j: (p, n2 // g + j, 0, 0)),
                pl.BlockSpec((g, n1, 2 * n1), lambda p, j: (j, 0, 0)),
                pl.BlockSpec((1, 2, g * n1h, c), lambda p, j: (p, 0, j, u_col)),
                pl.BlockSpec((1, 2, g * n1h, c), lambda p, j: (p, 0, j, gate_col)),
                pl.BlockSpec((1, c), lambda p, j: (0, 0))]
    args = [r, r, tabs["g4"], u4, gate4, skip.reshape(1, c)]
    if norm:
        ch = jnp.arange(c) // HEAD
        avg = ((ch[:, None] == ch[None, :]).astype(F32) / HEAD).astype(BF16)
        in_specs += [pl.BlockSpec((1, c), lambda p, j: (0, 0)), pl.BlockSpec((c, c), lambda p, j: (0, 0))]
        args += [norm_g.reshape(1, c), avg]
        out_shape = jax.ShapeDtypeStruct((npair, 2, n1h, n2, c), F32)
        out_spec = pl.BlockSpec((1, 2, n1h, g, c), lambda p, j: (p, 0, 0, j, 0))
    else:
        out_shape = jax.ShapeDtypeStruct((npair, 2, t, c), BF16)
        out_spec = pl.BlockSpec((1, 2, g * n1h, c), lambda p, j: (p, 0, j, 0))
    out = pl.pallas_call(
        functools.partial(_conv_s4_kernel, g=g, n1h=n1h, norm=norm),
        out_shape=out_shape,
        grid=(npair, n2 // g),
        in_specs=in_specs,
        out_specs=out_spec,
        compiler_params=_params(("parallel", "parallel")),
        name="hy_conv_dft4",
    )(*args)
    return out.reshape(b, t, c)


def _hyena_branch(p, conv_w, conv_b, fw1, fb1, fw2, fb2, fw3, fb3, ffreq, fwout, skip, norm_g, tabs):
    b, t, _ = p.shape
    c = norm_g.shape[0]
    n1, n2, n1h = tabs["n1"], tabs["n2"], tabs["n1h"]
    u = _shortconv(p, conv_w, conv_b, n1h, n2)
    two = _filter_taps(t, fw1, fb1, fw2, fb2, fw3, fb3, ffreq, fwout, c)
    hspec = _filter_spectrum(two, tabs)
    z = _long_conv_gate(u, 2, u, 0, hspec, 0, skip[0], tabs, c)
    return _long_conv_gate(z, 0, u, 1, hspec, 1, skip[1], tabs, c, norm_g=norm_g)


def _rw_prep_kernel(*refs, has_vres, c):
    if has_vres:
        (p_ref, pp_ref, pn_ref, mu_ref, w0_ref, w2_ref, a0_ref, a2_ref, g2_ref, kk_ref, ka_ref, rk_ref,
         sum_ref, vf_ref, v0_ref, v2_ref,
         r_o, v_o, kk_o, lwf_o, lwb_o, kf_o, kb_o, af_o, ab_o, g_o, bon_o) = refs
    else:
        (p_ref, pp_ref, pn_ref, mu_ref, w0_ref, w2_ref, a0_ref, a2_ref, g2_ref, kk_ref, ka_ref, rk_ref,
         sum_ref,
         r_o, v_o, kk_o, lwf_o, lwb_o, kf_o, kb_o, af_o, ab_o, g_o, bon_o) = refs
    i = pl.program_id(1)
    last = pl.num_programs(1) - 1
    nsh = mu_ref.shape[1]
    p = p_ref[0, :, :nsh].astype(F32)
    tt = p.shape[0]
    rows = lax.broadcasted_iota(jnp.int32, (tt, 1), 0)
    prev_row = jnp.where(i == 0, 0.0, pp_ref[0, SUB16 - 1:SUB16, :nsh].astype(F32))
    next_row = jnp.where(i == last, 0.0, pn_ref[0, 0:1, :nsh].astype(F32))
    prev = jnp.where(rows == 0, prev_row, pltpu.roll(p, 1, 0))
    nxt = jnp.where(rows == tt - 1, next_row, pltpu.roll(p, tt - 1, 0))
    mu0, mu1 = mu_ref[0:1, :], mu_ref[1:2, :]
    pf = p * (1.0 - mu0 - mu1) + mu0 * prev + mu1 * nxt
    r = pf[:, :c]
    k = pf[:, c:2 * c]
    v = pf[:, 2 * c:3 * c]
    lw = 3 * c
    nd = w2_ref.shape[0]
    na = a2_ref.shape[0]
    wd = pf[:, lw:lw + nd]
    ad = pf[:, lw + nd:lw + nd + na]
    gd = pf[:, lw + nd + na:]
    if has_vres:
        lora = _dot(p_ref[0, :, nsh:], v2_ref[...])
        v = v + (vf_ref[0].astype(F32) - v) * jax.nn.sigmoid(v0_ref[...] + lora)
    g = _dot(jax.nn.sigmoid(gd).astype(BF16), g2_ref[...])
    kk = k * kk_ref[...]
    kk = kk * lax.rsqrt(jnp.maximum(_dot((kk * kk).astype(BF16), sum_ref[...]), 1e-24))
    wl = w0_ref[...] + _dot(jnp.tanh(wd).astype(BF16), w2_ref[...])
    logw = -math.exp(-0.5) * jax.nn.sigmoid(wl)
    a = jax.nn.sigmoid(a0_ref[...] + _dot(ad.astype(BF16), a2_ref[...]))
    ka = ka_ref[...]
    rk = rk_ref[...]
    k_d = [k * (1.0 + (a[:, d * c:(d + 1) * c] - 1.0) * ka) for d in range(2)]
    bon = _dot((r * (k_d[0] + k_d[1]) * rk).astype(BF16), sum_ref[...]) * v
    r_o[0] = r.astype(BF16)
    v_o[0] = v.astype(BF16)
    kk_o[0] = kk.astype(BF16)
    lwf_o[0] = logw[:, :c]
    lwb_o[0] = logw[:, c:]
    kf_o[0] = k_d[0].astype(BF16)
    kb_o[0] = k_d[1].astype(BF16)
    af_o[0] = a[:, :c].astype(BF16)
    ab_o[0] = a[:, c:].astype(BF16)
    g_o[0] = g.astype(BF16)
    bon_o[0] = bon.astype(BF16)


def _blockdiag2(m):
    k, c = m.shape[1], m.shape[2]
    z = jnp.zeros((k, c), m.dtype)
    return jnp.concatenate([jnp.concatenate([m[0], z], axis=1), jnp.concatenate([z, m[1]], axis=1)], axis=0)


def _rw_prep(p, v_first, shift, w0, w2, a0, a2, g2, k_k, k_a, r_k, vres, c, tt=256):
    b, t, pw = p.shape
    tt = min(tt, t)
    has_vres = vres is not None
    ch = jnp.arange(c) // HEAD
    summ = (ch[:, None] == ch[None, :]).astype(BF16)
    row = lambda a: a.reshape(1, -1)
    args = [p, p, p, shift, row(w0), _blockdiag2(w2).astype(BF16), row(a0), _blockdiag2(a2).astype(BF16),
            g2.astype(BF16), row(k_k), row(k_a), row(r_k), summ]
    full = lambda a: pl.BlockSpec(a.shape, lambda i, j: (0,) * a.ndim)
    nhb = t // SUB16
    in_specs = [pl.BlockSpec((1, tt, pw), lambda i, j: (i, j, 0)),
                pl.BlockSpec((1, SUB16, pw), lambda i, j: (i, jnp.maximum(j * (tt // SUB16) - 1, 0), 0)),
                pl.BlockSpec((1, SUB16, pw), lambda i, j: (i, jnp.minimum((j + 1) * (tt // SUB16), nhb - 1), 0))]
    in_specs += [full(a) for a in args[3:]]
    if has_vres:
        v0, _, v2 = vres
        v2p = jnp.pad(v2, ((0, pw - shift.shape[1] - v2.shape[0]), (0, 0))).astype(BF16)
        extra = [v_first, row(v0), v2p]
        in_specs += [pl.BlockSpec((1, tt, c), lambda i, j: (i, j, 0))] + [full(a) for a in extra[1:]]
        args += extra
    outs = tuple(jax.ShapeDtypeStruct((b, t, c), F32 if i in (3, 4) else BF16) for i in range(11))
    return pl.pallas_call(
        functools.partial(_rw_prep_kernel, has_vres=has_vres, c=c),
        out_shape=outs,
        grid=(b, t // tt),
        in_specs=in_specs,
        out_specs=tuple(pl.BlockSpec((1, tt, c), lambda i, j: (i, j, 0)) for _ in range(11)),
        compiler_params=_params(("parallel", "parallel")),
        name="rw_prep",
    )(*args)


def _scan_chunks(s0, r, k, v, kk, a, lw, rev, bdm):
    L = CHUNK
    n = len(r)
    each = lambda f, *ls: [f(*xs) for xs in zip(*ls)]
    ti = lax.broadcasted_iota(jnp.int32, (L, L), 0)
    si = lax.broadcasted_iota(jnp.int32, (L, L), 1)
    tt = lax.broadcasted_iota(jnp.int32, (L, GW), 0)
    ss = lax.broadcasted_iota(jnp.int32, (L, GW), 1) % L
    eye = (ss == tt).astype(F32)
    tri = [((si >= ti) if q else (si <= ti)).astype(F32).astype(BF16) for q in rev]
    strict = [(ss > tt) if q else (ss < tt) for q in rev]
    incl = [(ss >= tt) if q else (ss <= tt) for q in rev]
    bdf = bdm.astype(F32)

    def bd(x):
        xb = x.astype(BF16)
        return jnp.concatenate([xb] * HEADS_PER_GROUP, axis=0) * bdm

    def bd_t(x):
        xt = x.T.astype(BF16)
        return jnp.concatenate([xt] * HEADS_PER_GROUP, axis=1) * bdm

    def rcmul(x, y):
        return _dot(x.astype(BF16), bd(y))

    parts = each(_split3, lw)
    cum = [_dot(t, p[0]) + _dot(t, p[1]) + _dot(t, p[2]) for t, p in zip(tri, parts)]
    tot = [c[0:1, :] if q else c[L - 1:L, :] for c, q in zip(cum, rev)]
    p_in = each(jnp.exp, cum)
    p_inv = each(lambda c: jnp.exp(-c), cum)
    p_ex = each(lambda c, w: jnp.exp(c - w), cum, lw)
    p_rem = each(lambda t, c: jnp.exp(t - c), tot, cum)
    beta = each(lambda x, y: x * y, kk, a)
    a_t = each(lambda x, p: -x * p, kk, p_ex)
    r_t = each(lambda x, p: x * p, r, p_in)
    k_t = each(lambda x, p: x * p, k, p_inv)
    b_t = each(lambda x, p: x * p, beta, p_inv)
    k_h = each(lambda x, p: x * p, k, p_rem)
    b_h = each(lambda x, p: x * p, beta, p_rem)

    lhs = each(lambda x, y: jnp.concatenate([x, y], axis=0).astype(BF16), a_t, r_t)
    akk = each(lambda x, y: _dot(x, bd_t(y)), lhs, k_t)
    abb = each(lambda x, y: _dot(x, bd_t(y)), lhs, b_t)
    a_ak = each(lambda m, x: jnp.where(m, x[:L], 0.0), strict, akk)
    a_rk = each(lambda m, x: jnp.where(m, x[L:], 0.0), incl, akk)
    a_ab = each(lambda m, x: jnp.where(m, x[:L], 0.0), strict, abb)
    a_rb = each(lambda m, x: jnp.where(m, x[L:], 0.0), incl, abb)

    same = lambda size: (tt // size) == (ss // size)
    d = each(lambda x: jnp.where(same(INV_BASE), x, 0.0), a_ab)
    tinv = each(lambda x: eye + x, d)
    pw = each(rcmul, d, d)
    size = 4
    while size < INV_BASE:
        both = each(lambda p, t: rcmul(jnp.concatenate([p, t], axis=0), p), pw, tinv)
        pw = [x[:L] for x in both]
        tinv = each(lambda t, x: t + x[L:], tinv, both)
        size *= 2
    tinv = each(lambda t, p: t + rcmul(t, p), tinv, pw)
    size = INV_BASE
    while size < L:
        size *= 2
        couple = same(size) & ~same(size // 2)
        e = each(lambda x: jnp.where(couple, x, 0.0), a_ab)
        tinv = each(lambda t, y: t + rcmul(rcmul(t, y), t), tinv, e)

    av = each(lambda x, y, z: rcmul(jnp.concatenate([x, y], axis=0), z), a_ak, a_rk, v)
    ct = each(rcmul, a_rb, tinv)
    wu2 = each(lambda t, c, x, y: _dot(jnp.concatenate([t, c], axis=0).astype(BF16),
                                       jnp.concatenate([bd(x), bd(y[:L])], axis=1)), tinv, ct, a_t, av)
    wu = [x[:L] for x in wu2]
    rbwu = [x[L:] for x in wu2]
    o = each(lambda x, w, s, y: _dot((x + w[:, :GW]).astype(BF16), bd_t(s)) + y[L:] + w[:, GW:],
             r_t, rbwu, s0, av)

    m_bd = each(lambda w, x: _dot_tn(w[:, :GW].astype(BF16), x.astype(BF16)) * bdf, wu, b_h)
    n_full = each(lambda x, w, y, z: _dot_tn(jnp.concatenate([x, w[:, GW:]], axis=0).astype(BF16),
                                             jnp.concatenate([y, z], axis=0).astype(BF16)) * bdf,
                  v, wu, k_h, b_h)
    s1 = []
    for i in range(n):
        n_rc = n_full[i][0:HEAD]
        for h in range(1, HEADS_PER_GROUP):
            n_rc = n_rc + n_full[i][h * HEAD:(h + 1) * HEAD]
        s1.append(_dot(s0[i].astype(BF16), m_bd[i].astype(BF16)) + s0[i] * jnp.exp(tot[i]) + n_rc)
    return s1, o


def _scan_kernel(rf_ref, kf_ref, vf_ref, kkf_ref, af_ref, lwf_ref,
                 rb_ref, kb_ref, vb_ref, kkb_ref, ab_ref, lwb_ref, of_ref, ob_ref, s_ref, *, ng, nb):
    @pl.when(pl.program_id(1) == 0)
    def _():
        s_ref[...] = jnp.zeros_like(s_ref)

    gi = lax.broadcasted_iota(jnp.int32, (GW, GW), 0) // HEAD
    gj = lax.broadcasted_iota(jnp.int32, (GW, GW), 1) // HEAD
    bdm = (gi == gj).astype(F32).astype(BF16)
    dirs = ((rf_ref, kf_ref, vf_ref, kkf_ref, af_ref, lwf_ref), (rb_ref, kb_ref, vb_ref, kkb_ref, ab_ref, lwb_ref))
    chains = [(i, d, h) for i in range(nb) for d in range(2) for h in range(ng)]
    ins = [[dirs[d][q][i, :, h * GW:(h + 1) * GW].astype(F32) for i, d, h in chains] for q in range(6)]
    s1, o = _scan_chunks([s_ref[i, d, h] for i, d, h in chains], *ins, [d == 1 for _, d, _ in chains], bdm)
    for (i, d, h), s_new, o_new in zip(chains, s1, o):
        (of_ref, ob_ref)[d][i, :, h * GW:(h + 1) * GW] = o_new
        s_ref[i, d, h] = s_new


SCAN_BATCH_ROWS = 4


def _wkv_scan(r, v, kk, kf, af, lwf, kb, ab, lwb):
    b, t, c = r.shape
    nc = t // CHUNK
    ng = c // GW
    nb = math.gcd(b, SCAN_BATCH_ROWS)
    fspec = pl.BlockSpec((nb, CHUNK, c), lambda i, j: (i, j, 0))
    bspec = pl.BlockSpec((nb, CHUNK, c), lambda i, j: (i, nc - 1 - j, 0))
    return pl.pallas_call(
        functools.partial(_scan_kernel, ng=ng, nb=nb),
        out_shape=(jax.ShapeDtypeStruct((b, t, c), F32), jax.ShapeDtypeStruct((b, t, c), F32)),
        grid=(b // nb, nc),
        in_specs=[fspec] * 6 + [bspec] * 6,
        out_specs=(fspec, bspec),
        scratch_shapes=[pltpu.VMEM((nb, 2, ng, HEAD, GW), F32)],
        compiler_params=_params(("parallel", "arbitrary")),
        name="rw_scan",
    )(r, kf, v, kk, af, lwf, r, kb, v, kk, ab, lwb)


def _out_proj_kernel(yh_ref, sf_ref, sb_ref, bon_ref, g_ref, lg_ref, lb_ref, avg_ref, wh_ref, wr_ref, r_ref,
                     o_ref):
    s = sf_ref[...] + sb_ref[...]
    mean = _dot3(s, avg_ref[...])
    d = s - mean
    var = _dot((d * d).astype(BF16), avg_ref[...])
    y = d * lax.rsqrt(var + GN_EPS) * lg_ref[...] + lb_ref[...] + bon_ref[...].astype(F32)
    yr = (y * g_ref[...].astype(F32)).astype(BF16)
    o_ref[...] = _dot(yh_ref[...].astype(BF16), wh_ref[...]) + _dot(yr, wr_ref[...]) + r_ref[...]


def _out_proj(yh, sf, sb, bon, g, lnx_g, lnx_b, w, res, tm=512):
    m, ch = yh.shape
    cr = sf.shape[1]
    d = w.shape[1]
    tm = min(tm, m)
    head = jnp.arange(cr) // HEAD
    avg = ((head[:, None] == head[None, :]).astype(F32) / HEAD).astype(BF16)
    row = lambda c: pl.BlockSpec((tm, c), lambda i: (i, 0))
    return pl.pallas_call(
        _out_proj_kernel,
        out_shape=jax.ShapeDtypeStruct((m, d), F32),
        grid=(m // tm,),
        in_specs=[row(ch), row(cr), row(cr), row(cr), row(cr), _const_spec((1, cr)), _const_spec((1, cr)),
                  _const_spec((cr, cr)), _const_spec((ch, d)), _const_spec((cr, d)), row(d)],
        out_specs=row(d),
        compiler_params=_params(("parallel",)),
        name="out_proj",
    )(yh, sf, sb, bon, g, lnx_g.reshape(1, cr), lnx_b.reshape(1, cr), avg, w[:ch], w[ch:], res)


def _rwkv_branch(p, v_first, shift, w0, w2, a0, a2, g2, k_k, k_a, r_k, vres, c):
    r, v, kk, lwf, lwb, kf, kb, af, ab, g, bon = _rw_prep(
        p, v_first, shift, w0, w2, a0, a2, g2, k_k, k_a, r_k, vres, c)
    sf, sb = _wkv_scan(r, v, kk, kf, af, lwf, kb, ab, lwb)
    return sf, sb, bon, g, (v if vres is None else v_first)


def _forward(x, norm1_g, w_in, hy_conv_w, hy_conv_b, hy_f_w1, hy_f_b1, hy_f_w2, hy_f_b2, hy_f_w3, hy_f_b3,
             hy_f_freq, hy_f_wout, hy_skip, hy_norm_g, rw_shift, rw_w0, rw_w2, rw_a0, rw_a2, rw_g2, rw_k_k,
             rw_k_a, rw_r_k, rw_lnx_g, rw_lnx_b, rw_v0, rw_v1, rw_v2, w_out, norm2_g, mlp_w1, mlp_w2,
             final_g):
    b, t, d = x.shape
    depth = w_in.shape[0]
    c_hy = hy_norm_g.shape[1]
    c_rw = rw_lnx_g.shape[1]
    hy_proj = hy_conv_b.shape[1]
    assert b % 2 == 0 and t % CHUNK == 0 and CHUNK == HEAD
    tabs = _dft_tables(t)
    assert tabs["n2"] % SUB16 == 0 and tabs["n1"] % SUB16 == 0 and tabs["n1h"] % SUB == 0
    v_first = None
    for l in range(depth):
        w_l = w_in[l].astype(BF16)
        w_rw = w_l[:, hy_proj:]
        vres = None if l == 0 else (rw_v0[l - 1], rw_v1[l - 1], rw_v2[l - 1])
        if vres is not None:
            v1 = vres[1].astype(BF16)
            w_rw = jnp.concatenate([w_rw, jnp.pad(v1, ((0, 0), (0, -v1.shape[1] % 128)))], axis=1)
        p_hy, p_rw = _in_proj(x, norm1_g[l], w_l[:, :hy_proj], w_rw, tabs["n1h"], tabs["n2"])
        y_hy = _hyena_branch(p_hy, hy_conv_w[l], hy_conv_b[l], hy_f_w1[l], hy_f_b1[l], hy_f_w2[l], hy_f_b2[l],
                             hy_f_w3[l], hy_f_b3[l], hy_f_freq[l], hy_f_wout[l], hy_skip[l], hy_norm_g[l], tabs)
        sf, sb, bon, g, v_first = _rwkv_branch(p_rw, v_first, rw_shift[l], rw_w0[l], rw_w2[l], rw_a0[l], rw_a2[l],
                                               rw_g2[l], rw_k_k[l], rw_k_a[l], rw_r_k[l], vres, c_rw)
        flat = lambda a: a.reshape(b * t, a.shape[-1])
        x = _out_proj(flat(y_hy), flat(sf), flat(sb), flat(bon), flat(g), rw_lnx_g[l], rw_lnx_b[l],
                      w_out[l].astype(BF16), flat(x))
        x = _mlp(x, norm2_g[l], mlp_w1[l].astype(BF16), mlp_w2[l].astype(BF16),
                 final_g=final_g if l == depth - 1 else None).reshape(b, t, d)
    return x


def kernel(x, norm1_g, w_in, hy_conv_w, hy_conv_b, hy_f_w1, hy_f_b1, hy_f_w2, hy_f_b2, hy_f_w3, hy_f_b3,
           hy_f_freq, hy_f_wout, hy_skip, hy_norm_g, rw_shift, rw_w0, rw_w2, rw_a0, rw_a2, rw_g2, rw_k_k,
           rw_k_a, rw_r_k, rw_lnx_g, rw_lnx_b, rw_v0, rw_v1, rw_v2, w_out, norm2_g, mlp_w1, mlp_w2, final_g):
    return _forward(x, norm1_g, w_in, hy_conv_w, hy_conv_b, hy_f_w1, hy_f_b1, hy_f_w2, hy_f_b2, hy_f_w3,
                    hy_f_b3, hy_f_freq, hy_f_wout, hy_skip, hy_norm_g, rw_shift, rw_w0, rw_w2, rw_a0, rw_a2,
                    rw_g2, rw_k_k, rw_k_a, rw_r_k, rw_lnx_g, rw_lnx_b, rw_v0, rw_v1, rw_v2, w_out, norm2_g,
                    mlp_w1, mlp_w2, final_g)
```

```python
import functools
import math

import jax
import jax.numpy as jnp
from jax import lax
from jax.experimental import pallas as pl
from jax.experimental.pallas import tpu as pltpu

F32 = jnp.float32
BF16 = jnp.bfloat16
HP = lax.Precision.HIGHEST

HEAD = 64
HEADS_PER_GROUP = 4
GW = HEAD * HEADS_PER_GROUP
CHUNK = 64
INV_BASE = 8
RMS_EPS = 1e-5
GN_EPS = HEAD * 1e-5
HY_TARGET = 1e-2
HY_FAST_PCT = 0.3
HY_SLOW_PCT = 1.5
VMEM_LIMIT = 56 * 1024 * 1024


def _params(sem, vmem=VMEM_LIMIT):
    return pltpu.CompilerParams(dimension_semantics=sem, vmem_limit_bytes=vmem)


def _dot(a, b, precision=None):
    return jnp.dot(a, b, preferred_element_type=F32, precision=precision)


def _dot_tn(a, b, precision=None):
    return lax.dot_general(a, b, (((0,), (0,)), ((), ())), preferred_element_type=F32,
                           precision=precision)


def _const_spec(shape):
    return pl.BlockSpec(shape, lambda *_: (0,) * len(shape), pipeline_mode=pl.Buffered(1))


def _split3(x):
    h1 = x.astype(BF16)
    r1 = x - h1.astype(F32)
    h2 = r1.astype(BF16)
    h3 = (r1 - h2.astype(F32)).astype(BF16)
    return h1, h2, h3


def _dot3(x, m):
    h1, h2, h3 = _split3(x)
    return _dot(h1, m) + _dot(h2, m) + _dot(h3, m)


def _mlp_kernel(x_ref, g_ref, w1_ref, w2_ref, *rest, nchunk):
    x = x_ref[...]
    hn = (x * lax.rsqrt(jnp.mean(x * x, axis=-1, keepdims=True) + RMS_EPS) * g_ref[...]).astype(BF16)
    ff = w1_ref.shape[1]
    cw = ff // nchunk
    acc = x
    for j in range(nchunk):
        h = _dot(hn, w1_ref[:, j * cw:(j + 1) * cw])
        h = jnp.square(jnp.maximum(h, 0.0)).astype(BF16)
        acc = acc + _dot(h, w2_ref[j * cw:(j + 1) * cw, :])
    if len(rest) == 2:
        gf_ref, o_ref = rest
        acc = acc * lax.rsqrt(jnp.mean(acc * acc, axis=-1, keepdims=True) + RMS_EPS) * gf_ref[...]
    else:
        (o_ref,) = rest
    o_ref[...] = acc


def _mlp(x2d, g, w1, w2, final_g=None, tm=512, nchunk=4):
    m, d = x2d.shape
    ff = w1.shape[1]
    tm = min(tm, m)
    in_specs = [pl.BlockSpec((tm, d), lambda i: (i, 0)), _const_spec((1, d)),
                _const_spec((d, ff)), _const_spec((ff, d))]
    args = [x2d, g.reshape(1, d), w1, w2]
    if final_g is not None:
        in_specs.append(_const_spec((1, d)))
        args.append(final_g.reshape(1, d))
    return pl.pallas_call(
        functools.partial(_mlp_kernel, nchunk=nchunk),
        out_shape=jax.ShapeDtypeStruct((m, d), F32),
        grid=(m // tm,),
        in_specs=in_specs,
        out_specs=pl.BlockSpec((tm, d), lambda i: (i, 0)),
        compiler_params=_params(("parallel",)),
        name="mlp",
    )(*args)


SUB = 8
SUB16 = 16
DFT_GROUP = 8


def _rms(x, g):
    return (x * lax.rsqrt(jnp.mean(x * x, axis=-1, keepdims=True) + RMS_EPS) * g).astype(BF16)


def _in_proj_kernel(x_ref, g_ref, wh_ref, wr_ref, oh_ref, or_ref, *, n2):
    hn = _rms(x_ref[0], g_ref[...])
    or_ref[0] = _dot(hn, wr_ref[...]).astype(or_ref.dtype)
    ph = _dot(hn, wh_ref[...])
    for j in range(SUB):
        oh_ref[0, :, j, :] = ph[j * n2:(j + 1) * n2]


def _in_proj(x, g, w_hy, w_rw, n1h, n2):
    b, t, d = x.shape
    ph, pr = w_hy.shape[1], w_rw.shape[1]
    tm = SUB * n2
    p_hy, p_rw = pl.pallas_call(
        functools.partial(_in_proj_kernel, n2=n2),
        out_shape=(jax.ShapeDtypeStruct((b, n2, n1h, ph), F32), jax.ShapeDtypeStruct((b, t, pr), BF16)),
        grid=(b, t // tm),
        in_specs=[pl.BlockSpec((1, tm, d), lambda i, j: (i, j, 0)), _const_spec((1, d)),
                  _const_spec((d, ph)), _const_spec((d, pr))],
        out_specs=(pl.BlockSpec((1, n2, SUB, ph), lambda i, j: (i, 0, j, 0)),
                   pl.BlockSpec((1, tm, pr), lambda i, j: (i, j, 0))),
        compiler_params=_params(("parallel", "parallel")),
        name="in_proj",
    )(x, g.reshape(1, d), w_hy, w_rw)
    return p_hy.reshape(b, t, ph), p_rw


def _shortconv_kernel(p_ref, w_ref, b_ref, o_ref, *, n1h, n2):
    w0 = w_ref[0:1, :]
    w1 = w_ref[1:2, :]
    w2 = w_ref[2:3, :]
    bias = b_ref[...]
    rows = lax.broadcasted_iota(jnp.int32, (n1h, 1), 0)

    def blk(i):
        return p_ref[0, pl.ds(pl.multiple_of(i * n1h, n1h), n1h), :].astype(F32)

    def body(i, carry):
        o_ref[0, pl.ds(pl.multiple_of(i * n1h, n1h), n1h), :] = (
            w0 * blk(i - 1) + w1 * blk(i) + w2 * blk(i + 1) + bias).astype(o_ref.dtype)
        return carry

    lax.fori_loop(1, n2 - 1, body, 0)
    last = p_ref[0, (n2 - 1) * n1h:, :].astype(F32)
    first = p_ref[0, :n1h, :].astype(F32)
    prev0 = jnp.where(rows == 0, 0.0, pltpu.roll(last, 1, 0))
    o_ref[0, :n1h, :] = (w0 * prev0 + w1 * first + w2 * p_ref[0, n1h:2 * n1h, :].astype(F32)
                         + bias).astype(o_ref.dtype)
    nxt = jnp.where(rows == n1h - 1, 0.0, pltpu.roll(first, n1h - 1, 0))
    o_ref[0, (n2 - 1) * n1h:, :] = (w0 * p_ref[0, (n2 - 2) * n1h:(n2 - 1) * n1h, :].astype(F32) + w1 * last
                                    + w2 * nxt + bias).astype(o_ref.dtype)


def _shortconv(p, w, bias, n1h, n2, cb=128):
    b, t, c = p.shape
    return pl.pallas_call(
        functools.partial(_shortconv_kernel, n1h=n1h, n2=n2),
        out_shape=jax.ShapeDtypeStruct((b, t, c), BF16),
        grid=(b, c // cb),
        in_specs=[pl.BlockSpec((1, t, cb), lambda i, j: (i, 0, j)),
                  pl.BlockSpec((3, cb), lambda i, j: (0, j)),
                  pl.BlockSpec((1, cb), lambda i, j: (0, j))],
        out_specs=pl.BlockSpec((1, t, cb), lambda i, j: (i, 0, j)),
        compiler_params=_params(("parallel", "parallel")),
        name="hy_shortconv",
    )(p, w, bias.reshape(1, c))


def _dft_tables(t):
    n = 2 * t
    n1 = 1 << ((n.bit_length() - 1) // 2)
    n2 = n // n1
    n1h = n1 // 2
    two_pi = 2.0 * math.pi
    k1 = jnp.arange(n1, dtype=jnp.int32)
    n2i = jnp.arange(n2, dtype=jnp.int32)

    def cs(prod, mod):
        ang = (two_pi / mod) * (prod % mod).astype(F32)
        return jnp.cos(ang), jnp.sin(ang)

    tpos = n2i[:, None] + n2 * jnp.arange(n1h, dtype=jnp.int32)[None, :]
    c, s = cs(k1[None, :, None] * tpos[:, None, :], n)
    g1 = jnp.concatenate([jnp.concatenate([c, s], axis=2), jnp.concatenate([-s, c], axis=2)], axis=1)
    ct, st = jnp.swapaxes(c, 1, 2) / n, jnp.swapaxes(s, 1, 2) / n
    g4a = jnp.concatenate([ct, st], axis=1)
    g4b = jnp.concatenate([-st, ct], axis=1)
    tposf = n2i[:, None] + n2 * jnp.arange(n1, dtype=jnp.int32)[None, :]
    cf, sf = cs(k1[None, :, None] * tposf[:, None, :], n)
    g1f = jnp.concatenate([cf, -sf], axis=1)
    c2, s2 = cs(n2i[:, None] * n2i[None, :], n2)
    f2a = jnp.concatenate([c2, -s2], axis=0)
    f2b = jnp.concatenate([s2, c2], axis=0)
    f3 = jnp.concatenate([jnp.concatenate([c2, -s2], axis=1), jnp.concatenate([s2, c2], axis=1)], axis=0)
    g4 = jnp.concatenate([g4a, g4b], axis=2)
    f2 = jnp.concatenate([f2a, f2b], axis=1)
    return dict(n1=n1, n2=n2, n1h=n1h, g1=g1.astype(BF16), g4=g4.astype(BF16), g1f=g1f.astype(BF16),
                f2=f2.astype(BF16), f3=f3.astype(BF16))


def _filter_mlp_kernel(z_ref, w1_ref, b1_ref, w2_ref, b2_ref, w3_ref, b3_ref, fr_ref, wo_ref, dl_ref,
                       o_ref, *, t):
    z = z_ref[...]
    h = jnp.sin(fr_ref[0:1, :] * (_dot(z, w1_ref[...], HP) + b1_ref[...]))
    h = jnp.sin(fr_ref[1:2, :] * (_dot(h, w2_ref[...], HP) + b2_ref[...]))
    h = jnp.sin(fr_ref[2:3, :] * (_dot(h, w3_ref[...], HP) + b3_ref[...]))
    window = jnp.exp(-z[:, 0:1] * dl_ref[...])
    tr = z.shape[0]
    row = pl.program_id(0) * tr + lax.broadcasted_iota(jnp.int32, (tr, 1), 0)
    o_ref[...] = jnp.where(row == t, 0.0, _dot(h, wo_ref[0], HP) * window)


def _filter_taps(t, w1, b1, w2, b2, w3, b3, freq, w_out, c_hy, tr=512):
    emb, width = w1.shape
    bands = (emb - 1) // 2
    pos = jnp.arange(t, dtype=F32)
    tt = pos / max(t - 1, 1)
    fr = jnp.linspace(1e-4, bands - 1, bands, dtype=F32)
    ang = (2.0 * math.pi / t) * pos[:, None] * fr[None, :]
    z = jnp.concatenate([tt[:, None], jnp.cos(ang), -jnp.sin(ang)], axis=-1)
    z = jnp.pad(z, ((0, 0), (0, 128 - emb)))
    w1 = jnp.pad(w1, ((0, 128 - emb), (0, 0)))
    emb = 128
    z2 = jnp.concatenate([z, z[:1], z[1:][::-1]], axis=0)
    max_decay = math.log(HY_TARGET) / HY_FAST_PCT
    min_decay = math.log(HY_TARGET) / HY_SLOW_PCT
    delta = jnp.abs(jnp.linspace(min_decay, max_decay, c_hy, dtype=F32))
    order = w_out.shape[1] // (2 * c_hy)
    nout = order * c_hy
    w_dir = w_out.reshape(width, order, 2, c_hy).transpose(2, 0, 1, 3).reshape(2, width, nout)
    delta_full = jnp.tile(delta, order).reshape(1, nout)
    tr = min(tr, t)
    nt = t // tr
    full = lambda a: pl.BlockSpec(a.shape, lambda i: (0,) * a.ndim)
    args = [z2, w1, b1.reshape(1, -1), w2, b2.reshape(1, -1), w3, b3.reshape(1, -1), freq, w_dir, delta_full]
    in_specs = [pl.BlockSpec((tr, emb), lambda i: (i, 0))] + [full(a) for a in args[1:]]
    in_specs[8] = pl.BlockSpec((1, width, nout), lambda i: (i // nt, 0, 0))
    return pl.pallas_call(
        functools.partial(_filter_mlp_kernel, t=t),
        out_shape=jax.ShapeDtypeStruct((2 * t, nout), F32),
        grid=(2 * nt,),
        in_specs=in_specs,
        out_specs=pl.BlockSpec((tr, nout), lambda i: (i, 0)),
        compiler_params=_params(("parallel",)),
        name="hy_filter_mlp",
    )(*args)


def _filter_s1_kernel(x_ref, g_ref, o_ref, ss_ref, *, g, oc):
    @pl.when(pl.program_id(0) == 0)
    def _():
        ss_ref[...] = jnp.zeros_like(ss_ref)

    n1 = x_ref.shape[0]
    x = jnp.concatenate([x_ref[:, j, :] for j in range(g)], axis=0)
    ss_ref[...] += jnp.sum(x * x, axis=0, keepdims=True)
    xb = x.astype(BF16)
    for j in range(g):
        o_ref[j] = _dot(g_ref[j], xb[j * n1:(j + 1) * n1])


def _filter_s2_kernel(sr_ref, si_ref, f2_ref, ss_ref, o_ref, *, g, oc):
    n2 = sr_ref.shape[0]
    scale = lax.rsqrt(ss_ref[...] + 1e-6)
    s = jnp.concatenate([ref[:, j, :] for j in range(g) for ref in (sr_ref, si_ref)], axis=0).astype(BF16)
    for j in range(g):
        o_ref[j] = _dot(f2_ref[...], s[2 * j * n2:2 * (j + 1) * n2]) * scale


def _filter_spectrum(two, tabs):
    n, oc = two.shape
    n1, n2 = tabs["n1"], tabs["n2"]
    g = SUB
    s1, ss = pl.pallas_call(
        functools.partial(_filter_s1_kernel, g=g, oc=oc),
        out_shape=(jax.ShapeDtypeStruct((n2, 2 * n1, oc), F32), jax.ShapeDtypeStruct((1, oc), F32)),
        grid=(n2 // g,),
        in_specs=[pl.BlockSpec((n1, g, oc), lambda j: (0, j, 0)),
                  pl.BlockSpec((g, 2 * n1, n1), lambda j: (j, 0, 0))],
        out_specs=(pl.BlockSpec((g, 2 * n1, oc), lambda j: (j, 0, 0)),
                   pl.BlockSpec((1, oc), lambda j: (0, 0))),
        compiler_params=_params(("arbitrary",)),
        name="hy_filter_dft1",
    )(two.reshape(n1, n2, oc), tabs["g1f"])
    return pl.pallas_call(
        functools.partial(_filter_s2_kernel, g=g, oc=oc),
        out_shape=jax.ShapeDtypeStruct((n1, 2 * n2, oc), F32),
        grid=(n1 // g,),
        in_specs=[pl.BlockSpec((n2, g, oc), lambda k: (0, k, 0)),
                  pl.BlockSpec((n2, g, oc), lambda k: (0, n1 // g + k, 0)),
                  pl.BlockSpec((2 * n2, 2 * n2), lambda k: (0, 0)),
                  pl.BlockSpec((1, oc), lambda k: (0, 0))],
        out_specs=pl.BlockSpec((g, 2 * n2, oc), lambda k: (k, 0, 0)),
        compiler_params=_params(("parallel",)),
        name="hy_filter_dft2",
    )(s1, s1, tabs["f2"], ss)


def _conv_s1_kernel(u_ref, g_ref, o_ref, t_ref, *, g, n1h):
    for j0 in range(0, g, DFT_GROUP):
        js = range(j0, j0 + DFT_GROUP)
        rs = [_dot(g_ref[j], jnp.concatenate([u_ref[0, 0, j * n1h:(j + 1) * n1h, :],
                                               u_ref[0, 1, j * n1h:(j + 1) * n1h, :]], axis=0)) for j in js]
        for j, r in zip(js, rs):
            t_ref[:, j, :] = r
    o_ref[0] = t_ref[...].astype(o_ref.dtype)


def _conv_s23_kernel(sr_ref, si_ref, f2_ref, f3_ref, h_ref, o_ref, t_ref, *, g, n2):
    for j0 in range(0, g, DFT_GROUP):
        js = range(j0, j0 + DFT_GROUP)
        xs = [_dot(f2_ref[...], jnp.concatenate([sr_ref[0, j], si_ref[0, j]], axis=0)) for j in js]
        ys = []
        for j, x in zip(js, xs):
            xr, xi = x[:n2], x[n2:]
            hr, hi = h_ref[j, :n2, :], h_ref[j, n2:, :]
            ys.append(jnp.concatenate([xr * hr - xi * hi, xr * hi + xi * hr], axis=0).astype(BF16))
        rs = [_dot(f3_ref[...], y) for y in ys]
        for j, r in zip(js, rs):
            t_ref[:, j, :] = r
    o_ref[0] = t_ref[...].astype(o_ref.dtype)


def _conv_s4_kernel(rr_ref, ri_ref, g4_ref, u_ref, gate_ref, skip_ref, *rest, g, n1h, norm):
    if norm:
        ng_ref, avg_ref, o_ref = rest
    else:
        (o_ref,) = rest
    skip = skip_ref[...]
    ys = [_dot(g4_ref[j], jnp.concatenate([rr_ref[0, j], ri_ref[0, j]], axis=0)) for j in range(g)]
    zs = [gate_ref[0, q, j * n1h:(j + 1) * n1h, :].astype(F32)
          * (ys[j][q * n1h:(q + 1) * n1h] + u_ref[0, q, j * n1h:(j + 1) * n1h, :].astype(F32) * skip)
          for j in range(g) for q in range(2)]
    if norm:
        ms = [_dot((z * z).astype(BF16), avg_ref[...]) for z in zs]
        zs = [z * lax.rsqrt(m + RMS_EPS) * ng_ref[...] for z, m in zip(zs, ms)]
    for j in range(g):
        for q in range(2):
            if norm:
                o_ref[0, q, :, j, :] = zs[2 * j + q]
            else:
                o_ref[0, q, j * n1h:(j + 1) * n1h, :] = zs[2 * j + q].astype(o_ref.dtype)


def _long_conv_gate(u_arr, u_col, gate_arr, gate_col, h_spec, h_col, skip, tabs, c, norm_g=None):
    b, t, _ = u_arr.shape
    n1, n2, n1h = tabs["n1"], tabs["n2"], tabs["n1h"]
    g = SUB
    npair = b // 2
    u4 = u_arr.reshape(npair, 2, t, u_arr.shape[2])
    gate4 = gate_arr.reshape(npair, 2, t, gate_arr.shape[2])
    gt = SUB16
    s1 = pl.pallas_call(
        functools.partial(_conv_s1_kernel, g=gt, n1h=n1h),
        out_shape=jax.ShapeDtypeStruct((npair, 2 * n1, n2, c), BF16),
        grid=(n2 // gt, npair),
        in_specs=[pl.BlockSpec((1, 2, gt * n1h, c), lambda j, p: (p, 0, j, u_col)),
                  pl.BlockSpec((gt, 2 * n1, n1), lambda j, p: (j, 0, 0))],
        out_specs=pl.BlockSpec((1, 2 * n1, gt, c), lambda j, p: (p, 0, j, 0)),
        scratch_shapes=[pltpu.VMEM((2 * n1, gt, c), F32)],
        compiler_params=_params(("parallel", "parallel")),
        name="hy_conv_dft1",
    )(u4, tabs["g1"])
    r = pl.pallas_call(
        functools.partial(_conv_s23_kernel, g=gt, n2=n2),
        out_shape=jax.ShapeDtypeStruct((npair, 2 * n2, n1, c), BF16),
        grid=(n1 // gt, npair),
        in_specs=[pl.BlockSpec((1, gt, n2, c), lambda k, p: (p, k, 0, 0)),
                  pl.BlockSpec((1, gt, n2, c), lambda k, p: (p, n1 // gt + k, 0, 0)),
                  pl.BlockSpec((2 * n2, 2 * n2), lambda k, p: (0, 0)),
                  pl.BlockSpec((2 * n2, 2 * n2), lambda k, p: (0, 0)),
                  pl.BlockSpec((gt, 2 * n2, c), lambda k, p: (k, 0, h_col))],
        out_specs=pl.BlockSpec((1, 2 * n2, gt, c), lambda k, p: (p, 0, k, 0)),
        scratch_shapes=[pltpu.VMEM((2 * n2, gt, c), F32)],
        compiler_params=_params(("parallel", "parallel")),
        name="hy_conv_dft23",
    )(s1, s1, tabs["f2"], tabs["f3"], h_spec)
    norm = norm_g is not None
    in_specs = [pl.BlockSpec((1, g, n1, c), lambda p, j: (p, j, 0, 0)),
                pl.BlockSpec((1, g, n1, c), lambda p, j: (p, n2 // g + j, 0, 0)),
                pl.BlockSpec((g, n1, 2 * n1), lambda p, j: (j, 0, 0)),
                pl.BlockSpec((1, 2, g * n1h, c), lambda p, j: (p, 0, j, u_col)),
                pl.BlockSpec((1, 2, g * n1h, c), lambda p, j: (p, 0, j, gate_col)),
                pl.BlockSpec((1, c), lambda p, j: (0, 0))]
    args = [r, r, tabs["g4"], u4, gate4, skip.reshape(1, c)]
    if norm:
        ch = jnp.arange(c) // HEAD
        avg = ((ch[:, None] == ch[None, :]).astype(F32) / HEAD).astype(BF16)
        in_specs += [pl.BlockSpec((1, c), lambda p, j: (0, 0)), pl.BlockSpec((c, c), lambda p, j: (0, 0))]
        args += [norm_g.reshape(1, c), avg]
        out_shape = jax.ShapeDtypeStruct((npair, 2, n1h, n2, c), F32)
        out_spec = pl.BlockSpec((1, 2, n1h, g, c), lambda p, j: (p, 0, 0, j, 0))
    else:
        out_shape = jax.ShapeDtypeStruct((npair, 2, t, c), BF16)
        out_spec = pl.BlockSpec((1, 2, g * n1h, c), lambda p, j: (p, 0, j, 0))
    out = pl.pallas_call(
        functools.partial(_conv_s4_kernel, g=g, n1h=n1h, norm=norm),
        out_shape=out_shape,
        grid=(npair, n2 // g),
        in_specs=in_specs,
        out_specs=out_spec,
        compiler_params=_params(("parallel", "parallel")),
        name="hy_conv_dft4",
    )(*args)
    return out.reshape(b, t, c)


def _hyena_branch(p, conv_w, conv_b, fw1, fb1, fw2, fb2, fw3, fb3, ffreq, fwout, skip, norm_g, tabs):
    b, t, _ = p.shape
    c = norm_g.shape[0]
    n1, n2, n1h = tabs["n1"], tabs["n2"], tabs["n1h"]
    u = _shortconv(p, conv_w, conv_b, n1h, n2)
    two = _filter_taps(t, fw1, fb1, fw2, fb2, fw3, fb3, ffreq, fwout, c)
    hspec = _filter_spectrum(two, tabs)
    z = _long_conv_gate(u, 2, u, 0, hspec, 0, skip[0], tabs, c)
    return _long_conv_gate(z, 0, u, 1, hspec, 1, skip[1], tabs, c, norm_g=norm_g)


def _rw_prep_kernel(*refs, has_vres, c):
    if has_vres:
        (p_ref, pp_ref, pn_ref, mu_ref, w0_ref, w2_ref, a0_ref, a2_ref, g2_ref, kk_ref, ka_ref, rk_ref,
         sum_ref, vf_ref, v0_ref, v2_ref,
         r_o, v_o, kk_o, lwf_o, lwb_o, kf_o, kb_o, af_o, ab_o, g_o, bon_o) = refs
    else:
        (p_ref, pp_ref, pn_ref, mu_ref, w0_ref, w2_ref, a0_ref, a2_ref, g2_ref, kk_ref, ka_ref, rk_ref,
         sum_ref,
         r_o, v_o, kk_o, lwf_o, lwb_o, kf_o, kb_o, af_o, ab_o, g_o, bon_o) = refs
    i = pl.program_id(1)
    last = pl.num_programs(1) - 1
    nsh = mu_ref.shape[1]
    p = p_ref[0, :, :nsh].astype(F32)
    tt = p.shape[0]
    rows = lax.broadcasted_iota(jnp.int32, (tt, 1), 0)
    prev_row = jnp.where(i == 0, 0.0, pp_ref[0, SUB16 - 1:SUB16, :nsh].astype(F32))
    next_row = jnp.where(i == last, 0.0, pn_ref[0, 0:1, :nsh].astype(F32))
    prev = jnp.where(rows == 0, prev_row, pltpu.roll(p, 1, 0))
    nxt = jnp.where(rows == tt - 1, next_row, pltpu.roll(p, tt - 1, 0))
    mu0, mu1 = mu_ref[0:1, :], mu_ref[1:2, :]
    pf = p * (1.0 - mu0 - mu1) + mu0 * prev + mu1 * nxt
    r = pf[:, :c]
    k = pf[:, c:2 * c]
    v = pf[:, 2 * c:3 * c]
    lw = 3 * c
    nd = w2_ref.shape[0]
    na = a2_ref.shape[0]
    wd = pf[:, lw:lw + nd]
    ad = pf[:, lw + nd:lw + nd + na]
    gd = pf[:, lw + nd + na:]
    if has_vres:
        lora = _dot(p_ref[0, :, nsh:], v2_ref[...])
        v = v + (vf_ref[0].astype(F32) - v) * jax.nn.sigmoid(v0_ref[...] + lora)
    g = _dot(jax.nn.sigmoid(gd).astype(BF16), g2_ref[...])
    kk = k * kk_ref[...]
    kk = kk * lax.rsqrt(jnp.maximum(_dot((kk * kk).astype(BF16), sum_ref[...]), 1e-24))
    wl = w0_ref[...] + _dot(jnp.tanh(wd).astype(BF16), w2_ref[...])
    logw = -math.exp(-0.5) * jax.nn.sigmoid(wl)
    a = jax.nn.sigmoid(a0_ref[...] + _dot(ad.astype(BF16), a2_ref[...]))
    ka = ka_ref[...]
    rk = rk_ref[...]
    k_d = [k * (1.0 + (a[:, d * c:(d + 1) * c] - 1.0) * ka) for d in range(2)]
    bon = _dot((r * (k_d[0] + k_d[1]) * rk).astype(BF16), sum_ref[...]) * v
    r_o[0] = r.astype(BF16)
    v_o[0] = v.astype(BF16)
    kk_o[0] = kk.astype(BF16)
    lwf_o[0] = logw[:, :c]
    lwb_o[0] = logw[:, c:]
    kf_o[0] = k_d[0].astype(BF16)
    kb_o[0] = k_d[1].astype(BF16)
    af_o[0] = a[:, :c].astype(BF16)
    ab_o[0] = a[:, c:].astype(BF16)
    g_o[0] = g.astype(BF16)
    bon_o[0] = bon.astype(BF16)


def _blockdiag2(m):
    k, c = m.shape[1], m.shape[2]
    z = jnp.zeros((k, c), m.dtype)
    return jnp.concatenate([jnp.concatenate([m[0], z], axis=1), jnp.concatenate([z, m[1]], axis=1)], axis=0)


def _rw_prep(p, v_first, shift, w0, w2, a0, a2, g2, k_k, k_a, r_k, vres, c, tt=256):
    b, t, pw = p.shape
    tt = min(tt, t)
    has_vres = vres is not None
    ch = jnp.arange(c) // HEAD
    summ = (ch[:, None] == ch[None, :]).astype(BF16)
    row = lambda a: a.reshape(1, -1)
    args = [p, p, p, shift, row(w0), _blockdiag2(w2).astype(BF16), row(a0), _blockdiag2(a2).astype(BF16),
            g2.astype(BF16), row(k_k), row(k_a), row(r_k), summ]
    full = lambda a: pl.BlockSpec(a.shape, lambda i, j: (0,) * a.ndim)
    nhb = t // SUB16
    in_specs = [pl.BlockSpec((1, tt, pw), lambda i, j: (i, j, 0)),
                pl.BlockSpec((1, SUB16, pw), lambda i, j: (i, jnp.maximum(j * (tt // SUB16) - 1, 0), 0)),
                pl.BlockSpec((1, SUB16, pw), lambda i, j: (i, jnp.minimum((j + 1) * (tt // SUB16), nhb - 1), 0))]
    in_specs += [full(a) for a in args[3:]]
    if has_vres:
        v0, _, v2 = vres
        v2p = jnp.pad(v2, ((0, pw - shift.shape[1] - v2.shape[0]), (0, 0))).astype(BF16)
        extra = [v_first, row(v0), v2p]
        in_specs += [pl.BlockSpec((1, tt, c), lambda i, j: (i, j, 0))] + [full(a) for a in extra[1:]]
        args += extra
    outs = tuple(jax.ShapeDtypeStruct((b, t, c), F32 if i in (3, 4) else BF16) for i in range(11))
    return pl.pallas_call(
        functools.partial(_rw_prep_kernel, has_vres=has_vres, c=c),
        out_shape=outs,
        grid=(b, t // tt),
        in_specs=in_specs,
        out_specs=tuple(pl.BlockSpec((1, tt, c), lambda i, j: (i, j, 0)) for _ in range(11)),
        compiler_params=_params(("parallel", "parallel")),
        name="rw_prep",
    )(*args)


def _scan_chunks(s0, r, k, v, kk, a, lw, rev, bdm):
    L = CHUNK
    n = len(r)
    each = lambda f, *ls: [f(*xs) for xs in zip(*ls)]
    ti = lax.broadcasted_iota(jnp.int32, (L, L), 0)
    si = lax.broadcasted_iota(jnp.int32, (L, L), 1)
    tt = lax.broadcasted_iota(jnp.int32, (L, GW), 0)
    ss = lax.broadcasted_iota(jnp.int32, (L, GW), 1) % L
    eye = (ss == tt).astype(F32)
    tri = [((si >= ti) if q else (si <= ti)).astype(F32).astype(BF16) for q in rev]
    strict = [(ss > tt) if q else (ss < tt) for q in rev]
    incl = [(ss >= tt) if q else (ss <= tt) for q in rev]
    bdf = bdm.astype(F32)

    def bd(x):
        xb = x.astype(BF16)
        return jnp.concatenate([xb] * HEADS_PER_GROUP, axis=0) * bdm

    def bd_t(x):
        xt = x.T.astype(BF16)
        return jnp.concatenate([xt] * HEADS_PER_GROUP, axis=1) * bdm

    def rcmul(x, y):
        return _dot(x.astype(BF16), bd(y))

    parts = each(_split3, lw)
    cum = [_dot(t, p[0]) + _dot(t, p[1]) + _dot(t, p[2]) for t, p in zip(tri, parts)]
    tot = [c[0:1, :] if q else c[L - 1:L, :] for c, q in zip(cum, rev)]
    p_in = each(jnp.exp, cum)
    p_inv = each(lambda c: jnp.exp(-c), cum)
    p_ex = each(lambda c, w: jnp.exp(c - w), cum, lw)
    p_rem = each(lambda t, c: jnp.exp(t - c), tot, cum)
    beta = each(lambda x, y: x * y, kk, a)
    a_t = each(lambda x, p: -x * p, kk, p_ex)
    r_t = each(lambda x, p: x * p, r, p_in)
    k_t = each(lambda x, p: x * p, k, p_inv)
    b_t = each(lambda x, p: x * p, beta, p_inv)
    k_h = each(lambda x, p: x * p, k, p_rem)
    b_h = each(lambda x, p: x * p, beta, p_rem)

    lhs = each(lambda x, y: jnp.concatenate([x, y], axis=0).astype(BF16), a_t, r_t)
    akk = each(lambda x, y: _dot(x, bd_t(y)), lhs, k_t)
    abb = each(lambda x, y: _dot(x, bd_t(y)), lhs, b_t)
    a_ak = each(lambda m, x: jnp.where(m, x[:L], 0.0), strict, akk)
    a_rk = each(lambda m, x: jnp.where(m, x[L:], 0.0), incl, akk)
    a_ab = each(lambda m, x: jnp.where(m, x[:L], 0.0), strict, abb)
    a_rb = each(lambda m, x: jnp.where(m, x[L:], 0.0), incl, abb)

    same = lambda size: (tt // size) == (ss // size)
    d = each(lambda x: jnp.where(same(INV_BASE), x, 0.0), a_ab)
    tinv = each(lambda x: eye + x, d)
    pw = each(rcmul, d, d)
    size = 4
    while size < INV_BASE:
        both = each(lambda p, t: rcmul(jnp.concatenate([p, t], axis=0), p), pw, tinv)
        pw = [x[:L] for x in both]
        tinv = each(lambda t, x: t + x[L:], tinv, both)
        size *= 2
    tinv = each(lambda t, p: t + rcmul(t, p), tinv, pw)
    size = INV_BASE
    while size < L:
        size *= 2
        couple = same(size) & ~same(size // 2)
        e = each(lambda x: jnp.where(couple, x, 0.0), a_ab)
        tinv = each(lambda t, y: t + rcmul(rcmul(t, y), t), tinv, e)

    av = each(lambda x, y, z: rcmul(jnp.concatenate([x, y], axis=0), z), a_ak, a_rk, v)
    ct = each(rcmul, a_rb, tinv)
    wu2 = each(lambda t, c, x, y: _dot(jnp.concatenate([t, c], axis=0).astype(BF16),
                                       jnp.concatenate([bd(x), bd(y[:L])], axis=1)), tinv, ct, a_t, av)
    wu = [x[:L] for x in wu2]
    rbwu = [x[L:] for x in wu2]
    o = each(lambda x, w, s, y: _dot((x + w[:, :GW]).astype(BF16), bd_t(s)) + y[L:] + w[:, GW:],
             r_t, rbwu, s0, av)

    m_bd = each(lambda w, x: _dot_tn(w[:, :GW].astype(BF16), x.astype(BF16)) * bdf, wu, b_h)
    n_full = each(lambda x, w, y, z: _dot_tn(jnp.concatenate([x, w[:, GW:]], axis=0).astype(BF16),
                                             jnp.concatenate([y, z], axis=0).astype(BF16)) * bdf,
                  v, wu, k_h, b_h)
    s1 = []
    for i in range(n):
        n_rc = n_full[i][0:HEAD]
        for h in range(1, HEADS_PER_GROUP):
            n_rc = n_rc + n_full[i][h * HEAD:(h + 1) * HEAD]
        s1.append(_dot(s0[i].astype(BF16), m_bd[i].astype(BF16)) + s0[i] * jnp.exp(tot[i]) + n_rc)
    return s1, o


def _scan_kernel(rf_ref, kf_ref, vf_ref, kkf_ref, af_ref, lwf_ref,
                 rb_ref, kb_ref, vb_ref, kkb_ref, ab_ref, lwb_ref, of_ref, ob_ref, s_ref, *, ng, nb):
    @pl.when(pl.program_id(1) == 0)
    def _():
        s_ref[...] = jnp.zeros_like(s_ref)

    gi = lax.broadcasted_iota(jnp.int32, (GW, GW), 0) // HEAD
    gj = lax.broadcasted_iota(jnp.int32, (GW, GW), 1) // HEAD
    bdm = (gi == gj).astype(F32).astype(BF16)
    dirs = ((rf_ref, kf_ref, vf_ref, kkf_ref, af_ref, lwf_ref), (rb_ref, kb_ref, vb_ref, kkb_ref, ab_ref, lwb_ref))
    chains = [(i, d, h) for i in range(nb) for d in range(2) for h in range(ng)]
    ins = [[dirs[d][q][i, :, h * GW:(h + 1) * GW].astype(F32) for i, d, h in chains] for q in range(6)]
    s1, o = _scan_chunks([s_ref[i, d, h] for i, d, h in chains], *ins, [d == 1 for _, d, _ in chains], bdm)
    for (i, d, h), s_new, o_new in zip(chains, s1, o):
        (of_ref, ob_ref)[d][i, :, h * GW:(h + 1) * GW] = o_new
        s_ref[i, d, h] = s_new


SCAN_BATCH_ROWS = 4


def _wkv_scan(r, v, kk, kf, af, lwf, kb, ab, lwb):
    b, t, c = r.shape
    nc = t // CHUNK
    ng = c // GW
    nb = math.gcd(b, SCAN_BATCH_ROWS)
    fspec = pl.BlockSpec((nb, CHUNK, c), lambda i, j: (i, j, 0))
    bspec = pl.BlockSpec((nb, CHUNK, c), lambda i, j: (i, nc - 1 - j, 0))
    return pl.pallas_call(
        functools.partial(_scan_kernel, ng=ng, nb=nb),
        out_shape=(jax.ShapeDtypeStruct((b, t, c), F32), jax.ShapeDtypeStruct((b, t, c), F32)),
        grid=(b // nb, nc),
        in_specs=[fspec] * 6 + [bspec] * 6,
        out_specs=(fspec, bspec),
        scratch_shapes=[pltpu.VMEM((nb, 2, ng, HEAD, GW), F32)],
        compiler_params=_params(("parallel", "arbitrary")),
        name="rw_scan",
    )(r, kf, v, kk, af, lwf, r, kb, v, kk, ab, lwb)


def _out_proj_kernel(yh_ref, sf_ref, sb_ref, bon_ref, g_ref, lg_ref, lb_ref, avg_ref, wh_ref, wr_ref, r_ref,
                     o_ref):
    s = sf_ref[...] + sb_ref[...]
    mean = _dot3(s, avg_ref[...])
    d = s - mean
    var = _dot((d * d).astype(BF16), avg_ref[...])
    y = d * lax.rsqrt(var + GN_EPS) * lg_ref[...] + lb_ref[...] + bon_ref[...].astype(F32)
    yr = (y * g_ref[...].astype(F32)).astype(BF16)
    o_ref[...] = _dot(yh_ref[...].astype(BF16), wh_ref[...]) + _dot(yr, wr_ref[...]) + r_ref[...]


def _out_proj(yh, sf, sb, bon, g, lnx_g, lnx_b, w, res, tm=512):
    m, ch = yh.shape
    cr = sf.shape[1]
    d = w.shape[1]
    tm = min(tm, m)
    head = jnp.arange(cr) // HEAD
    avg = ((head[:, None] == head[None, :]).astype(F32) / HEAD).astype(BF16)
    row = lambda c: pl.BlockSpec((tm, c), lambda i: (i, 0))
    return pl.pallas_call(
        _out_proj_kernel,
        out_shape=jax.ShapeDtypeStruct((m, d), F32),
        grid=(m // tm,),
        in_specs=[row(ch), row(cr), row(cr), row(cr), row(cr), _const_spec((1, cr)), _const_spec((1, cr)),
                  _const_spec((cr, cr)), _const_spec((ch, d)), _const_spec((cr, d)), row(d)],
        out_specs=row(d),
        compiler_params=_params(("parallel",)),
        name="out_proj",
    )(yh, sf, sb, bon, g, lnx_g.reshape(1, cr), lnx_b.reshape(1, cr), avg, w[:ch], w[ch:], res)


def _rwkv_branch(p, v_first, shift, w0, w2, a0, a2, g2, k_k, k_a, r_k, vres, c):
    r, v, kk, lwf, lwb, kf, kb, af, ab, g, bon = _rw_prep(
        p, v_first, shift, w0, w2, a0, a2, g2, k_k, k_a, r_k, vres, c)
    sf, sb = _wkv_scan(r, v, kk, kf, af, lwf, kb, ab, lwb)
    return sf, sb, bon, g, (v if vres is None else v_first)


def _forward(x, norm1_g, w_in, hy_conv_w, hy_conv_b, hy_f_w1, hy_f_b1, hy_f_w2, hy_f_b2, hy_f_w3, hy_f_b3,
             hy_f_freq, hy_f_wout, hy_skip, hy_norm_g, rw_shift, rw_w0, rw_w2, rw_a0, rw_a2, rw_g2, rw_k_k,
             rw_k_a, rw_r_k, rw_lnx_g, rw_lnx_b, rw_v0, rw_v1, rw_v2, w_out, norm2_g, mlp_w1, mlp_w2,
             final_g):
    b, t, d = x.shape
    depth = w_in.shape[0]
    c_hy = hy_norm_g.shape[1]
    c_rw = rw_lnx_g.shape[1]
    hy_proj = hy_conv_b.shape[1]
    assert b % 2 == 0 and t % CHUNK == 0 and CHUNK == HEAD
    tabs = _dft_tables(t)
    assert tabs["n2"] % SUB16 == 0 and tabs["n1"] % SUB16 == 0 and tabs["n1h"] % SUB == 0
    v_first = None
    for l in range(depth):
        w_l = w_in[l].astype(BF16)
        w_rw = w_l[:, hy_proj:]
        vres = None if l == 0 else (rw_v0[l - 1], rw_v1[l - 1], rw_v2[l - 1])
        if vres is not None:
            v1 = vres[1].astype(BF16)
            w_rw = jnp.concatenate([w_rw, jnp.pad(v1, ((0, 0), (0, -v1.shape[1] % 128)))], axis=1)
        p_hy, p_rw = _in_proj(x, norm1_g[l], w_l[:, :hy_proj], w_rw, tabs["n1h"], tabs["n2"])
        y_hy = _hyena_branch(p_hy, hy_conv_w[l], hy_conv_b[l], hy_f_w1[l], hy_f_b1[l], hy_f_w2[l], hy_f_b2[l],
                             hy_f_w3[l], hy_f_b3[l], hy_f_freq[l], hy_f_wout[l], hy_skip[l], hy_norm_g[l], tabs)
        sf, sb, bon, g, v_first = _rwkv_branch(p_rw, v_first, rw_shift[l], rw_w0[l], rw_w2[l], rw_a0[l], rw_a2[l],
                                               rw_g2[l], rw_k_k[l], rw_k_a[l], rw_r_k[l], vres, c_rw)
        flat = lambda a: a.reshape(b * t, a.shape[-1])
        x = _out_proj(flat(y_hy), flat(sf), flat(sb), flat(bon), flat(g), rw_lnx_g[l], rw_lnx_b[l],
                      w_out[l].astype(BF16), flat(x))
        x = _mlp(x, norm2_g[l], mlp_w1[l].astype(BF16), mlp_w2[l].astype(BF16),
                 final_g=final_g if l == depth - 1 else None).reshape(b, t, d)
    return x


def kernel(x, norm1_g, w_in, hy_conv_w, hy_conv_b, hy_f_w1, hy_f_b1, hy_f_w2, hy_f_b2, hy_f_w3, hy_f_b3,
           hy_f_freq, hy_f_wout, hy_skip, hy_norm_g, rw_shift, rw_w0, rw_w2, rw_a0, rw_a2, rw_g2, rw_k_k,
           rw_k_a, rw_r_k, rw_lnx_g, rw_lnx_b, rw_v0, rw_v1, rw_v2, w_out, norm2_g, mlp_w1, mlp_w2, final_g):
    return _forward(x, norm1_g, w_in, hy_conv_w, hy_conv_b, hy_f_w1, hy_f_b1, hy_f_w2, hy_f_b2, hy_f_w3,
                    hy_f_b3, hy_f_freq, hy_f_wout, hy_skip, hy_norm_g, rw_shift, rw_w0, rw_w2, rw_a0, rw_a2,
                    rw_g2, rw_k_k, rw_k_a, rw_r_k, rw_lnx_g, rw_lnx_b, rw_v0, rw_v1, rw_v2, w_out, norm2_g,
                    mlp_w1, mlp_w2, final_g)
```

```python
import functools
import math

import jax
import jax.numpy as jnp
from jax import lax
from jax.experimental import pallas as pl
from jax.experimental.pallas import tpu as pltpu

F32 = jnp.float32
BF16 = jnp.bfloat16
HP = lax.Precision.HIGHEST

HEAD = 64
HEADS_PER_GROUP = 4
GW = HEAD * HEADS_PER_GROUP
CHUNK = 64
INV_BASE = 8
RMS_EPS = 1e-5
GN_EPS = HEAD * 1e-5
HY_TARGET = 1e-2
HY_FAST_PCT = 0.3
HY_SLOW_PCT = 1.5
VMEM_LIMIT = 56 * 1024 * 1024


def _params(sem, vmem=VMEM_LIMIT):
    return pltpu.CompilerParams(dimension_semantics=sem, vmem_limit_bytes=vmem)


def _dot(a, b, precision=None):
    return jnp.dot(a, b, preferred_element_type=F32, precision=precision)


def _dot_tn(a, b, precision=None):
    return lax.dot_general(a, b, (((0,), (0,)), ((), ())), preferred_element_type=F32,
                           precision=precision)


def _const_spec(shape):
    return pl.BlockSpec(shape, lambda *_: (0,) * len(shape), pipeline_mode=pl.Buffered(1))


def _split3(x):
    h1 = x.astype(BF16)
    r1 = x - h1.astype(F32)
    h2 = r1.astype(BF16)
    h3 = (r1 - h2.astype(F32)).astype(BF16)
    return h1, h2, h3


def _dot3(x, m):
    h1, h2, h3 = _split3(x)
    return _dot(h1, m) + _dot(h2, m) + _dot(h3, m)


def _mlp_kernel(x_ref, g_ref, w1_ref, w2_ref, *rest, nchunk):
    x = x_ref[...]
    hn = (x * lax.rsqrt(jnp.mean(x * x, axis=-1, keepdims=True) + RMS_EPS) * g_ref[...]).astype(BF16)
    ff = w1_ref.shape[1]
    cw = ff // nchunk
    acc = x
    for j in range(nchunk):
        h = _dot(hn, w1_ref[:, j * cw:(j + 1) * cw])
        h = jnp.square(jnp.maximum(h, 0.0)).astype(BF16)
        acc = acc + _dot(h, w2_ref[j * cw:(j + 1) * cw, :])
    if len(rest) == 2:
        gf_ref, o_ref = rest
        acc = acc * lax.rsqrt(jnp.mean(acc * acc, axis=-1, keepdims=True) + RMS_EPS) * gf_ref[...]
    else:
        (o_ref,) = rest
    o_ref[...] = acc


def _mlp(x2d, g, w1, w2, final_g=None, tm=512, nchunk=4):
    m, d = x2d.shape
    ff = w1.shape[1]
    tm = min(tm, m)
    in_specs = [pl.BlockSpec((tm, d), lambda i: (i, 0)), _const_spec((1, d)),
                _const_spec((d, ff)), _const_spec((ff, d))]
    args = [x2d, g.reshape(1, d), w1, w2]
    if final_g is not None:
        in_specs.append(_const_spec((1, d)))
        args.append(final_g.reshape(1, d))
    return pl.pallas_call(
        functools.partial(_mlp_kernel, nchunk=nchunk),
        out_shape=jax.ShapeDtypeStruct((m, d), F32),
        grid=(m // tm,),
        in_specs=in_specs,
        out_specs=pl.BlockSpec((tm, d), lambda i: (i, 0)),
        compiler_params=_params(("parallel",)),
        name="mlp",
    )(*args)


SUB = 8
SUB16 = 16
DFT_GROUP = 8


def _rms(x, g):
    return (x * lax.rsqrt(jnp.mean(x * x, axis=-1, keepdims=True) + RMS_EPS) * g).astype(BF16)


def _in_proj_kernel(x_ref, g_ref, wh_ref, wr_ref, oh_ref, or_ref, *, n2):
    hn = _rms(x_ref[0], g_ref[...])
    or_ref[0] = _dot(hn, wr_ref[...]).astype(or_ref.dtype)
    ph = _dot(hn, wh_ref[...])
    for j in range(SUB):
        oh_ref[0, :, j, :] = ph[j * n2:(j + 1) * n2]


def _in_proj(x, g, w_hy, w_rw, n1h, n2):
    b, t, d = x.shape
    ph, pr = w_hy.shape[1], w_rw.shape[1]
    tm = SUB * n2
    p_hy, p_rw = pl.pallas_call(
        functools.partial(_in_proj_kernel, n2=n2),
        out_shape=(jax.ShapeDtypeStruct((b, n2, n1h, ph), F32), jax.ShapeDtypeStruct((b, t, pr), BF16)),
        grid=(b, t // tm),
        in_specs=[pl.BlockSpec((1, tm, d), lambda i, j: (i, j, 0)), _const_spec((1, d)),
                  _const_spec((d, ph)), _const_spec((d, pr))],
        out_specs=(pl.BlockSpec((1, n2, SUB, ph), lambda i, j: (i, 0, j, 0)),
                   pl.BlockSpec((1, tm, pr), lambda i, j: (i, j, 0))),
        compiler_params=_params(("parallel", "parallel")),
        name="in_proj",
    )(x, g.reshape(1, d), w_hy, w_rw)
    return p_hy.reshape(b, t, ph), p_rw


def _shortconv_kernel(p_ref, w_ref, b_ref, o_ref, *, n1h, n2):
    w0 = w_ref[0:1, :]
    w1 = w_ref[1:2, :]
    w2 = w_ref[2:3, :]
    bias = b_ref[...]
    rows = lax.broadcasted_iota(jnp.int32, (n1h, 1), 0)

    def blk(i):
        return p_ref[0, pl.ds(pl.multiple_of(i * n1h, n1h), n1h), :].astype(F32)

    def body(i, carry):
        o_ref[0, pl.ds(pl.multiple_of(i * n1h, n1h), n1h), :] = (
            w0 * blk(i - 1) + w1 * blk(i) + w2 * blk(i + 1) + bias).astype(o_ref.dtype)
        return carry

    lax.fori_loop(1, n2 - 1, body, 0)
    last = p_ref[0, (n2 - 1) * n1h:, :].astype(F32)
    first = p_ref[0, :n1h, :].astype(F32)
    prev0 = jnp.where(rows == 0, 0.0, pltpu.roll(last, 1, 0))
    o_ref[0, :n1h, :] = (w0 * prev0 + w1 * first + w2 * p_ref[0, n1h:2 * n1h, :].astype(F32)
                         + bias).astype(o_ref.dtype)
    nxt = jnp.where(rows == n1h - 1, 0.0, pltpu.roll(first, n1h - 1, 0))
    o_ref[0, (n2 - 1) * n1h:, :] = (w0 * p_ref[0, (n2 - 2) * n1h:(n2 - 1) * n1h, :].astype(F32) + w1 * last
                                    + w2 * nxt + bias).astype(o_ref.dtype)


def _shortconv(p, w, bias, n1h, n2, cb=256):
    b, t, c = p.shape
    return pl.pallas_call(
        functools.partial(_shortconv_kernel, n1h=n1h, n2=n2),
        out_shape=jax.ShapeDtypeStruct((b, t, c), BF16),
        grid=(b, c // cb),
        in_specs=[pl.BlockSpec((1, t, cb), lambda i, j: (i, 0, j)),
                  pl.BlockSpec((3, cb), lambda i, j: (0, j)),
                  pl.BlockSpec((1, cb), lambda i, j: (0, j))],
        out_specs=pl.BlockSpec((1, t, cb), lambda i, j: (i, 0, j)),
        compiler_params=_params(("parallel", "parallel")),
        name="hy_shortconv",
    )(p, w, bias.reshape(1, c))


def _dft_tables(t):
    n = 2 * t
    n1 = 1 << ((n.bit_length() - 1) // 2)
    n2 = n // n1
    n1h = n1 // 2
    two_pi = 2.0 * math.pi
    k1 = jnp.arange(n1, dtype=jnp.int32)
    n2i = jnp.arange(n2, dtype=jnp.int32)

    def cs(prod, mod):
        ang = (two_pi / mod) * (prod % mod).astype(F32)
        return jnp.cos(ang), jnp.sin(ang)

    tpos = n2i[:, None] + n2 * jnp.arange(n1h, dtype=jnp.int32)[None, :]
    c, s = cs(k1[None, :, None] * tpos[:, None, :], n)
    g1 = jnp.concatenate([jnp.concatenate([c, s], axis=2), jnp.concatenate([-s, c], axis=2)], axis=1)
    ct, st = jnp.swapaxes(c, 1, 2) / n, jnp.swapaxes(s, 1, 2) / n
    g4a = jnp.concatenate([ct, st], axis=1)
    g4b = jnp.concatenate([-st, ct], axis=1)
    tposf = n2i[:, None] + n2 * jnp.arange(n1, dtype=jnp.int32)[None, :]
    cf, sf = cs(k1[None, :, None] * tposf[:, None, :], n)
    g1f = jnp.concatenate([cf, -sf], axis=1)
    c2, s2 = cs(n2i[:, None] * n2i[None, :], n2)
    f2a = jnp.concatenate([c2, -s2], axis=0)
    f2b = jnp.concatenate([s2, c2], axis=0)
    f3 = jnp.concatenate([jnp.concatenate([c2, -s2], axis=1), jnp.concatenate([s2, c2], axis=1)], axis=0)
    g4 = jnp.concatenate([g4a, g4b], axis=2)
    f2 = jnp.concatenate([f2a, f2b], axis=1)
    return dict(n1=n1, n2=n2, n1h=n1h, g1=g1.astype(BF16), g4=g4.astype(BF16), g1f=g1f.astype(BF16),
                f2=f2.astype(BF16), f3=f3.astype(BF16))


def _filter_mlp_kernel(z_ref, w1_ref, b1_ref, w2_ref, b2_ref, w3_ref, b3_ref, fr_ref, wo_ref, dl_ref,
                       o_ref, *, t):
    z = z_ref[...]
    h = jnp.sin(fr_ref[0:1, :] * (_dot(z, w1_ref[...], HP) + b1_ref[...]))
    h = jnp.sin(fr_ref[1:2, :] * (_dot(h, w2_ref[...], HP) + b2_ref[...]))
    h = jnp.sin(fr_ref[2:3, :] * (_dot(h, w3_ref[...], HP) + b3_ref[...]))
    window = jnp.exp(-z[:, 0:1] * dl_ref[...])
    tr = z.shape[0]
    row = pl.program_id(0) * tr + lax.broadcasted_iota(jnp.int32, (tr, 1), 0)
    o_ref[...] = jnp.where(row == t, 0.0, _dot(h, wo_ref[0], HP) * window)


def _filter_taps(t, w1, b1, w2, b2, w3, b3, freq, w_out, c_hy, tr=512):
    emb, width = w1.shape
    bands = (emb - 1) // 2
    pos = jnp.arange(t, dtype=F32)
    tt = pos / max(t - 1, 1)
    fr = jnp.linspace(1e-4, bands - 1, bands, dtype=F32)
    ang = (2.0 * math.pi / t) * pos[:, None] * fr[None, :]
    z = jnp.concatenate([tt[:, None], jnp.cos(ang), -jnp.sin(ang)], axis=-1)
    z = jnp.pad(z, ((0, 0), (0, 128 - emb)))
    w1 = jnp.pad(w1, ((0, 128 - emb), (0, 0)))
    emb = 128
    z2 = jnp.concatenate([z, z[:1], z[1:][::-1]], axis=0)
    max_decay = math.log(HY_TARGET) / HY_FAST_PCT
    min_decay = math.log(HY_TARGET) / HY_SLOW_PCT
    delta = jnp.abs(jnp.linspace(min_decay, max_decay, c_hy, dtype=F32))
    order = w_out.shape[1] // (2 * c_hy)
    nout = order * c_hy
    w_dir = w_out.reshape(width, order, 2, c_hy).transpose(2, 0, 1, 3).reshape(2, width, nout)
    delta_full = jnp.tile(delta, order).reshape(1, nout)
    tr = min(tr, t)
    nt = t // tr
    full = lambda a: pl.BlockSpec(a.shape, lambda i: (0,) * a.ndim)
    args = [z2, w1, b1.reshape(1, -1), w2, b2.reshape(1, -1), w3, b3.reshape(1, -1), freq, w_dir, delta_full]
    in_specs = [pl.BlockSpec((tr, emb), lambda i: (i, 0))] + [full(a) for a in args[1:]]
    in_specs[8] = pl.BlockSpec((1, width, nout), lambda i: (i // nt, 0, 0))
    return pl.pallas_call(
        functools.partial(_filter_mlp_kernel, t=t),
        out_shape=jax.ShapeDtypeStruct((2 * t, nout), F32),
        grid=(2 * nt,),
        in_specs=in_specs,
        out_specs=pl.BlockSpec((tr, nout), lambda i: (i, 0)),
        compiler_params=_params(("parallel",)),
        name="hy_filter_mlp",
    )(*args)


def _filter_s1_kernel(x_ref, g_ref, o_ref, ss_ref, *, g, oc):
    @pl.when(pl.program_id(0) == 0)
    def _():
        ss_ref[...] = jnp.zeros_like(ss_ref)

    n1 = x_ref.shape[0]
    x = jnp.concatenate([x_ref[:, j, :] for j in range(g)], axis=0)
    ss_ref[...] += jnp.sum(x * x, axis=0, keepdims=True)
    xb = x.astype(BF16)
    for j in range(g):
        o_ref[j] = _dot(g_ref[j], xb[j * n1:(j + 1) * n1])


def _filter_s2_kernel(sr_ref, si_ref, f2_ref, ss_ref, o_ref, *, g, oc):
    n2 = sr_ref.shape[0]
    scale = lax.rsqrt(ss_ref[...] + 1e-6)
    s = jnp.concatenate([ref[:, j, :] for j in range(g) for ref in (sr_ref, si_ref)], axis=0).astype(BF16)
    for j in range(g):
        o_ref[j] = _dot(f2_ref[...], s[2 * j * n2:2 * (j + 1) * n2]) * scale


def _filter_spectrum(two, tabs):
    n, oc = two.shape
    n1, n2 = tabs["n1"], tabs["n2"]
    g = SUB
    s1, ss = pl.pallas_call(
        functools.partial(_filter_s1_kernel, g=g, oc=oc),
        out_shape=(jax.ShapeDtypeStruct((n2, 2 * n1, oc), F32), jax.ShapeDtypeStruct((1, oc), F32)),
        grid=(n2 // g,),
        in_specs=[pl.BlockSpec((n1, g, oc), lambda j: (0, j, 0)),
                  pl.BlockSpec((g, 2 * n1, n1), lambda j: (j, 0, 0))],
        out_specs=(pl.BlockSpec((g, 2 * n1, oc), lambda j: (j, 0, 0)),
                   pl.BlockSpec((1, oc), lambda j: (0, 0))),
        compiler_params=_params(("arbitrary",)),
        name="hy_filter_dft1",
    )(two.reshape(n1, n2, oc), tabs["g1f"])
    return pl.pallas_call(
        functools.partial(_filter_s2_kernel, g=g, oc=oc),
        out_shape=jax.ShapeDtypeStruct((n1, 2 * n2, oc), F32),
        grid=(n1 // g,),
        in_specs=[pl.BlockSpec((n2, g, oc), lambda k: (0, k, 0)),
                  pl.BlockSpec((n2, g, oc), lambda k: (0, n1 // g + k, 0)),
                  pl.BlockSpec((2 * n2, 2 * n2), lambda k: (0, 0)),
                  pl.BlockSpec((1, oc), lambda k: (0, 0))],
        out_specs=pl.BlockSpec((g, 2 * n2, oc), lambda k: (k, 0, 0)),
        compiler_params=_params(("parallel",)),
        name="hy_filter_dft2",
    )(s1, s1, tabs["f2"], ss)


def _conv_s1_kernel(u_ref, g_ref, o_ref, t_ref, *, g, n1h):
    for j0 in range(0, g, DFT_GROUP):
        js = range(j0, j0 + DFT_GROUP)
        rs = [_dot(g_ref[j], jnp.concatenate([u_ref[0, 0, j * n1h:(j + 1) * n1h, :],
                                               u_ref[0, 1, j * n1h:(j + 1) * n1h, :]], axis=0)) for j in js]
        for j, r in zip(js, rs):
            t_ref[:, j, :] = r
    o_ref[0] = t_ref[...].astype(o_ref.dtype)


def _conv_s23_kernel(sr_ref, si_ref, f2_ref, f3_ref, h_ref, o_ref, t_ref, *, g, n2):
    for j0 in range(0, g, DFT_GROUP):
        js = range(j0, j0 + DFT_GROUP)
        xs = [_dot(f2_ref[...], jnp.concatenate([sr_ref[0, j], si_ref[0, j]], axis=0)) for j in js]
        ys = []
        for j, x in zip(js, xs):
            xr, xi = x[:n2], x[n2:]
            hr, hi = h_ref[j, :n2, :], h_ref[j, n2:, :]
            ys.append(jnp.concatenate([xr * hr - xi * hi, xr * hi + xi * hr], axis=0).astype(BF16))
        rs = [_dot(f3_ref[...], y) for y in ys]
        for j, r in zip(js, rs):
            t_ref[:, j, :] = r
    o_ref[0] = t_ref[...].astype(o_ref.dtype)


def _conv_s4_kernel(rr_ref, ri_ref, g4_ref, u_ref, gate_ref, skip_ref, *rest, g, n1h, norm):
    if norm:
        ng_ref, avg_ref, o_ref = rest
    else:
        (o_ref,) = rest
    skip = skip_ref[...]
    ys = [_dot(g4_ref[j], jnp.concatenate([rr_ref[0, j], ri_ref[0, j]], axis=0)) for j in range(g)]
    zs = [gate_ref[0, q, j * n1h:(j + 1) * n1h, :].astype(F32)
          * (ys[j][q * n1h:(q + 1) * n1h] + u_ref[0, q, j * n1h:(j + 1) * n1h, :].astype(F32) * skip)
          for j in range(g) for q in range(2)]
    if norm:
        ms = [_dot((z * z).astype(BF16), avg_ref[...]) for z in zs]
        zs = [z * lax.rsqrt(m + RMS_EPS) * ng_ref[...] for z, m in zip(zs, ms)]
    for j in range(g):
        for q in range(2):
            if norm:
                o_ref[0, q, :, j, :] = zs[2 * j + q]
            else:
                o_ref[0, q, j * n1h:(j + 1) * n1h, :] = zs[2 * j + q].astype(o_ref.dtype)


def _long_conv_gate(u_arr, u_col, gate_arr, gate_col, h_spec, h_col, skip, tabs, c, norm_g=None):
    b, t, _ = u_arr.shape
    n1, n2, n1h = tabs["n1"], tabs["n2"], tabs["n1h"]
    g = SUB
    npair = b // 2
    u4 = u_arr.reshape(npair, 2, t, u_arr.shape[2])
    gate4 = gate_arr.reshape(npair, 2, t, gate_arr.shape[2])
    gt = SUB16
    s1 = pl.pallas_call(
        functools.partial(_conv_s1_kernel, g=gt, n1h=n1h),
        out_shape=jax.ShapeDtypeStruct((npair, 2 * n1, n2, c), BF16),
        grid=(n2 // gt, npair),
        in_specs=[pl.BlockSpec((1, 2, gt * n1h, c), lambda j, p: (p, 0, j, u_col)),
                  pl.BlockSpec((gt, 2 * n1, n1), lambda j, p: (j, 0, 0))],
        out_specs=pl.BlockSpec((1, 2 * n1, gt, c), lambda j, p: (p, 0, j, 0)),
        scratch_shapes=[pltpu.VMEM((2 * n1, gt, c), F32)],
        compiler_params=_params(("parallel", "parallel")),
        name="hy_conv_dft1",
    )(u4, tabs["g1"])
    r = pl.pallas_call(
        functools.partial(_conv_s23_kernel, g=gt, n2=n2),
        out_shape=jax.ShapeDtypeStruct((npair, 2 * n2, n1, c), BF16),
        grid=(n1 // gt, npair),
        in_specs=[pl.BlockSpec((1, gt, n2, c), lambda k, p: (p, k, 0, 0)),
                  pl.BlockSpec((1, gt, n2, c), lambda k, p: (p, n1 // gt + k, 0, 0)),
                  pl.BlockSpec((2 * n2, 2 * n2), lambda k, p: (0, 0)),
                  pl.BlockSpec((2 * n2, 2 * n2), lambda k, p: (0, 0)),
                  pl.BlockSpec((gt, 2 * n2, c), lambda k, p: (k, 0, h_col))],
        out_specs=pl.BlockSpec((1, 2 * n2, gt, c), lambda k, p: (p, 0, k, 0)),
        scratch_shapes=[pltpu.VMEM((2 * n2, gt, c), F32)],
        compiler_params=_params(("parallel", "parallel")),
        name="hy_conv_dft23",
    )(s1, s1, tabs["f2"], tabs["f3"], h_spec)
    norm = norm_g is not None
    in_specs = [pl.BlockSpec((1, g, n1, c), lambda p, j: (p, j, 0, 0)),
                pl.BlockSpec((1, g, n1, c), lambda p, j: (p, n2 // g + j, 0, 0)),
                pl.BlockSpec((g, n1, 2 * n1), lambda p, j: (j, 0, 0)),
                pl.BlockSpec((1, 2, g * n1h, c), lambda p, j: (p, 0, j, u_col)),
                pl.BlockSpec((1, 2, g * n1h, c), lambda p, j: (p, 0, j, gate_col)),
                pl.BlockSpec((1, c), lambda p, j: (0, 0))]
    args = [r, r, tabs["g4"], u4, gate4, skip.reshape(1, c)]
    if norm:
        ch = jnp.arange(c) // HEAD
        avg = ((ch[:, None] == ch[None, :]).astype(F32) / HEAD).astype(BF16)
        in_specs += [pl.BlockSpec((1, c), lambda p, j: (0, 0)), pl.BlockSpec((c, c), lambda p, j: (0, 0))]
        args += [norm_g.reshape(1, c), avg]
        out_shape = jax.ShapeDtypeStruct((npair, 2, n1h, n2, c), F32)
        out_spec = pl.BlockSpec((1, 2, n1h, g, c), lambda p, j: (p, 0, 0, j, 0))
    else:
        out_shape = jax.ShapeDtypeStruct((npair, 2, t, c), BF16)
        out_spec = pl.BlockSpec((1, 2, g * n1h, c), lambda p, j: (p, 0, j, 0))
    out = pl.pallas_call(
        functools.partial(_conv_s4_kernel, g=g, n1h=n1h, norm=norm),
        out_shape=out_shape,
        grid=(npair, n2 // g),
        in_specs=in_specs,
        out_specs=out_spec,
        compiler_params=_params(("parallel", "parallel")),
        name="hy_conv_dft4",
    )(*args)
    return out.reshape(b, t, c)


def _hyena_branch(p, conv_w, conv_b, fw1, fb1, fw2, fb2, fw3, fb3, ffreq, fwout, skip, norm_g, tabs):
    b, t, _ = p.shape
    c = norm_g.shape[0]
    n1, n2, n1h = tabs["n1"], tabs["n2"], tabs["n1h"]
    u = _shortconv(p, conv_w, conv_b, n1h, n2)
    two = _filter_taps(t, fw1, fb1, fw2, fb2, fw3, fb3, ffreq, fwout, c)
    hspec = _filter_spectrum(two, tabs)
    z = _long_conv_gate(u, 2, u, 0, hspec, 0, skip[0], tabs, c)
    return _long_conv_gate(z, 0, u, 1, hspec, 1, skip[1], tabs, c, norm_g=norm_g)


def _rw_prep_kernel(*refs, has_vres, c):
    if has_vres:
        (p_ref, pp_ref, pn_ref, mu_ref, w0_ref, w2_ref, a0_ref, a2_ref, g2_ref, kk_ref, ka_ref, rk_ref,
         sum_ref, vf_ref, v0_ref, v2_ref,
         r_o, v_o, kk_o, lwf_o, lwb_o, kf_o, kb_o, af_o, ab_o, g_o, bon_o) = refs
    else:
        (p_ref, pp_ref, pn_ref, mu_ref, w0_ref, w2_ref, a0_ref, a2_ref, g2_ref, kk_ref, ka_ref, rk_ref,
         sum_ref,
         r_o, v_o, kk_o, lwf_o, lwb_o, kf_o, kb_o, af_o, ab_o, g_o, bon_o) = refs
    i = pl.program_id(1)
    last = pl.num_programs(1) - 1
    nsh = mu_ref.shape[1]
    p = p_ref[0, :, :nsh].astype(F32)
    tt = p.shape[0]
    rows = lax.broadcasted_iota(jnp.int32, (tt, 1), 0)
    prev_row = jnp.where(i == 0, 0.0, pp_ref[0, SUB16 - 1:SUB16, :nsh].astype(F32))
    next_row = jnp.where(i == last, 0.0, pn_ref[0, 0:1, :nsh].astype(F32))
    prev = jnp.where(rows == 0, prev_row, pltpu.roll(p, 1, 0))
    nxt = jnp.where(rows == tt - 1, next_row, pltpu.roll(p, tt - 1, 0))
    mu0, mu1 = mu_ref[0:1, :], mu_ref[1:2, :]
    pf = p * (1.0 - mu0 - mu1) + mu0 * prev + mu1 * nxt
    r = pf[:, :c]
    k = pf[:, c:2 * c]
    v = pf[:, 2 * c:3 * c]
    lw = 3 * c
    nd = w2_ref.shape[0]
    na = a2_ref.shape[0]
    wd = pf[:, lw:lw + nd]
    ad = pf[:, lw + nd:lw + nd + na]
    gd = pf[:, lw + nd + na:]
    if has_vres:
        lora = _dot(p_ref[0, :, nsh:], v2_ref[...])
        v = v + (vf_ref[0].astype(F32) - v) * jax.nn.sigmoid(v0_ref[...] + lora)
    g = _dot(jax.nn.sigmoid(gd).astype(BF16), g2_ref[...])
    kk = k * kk_ref[...]
    kk = kk * lax.rsqrt(jnp.maximum(_dot((kk * kk).astype(BF16), sum_ref[...]), 1e-24))
    wl = w0_ref[...] + _dot(jnp.tanh(wd).astype(BF16), w2_ref[...])
    logw = -math.exp(-0.5) * jax.nn.sigmoid(wl)
    a = jax.nn.sigmoid(a0_ref[...] + _dot(ad.astype(BF16), a2_ref[...]))
    ka = ka_ref[...]
    rk = rk_ref[...]
    k_d = [k * (1.0 + (a[:, d * c:(d + 1) * c] - 1.0) * ka) for d in range(2)]
    bon = _dot((r * (k_d[0] + k_d[1]) * rk).astype(BF16), sum_ref[...]) * v
    r_o[0] = r.astype(BF16)
    v_o[0] = v.astype(BF16)
    kk_o[0] = kk.astype(BF16)
    lwf_o[0] = logw[:, :c]
    lwb_o[0] = logw[:, c:]
    kf_o[0] = k_d[0].astype(BF16)
    kb_o[0] = k_d[1].astype(BF16)
    af_o[0] = a[:, :c].astype(BF16)
    ab_o[0] = a[:, c:].astype(BF16)
    g_o[0] = g.astype(BF16)
    bon_o[0] = bon.astype(BF16)


def _blockdiag2(m):
    k, c = m.shape[1], m.shape[2]
    z = jnp.zeros((k, c), m.dtype)
    return jnp.concatenate([jnp.concatenate([m[0], z], axis=1), jnp.concatenate([z, m[1]], axis=1)], axis=0)


def _rw_prep(p, v_first, shift, w0, w2, a0, a2, g2, k_k, k_a, r_k, vres, c, tt=256):
    b, t, pw = p.shape
    tt = min(tt, t)
    has_vres = vres is not None
    ch = jnp.arange(c) // HEAD
    summ = (ch[:, None] == ch[None, :]).astype(BF16)
    row = lambda a: a.reshape(1, -1)
    args = [p, p, p, shift, row(w0), _blockdiag2(w2).astype(BF16), row(a0), _blockdiag2(a2).astype(BF16),
            g2.astype(BF16), row(k_k), row(k_a), row(r_k), summ]
    full = lambda a: pl.BlockSpec(a.shape, lambda i, j: (0,) * a.ndim)
    nhb = t // SUB16
    in_specs = [pl.BlockSpec((1, tt, pw), lambda i, j: (i, j, 0)),
                pl.BlockSpec((1, SUB16, pw), lambda i, j: (i, jnp.maximum(j * (tt // SUB16) - 1, 0), 0)),
                pl.BlockSpec((1, SUB16, pw), lambda i, j: (i, jnp.minimum((j + 1) * (tt // SUB16), nhb - 1), 0))]
    in_specs += [full(a) for a in args[3:]]
    if has_vres:
        v0, _, v2 = vres
        v2p = jnp.pad(v2, ((0, pw - shift.shape[1] - v2.shape[0]), (0, 0))).astype(BF16)
        extra = [v_first, row(v0), v2p]
        in_specs += [pl.BlockSpec((1, tt, c), lambda i, j: (i, j, 0))] + [full(a) for a in extra[1:]]
        args += extra
    outs = tuple(jax.ShapeDtypeStruct((b, t, c), F32 if i in (3, 4) else BF16) for i in range(11))
    return pl.pallas_call(
        functools.partial(_rw_prep_kernel, has_vres=has_vres, c=c),
        out_shape=outs,
        grid=(b, t // tt),
        in_specs=in_specs,
        out_specs=tuple(pl.BlockSpec((1, tt, c), lambda i, j: (i, j, 0)) for _ in range(11)),
        compiler_params=_params(("parallel", "parallel")),
        name="rw_prep",
    )(*args)


def _scan_chunks(s0, r, k, v, kk, a, lw, rev, bdm):
    L = CHUNK
    n = len(r)
    each = lambda f, *ls: [f(*xs) for xs in zip(*ls)]
    ti = lax.broadcasted_iota(jnp.int32, (L, L), 0)
    si = lax.broadcasted_iota(jnp.int32, (L, L), 1)
    tt = lax.broadcasted_iota(jnp.int32, (L, GW), 0)
    ss = lax.broadcasted_iota(jnp.int32, (L, GW), 1) % L
    eye = (ss == tt).astype(F32)
    tri = [((si >= ti) if q else (si <= ti)).astype(F32).astype(BF16) for q in rev]
    strict = [(ss > tt) if q else (ss < tt) for q in rev]
    incl = [(ss >= tt) if q else (ss <= tt) for q in rev]
    bdf = bdm.astype(F32)

    def bd(x):
        xb = x.astype(BF16)
        return jnp.concatenate([xb] * HEADS_PER_GROUP, axis=0) * bdm

    def bd_t(x):
        xt = x.T.astype(BF16)
        return jnp.concatenate([xt] * HEADS_PER_GROUP, axis=1) * bdm

    def rcmul(x, y):
        return _dot(x.astype(BF16), bd(y))

    parts = each(_split3, lw)
    cum = [_dot(t, p[0]) + _dot(t, p[1]) + _dot(t, p[2]) for t, p in zip(tri, parts)]
    tot = [c[0:1, :] if q else c[L - 1:L, :] for c, q in zip(cum, rev)]
    p_in = each(jnp.exp, cum)
    p_inv = each(lambda c: jnp.exp(-c), cum)
    p_ex = each(lambda c, w: jnp.exp(c - w), cum, lw)
    p_rem = each(lambda t, c: jnp.exp(t - c), tot, cum)
    beta = each(lambda x, y: x * y, kk, a)
    a_t = each(lambda x, p: -x * p, kk, p_ex)
    r_t = each(lambda x, p: x * p, r, p_in)
    k_t = each(lambda x, p: x * p, k, p_inv)
    b_t = each(lambda x, p: x * p, beta, p_inv)
    k_h = each(lambda x, p: x * p, k, p_rem)
    b_h = each(lambda x, p: x * p, beta, p_rem)

    lhs = each(lambda x, y: jnp.concatenate([x, y], axis=0).astype(BF16), a_t, r_t)
    akk = each(lambda x, y: _dot(x, bd_t(y)), lhs, k_t)
    abb = each(lambda x, y: _dot(x, bd_t(y)), lhs, b_t)
    a_ak = each(lambda m, x: jnp.where(m, x[:L], 0.0), strict, akk)
    a_rk = each(lambda m, x: jnp.where(m, x[L:], 0.0), incl, akk)
    a_ab = each(lambda m, x: jnp.where(m, x[:L], 0.0), strict, abb)
    a_rb = each(lambda m, x: jnp.where(m, x[L:], 0.0), incl, abb)

    same = lambda size: (tt // size) == (ss // size)
    d = each(lambda x: jnp.where(same(INV_BASE), x, 0.0), a_ab)
    tinv = each(lambda x: eye + x, d)
    pw = each(rcmul, d, d)
    size = 4
    while size < INV_BASE:
        both = each(lambda p, t: rcmul(jnp.concatenate([p, t], axis=0), p), pw, tinv)
        pw = [x[:L] for x in both]
        tinv = each(lambda t, x: t + x[L:], tinv, both)
        size *= 2
    tinv = each(lambda t, p: t + rcmul(t, p), tinv, pw)
    size = INV_BASE
    while size < L:
        size *= 2
        couple = same(size) & ~same(size // 2)
        e = each(lambda x: jnp.where(couple, x, 0.0), a_ab)
        tinv = each(lambda t, y: t + rcmul(rcmul(t, y), t), tinv, e)

    av = each(lambda x, y, z: rcmul(jnp.concatenate([x, y], axis=0), z), a_ak, a_rk, v)
    ct = each(rcmul, a_rb, tinv)
    wu2 = each(lambda t, c, x, y: _dot(jnp.concatenate([t, c], axis=0).astype(BF16),
                                       jnp.concatenate([bd(x), bd(y[:L])], axis=1)), tinv, ct, a_t, av)
    wu = [x[:L] for x in wu2]
    rbwu = [x[L:] for x in wu2]
    o = each(lambda x, w, s, y: _dot((x + w[:, :GW]).astype(BF16), bd_t(s)) + y[L:] + w[:, GW:],
             r_t, rbwu, s0, av)

    m_bd = each(lambda w, x: _dot_tn(w[:, :GW].astype(BF16), x.astype(BF16)) * bdf, wu, b_h)
    n_full = each(lambda x, w, y, z: _dot_tn(jnp.concatenate([x, w[:, GW:]], axis=0).astype(BF16),
                                             jnp.concatenate([y, z], axis=0).astype(BF16)) * bdf,
                  v, wu, k_h, b_h)
    s1 = []
    for i in range(n):
        n_rc = n_full[i][0:HEAD]
        for h in range(1, HEADS_PER_GROUP):
            n_rc = n_rc + n_full[i][h * HEAD:(h + 1) * HEAD]
        s1.append(_dot(s0[i].astype(BF16), m_bd[i].astype(BF16)) + s0[i] * jnp.exp(tot[i]) + n_rc)
    return s1, o


def _scan_kernel(rf_ref, kf_ref, vf_ref, kkf_ref, af_ref, lwf_ref,
                 rb_ref, kb_ref, vb_ref, kkb_ref, ab_ref, lwb_ref, of_ref, ob_ref, s_ref, *, ng, nb):
    @pl.when(pl.program_id(1) == 0)
    def _():
        s_ref[...] = jnp.zeros_like(s_ref)

    gi = lax.broadcasted_iota(jnp.int32, (GW, GW), 0) // HEAD
    gj = lax.broadcasted_iota(jnp.int32, (GW, GW), 1) // HEAD
    bdm = (gi == gj).astype(F32).astype(BF16)
    dirs = ((rf_ref, kf_ref, vf_ref, kkf_ref, af_ref, lwf_ref), (rb_ref, kb_ref, vb_ref, kkb_ref, ab_ref, lwb_ref))
    chains = [(i, d, h) for i in range(nb) for d in range(2) for h in range(ng)]
    ins = [[dirs[d][q][i, :, h * GW:(h + 1) * GW].astype(F32) for i, d, h in chains] for q in range(6)]
    s1, o = _scan_chunks([s_ref[i, d, h] for i, d, h in chains], *ins, [d == 1 for _, d, _ in chains], bdm)
    for (i, d, h), s_new, o_new in zip(chains, s1, o):
        (of_ref, ob_ref)[d][i, :, h * GW:(h + 1) * GW] = o_new
        s_ref[i, d, h] = s_new


SCAN_BATCH_ROWS = 4


def _wkv_scan(r, v, kk, kf, af, lwf, kb, ab, lwb):
    b, t, c = r.shape
    nc = t // CHUNK
    ng = c // GW
    nb = math.gcd(b, SCAN_BATCH_ROWS)
    fspec = pl.BlockSpec((nb, CHUNK, c), lambda i, j: (i, j, 0))
    bspec = pl.BlockSpec((nb, CHUNK, c), lambda i, j: (i, nc - 1 - j, 0))
    return pl.pallas_call(
        functools.partial(_scan_kernel, ng=ng, nb=nb),
        out_shape=(jax.ShapeDtypeStruct((b, t, c), F32), jax.ShapeDtypeStruct((b, t, c), F32)),
        grid=(b // nb, nc),
        in_specs=[fspec] * 6 + [bspec] * 6,
        out_specs=(fspec, bspec),
        scratch_shapes=[pltpu.VMEM((nb, 2, ng, HEAD, GW), F32)],
        compiler_params=_params(("parallel", "arbitrary")),
        name="rw_scan",
    )(r, kf, v, kk, af, lwf, r, kb, v, kk, ab, lwb)


def _out_proj_kernel(yh_ref, sf_ref, sb_ref, bon_ref, g_ref, lg_ref, lb_ref, avg_ref, wh_ref, wr_ref, r_ref,
                     o_ref):
    s = sf_ref[...] + sb_ref[...]
    mean = _dot3(s, avg_ref[...])
    d = s - mean
    var = _dot((d * d).astype(BF16), avg_ref[...])
    y = d * lax.rsqrt(var + GN_EPS) * lg_ref[...] + lb_ref[...] + bon_ref[...].astype(F32)
    yr = (y * g_ref[...].astype(F32)).astype(BF16)
    o_ref[...] = _dot(yh_ref[...].astype(BF16), wh_ref[...]) + _dot(yr, wr_ref[...]) + r_ref[...]


def _out_proj(yh, sf, sb, bon, g, lnx_g, lnx_b, w, res, tm=512):
    m, ch = yh.shape
    cr = sf.shape[1]
    d = w.shape[1]
    tm = min(tm, m)
    head = jnp.arange(cr) // HEAD
    avg = ((head[:, None] == head[None, :]).astype(F32) / HEAD).astype(BF16)
    row = lambda c: pl.BlockSpec((tm, c), lambda i: (i, 0))
    return pl.pallas_call(
        _out_proj_kernel,
        out_shape=jax.ShapeDtypeStruct((m, d), F32),
        grid=(m // tm,),
        in_specs=[row(ch), row(cr), row(cr), row(cr), row(cr), _const_spec((1, cr)), _const_spec((1, cr)),
                  _const_spec((cr, cr)), _const_spec((ch, d)), _const_spec((cr, d)), row(d)],
        out_specs=row(d),
        compiler_params=_params(("parallel",)),
        name="out_proj",
    )(yh, sf, sb, bon, g, lnx_g.reshape(1, cr), lnx_b.reshape(1, cr), avg, w[:ch], w[ch:], res)


def _rwkv_branch(p, v_first, shift, w0, w2, a0, a2, g2, k_k, k_a, r_k, vres, c):
    r, v, kk, lwf, lwb, kf, kb, af, ab, g, bon = _rw_prep(
        p, v_first, shift, w0, w2, a0, a2, g2, k_k, k_a, r_k, vres, c)
    sf, sb = _wkv_scan(r, v, kk, kf, af, lwf, kb, ab, lwb)
    return sf, sb, bon, g, (v if vres is None else v_first)


def _forward(x, norm1_g, w_in, hy_conv_w, hy_conv_b, hy_f_w1, hy_f_b1, hy_f_w2, hy_f_b2, hy_f_w3, hy_f_b3,
             hy_f_freq, hy_f_wout, hy_skip, hy_norm_g, rw_shift, rw_w0, rw_w2, rw_a0, rw_a2, rw_g2, rw_k_k,
             rw_k_a, rw_r_k, rw_lnx_g, rw_lnx_b, rw_v0, rw_v1, rw_v2, w_out, norm2_g, mlp_w1, mlp_w2,
             final_g):
    b, t, d = x.shape
    depth = w_in.shape[0]
    c_hy = hy_norm_g.shape[1]
    c_rw = rw_lnx_g.shape[1]
    hy_proj = hy_conv_b.shape[1]
    assert b % 2 == 0 and t % CHUNK == 0 and CHUNK == HEAD
    tabs = _dft_tables(t)
    assert tabs["n2"] % SUB16 == 0 and tabs["n1"] % SUB16 == 0 and tabs["n1h"] % SUB == 0
    v_first = None
    for l in range(depth):
        w_l = w_in[l].astype(BF16)
        w_rw = w_l[:, hy_proj:]
        vres = None if l == 0 else (rw_v0[l - 1], rw_v1[l - 1], rw_v2[l - 1])
        if vres is not None:
            v1 = vres[1].astype(BF16)
            w_rw = jnp.concatenate([w_rw, jnp.pad(v1, ((0, 0), (0, -v1.shape[1] % 128)))], axis=1)
        p_hy, p_rw = _in_proj(x, norm1_g[l], w_l[:, :hy_proj], w_rw, tabs["n1h"], tabs["n2"])
        y_hy = _hyena_branch(p_hy, hy_conv_w[l], hy_conv_b[l], hy_f_w1[l], hy_f_b1[l], hy_f_w2[l], hy_f_b2[l],
                             hy_f_w3[l], hy_f_b3[l], hy_f_freq[l], hy_f_wout[l], hy_skip[l], hy_norm_g[l], tabs)
        sf, sb, bon, g, v_first = _rwkv_branch(p_rw, v_first, rw_shift[l], rw_w0[l], rw_w2[l], rw_a0[l], rw_a2[l],
                                               rw_g2[l], rw_k_k[l], rw_k_a[l], rw_r_k[l], vres, c_rw)
        flat = lambda a: a.reshape(b * t, a.shape[-1])
        x = _out_proj(flat(y_hy), flat(sf), flat(sb), flat(bon), flat(g), rw_lnx_g[l], rw_lnx_b[l],
                      w_out[l].astype(BF16), flat(x))
        x = _mlp(x, norm2_g[l], mlp_w1[l].astype(BF16), mlp_w2[l].astype(BF16),
                 final_g=final_g if l == depth - 1 else None).reshape(b, t, d)
    return x


def kernel(x, norm1_g, w_in, hy_conv_w, hy_conv_b, hy_f_w1, hy_f_b1, hy_f_w2, hy_f_b2, hy_f_w3, hy_f_b3,
           hy_f_freq, hy_f_wout, hy_skip, hy_norm_g, rw_shift, rw_w0, rw_w2, rw_a0, rw_a2, rw_g2, rw_k_k,
           rw_k_a, rw_r_k, rw_lnx_g, rw_lnx_b, rw_v0, rw_v1, rw_v2, w_out, norm2_g, mlp_w1, mlp_w2, final_g):
    return _forward(x, norm1_g, w_in, hy_conv_w, hy_conv_b, hy_f_w1, hy_f_b1, hy_f_w2, hy_f_b2, hy_f_w3,
                    hy_f_b3, hy_f_freq, hy_f_wout, hy_skip, hy_norm_g, rw_shift, rw_w0, rw_w2, rw_a0, rw_a2,
                    rw_g2, rw_k_k, rw_k_a, rw_r_k, rw_lnx_g, rw_lnx_b, rw_v0, rw_v1, rw_v2, w_out, norm2_g,
                    mlp_w1, mlp_w2, final_g)
```

```python
import functools
import math

import jax
import jax.numpy as jnp
from jax import lax
from jax.experimental import pallas as pl
from jax.experimental.pallas import tpu as pltpu

F32 = jnp.float32
BF16 = jnp.bfloat16
HP = lax.Precision.HIGHEST

HEAD = 64
HEADS_PER_GROUP = 4
GW = HEAD * HEADS_PER_GROUP
CHUNK = 64
INV_BASE = 8
RMS_EPS = 1e-5
GN_EPS = HEAD * 1e-5
HY_TARGET = 1e-2
HY_FAST_PCT = 0.3
HY_SLOW_PCT = 1.5
VMEM_LIMIT = 56 * 1024 * 1024


def _params(sem, vmem=VMEM_LIMIT):
    return pltpu.CompilerParams(dimension_semantics=sem, vmem_limit_bytes=vmem)


def _dot(a, b, precision=None):
    return jnp.dot(a, b, preferred_element_type=F32, precision=precision)


def _dot_tn(a, b, precision=None):
    return lax.dot_general(a, b, (((0,), (0,)), ((), ())), preferred_element_type=F32,
                           precision=precision)


def _const_spec(shape):
    return pl.BlockSpec(shape, lambda *_: (0,) * len(shape), pipeline_mode=pl.Buffered(1))


def _split3(x):
    h1 = x.astype(BF16)
    r1 = x - h1.astype(F32)
    h2 = r1.astype(BF16)
    h3 = (r1 - h2.astype(F32)).astype(BF16)
    return h1, h2, h3


def _dot3(x, m):
    h1, h2, h3 = _split3(x)
    return _dot(h1, m) + _dot(h2, m) + _dot(h3, m)


def _mlp_kernel(x_ref, g_ref, w1_ref, w2_ref, *rest, nchunk):
    x = x_ref[...]
    hn = (x * lax.rsqrt(jnp.mean(x * x, axis=-1, keepdims=True) + RMS_EPS) * g_ref[...]).astype(BF16)
    ff = w1_ref.shape[1]
    cw = ff // nchunk
    acc = x
    for j in range(nchunk):
        h = _dot(hn, w1_ref[:, j * cw:(j + 1) * cw])
        h = jnp.square(jnp.maximum(h, 0.0)).astype(BF16)
        acc = acc + _dot(h, w2_ref[j * cw:(j + 1) * cw, :])
    if len(rest) == 2:
        gf_ref, o_ref = rest
        acc = acc * lax.rsqrt(jnp.mean(acc * acc, axis=-1, keepdims=True) + RMS_EPS) * gf_ref[...]
    else:
        (o_ref,) = rest
    o_ref[...] = acc


def _mlp(x2d, g, w1, w2, final_g=None, tm=512, nchunk=4):
    m, d = x2d.shape
    ff = w1.shape[1]
    tm = min(tm, m)
    in_specs = [pl.BlockSpec((tm, d), lambda i: (i, 0)), _const_spec((1, d)),
                _const_spec((d, ff)), _const_spec((ff, d))]
    args = [x2d, g.reshape(1, d), w1, w2]
    if final_g is not None:
        in_specs.append(_const_spec((1, d)))
        args.append(final_g.reshape(1, d))
    return pl.pallas_call(
        functools.partial(_mlp_kernel, nchunk=nchunk),
        out_shape=jax.ShapeDtypeStruct((m, d), F32),
        grid=(m // tm,),
        in_specs=in_specs,
        out_specs=pl.BlockSpec((tm, d), lambda i: (i, 0)),
        compiler_params=_params(("parallel",)),
        name="mlp",
    )(*args)


SUB = 8
SUB16 = 16
DFT_GROUP = 8


def _rms(x, g):
    return (x * lax.rsqrt(jnp.mean(x * x, axis=-1, keepdims=True) + RMS_EPS) * g).astype(BF16)


def _in_proj_kernel(x_ref, g_ref, wh_ref, wr_ref, oh_ref, or_ref, *, n2):
    hn = _rms(x_ref[0], g_ref[...])
    or_ref[0] = _dot(hn, wr_ref[...]).astype(or_ref.dtype)
    ph = _dot(hn, wh_ref[...])
    for j in range(SUB):
        oh_ref[0, :, j, :] = ph[j * n2:(j + 1) * n2]


def _in_proj(x, g, w_hy, w_rw, n1h, n2):
    b, t, d = x.shape
    ph, pr = w_hy.shape[1], w_rw.shape[1]
    tm = SUB * n2
    p_hy, p_rw = pl.pallas_call(
        functools.partial(_in_proj_kernel, n2=n2),
        out_shape=(jax.ShapeDtypeStruct((b, n2, n1h, ph), F32), jax.ShapeDtypeStruct((b, t, pr), BF16)),
        grid=(b, t // tm),
        in_specs=[pl.BlockSpec((1, tm, d), lambda i, j: (i, j, 0)), _const_spec((1, d)),
                  _const_spec((d, ph)), _const_spec((d, pr))],
        out_specs=(pl.BlockSpec((1, n2, SUB, ph), lambda i, j: (i, 0, j, 0)),
                   pl.BlockSpec((1, tm, pr), lambda i, j: (i, j, 0))),
        compiler_params=_params(("parallel", "parallel")),
        name="in_proj",
    )(x, g.reshape(1, d), w_hy, w_rw)
    return p_hy.reshape(b, t, ph), p_rw


def _shortconv_kernel(p_ref, w_ref, b_ref, o_ref, *, n1h, n2):
    w0 = w_ref[0:1, :]
    w1 = w_ref[1:2, :]
    w2 = w_ref[2:3, :]
    bias = b_ref[...]
    rows = lax.broadcasted_iota(jnp.int32, (n1h, 1), 0)

    def blk(i):
        return p_ref[0, pl.ds(pl.multiple_of(i * n1h, n1h), n1h), :].astype(F32)

    def body(i, carry):
        o_ref[0, pl.ds(pl.multiple_of(i * n1h, n1h), n1h), :] = (
            w0 * blk(i - 1) + w1 * blk(i) + w2 * blk(i + 1) + bias).astype(o_ref.dtype)
        return carry

    lax.fori_loop(1, n2 - 1, body, 0)
    last = p_ref[0, (n2 - 1) * n1h:, :].astype(F32)
    first = p_ref[0, :n1h, :].astype(F32)
    prev0 = jnp.where(rows == 0, 0.0, pltpu.roll(last, 1, 0))
    o_ref[0, :n1h, :] = (w0 * prev0 + w1 * first + w2 * p_ref[0, n1h:2 * n1h, :].astype(F32)
                         + bias).astype(o_ref.dtype)
    nxt = jnp.where(rows == n1h - 1, 0.0, pltpu.roll(first, n1h - 1, 0))
    o_ref[0, (n2 - 1) * n1h:, :] = (w0 * p_ref[0, (n2 - 2) * n1h:(n2 - 1) * n1h, :].astype(F32) + w1 * last
                                    + w2 * nxt + bias).astype(o_ref.dtype)


def _shortconv(p, w, bias, n1h, n2, cb=512):
    b, t, c = p.shape
    return pl.pallas_call(
        functools.partial(_shortconv_kernel, n1h=n1h, n2=n2),
        out_shape=jax.ShapeDtypeStruct((b, t, c), BF16),
        grid=(b, c // cb),
        in_specs=[pl.BlockSpec((1, t, cb), lambda i, j: (i, 0, j)),
                  pl.BlockSpec((3, cb), lambda i, j: (0, j)),
                  pl.BlockSpec((1, cb), lambda i, j: (0, j))],
        out_specs=pl.BlockSpec((1, t, cb), lambda i, j: (i, 0, j)),
        compiler_params=_params(("parallel", "parallel")),
        name="hy_shortconv",
    )(p, w, bias.reshape(1, c))


def _dft_tables(t):
    n = 2 * t
    n1 = 1 << ((n.bit_length() - 1) // 2)
    n2 = n // n1
    n1h = n1 // 2
    two_pi = 2.0 * math.pi
    k1 = jnp.arange(n1, dtype=jnp.int32)
    n2i = jnp.arange(n2, dtype=jnp.int32)

    def cs(prod, mod):
        ang = (two_pi / mod) * (prod % mod).astype(F32)
        return jnp.cos(ang), jnp.sin(ang)

    tpos = n2i[:, None] + n2 * jnp.arange(n1h, dtype=jnp.int32)[None, :]
    c, s = cs(k1[None, :, None] * tpos[:, None, :], n)
    g1 = jnp.concatenate([jnp.concatenate([c, s], axis=2), jnp.concatenate([-s, c], axis=2)], axis=1)
    ct, st = jnp.swapaxes(c, 1, 2) / n, jnp.swapaxes(s, 1, 2) / n
    g4a = jnp.concatenate([ct, st], axis=1)
    g4b = jnp.concatenate([-st, ct], axis=1)
    tposf = n2i[:, None] + n2 * jnp.arange(n1, dtype=jnp.int32)[None, :]
    cf, sf = cs(k1[None, :, None] * tposf[:, None, :], n)
    g1f = jnp.concatenate([cf, -sf], axis=1)
    c2, s2 = cs(n2i[:, None] * n2i[None, :], n2)
    f2a = jnp.concatenate([c2, -s2], axis=0)
    f2b = jnp.concatenate([s2, c2], axis=0)
    f3 = jnp.concatenate([jnp.concatenate([c2, -s2], axis=1), jnp.concatenate([s2, c2], axis=1)], axis=0)
    g4 = jnp.concatenate([g4a, g4b], axis=2)
    f2 = jnp.concatenate([f2a, f2b], axis=1)
    return dict(n1=n1, n2=n2, n1h=n1h, g1=g1.astype(BF16), g4=g4.astype(BF16), g1f=g1f.astype(BF16),
                f2=f2.astype(BF16), f3=f3.astype(BF16))


def _filter_mlp_kernel(z_ref, w1_ref, b1_ref, w2_ref, b2_ref, w3_ref, b3_ref, fr_ref, wo_ref, dl_ref,
                       o_ref, *, t):
    z = z_ref[...]
    h = jnp.sin(fr_ref[0:1, :] * (_dot(z, w1_ref[...], HP) + b1_ref[...]))
    h = jnp.sin(fr_ref[1:2, :] * (_dot(h, w2_ref[...], HP) + b2_ref[...]))
    h = jnp.sin(fr_ref[2:3, :] * (_dot(h, w3_ref[...], HP) + b3_ref[...]))
    window = jnp.exp(-z[:, 0:1] * dl_ref[...])
    tr = z.shape[0]
    row = pl.program_id(0) * tr + lax.broadcasted_iota(jnp.int32, (tr, 1), 0)
    o_ref[...] = jnp.where(row == t, 0.0, _dot(h, wo_ref[0], HP) * window)


def _filter_taps(t, w1, b1, w2, b2, w3, b3, freq, w_out, c_hy, tr=512):
    emb, width = w1.shape
    bands = (emb - 1) // 2
    pos = jnp.arange(t, dtype=F32)
    tt = pos / max(t - 1, 1)
    fr = jnp.linspace(1e-4, bands - 1, bands, dtype=F32)
    ang = (2.0 * math.pi / t) * pos[:, None] * fr[None, :]
    z = jnp.concatenate([tt[:, None], jnp.cos(ang), -jnp.sin(ang)], axis=-1)
    z = jnp.pad(z, ((0, 0), (0, 128 - emb)))
    w1 = jnp.pad(w1, ((0, 128 - emb), (0, 0)))
    emb = 128
    z2 = jnp.concatenate([z, z[:1], z[1:][::-1]], axis=0)
    max_decay = math.log(HY_TARGET) / HY_FAST_PCT
    min_decay = math.log(HY_TARGET) / HY_SLOW_PCT
    delta = jnp.abs(jnp.linspace(min_decay, max_decay, c_hy, dtype=F32))
    order = w_out.shape[1] // (2 * c_hy)
    nout = order * c_hy
    w_dir = w_out.reshape(width, order, 2, c_hy).transpose(2, 0, 1, 3).reshape(2, width, nout)
    delta_full = jnp.tile(delta, order).reshape(1, nout)
    tr = min(tr, t)
    nt = t // tr
    full = lambda a: pl.BlockSpec(a.shape, lambda i: (0,) * a.ndim)
    args = [z2, w1, b1.reshape(1, -1), w2, b2.reshape(1, -1), w3, b3.reshape(1, -1), freq, w_dir, delta_full]
    in_specs = [pl.BlockSpec((tr, emb), lambda i: (i, 0))] + [full(a) for a in args[1:]]
    in_specs[8] = pl.BlockSpec((1, width, nout), lambda i: (i // nt, 0, 0))
    return pl.pallas_call(
        functools.partial(_filter_mlp_kernel, t=t),
        out_shape=jax.ShapeDtypeStruct((2 * t, nout), F32),
        grid=(2 * nt,),
        in_specs=in_specs,
        out_specs=pl.BlockSpec((tr, nout), lambda i: (i, 0)),
        compiler_params=_params(("parallel",)),
        name="hy_filter_mlp",
    )(*args)


def _filter_s1_kernel(x_ref, g_ref, o_ref, ss_ref, *, g, oc):
    @pl.when(pl.program_id(0) == 0)
    def _():
        ss_ref[...] = jnp.zeros_like(ss_ref)

    n1 = x_ref.shape[0]
    x = jnp.concatenate([x_ref[:, j, :] for j in range(g)], axis=0)
    ss_ref[...] += jnp.sum(x * x, axis=0, keepdims=True)
    xb = x.astype(BF16)
    for j in range(g):
        o_ref[j] = _dot(g_ref[j], xb[j * n1:(j + 1) * n1])


def _filter_s2_kernel(sr_ref, si_ref, f2_ref, ss_ref, o_ref, *, g, oc):
    n2 = sr_ref.shape[0]
    scale = lax.rsqrt(ss_ref[...] + 1e-6)
    s = jnp.concatenate([ref[:, j, :] for j in range(g) for ref in (sr_ref, si_ref)], axis=0).astype(BF16)
    for j in range(g):
        o_ref[j] = _dot(f2_ref[...], s[2 * j * n2:2 * (j + 1) * n2]) * scale


def _filter_spectrum(two, tabs):
    n, oc = two.shape
    n1, n2 = tabs["n1"], tabs["n2"]
    g = SUB
    s1, ss = pl.pallas_call(
        functools.partial(_filter_s1_kernel, g=g, oc=oc),
        out_shape=(jax.ShapeDtypeStruct((n2, 2 * n1, oc), F32), jax.ShapeDtypeStruct((1, oc), F32)),
        grid=(n2 // g,),
        in_specs=[pl.BlockSpec((n1, g, oc), lambda j: (0, j, 0)),
                  pl.BlockSpec((g, 2 * n1, n1), lambda j: (j, 0, 0))],
        out_specs=(pl.BlockSpec((g, 2 * n1, oc), lambda j: (j, 0, 0)),
                   pl.BlockSpec((1, oc), lambda j: (0, 0))),
        compiler_params=_params(("arbitrary",)),
        name="hy_filter_dft1",
    )(two.reshape(n1, n2, oc), tabs["g1f"])
    return pl.pallas_call(
        functools.partial(_filter_s2_kernel, g=g, oc=oc),
        out_shape=jax.ShapeDtypeStruct((n1, 2 * n2, oc), F32),
        grid=(n1 // g,),
        in_specs=[pl.BlockSpec((n2, g, oc), lambda k: (0, k, 0)),
                  pl.BlockSpec((n2, g, oc), lambda k: (0, n1 // g + k, 0)),
                  pl.BlockSpec((2 * n2, 2 * n2), lambda k: (0, 0)),
                  pl.BlockSpec((1, oc), lambda k: (0, 0))],
        out_specs=pl.BlockSpec((g, 2 * n2, oc), lambda k: (k, 0, 0)),
        compiler_params=_params(("parallel",)),
        name="hy_filter_dft2",
    )(s1, s1, tabs["f2"], ss)


def _conv_s1_kernel(u_ref, g_ref, o_ref, t_ref, *, g, n1h):
    for j0 in range(0, g, DFT_GROUP):
        js = range(j0, j0 + DFT_GROUP)
        rs = [_dot(g_ref[j], jnp.concatenate([u_ref[0, 0, j * n1h:(j + 1) * n1h, :],
                                               u_ref[0, 1, j * n1h:(j + 1) * n1h, :]], axis=0)) for j in js]
        for j, r in zip(js, rs):
            t_ref[:, j, :] = r
    o_ref[0] = t_ref[...].astype(o_ref.dtype)


def _conv_s23_kernel(sr_ref, si_ref, f2_ref, f3_ref, h_ref, o_ref, t_ref, *, g, n2):
    for j0 in range(0, g, DFT_GROUP):
        js = range(j0, j0 + DFT_GROUP)
        xs = [_dot(f2_ref[...], jnp.concatenate([sr_ref[0, j], si_ref[0, j]], axis=0)) for j in js]
        ys = []
        for j, x in zip(js, xs):
            xr, xi = x[:n2], x[n2:]
            hr, hi = h_ref[j, :n2, :], h_ref[j, n2:, :]
            ys.append(jnp.concatenate([xr * hr - xi * hi, xr * hi + xi * hr], axis=0).astype(BF16))
        rs = [_dot(f3_ref[...], y) for y in ys]
        for j, r in zip(js, rs):
            t_ref[:, j, :] = r
    o_ref[0] = t_ref[...].astype(o_ref.dtype)


def _conv_s4_kernel(rr_ref, ri_ref, g4_ref, u_ref, gate_ref, skip_ref, *rest, g, n1h, norm):
    if norm:
        ng_ref, avg_ref, o_ref = rest
    else:
        (o_ref,) = rest
    skip = skip_ref[...]
    ys = [_dot(g4_ref[j], jnp.concatenate([rr_ref[0, j], ri_ref[0, j]], axis=0)) for j in range(g)]
    zs = [gate_ref[0, q, j * n1h:(j + 1) * n1h, :].astype(F32)
          * (ys[j][q * n1h:(q + 1) * n1h] + u_ref[0, q, j * n1h:(j + 1) * n1h, :].astype(F32) * skip)
          for j in range(g) for q in range(2)]
    if norm:
        ms = [_dot((z * z).astype(BF16), avg_ref[...]) for z in zs]
        zs = [z * lax.rsqrt(m + RMS_EPS) * ng_ref[...] for z, m in zip(zs, ms)]
    for j in range(g):
        for q in range(2):
            if norm:
                o_ref[0, q, :, j, :] = zs[2 * j + q]
            else:
                o_ref[0, q, j * n1h:(j + 1) * n1h, :] = zs[2 * j + q].astype(o_ref.dtype)


def _long_conv_gate(u_arr, u_col, gate_arr, gate_col, h_spec, h_col, skip, tabs, c, norm_g=None):
    b, t, _ = u_arr.shape
    n1, n2, n1h = tabs["n1"], tabs["n2"], tabs["n1h"]
    g = SUB
    npair = b // 2
    u4 = u_arr.reshape(npair, 2, t, u_arr.shape[2])
    gate4 = gate_arr.reshape(npair, 2, t, gate_arr.shape[2])
    gt = SUB16
    s1 = pl.pallas_call(
        functools.partial(_conv_s1_kernel, g=gt, n1h=n1h),
        out_shape=jax.ShapeDtypeStruct((npair, 2 * n1, n2, c), BF16),
        grid=(n2 // gt, npair),
        in_specs=[pl.BlockSpec((1, 2, gt * n1h, c), lambda j, p: (p, 0, j, u_col)),
                  pl.BlockSpec((gt, 2 * n1, n1), lambda j, p: (j, 0, 0))],
        out_specs=pl.BlockSpec((1, 2 * n1, gt, c), lambda j, p: (p, 0, j, 0)),
        scratch_shapes=[pltpu.VMEM((2 * n1, gt, c), F32)],
        compiler_params=_params(("parallel", "parallel")),
        name="hy_conv_dft1",
    )(u4, tabs["g1"])
    r = pl.pallas_call(
        functools.partial(_conv_s23_kernel, g=gt, n2=n2),
        out_shape=jax.ShapeDtypeStruct((npair, 2 * n2, n1, c), BF16),
        grid=(n1 // gt, npair),
        in_specs=[pl.BlockSpec((1, gt, n2, c), lambda k, p: (p, k, 0, 0)),
                  pl.BlockSpec((1, gt, n2, c), lambda k, p: (p, n1 // gt + k, 0, 0)),
                  pl.BlockSpec((2 * n2, 2 * n2), lambda k, p: (0, 0)),
                  pl.BlockSpec((2 * n2, 2 * n2), lambda k, p: (0, 0)),
                  pl.BlockSpec((gt, 2 * n2, c), lambda k, p: (k, 0, h_col))],
        out_specs=pl.BlockSpec((1, 2 * n2, gt, c), lambda k, p: (p, 0, k, 0)),
        scratch_shapes=[pltpu.VMEM((2 * n2, gt, c), F32)],
        compiler_params=_params(("parallel", "parallel")),
        name="hy_conv_dft23",
    )(s1, s1, tabs["f2"], tabs["f3"], h_spec)
    norm = norm_g is not None
    in_specs = [pl.BlockSpec((1, g, n1, c), lambda p, j: (p, j, 0, 0)),
                pl.BlockSpec((1, g, n1, c), lambda p, j: (p, n2 // g + j, 0, 0)),
                pl.BlockSpec((g, n1, 2 * n1), lambda p, j: (j, 0, 0)),
                pl.BlockSpec((1, 2, g * n1h, c), lambda p, j: (p, 0, j, u_col)),
                pl.BlockSpec((1, 2, g * n1h, c), lambda p, j: (p, 0, j, gate_col)),
                pl.BlockSpec((1, c), lambda p, j: (0, 0))]
    args = [r, r, tabs["g4"], u4, gate4, skip.reshape(1, c)]
    if norm:
        ch = jnp.arange(c) // HEAD
        avg = ((ch[:, None] == ch[None, :]).astype(F32) / HEAD).astype(BF16)
        in_specs += [pl.BlockSpec((1, c), lambda p, j: (0, 0)), pl.BlockSpec((c, c), lambda p, j: (0, 0))]
        args += [norm_g.reshape(1, c), avg]
        out_shape = jax.ShapeDtypeStruct((npair, 2, n1h, n2, c), F32)
        out_spec = pl.BlockSpec((1, 2, n1h, g, c), lambda p, j: (p, 0, 0, j, 0))
    else:
        out_shape = jax.ShapeDtypeStruct((npair, 2, t, c), BF16)
        out_spec = pl.BlockSpec((1, 2, g * n1h, c), lambda p, j: (p, 0, j, 0))
    out = pl.pallas_call(
        functools.partial(_conv_s4_kernel, g=g, n1h=n1h, norm=norm),
        out_shape=out_shape,
        grid=(npair, n2 // g),
        in_specs=in_specs,
        out_specs=out_spec,
        compiler_params=_params(("parallel", "parallel")),
        name="hy_conv_dft4",
    )(*args)
    return out.reshape(b, t, c)


def _hyena_branch(p, conv_w, conv_b, fw1, fb1, fw2, fb2, fw3, fb3, ffreq, fwout, skip, norm_g, tabs):
    b, t, _ = p.shape
    c = norm_g.shape[0]
    n1, n2, n1h = tabs["n1"], tabs["n2"], tabs["n1h"]
    u = _shortconv(p, conv_w, conv_b, n1h, n2)
    two = _filter_taps(t, fw1, fb1, fw2, fb2, fw3, fb3, ffreq, fwout, c)
    hspec = _filter_spectrum(two, tabs)
    z = _long_conv_gate(u, 2, u, 0, hspec, 0, skip[0], tabs, c)
    return _long_conv_gate(z, 0, u, 1, hspec, 1, skip[1], tabs, c, norm_g=norm_g)


def _rw_prep_kernel(*refs, has_vres, c):
    if has_vres:
        (p_ref, pp_ref, pn_ref, mu_ref, w0_ref, w2_ref, a0_ref, a2_ref, g2_ref, kk_ref, ka_ref, rk_ref,
         sum_ref, vf_ref, v0_ref, v2_ref,
         r_o, v_o, kk_o, lwf_o, lwb_o, kf_o, kb_o, af_o, ab_o, g_o, bon_o) = refs
    else:
        (p_ref, pp_ref, pn_ref, mu_ref, w0_ref, w2_ref, a0_ref, a2_ref, g2_ref, kk_ref, ka_ref, rk_ref,
         sum_ref,
         r_o, v_o, kk_o, lwf_o, lwb_o, kf_o, kb_o, af_o, ab_o, g_o, bon_o) = refs
    i = pl.program_id(1)
    last = pl.num_programs(1) - 1
    nsh = mu_ref.shape[1]
    p = p_ref[0, :, :nsh].astype(F32)
    tt = p.shape[0]
    rows = lax.broadcasted_iota(jnp.int32, (tt, 1), 0)
    prev_row = jnp.where(i == 0, 0.0, pp_ref[0, SUB16 - 1:SUB16, :nsh].astype(F32))
    next_row = jnp.where(i == last, 0.0, pn_ref[0, 0:1, :nsh].astype(F32))
    prev = jnp.where(rows == 0, prev_row, pltpu.roll(p, 1, 0))
    nxt = jnp.where(rows == tt - 1, next_row, pltpu.roll(p, tt - 1, 0))
    mu0, mu1 = mu_ref[0:1, :], mu_ref[1:2, :]
    pf = p * (1.0 - mu0 - mu1) + mu0 * prev + mu1 * nxt
    r = pf[:, :c]
    k = pf[:, c:2 * c]
    v = pf[:, 2 * c:3 * c]
    lw = 3 * c
    nd = w2_ref.shape[0]
    na = a2_ref.shape[0]
    wd = pf[:, lw:lw + nd]
    ad = pf[:, lw + nd:lw + nd + na]
    gd = pf[:, lw + nd + na:]
    if has_vres:
        lora = _dot(p_ref[0, :, nsh:], v2_ref[...])
        v = v + (vf_ref[0].astype(F32) - v) * jax.nn.sigmoid(v0_ref[...] + lora)
    g = _dot(jax.nn.sigmoid(gd).astype(BF16), g2_ref[...])
    kk = k * kk_ref[...]
    kk = kk * lax.rsqrt(jnp.maximum(_dot((kk * kk).astype(BF16), sum_ref[...]), 1e-24))
    wl = w0_ref[...] + _dot(jnp.tanh(wd).astype(BF16), w2_ref[...])
    logw = -math.exp(-0.5) * jax.nn.sigmoid(wl)
    a = jax.nn.sigmoid(a0_ref[...] + _dot(ad.astype(BF16), a2_ref[...]))
    ka = ka_ref[...]
    rk = rk_ref[...]
    k_d = [k * (1.0 + (a[:, d * c:(d + 1) * c] - 1.0) * ka) for d in range(2)]
    bon = _dot((r * (k_d[0] + k_d[1]) * rk).astype(BF16), sum_ref[...]) * v
    r_o[0] = r.astype(BF16)
    v_o[0] = v.astype(BF16)
    kk_o[0] = kk.astype(BF16)
    lwf_o[0] = logw[:, :c]
    lwb_o[0] = logw[:, c:]
    kf_o[0] = k_d[0].astype(BF16)
    kb_o[0] = k_d[1].astype(BF16)
    af_o[0] = a[:, :c].astype(BF16)
    ab_o[0] = a[:, c:].astype(BF16)
    g_o[0] = g.astype(BF16)
    bon_o[0] = bon.astype(BF16)


def _blockdiag2(m):
    k, c = m.shape[1], m.shape[2]
    z = jnp.zeros((k, c), m.dtype)
    return jnp.concatenate([jnp.concatenate([m[0], z], axis=1), jnp.concatenate([z, m[1]], axis=1)], axis=0)


def _rw_prep(p, v_first, shift, w0, w2, a0, a2, g2, k_k, k_a, r_k, vres, c, tt=256):
    b, t, pw = p.shape
    tt = min(tt, t)
    has_vres = vres is not None
    ch = jnp.arange(c) // HEAD
    summ = (ch[:, None] == ch[None, :]).astype(BF16)
    row = lambda a: a.reshape(1, -1)
    args = [p, p, p, shift, row(w0), _blockdiag2(w2).astype(BF16), row(a0), _blockdiag2(a2).astype(BF16),
            g2.astype(BF16), row(k_k), row(k_a), row(r_k), summ]
    full = lambda a: pl.BlockSpec(a.shape, lambda i, j: (0,) * a.ndim)
    nhb = t // SUB16
    in_specs = [pl.BlockSpec((1, tt, pw), lambda i, j: (i, j, 0)),
                pl.BlockSpec((1, SUB16, pw), lambda i, j: (i, jnp.maximum(j * (tt // SUB16) - 1, 0), 0)),
                pl.BlockSpec((1, SUB16, pw), lambda i, j: (i, jnp.minimum((j + 1) * (tt // SUB16), nhb - 1), 0))]
    in_specs += [full(a) for a in args[3:]]
    if has_vres:
        v0, _, v2 = vres
        v2p = jnp.pad(v2, ((0, pw - shift.shape[1] - v2.shape[0]), (0, 0))).astype(BF16)
        extra = [v_first, row(v0), v2p]
        in_specs += [pl.BlockSpec((1, tt, c), lambda i, j: (i, j, 0))] + [full(a) for a in extra[1:]]
        args += extra
    outs = tuple(jax.ShapeDtypeStruct((b, t, c), F32 if i in (3, 4) else BF16) for i in range(11))
    return pl.pallas_call(
        functools.partial(_rw_prep_kernel, has_vres=has_vres, c=c),
        out_shape=outs,
        grid=(b, t // tt),
        in_specs=in_specs,
        out_specs=tuple(pl.BlockSpec((1, tt, c), lambda i, j: (i, j, 0)) for _ in range(11)),
        compiler_params=_params(("parallel", "parallel")),
        name="rw_prep",
    )(*args)


def _scan_chunks(s0, r, k, v, kk, a, lw, rev, bdm):
    L = CHUNK
    n = len(r)
    each = lambda f, *ls: [f(*xs) for xs in zip(*ls)]
    ti = lax.broadcasted_iota(jnp.int32, (L, L), 0)
    si = lax.broadcasted_iota(jnp.int32, (L, L), 1)
    tt = lax.broadcasted_iota(jnp.int32, (L, GW), 0)
    ss = lax.broadcasted_iota(jnp.int32, (L, GW), 1) % L
    eye = (ss == tt).astype(F32)
    tri = [((si >= ti) if q else (si <= ti)).astype(F32).astype(BF16) for q in rev]
    strict = [(ss > tt) if q else (ss < tt) for q in rev]
    incl = [(ss >= tt) if q else (ss <= tt) for q in rev]
    bdf = bdm.astype(F32)

    def bd(x):
        xb = x.astype(BF16)
        return jnp.concatenate([xb] * HEADS_PER_GROUP, axis=0) * bdm

    def bd_t(x):
        xt = x.T.astype(BF16)
        return jnp.concatenate([xt] * HEADS_PER_GROUP, axis=1) * bdm

    def rcmul(x, y):
        return _dot(x.astype(BF16), bd(y))

    parts = each(_split3, lw)
    cum = [_dot(t, p[0]) + _dot(t, p[1]) + _dot(t, p[2]) for t, p in zip(tri, parts)]
    tot = [c[0:1, :] if q else c[L - 1:L, :] for c, q in zip(cum, rev)]
    p_in = each(jnp.exp, cum)
    p_inv = each(lambda c: jnp.exp(-c), cum)
    p_ex = each(lambda c, w: jnp.exp(c - w), cum, lw)
    p_rem = each(lambda t, c: jnp.exp(t - c), tot, cum)
    beta = each(lambda x, y: x * y, kk, a)
    a_t = each(lambda x, p: -x * p, kk, p_ex)
    r_t = each(lambda x, p: x * p, r, p_in)
    k_t = each(lambda x, p: x * p, k, p_inv)
    b_t = each(lambda x, p: x * p, beta, p_inv)
    k_h = each(lambda x, p: x * p, k, p_rem)
    b_h = each(lambda x, p: x * p, beta, p_rem)

    lhs = each(lambda x, y: jnp.concatenate([x, y], axis=0).astype(BF16), a_t, r_t)
    akk = each(lambda x, y: _dot(x, bd_t(y)), lhs, k_t)
    abb = each(lambda x, y: _dot(x, bd_t(y)), lhs, b_t)
    a_ak = each(lambda m, x: jnp.where(m, x[:L], 0.0), strict, akk)
    a_rk = each(lambda m, x: jnp.where(m, x[L:], 0.0), incl, akk)
    a_ab = each(lambda m, x: jnp.where(m, x[:L], 0.0), strict, abb)
    a_rb = each(lambda m, x: jnp.where(m, x[L:], 0.0), incl, abb)

    same = lambda size: (tt // size) == (ss // size)
    d = each(lambda x: jnp.where(same(INV_BASE), x, 0.0), a_ab)
    tinv = each(lambda x: eye + x, d)
    pw = each(rcmul, d, d)
    size = 4
    while size < INV_BASE:
        both = each(lambda p, t: rcmul(jnp.concatenate([p, t], axis=0), p), pw, tinv)
        pw = [x[:L] for x in both]
        tinv = each(lambda t, x: t + x[L:], tinv, both)
        size *= 2
    tinv = each(lambda t, p: t + rcmul(t, p), tinv, pw)
    size = INV_BASE
    while size < L:
        size *= 2
        couple = same(size) & ~same(size // 2)
        e = each(lambda x: jnp.where(couple, x, 0.0), a_ab)
        tinv = each(lambda t, y: t + rcmul(rcmul(t, y), t), tinv, e)

    av = each(lambda x, y, z: rcmul(jnp.concatenate([x, y], axis=0), z), a_ak, a_rk, v)
    ct = each(rcmul, a_rb, tinv)
    wu2 = each(lambda t, c, x, y: _dot(jnp.concatenate([t, c], axis=0).astype(BF16),
                                       jnp.concatenate([bd(x), bd(y[:L])], axis=1)), tinv, ct, a_t, av)
    wu = [x[:L] for x in wu2]
    rbwu = [x[L:] for x in wu2]
    o = each(lambda x, w, s, y: _dot((x + w[:, :GW]).astype(BF16), bd_t(s)) + y[L:] + w[:, GW:],
             r_t, rbwu, s0, av)

    m_bd = each(lambda w, x: _dot_tn(w[:, :GW].astype(BF16), x.astype(BF16)) * bdf, wu, b_h)
    n_full = each(lambda x, w, y, z: _dot_tn(jnp.concatenate([x, w[:, GW:]], axis=0).astype(BF16),
                                             jnp.concatenate([y, z], axis=0).astype(BF16)) * bdf,
                  v, wu, k_h, b_h)
    s1 = []
    for i in range(n):
        n_rc = n_full[i][0:HEAD]
        for h in range(1, HEADS_PER_GROUP):
            n_rc = n_rc + n_full[i][h * HEAD:(h + 1) * HEAD]
        s1.append(_dot(s0[i].astype(BF16), m_bd[i].astype(BF16)) + s0[i] * jnp.exp(tot[i]) + n_rc)
    return s1, o


def _scan_kernel(rf_ref, kf_ref, vf_ref, kkf_ref, af_ref, lwf_ref,
                 rb_ref, kb_ref, vb_ref, kkb_ref, ab_ref, lwb_ref, of_ref, ob_ref, s_ref, *, ng, nb):
    @pl.when(pl.program_id(1) == 0)
    def _():
        s_ref[...] = jnp.zeros_like(s_ref)

    gi = lax.broadcasted_iota(jnp.int32, (GW, GW), 0) // HEAD
    gj = lax.broadcasted_iota(jnp.int32, (GW, GW), 1) // HEAD
    bdm = (gi == gj).astype(F32).astype(BF16)
    dirs = ((rf_ref, kf_ref, vf_ref, kkf_ref, af_ref, lwf_ref), (rb_ref, kb_ref, vb_ref, kkb_ref, ab_ref, lwb_ref))
    chains = [(i, d, h) for i in range(nb) for d in range(2) for h in range(ng)]
    ins = [[dirs[d][q][i, :, h * GW:(h + 1) * GW].astype(F32) for i, d, h in chains] for q in range(6)]
    s1, o = _scan_chunks([s_ref[i, d, h] for i, d, h in chains], *ins, [d == 1 for _, d, _ in chains], bdm)
    for (i, d, h), s_new, o_new in zip(chains, s1, o):
        (of_ref, ob_ref)[d][i, :, h * GW:(h + 1) * GW] = o_new
        s_ref[i, d, h] = s_new


SCAN_BATCH_ROWS = 4


def _wkv_scan(r, v, kk, kf, af, lwf, kb, ab, lwb):
    b, t, c = r.shape
    nc = t // CHUNK
    ng = c // GW
    nb = math.gcd(b, SCAN_BATCH_ROWS)
    fspec = pl.BlockSpec((nb, CHUNK, c), lambda i, j: (i, j, 0))
    bspec = pl.BlockSpec((nb, CHUNK, c), lambda i, j: (i, nc - 1 - j, 0))
    return pl.pallas_call(
        functools.partial(_scan_kernel, ng=ng, nb=nb),
        out_shape=(jax.ShapeDtypeStruct((b, t, c), F32), jax.ShapeDtypeStruct((b, t, c), F32)),
        grid=(b // nb, nc),
        in_specs=[fspec] * 6 + [bspec] * 6,
        out_specs=(fspec, bspec),
        scratch_shapes=[pltpu.VMEM((nb, 2, ng, HEAD, GW), F32)],
        compiler_params=_params(("parallel", "arbitrary")),
        name="rw_scan",
    )(r, kf, v, kk, af, lwf, r, kb, v, kk, ab, lwb)


def _out_proj_kernel(yh_ref, sf_ref, sb_ref, bon_ref, g_ref, lg_ref, lb_ref, avg_ref, wh_ref, wr_ref, r_ref,
                     o_ref):
    s = sf_ref[...] + sb_ref[...]
    mean = _dot3(s, avg_ref[...])
    d = s - mean
    var = _dot((d * d).astype(BF16), avg_ref[...])
    y = d * lax.rsqrt(var + GN_EPS) * lg_ref[...] + lb_ref[...] + bon_ref[...].astype(F32)
    yr = (y * g_ref[...].astype(F32)).astype(BF16)
    o_ref[...] = _dot(yh_ref[...].astype(BF16), wh_ref[...]) + _dot(yr, wr_ref[...]) + r_ref[...]


def _out_proj(yh, sf, sb, bon, g, lnx_g, lnx_b, w, res, tm=1024):
    m, ch = yh.shape
    cr = sf.shape[1]
    d = w.shape[1]
    tm = min(tm, m)
    head = jnp.arange(cr) // HEAD
    avg = ((head[:, None] == head[None, :]).astype(F32) / HEAD).astype(BF16)
    row = lambda c: pl.BlockSpec((tm, c), lambda i: (i, 0))
    return pl.pallas_call(
        _out_proj_kernel,
        out_shape=jax.ShapeDtypeStruct((m, d), F32),
        grid=(m // tm,),
        in_specs=[row(ch), row(cr), row(cr), row(cr), row(cr), _const_spec((1, cr)), _const_spec((1, cr)),
                  _const_spec((cr, cr)), _const_spec((ch, d)), _const_spec((cr, d)), row(d)],
        out_specs=row(d),
        compiler_params=_params(("parallel",)),
        name="out_proj",
    )(yh, sf, sb, bon, g, lnx_g.reshape(1, cr), lnx_b.reshape(1, cr), avg, w[:ch], w[ch:], res)


def _rwkv_branch(p, v_first, shift, w0, w2, a0, a2, g2, k_k, k_a, r_k, vres, c):
    r, v, kk, lwf, lwb, kf, kb, af, ab, g, bon = _rw_prep(
        p, v_first, shift, w0, w2, a0, a2, g2, k_k, k_a, r_k, vres, c)
    sf, sb = _wkv_scan(r, v, kk, kf, af, lwf, kb, ab, lwb)
    return sf, sb, bon, g, (v if vres is None else v_first)


def _forward(x, norm1_g, w_in, hy_conv_w, hy_conv_b, hy_f_w1, hy_f_b1, hy_f_w2, hy_f_b2, hy_f_w3, hy_f_b3,
             hy_f_freq, hy_f_wout, hy_skip, hy_norm_g, rw_shift, rw_w0, rw_w2, rw_a0, rw_a2, rw_g2, rw_k_k,
             rw_k_a, rw_r_k, rw_lnx_g, rw_lnx_b, rw_v0, rw_v1, rw_v2, w_out, norm2_g, mlp_w1, mlp_w2,
             final_g):
    b, t, d = x.shape
    depth = w_in.shape[0]
    c_hy = hy_norm_g.shape[1]
    c_rw = rw_lnx_g.shape[1]
    hy_proj = hy_conv_b.shape[1]
    assert b % 2 == 0 and t % CHUNK == 0 and CHUNK == HEAD
    tabs = _dft_tables(t)
    assert tabs["n2"] % SUB16 == 0 and tabs["n1"] % SUB16 == 0 and tabs["n1h"] % SUB == 0
    v_first = None
    for l in range(depth):
        w_l = w_in[l].astype(BF16)
        w_rw = w_l[:, hy_proj:]
        vres = None if l == 0 else (rw_v0[l - 1], rw_v1[l - 1], rw_v2[l - 1])
        if vres is not None:
            v1 = vres[1].astype(BF16)
            w_rw = jnp.concatenate([w_rw, jnp.pad(v1, ((0, 0), (0, -v1.shape[1] % 128)))], axis=1)
        p_hy, p_rw = _in_proj(x, norm1_g[l], w_l[:, :hy_proj], w_rw, tabs["n1h"], tabs["n2"])
        y_hy = _hyena_branch(p_hy, hy_conv_w[l], hy_conv_b[l], hy_f_w1[l], hy_f_b1[l], hy_f_w2[l], hy_f_b2[l],
                             hy_f_w3[l], hy_f_b3[l], hy_f_freq[l], hy_f_wout[l], hy_skip[l], hy_norm_g[l], tabs)
        sf, sb, bon, g, v_first = _rwkv_branch(p_rw, v_first, rw_shift[l], rw_w0[l], rw_w2[l], rw_a0[l], rw_a2[l],
                                               rw_g2[l], rw_k_k[l], rw_k_a[l], rw_r_k[l], vres, c_rw)
        flat = lambda a: a.reshape(b * t, a.shape[-1])
        x = _out_proj(flat(y_hy), flat(sf), flat(sb), flat(bon), flat(g), rw_lnx_g[l], rw_lnx_b[l],
                      w_out[l].astype(BF16), flat(x))
        x = _mlp(x, norm2_g[l], mlp_w1[l].astype(BF16), mlp_w2[l].astype(BF16),
                 final_g=final_g if l == depth - 1 else None).reshape(b, t, d)
    return x


def kernel(x, norm1_g, w_in, hy_conv_w, hy_conv_b, hy_f_w1, hy_f_b1, hy_f_w2, hy_f_b2, hy_f_w3, hy_f_b3,
           hy_f_freq, hy_f_wout, hy_skip, hy_norm_g, rw_shift, rw_w0, rw_w2, rw_a0, rw_a2, rw_g2, rw_k_k,
           rw_k_a, rw_r_k, rw_lnx_g, rw_lnx_b, rw_v0, rw_v1, rw_v2, w_out, norm2_g, mlp_w1, mlp_w2, final_g):
    return _forward(x, norm1_g, w_in, hy_conv_w, hy_conv_b, hy_f_w1, hy_f_b1, hy_f_w2, hy_f_b2, hy_f_w3,
                    hy_f_b3, hy_f_freq, hy_f_wout, hy_skip, hy_norm_g, rw_shift, rw_w0, rw_w2, rw_a0, rw_a2,
                    rw_g2, rw_k_k, rw_k_a, rw_r_k, rw_lnx_g, rw_lnx_b, rw_v0, rw_v1, rw_v2, w_out, norm2_g,
                    mlp_w1, mlp_w2, final_g)
```

```python
import functools
import math

import jax
import jax.numpy as jnp
from jax import lax
from jax.experimental import pallas as pl
from jax.experimental.pallas import tpu as pltpu

F32 = jnp.float32
BF16 = jnp.bfloat16
HP = lax.Precision.HIGHEST

HEAD = 64
HEADS_PER_GROUP = 4
GW = HEAD * HEADS_PER_GROUP
CHUNK = 64
INV_BASE = 8
RMS_EPS = 1e-5
GN_EPS = HEAD * 1e-5
HY_TARGET = 1e-2
HY_FAST_PCT = 0.3
HY_SLOW_PCT = 1.5
VMEM_LIMIT = 56 * 1024 * 1024


def _params(sem, vmem=VMEM_LIMIT):
    return pltpu.CompilerParams(dimension_semantics=sem, vmem_limit_bytes=vmem)


def _dot(a, b, precision=None):
    return jnp.dot(a, b, preferred_element_type=F32, precision=precision)


def _dot_tn(a, b, precision=None):
    return lax.dot_general(a, b, (((0,), (0,)), ((), ())), preferred_element_type=F32,
                           precision=precision)


def _const_spec(shape):
    return pl.BlockSpec(shape, lambda *_: (0,) * len(shape), pipeline_mode=pl.Buffered(1))


def _split3(x):
    h1 = x.astype(BF16)
    r1 = x - h1.astype(F32)
    h2 = r1.astype(BF16)
    h3 = (r1 - h2.astype(F32)).astype(BF16)
    return h1, h2, h3


def _dot3(x, m):
    h1, h2, h3 = _split3(x)
    return _dot(h1, m) + _dot(h2, m) + _dot(h3, m)


def _mlp_kernel(x_ref, g_ref, w1_ref, w2_ref, *rest, nchunk):
    x = x_ref[...]
    hn = (x * lax.rsqrt(jnp.mean(x * x, axis=-1, keepdims=True) + RMS_EPS) * g_ref[...]).astype(BF16)
    ff = w1_ref.shape[1]
    cw = ff // nchunk
    acc = x
    for j in range(nchunk):
        h = _dot(hn, w1_ref[:, j * cw:(j + 1) * cw])
        h = jnp.square(jnp.maximum(h, 0.0)).astype(BF16)
        acc = acc + _dot(h, w2_ref[j * cw:(j + 1) * cw, :])
    if len(rest) == 2:
        gf_ref, o_ref = rest
        acc = acc * lax.rsqrt(jnp.mean(acc * acc, axis=-1, keepdims=True) + RMS_EPS) * gf_ref[...]
    else:
        (o_ref,) = rest
    o_ref[...] = acc


def _mlp(x2d, g, w1, w2, final_g=None, tm=1024, nchunk=4):
    m, d = x2d.shape
    ff = w1.shape[1]
    tm = min(tm, m)
    in_specs = [pl.BlockSpec((tm, d), lambda i: (i, 0)), _const_spec((1, d)),
                _const_spec((d, ff)), _const_spec((ff, d))]
    args = [x2d, g.reshape(1, d), w1, w2]
    if final_g is not None:
        in_specs.append(_const_spec((1, d)))
        args.append(final_g.reshape(1, d))
    return pl.pallas_call(
        functools.partial(_mlp_kernel, nchunk=nchunk),
        out_shape=jax.ShapeDtypeStruct((m, d), F32),
        grid=(m // tm,),
        in_specs=in_specs,
        out_specs=pl.BlockSpec((tm, d), lambda i: (i, 0)),
        compiler_params=_params(("parallel",)),
        name="mlp",
    )(*args)


SUB = 8
SUB16 = 16
DFT_GROUP = 8


def _rms(x, g):
    return (x * lax.rsqrt(jnp.mean(x * x, axis=-1, keepdims=True) + RMS_EPS) * g).astype(BF16)


def _in_proj_kernel(x_ref, g_ref, wh_ref, wr_ref, oh_ref, or_ref, *, n2):
    hn = _rms(x_ref[0], g_ref[...])
    or_ref[0] = _dot(hn, wr_ref[...]).astype(or_ref.dtype)
    ph = _dot(hn, wh_ref[...])
    for j in range(SUB):
        oh_ref[0, :, j, :] = ph[j * n2:(j + 1) * n2]


def _in_proj(x, g, w_hy, w_rw, n1h, n2):
    b, t, d = x.shape
    ph, pr = w_hy.shape[1], w_rw.shape[1]
    tm = SUB * n2
    p_hy, p_rw = pl.pallas_call(
        functools.partial(_in_proj_kernel, n2=n2),
        out_shape=(jax.ShapeDtypeStruct((b, n2, n1h, ph), F32), jax.ShapeDtypeStruct((b, t, pr), BF16)),
        grid=(b, t // tm),
        in_specs=[pl.BlockSpec((1, tm, d), lambda i, j: (i, j, 0)), _const_spec((1, d)),
                  _const_spec((d, ph)), _const_spec((d, pr))],
        out_specs=(pl.BlockSpec((1, n2, SUB, ph), lambda i, j: (i, 0, j, 0)),
                   pl.BlockSpec((1, tm, pr), lambda i, j: (i, j, 0))),
        compiler_params=_params(("parallel", "parallel")),
        name="in_proj",
    )(x, g.reshape(1, d), w_hy, w_rw)
    return p_hy.reshape(b, t, ph), p_rw


def _shortconv_kernel(p_ref, w_ref, b_ref, o_ref, *, n1h, n2):
    w0 = w_ref[0:1, :]
    w1 = w_ref[1:2, :]
    w2 = w_ref[2:3, :]
    bias = b_ref[...]
    rows = lax.broadcasted_iota(jnp.int32, (n1h, 1), 0)

    def blk(i):
        return p_ref[0, pl.ds(pl.multiple_of(i * n1h, n1h), n1h), :].astype(F32)

    def body(i, carry):
        o_ref[0, pl.ds(pl.multiple_of(i * n1h, n1h), n1h), :] = (
            w0 * blk(i - 1) + w1 * blk(i) + w2 * blk(i + 1) + bias).astype(o_ref.dtype)
        return carry

    lax.fori_loop(1, n2 - 1, body, 0)
    last = p_ref[0, (n2 - 1) * n1h:, :].astype(F32)
    first = p_ref[0, :n1h, :].astype(F32)
    prev0 = jnp.where(rows == 0, 0.0, pltpu.roll(last, 1, 0))
    o_ref[0, :n1h, :] = (w0 * prev0 + w1 * first + w2 * p_ref[0, n1h:2 * n1h, :].astype(F32)
                         + bias).astype(o_ref.dtype)
    nxt = jnp.where(rows == n1h - 1, 0.0, pltpu.roll(first, n1h - 1, 0))
    o_ref[0, (n2 - 1) * n1h:, :] = (w0 * p_ref[0, (n2 - 2) * n1h:(n2 - 1) * n1h, :].astype(F32) + w1 * last
                                    + w2 * nxt + bias).astype(o_ref.dtype)


def _shortconv(p, w, bias, n1h, n2, cb=512):
    b, t, c = p.shape
    return pl.pallas_call(
        functools.partial(_shortconv_kernel, n1h=n1h, n2=n2),
        out_shape=jax.ShapeDtypeStruct((b, t, c), BF16),
        grid=(b, c // cb),
        in_specs=[pl.BlockSpec((1, t, cb), lambda i, j: (i, 0, j)),
                  pl.BlockSpec((3, cb), lambda i, j: (0, j)),
                  pl.BlockSpec((1, cb), lambda i, j: (0, j))],
        out_specs=pl.BlockSpec((1, t, cb), lambda i, j: (i, 0, j)),
        compiler_params=_params(("parallel", "parallel")),
        name="hy_shortconv",
    )(p, w, bias.reshape(1, c))


def _dft_tables(t):
    n = 2 * t
    n1 = 1 << ((n.bit_length() - 1) // 2)
    n2 = n // n1
    n1h = n1 // 2
    two_pi = 2.0 * math.pi
    k1 = jnp.arange(n1, dtype=jnp.int32)
    n2i = jnp.arange(n2, dtype=jnp.int32)

    def cs(prod, mod):
        ang = (two_pi / mod) * (prod % mod).astype(F32)
        return jnp.cos(ang), jnp.sin(ang)

    tpos = n2i[:, None] + n2 * jnp.arange(n1h, dtype=jnp.int32)[None, :]
    c, s = cs(k1[None, :, None] * tpos[:, None, :], n)
    g1 = jnp.concatenate([jnp.concatenate([c, s], axis=2), jnp.concatenate([-s, c], axis=2)], axis=1)
    ct, st = jnp.swapaxes(c, 1, 2) / n, jnp.swapaxes(s, 1, 2) / n
    g4a = jnp.concatenate([ct, st], axis=1)
    g4b = jnp.concatenate([-st, ct], axis=1)
    tposf = n2i[:, None] + n2 * jnp.arange(n1, dtype=jnp.int32)[None, :]
    cf, sf = cs(k1[None, :, None] * tposf[:, None, :], n)
    g1f = jnp.concatenate([cf, -sf], axis=1)
    c2, s2 = cs(n2i[:, None] * n2i[None, :], n2)
    f2a = jnp.concatenate([c2, -s2], axis=0)
    f2b = jnp.concatenate([s2, c2], axis=0)
    f3 = jnp.concatenate([jnp.concatenate([c2, -s2], axis=1), jnp.concatenate([s2, c2], axis=1)], axis=0)
    g4 = jnp.concatenate([g4a, g4b], axis=2)
    f2 = jnp.concatenate([f2a, f2b], axis=1)
    return dict(n1=n1, n2=n2, n1h=n1h, g1=g1.astype(BF16), g4=g4.astype(BF16), g1f=g1f.astype(BF16),
                f2=f2.astype(BF16), f3=f3.astype(BF16))


def _filter_mlp_kernel(z_ref, w1_ref, b1_ref, w2_ref, b2_ref, w3_ref, b3_ref, fr_ref, wo_ref, dl_ref,
                       o_ref, *, t):
    z = z_ref[...]
    h = jnp.sin(fr_ref[0:1, :] * (_dot(z, w1_ref[...], HP) + b1_ref[...]))
    h = jnp.sin(fr_ref[1:2, :] * (_dot(h, w2_ref[...], HP) + b2_ref[...]))
    h = jnp.sin(fr_ref[2:3, :] * (_dot(h, w3_ref[...], HP) + b3_ref[...]))
    window = jnp.exp(-z[:, 0:1] * dl_ref[...])
    tr = z.shape[0]
    row = pl.program_id(0) * tr + lax.broadcasted_iota(jnp.int32, (tr, 1), 0)
    o_ref[...] = jnp.where(row == t, 0.0, _dot(h, wo_ref[0], HP) * window)


def _filter_taps(t, w1, b1, w2, b2, w3, b3, freq, w_out, c_hy, tr=512):
    emb, width = w1.shape
    bands = (emb - 1) // 2
    pos = jnp.arange(t, dtype=F32)
    tt = pos / max(t - 1, 1)
    fr = jnp.linspace(1e-4, bands - 1, bands, dtype=F32)
    ang = (2.0 * math.pi / t) * pos[:, None] * fr[None, :]
    z = jnp.concatenate([tt[:, None], jnp.cos(ang), -jnp.sin(ang)], axis=-1)
    z = jnp.pad(z, ((0, 0), (0, 128 - emb)))
    w1 = jnp.pad(w1, ((0, 128 - emb), (0, 0)))
    emb = 128
    z2 = jnp.concatenate([z, z[:1], z[1:][::-1]], axis=0)
    max_decay = math.log(HY_TARGET) / HY_FAST_PCT
    min_decay = math.log(HY_TARGET) / HY_SLOW_PCT
    delta = jnp.abs(jnp.linspace(min_decay, max_decay, c_hy, dtype=F32))
    order = w_out.shape[1] // (2 * c_hy)
    nout = order * c_hy
    w_dir = w_out.reshape(width, order, 2, c_hy).transpose(2, 0, 1, 3).reshape(2, width, nout)
    delta_full = jnp.tile(delta, order).reshape(1, nout)
    tr = min(tr, t)
    nt = t // tr
    full = lambda a: pl.BlockSpec(a.shape, lambda i: (0,) * a.ndim)
    args = [z2, w1, b1.reshape(1, -1), w2, b2.reshape(1, -1), w3, b3.reshape(1, -1), freq, w_dir, delta_full]
    in_specs = [pl.BlockSpec((tr, emb), lambda i: (i, 0))] + [full(a) for a in args[1:]]
    in_specs[8] = pl.BlockSpec((1, width, nout), lambda i: (i // nt, 0, 0))
    return pl.pallas_call(
        functools.partial(_filter_mlp_kernel, t=t),
        out_shape=jax.ShapeDtypeStruct((2 * t, nout), F32),
        grid=(2 * nt,),
        in_specs=in_specs,
        out_specs=pl.BlockSpec((tr, nout), lambda i: (i, 0)),
        compiler_params=_params(("parallel",)),
        name="hy_filter_mlp",
    )(*args)


def _filter_s1_kernel(x_ref, g_ref, o_ref, ss_ref, *, g, oc):
    @pl.when(pl.program_id(0) == 0)
    def _():
        ss_ref[...] = jnp.zeros_like(ss_ref)

    n1 = x_ref.shape[0]
    x = jnp.concatenate([x_ref[:, j, :] for j in range(g)], axis=0)
    ss_ref[...] += jnp.sum(x * x, axis=0, keepdims=True)
    xb = x.astype(BF16)
    for j in range(g):
        o_ref[j] = _dot(g_ref[j], xb[j * n1:(j + 1) * n1])


def _filter_s2_kernel(sr_ref, si_ref, f2_ref, ss_ref, o_ref, *, g, oc):
    n2 = sr_ref.shape[0]
    scale = lax.rsqrt(ss_ref[...] + 1e-6)
    s = jnp.concatenate([ref[:, j, :] for j in range(g) for ref in (sr_ref, si_ref)], axis=0).astype(BF16)
    for j in range(g):
        o_ref[j] = _dot(f2_ref[...], s[2 * j * n2:2 * (j + 1) * n2]) * scale


def _filter_spectrum(two, tabs):
    n, oc = two.shape
    n1, n2 = tabs["n1"], tabs["n2"]
    g = SUB
    s1, ss = pl.pallas_call(
        functools.partial(_filter_s1_kernel, g=g, oc=oc),
        out_shape=(jax.ShapeDtypeStruct((n2, 2 * n1, oc), F32), jax.ShapeDtypeStruct((1, oc), F32)),
        grid=(n2 // g,),
        in_specs=[pl.BlockSpec((n1, g, oc), lambda j: (0, j, 0)),
                  pl.BlockSpec((g, 2 * n1, n1), lambda j: (j, 0, 0))],
        out_specs=(pl.BlockSpec((g, 2 * n1, oc), lambda j: (j, 0, 0)),
                   pl.BlockSpec((1, oc), lambda j: (0, 0))),
        compiler_params=_params(("arbitrary",)),
        name="hy_filter_dft1",
    )(two.reshape(n1, n2, oc), tabs["g1f"])
    return pl.pallas_call(
        functools.partial(_filter_s2_kernel, g=g, oc=oc),
        out_shape=jax.ShapeDtypeStruct((n1, 2 * n2, oc), F32),
        grid=(n1 // g,),
        in_specs=[pl.BlockSpec((n2, g, oc), lambda k: (0, k, 0)),
                  pl.BlockSpec((n2, g, oc), lambda k: (0, n1 // g + k, 0)),
                  pl.BlockSpec((2 * n2, 2 * n2), lambda k: (0, 0)),
                  pl.BlockSpec((1, oc), lambda k: (0, 0))],
        out_specs=pl.BlockSpec((g, 2 * n2, oc), lambda k: (k, 0, 0)),
        compiler_params=_params(("parallel",)),
        name="hy_filter_dft2",
    )(s1, s1, tabs["f2"], ss)


def _conv_s1_kernel(u_ref, g_ref, o_ref, t_ref, *, g, n1h):
    for j0 in range(0, g, DFT_GROUP):
        js = range(j0, j0 + DFT_GROUP)
        rs = [_dot(g_ref[j], jnp.concatenate([u_ref[0, 0, j * n1h:(j + 1) * n1h, :],
                                               u_ref[0, 1, j * n1h:(j + 1) * n1h, :]], axis=0)) for j in js]
        for j, r in zip(js, rs):
            t_ref[:, j, :] = r
    o_ref[0] = t_ref[...].astype(o_ref.dtype)


def _conv_s23_kernel(sr_ref, si_ref, f2_ref, f3_ref, h_ref, o_ref, t_ref, *, g, n2):
    for j0 in range(0, g, DFT_GROUP):
        js = range(j0, j0 + DFT_GROUP)
        xs = [_dot(f2_ref[...], jnp.concatenate([sr_ref[0, j], si_ref[0, j]], axis=0)) for j in js]
        ys = []
        for j, x in zip(js, xs):
            xr, xi = x[:n2], x[n2:]
            hr, hi = h_ref[j, :n2, :], h_ref[j, n2:, :]
            ys.append(jnp.concatenate([xr * hr - xi * hi, xr * hi + xi * hr], axis=0).astype(BF16))
        rs = [_dot(f3_ref[...], y) for y in ys]
        for j, r in zip(js, rs):
            t_ref[:, j, :] = r
    o_ref[0] = t_ref[...].astype(o_ref.dtype)


def _conv_s4_kernel(rr_ref, ri_ref, g4_ref, u_ref, gate_ref, skip_ref, *rest, g, n1h, norm):
    if norm:
        ng_ref, avg_ref, o_ref = rest
    else:
        (o_ref,) = rest
    skip = skip_ref[...]
    ys = [_dot(g4_ref[j], jnp.concatenate([rr_ref[0, j], ri_ref[0, j]], axis=0)) for j in range(g)]
    zs = [gate_ref[0, q, j * n1h:(j + 1) * n1h, :].astype(F32)
          * (ys[j][q * n1h:(q + 1) * n1h] + u_ref[0, q, j * n1h:(j + 1) * n1h, :].astype(F32) * skip)
          for j in range(g) for q in range(2)]
    if norm:
        ms = [_dot((z * z).astype(BF16), avg_ref[...]) for z in zs]
        zs = [z * lax.rsqrt(m + RMS_EPS) * ng_ref[...] for z, m in zip(zs, ms)]
    for j in range(g):
        for q in range(2):
            if norm:
                o_ref[0, q, :, j, :] = zs[2 * j + q]
            else:
                o_ref[0, q, j * n1h:(j + 1) * n1h, :] = zs[2 * j + q].astype(o_ref.dtype)


def _long_conv_gate(u_arr, u_col, gate_arr, gate_col, h_spec, h_col, skip, tabs, c, norm_g=None):
    b, t, _ = u_arr.shape
    n1, n2, n1h = tabs["n1"], tabs["n2"], tabs["n1h"]
    g = SUB
    npair = b // 2
    u4 = u_arr.reshape(npair, 2, t, u_arr.shape[2])
    gate4 = gate_arr.reshape(npair, 2, t, gate_arr.shape[2])
    gt = SUB16
    s1 = pl.pallas_call(
        functools.partial(_conv_s1_kernel, g=gt, n1h=n1h),
        out_shape=jax.ShapeDtypeStruct((npair, 2 * n1, n2, c), BF16),
        grid=(n2 // gt, npair),
        in_specs=[pl.BlockSpec((1, 2, gt * n1h, c), lambda j, p: (p, 0, j, u_col)),
                  pl.BlockSpec((gt, 2 * n1, n1), lambda j, p: (j, 0, 0))],
        out_specs=pl.BlockSpec((1, 2 * n1, gt, c), lambda j, p: (p, 0, j, 0)),
        scratch_shapes=[pltpu.VMEM((2 * n1, gt, c), F32)],
        compiler_params=_params(("parallel", "parallel")),
        name="hy_conv_dft1",
    )(u4, tabs["g1"])
    r = pl.pallas_call(
        functools.partial(_conv_s23_kernel, g=gt, n2=n2),
        out_shape=jax.ShapeDtypeStruct((npair, 2 * n2, n1, c), BF16),
        grid=(n1 // gt, npair),
        in_specs=[pl.BlockSpec((1, gt, n2, c), lambda k, p: (p, k, 0, 0)),
                  pl.BlockSpec((1, gt, n2, c), lambda k, p: (p, n1 // gt + k, 0, 0)),
                  pl.BlockSpec((2 * n2, 2 * n2), lambda k, p: (0, 0)),
                  pl.BlockSpec((2 * n2, 2 * n2), lambda k, p: (0, 0)),
                  pl.BlockSpec((gt, 2 * n2, c), lambda k, p: (k, 0, h_col))],
        out_specs=pl.BlockSpec((1, 2 * n2, gt, c), lambda k, p: (p, 0, k, 0)),
        scratch_shapes=[pltpu.VMEM((2 * n2, gt, c), F32)],
        compiler_params=_params(("parallel", "parallel")),
        name="hy_conv_dft23",
    )(s1, s1, tabs["f2"], tabs["f3"], h_spec)
    norm = norm_g is not None
    in_specs = [pl.BlockSpec((1, g, n1, c), lambda p, j: (p, j, 0, 0)),
                pl.BlockSpec((1, g, n1, c), lambda p, j: (p, n2 // g + j, 0, 0)),
                pl.BlockSpec((g, n1, 2 * n1), lambda p, j: (j, 0, 0)),
                pl.BlockSpec((1, 2, g * n1h, c), lambda p, j: (p, 0, j, u_col)),
                pl.BlockSpec((1, 2, g * n1h, c), lambda p, j: (p, 0, j, gate_col)),
                pl.BlockSpec((1, c), lambda p, j: (0, 0))]
    args = [r, r, tabs["g4"], u4, gate4, skip.reshape(1, c)]
    if norm:
        ch = jnp.arange(c) // HEAD
        avg = ((ch[:, None] == ch[None, :]).astype(F32) / HEAD).astype(BF16)
        in_specs += [pl.BlockSpec((1, c), lambda p, j: (0, 0)), pl.BlockSpec((c, c), lambda p, j: (0, 0))]
        args += [norm_g.reshape(1, c), avg]
        out_shape = jax.ShapeDtypeStruct((npair, 2, n1h, n2, c), F32)
        out_spec = pl.BlockSpec((1, 2, n1h, g, c), lambda p, j: (p, 0, 0, j, 0))
    else:
        out_shape = jax.ShapeDtypeStruct((npair, 2, t, c), BF16)
        out_spec = pl.BlockSpec((1, 2, g * n1h, c), lambda p, j: (p, 0, j, 0))
    out = pl.pallas_call(
        functools.partial(_conv_s4_kernel, g=g, n1h=n1h, norm=norm),
        out_shape=out_shape,
        grid=(npair, n2 // g),
        in_specs=in_specs,
        out_specs=out_spec,
        compiler_params=_params(("parallel", "parallel")),
        name="hy_conv_dft4",
    )(*args)
    return out.reshape(b, t, c)


def _hyena_branch(p, conv_w, conv_b, fw1, fb1, fw2, fb2, fw3, fb3, ffreq, fwout, skip, norm_g, tabs):
    b, t, _ = p.shape
    c = norm_g.shape[0]
    n1, n2, n1h = tabs["n1"], tabs["n2"], tabs["n1h"]
    u = _shortconv(p, conv_w, conv_b, n1h, n2)
    two = _filter_taps(t, fw1, fb1, fw2, fb2, fw3, fb3, ffreq, fwout, c)
    hspec = _filter_spectrum(two, tabs)
    z = _long_conv_gate(u, 2, u, 0, hspec, 0, skip[0], tabs, c)
    return _long_conv_gate(z, 0, u, 1, hspec, 1, skip[1], tabs, c, norm_g=norm_g)


def _rw_prep_kernel(*refs, has_vres, c):
    if has_vres:
        (p_ref, pp_ref, pn_ref, mu_ref, w0_ref, w2_ref, a0_ref, a2_ref, g2_ref, kk_ref, ka_ref, rk_ref,
         sum_ref, vf_ref, v0_ref, v2_ref,
         r_o, v_o, kk_o, lwf_o, lwb_o, kf_o, kb_o, af_o, ab_o, g_o, bon_o) = refs
    else:
        (p_ref, pp_ref, pn_ref, mu_ref, w0_ref, w2_ref, a0_ref, a2_ref, g2_ref, kk_ref, ka_ref, rk_ref,
         sum_ref,
         r_o, v_o, kk_o, lwf_o, lwb_o, kf_o, kb_o, af_o, ab_o, g_o, bon_o) = refs
    i = pl.program_id(1)
    last = pl.num_programs(1) - 1
    nsh = mu_ref.shape[1]
    p = p_ref[0, :, :nsh].astype(F32)
    tt = p.shape[0]
    rows = lax.broadcasted_iota(jnp.int32, (tt, 1), 0)
    prev_row = jnp.where(i == 0, 0.0, pp_ref[0, SUB16 - 1:SUB16, :nsh].astype(F32))
    next_row = jnp.where(i == last, 0.0, pn_ref[0, 0:1, :nsh].astype(F32))
    prev = jnp.where(rows == 0, prev_row, pltpu.roll(p, 1, 0))
    nxt = jnp.where(rows == tt - 1, next_row, pltpu.roll(p, tt - 1, 0))
    mu0, mu1 = mu_ref[0:1, :], mu_ref[1:2, :]
    pf = p * (1.0 - mu0 - mu1) + mu0 * prev + mu1 * nxt
    r = pf[:, :c]
    k = pf[:, c:2 * c]
    v = pf[:, 2 * c:3 * c]
    lw = 3 * c
    nd = w2_ref.shape[0]
    na = a2_ref.shape[0]
    wd = pf[:, lw:lw + nd]
    ad = pf[:, lw + nd:lw + nd + na]
    gd = pf[:, lw + nd + na:]
    if has_vres:
        lora = _dot(p_ref[0, :, nsh:], v2_ref[...])
        v = v + (vf_ref[0].astype(F32) - v) * jax.nn.sigmoid(v0_ref[...] + lora)
    g = _dot(jax.nn.sigmoid(gd).astype(BF16), g2_ref[...])
    kk = k * kk_ref[...]
    kk = kk * lax.rsqrt(jnp.maximum(_dot((kk * kk).astype(BF16), sum_ref[...]), 1e-24))
    wl = w0_ref[...] + _dot(jnp.tanh(wd).astype(BF16), w2_ref[...])
    logw = -math.exp(-0.5) * jax.nn.sigmoid(wl)
    a = jax.nn.sigmoid(a0_ref[...] + _dot(ad.astype(BF16), a2_ref[...]))
    ka = ka_ref[...]
    rk = rk_ref[...]
    k_d = [k * (1.0 + (a[:, d * c:(d + 1) * c] - 1.0) * ka) for d in range(2)]
    bon = _dot((r * (k_d[0] + k_d[1]) * rk).astype(BF16), sum_ref[...]) * v
    r_o[0] = r.astype(BF16)
    v_o[0] = v.astype(BF16)
    kk_o[0] = kk.astype(BF16)
    lwf_o[0] = logw[:, :c]
    lwb_o[0] = logw[:, c:]
    kf_o[0] = k_d[0].astype(BF16)
    kb_o[0] = k_d[1].astype(BF16)
    af_o[0] = a[:, :c].astype(BF16)
    ab_o[0] = a[:, c:].astype(BF16)
    g_o[0] = g.astype(BF16)
    bon_o[0] = bon.astype(BF16)


def _blockdiag2(m):
    k, c = m.shape[1], m.shape[2]
    z = jnp.zeros((k, c), m.dtype)
    return jnp.concatenate([jnp.concatenate([m[0], z], axis=1), jnp.concatenate([z, m[1]], axis=1)], axis=0)


def _rw_prep(p, v_first, shift, w0, w2, a0, a2, g2, k_k, k_a, r_k, vres, c, tt=512):
    b, t, pw = p.shape
    tt = min(tt, t)
    has_vres = vres is not None
    ch = jnp.arange(c) // HEAD
    summ = (ch[:, None] == ch[None, :]).astype(BF16)
    row = lambda a: a.reshape(1, -1)
    args = [p, p, p, shift, row(w0), _blockdiag2(w2).astype(BF16), row(a0), _blockdiag2(a2).astype(BF16),
            g2.astype(BF16), row(k_k), row(k_a), row(r_k), summ]
    full = lambda a: pl.BlockSpec(a.shape, lambda i, j: (0,) * a.ndim)
    nhb = t // SUB16
    in_specs = [pl.BlockSpec((1, tt, pw), lambda i, j: (i, j, 0)),
                pl.BlockSpec((1, SUB16, pw), lambda i, j: (i, jnp.maximum(j * (tt // SUB16) - 1, 0), 0)),
                pl.BlockSpec((1, SUB16, pw), lambda i, j: (i, jnp.minimum((j + 1) * (tt // SUB16), nhb - 1), 0))]
    in_specs += [full(a) for a in args[3:]]
    if has_vres:
        v0, _, v2 = vres
        v2p = jnp.pad(v2, ((0, pw - shift.shape[1] - v2.shape[0]), (0, 0))).astype(BF16)
        extra = [v_first, row(v0), v2p]
        in_specs += [pl.BlockSpec((1, tt, c), lambda i, j: (i, j, 0))] + [full(a) for a in extra[1:]]
        args += extra
    outs = tuple(jax.ShapeDtypeStruct((b, t, c), F32 if i in (3, 4) else BF16) for i in range(11))
    return pl.pallas_call(
        functools.partial(_rw_prep_kernel, has_vres=has_vres, c=c),
        out_shape=outs,
        grid=(b, t // tt),
        in_specs=in_specs,
        out_specs=tuple(pl.BlockSpec((1, tt, c), lambda i, j: (i, j, 0)) for _ in range(11)),
        compiler_params=_params(("parallel", "parallel")),
        name="rw_prep",
    )(*args)


def _scan_chunks(s0, r, k, v, kk, a, lw, rev, bdm):
    L = CHUNK
    n = len(r)
    each = lambda f, *ls: [f(*xs) for xs in zip(*ls)]
    ti = lax.broadcasted_iota(jnp.int32, (L, L), 0)
    si = lax.broadcasted_iota(jnp.int32, (L, L), 1)
    tt = lax.broadcasted_iota(jnp.int32, (L, GW), 0)
    ss = lax.broadcasted_iota(jnp.int32, (L, GW), 1) % L
    eye = (ss == tt).astype(F32)
    tri = [((si >= ti) if q else (si <= ti)).astype(F32).astype(BF16) for q in rev]
    strict = [(ss > tt) if q else (ss < tt) for q in rev]
    incl = [(ss >= tt) if q else (ss <= tt) for q in rev]
    bdf = bdm.astype(F32)

    def bd(x):
        xb = x.astype(BF16)
        return jnp.concatenate([xb] * HEADS_PER_GROUP, axis=0) * bdm

    def bd_t(x):
        xt = x.T.astype(BF16)
        return jnp.concatenate([xt] * HEADS_PER_GROUP, axis=1) * bdm

    def rcmul(x, y):
        return _dot(x.astype(BF16), bd(y))

    parts = each(_split3, lw)
    cum = [_dot(t, p[0]) + _dot(t, p[1]) + _dot(t, p[2]) for t, p in zip(tri, parts)]
    tot = [c[0:1, :] if q else c[L - 1:L, :] for c, q in zip(cum, rev)]
    p_in = each(jnp.exp, cum)
    p_inv = each(lambda c: jnp.exp(-c), cum)
    p_ex = each(lambda c, w: jnp.exp(c - w), cum, lw)
    p_rem = each(lambda t, c: jnp.exp(t - c), tot, cum)
    beta = each(lambda x, y: x * y, kk, a)
    a_t = each(lambda x, p: -x * p, kk, p_ex)
    r_t = each(lambda x, p: x * p, r, p_in)
    k_t = each(lambda x, p: x * p, k, p_inv)
    b_t = each(lambda x, p: x * p, beta, p_inv)
    k_h = each(lambda x, p: x * p, k, p_rem)
    b_h = each(lambda x, p: x * p, beta, p_rem)

    lhs = each(lambda x, y: jnp.concatenate([x, y], axis=0).astype(BF16), a_t, r_t)
    akk = each(lambda x, y: _dot(x, bd_t(y)), lhs, k_t)
    abb = each(lambda x, y: _dot(x, bd_t(y)), lhs, b_t)
    a_ak = each(lambda m, x: jnp.where(m, x[:L], 0.0), strict, akk)
    a_rk = each(lambda m, x: jnp.where(m, x[L:], 0.0), incl, akk)
    a_ab = each(lambda m, x: jnp.where(m, x[:L], 0.0), strict, abb)
    a_rb = each(lambda m, x: jnp.where(m, x[L:], 0.0), incl, abb)

    same = lambda size: (tt // size) == (ss // size)
    d = each(lambda x: jnp.where(same(INV_BASE), x, 0.0), a_ab)
    tinv = each(lambda x: eye + x, d)
    pw = each(rcmul, d, d)
    size = 4
    while size < INV_BASE:
        both = each(lambda p, t: rcmul(jnp.concatenate([p, t], axis=0), p), pw, tinv)
        pw = [x[:L] for x in both]
        tinv = each(lambda t, x: t + x[L:], tinv, both)
        size *= 2
    tinv = each(lambda t, p: t + rcmul(t, p), tinv, pw)
    size = INV_BASE
    while size < L:
        size *= 2
        couple = same(size) & ~same(size // 2)
        e = each(lambda x: jnp.where(couple, x, 0.0), a_ab)
        tinv = each(lambda t, y: t + rcmul(rcmul(t, y), t), tinv, e)

    av = each(lambda x, y, z: rcmul(jnp.concatenate([x, y], axis=0), z), a_ak, a_rk, v)
    ct = each(rcmul, a_rb, tinv)
    wu2 = each(lambda t, c, x, y: _dot(jnp.concatenate([t, c], axis=0).astype(BF16),
                                       jnp.concatenate([bd(x), bd(y[:L])], axis=1)), tinv, ct, a_t, av)
    wu = [x[:L] for x in wu2]
    rbwu = [x[L:] for x in wu2]
    o = each(lambda x, w, s, y: _dot((x + w[:, :GW]).astype(BF16), bd_t(s)) + y[L:] + w[:, GW:],
             r_t, rbwu, s0, av)

    m_bd = each(lambda w, x: _dot_tn(w[:, :GW].astype(BF16), x.astype(BF16)) * bdf, wu, b_h)
    n_full = each(lambda x, w, y, z: _dot_tn(jnp.concatenate([x, w[:, GW:]], axis=0).astype(BF16),
                                             jnp.concatenate([y, z], axis=0).astype(BF16)) * bdf,
                  v, wu, k_h, b_h)
    s1 = []
    for i in range(n):
        n_rc = n_full[i][0:HEAD]
        for h in range(1, HEADS_PER_GROUP):
            n_rc = n_rc + n_full[i][h * HEAD:(h + 1) * HEAD]
        s1.append(_dot(s0[i].astype(BF16), m_bd[i].astype(BF16)) + s0[i] * jnp.exp(tot[i]) + n_rc)
    return s1, o


def _scan_kernel(rf_ref, kf_ref, vf_ref, kkf_ref, af_ref, lwf_ref,
                 rb_ref, kb_ref, vb_ref, kkb_ref, ab_ref, lwb_ref, of_ref, ob_ref, s_ref, *, ng, nb):
    @pl.when(pl.program_id(1) == 0)
    def _():
        s_ref[...] = jnp.zeros_like(s_ref)

    gi = lax.broadcasted_iota(jnp.int32, (GW, GW), 0) // HEAD
    gj = lax.broadcasted_iota(jnp.int32, (GW, GW), 1) // HEAD
    bdm = (gi == gj).astype(F32).astype(BF16)
    dirs = ((rf_ref, kf_ref, vf_ref, kkf_ref, af_ref, lwf_ref), (rb_ref, kb_ref, vb_ref, kkb_ref, ab_ref, lwb_ref))
    chains = [(i, d, h) for i in range(nb) for d in range(2) for h in range(ng)]
    ins = [[dirs[d][q][i, :, h * GW:(h + 1) * GW].astype(F32) for i, d, h in chains] for q in range(6)]
    s1, o = _scan_chunks([s_ref[i, d, h] for i, d, h in chains], *ins, [d == 1 for _, d, _ in chains], bdm)
    for (i, d, h), s_new, o_new in zip(chains, s1, o):
        (of_ref, ob_ref)[d][i, :, h * GW:(h + 1) * GW] = o_new
        s_ref[i, d, h] = s_new


SCAN_BATCH_ROWS = 4


def _wkv_scan(r, v, kk, kf, af, lwf, kb, ab, lwb):
    b, t, c = r.shape
    nc = t // CHUNK
    ng = c // GW
    nb = math.gcd(b, SCAN_BATCH_ROWS)
    fspec = pl.BlockSpec((nb, CHUNK, c), lambda i, j: (i, j, 0))
    bspec = pl.BlockSpec((nb, CHUNK, c), lambda i, j: (i, nc - 1 - j, 0))
    return pl.pallas_call(
        functools.partial(_scan_kernel, ng=ng, nb=nb),
        out_shape=(jax.ShapeDtypeStruct((b, t, c), F32), jax.ShapeDtypeStruct((b, t, c), F32)),
        grid=(b // nb, nc),
        in_specs=[fspec] * 6 + [bspec] * 6,
        out_specs=(fspec, bspec),
        scratch_shapes=[pltpu.VMEM((nb, 2, ng, HEAD, GW), F32)],
        compiler_params=_params(("parallel", "arbitrary")),
        name="rw_scan",
    )(r, kf, v, kk, af, lwf, r, kb, v, kk, ab, lwb)


def _out_proj_kernel(yh_ref, sf_ref, sb_ref, bon_ref, g_ref, lg_ref, lb_ref, avg_ref, wh_ref, wr_ref, r_ref,
                     o_ref):
    s = sf_ref[...] + sb_ref[...]
    mean = _dot3(s, avg_ref[...])
    d = s - mean
    var = _dot((d * d).astype(BF16), avg_ref[...])
    y = d * lax.rsqrt(var + GN_EPS) * lg_ref[...] + lb_ref[...] + bon_ref[...].astype(F32)
    yr = (y * g_ref[...].astype(F32)).astype(BF16)
    o_ref[...] = _dot(yh_ref[...].astype(BF16), wh_ref[...]) + _dot(yr, wr_ref[...]) + r_ref[...]


def _out_proj(yh, sf, sb, bon, g, lnx_g, lnx_b, w, res, tm=1024):
    m, ch = yh.shape
    cr = sf.shape[1]
    d = w.shape[1]
    tm = min(tm, m)
    head = jnp.arange(cr) // HEAD
    avg = ((head[:, None] == head[None, :]).astype(F32) / HEAD).astype(BF16)
    row = lambda c: pl.BlockSpec((tm, c), lambda i: (i, 0))
    return pl.pallas_call(
        _out_proj_kernel,
        out_shape=jax.ShapeDtypeStruct((m, d), F32),
        grid=(m // tm,),
        in_specs=[row(ch), row(cr), row(cr), row(cr), row(cr), _const_spec((1, cr)), _const_spec((1, cr)),
                  _const_spec((cr, cr)), _const_spec((ch, d)), _const_spec((cr, d)), row(d)],
        out_specs=row(d),
        compiler_params=_params(("parallel",)),
        name="out_proj",
    )(yh, sf, sb, bon, g, lnx_g.reshape(1, cr), lnx_b.reshape(1, cr), avg, w[:ch], w[ch:], res)


def _rwkv_branch(p, v_first, shift, w0, w2, a0, a2, g2, k_k, k_a, r_k, vres, c):
    r, v, kk, lwf, lwb, kf, kb, af, ab, g, bon = _rw_prep(
        p, v_first, shift, w0, w2, a0, a2, g2, k_k, k_a, r_k, vres, c)
    sf, sb = _wkv_scan(r, v, kk, kf, af, lwf, kb, ab, lwb)
    return sf, sb, bon, g, (v if vres is None else v_first)


def _forward(x, norm1_g, w_in, hy_conv_w, hy_conv_b, hy_f_w1, hy_f_b1, hy_f_w2, hy_f_b2, hy_f_w3, hy_f_b3,
             hy_f_freq, hy_f_wout, hy_skip, hy_norm_g, rw_shift, rw_w0, rw_w2, rw_a0, rw_a2, rw_g2, rw_k_k,
             rw_k_a, rw_r_k, rw_lnx_g, rw_lnx_b, rw_v0, rw_v1, rw_v2, w_out, norm2_g, mlp_w1, mlp_w2,
             final_g):
    b, t, d = x.shape
    depth = w_in.shape[0]
    c_hy = hy_norm_g.shape[1]
    c_rw = rw_lnx_g.shape[1]
    hy_proj = hy_conv_b.shape[1]
    assert b % 2 == 0 and t % CHUNK == 0 and CHUNK == HEAD
    tabs = _dft_tables(t)
    assert tabs["n2"] % SUB16 == 0 and tabs["n1"] % SUB16 == 0 and tabs["n1h"] % SUB == 0
    v_first = None
    for l in range(depth):
        w_l = w_in[l].astype(BF16)
        w_rw = w_l[:, hy_proj:]
        vres = None if l == 0 else (rw_v0[l - 1], rw_v1[l - 1], rw_v2[l - 1])
        if vres is not None:
            v1 = vres[1].astype(BF16)
            w_rw = jnp.concatenate([w_rw, jnp.pad(v1, ((0, 0), (0, -v1.shape[1] % 128)))], axis=1)
        p_hy, p_rw = _in_proj(x, norm1_g[l], w_l[:, :hy_proj], w_rw, tabs["n1h"], tabs["n2"])
        y_hy = _hyena_branch(p_hy, hy_conv_w[l], hy_conv_b[l], hy_f_w1[l], hy_f_b1[l], hy_f_w2[l], hy_f_b2[l],
                             hy_f_w3[l], hy_f_b3[l], hy_f_freq[l], hy_f_wout[l], hy_skip[l], hy_norm_g[l], tabs)
        sf, sb, bon, g, v_first = _rwkv_branch(p_rw, v_first, rw_shift[l], rw_w0[l], rw_w2[l], rw_a0[l], rw_a2[l],
                                               rw_g2[l], rw_k_k[l], rw_k_a[l], rw_r_k[l], vres, c_rw)
        flat = lambda a: a.reshape(b * t, a.shape[-1])
        x = _out_proj(flat(y_hy), flat(sf), flat(sb), flat(bon), flat(g), rw_lnx_g[l], rw_lnx_b[l],
                      w_out[l].astype(BF16), flat(x))
        x = _mlp(x, norm2_g[l], mlp_w1[l].astype(BF16), mlp_w2[l].astype(BF16),
                 final_g=final_g if l == depth - 1 else None).reshape(b, t, d)
    return x


def kernel(x, norm1_g, w_in, hy_conv_w, hy_conv_b, hy_f_w1, hy_f_b1, hy_f_w2, hy_f_b2, hy_f_w3, hy_f_b3,
           hy_f_freq, hy_f_wout, hy_skip, hy_norm_g, rw_shift, rw_w0, rw_w2, rw_a0, rw_a2, rw_g2, rw_k_k,
           rw_k_a, rw_r_k, rw_lnx_g, rw_lnx_b, rw_v0, rw_v1, rw_v2, w_out, norm2_g, mlp_w1, mlp_w2, final_g):
    return _forward(x, norm1_g, w_in, hy_conv_w, hy_conv_b, hy_f_w1, hy_f_b1, hy_f_w2, hy_f_b2, hy_f_w3,
                    hy_f_b3, hy_f_freq, hy_f_wout, hy_skip, hy_norm_g, rw_shift, rw_w0, rw_w2, rw_a0, rw_a2,
                    rw_g2, rw_k_k, rw_k_a, rw_r_k, rw_lnx_g, rw_lnx_b, rw_v0, rw_v1, rw_v2, w_out, norm2_g,
                    mlp_w1, mlp_w2, final_g)
```

```python
import functools
import math

import jax
import jax.numpy as jnp
from jax import lax
from jax.experimental import pallas as pl
from jax.experimental.pallas import tpu as pltpu

F32 = jnp.float32
BF16 = jnp.bfloat16
HP = lax.Precision.HIGHEST

HEAD = 64
HEADS_PER_GROUP = 4
GW = HEAD * HEADS_PER_GROUP
CHUNK = 64
INV_BASE = 8
RMS_EPS = 1e-5
GN_EPS = HEAD * 1e-5
HY_TARGET = 1e-2
HY_FAST_PCT = 0.3
HY_SLOW_PCT = 1.5
VMEM_LIMIT = 56 * 1024 * 1024


def _params(sem, vmem=VMEM_LIMIT):
    return pltpu.CompilerParams(dimension_semantics=sem, vmem_limit_bytes=vmem)


def _dot(a, b, precision=None):
    return jnp.dot(a, b, preferred_element_type=F32, precision=precision)


def _dot_tn(a, b, precision=None):
    return lax.dot_general(a, b, (((0,), (0,)), ((), ())), preferred_element_type=F32,
                           precision=precision)


def _const_spec(shape):
    return pl.BlockSpec(shape, lambda *_: (0,) * len(shape), pipeline_mode=pl.Buffered(1))


def _split3(x):
    h1 = x.astype(BF16)
    r1 = x - h1.astype(F32)
    h2 = r1.astype(BF16)
    h3 = (r1 - h2.astype(F32)).astype(BF16)
    return h1, h2, h3


def _dot3(x, m):
    h1, h2, h3 = _split3(x)
    return _dot(h1, m) + _dot(h2, m) + _dot(h3, m)


def _mlp_kernel(x_ref, g_ref, w1_ref, w2_ref, *rest, nchunk):
    x = x_ref[...]
    hn = (x * lax.rsqrt(jnp.mean(x * x, axis=-1, keepdims=True) + RMS_EPS) * g_ref[...]).astype(BF16)
    ff = w1_ref.shape[1]
    cw = ff // nchunk
    acc = x
    for j in range(nchunk):
        h = _dot(hn, w1_ref[:, j * cw:(j + 1) * cw])
        h = jnp.square(jnp.maximum(h, 0.0)).astype(BF16)
        acc = acc + _dot(h, w2_ref[j * cw:(j + 1) * cw, :])
    if len(rest) == 2:
        gf_ref, o_ref = rest
        acc = acc * lax.rsqrt(jnp.mean(acc * acc, axis=-1, keepdims=True) + RMS_EPS) * gf_ref[...]
    else:
        (o_ref,) = rest
    o_ref[...] = acc


def _mlp(x2d, g, w1, w2, final_g=None, tm=1024, nchunk=4):
    m, d = x2d.shape
    ff = w1.shape[1]
    tm = min(tm, m)
    in_specs = [pl.BlockSpec((tm, d), lambda i: (i, 0)), _const_spec((1, d)),
                _const_spec((d, ff)), _const_spec((ff, d))]
    args = [x2d, g.reshape(1, d), w1, w2]
    if final_g is not None:
        in_specs.append(_const_spec((1, d)))
        args.append(final_g.reshape(1, d))
    return pl.pallas_call(
        functools.partial(_mlp_kernel, nchunk=nchunk),
        out_shape=jax.ShapeDtypeStruct((m, d), F32),
        grid=(m // tm,),
        in_specs=in_specs,
        out_specs=pl.BlockSpec((tm, d), lambda i: (i, 0)),
        compiler_params=_params(("parallel",)),
        name="mlp",
    )(*args)


SUB = 8
SUB16 = 16
DFT_GROUP = 8


def _rms(x, g):
    return (x * lax.rsqrt(jnp.mean(x * x, axis=-1, keepdims=True) + RMS_EPS) * g).astype(BF16)


def _in_proj_kernel(x_ref, g_ref, wh_ref, wr_ref, oh_ref, or_ref, *, n2):
    hn = _rms(x_ref[0], g_ref[...])
    or_ref[0] = _dot(hn, wr_ref[...]).astype(or_ref.dtype)
    ph = _dot(hn, wh_ref[...])
    for j in range(SUB):
        oh_ref[0, :, j, :] = ph[j * n2:(j + 1) * n2]


def _in_proj(x, g, w_hy, w_rw, n1h, n2):
    b, t, d = x.shape
    ph, pr = w_hy.shape[1], w_rw.shape[1]
    tm = SUB * n2
    p_hy, p_rw = pl.pallas_call(
        functools.partial(_in_proj_kernel, n2=n2),
        out_shape=(jax.ShapeDtypeStruct((b, n2, n1h, ph), F32), jax.ShapeDtypeStruct((b, t, pr), BF16)),
        grid=(b, t // tm),
        in_specs=[pl.BlockSpec((1, tm, d), lambda i, j: (i, j, 0)), _const_spec((1, d)),
                  _const_spec((d, ph)), _const_spec((d, pr))],
        out_specs=(pl.BlockSpec((1, n2, SUB, ph), lambda i, j: (i, 0, j, 0)),
                   pl.BlockSpec((1, tm, pr), lambda i, j: (i, j, 0))),
        compiler_params=_params(("parallel", "parallel")),
        name="in_proj",
    )(x, g.reshape(1, d), w_hy, w_rw)
    return p_hy.reshape(b, t, ph), p_rw


def _shortconv_kernel(p_ref, w_ref, b_ref, o_ref, *, n1h, n2):
    w0 = w_ref[0:1, :]
    w1 = w_ref[1:2, :]
    w2 = w_ref[2:3, :]
    bias = b_ref[...]
    rows = lax.broadcasted_iota(jnp.int32, (n1h, 1), 0)

    def blk(i):
        return p_ref[0, pl.ds(pl.multiple_of(i * n1h, n1h), n1h), :].astype(F32)

    def body(i, carry):
        o_ref[0, pl.ds(pl.multiple_of(i * n1h, n1h), n1h), :] = (
            w0 * blk(i - 1) + w1 * blk(i) + w2 * blk(i + 1) + bias).astype(o_ref.dtype)
        return carry

    lax.fori_loop(1, n2 - 1, body, 0)
    last = p_ref[0, (n2 - 1) * n1h:, :].astype(F32)
    first = p_ref[0, :n1h, :].astype(F32)
    prev0 = jnp.where(rows == 0, 0.0, pltpu.roll(last, 1, 0))
    o_ref[0, :n1h, :] = (w0 * prev0 + w1 * first + w2 * p_ref[0, n1h:2 * n1h, :].astype(F32)
                         + bias).astype(o_ref.dtype)
    nxt = jnp.where(rows == n1h - 1, 0.0, pltpu.roll(first, n1h - 1, 0))
    o_ref[0, (n2 - 1) * n1h:, :] = (w0 * p_ref[0, (n2 - 2) * n1h:(n2 - 1) * n1h, :].astype(F32) + w1 * last
                                    + w2 * nxt + bias).astype(o_ref.dtype)


def _shortconv(p, w, bias, n1h, n2, cb=512):
    b, t, c = p.shape
    return pl.pallas_call(
        functools.partial(_shortconv_kernel, n1h=n1h, n2=n2),
        out_shape=jax.ShapeDtypeStruct((b, t, c), BF16),
        grid=(b, c // cb),
        in_specs=[pl.BlockSpec((1, t, cb), lambda i, j: (i, 0, j)),
                  pl.BlockSpec((3, cb), lambda i, j: (0, j)),
                  pl.BlockSpec((1, cb), lambda i, j: (0, j))],
        out_specs=pl.BlockSpec((1, t, cb), lambda i, j: (i, 0, j)),
        compiler_params=_params(("parallel", "parallel")),
        name="hy_shortconv",
    )(p, w, bias.reshape(1, c))


def _dft_tables(t):
    n = 2 * t
    n1 = 1 << ((n.bit_length() - 1) // 2)
    n2 = n // n1
    n1h = n1 // 2
    two_pi = 2.0 * math.pi
    k1 = jnp.arange(n1, dtype=jnp.int32)
    n2i = jnp.arange(n2, dtype=jnp.int32)

    def cs(prod, mod):
        ang = (two_pi / mod) * (prod % mod).astype(F32)
        return jnp.cos(ang), jnp.sin(ang)

    tpos = n2i[:, None] + n2 * jnp.arange(n1h, dtype=jnp.int32)[None, :]
    c, s = cs(k1[None, :, None] * tpos[:, None, :], n)
    g1 = jnp.concatenate([jnp.concatenate([c, s], axis=2), jnp.concatenate([-s, c], axis=2)], axis=1)
    ct, st = jnp.swapaxes(c, 1, 2) / n, jnp.swapaxes(s, 1, 2) / n
    g4a = jnp.concatenate([ct, st], axis=1)
    g4b = jnp.concatenate([-st, ct], axis=1)
    tposf = n2i[:, None] + n2 * jnp.arange(n1, dtype=jnp.int32)[None, :]
    cf, sf = cs(k1[None, :, None] * tposf[:, None, :], n)
    g1f = jnp.concatenate([cf, -sf], axis=1)
    c2, s2 = cs(n2i[:, None] * n2i[None, :], n2)
    f2a = jnp.concatenate([c2, -s2], axis=0)
    f2b = jnp.concatenate([s2, c2], axis=0)
    f3 = jnp.concatenate([jnp.concatenate([c2, -s2], axis=1), jnp.concatenate([s2, c2], axis=1)], axis=0)
    g4 = jnp.concatenate([g4a, g4b], axis=2)
    f2 = jnp.concatenate([f2a, f2b], axis=1)
    return dict(n1=n1, n2=n2, n1h=n1h, g1=g1.astype(BF16), g4=g4.astype(BF16), g1f=g1f.astype(BF16),
                f2=f2.astype(BF16), f3=f3.astype(BF16))


def _filter_mlp_kernel(z_ref, w1_ref, b1_ref, w2_ref, b2_ref, w3_ref, b3_ref, fr_ref, wo_ref, dl_ref,
                       o_ref, *, t):
    z = z_ref[...]
    h = jnp.sin(fr_ref[0:1, :] * (_dot(z, w1_ref[...], HP) + b1_ref[...]))
    h = jnp.sin(fr_ref[1:2, :] * (_dot(h, w2_ref[...], HP) + b2_ref[...]))
    h = jnp.sin(fr_ref[2:3, :] * (_dot(h, w3_ref[...], HP) + b3_ref[...]))
    window = jnp.exp(-z[:, 0:1] * dl_ref[...])
    tr = z.shape[0]
    row = pl.program_id(0) * tr + lax.broadcasted_iota(jnp.int32, (tr, 1), 0)
    o_ref[...] = jnp.where(row == t, 0.0, _dot(h, wo_ref[0], HP) * window)


def _filter_taps(t, w1, b1, w2, b2, w3, b3, freq, w_out, c_hy, tr=512):
    emb, width = w1.shape
    bands = (emb - 1) // 2
    pos = jnp.arange(t, dtype=F32)
    tt = pos / max(t - 1, 1)
    fr = jnp.linspace(1e-4, bands - 1, bands, dtype=F32)
    ang = (2.0 * math.pi / t) * pos[:, None] * fr[None, :]
    z = jnp.concatenate([tt[:, None], jnp.cos(ang), -jnp.sin(ang)], axis=-1)
    z = jnp.pad(z, ((0, 0), (0, 128 - emb)))
    w1 = jnp.pad(w1, ((0, 128 - emb), (0, 0)))
    emb = 128
    z2 = jnp.concatenate([z, z[:1], z[1:][::-1]], axis=0)
    max_decay = math.log(HY_TARGET) / HY_FAST_PCT
    min_decay = math.log(HY_TARGET) / HY_SLOW_PCT
    delta = jnp.abs(jnp.linspace(min_decay, max_decay, c_hy, dtype=F32))
    order = w_out.shape[1] // (2 * c_hy)
    nout = order * c_hy
    w_dir = w_out.reshape(width, order, 2, c_hy).transpose(2, 0, 1, 3).reshape(2, width, nout)
    delta_full = jnp.tile(delta, order).reshape(1, nout)
    tr = min(tr, t)
    nt = t // tr
    full = lambda a: pl.BlockSpec(a.shape, lambda i: (0,) * a.ndim)
    args = [z2, w1, b1.reshape(1, -1), w2, b2.reshape(1, -1), w3, b3.reshape(1, -1), freq, w_dir, delta_full]
    in_specs = [pl.BlockSpec((tr, emb), lambda i: (i, 0))] + [full(a) for a in args[1:]]
    in_specs[8] = pl.BlockSpec((1, width, nout), lambda i: (i // nt, 0, 0))
    return pl.pallas_call(
        functools.partial(_filter_mlp_kernel, t=t),
        out_shape=jax.ShapeDtypeStruct((2 * t, nout), F32),
        grid=(2 * nt,),
        in_specs=in_specs,
        out_specs=pl.BlockSpec((tr, nout), lambda i: (i, 0)),
        compiler_params=_params(("parallel",)),
        name="hy_filter_mlp",
    )(*args)


def _filter_s1_kernel(x_ref, g_ref, o_ref, ss_ref, *, g, oc):
    @pl.when(pl.program_id(0) == 0)
    def _():
        ss_ref[...] = jnp.zeros_like(ss_ref)

    n1 = x_ref.shape[0]
    x = jnp.concatenate([x_ref[:, j, :] for j in range(g)], axis=0)
    ss_ref[...] += jnp.sum(x * x, axis=0, keepdims=True)
    xb = x.astype(BF16)
    for j in range(g):
        o_ref[j] = _dot(g_ref[j], xb[j * n1:(j + 1) * n1])


def _filter_s2_kernel(sr_ref, si_ref, f2_ref, ss_ref, o_ref, *, g, oc):
    n2 = sr_ref.shape[0]
    scale = lax.rsqrt(ss_ref[...] + 1e-6)
    s = jnp.concatenate([ref[:, j, :] for j in range(g) for ref in (sr_ref, si_ref)], axis=0).astype(BF16)
    for j in range(g):
        o_ref[j] = _dot(f2_ref[...], s[2 * j * n2:2 * (j + 1) * n2]) * scale


def _filter_spectrum(two, tabs):
    n, oc = two.shape
    n1, n2 = tabs["n1"], tabs["n2"]
    g = SUB
    s1, ss = pl.pallas_call(
        functools.partial(_filter_s1_kernel, g=g, oc=oc),
        out_shape=(jax.ShapeDtypeStruct((n2, 2 * n1, oc), F32), jax.ShapeDtypeStruct((1, oc), F32)),
        grid=(n2 // g,),
        in_specs=[pl.BlockSpec((n1, g, oc), lambda j: (0, j, 0)),
                  pl.BlockSpec((g, 2 * n1, n1), lambda j: (j, 0, 0))],
        out_specs=(pl.BlockSpec((g, 2 * n1, oc), lambda j: (j, 0, 0)),
                   pl.BlockSpec((1, oc), lambda j: (0, 0))),
        compiler_params=_params(("arbitrary",)),
        name="hy_filter_dft1",
    )(two.reshape(n1, n2, oc), tabs["g1f"])
    return pl.pallas_call(
        functools.partial(_filter_s2_kernel, g=g, oc=oc),
        out_shape=jax.ShapeDtypeStruct((n1, 2 * n2, oc), F32),
        grid=(n1 // g,),
        in_specs=[pl.BlockSpec((n2, g, oc), lambda k: (0, k, 0)),
                  pl.BlockSpec((n2, g, oc), lambda k: (0, n1 // g + k, 0)),
                  pl.BlockSpec((2 * n2, 2 * n2), lambda k: (0, 0)),
                  pl.BlockSpec((1, oc), lambda k: (0, 0))],
        out_specs=pl.BlockSpec((g, 2 * n2, oc), lambda k: (k, 0, 0)),
        compiler_params=_params(("parallel",)),
        name="hy_filter_dft2",
    )(s1, s1, tabs["f2"], ss)


def _conv_s1_kernel(u_ref, g_ref, o_ref, t_ref, *, g, n1h):
    for j0 in range(0, g, DFT_GROUP):
        js = range(j0, j0 + DFT_GROUP)
        rs = [_dot(g_ref[j], jnp.concatenate([u_ref[0, 0, j * n1h:(j + 1) * n1h, :],
                                               u_ref[0, 1, j * n1h:(j + 1) * n1h, :]], axis=0)) for j in js]
        for j, r in zip(js, rs):
            t_ref[:, j, :] = r
    o_ref[0] = t_ref[...].astype(o_ref.dtype)


def _conv_s23_kernel(sr_ref, si_ref, f2_ref, f3_ref, h_ref, o_ref, t_ref, *, g, n2):
    for j0 in range(0, g, DFT_GROUP):
        js = range(j0, j0 + DFT_GROUP)
        xs = [_dot(f2_ref[...], jnp.concatenate([sr_ref[0, j], si_ref[0, j]], axis=0)) for j in js]
        ys = []
        for j, x in zip(js, xs):
            xr, xi = x[:n2], x[n2:]
            hr, hi = h_ref[j, :n2, :], h_ref[j, n2:, :]
            ys.append(jnp.concatenate([xr * hr - xi * hi, xr * hi + xi * hr], axis=0).astype(BF16))
        rs = [_dot(f3_ref[...], y) for y in ys]
        for j, r in zip(js, rs):
            t_ref[:, j, :] = r
    o_ref[0] = t_ref[...].astype(o_ref.dtype)


def _conv_s4_kernel(rr_ref, ri_ref, g4_ref, u_ref, gate_ref, skip_ref, *rest, g, n1h, norm):
    if norm:
        ng_ref, avg_ref, o_ref = rest
    else:
        (o_ref,) = rest
    skip = skip_ref[...]
    ys = [_dot(g4_ref[j], jnp.concatenate([rr_ref[0, j], ri_ref[0, j]], axis=0)) for j in range(g)]
    zs = [gate_ref[0, q, j * n1h:(j + 1) * n1h, :].astype(F32)
          * (ys[j][q * n1h:(q + 1) * n1h] + u_ref[0, q, j * n1h:(j + 1) * n1h, :].astype(F32) * skip)
          for j in range(g) for q in range(2)]
    if norm:
        ms = [_dot((z * z).astype(BF16), avg_ref[...]) for z in zs]
        zs = [z * lax.rsqrt(m + RMS_EPS) * ng_ref[...] for z, m in zip(zs, ms)]
    for j in range(g):
        for q in range(2):
            if norm:
                o_ref[0, q, :, j, :] = zs[2 * j + q]
            else:
                o_ref[0, q, j * n1h:(j + 1) * n1h, :] = zs[2 * j + q].astype(o_ref.dtype)


def _long_conv_gate(u_arr, u_col, gate_arr, gate_col, h_spec, h_col, skip, tabs, c, norm_g=None):
    b, t, _ = u_arr.shape
    n1, n2, n1h = tabs["n1"], tabs["n2"], tabs["n1h"]
    g = SUB16
    npair = b // 2
    u4 = u_arr.reshape(npair, 2, t, u_arr.shape[2])
    gate4 = gate_arr.reshape(npair, 2, t, gate_arr.shape[2])
    gt = SUB16
    s1 = pl.pallas_call(
        functools.partial(_conv_s1_kernel, g=gt, n1h=n1h),
        out_shape=jax.ShapeDtypeStruct((npair, 2 * n1, n2, c), BF16),
        grid=(n2 // gt, npair),
        in_specs=[pl.BlockSpec((1, 2, gt * n1h, c), lambda j, p: (p, 0, j, u_col)),
                  pl.BlockSpec((gt, 2 * n1, n1), lambda j, p: (j, 0, 0))],
        out_specs=pl.BlockSpec((1, 2 * n1, gt, c), lambda j, p: (p, 0, j, 0)),
        scratch_shapes=[pltpu.VMEM((2 * n1, gt, c), F32)],
        compiler_params=_params(("parallel", "parallel")),
        name="hy_conv_dft1",
    )(u4, tabs["g1"])
    r = pl.pallas_call(
        functools.partial(_conv_s23_kernel, g=gt, n2=n2),
        out_shape=jax.ShapeDtypeStruct((npair, 2 * n2, n1, c), BF16),
        grid=(n1 // gt, npair),
        in_specs=[pl.BlockSpec((1, gt, n2, c), lambda k, p: (p, k, 0, 0)),
                  pl.BlockSpec((1, gt, n2, c), lambda k, p: (p, n1 // gt + k, 0, 0)),
                  pl.BlockSpec((2 * n2, 2 * n2), lambda k, p: (0, 0)),
                  pl.BlockSpec((2 * n2, 2 * n2), lambda k, p: (0, 0)),
                  pl.BlockSpec((gt, 2 * n2, c), lambda k, p: (k, 0, h_col))],
        out_specs=pl.BlockSpec((1, 2 * n2, gt, c), lambda k, p: (p, 0, k, 0)),
        scratch_shapes=[pltpu.VMEM((2 * n2, gt, c), F32)],
        compiler_params=_params(("parallel", "parallel")),
        name="hy_conv_dft23",
    )(s1, s1, tabs["f2"], tabs["f3"], h_spec)
    norm = norm_g is not None
    in_specs = [pl.BlockSpec((1, g, n1, c), lambda p, j: (p, j, 0, 0)),
                pl.BlockSpec((1, g, n1, c), lambda p, j: (p, n2 // g + j, 0, 0)),
                pl.BlockSpec((g, n1, 2 * n1), lambda p, j: (j, 0, 0)),
                pl.BlockSpec((1, 2, g * n1h, c), lambda p, j: (p, 0, j, u_col)),
                pl.BlockSpec((1, 2, g * n1h, c), lambda p, j: (p, 0, j, gate_col)),
                pl.BlockSpec((1, c), lambda p, j: (0, 0))]
    args = [r, r, tabs["g4"], u4, gate4, skip.reshape(1, c)]
    if norm:
        ch = jnp.arange(c) // HEAD
        avg = ((ch[:, None] == ch[None, :]).astype(F32) / HEAD).astype(BF16)
        in_specs += [pl.BlockSpec((1, c), lambda p, j: (0, 0)), pl.BlockSpec((c, c), lambda p, j: (0, 0))]
        args += [norm_g.reshape(1, c), avg]
        out_shape = jax.ShapeDtypeStruct((npair, 2, n1h, n2, c), F32)
        out_spec = pl.BlockSpec((1, 2, n1h, g, c), lambda p, j: (p, 0, 0, j, 0))
    else:
        out_shape = jax.ShapeDtypeStruct((npair, 2, t, c), BF16)
        out_spec = pl.BlockSpec((1, 2, g * n1h, c), lambda p, j: (p, 0, j, 0))
    out = pl.pallas_call(
        functools.partial(_conv_s4_kernel, g=g, n1h=n1h, norm=norm),
        out_shape=out_shape,
        grid=(npair, n2 // g),
        in_specs=in_specs,
        out_specs=out_spec,
        compiler_params=_params(("parallel", "parallel")),
        name="hy_conv_dft4",
    )(*args)
    return out.reshape(b, t, c)


def _hyena_branch(p, conv_w, conv_b, fw1, fb1, fw2, fb2, fw3, fb3, ffreq, fwout, skip, norm_g, tabs):
    b, t, _ = p.shape
    c = norm_g.shape[0]
    n1, n2, n1h = tabs["n1"], tabs["n2"], tabs["n1h"]
    u = _shortconv(p, conv_w, conv_b, n1h, n2)
    two = _filter_taps(t, fw1, fb1, fw2, fb2, fw3, fb3, ffreq, fwout, c)
    hspec = _filter_spectrum(two, tabs)
    z = _long_conv_gate(u, 2, u, 0, hspec, 0, skip[0], tabs, c)
    return _long_conv_gate(z, 0, u, 1, hspec, 1, skip[1], tabs, c, norm_g=norm_g)


def _rw_prep_kernel(*refs, has_vres, c):
    if has_vres:
        (p_ref, pp_ref, pn_ref, mu_ref, w0_ref, w2_ref, a0_ref, a2_ref, g2_ref, kk_ref, ka_ref, rk_ref,
         sum_ref, vf_ref, v0_ref, v2_ref,
         r_o, v_o, kk_o, lwf_o, lwb_o, kf_o, kb_o, af_o, ab_o, g_o, bon_o) = refs
    else:
        (p_ref, pp_ref, pn_ref, mu_ref, w0_ref, w2_ref, a0_ref, a2_ref, g2_ref, kk_ref, ka_ref, rk_ref,
         sum_ref,
         r_o, v_o, kk_o, lwf_o, lwb_o, kf_o, kb_o, af_o, ab_o, g_o, bon_o) = refs
    i = pl.program_id(1)
    last = pl.num_programs(1) - 1
    nsh = mu_ref.shape[1]
    p = p_ref[0, :, :nsh].astype(F32)
    tt = p.shape[0]
    rows = lax.broadcasted_iota(jnp.int32, (tt, 1), 0)
    prev_row = jnp.where(i == 0, 0.0, pp_ref[0, SUB16 - 1:SUB16, :nsh].astype(F32))
    next_row = jnp.where(i == last, 0.0, pn_ref[0, 0:1, :nsh].astype(F32))
    prev = jnp.where(rows == 0, prev_row, pltpu.roll(p, 1, 0))
    nxt = jnp.where(rows == tt - 1, next_row, pltpu.roll(p, tt - 1, 0))
    mu0, mu1 = mu_ref[0:1, :], mu_ref[1:2, :]
    pf = p * (1.0 - mu0 - mu1) + mu0 * prev + mu1 * nxt
    r = pf[:, :c]
    k = pf[:, c:2 * c]
    v = pf[:, 2 * c:3 * c]
    lw = 3 * c
    nd = w2_ref.shape[0]
    na = a2_ref.shape[0]
    wd = pf[:, lw:lw + nd]
    ad = pf[:, lw + nd:lw + nd + na]
    gd = pf[:, lw + nd + na:]
    if has_vres:
        lora = _dot(p_ref[0, :, nsh:], v2_ref[...])
        v = v + (vf_ref[0].astype(F32) - v) * jax.nn.sigmoid(v0_ref[...] + lora)
    g = _dot(jax.nn.sigmoid(gd).astype(BF16), g2_ref[...])
    kk = k * kk_ref[...]
    kk = kk * lax.rsqrt(jnp.maximum(_dot((kk * kk).astype(BF16), sum_ref[...]), 1e-24))
    wl = w0_ref[...] + _dot(jnp.tanh(wd).astype(BF16), w2_ref[...])
    logw = -math.exp(-0.5) * jax.nn.sigmoid(wl)
    a = jax.nn.sigmoid(a0_ref[...] + _dot(ad.astype(BF16), a2_ref[...]))
    ka = ka_ref[...]
    rk = rk_ref[...]
    k_d = [k * (1.0 + (a[:, d * c:(d + 1) * c] - 1.0) * ka) for d in range(2)]
    bon = _dot((r * (k_d[0] + k_d[1]) * rk).astype(BF16), sum_ref[...]) * v
    r_o[0] = r.astype(BF16)
    v_o[0] = v.astype(BF16)
    kk_o[0] = kk.astype(BF16)
    lwf_o[0] = logw[:, :c]
    lwb_o[0] = logw[:, c:]
    kf_o[0] = k_d[0].astype(BF16)
    kb_o[0] = k_d[1].astype(BF16)
    af_o[0] = a[:, :c].astype(BF16)
    ab_o[0] = a[:, c:].astype(BF16)
    g_o[0] = g.astype(BF16)
    bon_o[0] = bon.astype(BF16)


def _blockdiag2(m):
    k, c = m.shape[1], m.shape[2]
    z = jnp.zeros((k, c), m.dtype)
    return jnp.concatenate([jnp.concatenate([m[0], z], axis=1), jnp.concatenate([z, m[1]], axis=1)], axis=0)


def _rw_prep(p, v_first, shift, w0, w2, a0, a2, g2, k_k, k_a, r_k, vres, c, tt=1024):
    b, t, pw = p.shape
    tt = min(tt, t)
    has_vres = vres is not None
    ch = jnp.arange(c) // HEAD
    summ = (ch[:, None] == ch[None, :]).astype(BF16)
    row = lambda a: a.reshape(1, -1)
    args = [p, p, p, shift, row(w0), _blockdiag2(w2).astype(BF16), row(a0), _blockdiag2(a2).astype(BF16),
            g2.astype(BF16), row(k_k), row(k_a), row(r_k), summ]
    full = lambda a: pl.BlockSpec(a.shape, lambda i, j: (0,) * a.ndim)
    nhb = t // SUB16
    in_specs = [pl.BlockSpec((1, tt, pw), lambda i, j: (i, j, 0)),
                pl.BlockSpec((1, SUB16, pw), lambda i, j: (i, jnp.maximum(j * (tt // SUB16) - 1, 0), 0)),
                pl.BlockSpec((1, SUB16, pw), lambda i, j: (i, jnp.minimum((j + 1) * (tt // SUB16), nhb - 1), 0))]
    in_specs += [full(a) for a in args[3:]]
    if has_vres:
        v0, _, v2 = vres
        v2p = jnp.pad(v2, ((0, pw - shift.shape[1] - v2.shape[0]), (0, 0))).astype(BF16)
        extra = [v_first, row(v0), v2p]
        in_specs += [pl.BlockSpec((1, tt, c), lambda i, j: (i, j, 0))] + [full(a) for a in extra[1:]]
        args += extra
    outs = tuple(jax.ShapeDtypeStruct((b, t, c), F32 if i in (3, 4) else BF16) for i in range(11))
    return pl.pallas_call(
        functools.partial(_rw_prep_kernel, has_vres=has_vres, c=c),
        out_shape=outs,
        grid=(b, t // tt),
        in_specs=in_specs,
        out_specs=tuple(pl.BlockSpec((1, tt, c), lambda i, j: (i, j, 0)) for _ in range(11)),
        compiler_params=_params(("parallel", "parallel")),
        name="rw_prep",
    )(*args)


def _scan_chunks(s0, r, k, v, kk, a, lw, rev, bdm):
    L = CHUNK
    n = len(r)
    each = lambda f, *ls: [f(*xs) for xs in zip(*ls)]
    ti = lax.broadcasted_iota(jnp.int32, (L, L), 0)
    si = lax.broadcasted_iota(jnp.int32, (L, L), 1)
    tt = lax.broadcasted_iota(jnp.int32, (L, GW), 0)
    ss = lax.broadcasted_iota(jnp.int32, (L, GW), 1) % L
    eye = (ss == tt).astype(F32)
    tri = [((si >= ti) if q else (si <= ti)).astype(F32).astype(BF16) for q in rev]
    strict = [(ss > tt) if q else (ss < tt) for q in rev]
    incl = [(ss >= tt) if q else (ss <= tt) for q in rev]
    bdf = bdm.astype(F32)

    def bd(x):
        xb = x.astype(BF16)
        return jnp.concatenate([xb] * HEADS_PER_GROUP, axis=0) * bdm

    def bd_t(x):
        xt = x.T.astype(BF16)
        return jnp.concatenate([xt] * HEADS_PER_GROUP, axis=1) * bdm

    def rcmul(x, y):
        return _dot(x.astype(BF16), bd(y))

    parts = each(_split3, lw)
    cum = [_dot(t, p[0]) + _dot(t, p[1]) + _dot(t, p[2]) for t, p in zip(tri, parts)]
    tot = [c[0:1, :] if q else c[L - 1:L, :] for c, q in zip(cum, rev)]
    p_in = each(jnp.exp, cum)
    p_inv = each(lambda c: jnp.exp(-c), cum)
    p_ex = each(lambda c, w: jnp.exp(c - w), cum, lw)
    p_rem = each(lambda t, c: jnp.exp(t - c), tot, cum)
    beta = each(lambda x, y: x * y, kk, a)
    a_t = each(lambda x, p: -x * p, kk, p_ex)
    r_t = each(lambda x, p: x * p, r, p_in)
    k_t = each(lambda x, p: x * p, k, p_inv)
    b_t = each(lambda x, p: x * p, beta, p_inv)
    k_h = each(lambda x, p: x * p, k, p_rem)
    b_h = each(lambda x, p: x * p, beta, p_rem)

    lhs = each(lambda x, y: jnp.concatenate([x, y], axis=0).astype(BF16), a_t, r_t)
    akk = each(lambda x, y: _dot(x, bd_t(y)), lhs, k_t)
    abb = each(lambda x, y: _dot(x, bd_t(y)), lhs, b_t)
    a_ak = each(lambda m, x: jnp.where(m, x[:L], 0.0), strict, akk)
    a_rk = each(lambda m, x: jnp.where(m, x[L:], 0.0), incl, akk)
    a_ab = each(lambda m, x: jnp.where(m, x[:L], 0.0), strict, abb)
    a_rb = each(lambda m, x: jnp.where(m, x[L:], 0.0), incl, abb)

    same = lambda size: (tt // size) == (ss // size)
    d = each(lambda x: jnp.where(same(INV_BASE), x, 0.0), a_ab)
    tinv = each(lambda x: eye + x, d)
    pw = each(rcmul, d, d)
    size = 4
    while size < INV_BASE:
        both = each(lambda p, t: rcmul(jnp.concatenate([p, t], axis=0), p), pw, tinv)
        pw = [x[:L] for x in both]
        tinv = each(lambda t, x: t + x[L:], tinv, both)
        size *= 2
    tinv = each(lambda t, p: t + rcmul(t, p), tinv, pw)
    size = INV_BASE
    while size < L:
        size *= 2
        couple = same(size) & ~same(size // 2)
        e = each(lambda x: jnp.where(couple, x, 0.0), a_ab)
        tinv = each(lambda t, y: t + rcmul(rcmul(t, y), t), tinv, e)

    av = each(lambda x, y, z: rcmul(jnp.concatenate([x, y], axis=0), z), a_ak, a_rk, v)
    ct = each(rcmul, a_rb, tinv)
    wu2 = each(lambda t, c, x, y: _dot(jnp.concatenate([t, c], axis=0).astype(BF16),
                                       jnp.concatenate([bd(x), bd(y[:L])], axis=1)), tinv, ct, a_t, av)
    wu = [x[:L] for x in wu2]
    rbwu = [x[L:] for x in wu2]
    o = each(lambda x, w, s, y: _dot((x + w[:, :GW]).astype(BF16), bd_t(s)) + y[L:] + w[:, GW:],
             r_t, rbwu, s0, av)

    m_bd = each(lambda w, x: _dot_tn(w[:, :GW].astype(BF16), x.astype(BF16)) * bdf, wu, b_h)
    n_full = each(lambda x, w, y, z: _dot_tn(jnp.concatenate([x, w[:, GW:]], axis=0).astype(BF16),
                                             jnp.concatenate([y, z], axis=0).astype(BF16)) * bdf,
                  v, wu, k_h, b_h)
    s1 = []
    for i in range(n):
        n_rc = n_full[i][0:HEAD]
        for h in range(1, HEADS_PER_GROUP):
            n_rc = n_rc + n_full[i][h * HEAD:(h + 1) * HEAD]
        s1.append(_dot(s0[i].astype(BF16), m_bd[i].astype(BF16)) + s0[i] * jnp.exp(tot[i]) + n_rc)
    return s1, o


def _scan_kernel(rf_ref, kf_ref, vf_ref, kkf_ref, af_ref, lwf_ref,
                 rb_ref, kb_ref, vb_ref, kkb_ref, ab_ref, lwb_ref, of_ref, ob_ref, s_ref, *, ng, nb):
    @pl.when(pl.program_id(1) == 0)
    def _():
        s_ref[...] = jnp.zeros_like(s_ref)

    gi = lax.broadcasted_iota(jnp.int32, (GW, GW), 0) // HEAD
    gj = lax.broadcasted_iota(jnp.int32, (GW, GW), 1) // HEAD
    bdm = (gi == gj).astype(F32).astype(BF16)
    dirs = ((rf_ref, kf_ref, vf_ref, kkf_ref, af_ref, lwf_ref), (rb_ref, kb_ref, vb_ref, kkb_ref, ab_ref, lwb_ref))
    chains = [(i, d, h) for i in range(nb) for d in range(2) for h in range(ng)]
    ins = [[dirs[d][q][i, :, h * GW:(h + 1) * GW].astype(F32) for i, d, h in chains] for q in range(6)]
    s1, o = _scan_chunks([s_ref[i, d, h] for i, d, h in chains], *ins, [d == 1 for _, d, _ in chains], bdm)
    for (i, d, h), s_new, o_new in zip(chains, s1, o):
        (of_ref, ob_ref)[d][i, :, h * GW:(h + 1) * GW] = o_new
        s_ref[i, d, h] = s_new


SCAN_BATCH_ROWS = 4


def _wkv_scan(r, v, kk, kf, af, lwf, kb, ab, lwb):
    b, t, c = r.shape
    nc = t // CHUNK
    ng = c // GW
    nb = math.gcd(b, SCAN_BATCH_ROWS)
    fspec = pl.BlockSpec((nb, CHUNK, c), lambda i, j: (i, j, 0))
    bspec = pl.BlockSpec((nb, CHUNK, c), lambda i, j: (i, nc - 1 - j, 0))
    return pl.pallas_call(
        functools.partial(_scan_kernel, ng=ng, nb=nb),
        out_shape=(jax.ShapeDtypeStruct((b, t, c), F32), jax.ShapeDtypeStruct((b, t, c), F32)),
        grid=(b // nb, nc),
        in_specs=[fspec] * 6 + [bspec] * 6,
        out_specs=(fspec, bspec),
        scratch_shapes=[pltpu.VMEM((nb, 2, ng, HEAD, GW), F32)],
        compiler_params=_params(("parallel", "arbitrary")),
        name="rw_scan",
    )(r, kf, v, kk, af, lwf, r, kb, v, kk, ab, lwb)


def _out_proj_kernel(yh_ref, sf_ref, sb_ref, bon_ref, g_ref, lg_ref, lb_ref, avg_ref, wh_ref, wr_ref, r_ref,
                     o_ref):
    s = sf_ref[...] + sb_ref[...]
    mean = _dot3(s, avg_ref[...])
    d = s - mean
    var = _dot((d * d).astype(BF16), avg_ref[...])
    y = d * lax.rsqrt(var + GN_EPS) * lg_ref[...] + lb_ref[...] + bon_ref[...].astype(F32)
    yr = (y * g_ref[...].astype(F32)).astype(BF16)
    o_ref[...] = _dot(yh_ref[...].astype(BF16), wh_ref[...]) + _dot(yr, wr_ref[...]) + r_ref[...]


def _out_proj(yh, sf, sb, bon, g, lnx_g, lnx_b, w, res, tm=1024):
    m, ch = yh.shape
    cr = sf.shape[1]
    d = w.shape[1]
    tm = min(tm, m)
    head = jnp.arange(cr) // HEAD
    avg = ((head[:, None] == head[None, :]).astype(F32) / HEAD).astype(BF16)
    row = lambda c: pl.BlockSpec((tm, c), lambda i: (i, 0))
    return pl.pallas_call(
        _out_proj_kernel,
        out_shape=jax.ShapeDtypeStruct((m, d), F32),
        grid=(m // tm,),
        in_specs=[row(ch), row(cr), row(cr), row(cr), row(cr), _const_spec((1, cr)), _const_spec((1, cr)),
                  _const_spec((cr, cr)), _const_spec((ch, d)), _const_spec((cr, d)), row(d)],
        out_specs=row(d),
        compiler_params=_params(("parallel",)),
        name="out_proj",
    )(yh, sf, sb, bon, g, lnx_g.reshape(1, cr), lnx_b.reshape(1, cr), avg, w[:ch], w[ch:], res)


def _rwkv_branch(p, v_first, shift, w0, w2, a0, a2, g2, k_k, k_a, r_k, vres, c):
    r, v, kk, lwf, lwb, kf, kb, af, ab, g, bon = _rw_prep(
        p, v_first, shift, w0, w2, a0, a2, g2, k_k, k_a, r_k, vres, c)
    sf, sb = _wkv_scan(r, v, kk, kf, af, lwf, kb, ab, lwb)
    return sf, sb, bon, g, (v if vres is None else v_first)


def _forward(x, norm1_g, w_in, hy_conv_w, hy_conv_b, hy_f_w1, hy_f_b1, hy_f_w2, hy_f_b2, hy_f_w3, hy_f_b3,
             hy_f_freq, hy_f_wout, hy_skip, hy_norm_g, rw_shift, rw_w0, rw_w2, rw_a0, rw_a2, rw_g2, rw_k_k,
             rw_k_a, rw_r_k, rw_lnx_g, rw_lnx_b, rw_v0, rw_v1, rw_v2, w_out, norm2_g, mlp_w1, mlp_w2,
             final_g):
    b, t, d = x.shape
    depth = w_in.shape[0]
    c_hy = hy_norm_g.shape[1]
    c_rw = rw_lnx_g.shape[1]
    hy_proj = hy_conv_b.shape[1]
    assert b % 2 == 0 and t % CHUNK == 0 and CHUNK == HEAD
    tabs = _dft_tables(t)
    assert tabs["n2"] % SUB16 == 0 and tabs["n1"] % SUB16 == 0 and tabs["n1h"] % SUB == 0
    v_first = None
    for l in range(depth):
        w_l = w_in[l].astype(BF16)
        w_rw = w_l[:, hy_proj:]
        vres = None if l == 0 else (rw_v0[l - 1], rw_v1[l - 1], rw_v2[l - 1])
        if vres is not None:
            v1 = vres[1].astype(BF16)
            w_rw = jnp.concatenate([w_rw, jnp.pad(v1, ((0, 0), (0, -v1.shape[1] % 128)))], axis=1)
        p_hy, p_rw = _in_proj(x, norm1_g[l], w_l[:, :hy_proj], w_rw, tabs["n1h"], tabs["n2"])
        y_hy = _hyena_branch(p_hy, hy_conv_w[l], hy_conv_b[l], hy_f_w1[l], hy_f_b1[l], hy_f_w2[l], hy_f_b2[l],
                             hy_f_w3[l], hy_f_b3[l], hy_f_freq[l], hy_f_wout[l], hy_skip[l], hy_norm_g[l], tabs)
        sf, sb, bon, g, v_first = _rwkv_branch(p_rw, v_first, rw_shift[l], rw_w0[l], rw_w2[l], rw_a0[l], rw_a2[l],
                                               rw_g2[l], rw_k_k[l], rw_k_a[l], rw_r_k[l], vres, c_rw)
        flat = lambda a: a.reshape(b * t, a.shape[-1])
        x = _out_proj(flat(y_hy), flat(sf), flat(sb), flat(bon), flat(g), rw_lnx_g[l], rw_lnx_b[l],
                      w_out[l].astype(BF16), flat(x))
        x = _mlp(x, norm2_g[l], mlp_w1[l].astype(BF16), mlp_w2[l].astype(BF16),
                 final_g=final_g if l == depth - 1 else None).reshape(b, t, d)
    return x


def kernel(x, norm1_g, w_in, hy_conv_w, hy_conv_b, hy_f_w1, hy_f_b1, hy_f_w2, hy_f_b2, hy_f_w3, hy_f_b3,
           hy_f_freq, hy_f_wout, hy_skip, hy_norm_g, rw_shift, rw_w0, rw_w2, rw_a0, rw_a2, rw_g2, rw_k_k,
           rw_k_a, rw_r_k, rw_lnx_g, rw_lnx_b, rw_v0, rw_v1, rw_v2, w_out, norm2_g, mlp_w1, mlp_w2, final_g):
    return _forward(x, norm1_g, w_in, hy_conv_w, hy_conv_b, hy_f_w1, hy_f_b1, hy_f_w2, hy_f_b2, hy_f_w3,
                    hy_f_b3, hy_f_freq, hy_f_wout, hy_skip, hy_norm_g, rw_shift, rw_w0, rw_w2, rw_a0, rw_a2,
                    rw_g2, rw_k_k, rw_k_a, rw_r_k, rw_lnx_g, rw_lnx_b, rw_v0, rw_v1, rw_v2, w_out, norm2_g,
                    mlp_w1, mlp_w2, final_g)
```

```python
import functools
import math

import jax
import jax.numpy as jnp
from jax import lax
from jax.experimental import pallas as pl
from jax.experimental.pallas import tpu as pltpu

F32 = jnp.float32
BF16 = jnp.bfloat16
HP = lax.Precision.HIGHEST

HEAD = 64
HEADS_PER_GROUP = 4
GW = HEAD * HEADS_PER_GROUP
CHUNK = 64
INV_BASE = 8
RMS_EPS = 1e-5
GN_EPS = HEAD * 1e-5
HY_TARGET = 1e-2
HY_FAST_PCT = 0.3
HY_SLOW_PCT = 1.5
VMEM_LIMIT = 56 * 1024 * 1024


def _params(sem, vmem=VMEM_LIMIT):
    return pltpu.CompilerParams(dimension_semantics=sem, vmem_limit_bytes=vmem)


def _dot(a, b, precision=None):
    return jnp.dot(a, b, preferred_element_type=F32, precision=precision)


def _dot_tn(a, b, precision=None):
    return lax.dot_general(a, b, (((0,), (0,)), ((), ())), preferred_element_type=F32,
                           precision=precision)


def _const_spec(shape):
    return pl.BlockSpec(shape, lambda *_: (0,) * len(shape), pipeline_mode=pl.Buffered(1))


def _split3(x):
    h1 = x.astype(BF16)
    r1 = x - h1.astype(F32)
    h2 = r1.astype(BF16)
    h3 = (r1 - h2.astype(F32)).astype(BF16)
    return h1, h2, h3


def _dot3(x, m):
    h1, h2, h3 = _split3(x)
    return _dot(h1, m) + _dot(h2, m) + _dot(h3, m)


def _mlp_kernel(x_ref, g_ref, w1_ref, w2_ref, *rest, nchunk):
    x = x_ref[...]
    hn = (x * lax.rsqrt(jnp.mean(x * x, axis=-1, keepdims=True) + RMS_EPS) * g_ref[...]).astype(BF16)
    ff = w1_ref.shape[1]
    cw = ff // nchunk
    acc = x
    for j in range(nchunk):
        h = _dot(hn, w1_ref[:, j * cw:(j + 1) * cw])
        h = jnp.square(jnp.maximum(h, 0.0)).astype(BF16)
        acc = acc + _dot(h, w2_ref[j * cw:(j + 1) * cw, :])
    if len(rest) == 2:
        gf_ref, o_ref = rest
        acc = acc * lax.rsqrt(jnp.mean(acc * acc, axis=-1, keepdims=True) + RMS_EPS) * gf_ref[...]
    else:
        (o_ref,) = rest
    o_ref[...] = acc


def _mlp(x2d, g, w1, w2, final_g=None, tm=1024, nchunk=4):
    m, d = x2d.shape
    ff = w1.shape[1]
    tm = min(tm, m)
    in_specs = [pl.BlockSpec((tm, d), lambda i: (i, 0)), _const_spec((1, d)),
                _const_spec((d, ff)), _const_spec((ff, d))]
    args = [x2d, g.reshape(1, d), w1, w2]
    if final_g is not None:
        in_specs.append(_const_spec((1, d)))
        args.append(final_g.reshape(1, d))
    return pl.pallas_call(
        functools.partial(_mlp_kernel, nchunk=nchunk),
        out_shape=jax.ShapeDtypeStruct((m, d), F32),
        grid=(m // tm,),
        in_specs=in_specs,
        out_specs=pl.BlockSpec((tm, d), lambda i: (i, 0)),
        compiler_params=_params(("parallel",)),
        name="mlp",
    )(*args)


SUB = 8
SUB16 = 16
DFT_GROUP = 16


def _rms(x, g):
    return (x * lax.rsqrt(jnp.mean(x * x, axis=-1, keepdims=True) + RMS_EPS) * g).astype(BF16)


def _in_proj_kernel(x_ref, g_ref, wh_ref, wr_ref, oh_ref, or_ref, *, n2):
    hn = _rms(x_ref[0], g_ref[...])
    or_ref[0] = _dot(hn, wr_ref[...]).astype(or_ref.dtype)
    ph = _dot(hn, wh_ref[...])
    for j in range(SUB):
        oh_ref[0, :, j, :] = ph[j * n2:(j + 1) * n2]


def _in_proj(x, g, w_hy, w_rw, n1h, n2):
    b, t, d = x.shape
    ph, pr = w_hy.shape[1], w_rw.shape[1]
    tm = SUB * n2
    p_hy, p_rw = pl.pallas_call(
        functools.partial(_in_proj_kernel, n2=n2),
        out_shape=(jax.ShapeDtypeStruct((b, n2, n1h, ph), F32), jax.ShapeDtypeStruct((b, t, pr), BF16)),
        grid=(b, t // tm),
        in_specs=[pl.BlockSpec((1, tm, d), lambda i, j: (i, j, 0)), _const_spec((1, d)),
                  _const_spec((d, ph)), _const_spec((d, pr))],
        out_specs=(pl.BlockSpec((1, n2, SUB, ph), lambda i, j: (i, 0, j, 0)),
                   pl.BlockSpec((1, tm, pr), lambda i, j: (i, j, 0))),
        compiler_params=_params(("parallel", "parallel")),
        name="in_proj",
    )(x, g.reshape(1, d), w_hy, w_rw)
    return p_hy.reshape(b, t, ph), p_rw


def _shortconv_kernel(p_ref, w_ref, b_ref, o_ref, *, n1h, n2):
    w0 = w_ref[0:1, :]
    w1 = w_ref[1:2, :]
    w2 = w_ref[2:3, :]
    bias = b_ref[...]
    rows = lax.broadcasted_iota(jnp.int32, (n1h, 1), 0)

    def blk(i):
        return p_ref[0, pl.ds(pl.multiple_of(i * n1h, n1h), n1h), :].astype(F32)

    def body(i, carry):
        o_ref[0, pl.ds(pl.multiple_of(i * n1h, n1h), n1h), :] = (
            w0 * blk(i - 1) + w1 * blk(i) + w2 * blk(i + 1) + bias).astype(o_ref.dtype)
        return carry

    lax.fori_loop(1, n2 - 1, body, 0)
    last = p_ref[0, (n2 - 1) * n1h:, :].astype(F32)
    first = p_ref[0, :n1h, :].astype(F32)
    prev0 = jnp.where(rows == 0, 0.0, pltpu.roll(last, 1, 0))
    o_ref[0, :n1h, :] = (w0 * prev0 + w1 * first + w2 * p_ref[0, n1h:2 * n1h, :].astype(F32)
                         + bias).astype(o_ref.dtype)
    nxt = jnp.where(rows == n1h - 1, 0.0, pltpu.roll(first, n1h - 1, 0))
    o_ref[0, (n2 - 1) * n1h:, :] = (w0 * p_ref[0, (n2 - 2) * n1h:(n2 - 1) * n1h, :].astype(F32) + w1 * last
                                    + w2 * nxt + bias).astype(o_ref.dtype)


def _shortconv(p, w, bias, n1h, n2, cb=512):
    b, t, c = p.shape
    return pl.pallas_call(
        functools.partial(_shortconv_kernel, n1h=n1h, n2=n2),
        out_shape=jax.ShapeDtypeStruct((b, t, c), BF16),
        grid=(b, c // cb),
        in_specs=[pl.BlockSpec((1, t, cb), lambda i, j: (i, 0, j)),
                  pl.BlockSpec((3, cb), lambda i, j: (0, j)),
                  pl.BlockSpec((1, cb), lambda i, j: (0, j))],
        out_specs=pl.BlockSpec((1, t, cb), lambda i, j: (i, 0, j)),
        compiler_params=_params(("parallel", "parallel")),
        name="hy_shortconv",
    )(p, w, bias.reshape(1, c))


def _dft_tables(t):
    n = 2 * t
    n1 = 1 << ((n.bit_length() - 1) // 2)
    n2 = n // n1
    n1h = n1 // 2
    two_pi = 2.0 * math.pi
    k1 = jnp.arange(n1, dtype=jnp.int32)
    n2i = jnp.arange(n2, dtype=jnp.int32)

    def cs(prod, mod):
        ang = (two_pi / mod) * (prod % mod).astype(F32)
        return jnp.cos(ang), jnp.sin(ang)

    tpos = n2i[:, None] + n2 * jnp.arange(n1h, dtype=jnp.int32)[None, :]
    c, s = cs(k1[None, :, None] * tpos[:, None, :], n)
    g1 = jnp.concatenate([jnp.concatenate([c, s], axis=2), jnp.concatenate([-s, c], axis=2)], axis=1)
    ct, st = jnp.swapaxes(c, 1, 2) / n, jnp.swapaxes(s, 1, 2) / n
    g4a = jnp.concatenate([ct, st], axis=1)
    g4b = jnp.concatenate([-st, ct], axis=1)
    tposf = n2i[:, None] + n2 * jnp.arange(n1, dtype=jnp.int32)[None, :]
    cf, sf = cs(k1[None, :, None] * tposf[:, None, :], n)
    g1f = jnp.concatenate([cf, -sf], axis=1)
    c2, s2 = cs(n2i[:, None] * n2i[None, :], n2)
    f2a = jnp.concatenate([c2, -s2], axis=0)
    f2b = jnp.concatenate([s2, c2], axis=0)
    f3 = jnp.concatenate([jnp.concatenate([c2, -s2], axis=1), jnp.concatenate([s2, c2], axis=1)], axis=0)
    g4 = jnp.concatenate([g4a, g4b], axis=2)
    f2 = jnp.concatenate([f2a, f2b], axis=1)
    return dict(n1=n1, n2=n2, n1h=n1h, g1=g1.astype(BF16), g4=g4.astype(BF16), g1f=g1f.astype(BF16),
                f2=f2.astype(BF16), f3=f3.astype(BF16))


def _filter_mlp_kernel(z_ref, w1_ref, b1_ref, w2_ref, b2_ref, w3_ref, b3_ref, fr_ref, wo_ref, dl_ref,
                       o_ref, *, t):
    z = z_ref[...]
    h = jnp.sin(fr_ref[0:1, :] * (_dot(z, w1_ref[...], HP) + b1_ref[...]))
    h = jnp.sin(fr_ref[1:2, :] * (_dot(h, w2_ref[...], HP) + b2_ref[...]))
    h = jnp.sin(fr_ref[2:3, :] * (_dot(h, w3_ref[...], HP) + b3_ref[...]))
    window = jnp.exp(-z[:, 0:1] * dl_ref[...])
    tr = z.shape[0]
    row = pl.program_id(0) * tr + lax.broadcasted_iota(jnp.int32, (tr, 1), 0)
    o_ref[...] = jnp.where(row == t, 0.0, _dot(h, wo_ref[0], HP) * window)


def _filter_taps(t, w1, b1, w2, b2, w3, b3, freq, w_out, c_hy, tr=1024):
    emb, width = w1.shape
    bands = (emb - 1) // 2
    pos = jnp.arange(t, dtype=F32)
    tt = pos / max(t - 1, 1)
    fr = jnp.linspace(1e-4, bands - 1, bands, dtype=F32)
    ang = (2.0 * math.pi / t) * pos[:, None] * fr[None, :]
    z = jnp.concatenate([tt[:, None], jnp.cos(ang), -jnp.sin(ang)], axis=-1)
    z = jnp.pad(z, ((0, 0), (0, 128 - emb)))
    w1 = jnp.pad(w1, ((0, 128 - emb), (0, 0)))
    emb = 128
    z2 = jnp.concatenate([z, z[:1], z[1:][::-1]], axis=0)
    max_decay = math.log(HY_TARGET) / HY_FAST_PCT
    min_decay = math.log(HY_TARGET) / HY_SLOW_PCT
    delta = jnp.abs(jnp.linspace(min_decay, max_decay, c_hy, dtype=F32))
    order = w_out.shape[1] // (2 * c_hy)
    nout = order * c_hy
    w_dir = w_out.reshape(width, order, 2, c_hy).transpose(2, 0, 1, 3).reshape(2, width, nout)
    delta_full = jnp.tile(delta, order).reshape(1, nout)
    tr = min(tr, t)
    nt = t // tr
    full = lambda a: pl.BlockSpec(a.shape, lambda i: (0,) * a.ndim)
    args = [z2, w1, b1.reshape(1, -1), w2, b2.reshape(1, -1), w3, b3.reshape(1, -1), freq, w_dir, delta_full]
    in_specs = [pl.BlockSpec((tr, emb), lambda i: (i, 0))] + [full(a) for a in args[1:]]
    in_specs[8] = pl.BlockSpec((1, width, nout), lambda i: (i // nt, 0, 0))
    return pl.pallas_call(
        functools.partial(_filter_mlp_kernel, t=t),
        out_shape=jax.ShapeDtypeStruct((2 * t, nout), F32),
        grid=(2 * nt,),
        in_specs=in_specs,
        out_specs=pl.BlockSpec((tr, nout), lambda i: (i, 0)),
        compiler_params=_params(("parallel",)),
        name="hy_filter_mlp",
    )(*args)


def _filter_s1_kernel(x_ref, g_ref, o_ref, ss_ref, *, g, oc):
    @pl.when(pl.program_id(0) == 0)
    def _():
        ss_ref[...] = jnp.zeros_like(ss_ref)

    n1 = x_ref.shape[0]
    x = jnp.concatenate([x_ref[:, j, :] for j in range(g)], axis=0)
    ss_ref[...] += jnp.sum(x * x, axis=0, keepdims=True)
    xb = x.astype(BF16)
    for j in range(g):
        o_ref[j] = _dot(g_ref[j], xb[j * n1:(j + 1) * n1])


def _filter_s2_kernel(sr_ref, si_ref, f2_ref, ss_ref, o_ref, *, g, oc):
    n2 = sr_ref.shape[0]
    scale = lax.rsqrt(ss_ref[...] + 1e-6)
    s = jnp.concatenate([ref[:, j, :] for j in range(g) for ref in (sr_ref, si_ref)], axis=0).astype(BF16)
    for j in range(g):
        o_ref[j] = _dot(f2_ref[...], s[2 * j * n2:2 * (j + 1) * n2]) * scale


def _filter_spectrum(two, tabs):
    n, oc = two.shape
    n1, n2 = tabs["n1"], tabs["n2"]
    g = SUB
    s1, ss = pl.pallas_call(
        functools.partial(_filter_s1_kernel, g=g, oc=oc),
        out_shape=(jax.ShapeDtypeStruct((n2, 2 * n1, oc), F32), jax.ShapeDtypeStruct((1, oc), F32)),
        grid=(n2 // g,),
        in_specs=[pl.BlockSpec((n1, g, oc), lambda j: (0, j, 0)),
                  pl.BlockSpec((g, 2 * n1, n1), lambda j: (j, 0, 0))],
        out_specs=(pl.BlockSpec((g, 2 * n1, oc), lambda j: (j, 0, 0)),
                   pl.BlockSpec((1, oc), lambda j: (0, 0))),
        compiler_params=_params(("arbitrary",)),
        name="hy_filter_dft1",
    )(two.reshape(n1, n2, oc), tabs["g1f"])
    return pl.pallas_call(
        functools.partial(_filter_s2_kernel, g=g, oc=oc),
        out_shape=jax.ShapeDtypeStruct((n1, 2 * n2, oc), F32),
        grid=(n1 // g,),
        in_specs=[pl.BlockSpec((n2, g, oc), lambda k: (0, k, 0)),
                  pl.BlockSpec((n2, g, oc), lambda k: (0, n1 // g + k, 0)),
                  pl.BlockSpec((2 * n2, 2 * n2), lambda k: (0, 0)),
                  pl.BlockSpec((1, oc), lambda k: (0, 0))],
        out_specs=pl.BlockSpec((g, 2 * n2, oc), lambda k: (k, 0, 0)),
        compiler_params=_params(("parallel",)),
        name="hy_filter_dft2",
    )(s1, s1, tabs["f2"], ss)


def _conv_s1_kernel(u_ref, g_ref, o_ref, t_ref, *, g, n1h):
    for j0 in range(0, g, DFT_GROUP):
        js = range(j0, j0 + DFT_GROUP)
        rs = [_dot(g_ref[j], jnp.concatenate([u_ref[0, 0, j * n1h:(j + 1) * n1h, :],
                                               u_ref[0, 1, j * n1h:(j + 1) * n1h, :]], axis=0)) for j in js]
        for j, r in zip(js, rs):
            t_ref[:, j, :] = r
    o_ref[0] = t_ref[...].astype(o_ref.dtype)


def _conv_s23_kernel(sr_ref, si_ref, f2_ref, f3_ref, h_ref, o_ref, t_ref, *, g, n2):
    for j0 in range(0, g, DFT_GROUP):
        js = range(j0, j0 + DFT_GROUP)
        xs = [_dot(f2_ref[...], jnp.concatenate([sr_ref[0, j], si_ref[0, j]], axis=0)) for j in js]
        ys = []
        for j, x in zip(js, xs):
            xr, xi = x[:n2], x[n2:]
            hr, hi = h_ref[j, :n2, :], h_ref[j, n2:, :]
            ys.append(jnp.concatenate([xr * hr - xi * hi, xr * hi + xi * hr], axis=0).astype(BF16))
        rs = [_dot(f3_ref[...], y) for y in ys]
        for j, r in zip(js, rs):
            t_ref[:, j, :] = r
    o_ref[0] = t_ref[...].astype(o_ref.dtype)


def _conv_s4_kernel(rr_ref, ri_ref, g4_ref, u_ref, gate_ref, skip_ref, *rest, g, n1h, norm):
    if norm:
        ng_ref, avg_ref, o_ref = rest
    else:
        (o_ref,) = rest
    skip = skip_ref[...]
    ys = [_dot(g4_ref[j], jnp.concatenate([rr_ref[0, j], ri_ref[0, j]], axis=0)) for j in range(g)]
    zs = [gate_ref[0, q, j * n1h:(j + 1) * n1h, :].astype(F32)
          * (ys[j][q * n1h:(q + 1) * n1h] + u_ref[0, q, j * n1h:(j + 1) * n1h, :].astype(F32) * skip)
          for j in range(g) for q in range(2)]
    if norm:
        ms = [_dot((z * z).astype(BF16), avg_ref[...]) for z in zs]
        zs = [z * lax.rsqrt(m + RMS_EPS) * ng_ref[...] for z, m in zip(zs, ms)]
    for j in range(g):
        for q in range(2):
            if norm:
                o_ref[0, q, :, j, :] = zs[2 * j + q]
            else:
                o_ref[0, q, j * n1h:(j + 1) * n1h, :] = zs[2 * j + q].astype(o_ref.dtype)


def _long_conv_gate(u_arr, u_col, gate_arr, gate_col, h_spec, h_col, skip, tabs, c, norm_g=None):
    b, t, _ = u_arr.shape
    n1, n2, n1h = tabs["n1"], tabs["n2"], tabs["n1h"]
    g = SUB16
    npair = b // 2
    u4 = u_arr.reshape(npair, 2, t, u_arr.shape[2])
    gate4 = gate_arr.reshape(npair, 2, t, gate_arr.shape[2])
    gt = SUB16
    s1 = pl.pallas_call(
        functools.partial(_conv_s1_kernel, g=gt, n1h=n1h),
        out_shape=jax.ShapeDtypeStruct((npair, 2 * n1, n2, c), BF16),
        grid=(n2 // gt, npair),
        in_specs=[pl.BlockSpec((1, 2, gt * n1h, c), lambda j, p: (p, 0, j, u_col)),
                  pl.BlockSpec((gt, 2 * n1, n1), lambda j, p: (j, 0, 0))],
        out_specs=pl.BlockSpec((1, 2 * n1, gt, c), lambda j, p: (p, 0, j, 0)),
        scratch_shapes=[pltpu.VMEM((2 * n1, gt, c), F32)],
        compiler_params=_params(("parallel", "parallel")),
        name="hy_conv_dft1",
    )(u4, tabs["g1"])
    r = pl.pallas_call(
        functools.partial(_conv_s23_kernel, g=gt, n2=n2),
        out_shape=jax.ShapeDtypeStruct((npair, 2 * n2, n1, c), BF16),
        grid=(n1 // gt, npair),
        in_specs=[pl.BlockSpec((1, gt, n2, c), lambda k, p: (p, k, 0, 0)),
                  pl.BlockSpec((1, gt, n2, c), lambda k, p: (p, n1 // gt + k, 0, 0)),
                  pl.BlockSpec((2 * n2, 2 * n2), lambda k, p: (0, 0)),
                  pl.BlockSpec((2 * n2, 2 * n2), lambda k, p: (0, 0)),
                  pl.BlockSpec((gt, 2 * n2, c), lambda k, p: (k, 0, h_col))],
        out_specs=pl.BlockSpec((1, 2 * n2, gt, c), lambda k, p: (p, 0, k, 0)),
        scratch_shapes=[pltpu.VMEM((2 * n2, gt, c), F32)],
        compiler_params=_params(("parallel", "parallel")),
        name="hy_conv_dft23",
    )(s1, s1, tabs["f2"], tabs["f3"], h_spec)
    norm = norm_g is not None
    in_specs = [pl.BlockSpec((1, g, n1, c), lambda p, j: (p, j, 0, 0)),
                pl.BlockSpec((1, g, n1, c), lambda p, j: (p, n2 // g + j, 0, 0)),
                pl.BlockSpec((g, n1, 2 * n1), lambda p, j: (j, 0, 0)),
                pl.BlockSpec((1, 2, g * n1h, c), lambda p, j: (p, 0, j, u_col)),
                pl.BlockSpec((1, 2, g * n1h, c), lambda p, j: (p, 0, j, gate_col)),
                pl.BlockSpec((1, c), lambda p, j: (0, 0))]
    args = [r, r, tabs["g4"], u4, gate4, skip.reshape(1, c)]
    if norm:
        ch = jnp.arange(c) // HEAD
        avg = ((ch[:, None] == ch[None, :]).astype(F32) / HEAD).astype(BF16)
        in_specs += [pl.BlockSpec((1, c), lambda p, j: (0, 0)), pl.BlockSpec((c, c), lambda p, j: (0, 0))]
        args += [norm_g.reshape(1, c), avg]
        out_shape = jax.ShapeDtypeStruct((npair, 2, n1h, n2, c), F32)
        out_spec = pl.BlockSpec((1, 2, n1h, g, c), lambda p, j: (p, 0, 0, j, 0))
    else:
        out_shape = jax.ShapeDtypeStruct((npair, 2, t, c), BF16)
        out_spec = pl.BlockSpec((1, 2, g * n1h, c), lambda p, j: (p, 0, j, 0))
    out = pl.pallas_call(
        functools.partial(_conv_s4_kernel, g=g, n1h=n1h, norm=norm),
        out_shape=out_shape,
        grid=(npair, n2 // g),
        in_specs=in_specs,
        out_specs=out_spec,
        compiler_params=_params(("parallel", "parallel")),
        name="hy_conv_dft4",
    )(*args)
    return out.reshape(b, t, c)


def _hyena_branch(p, conv_w, conv_b, fw1, fb1, fw2, fb2, fw3, fb3, ffreq, fwout, skip, norm_g, tabs):
    b, t, _ = p.shape
    c = norm_g.shape[0]
    n1, n2, n1h = tabs["n1"], tabs["n2"], tabs["n1h"]
    u = _shortconv(p, conv_w, conv_b, n1h, n2)
    two = _filter_taps(t, fw1, fb1, fw2, fb2, fw3, fb3, ffreq, fwout, c)
    hspec = _filter_spectrum(two, tabs)
    z = _long_conv_gate(u, 2, u, 0, hspec, 0, skip[0], tabs, c)
    return _long_conv_gate(z, 0, u, 1, hspec, 1, skip[1], tabs, c, norm_g=norm_g)


def _rw_prep_kernel(*refs, has_vres, c):
    if has_vres:
        (p_ref, pp_ref, pn_ref, mu_ref, w0_ref, w2_ref, a0_ref, a2_ref, g2_ref, kk_ref, ka_ref, rk_ref,
         sum_ref, vf_ref, v0_ref, v2_ref,
         r_o, v_o, kk_o, lwf_o, lwb_o, kf_o, kb_o, af_o, ab_o, g_o, bon_o) = refs
    else:
        (p_ref, pp_ref, pn_ref, mu_ref, w0_ref, w2_ref, a0_ref, a2_ref, g2_ref, kk_ref, ka_ref, rk_ref,
         sum_ref,
         r_o, v_o, kk_o, lwf_o, lwb_o, kf_o, kb_o, af_o, ab_o, g_o, bon_o) = refs
    i = pl.program_id(1)
    last = pl.num_programs(1) - 1
    nsh = mu_ref.shape[1]
    p = p_ref[0, :, :nsh].astype(F32)
    tt = p.shape[0]
    rows = lax.broadcasted_iota(jnp.int32, (tt, 1), 0)
    prev_row = jnp.where(i == 0, 0.0, pp_ref[0, SUB16 - 1:SUB16, :nsh].astype(F32))
    next_row = jnp.where(i == last, 0.0, pn_ref[0, 0:1, :nsh].astype(F32))
    prev = jnp.where(rows == 0, prev_row, pltpu.roll(p, 1, 0))
    nxt = jnp.where(rows == tt - 1, next_row, pltpu.roll(p, tt - 1, 0))
    mu0, mu1 = mu_ref[0:1, :], mu_ref[1:2, :]
    pf = p * (1.0 - mu0 - mu1) + mu0 * prev + mu1 * nxt
    r = pf[:, :c]
    k = pf[:, c:2 * c]
    v = pf[:, 2 * c:3 * c]
    lw = 3 * c
    nd = w2_ref.shape[0]
    na = a2_ref.shape[0]
    wd = pf[:, lw:lw + nd]
    ad = pf[:, lw + nd:lw + nd + na]
    gd = pf[:, lw + nd + na:]
    if has_vres:
        lora = _dot(p_ref[0, :, nsh:], v2_ref[...])
        v = v + (vf_ref[0].astype(F32) - v) * jax.nn.sigmoid(v0_ref[...] + lora)
    g = _dot(jax.nn.sigmoid(gd).astype(BF16), g2_ref[...])
    kk = k * kk_ref[...]
    kk = kk * lax.rsqrt(jnp.maximum(_dot((kk * kk).astype(BF16), sum_ref[...]), 1e-24))
    wl = w0_ref[...] + _dot(jnp.tanh(wd).astype(BF16), w2_ref[...])
    logw = -math.exp(-0.5) * jax.nn.sigmoid(wl)
    a = jax.nn.sigmoid(a0_ref[...] + _dot(ad.astype(BF16), a2_ref[...]))
    ka = ka_ref[...]
    rk = rk_ref[...]
    k_d = [k * (1.0 + (a[:, d * c:(d + 1) * c] - 1.0) * ka) for d in range(2)]
    bon = _dot((r * (k_d[0] + k_d[1]) * rk).astype(BF16), sum_ref[...]) * v
    r_o[0] = r.astype(BF16)
    v_o[0] = v.astype(BF16)
    kk_o[0] = kk.astype(BF16)
    lwf_o[0] = logw[:, :c]
    lwb_o[0] = logw[:, c:]
    kf_o[0] = k_d[0].astype(BF16)
    kb_o[0] = k_d[1].astype(BF16)
    af_o[0] = a[:, :c].astype(BF16)
    ab_o[0] = a[:, c:].astype(BF16)
    g_o[0] = g.astype(BF16)
    bon_o[0] = bon.astype(BF16)


def _blockdiag2(m):
    k, c = m.shape[1], m.shape[2]
    z = jnp.zeros((k, c), m.dtype)
    return jnp.concatenate([jnp.concatenate([m[0], z], axis=1), jnp.concatenate([z, m[1]], axis=1)], axis=0)


def _rw_prep(p, v_first, shift, w0, w2, a0, a2, g2, k_k, k_a, r_k, vres, c, tt=1024):
    b, t, pw = p.shape
    tt = min(tt, t)
    has_vres = vres is not None
    ch = jnp.arange(c) // HEAD
    summ = (ch[:, None] == ch[None, :]).astype(BF16)
    row = lambda a: a.reshape(1, -1)
    args = [p, p, p, shift, row(w0), _blockdiag2(w2).astype(BF16), row(a0), _blockdiag2(a2).astype(BF16),
            g2.astype(BF16), row(k_k), row(k_a), row(r_k), summ]
    full = lambda a: pl.BlockSpec(a.shape, lambda i, j: (0,) * a.ndim)
    nhb = t // SUB16
    in_specs = [pl.BlockSpec((1, tt, pw), lambda i, j: (i, j, 0)),
                pl.BlockSpec((1, SUB16, pw), lambda i, j: (i, jnp.maximum(j * (tt // SUB16) - 1, 0), 0)),
                pl.BlockSpec((1, SUB16, pw), lambda i, j: (i, jnp.minimum((j + 1) * (tt // SUB16), nhb - 1), 0))]
    in_specs += [full(a) for a in args[3:]]
    if has_vres:
        v0, _, v2 = vres
        v2p = jnp.pad(v2, ((0, pw - shift.shape[1] - v2.shape[0]), (0, 0))).astype(BF16)
        extra = [v_first, row(v0), v2p]
        in_specs += [pl.BlockSpec((1, tt, c), lambda i, j: (i, j, 0))] + [full(a) for a in extra[1:]]
        args += extra
    outs = tuple(jax.ShapeDtypeStruct((b, t, c), F32 if i in (3, 4) else BF16) for i in range(11))
    return pl.pallas_call(
        functools.partial(_rw_prep_kernel, has_vres=has_vres, c=c),
        out_shape=outs,
        grid=(b, t // tt),
        in_specs=in_specs,
        out_specs=tuple(pl.BlockSpec((1, tt, c), lambda i, j: (i, j, 0)) for _ in range(11)),
        compiler_params=_params(("parallel", "parallel")),
        name="rw_prep",
    )(*args)


def _scan_chunks(s0, r, k, v, kk, a, lw, rev, bdm):
    L = CHUNK
    n = len(r)
    each = lambda f, *ls: [f(*xs) for xs in zip(*ls)]
    ti = lax.broadcasted_iota(jnp.int32, (L, L), 0)
    si = lax.broadcasted_iota(jnp.int32, (L, L), 1)
    tt = lax.broadcasted_iota(jnp.int32, (L, GW), 0)
    ss = lax.broadcasted_iota(jnp.int32, (L, GW), 1) % L
    eye = (ss == tt).astype(F32)
    tri = [((si >= ti) if q else (si <= ti)).astype(F32).astype(BF16) for q in rev]
    strict = [(ss > tt) if q else (ss < tt) for q in rev]
    incl = [(ss >= tt) if q else (ss <= tt) for q in rev]
    bdf = bdm.astype(F32)

    def bd(x):
        xb = x.astype(BF16)
        return jnp.concatenate([xb] * HEADS_PER_GROUP, axis=0) * bdm

    def bd_t(x):
        xt = x.T.astype(BF16)
        return jnp.concatenate([xt] * HEADS_PER_GROUP, axis=1) * bdm

    def rcmul(x, y):
        return _dot(x.astype(BF16), bd(y))

    parts = each(_split3, lw)
    cum = [_dot(t, p[0]) + _dot(t, p[1]) + _dot(t, p[2]) for t, p in zip(tri, parts)]
    tot = [c[0:1, :] if q else c[L - 1:L, :] for c, q in zip(cum, rev)]
    p_in = each(jnp.exp, cum)
    p_inv = each(lambda c: jnp.exp(-c), cum)
    p_ex = each(lambda c, w: jnp.exp(c - w), cum, lw)
    p_rem = each(lambda t, c: jnp.exp(t - c), tot, cum)
    beta = each(lambda x, y: x * y, kk, a)
    a_t = each(lambda x, p: -x * p, kk, p_ex)
    r_t = each(lambda x, p: x * p, r, p_in)
    k_t = each(lambda x, p: x * p, k, p_inv)
    b_t = each(lambda x, p: x * p, beta, p_inv)
    k_h = each(lambda x, p: x * p, k, p_rem)
    b_h = each(lambda x, p: x * p, beta, p_rem)

    lhs = each(lambda x, y: jnp.concatenate([x, y], axis=0).astype(BF16), a_t, r_t)
    akk = each(lambda x, y: _dot(x, bd_t(y)), lhs, k_t)
    abb = each(lambda x, y: _dot(x, bd_t(y)), lhs, b_t)
    a_ak = each(lambda m, x: jnp.where(m, x[:L], 0.0), strict, akk)
    a_rk = each(lambda m, x: jnp.where(m, x[L:], 0.0), incl, akk)
    a_ab = each(lambda m, x: jnp.where(m, x[:L], 0.0), strict, abb)
    a_rb = each(lambda m, x: jnp.where(m, x[L:], 0.0), incl, abb)

    same = lambda size: (tt // size) == (ss // size)
    d = each(lambda x: jnp.where(same(INV_BASE), x, 0.0), a_ab)
    tinv = each(lambda x: eye + x, d)
    pw = each(rcmul, d, d)
    size = 4
    while size < INV_BASE:
        both = each(lambda p, t: rcmul(jnp.concatenate([p, t], axis=0), p), pw, tinv)
        pw = [x[:L] for x in both]
        tinv = each(lambda t, x: t + x[L:], tinv, both)
        size *= 2
    tinv = each(lambda t, p: t + rcmul(t, p), tinv, pw)
    size = INV_BASE
    while size < L:
        size *= 2
        couple = same(size) & ~same(size // 2)
        e = each(lambda x: jnp.where(couple, x, 0.0), a_ab)
        tinv = each(lambda t, y: t + rcmul(rcmul(t, y), t), tinv, e)

    av = each(lambda x, y, z: rcmul(jnp.concatenate([x, y], axis=0), z), a_ak, a_rk, v)
    ct = each(rcmul, a_rb, tinv)
    wu2 = each(lambda t, c, x, y: _dot(jnp.concatenate([t, c], axis=0).astype(BF16),
                                       jnp.concatenate([bd(x), bd(y[:L])], axis=1)), tinv, ct, a_t, av)
    wu = [x[:L] for x in wu2]
    rbwu = [x[L:] for x in wu2]
    o = each(lambda x, w, s, y: _dot((x + w[:, :GW]).astype(BF16), bd_t(s)) + y[L:] + w[:, GW:],
             r_t, rbwu, s0, av)

    m_bd = each(lambda w, x: _dot_tn(w[:, :GW].astype(BF16), x.astype(BF16)) * bdf, wu, b_h)
    n_full = each(lambda x, w, y, z: _dot_tn(jnp.concatenate([x, w[:, GW:]], axis=0).astype(BF16),
                                             jnp.concatenate([y, z], axis=0).astype(BF16)) * bdf,
                  v, wu, k_h, b_h)
    s1 = []
    for i in range(n):
        n_rc = n_full[i][0:HEAD]
        for h in range(1, HEADS_PER_GROUP):
            n_rc = n_rc + n_full[i][h * HEAD:(h + 1) * HEAD]
        s1.append(_dot(s0[i].astype(BF16), m_bd[i].astype(BF16)) + s0[i] * jnp.exp(tot[i]) + n_rc)
    return s1, o


def _scan_kernel(rf_ref, kf_ref, vf_ref, kkf_ref, af_ref, lwf_ref,
                 rb_ref, kb_ref, vb_ref, kkb_ref, ab_ref, lwb_ref, of_ref, ob_ref, s_ref, *, ng, nb):
    @pl.when(pl.program_id(1) == 0)
    def _():
        s_ref[...] = jnp.zeros_like(s_ref)

    gi = lax.broadcasted_iota(jnp.int32, (GW, GW), 0) // HEAD
    gj = lax.broadcasted_iota(jnp.int32, (GW, GW), 1) // HEAD
    bdm = (gi == gj).astype(F32).astype(BF16)
    dirs = ((rf_ref, kf_ref, vf_ref, kkf_ref, af_ref, lwf_ref), (rb_ref, kb_ref, vb_ref, kkb_ref, ab_ref, lwb_ref))
    chains = [(i, d, h) for i in range(nb) for d in range(2) for h in range(ng)]
    ins = [[dirs[d][q][i, :, h * GW:(h + 1) * GW].astype(F32) for i, d, h in chains] for q in range(6)]
    s1, o = _scan_chunks([s_ref[i, d, h] for i, d, h in chains], *ins, [d == 1 for _, d, _ in chains], bdm)
    for (i, d, h), s_new, o_new in zip(chains, s1, o):
        (of_ref, ob_ref)[d][i, :, h * GW:(h + 1) * GW] = o_new
        s_ref[i, d, h] = s_new


SCAN_BATCH_ROWS = 4


def _wkv_scan(r, v, kk, kf, af, lwf, kb, ab, lwb):
    b, t, c = r.shape
    nc = t // CHUNK
    ng = c // GW
    nb = math.gcd(b, SCAN_BATCH_ROWS)
    fspec = pl.BlockSpec((nb, CHUNK, c), lambda i, j: (i, j, 0))
    bspec = pl.BlockSpec((nb, CHUNK, c), lambda i, j: (i, nc - 1 - j, 0))
    return pl.pallas_call(
        functools.partial(_scan_kernel, ng=ng, nb=nb),
        out_shape=(jax.ShapeDtypeStruct((b, t, c), F32), jax.ShapeDtypeStruct((b, t, c), F32)),
        grid=(b // nb, nc),
        in_specs=[fspec] * 6 + [bspec] * 6,
        out_specs=(fspec, bspec),
        scratch_shapes=[pltpu.VMEM((nb, 2, ng, HEAD, GW), F32)],
        compiler_params=_params(("parallel", "arbitrary")),
        name="rw_scan",
    )(r, kf, v, kk, af, lwf, r, kb, v, kk, ab, lwb)


def _out_proj_kernel(yh_ref, sf_ref, sb_ref, bon_ref, g_ref, lg_ref, lb_ref, avg_ref, wh_ref, wr_ref, r_ref,
                     o_ref):
    s = sf_ref[...] + sb_ref[...]
    mean = _dot3(s, avg_ref[...])
    d = s - mean
    var = _dot((d * d).astype(BF16), avg_ref[...])
    y = d * lax.rsqrt(var + GN_EPS) * lg_ref[...] + lb_ref[...] + bon_ref[...].astype(F32)
    yr = (y * g_ref[...].astype(F32)).astype(BF16)
    o_ref[...] = _dot(yh_ref[...].astype(BF16), wh_ref[...]) + _dot(yr, wr_ref[...]) + r_ref[...]


def _out_proj(yh, sf, sb, bon, g, lnx_g, lnx_b, w, res, tm=1024):
    m, ch = yh.shape
    cr = sf.shape[1]
    d = w.shape[1]
    tm = min(tm, m)
    head = jnp.arange(cr) // HEAD
    avg = ((head[:, None] == head[None, :]).astype(F32) / HEAD).astype(BF16)
    row = lambda c: pl.BlockSpec((tm, c), lambda i: (i, 0))
    return pl.pallas_call(
        _out_proj_kernel,
        out_shape=jax.ShapeDtypeStruct((m, d), F32),
        grid=(m // tm,),
        in_specs=[row(ch), row(cr), row(cr), row(cr), row(cr), _const_spec((1, cr)), _const_spec((1, cr)),
                  _const_spec((cr, cr)), _const_spec((ch, d)), _const_spec((cr, d)), row(d)],
        out_specs=row(d),
        compiler_params=_params(("parallel",)),
        name="out_proj",
    )(yh, sf, sb, bon, g, lnx_g.reshape(1, cr), lnx_b.reshape(1, cr), avg, w[:ch], w[ch:], res)


def _rwkv_branch(p, v_first, shift, w0, w2, a0, a2, g2, k_k, k_a, r_k, vres, c):
    r, v, kk, lwf, lwb, kf, kb, af, ab, g, bon = _rw_prep(
        p, v_first, shift, w0, w2, a0, a2, g2, k_k, k_a, r_k, vres, c)
    sf, sb = _wkv_scan(r, v, kk, kf, af, lwf, kb, ab, lwb)
    return sf, sb, bon, g, (v if vres is None else v_first)


def _forward(x, norm1_g, w_in, hy_conv_w, hy_conv_b, hy_f_w1, hy_f_b1, hy_f_w2, hy_f_b2, hy_f_w3, hy_f_b3,
             hy_f_freq, hy_f_wout, hy_skip, hy_norm_g, rw_shift, rw_w0, rw_w2, rw_a0, rw_a2, rw_g2, rw_k_k,
             rw_k_a, rw_r_k, rw_lnx_g, rw_lnx_b, rw_v0, rw_v1, rw_v2, w_out, norm2_g, mlp_w1, mlp_w2,
             final_g):
    b, t, d = x.shape
    depth = w_in.shape[0]
    c_hy = hy_norm_g.shape[1]
    c_rw = rw_lnx_g.shape[1]
    hy_proj = hy_conv_b.shape[1]
    assert b % 2 == 0 and t % CHUNK == 0 and CHUNK == HEAD
    tabs = _dft_tables(t)
    assert tabs["n2"] % SUB16 == 0 and tabs["n1"] % SUB16 == 0 and tabs["n1h"] % SUB == 0
    v_first = None
    for l in range(depth):
        w_l = w_in[l].astype(BF16)
        w_rw = w_l[:, hy_proj:]
        vres = None if l == 0 else (rw_v0[l - 1], rw_v1[l - 1], rw_v2[l - 1])
        if vres is not None:
            v1 = vres[1].astype(BF16)
            w_rw = jnp.concatenate([w_rw, jnp.pad(v1, ((0, 0), (0, -v1.shape[1] % 128)))], axis=1)
        p_hy, p_rw = _in_proj(x, norm1_g[l], w_l[:, :hy_proj], w_rw, tabs["n1h"], tabs["n2"])
        y_hy = _hyena_branch(p_hy, hy_conv_w[l], hy_conv_b[l], hy_f_w1[l], hy_f_b1[l], hy_f_w2[l], hy_f_b2[l],
                             hy_f_w3[l], hy_f_b3[l], hy_f_freq[l], hy_f_wout[l], hy_skip[l], hy_norm_g[l], tabs)
        sf, sb, bon, g, v_first = _rwkv_branch(p_rw, v_first, rw_shift[l], rw_w0[l], rw_w2[l], rw_a0[l], rw_a2[l],
                                               rw_g2[l], rw_k_k[l], rw_k_a[l], rw_r_k[l], vres, c_rw)
        flat = lambda a: a.reshape(b * t, a.shape[-1])
        x = _out_proj(flat(y_hy), flat(sf), flat(sb), flat(bon), flat(g), rw_lnx_g[l], rw_lnx_b[l],
                      w_out[l].astype(BF16), flat(x))
        x = _mlp(x, norm2_g[l], mlp_w1[l].astype(BF16), mlp_w2[l].astype(BF16),
                 final_g=final_g if l == depth - 1 else None).reshape(b, t, d)
    return x


def kernel(x, norm1_g, w_in, hy_conv_w, hy_conv_b, hy_f_w1, hy_f_b1, hy_f_w2, hy_f_b2, hy_f_w3, hy_f_b3,
           hy_f_freq, hy_f_wout, hy_skip, hy_norm_g, rw_shift, rw_w0, rw_w2, rw_a0, rw_a2, rw_g2, rw_k_k,
           rw_k_a, rw_r_k, rw_lnx_g, rw_lnx_b, rw_v0, rw_v1, rw_v2, w_out, norm2_g, mlp_w1, mlp_w2, final_g):
    return _forward(x, norm1_g, w_in, hy_conv_w, hy_conv_b, hy_f_w1, hy_f_b1, hy_f_w2, hy_f_b2, hy_f_w3,
                    hy_f_b3, hy_f_freq, hy_f_wout, hy_skip, hy_norm_g, rw_shift, rw_w0, rw_w2, rw_a0, rw_a2,
                    rw_g2, rw_k_k, rw_k_a, rw_r_k, rw_lnx_g, rw_lnx_b, rw_v0, rw_v1, rw_v2, w_out, norm2_g,
                    mlp_w1, mlp_w2, final_g)
```

```python
import functools
import math

import jax
import jax.numpy as jnp
import numpy as np
from jax import lax
from jax.experimental import pallas as pl
from jax.experimental.pallas import tpu as pltpu

F32 = jnp.float32
BF16 = jnp.bfloat16
HP = lax.Precision.HIGHEST

HEAD = 64
HEADS_PER_GROUP = 4
GW = HEAD * HEADS_PER_GROUP
CHUNK = 64
INV_BASE = 8
RMS_EPS = 1e-5
GN_EPS = HEAD * 1e-5
HY_TARGET = 1e-2
HY_FAST_PCT = 0.3
HY_SLOW_PCT = 1.5
VMEM_LIMIT = 56 * 1024 * 1024


def _params(sem, vmem=VMEM_LIMIT):
    return pltpu.CompilerParams(dimension_semantics=sem, vmem_limit_bytes=vmem)


def _dot(a, b, precision=None):
    return jnp.dot(a, b, preferred_element_type=F32, precision=precision)


def _dot_tn(a, b, precision=None):
    return lax.dot_general(a, b, (((0,), (0,)), ((), ())), preferred_element_type=F32,
                           precision=precision)


def _const_spec(shape):
    return pl.BlockSpec(shape, lambda *_: (0,) * len(shape), pipeline_mode=pl.Buffered(1))


def _split3(x):
    h1 = x.astype(BF16)
    r1 = x - h1.astype(F32)
    h2 = r1.astype(BF16)
    h3 = (r1 - h2.astype(F32)).astype(BF16)
    return h1, h2, h3


def _dot3(x, m):
    h1, h2, h3 = _split3(x)
    return _dot(h1, m) + _dot(h2, m) + _dot(h3, m)


def _mlp_kernel(x_ref, g_ref, w1_ref, w2_ref, *rest, nchunk):
    x = x_ref[...]
    hn = (x * lax.rsqrt(jnp.mean(x * x, axis=-1, keepdims=True) + RMS_EPS) * g_ref[...]).astype(BF16)
    ff = w1_ref.shape[1]
    cw = ff // nchunk
    acc = x
    for j in range(nchunk):
        h = _dot(hn, w1_ref[:, j * cw:(j + 1) * cw])
        h = jnp.square(jnp.maximum(h, 0.0)).astype(BF16)
        acc = acc + _dot(h, w2_ref[j * cw:(j + 1) * cw, :])
    if len(rest) == 2:
        gf_ref, o_ref = rest
        acc = acc * lax.rsqrt(jnp.mean(acc * acc, axis=-1, keepdims=True) + RMS_EPS) * gf_ref[...]
    else:
        (o_ref,) = rest
    o_ref[...] = acc


def _mlp(x2d, g, w1, w2, final_g=None, tm=1024, nchunk=4):
    m, d = x2d.shape
    ff = w1.shape[1]
    tm = min(tm, m)
    in_specs = [pl.BlockSpec((tm, d), lambda i: (i, 0)), _const_spec((1, d)),
                _const_spec((d, ff)), _const_spec((ff, d))]
    args = [x2d, g.reshape(1, d), w1, w2]
    if final_g is not None:
        in_specs.append(_const_spec((1, d)))
        args.append(final_g.reshape(1, d))
    return pl.pallas_call(
        functools.partial(_mlp_kernel, nchunk=nchunk),
        out_shape=jax.ShapeDtypeStruct((m, d), F32),
        grid=(m // tm,),
        in_specs=in_specs,
        out_specs=pl.BlockSpec((tm, d), lambda i: (i, 0)),
        compiler_params=_params(("parallel",)),
        name="mlp",
    )(*args)


SUB = 8
SUB16 = 16
DFT_GROUP = 16


def _rms(x, g):
    return (x * lax.rsqrt(jnp.mean(x * x, axis=-1, keepdims=True) + RMS_EPS) * g).astype(BF16)


def _in_proj_kernel(x_ref, g_ref, wh_ref, wr_ref, oh_ref, or_ref, *, n2):
    hn = _rms(x_ref[0], g_ref[...])
    or_ref[0] = _dot(hn, wr_ref[...]).astype(or_ref.dtype)
    ph = _dot(hn, wh_ref[...])
    for j in range(SUB):
        oh_ref[0, :, j, :] = ph[j * n2:(j + 1) * n2]


def _in_proj(x, g, w_hy, w_rw, n1h, n2):
    b, t, d = x.shape
    ph, pr = w_hy.shape[1], w_rw.shape[1]
    tm = SUB * n2
    p_hy, p_rw = pl.pallas_call(
        functools.partial(_in_proj_kernel, n2=n2),
        out_shape=(jax.ShapeDtypeStruct((b, n2, n1h, ph), F32), jax.ShapeDtypeStruct((b, t, pr), BF16)),
        grid=(b, t // tm),
        in_specs=[pl.BlockSpec((1, tm, d), lambda i, j: (i, j, 0)), _const_spec((1, d)),
                  _const_spec((d, ph)), _const_spec((d, pr))],
        out_specs=(pl.BlockSpec((1, n2, SUB, ph), lambda i, j: (i, 0, j, 0)),
                   pl.BlockSpec((1, tm, pr), lambda i, j: (i, j, 0))),
        compiler_params=_params(("parallel", "parallel")),
        name="in_proj",
    )(x, g.reshape(1, d), w_hy, w_rw)
    return p_hy.reshape(b, t, ph), p_rw


def _shortconv_kernel(p_ref, w_ref, b_ref, o_ref, *, n1h, n2):
    w0 = w_ref[0:1, :]
    w1 = w_ref[1:2, :]
    w2 = w_ref[2:3, :]
    bias = b_ref[...]
    rows = lax.broadcasted_iota(jnp.int32, (n1h, 1), 0)

    def blk(i):
        return p_ref[0, pl.ds(pl.multiple_of(i * n1h, n1h), n1h), :].astype(F32)

    def body(i, carry):
        o_ref[0, pl.ds(pl.multiple_of(i * n1h, n1h), n1h), :] = (
            w0 * blk(i - 1) + w1 * blk(i) + w2 * blk(i + 1) + bias).astype(o_ref.dtype)
        return carry

    lax.fori_loop(1, n2 - 1, body, 0)
    last = p_ref[0, (n2 - 1) * n1h:, :].astype(F32)
    first = p_ref[0, :n1h, :].astype(F32)
    prev0 = jnp.where(rows == 0, 0.0, pltpu.roll(last, 1, 0))
    o_ref[0, :n1h, :] = (w0 * prev0 + w1 * first + w2 * p_ref[0, n1h:2 * n1h, :].astype(F32)
                         + bias).astype(o_ref.dtype)
    nxt = jnp.where(rows == n1h - 1, 0.0, pltpu.roll(first, n1h - 1, 0))
    o_ref[0, (n2 - 1) * n1h:, :] = (w0 * p_ref[0, (n2 - 2) * n1h:(n2 - 1) * n1h, :].astype(F32) + w1 * last
                                    + w2 * nxt + bias).astype(o_ref.dtype)


def _shortconv(p, w, bias, n1h, n2, cb=512):
    b, t, c = p.shape
    return pl.pallas_call(
        functools.partial(_shortconv_kernel, n1h=n1h, n2=n2),
        out_shape=jax.ShapeDtypeStruct((b, t, c), BF16),
        grid=(b, c // cb),
        in_specs=[pl.BlockSpec((1, t, cb), lambda i, j: (i, 0, j)),
                  pl.BlockSpec((3, cb), lambda i, j: (0, j)),
                  pl.BlockSpec((1, cb), lambda i, j: (0, j))],
        out_specs=pl.BlockSpec((1, t, cb), lambda i, j: (i, 0, j)),
        compiler_params=_params(("parallel", "parallel")),
        name="hy_shortconv",
    )(p, w, bias.reshape(1, c))


def _dft_tables(t):
    n = 2 * t
    n1 = 1 << ((n.bit_length() - 1) // 2)
    n2 = n // n1
    n1h = n1 // 2
    k1 = np.arange(n1)
    n2i = np.arange(n2)

    def cs(prod, mod):
        ang = (2.0 * np.pi / mod) * (prod % mod)
        return np.cos(ang), np.sin(ang)

    tposf = n2i[:, None] + n2 * np.arange(n1)[None, :]
    cf, sf = cs(k1[None, :, None] * tposf[:, None, :], n)
    g1f = np.concatenate([cf, -sf], axis=1)
    c, s = cf[:, :, :n1h], sf[:, :, :n1h]
    g1 = np.concatenate([np.concatenate([c, s], axis=2), np.concatenate([-s, c], axis=2)], axis=1)
    ct, st = np.swapaxes(c, 1, 2) / n, np.swapaxes(s, 1, 2) / n
    g4a = np.concatenate([ct, st], axis=1)
    g4b = np.concatenate([-st, ct], axis=1)
    c2, s2 = cs(n2i[:, None] * n2i[None, :], n2)
    f2a = np.concatenate([c2, -s2], axis=0)
    f2b = np.concatenate([s2, c2], axis=0)
    f3 = np.concatenate([np.concatenate([c2, -s2], axis=1), np.concatenate([s2, c2], axis=1)], axis=0)
    g4 = np.concatenate([g4a, g4b], axis=2)
    f2 = np.concatenate([f2a, f2b], axis=1)
    const = lambda a: jnp.asarray(a.astype(np.float32).astype(BF16))
    return dict(n1=n1, n2=n2, n1h=n1h, g1=const(g1), g4=const(g4), g1f=const(g1f), f2=const(f2), f3=const(f3))


def _filter_mlp_kernel(z_ref, w1_ref, b1_ref, w2_ref, b2_ref, w3_ref, b3_ref, fr_ref, wo_ref, dl_ref,
                       o_ref, *, t):
    z = z_ref[...]
    h = jnp.sin(fr_ref[0:1, :] * (_dot(z, w1_ref[...], HP) + b1_ref[...]))
    h = jnp.sin(fr_ref[1:2, :] * (_dot(h, w2_ref[...], HP) + b2_ref[...]))
    h = jnp.sin(fr_ref[2:3, :] * (_dot(h, w3_ref[...], HP) + b3_ref[...]))
    window = jnp.exp(-z[:, 0:1] * dl_ref[...])
    tr = z.shape[0]
    row = pl.program_id(0) * tr + lax.broadcasted_iota(jnp.int32, (tr, 1), 0)
    o_ref[...] = jnp.where(row == t, 0.0, _dot(h, wo_ref[0], HP) * window)


def _filter_taps(t, w1, b1, w2, b2, w3, b3, freq, w_out, c_hy, tr=1024):
    emb, width = w1.shape
    bands = (emb - 1) // 2
    pos = np.arange(t, dtype=np.float32)
    tt = pos / np.float32(max(t - 1, 1))
    fr = np.linspace(1e-4, bands - 1, bands, dtype=np.float32)
    ang = np.float32(2.0 * math.pi / t) * pos[:, None] * fr[None, :]
    z = np.concatenate([tt[:, None], np.cos(ang), -np.sin(ang)], axis=-1)
    z = np.pad(z, ((0, 0), (0, 128 - emb)))
    w1 = jnp.pad(w1, ((0, 128 - emb), (0, 0)))
    emb = 128
    z2 = jnp.asarray(np.concatenate([z, z[:1], z[1:][::-1]], axis=0))
    max_decay = math.log(HY_TARGET) / HY_FAST_PCT
    min_decay = math.log(HY_TARGET) / HY_SLOW_PCT
    delta = jnp.abs(jnp.linspace(min_decay, max_decay, c_hy, dtype=F32))
    order = w_out.shape[1] // (2 * c_hy)
    nout = order * c_hy
    w_dir = w_out.reshape(width, order, 2, c_hy).transpose(2, 0, 1, 3).reshape(2, width, nout)
    delta_full = jnp.tile(delta, order).reshape(1, nout)
    tr = min(tr, t)
    nt = t // tr
    full = lambda a: pl.BlockSpec(a.shape, lambda i: (0,) * a.ndim)
    args = [z2, w1, b1.reshape(1, -1), w2, b2.reshape(1, -1), w3, b3.reshape(1, -1), freq, w_dir, delta_full]
    in_specs = [pl.BlockSpec((tr, emb), lambda i: (i, 0))] + [full(a) for a in args[1:]]
    in_specs[8] = pl.BlockSpec((1, width, nout), lambda i: (i // nt, 0, 0))
    return pl.pallas_call(
        functools.partial(_filter_mlp_kernel, t=t),
        out_shape=jax.ShapeDtypeStruct((2 * t, nout), F32),
        grid=(2 * nt,),
        in_specs=in_specs,
        out_specs=pl.BlockSpec((tr, nout), lambda i: (i, 0)),
        compiler_params=_params(("parallel",)),
        name="hy_filter_mlp",
    )(*args)


def _filter_s1_kernel(x_ref, g_ref, o_ref, ss_ref, *, g, oc):
    @pl.when(pl.program_id(0) == 0)
    def _():
        ss_ref[...] = jnp.zeros_like(ss_ref)

    n1 = x_ref.shape[0]
    x = jnp.concatenate([x_ref[:, j, :] for j in range(g)], axis=0)
    ss_ref[...] += jnp.sum(x * x, axis=0, keepdims=True)
    xb = x.astype(BF16)
    for j in range(g):
        o_ref[j] = _dot(g_ref[j], xb[j * n1:(j + 1) * n1])


def _filter_s2_kernel(sr_ref, si_ref, f2_ref, ss_ref, o_ref, *, g, oc):
    n2 = sr_ref.shape[0]
    scale = lax.rsqrt(ss_ref[...] + 1e-6)
    s = jnp.concatenate([ref[:, j, :] for j in range(g) for ref in (sr_ref, si_ref)], axis=0).astype(BF16)
    for j in range(g):
        o_ref[j] = _dot(f2_ref[...], s[2 * j * n2:2 * (j + 1) * n2]) * scale


def _filter_spectrum(two, tabs):
    n, oc = two.shape
    n1, n2 = tabs["n1"], tabs["n2"]
    g = SUB
    s1, ss = pl.pallas_call(
        functools.partial(_filter_s1_kernel, g=g, oc=oc),
        out_shape=(jax.ShapeDtypeStruct((n2, 2 * n1, oc), F32), jax.ShapeDtypeStruct((1, oc), F32)),
        grid=(n2 // g,),
        in_specs=[pl.BlockSpec((n1, g, oc), lambda j: (0, j, 0)),
                  pl.BlockSpec((g, 2 * n1, n1), lambda j: (j, 0, 0))],
        out_specs=(pl.BlockSpec((g, 2 * n1, oc), lambda j: (j, 0, 0)),
                   pl.BlockSpec((1, oc), lambda j: (0, 0))),
        compiler_params=_params(("arbitrary",)),
        name="hy_filter_dft1",
    )(two.reshape(n1, n2, oc), tabs["g1f"])
    return pl.pallas_call(
        functools.partial(_filter_s2_kernel, g=g, oc=oc),
        out_shape=jax.ShapeDtypeStruct((n1, 2 * n2, oc), F32),
        grid=(n1 // g,),
        in_specs=[pl.BlockSpec((n2, g, oc), lambda k: (0, k, 0)),
                  pl.BlockSpec((n2, g, oc), lambda k: (0, n1 // g + k, 0)),
                  pl.BlockSpec((2 * n2, 2 * n2), lambda k: (0, 0)),
                  pl.BlockSpec((1, oc), lambda k: (0, 0))],
        out_specs=pl.BlockSpec((g, 2 * n2, oc), lambda k: (k, 0, 0)),
        compiler_params=_params(("parallel",)),
        name="hy_filter_dft2",
    )(s1, s1, tabs["f2"], ss)


def _conv_s1_kernel(u_ref, g_ref, o_ref, t_ref, *, g, n1h):
    for j0 in range(0, g, DFT_GROUP):
        js = range(j0, j0 + DFT_GROUP)
        rs = [_dot(g_ref[j], jnp.concatenate([u_ref[0, 0, j * n1h:(j + 1) * n1h, :],
                                               u_ref[0, 1, j * n1h:(j + 1) * n1h, :]], axis=0)) for j in js]
        for j, r in zip(js, rs):
            t_ref[:, j, :] = r
    o_ref[0] = t_ref[...].astype(o_ref.dtype)


def _conv_s23_kernel(sr_ref, si_ref, f2_ref, f3_ref, h_ref, o_ref, t_ref, *, g, n2):
    for j0 in range(0, g, DFT_GROUP):
        js = range(j0, j0 + DFT_GROUP)
        xs = [_dot(f2_ref[...], jnp.concatenate([sr_ref[0, j], si_ref[0, j]], axis=0)) for j in js]
        ys = []
        for j, x in zip(js, xs):
            xr, xi = x[:n2], x[n2:]
            hr, hi = h_ref[j, :n2, :], h_ref[j, n2:, :]
            ys.append(jnp.concatenate([xr * hr - xi * hi, xr * hi + xi * hr], axis=0).astype(BF16))
        rs = [_dot(f3_ref[...], y) for y in ys]
        for j, r in zip(js, rs):
            t_ref[:, j, :] = r
    o_ref[0] = t_ref[...].astype(o_ref.dtype)


def _conv_s4_kernel(rr_ref, ri_ref, g4_ref, u_ref, gate_ref, skip_ref, *rest, g, n1h, norm):
    if norm:
        ng_ref, avg_ref, o_ref = rest
    else:
        (o_ref,) = rest
    skip = skip_ref[...]
    ys = [_dot(g4_ref[j], jnp.concatenate([rr_ref[0, j], ri_ref[0, j]], axis=0)) for j in range(g)]
    zs = [gate_ref[0, q, j * n1h:(j + 1) * n1h, :].astype(F32)
          * (ys[j][q * n1h:(q + 1) * n1h] + u_ref[0, q, j * n1h:(j + 1) * n1h, :].astype(F32) * skip)
          for j in range(g) for q in range(2)]
    if norm:
        ms = [_dot((z * z).astype(BF16), avg_ref[...]) for z in zs]
        zs = [z * lax.rsqrt(m + RMS_EPS) * ng_ref[...] for z, m in zip(zs, ms)]
    for j in range(g):
        for q in range(2):
            if norm:
                o_ref[0, q, :, j, :] = zs[2 * j + q]
            else:
                o_ref[0, q, j * n1h:(j + 1) * n1h, :] = zs[2 * j + q].astype(o_ref.dtype)


def _long_conv_gate(u_arr, u_col, gate_arr, gate_col, h_spec, h_col, skip, tabs, c, norm_g=None):
    b, t, _ = u_arr.shape
    n1, n2, n1h = tabs["n1"], tabs["n2"], tabs["n1h"]
    g = SUB16
    npair = b // 2
    u4 = u_arr.reshape(npair, 2, t, u_arr.shape[2])
    gate4 = gate_arr.reshape(npair, 2, t, gate_arr.shape[2])
    gt = SUB16
    s1 = pl.pallas_call(
        functools.partial(_conv_s1_kernel, g=gt, n1h=n1h),
        out_shape=jax.ShapeDtypeStruct((npair, 2 * n1, n2, c), BF16),
        grid=(n2 // gt, npair),
        in_specs=[pl.BlockSpec((1, 2, gt * n1h, c), lambda j, p: (p, 0, j, u_col)),
                  pl.BlockSpec((gt, 2 * n1, n1), lambda j, p: (j, 0, 0))],
        out_specs=pl.BlockSpec((1, 2 * n1, gt, c), lambda j, p: (p, 0, j, 0)),
        scratch_shapes=[pltpu.VMEM((2 * n1, gt, c), F32)],
        compiler_params=_params(("parallel", "parallel")),
        name="hy_conv_dft1",
    )(u4, tabs["g1"])
    r = pl.pallas_call(
        functools.partial(_conv_s23_kernel, g=gt, n2=n2),
        out_shape=jax.ShapeDtypeStruct((npair, 2 * n2, n1, c), BF16),
        grid=(n1 // gt, npair),
        in_specs=[pl.BlockSpec((1, gt, n2, c), lambda k, p: (p, k, 0, 0)),
                  pl.BlockSpec((1, gt, n2, c), lambda k, p: (p, n1 // gt + k, 0, 0)),
                  pl.BlockSpec((2 * n2, 2 * n2), lambda k, p: (0, 0)),
                  pl.BlockSpec((2 * n2, 2 * n2), lambda k, p: (0, 0)),
                  pl.BlockSpec((gt, 2 * n2, c), lambda k, p: (k, 0, h_col))],
        out_specs=pl.BlockSpec((1, 2 * n2, gt, c), lambda k, p: (p, 0, k, 0)),
        scratch_shapes=[pltpu.VMEM((2 * n2, gt, c), F32)],
        compiler_params=_params(("parallel", "parallel")),
        name="hy_conv_dft23",
    )(s1, s1, tabs["f2"], tabs["f3"], h_spec)
    norm = norm_g is not None
    in_specs = [pl.BlockSpec((1, g, n1, c), lambda p, j: (p, j, 0, 0)),
                pl.BlockSpec((1, g, n1, c), lambda p, j: (p, n2 // g + j, 0, 0)),
                pl.BlockSpec((g, n1, 2 * n1), lambda p, j: (j, 0, 0)),
                pl.BlockSpec((1, 2, g * n1h, c), lambda p, j: (p, 0, j, u_col)),
                pl.BlockSpec((1, 2, g * n1h, c), lambda p, j: (p, 0, j, gate_col)),
                pl.BlockSpec((1, c), lambda p, j: (0, 0))]
    args = [r, r, tabs["g4"], u4, gate4, skip.reshape(1, c)]
    if norm:
        ch = jnp.arange(c) // HEAD
        avg = ((ch[:, None] == ch[None, :]).astype(F32) / HEAD).astype(BF16)
        in_specs += [pl.BlockSpec((1, c), lambda p, j: (0, 0)), pl.BlockSpec((c, c), lambda p, j: (0, 0))]
        args += [norm_g.reshape(1, c), avg]
        out_shape = jax.ShapeDtypeStruct((npair, 2, n1h, n2, c), F32)
        out_spec = pl.BlockSpec((1, 2, n1h, g, c), lambda p, j: (p, 0, 0, j, 0))
    else:
        out_shape = jax.ShapeDtypeStruct((npair, 2, t, c), BF16)
        out_spec = pl.BlockSpec((1, 2, g * n1h, c), lambda p, j: (p, 0, j, 0))
    out = pl.pallas_call(
        functools.partial(_conv_s4_kernel, g=g, n1h=n1h, norm=norm),
        out_shape=out_shape,
        grid=(npair, n2 // g),
        in_specs=in_specs,
        out_specs=out_spec,
        compiler_params=_params(("parallel", "parallel")),
        name="hy_conv_dft4",
    )(*args)
    return out.reshape(b, t, c)


def _hyena_branch(p, conv_w, conv_b, fw1, fb1, fw2, fb2, fw3, fb3, ffreq, fwout, skip, norm_g, tabs):
    b, t, _ = p.shape
    c = norm_g.shape[0]
    n1, n2, n1h = tabs["n1"], tabs["n2"], tabs["n1h"]
    u = _shortconv(p, conv_w, conv_b, n1h, n2)
    two = _filter_taps(t, fw1, fb1, fw2, fb2, fw3, fb3, ffreq, fwout, c)
    hspec = _filter_spectrum(two, tabs)
    z = _long_conv_gate(u, 2, u, 0, hspec, 0, skip[0], tabs, c)
    return _long_conv_gate(z, 0, u, 1, hspec, 1, skip[1], tabs, c, norm_g=norm_g)


def _rw_prep_kernel(*refs, has_vres, c):
    if has_vres:
        (p_ref, pp_ref, pn_ref, mu_ref, w0_ref, w2_ref, a0_ref, a2_ref, g2_ref, kk_ref, ka_ref, rk_ref,
         sum_ref, vf_ref, v0_ref, v2_ref,
         r_o, v_o, kk_o, lwf_o, lwb_o, kf_o, kb_o, af_o, ab_o, g_o, bon_o) = refs
    else:
        (p_ref, pp_ref, pn_ref, mu_ref, w0_ref, w2_ref, a0_ref, a2_ref, g2_ref, kk_ref, ka_ref, rk_ref,
         sum_ref,
         r_o, v_o, kk_o, lwf_o, lwb_o, kf_o, kb_o, af_o, ab_o, g_o, bon_o) = refs
    i = pl.program_id(1)
    last = pl.num_programs(1) - 1
    nsh = mu_ref.shape[1]
    p = p_ref[0, :, :nsh].astype(F32)
    tt = p.shape[0]
    rows = lax.broadcasted_iota(jnp.int32, (tt, 1), 0)
    prev_row = jnp.where(i == 0, 0.0, pp_ref[0, SUB16 - 1:SUB16, :nsh].astype(F32))
    next_row = jnp.where(i == last, 0.0, pn_ref[0, 0:1, :nsh].astype(F32))
    prev = jnp.where(rows == 0, prev_row, pltpu.roll(p, 1, 0))
    nxt = jnp.where(rows == tt - 1, next_row, pltpu.roll(p, tt - 1, 0))
    mu0, mu1 = mu_ref[0:1, :], mu_ref[1:2, :]
    pf = p * (1.0 - mu0 - mu1) + mu0 * prev + mu1 * nxt
    r = pf[:, :c]
    k = pf[:, c:2 * c]
    v = pf[:, 2 * c:3 * c]
    lw = 3 * c
    nd = w2_ref.shape[0]
    na = a2_ref.shape[0]
    wd = pf[:, lw:lw + nd]
    ad = pf[:, lw + nd:lw + nd + na]
    gd = pf[:, lw + nd + na:]
    if has_vres:
        lora = _dot(p_ref[0, :, nsh:], v2_ref[...])
        v = v + (vf_ref[0].astype(F32) - v) * jax.nn.sigmoid(v0_ref[...] + lora)
    g = _dot(jax.nn.sigmoid(gd).astype(BF16), g2_ref[...])
    kk = k * kk_ref[...]
    kk = kk * lax.rsqrt(jnp.maximum(_dot((kk * kk).astype(BF16), sum_ref[...]), 1e-24))
    wl = w0_ref[...] + _dot(jnp.tanh(wd).astype(BF16), w2_ref[...])
    logw = -math.exp(-0.5) * jax.nn.sigmoid(wl)
    a = jax.nn.sigmoid(a0_ref[...] + _dot(ad.astype(BF16), a2_ref[...]))
    ka = ka_ref[...]
    rk = rk_ref[...]
    k_d = [k * (1.0 + (a[:, d * c:(d + 1) * c] - 1.0) * ka) for d in range(2)]
    bon = _dot((r * (k_d[0] + k_d[1]) * rk).astype(BF16), sum_ref[...]) * v
    r_o[0] = r.astype(BF16)
    v_o[0] = v.astype(BF16)
    kk_o[0] = kk.astype(BF16)
    lwf_o[0] = logw[:, :c]
    lwb_o[0] = logw[:, c:]
    kf_o[0] = k_d[0].astype(BF16)
    kb_o[0] = k_d[1].astype(BF16)
    af_o[0] = a[:, :c].astype(BF16)
    ab_o[0] = a[:, c:].astype(BF16)
    g_o[0] = g.astype(BF16)
    bon_o[0] = bon.astype(BF16)


def _blockdiag2(m):
    k, c = m.shape[1], m.shape[2]
    z = jnp.zeros((k, c), m.dtype)
    return jnp.concatenate([jnp.concatenate([m[0], z], axis=1), jnp.concatenate([z, m[1]], axis=1)], axis=0)


def _rw_prep(p, v_first, shift, w0, w2, a0, a2, g2, k_k, k_a, r_k, vres, c, tt=1024):
    b, t, pw = p.shape
    tt = min(tt, t)
    has_vres = vres is not None
    ch = jnp.arange(c) // HEAD
    summ = (ch[:, None] == ch[None, :]).astype(BF16)
    row = lambda a: a.reshape(1, -1)
    args = [p, p, p, shift, row(w0), _blockdiag2(w2).astype(BF16), row(a0), _blockdiag2(a2).astype(BF16),
            g2.astype(BF16), row(k_k), row(k_a), row(r_k), summ]
    full = lambda a: pl.BlockSpec(a.shape, lambda i, j: (0,) * a.ndim)
    nhb = t // SUB16
    in_specs = [pl.BlockSpec((1, tt, pw), lambda i, j: (i, j, 0)),
                pl.BlockSpec((1, SUB16, pw), lambda i, j: (i, jnp.maximum(j * (tt // SUB16) - 1, 0), 0)),
                pl.BlockSpec((1, SUB16, pw), lambda i, j: (i, jnp.minimum((j + 1) * (tt // SUB16), nhb - 1), 0))]
    in_specs += [full(a) for a in args[3:]]
    if has_vres:
        v0, _, v2 = vres
        v2p = jnp.pad(v2, ((0, pw - shift.shape[1] - v2.shape[0]), (0, 0))).astype(BF16)
        extra = [v_first, row(v0), v2p]
        in_specs += [pl.BlockSpec((1, tt, c), lambda i, j: (i, j, 0))] + [full(a) for a in extra[1:]]
        args += extra
    outs = tuple(jax.ShapeDtypeStruct((b, t, c), F32 if i in (3, 4) else BF16) for i in range(11))
    return pl.pallas_call(
        functools.partial(_rw_prep_kernel, has_vres=has_vres, c=c),
        out_shape=outs,
        grid=(b, t // tt),
        in_specs=in_specs,
        out_specs=tuple(pl.BlockSpec((1, tt, c), lambda i, j: (i, j, 0)) for _ in range(11)),
        compiler_params=_params(("parallel", "parallel")),
        name="rw_prep",
    )(*args)


def _scan_chunks(s0, r, k, v, kk, a, lw, rev, bdm):
    L = CHUNK
    n = len(r)
    each = lambda f, *ls: [f(*xs) for xs in zip(*ls)]
    ti = lax.broadcasted_iota(jnp.int32, (L, L), 0)
    si = lax.broadcasted_iota(jnp.int32, (L, L), 1)
    tt = lax.broadcasted_iota(jnp.int32, (L, GW), 0)
    ss = lax.broadcasted_iota(jnp.int32, (L, GW), 1) % L
    eye = (ss == tt).astype(F32)
    tri = [((si >= ti) if q else (si <= ti)).astype(F32).astype(BF16) for q in rev]
    strict = [(ss > tt) if q else (ss < tt) for q in rev]
    incl = [(ss >= tt) if q else (ss <= tt) for q in rev]
    bdf = bdm.astype(F32)

    def bd(x):
        xb = x.astype(BF16)
        return jnp.concatenate([xb] * HEADS_PER_GROUP, axis=0) * bdm

    def bd_t(x):
        xt = x.T.astype(BF16)
        return jnp.concatenate([xt] * HEADS_PER_GROUP, axis=1) * bdm

    def rcmul(x, y):
        return _dot(x.astype(BF16), bd(y))

    parts = each(_split3, lw)
    cum = [_dot(t, p[0]) + _dot(t, p[1]) + _dot(t, p[2]) for t, p in zip(tri, parts)]
    tot = [c[0:1, :] if q else c[L - 1:L, :] for c, q in zip(cum, rev)]
    p_in = each(jnp.exp, cum)
    p_inv = each(lambda c: jnp.exp(-c), cum)
    p_ex = each(lambda c, w: jnp.exp(c - w), cum, lw)
    p_rem = each(lambda t, c: jnp.exp(t - c), tot, cum)
    beta = each(lambda x, y: x * y, kk, a)
    a_t = each(lambda x, p: -x * p, kk, p_ex)
    r_t = each(lambda x, p: x * p, r, p_in)
    k_t = each(lambda x, p: x * p, k, p_inv)
    b_t = each(lambda x, p: x * p, beta, p_inv)
    k_h = each(lambda x, p: x * p, k, p_rem)
    b_h = each(lambda x, p: x * p, beta, p_rem)

    lhs = each(lambda x, y: jnp.concatenate([x, y], axis=0).astype(BF16), a_t, r_t)
    akk = each(lambda x, y: _dot(x, bd_t(y)), lhs, k_t)
    abb = each(lambda x, y: _dot(x, bd_t(y)), lhs, b_t)
    a_ak = each(lambda m, x: jnp.where(m, x[:L], 0.0), strict, akk)
    a_rk = each(lambda m, x: jnp.where(m, x[L:], 0.0), incl, akk)
    a_ab = each(lambda m, x: jnp.where(m, x[:L], 0.0), strict, abb)
    a_rb = each(lambda m, x: jnp.where(m, x[L:], 0.0), incl, abb)

    same = lambda size: (tt // size) == (ss // size)
    d = each(lambda x: jnp.where(same(INV_BASE), x, 0.0), a_ab)
    tinv = each(lambda x: eye + x, d)
    pw = each(rcmul, d, d)
    size = 4
    while size < INV_BASE:
        both = each(lambda p, t: rcmul(jnp.concatenate([p, t], axis=0), p), pw, tinv)
        pw = [x[:L] for x in both]
        tinv = each(lambda t, x: t + x[L:], tinv, both)
        size *= 2
    tinv = each(lambda t, p: t + rcmul(t, p), tinv, pw)
    size = INV_BASE
    while size < L:
        size *= 2
        couple = same(size) & ~same(size // 2)
        e = each(lambda x: jnp.where(couple, x, 0.0), a_ab)
        tinv = each(lambda t, y: t + rcmul(rcmul(t, y), t), tinv, e)

    av = each(lambda x, y, z: rcmul(jnp.concatenate([x, y], axis=0), z), a_ak, a_rk, v)
    ct = each(rcmul, a_rb, tinv)
    wu2 = each(lambda t, c, x, y: _dot(jnp.concatenate([t, c], axis=0).astype(BF16),
                                       jnp.concatenate([bd(x), bd(y[:L])], axis=1)), tinv, ct, a_t, av)
    wu = [x[:L] for x in wu2]
    rbwu = [x[L:] for x in wu2]
    o = each(lambda x, w, s, y: _dot((x + w[:, :GW]).astype(BF16), bd_t(s)) + y[L:] + w[:, GW:],
             r_t, rbwu, s0, av)

    m_bd = each(lambda w, x: _dot_tn(w[:, :GW].astype(BF16), x.astype(BF16)) * bdf, wu, b_h)
    n_full = each(lambda x, w, y, z: _dot_tn(jnp.concatenate([x, w[:, GW:]], axis=0).astype(BF16),
                                             jnp.concatenate([y, z], axis=0).astype(BF16)) * bdf,
                  v, wu, k_h, b_h)
    s1 = []
    for i in range(n):
        n_rc = n_full[i][0:HEAD]
        for h in range(1, HEADS_PER_GROUP):
            n_rc = n_rc + n_full[i][h * HEAD:(h + 1) * HEAD]
        s1.append(_dot(s0[i].astype(BF16), m_bd[i].astype(BF16)) + s0[i] * jnp.exp(tot[i]) + n_rc)
    return s1, o


def _scan_kernel(rf_ref, kf_ref, vf_ref, kkf_ref, af_ref, lwf_ref,
                 rb_ref, kb_ref, vb_ref, kkb_ref, ab_ref, lwb_ref, of_ref, ob_ref, s_ref, *, ng, nb):
    @pl.when(pl.program_id(1) == 0)
    def _():
        s_ref[...] = jnp.zeros_like(s_ref)

    gi = lax.broadcasted_iota(jnp.int32, (GW, GW), 0) // HEAD
    gj = lax.broadcasted_iota(jnp.int32, (GW, GW), 1) // HEAD
    bdm = (gi == gj).astype(F32).astype(BF16)
    dirs = ((rf_ref, kf_ref, vf_ref, kkf_ref, af_ref, lwf_ref), (rb_ref, kb_ref, vb_ref, kkb_ref, ab_ref, lwb_ref))
    chains = [(i, d, h) for i in range(nb) for d in range(2) for h in range(ng)]
    ins = [[dirs[d][q][i, :, h * GW:(h + 1) * GW].astype(F32) for i, d, h in chains] for q in range(6)]
    s1, o = _scan_chunks([s_ref[i, d, h] for i, d, h in chains], *ins, [d == 1 for _, d, _ in chains], bdm)
    for (i, d, h), s_new, o_new in zip(chains, s1, o):
        (of_ref, ob_ref)[d][i, :, h * GW:(h + 1) * GW] = o_new
        s_ref[i, d, h] = s_new


SCAN_BATCH_ROWS = 4


def _wkv_scan(r, v, kk, kf, af, lwf, kb, ab, lwb):
    b, t, c = r.shape
    nc = t // CHUNK
    ng = c // GW
    nb = math.gcd(b, SCAN_BATCH_ROWS)
    fspec = pl.BlockSpec((nb, CHUNK, c), lambda i, j: (i, j, 0))
    bspec = pl.BlockSpec((nb, CHUNK, c), lambda i, j: (i, nc - 1 - j, 0))
    return pl.pallas_call(
        functools.partial(_scan_kernel, ng=ng, nb=nb),
        out_shape=(jax.ShapeDtypeStruct((b, t, c), F32), jax.ShapeDtypeStruct((b, t, c), F32)),
        grid=(b // nb, nc),
        in_specs=[fspec] * 6 + [bspec] * 6,
        out_specs=(fspec, bspec),
        scratch_shapes=[pltpu.VMEM((nb, 2, ng, HEAD, GW), F32)],
        compiler_params=_params(("parallel", "arbitrary")),
        name="rw_scan",
    )(r, kf, v, kk, af, lwf, r, kb, v, kk, ab, lwb)


def _out_proj_kernel(yh_ref, sf_ref, sb_ref, bon_ref, g_ref, lg_ref, lb_ref, avg_ref, wh_ref, wr_ref, r_ref,
                     o_ref):
    s = sf_ref[...] + sb_ref[...]
    mean = _dot3(s, avg_ref[...])
    d = s - mean
    var = _dot((d * d).astype(BF16), avg_ref[...])
    y = d * lax.rsqrt(var + GN_EPS) * lg_ref[...] + lb_ref[...] + bon_ref[...].astype(F32)
    yr = (y * g_ref[...].astype(F32)).astype(BF16)
    o_ref[...] = _dot(yh_ref[...].astype(BF16), wh_ref[...]) + _dot(yr, wr_ref[...]) + r_ref[...]


def _out_proj(yh, sf, sb, bon, g, lnx_g, lnx_b, w, res, tm=1024):
    m, ch = yh.shape
    cr = sf.shape[1]
    d = w.shape[1]
    tm = min(tm, m)
    head = jnp.arange(cr) // HEAD
    avg = ((head[:, None] == head[None, :]).astype(F32) / HEAD).astype(BF16)
    row = lambda c: pl.BlockSpec((tm, c), lambda i: (i, 0))
    return pl.pallas_call(
        _out_proj_kernel,
        out_shape=jax.ShapeDtypeStruct((m, d), F32),
        grid=(m // tm,),
        in_specs=[row(ch), row(cr), row(cr), row(cr), row(cr), _const_spec((1, cr)), _const_spec((1, cr)),
                  _const_spec((cr, cr)), _const_spec((ch, d)), _const_spec((cr, d)), row(d)],
        out_specs=row(d),
        compiler_params=_params(("parallel",)),
        name="out_proj",
    )(yh, sf, sb, bon, g, lnx_g.reshape(1, cr), lnx_b.reshape(1, cr), avg, w[:ch], w[ch:], res)


def _rwkv_branch(p, v_first, shift, w0, w2, a0, a2, g2, k_k, k_a, r_k, vres, c):
    r, v, kk, lwf, lwb, kf, kb, af, ab, g, bon = _rw_prep(
        p, v_first, shift, w0, w2, a0, a2, g2, k_k, k_a, r_k, vres, c)
    sf, sb = _wkv_scan(r, v, kk, kf, af, lwf, kb, ab, lwb)
    return sf, sb, bon, g, (v if vres is None else v_first)


def _forward(x, norm1_g, w_in, hy_conv_w, hy_conv_b, hy_f_w1, hy_f_b1, hy_f_w2, hy_f_b2, hy_f_w3, hy_f_b3,
             hy_f_freq, hy_f_wout, hy_skip, hy_norm_g, rw_shift, rw_w0, rw_w2, rw_a0, rw_a2, rw_g2, rw_k_k,
             rw_k_a, rw_r_k, rw_lnx_g, rw_lnx_b, rw_v0, rw_v1, rw_v2, w_out, norm2_g, mlp_w1, mlp_w2,
             final_g):
    b, t, d = x.shape
    depth = w_in.shape[0]
    c_hy = hy_norm_g.shape[1]
    c_rw = rw_lnx_g.shape[1]
    hy_proj = hy_conv_b.shape[1]
    assert b % 2 == 0 and t % CHUNK == 0 and CHUNK == HEAD
    tabs = _dft_tables(t)
    assert tabs["n2"] % SUB16 == 0 and tabs["n1"] % SUB16 == 0 and tabs["n1h"] % SUB == 0
    v_first = None
    for l in range(depth):
        w_l = w_in[l].astype(BF16)
        w_rw = w_l[:, hy_proj:]
        vres = None if l == 0 else (rw_v0[l - 1], rw_v1[l - 1], rw_v2[l - 1])
        if vres is not None:
            v1 = vres[1].astype(BF16)
            w_rw = jnp.concatenate([w_rw, jnp.pad(v1, ((0, 0), (0, -v1.shape[1] % 128)))], axis=1)
        p_hy, p_rw = _in_proj(x, norm1_g[l], w_l[:, :hy_proj], w_rw, tabs["n1h"], tabs["n2"])
        y_hy = _hyena_branch(p_hy, hy_conv_w[l], hy_conv_b[l], hy_f_w1[l], hy_f_b1[l], hy_f_w2[l], hy_f_b2[l],
                             hy_f_w3[l], hy_f_b3[l], hy_f_freq[l], hy_f_wout[l], hy_skip[l], hy_norm_g[l], tabs)
        sf, sb, bon, g, v_first = _rwkv_branch(p_rw, v_first, rw_shift[l], rw_w0[l], rw_w2[l], rw_a0[l], rw_a2[l],
                                               rw_g2[l], rw_k_k[l], rw_k_a[l], rw_r_k[l], vres, c_rw)
        flat = lambda a: a.reshape(b * t, a.shape[-1])
        x = _out_proj(flat(y_hy), flat(sf), flat(sb), flat(bon), flat(g), rw_lnx_g[l], rw_lnx_b[l],
                      w_out[l].astype(BF16), flat(x))
        x = _mlp(x, norm2_g[l], mlp_w1[l].astype(BF16), mlp_w2[l].astype(BF16),
                 final_g=final_g if l == depth - 1 else None).reshape(b, t, d)
    return x


def kernel(x, norm1_g, w_in, hy_conv_w, hy_conv_b, hy_f_w1, hy_f_b1, hy_f_w2, hy_f_b2, hy_f_w3, hy_f_b3,
           hy_f_freq, hy_f_wout, hy_skip, hy_norm_g, rw_shift, rw_w0, rw_w2, rw_a0, rw_a2, rw_g2, rw_k_k,
           rw_k_a, rw_r_k, rw_lnx_g, rw_lnx_b, rw_v0, rw_v1, rw_v2, w_out, norm2_g, mlp_w1, mlp_w2, final_g):
    return _forward(x, norm1_g, w_in, hy_conv_w, hy_conv_b, hy_f_w1, hy_f_b1, hy_f_w2, hy_f_b2, hy_f_w3,
                    hy_f_b3, hy_f_freq, hy_f_wout, hy_skip, hy_norm_g, rw_shift, rw_w0, rw_w2, rw_a0, rw_a2,
                    rw_g2, rw_k_k, rw_k_a, rw_r_k, rw_lnx_g, rw_lnx_b, rw_v0, rw_v1, rw_v2, w_out, norm2_g,
                    mlp_w1, mlp_w2, final_g)
```

```python
import functools
import math

import jax
import jax.numpy as jnp
import numpy as np
from jax import lax
from jax.experimental import pallas as pl
from jax.experimental.pallas import tpu as pltpu

F32 = jnp.float32
BF16 = jnp.bfloat16
HP = lax.Precision.HIGHEST

HEAD = 64
HEADS_PER_GROUP = 4
GW = HEAD * HEADS_PER_GROUP
CHUNK = 64
INV_BASE = 8
RMS_EPS = 1e-5
GN_EPS = HEAD * 1e-5
HY_TARGET = 1e-2
HY_FAST_PCT = 0.3
HY_SLOW_PCT = 1.5
VMEM_LIMIT = 56 * 1024 * 1024


def _params(sem, vmem=VMEM_LIMIT):
    return pltpu.CompilerParams(dimension_semantics=sem, vmem_limit_bytes=vmem)


def _dot(a, b, precision=None):
    return jnp.dot(a, b, preferred_element_type=F32, precision=precision)


def _dot_tn(a, b, precision=None):
    return lax.dot_general(a, b, (((0,), (0,)), ((), ())), preferred_element_type=F32,
                           precision=precision)


def _const_spec(shape):
    return pl.BlockSpec(shape, lambda *_: (0,) * len(shape), pipeline_mode=pl.Buffered(1))


def _split3(x):
    h1 = x.astype(BF16)
    r1 = x - h1.astype(F32)
    h2 = r1.astype(BF16)
    h3 = (r1 - h2.astype(F32)).astype(BF16)
    return h1, h2, h3


def _dot3(x, m):
    h1, h2, h3 = _split3(x)
    return _dot(h1, m) + _dot(h2, m) + _dot(h3, m)


def _mlp_kernel(x_ref, g_ref, w1_ref, w2_ref, *rest, nchunk):
    x = x_ref[...]
    hn = (x * lax.rsqrt(jnp.mean(x * x, axis=-1, keepdims=True) + RMS_EPS) * g_ref[...]).astype(BF16)
    ff = w1_ref.shape[1]
    cw = ff // nchunk
    acc = x
    for j in range(nchunk):
        h = _dot(hn, w1_ref[:, j * cw:(j + 1) * cw])
        h = jnp.square(jnp.maximum(h, 0.0)).astype(BF16)
        acc = acc + _dot(h, w2_ref[j * cw:(j + 1) * cw, :])
    if len(rest) == 2:
        gf_ref, o_ref = rest
        acc = acc * lax.rsqrt(jnp.mean(acc * acc, axis=-1, keepdims=True) + RMS_EPS) * gf_ref[...]
    else:
        (o_ref,) = rest
    o_ref[...] = acc


def _mlp(x2d, g, w1, w2, final_g=None, tm=1024, nchunk=4):
    m, d = x2d.shape
    ff = w1.shape[1]
    tm = min(tm, m)
    in_specs = [pl.BlockSpec((tm, d), lambda i: (i, 0)), _const_spec((1, d)),
                _const_spec((d, ff)), _const_spec((ff, d))]
    args = [x2d, g.reshape(1, d), w1, w2]
    if final_g is not None:
        in_specs.append(_const_spec((1, d)))
        args.append(final_g.reshape(1, d))
    return pl.pallas_call(
        functools.partial(_mlp_kernel, nchunk=nchunk),
        out_shape=jax.ShapeDtypeStruct((m, d), F32),
        grid=(m // tm,),
        in_specs=in_specs,
        out_specs=pl.BlockSpec((tm, d), lambda i: (i, 0)),
        compiler_params=_params(("parallel",)),
        name="mlp",
    )(*args)


SUB = 8
SUB16 = 16
DFT_GROUP = 16


def _rms(x, g):
    return (x * lax.rsqrt(jnp.mean(x * x, axis=-1, keepdims=True) + RMS_EPS) * g).astype(BF16)


def _in_proj_kernel(x_ref, g_ref, wh_ref, wr_ref, oh_ref, or_ref, *, n2):
    hn = _rms(x_ref[0], g_ref[...])
    or_ref[0] = _dot(hn, wr_ref[...]).astype(or_ref.dtype)
    ph = _dot(hn, wh_ref[...])
    for j in range(SUB):
        oh_ref[0, :, j, :] = ph[j * n2:(j + 1) * n2]


def _in_proj(x, g, w_hy, w_rw, n1h, n2):
    b, t, d = x.shape
    ph, pr = w_hy.shape[1], w_rw.shape[1]
    tm = SUB * n2
    p_hy, p_rw = pl.pallas_call(
        functools.partial(_in_proj_kernel, n2=n2),
        out_shape=(jax.ShapeDtypeStruct((b, n2, n1h, ph), F32), jax.ShapeDtypeStruct((b, t, pr), BF16)),
        grid=(b, t // tm),
        in_specs=[pl.BlockSpec((1, tm, d), lambda i, j: (i, j, 0)), _const_spec((1, d)),
                  _const_spec((d, ph)), _const_spec((d, pr))],
        out_specs=(pl.BlockSpec((1, n2, SUB, ph), lambda i, j: (i, 0, j, 0)),
                   pl.BlockSpec((1, tm, pr), lambda i, j: (i, j, 0))),
        compiler_params=_params(("parallel", "parallel")),
        name="in_proj",
    )(x, g.reshape(1, d), w_hy, w_rw)
    return p_hy.reshape(b, t, ph), p_rw


def _shortconv_kernel(p_ref, w_ref, b_ref, o_ref, *, n1h, n2):
    w0 = w_ref[0:1, :]
    w1 = w_ref[1:2, :]
    w2 = w_ref[2:3, :]
    bias = b_ref[...]
    rows = lax.broadcasted_iota(jnp.int32, (n1h, 1), 0)

    def blk(i):
        return p_ref[0, pl.ds(pl.multiple_of(i * n1h, n1h), n1h), :].astype(F32)

    def body(i, carry):
        o_ref[0, pl.ds(pl.multiple_of(i * n1h, n1h), n1h), :] = (
            w0 * blk(i - 1) + w1 * blk(i) + w2 * blk(i + 1) + bias).astype(o_ref.dtype)
        return carry

    lax.fori_loop(1, n2 - 1, body, 0)
    last = p_ref[0, (n2 - 1) * n1h:, :].astype(F32)
    first = p_ref[0, :n1h, :].astype(F32)
    prev0 = jnp.where(rows == 0, 0.0, pltpu.roll(last, 1, 0))
    o_ref[0, :n1h, :] = (w0 * prev0 + w1 * first + w2 * p_ref[0, n1h:2 * n1h, :].astype(F32)
                         + bias).astype(o_ref.dtype)
    nxt = jnp.where(rows == n1h - 1, 0.0, pltpu.roll(first, n1h - 1, 0))
    o_ref[0, (n2 - 1) * n1h:, :] = (w0 * p_ref[0, (n2 - 2) * n1h:(n2 - 1) * n1h, :].astype(F32) + w1 * last
                                    + w2 * nxt + bias).astype(o_ref.dtype)


def _shortconv(p, w, bias, n1h, n2, cb=512):
    b, t, c = p.shape
    return pl.pallas_call(
        functools.partial(_shortconv_kernel, n1h=n1h, n2=n2),
        out_shape=jax.ShapeDtypeStruct((b, t, c), BF16),
        grid=(b, c // cb),
        in_specs=[pl.BlockSpec((1, t, cb), lambda i, j: (i, 0, j)),
                  pl.BlockSpec((3, cb), lambda i, j: (0, j)),
                  pl.BlockSpec((1, cb), lambda i, j: (0, j))],
        out_specs=pl.BlockSpec((1, t, cb), lambda i, j: (i, 0, j)),
        compiler_params=_params(("parallel", "parallel")),
        name="hy_shortconv",
    )(p, w, bias.reshape(1, c))


def _dft_tables(t):
    n = 2 * t
    n1 = 1 << ((n.bit_length() - 1) // 2)
    n2 = n // n1
    n1h = n1 // 2
    k1 = np.arange(n1)
    n2i = np.arange(n2)

    def cs(prod, mod):
        ang = (2.0 * np.pi / mod) * (prod % mod)
        return np.cos(ang), np.sin(ang)

    tposf = n2i[:, None] + n2 * np.arange(n1)[None, :]
    cf, sf = cs(k1[None, :, None] * tposf[:, None, :], n)
    g1f = np.concatenate([cf, -sf], axis=1)
    c, s = cf[:, :, :n1h], sf[:, :, :n1h]
    g1 = np.concatenate([np.concatenate([c, s], axis=2), np.concatenate([-s, c], axis=2)], axis=1)
    ct, st = np.swapaxes(c, 1, 2) / n, np.swapaxes(s, 1, 2) / n
    g4a = np.concatenate([ct, st], axis=1)
    g4b = np.concatenate([-st, ct], axis=1)
    c2, s2 = cs(n2i[:, None] * n2i[None, :], n2)
    f2a = np.concatenate([c2, -s2], axis=0)
    f2b = np.concatenate([s2, c2], axis=0)
    f3 = np.concatenate([np.concatenate([c2, -s2], axis=1), np.concatenate([s2, c2], axis=1)], axis=0)
    g4 = np.concatenate([g4a, g4b], axis=2)
    f2 = np.concatenate([f2a, f2b], axis=1)
    const = lambda a: jnp.asarray(a.astype(np.float32).astype(BF16))
    return dict(n1=n1, n2=n2, n1h=n1h, g1=const(g1), g4=const(g4), g1f=const(g1f), f2=const(f2), f3=const(f3))


def _filter_mlp_kernel(z_ref, w1_ref, b1_ref, w2_ref, b2_ref, w3_ref, b3_ref, fr_ref, wo_ref, dl_ref,
                       o_ref, *, t):
    z = z_ref[...]
    h = jnp.sin(fr_ref[0:1, :] * (_dot(z, w1_ref[...], HP) + b1_ref[...]))
    h = jnp.sin(fr_ref[1:2, :] * (_dot(h, w2_ref[...], HP) + b2_ref[...]))
    h = jnp.sin(fr_ref[2:3, :] * (_dot(h, w3_ref[...], HP) + b3_ref[...]))
    window = jnp.exp(-z[:, 0:1] * dl_ref[...])
    tr = z.shape[0]
    row = pl.program_id(0) * tr + lax.broadcasted_iota(jnp.int32, (tr, 1), 0)
    o_ref[...] = jnp.where(row == t, 0.0, _dot(h, wo_ref[0], HP) * window)


def _filter_taps(t, w1, b1, w2, b2, w3, b3, freq, w_out, c_hy, tr=1024):
    emb, width = w1.shape
    bands = (emb - 1) // 2
    pos = np.arange(t, dtype=np.float32)
    tt = pos / np.float32(max(t - 1, 1))
    fr = np.linspace(1e-4, bands - 1, bands, dtype=np.float32)
    ang = np.float32(2.0 * math.pi / t) * pos[:, None] * fr[None, :]
    z = np.concatenate([tt[:, None], np.cos(ang), -np.sin(ang)], axis=-1)
    z = np.pad(z, ((0, 0), (0, 128 - emb)))
    w1 = jnp.pad(w1, ((0, 128 - emb), (0, 0)))
    emb = 128
    z2 = jnp.asarray(np.concatenate([z, z[:1], z[1:][::-1]], axis=0))
    max_decay = math.log(HY_TARGET) / HY_FAST_PCT
    min_decay = math.log(HY_TARGET) / HY_SLOW_PCT
    delta = jnp.abs(jnp.linspace(min_decay, max_decay, c_hy, dtype=F32))
    order = w_out.shape[1] // (2 * c_hy)
    nout = order * c_hy
    w_dir = w_out.reshape(width, order, 2, c_hy).transpose(2, 0, 1, 3).reshape(2, width, nout)
    delta_full = jnp.tile(delta, order).reshape(1, nout)
    tr = min(tr, t)
    nt = t // tr
    full = lambda a: pl.BlockSpec(a.shape, lambda i: (0,) * a.ndim)
    args = [z2, w1, b1.reshape(1, -1), w2, b2.reshape(1, -1), w3, b3.reshape(1, -1), freq, w_dir, delta_full]
    in_specs = [pl.BlockSpec((tr, emb), lambda i: (i, 0))] + [full(a) for a in args[1:]]
    in_specs[8] = pl.BlockSpec((1, width, nout), lambda i: (i // nt, 0, 0))
    return pl.pallas_call(
        functools.partial(_filter_mlp_kernel, t=t),
        out_shape=jax.ShapeDtypeStruct((2 * t, nout), F32),
        grid=(2 * nt,),
        in_specs=in_specs,
        out_specs=pl.BlockSpec((tr, nout), lambda i: (i, 0)),
        compiler_params=_params(("parallel",)),
        name="hy_filter_mlp",
    )(*args)


def _filter_s1_kernel(x_ref, g_ref, o_ref, ss_ref, *, g, oc):
    @pl.when(pl.program_id(0) == 0)
    def _():
        ss_ref[...] = jnp.zeros_like(ss_ref)

    n1 = x_ref.shape[0]
    x = jnp.concatenate([x_ref[:, j, :] for j in range(g)], axis=0)
    ss_ref[...] += jnp.sum(x * x, axis=0, keepdims=True)
    xb = x.astype(BF16)
    for j in range(g):
        o_ref[j] = _dot(g_ref[j], xb[j * n1:(j + 1) * n1])


def _filter_s2_kernel(sr_ref, si_ref, f2_ref, ss_ref, o_ref, *, g, oc):
    n2 = sr_ref.shape[0]
    scale = lax.rsqrt(ss_ref[...] + 1e-6)
    s = jnp.concatenate([ref[:, j, :] for j in range(g) for ref in (sr_ref, si_ref)], axis=0).astype(BF16)
    for j in range(g):
        o_ref[j] = _dot(f2_ref[...], s[2 * j * n2:2 * (j + 1) * n2]) * scale


def _filter_spectrum(two, tabs):
    n, oc = two.shape
    n1, n2 = tabs["n1"], tabs["n2"]
    g = SUB
    s1, ss = pl.pallas_call(
        functools.partial(_filter_s1_kernel, g=g, oc=oc),
        out_shape=(jax.ShapeDtypeStruct((n2, 2 * n1, oc), F32), jax.ShapeDtypeStruct((1, oc), F32)),
        grid=(n2 // g,),
        in_specs=[pl.BlockSpec((n1, g, oc), lambda j: (0, j, 0)),
                  pl.BlockSpec((g, 2 * n1, n1), lambda j: (j, 0, 0))],
        out_specs=(pl.BlockSpec((g, 2 * n1, oc), lambda j: (j, 0, 0)),
                   pl.BlockSpec((1, oc), lambda j: (0, 0))),
        compiler_params=_params(("arbitrary",)),
        name="hy_filter_dft1",
    )(two.reshape(n1, n2, oc), tabs["g1f"])
    return pl.pallas_call(
        functools.partial(_filter_s2_kernel, g=g, oc=oc),
        out_shape=jax.ShapeDtypeStruct((n1, 2 * n2, oc), F32),
        grid=(n1 // g,),
        in_specs=[pl.BlockSpec((n2, g, oc), lambda k: (0, k, 0)),
                  pl.BlockSpec((n2, g, oc), lambda k: (0, n1 // g + k, 0)),
                  pl.BlockSpec((2 * n2, 2 * n2), lambda k: (0, 0)),
                  pl.BlockSpec((1, oc), lambda k: (0, 0))],
        out_specs=pl.BlockSpec((g, 2 * n2, oc), lambda k: (k, 0, 0)),
        compiler_params=_params(("parallel",)),
        name="hy_filter_dft2",
    )(s1, s1, tabs["f2"], ss)


def _conv_s1_kernel(u_ref, g_ref, o_ref, t_ref, *, g, n1h):
    for j0 in range(0, g, DFT_GROUP):
        js = range(j0, j0 + DFT_GROUP)
        rs = [_dot(g_ref[j], jnp.concatenate([u_ref[0, 0, j * n1h:(j + 1) * n1h, :],
                                               u_ref[0, 1, j * n1h:(j + 1) * n1h, :]], axis=0)) for j in js]
        for j, r in zip(js, rs):
            t_ref[:, j, :] = r
    o_ref[0] = t_ref[...].astype(o_ref.dtype)


def _conv_s23_kernel(sr_ref, si_ref, f2_ref, f3_ref, h_ref, o_ref, t_ref, *, g, n2):
    for j0 in range(0, g, DFT_GROUP):
        js = range(j0, j0 + DFT_GROUP)
        xs = [_dot(f2_ref[...], jnp.concatenate([sr_ref[0, j], si_ref[0, j]], axis=0)) for j in js]
        ys = []
        for j, x in zip(js, xs):
            xr, xi = x[:n2], x[n2:]
            hr, hi = h_ref[j, :n2, :], h_ref[j, n2:, :]
            ys.append(jnp.concatenate([xr * hr - xi * hi, xr * hi + xi * hr], axis=0).astype(BF16))
        rs = [_dot(f3_ref[...], y) for y in ys]
        for j, r in zip(js, rs):
            t_ref[:, j, :] = r
    o_ref[0] = t_ref[...].astype(o_ref.dtype)


def _conv_s4_kernel(rr_ref, ri_ref, g4_ref, u_ref, gate_ref, skip_ref, *rest, g, n1h, norm):
    if norm:
        ng_ref, avg_ref, o_ref = rest
    else:
        (o_ref,) = rest
    skip = skip_ref[...]
    ys = [_dot(g4_ref[j], jnp.concatenate([rr_ref[0, j], ri_ref[0, j]], axis=0)) for j in range(g)]
    zs = [gate_ref[0, q, j * n1h:(j + 1) * n1h, :].astype(F32)
          * (ys[j][q * n1h:(q + 1) * n1h] + u_ref[0, q, j * n1h:(j + 1) * n1h, :].astype(F32) * skip)
          for j in range(g) for q in range(2)]
    if norm:
        zz = jnp.concatenate([(z * z).astype(BF16) for z in zs], axis=0)
        ms = _dot(zz, avg_ref[...])
        zs = [z * lax.rsqrt(ms[i * n1h:(i + 1) * n1h] + RMS_EPS) * ng_ref[...] for i, z in enumerate(zs)]
    for j in range(g):
        for q in range(2):
            if norm:
                o_ref[0, q, :, j, :] = zs[2 * j + q]
            else:
                o_ref[0, q, j * n1h:(j + 1) * n1h, :] = zs[2 * j + q].astype(o_ref.dtype)


def _long_conv_gate(u_arr, u_col, gate_arr, gate_col, h_spec, h_col, skip, tabs, c, norm_g=None):
    b, t, _ = u_arr.shape
    n1, n2, n1h = tabs["n1"], tabs["n2"], tabs["n1h"]
    g = SUB16
    npair = b // 2
    u4 = u_arr.reshape(npair, 2, t, u_arr.shape[2])
    gate4 = gate_arr.reshape(npair, 2, t, gate_arr.shape[2])
    gt = SUB16
    s1 = pl.pallas_call(
        functools.partial(_conv_s1_kernel, g=gt, n1h=n1h),
        out_shape=jax.ShapeDtypeStruct((npair, 2 * n1, n2, c), BF16),
        grid=(n2 // gt, npair),
        in_specs=[pl.BlockSpec((1, 2, gt * n1h, c), lambda j, p: (p, 0, j, u_col)),
                  pl.BlockSpec((gt, 2 * n1, n1), lambda j, p: (j, 0, 0))],
        out_specs=pl.BlockSpec((1, 2 * n1, gt, c), lambda j, p: (p, 0, j, 0)),
        scratch_shapes=[pltpu.VMEM((2 * n1, gt, c), F32)],
        compiler_params=_params(("parallel", "parallel")),
        name="hy_conv_dft1",
    )(u4, tabs["g1"])
    r = pl.pallas_call(
        functools.partial(_conv_s23_kernel, g=gt, n2=n2),
        out_shape=jax.ShapeDtypeStruct((npair, 2 * n2, n1, c), BF16),
        grid=(n1 // gt, npair),
        in_specs=[pl.BlockSpec((1, gt, n2, c), lambda k, p: (p, k, 0, 0)),
                  pl.BlockSpec((1, gt, n2, c), lambda k, p: (p, n1 // gt + k, 0, 0)),
                  pl.BlockSpec((2 * n2, 2 * n2), lambda k, p: (0, 0)),
                  pl.BlockSpec((2 * n2, 2 * n2), lambda k, p: (0, 0)),
                  pl.BlockSpec((gt, 2 * n2, c), lambda k, p: (k, 0, h_col))],
        out_specs=pl.BlockSpec((1, 2 * n2, gt, c), lambda k, p: (p, 0, k, 0)),
        scratch_shapes=[pltpu.VMEM((2 * n2, gt, c), F32)],
        compiler_params=_params(("parallel", "parallel")),
        name="hy_conv_dft23",
    )(s1, s1, tabs["f2"], tabs["f3"], h_spec)
    norm = norm_g is not None
    in_specs = [pl.BlockSpec((1, g, n1, c), lambda p, j: (p, j, 0, 0)),
                pl.BlockSpec((1, g, n1, c), lambda p, j: (p, n2 // g + j, 0, 0)),
                pl.BlockSpec((g, n1, 2 * n1), lambda p, j: (j, 0, 0)),
                pl.BlockSpec((1, 2, g * n1h, c), lambda p, j: (p, 0, j, u_col)),
                pl.BlockSpec((1, 2, g * n1h, c), lambda p, j: (p, 0, j, gate_col)),
                pl.BlockSpec((1, c), lambda p, j: (0, 0))]
    args = [r, r, tabs["g4"], u4, gate4, skip.reshape(1, c)]
    if norm:
        ch = jnp.arange(c) // HEAD
        avg = ((ch[:, None] == ch[None, :]).astype(F32) / HEAD).astype(BF16)
        in_specs += [pl.BlockSpec((1, c), lambda p, j: (0, 0)), pl.BlockSpec((c, c), lambda p, j: (0, 0))]
        args += [norm_g.reshape(1, c), avg]
        out_shape = jax.ShapeDtypeStruct((npair, 2, n1h, n2, c), F32)
        out_spec = pl.BlockSpec((1, 2, n1h, g, c), lambda p, j: (p, 0, 0, j, 0))
    else:
        out_shape = jax.ShapeDtypeStruct((npair, 2, t, c), BF16)
        out_spec = pl.BlockSpec((1, 2, g * n1h, c), lambda p, j: (p, 0, j, 0))
    out = pl.pallas_call(
        functools.partial(_conv_s4_kernel, g=g, n1h=n1h, norm=norm),
        out_shape=out_shape,
        grid=(npair, n2 // g),
        in_specs=in_specs,
        out_specs=out_spec,
        compiler_params=_params(("parallel", "parallel")),
        name="hy_conv_dft4",
    )(*args)
    return out.reshape(b, t, c)


def _hyena_branch(p, conv_w, conv_b, fw1, fb1, fw2, fb2, fw3, fb3, ffreq, fwout, skip, norm_g, tabs):
    b, t, _ = p.shape
    c = norm_g.shape[0]
    n1, n2, n1h = tabs["n1"], tabs["n2"], tabs["n1h"]
    u = _shortconv(p, conv_w, conv_b, n1h, n2)
    two = _filter_taps(t, fw1, fb1, fw2, fb2, fw3, fb3, ffreq, fwout, c)
    hspec = _filter_spectrum(two, tabs)
    z = _long_conv_gate(u, 2, u, 0, hspec, 0, skip[0], tabs, c)
    return _long_conv_gate(z, 0, u, 1, hspec, 1, skip[1], tabs, c, norm_g=norm_g)


def _rw_prep_kernel(*refs, has_vres, c):
    if has_vres:
        (p_ref, pp_ref, pn_ref, mu_ref, w0_ref, w2_ref, a0_ref, a2_ref, g2_ref, kk_ref, ka_ref, rk_ref,
         sum_ref, vf_ref, v0_ref, v2_ref,
         r_o, v_o, kk_o, lwf_o, lwb_o, kf_o, kb_o, af_o, ab_o, g_o, bon_o) = refs
    else:
        (p_ref, pp_ref, pn_ref, mu_ref, w0_ref, w2_ref, a0_ref, a2_ref, g2_ref, kk_ref, ka_ref, rk_ref,
         sum_ref,
         r_o, v_o, kk_o, lwf_o, lwb_o, kf_o, kb_o, af_o, ab_o, g_o, bon_o) = refs
    i = pl.program_id(1)
    last = pl.num_programs(1) - 1
    nsh = mu_ref.shape[1]
    p = p_ref[0, :, :nsh].astype(F32)
    tt = p.shape[0]
    rows = lax.broadcasted_iota(jnp.int32, (tt, 1), 0)
    prev_row = jnp.where(i == 0, 0.0, pp_ref[0, SUB16 - 1:SUB16, :nsh].astype(F32))
    next_row = jnp.where(i == last, 0.0, pn_ref[0, 0:1, :nsh].astype(F32))
    prev = jnp.where(rows == 0, prev_row, pltpu.roll(p, 1, 0))
    nxt = jnp.where(rows == tt - 1, next_row, pltpu.roll(p, tt - 1, 0))
    mu0, mu1 = mu_ref[0:1, :], mu_ref[1:2, :]
    pf = p * (1.0 - mu0 - mu1) + mu0 * prev + mu1 * nxt
    r = pf[:, :c]
    k = pf[:, c:2 * c]
    v = pf[:, 2 * c:3 * c]
    lw = 3 * c
    nd = w2_ref.shape[0]
    na = a2_ref.shape[0]
    wd = pf[:, lw:lw + nd]
    ad = pf[:, lw + nd:lw + nd + na]
    gd = pf[:, lw + nd + na:]
    if has_vres:
        lora = _dot(p_ref[0, :, nsh:], v2_ref[...])
        v = v + (vf_ref[0].astype(F32) - v) * jax.nn.sigmoid(v0_ref[...] + lora)
    g = _dot(jax.nn.sigmoid(gd).astype(BF16), g2_ref[...])
    kk = k * kk_ref[...]
    kk = kk * lax.rsqrt(jnp.maximum(_dot((kk * kk).astype(BF16), sum_ref[...]), 1e-24))
    wl = w0_ref[...] + _dot(jnp.tanh(wd).astype(BF16), w2_ref[...])
    logw = -math.exp(-0.5) * jax.nn.sigmoid(wl)
    a = jax.nn.sigmoid(a0_ref[...] + _dot(ad.astype(BF16), a2_ref[...]))
    ka = ka_ref[...]
    rk = rk_ref[...]
    k_d = [k * (1.0 + (a[:, d * c:(d + 1) * c] - 1.0) * ka) for d in range(2)]
    bon = _dot((r * (k_d[0] + k_d[1]) * rk).astype(BF16), sum_ref[...]) * v
    r_o[0] = r.astype(BF16)
    v_o[0] = v.astype(BF16)
    kk_o[0] = kk.astype(BF16)
    lwf_o[0] = logw[:, :c]
    lwb_o[0] = logw[:, c:]
    kf_o[0] = k_d[0].astype(BF16)
    kb_o[0] = k_d[1].astype(BF16)
    af_o[0] = a[:, :c].astype(BF16)
    ab_o[0] = a[:, c:].astype(BF16)
    g_o[0] = g.astype(BF16)
    bon_o[0] = bon.astype(BF16)


def _blockdiag2(m):
    k, c = m.shape[1], m.shape[2]
    z = jnp.zeros((k, c), m.dtype)
    return jnp.concatenate([jnp.concatenate([m[0], z], axis=1), jnp.concatenate([z, m[1]], axis=1)], axis=0)


def _rw_prep(p, v_first, shift, w0, w2, a0, a2, g2, k_k, k_a, r_k, vres, c, tt=1024):
    b, t, pw = p.shape
    tt = min(tt, t)
    has_vres = vres is not None
    ch = jnp.arange(c) // HEAD
    summ = (ch[:, None] == ch[None, :]).astype(BF16)
    row = lambda a: a.reshape(1, -1)
    args = [p, p, p, shift, row(w0), _blockdiag2(w2).astype(BF16), row(a0), _blockdiag2(a2).astype(BF16),
            g2.astype(BF16), row(k_k), row(k_a), row(r_k), summ]
    full = lambda a: pl.BlockSpec(a.shape, lambda i, j: (0,) * a.ndim)
    nhb = t // SUB16
    in_specs = [pl.BlockSpec((1, tt, pw), lambda i, j: (i, j, 0)),
                pl.BlockSpec((1, SUB16, pw), lambda i, j: (i, jnp.maximum(j * (tt // SUB16) - 1, 0), 0)),
                pl.BlockSpec((1, SUB16, pw), lambda i, j: (i, jnp.minimum((j + 1) * (tt // SUB16), nhb - 1), 0))]
    in_specs += [full(a) for a in args[3:]]
    if has_vres:
        v0, _, v2 = vres
        v2p = jnp.pad(v2, ((0, pw - shift.shape[1] - v2.shape[0]), (0, 0))).astype(BF16)
        extra = [v_first, row(v0), v2p]
        in_specs += [pl.BlockSpec((1, tt, c), lambda i, j: (i, j, 0))] + [full(a) for a in extra[1:]]
        args += extra
    outs = tuple(jax.ShapeDtypeStruct((b, t, c), F32 if i in (3, 4) else BF16) for i in range(11))
    return pl.pallas_call(
        functools.partial(_rw_prep_kernel, has_vres=has_vres, c=c),
        out_shape=outs,
        grid=(b, t // tt),
        in_specs=in_specs,
        out_specs=tuple(pl.BlockSpec((1, tt, c), lambda i, j: (i, j, 0)) for _ in range(11)),
        compiler_params=_params(("parallel", "parallel")),
        name="rw_prep",
    )(*args)


def _scan_chunks(s0, r, k, v, kk, a, lw, rev, bdm):
    L = CHUNK
    n = len(r)
    each = lambda f, *ls: [f(*xs) for xs in zip(*ls)]
    ti = lax.broadcasted_iota(jnp.int32, (L, L), 0)
    si = lax.broadcasted_iota(jnp.int32, (L, L), 1)
    tt = lax.broadcasted_iota(jnp.int32, (L, GW), 0)
    ss = lax.broadcasted_iota(jnp.int32, (L, GW), 1) % L
    eye = (ss == tt).astype(F32)
    tri = [((si >= ti) if q else (si <= ti)).astype(F32).astype(BF16) for q in rev]
    strict = [(ss > tt) if q else (ss < tt) for q in rev]
    incl = [(ss >= tt) if q else (ss <= tt) for q in rev]
    bdf = bdm.astype(F32)

    def bd(x):
        xb = x.astype(BF16)
        return jnp.concatenate([xb] * HEADS_PER_GROUP, axis=0) * bdm

    def bd_t(x):
        xt = x.T.astype(BF16)
        return jnp.concatenate([xt] * HEADS_PER_GROUP, axis=1) * bdm

    def rcmul(x, y):
        return _dot(x.astype(BF16), bd(y))

    parts = each(_split3, lw)
    cum = [_dot(t, p[0]) + _dot(t, p[1]) + _dot(t, p[2]) for t, p in zip(tri, parts)]
    tot = [c[0:1, :] if q else c[L - 1:L, :] for c, q in zip(cum, rev)]
    p_in = each(jnp.exp, cum)
    p_inv = each(lambda c: jnp.exp(-c), cum)
    p_ex = each(lambda c, w: jnp.exp(c - w), cum, lw)
    p_rem = each(lambda t, c: jnp.exp(t - c), tot, cum)
    beta = each(lambda x, y: x * y, kk, a)
    a_t = each(lambda x, p: -x * p, kk, p_ex)
    r_t = each(lambda x, p: x * p, r, p_in)
    k_t = each(lambda x, p: x * p, k, p_inv)
    b_t = each(lambda x, p: x * p, beta, p_inv)
    k_h = each(lambda x, p: x * p, k, p_rem)
    b_h = each(lambda x, p: x * p, beta, p_rem)

    lhs = each(lambda x, y: jnp.concatenate([x, y], axis=0).astype(BF16), a_t, r_t)
    akk = each(lambda x, y: _dot(x, bd_t(y)), lhs, k_t)
    abb = each(lambda x, y: _dot(x, bd_t(y)), lhs, b_t)
    a_ak = each(lambda m, x: jnp.where(m, x[:L], 0.0), strict, akk)
    a_rk = each(lambda m, x: jnp.where(m, x[L:], 0.0), incl, akk)
    a_ab = each(lambda m, x: jnp.where(m, x[:L], 0.0), strict, abb)
    a_rb = each(lambda m, x: jnp.where(m, x[L:], 0.0), incl, abb)

    same = lambda size: (tt // size) == (ss // size)
    d = each(lambda x: jnp.where(same(INV_BASE), x, 0.0), a_ab)
    tinv = each(lambda x: eye + x, d)
    pw = each(rcmul, d, d)
    size = 4
    while size < INV_BASE:
        both = each(lambda p, t: rcmul(jnp.concatenate([p, t], axis=0), p), pw, tinv)
        pw = [x[:L] for x in both]
        tinv = each(lambda t, x: t + x[L:], tinv, both)
        size *= 2
    tinv = each(lambda t, p: t + rcmul(t, p), tinv, pw)
    size = INV_BASE
    while size < L:
        size *= 2
        couple = same(size) & ~same(size // 2)
        e = each(lambda x: jnp.where(couple, x, 0.0), a_ab)
        tinv = each(lambda t, y: t + rcmul(rcmul(t, y), t), tinv, e)

    av = each(lambda x, y, z: rcmul(jnp.concatenate([x, y], axis=0), z), a_ak, a_rk, v)
    ct = each(rcmul, a_rb, tinv)
    wu2 = each(lambda t, c, x, y: _dot(jnp.concatenate([t, c], axis=0).astype(BF16),
                                       jnp.concatenate([bd(x), bd(y[:L])], axis=1)), tinv, ct, a_t, av)
    wu = [x[:L] for x in wu2]
    rbwu = [x[L:] for x in wu2]
    o = each(lambda x, w, s, y: _dot((x + w[:, :GW]).astype(BF16), bd_t(s)) + y[L:] + w[:, GW:],
             r_t, rbwu, s0, av)

    m_bd = each(lambda w, x: _dot_tn(w[:, :GW].astype(BF16), x.astype(BF16)) * bdf, wu, b_h)
    n_full = each(lambda x, w, y, z: _dot_tn(jnp.concatenate([x, w[:, GW:]], axis=0).astype(BF16),
                                             jnp.concatenate([y, z], axis=0).astype(BF16)) * bdf,
                  v, wu, k_h, b_h)
    s1 = []
    for i in range(n):
        n_rc = n_full[i][0:HEAD]
        for h in range(1, HEADS_PER_GROUP):
            n_rc = n_rc + n_full[i][h * HEAD:(h + 1) * HEAD]
        s1.append(_dot(s0[i].astype(BF16), m_bd[i].astype(BF16)) + s0[i] * jnp.exp(tot[i]) + n_rc)
    return s1, o


def _scan_kernel(rf_ref, kf_ref, vf_ref, kkf_ref, af_ref, lwf_ref,
                 rb_ref, kb_ref, vb_ref, kkb_ref, ab_ref, lwb_ref, of_ref, ob_ref, s_ref, *, ng, nb):
    @pl.when(pl.program_id(1) == 0)
    def _():
        s_ref[...] = jnp.zeros_like(s_ref)

    gi = lax.broadcasted_iota(jnp.int32, (GW, GW), 0) // HEAD
    gj = lax.broadcasted_iota(jnp.int32, (GW, GW), 1) // HEAD
    bdm = (gi == gj).astype(F32).astype(BF16)
    dirs = ((rf_ref, kf_ref, vf_ref, kkf_ref, af_ref, lwf_ref), (rb_ref, kb_ref, vb_ref, kkb_ref, ab_ref, lwb_ref))
    chains = [(i, d, h) for i in range(nb) for d in range(2) for h in range(ng)]
    ins = [[dirs[d][q][i, :, h * GW:(h + 1) * GW].astype(F32) for i, d, h in chains] for q in range(6)]
    s1, o = _scan_chunks([s_ref[i, d, h] for i, d, h in chains], *ins, [d == 1 for _, d, _ in chains], bdm)
    for (i, d, h), s_new, o_new in zip(chains, s1, o):
        (of_ref, ob_ref)[d][i, :, h * GW:(h + 1) * GW] = o_new
        s_ref[i, d, h] = s_new


SCAN_BATCH_ROWS = 4


def _wkv_scan(r, v, kk, kf, af, lwf, kb, ab, lwb):
    b, t, c = r.shape
    nc = t // CHUNK
    ng = c // GW
    nb = math.gcd(b, SCAN_BATCH_ROWS)
    fspec = pl.BlockSpec((nb, CHUNK, c), lambda i, j: (i, j, 0))
    bspec = pl.BlockSpec((nb, CHUNK, c), lambda i, j: (i, nc - 1 - j, 0))
    return pl.pallas_call(
        functools.partial(_scan_kernel, ng=ng, nb=nb),
        out_shape=(jax.ShapeDtypeStruct((b, t, c), F32), jax.ShapeDtypeStruct((b, t, c), F32)),
        grid=(b // nb, nc),
        in_specs=[fspec] * 6 + [bspec] * 6,
        out_specs=(fspec, bspec),
        scratch_shapes=[pltpu.VMEM((nb, 2, ng, HEAD, GW), F32)],
        compiler_params=_params(("parallel", "arbitrary")),
        name="rw_scan",
    )(r, kf, v, kk, af, lwf, r, kb, v, kk, ab, lwb)


def _out_proj_kernel(yh_ref, sf_ref, sb_ref, bon_ref, g_ref, lg_ref, lb_ref, avg_ref, wh_ref, wr_ref, r_ref,
                     o_ref):
    s = sf_ref[...] + sb_ref[...]
    mean = _dot3(s, avg_ref[...])
    d = s - mean
    var = _dot((d * d).astype(BF16), avg_ref[...])
    y = d * lax.rsqrt(var + GN_EPS) * lg_ref[...] + lb_ref[...] + bon_ref[...].astype(F32)
    yr = (y * g_ref[...].astype(F32)).astype(BF16)
    o_ref[...] = _dot(yh_ref[...].astype(BF16), wh_ref[...]) + _dot(yr, wr_ref[...]) + r_ref[...]


def _out_proj(yh, sf, sb, bon, g, lnx_g, lnx_b, w, res, tm=1024):
    m, ch = yh.shape
    cr = sf.shape[1]
    d = w.shape[1]
    tm = min(tm, m)
    head = jnp.arange(cr) // HEAD
    avg = ((head[:, None] == head[None, :]).astype(F32) / HEAD).astype(BF16)
    row = lambda c: pl.BlockSpec((tm, c), lambda i: (i, 0))
    return pl.pallas_call(
        _out_proj_kernel,
        out_shape=jax.ShapeDtypeStruct((m, d), F32),
        grid=(m // tm,),
        in_specs=[row(ch), row(cr), row(cr), row(cr), row(cr), _const_spec((1, cr)), _const_spec((1, cr)),
                  _const_spec((cr, cr)), _const_spec((ch, d)), _const_spec((cr, d)), row(d)],
        out_specs=row(d),
        compiler_params=_params(("parallel",)),
        name="out_proj",
    )(yh, sf, sb, bon, g, lnx_g.reshape(1, cr), lnx_b.reshape(1, cr), avg, w[:ch], w[ch:], res)


def _rwkv_branch(p, v_first, shift, w0, w2, a0, a2, g2, k_k, k_a, r_k, vres, c):
    r, v, kk, lwf, lwb, kf, kb, af, ab, g, bon = _rw_prep(
        p, v_first, shift, w0, w2, a0, a2, g2, k_k, k_a, r_k, vres, c)
    sf, sb = _wkv_scan(r, v, kk, kf, af, lwf, kb, ab, lwb)
    return sf, sb, bon, g, (v if vres is None else v_first)


def _forward(x, norm1_g, w_in, hy_conv_w, hy_conv_b, hy_f_w1, hy_f_b1, hy_f_w2, hy_f_b2, hy_f_w3, hy_f_b3,
             hy_f_freq, hy_f_wout, hy_skip, hy_norm_g, rw_shift, rw_w0, rw_w2, rw_a0, rw_a2, rw_g2, rw_k_k,
             rw_k_a, rw_r_k, rw_lnx_g, rw_lnx_b, rw_v0, rw_v1, rw_v2, w_out, norm2_g, mlp_w1, mlp_w2,
             final_g):
    b, t, d = x.shape
    depth = w_in.shape[0]
    c_hy = hy_norm_g.shape[1]
    c_rw = rw_lnx_g.shape[1]
    hy_proj = hy_conv_b.shape[1]
    assert b % 2 == 0 and t % CHUNK == 0 and CHUNK == HEAD
    tabs = _dft_tables(t)
    assert tabs["n2"] % SUB16 == 0 and tabs["n1"] % SUB16 == 0 and tabs["n1h"] % SUB == 0
    v_first = None
    for l in range(depth):
        w_l = w_in[l].astype(BF16)
        w_rw = w_l[:, hy_proj:]
        vres = None if l == 0 else (rw_v0[l - 1], rw_v1[l - 1], rw_v2[l - 1])
        if vres is not None:
            v1 = vres[1].astype(BF16)
            w_rw = jnp.concatenate([w_rw, jnp.pad(v1, ((0, 0), (0, -v1.shape[1] % 128)))], axis=1)
        p_hy, p_rw = _in_proj(x, norm1_g[l], w_l[:, :hy_proj], w_rw, tabs["n1h"], tabs["n2"])
        y_hy = _hyena_branch(p_hy, hy_conv_w[l], hy_conv_b[l], hy_f_w1[l], hy_f_b1[l], hy_f_w2[l], hy_f_b2[l],
                             hy_f_w3[l], hy_f_b3[l], hy_f_freq[l], hy_f_wout[l], hy_skip[l], hy_norm_g[l], tabs)
        sf, sb, bon, g, v_first = _rwkv_branch(p_rw, v_first, rw_shift[l], rw_w0[l], rw_w2[l], rw_a0[l], rw_a2[l],
                                               rw_g2[l], rw_k_k[l], rw_k_a[l], rw_r_k[l], vres, c_rw)
        flat = lambda a: a.reshape(b * t, a.shape[-1])
        x = _out_proj(flat(y_hy), flat(sf), flat(sb), flat(bon), flat(g), rw_lnx_g[l], rw_lnx_b[l],
                      w_out[l].astype(BF16), flat(x))
        x = _mlp(x, norm2_g[l], mlp_w1[l].astype(BF16), mlp_w2[l].astype(BF16),
                 final_g=final_g if l == depth - 1 else None).reshape(b, t, d)
    return x


def kernel(x, norm1_g, w_in, hy_conv_w, hy_conv_b, hy_f_w1, hy_f_b1, hy_f_w2, hy_f_b2, hy_f_w3, hy_f_b3,
           hy_f_freq, hy_f_wout, hy_skip, hy_norm_g, rw_shift, rw_w0, rw_w2, rw_a0, rw_a2, rw_g2, rw_k_k,
           rw_k_a, rw_r_k, rw_lnx_g, rw_lnx_b, rw_v0, rw_v1, rw_v2, w_out, norm2_g, mlp_w1, mlp_w2, final_g):
    return _forward(x, norm1_g, w_in, hy_conv_w, hy_conv_b, hy_f_w1, hy_f_b1, hy_f_w2, hy_f_b2, hy_f_w3,
                    hy_f_b3, hy_f_freq, hy_f_wout, hy_skip, hy_norm_g, rw_shift, rw_w0, rw_w2, rw_a0, rw_a2,
                    rw_g2, rw_k_k, rw_k_a, rw_r_k, rw_lnx_g, rw_lnx_b, rw_v0, rw_v1, rw_v2, w_out, norm2_g,
                    mlp_w1, mlp_w2, final_g)
```

```python
import functools
import math

import jax
import jax.numpy as jnp
import numpy as np
from jax import lax
from jax.experimental import pallas as pl
from jax.experimental.pallas import tpu as pltpu

F32 = jnp.float32
BF16 = jnp.bfloat16
HP = lax.Precision.HIGHEST

HEAD = 64
HEADS_PER_GROUP = 4
GW = HEAD * HEADS_PER_GROUP
CHUNK = 64
INV_BASE = 8
RMS_EPS = 1e-5
GN_EPS = HEAD * 1e-5
HY_TARGET = 1e-2
HY_FAST_PCT = 0.3
HY_SLOW_PCT = 1.5
VMEM_LIMIT = 56 * 1024 * 1024


def _params(sem, vmem=VMEM_LIMIT):
    return pltpu.CompilerParams(dimension_semantics=sem, vmem_limit_bytes=vmem)


def _dot(a, b, precision=None):
    return jnp.dot(a, b, preferred_element_type=F32, precision=precision)


def _dot_tn(a, b, precision=None):
    return lax.dot_general(a, b, (((0,), (0,)), ((), ())), preferred_element_type=F32,
                           precision=precision)


def _const_spec(shape):
    return pl.BlockSpec(shape, lambda *_: (0,) * len(shape), pipeline_mode=pl.Buffered(1))


def _split3(x):
    h1 = x.astype(BF16)
    r1 = x - h1.astype(F32)
    h2 = r1.astype(BF16)
    h3 = (r1 - h2.astype(F32)).astype(BF16)
    return h1, h2, h3


def _dot3(x, m):
    h1, h2, h3 = _split3(x)
    return _dot(h1, m) + _dot(h2, m) + _dot(h3, m)


def _mlp_kernel(x_ref, g_ref, w1_ref, w2_ref, *rest, nchunk):
    x = x_ref[...]
    hn = (x * lax.rsqrt(jnp.mean(x * x, axis=-1, keepdims=True) + RMS_EPS) * g_ref[...]).astype(BF16)
    ff = w1_ref.shape[1]
    cw = ff // nchunk
    acc = x
    for j in range(nchunk):
        h = _dot(hn, w1_ref[:, j * cw:(j + 1) * cw])
        h = jnp.square(jnp.maximum(h, 0.0)).astype(BF16)
        acc = acc + _dot(h, w2_ref[j * cw:(j + 1) * cw, :])
    if len(rest) == 2:
        gf_ref, o_ref = rest
        acc = acc * lax.rsqrt(jnp.mean(acc * acc, axis=-1, keepdims=True) + RMS_EPS) * gf_ref[...]
    else:
        (o_ref,) = rest
    o_ref[...] = acc


def _mlp(x2d, g, w1, w2, final_g=None, tm=1024, nchunk=4):
    m, d = x2d.shape
    ff = w1.shape[1]
    tm = min(tm, m)
    in_specs = [pl.BlockSpec((tm, d), lambda i: (i, 0)), _const_spec((1, d)),
                _const_spec((d, ff)), _const_spec((ff, d))]
    args = [x2d, g.reshape(1, d), w1, w2]
    if final_g is not None:
        in_specs.append(_const_spec((1, d)))
        args.append(final_g.reshape(1, d))
    return pl.pallas_call(
        functools.partial(_mlp_kernel, nchunk=nchunk),
        out_shape=jax.ShapeDtypeStruct((m, d), F32),
        grid=(m // tm,),
        in_specs=in_specs,
        out_specs=pl.BlockSpec((tm, d), lambda i: (i, 0)),
        compiler_params=_params(("parallel",)),
        name="mlp",
    )(*args)


SUB = 8
SUB16 = 16
DFT_GROUP = 16


def _rms(x, g):
    return (x * lax.rsqrt(jnp.mean(x * x, axis=-1, keepdims=True) + RMS_EPS) * g).astype(BF16)


def _in_proj_kernel(x_ref, g_ref, wh_ref, wr_ref, oh_ref, or_ref, *, n2):
    hn = _rms(x_ref[0], g_ref[...])
    or_ref[0] = _dot(hn, wr_ref[...]).astype(or_ref.dtype)
    ph = _dot(hn, wh_ref[...])
    for j in range(SUB):
        oh_ref[0, :, j, :] = ph[j * n2:(j + 1) * n2]


def _in_proj(x, g, w_hy, w_rw, n1h, n2):
    b, t, d = x.shape
    ph, pr = w_hy.shape[1], w_rw.shape[1]
    tm = SUB * n2
    p_hy, p_rw = pl.pallas_call(
        functools.partial(_in_proj_kernel, n2=n2),
        out_shape=(jax.ShapeDtypeStruct((b, n2, n1h, ph), F32), jax.ShapeDtypeStruct((b, t, pr), BF16)),
        grid=(b, t // tm),
        in_specs=[pl.BlockSpec((1, tm, d), lambda i, j: (i, j, 0)), _const_spec((1, d)),
                  _const_spec((d, ph)), _const_spec((d, pr))],
        out_specs=(pl.BlockSpec((1, n2, SUB, ph), lambda i, j: (i, 0, j, 0)),
                   pl.BlockSpec((1, tm, pr), lambda i, j: (i, j, 0))),
        compiler_params=_params(("parallel", "parallel")),
        name="in_proj",
    )(x, g.reshape(1, d), w_hy, w_rw)
    return p_hy.reshape(b, t, ph), p_rw


def _shortconv_kernel(p_ref, w_ref, b_ref, o_ref, *, n1h, n2):
    w0 = w_ref[0:1, :]
    w1 = w_ref[1:2, :]
    w2 = w_ref[2:3, :]
    bias = b_ref[...]
    rows = lax.broadcasted_iota(jnp.int32, (n1h, 1), 0)

    def blk(i):
        return p_ref[0, pl.ds(pl.multiple_of(i * n1h, n1h), n1h), :].astype(F32)

    def body(i, carry):
        o_ref[0, pl.ds(pl.multiple_of(i * n1h, n1h), n1h), :] = (
            w0 * blk(i - 1) + w1 * blk(i) + w2 * blk(i + 1) + bias).astype(o_ref.dtype)
        return carry

    lax.fori_loop(1, n2 - 1, body, 0)
    last = p_ref[0, (n2 - 1) * n1h:, :].astype(F32)
    first = p_ref[0, :n1h, :].astype(F32)
    prev0 = jnp.where(rows == 0, 0.0, pltpu.roll(last, 1, 0))
    o_ref[0, :n1h, :] = (w0 * prev0 + w1 * first + w2 * p_ref[0, n1h:2 * n1h, :].astype(F32)
                         + bias).astype(o_ref.dtype)
    nxt = jnp.where(rows == n1h - 1, 0.0, pltpu.roll(first, n1h - 1, 0))
    o_ref[0, (n2 - 1) * n1h:, :] = (w0 * p_ref[0, (n2 - 2) * n1h:(n2 - 1) * n1h, :].astype(F32) + w1 * last
                                    + w2 * nxt + bias).astype(o_ref.dtype)


def _shortconv(p, w, bias, n1h, n2, cb=512):
    b, t, c = p.shape
    return pl.pallas_call(
        functools.partial(_shortconv_kernel, n1h=n1h, n2=n2),
        out_shape=jax.ShapeDtypeStruct((b, t, c), BF16),
        grid=(b, c // cb),
        in_specs=[pl.BlockSpec((1, t, cb), lambda i, j: (i, 0, j)),
                  pl.BlockSpec((3, cb), lambda i, j: (0, j)),
                  pl.BlockSpec((1, cb), lambda i, j: (0, j))],
        out_specs=pl.BlockSpec((1, t, cb), lambda i, j: (i, 0, j)),
        compiler_params=_params(("parallel", "parallel")),
        name="hy_shortconv",
    )(p, w, bias.reshape(1, c))


def _dft_tables(t):
    n = 2 * t
    n1 = 1 << ((n.bit_length() - 1) // 2)
    n2 = n // n1
    n1h = n1 // 2
    k1 = np.arange(n1)
    n2i = np.arange(n2)

    def cs(prod, mod):
        ang = (2.0 * np.pi / mod) * (prod % mod)
        return np.cos(ang), np.sin(ang)

    tposf = n2i[:, None] + n2 * np.arange(n1)[None, :]
    cf, sf = cs(k1[None, :, None] * tposf[:, None, :], n)
    g1f = np.concatenate([cf, -sf], axis=1)
    c, s = cf[:, :, :n1h], sf[:, :, :n1h]
    g1 = np.concatenate([np.concatenate([c, s], axis=2), np.concatenate([-s, c], axis=2)], axis=1)
    ct, st = np.swapaxes(c, 1, 2) / n, np.swapaxes(s, 1, 2) / n
    g4a = np.concatenate([ct, st], axis=1)
    g4b = np.concatenate([-st, ct], axis=1)
    c2, s2 = cs(n2i[:, None] * n2i[None, :], n2)
    f2a = np.concatenate([c2, -s2], axis=0)
    f2b = np.concatenate([s2, c2], axis=0)
    f3 = np.concatenate([np.concatenate([c2, -s2], axis=1), np.concatenate([s2, c2], axis=1)], axis=0)
    g4 = np.concatenate([g4a, g4b], axis=2)
    f2 = np.concatenate([f2a, f2b], axis=1)
    const = lambda a: jnp.asarray(a.astype(np.float32).astype(BF16))
    return dict(n1=n1, n2=n2, n1h=n1h, g1=const(g1), g4=const(g4), g1f=const(g1f), f2=const(f2), f3=const(f3))


def _filter_mlp_kernel(z_ref, w1_ref, b1_ref, w2_ref, b2_ref, w3_ref, b3_ref, fr_ref, wo_ref, dl_ref,
                       o_ref, *, t):
    z = z_ref[...]
    h = jnp.sin(fr_ref[0:1, :] * (_dot(z, w1_ref[...], HP) + b1_ref[...]))
    h = jnp.sin(fr_ref[1:2, :] * (_dot(h, w2_ref[...], HP) + b2_ref[...]))
    h = jnp.sin(fr_ref[2:3, :] * (_dot(h, w3_ref[...], HP) + b3_ref[...]))
    window = jnp.exp(-z[:, 0:1] * dl_ref[...])
    tr = z.shape[0]
    row = pl.program_id(0) * tr + lax.broadcasted_iota(jnp.int32, (tr, 1), 0)
    o_ref[...] = jnp.where(row == t, 0.0, _dot(h, wo_ref[0], HP) * window)


def _filter_taps(t, w1, b1, w2, b2, w3, b3, freq, w_out, c_hy, tr=1024):
    emb, width = w1.shape
    bands = (emb - 1) // 2
    pos = np.arange(t, dtype=np.float32)
    tt = pos / np.float32(max(t - 1, 1))
    fr = np.linspace(1e-4, bands - 1, bands, dtype=np.float32)
    ang = np.float32(2.0 * math.pi / t) * pos[:, None] * fr[None, :]
    z = np.concatenate([tt[:, None], np.cos(ang), -np.sin(ang)], axis=-1)
    z = np.pad(z, ((0, 0), (0, 128 - emb)))
    w1 = jnp.pad(w1, ((0, 128 - emb), (0, 0)))
    emb = 128
    z2 = jnp.asarray(np.concatenate([z, z[:1], z[1:][::-1]], axis=0))
    max_decay = math.log(HY_TARGET) / HY_FAST_PCT
    min_decay = math.log(HY_TARGET) / HY_SLOW_PCT
    delta = jnp.abs(jnp.linspace(min_decay, max_decay, c_hy, dtype=F32))
    order = w_out.shape[1] // (2 * c_hy)
    nout = order * c_hy
    w_dir = w_out.reshape(width, order, 2, c_hy).transpose(2, 0, 1, 3).reshape(2, width, nout)
    delta_full = jnp.tile(delta, order).reshape(1, nout)
    tr = min(tr, t)
    nt = t // tr
    full = lambda a: pl.BlockSpec(a.shape, lambda i: (0,) * a.ndim)
    args = [z2, w1, b1.reshape(1, -1), w2, b2.reshape(1, -1), w3, b3.reshape(1, -1), freq, w_dir, delta_full]
    in_specs = [pl.BlockSpec((tr, emb), lambda i: (i, 0))] + [full(a) for a in args[1:]]
    in_specs[8] = pl.BlockSpec((1, width, nout), lambda i: (i // nt, 0, 0))
    return pl.pallas_call(
        functools.partial(_filter_mlp_kernel, t=t),
        out_shape=jax.ShapeDtypeStruct((2 * t, nout), F32),
        grid=(2 * nt,),
        in_specs=in_specs,
        out_specs=pl.BlockSpec((tr, nout), lambda i: (i, 0)),
        compiler_params=_params(("parallel",)),
        name="hy_filter_mlp",
    )(*args)


def _filter_s1_kernel(x_ref, g_ref, o_ref, ss_ref, *, g, oc):
    @pl.when(pl.program_id(0) == 0)
    def _():
        ss_ref[...] = jnp.zeros_like(ss_ref)

    n1 = x_ref.shape[0]
    x = jnp.concatenate([x_ref[:, j, :] for j in range(g)], axis=0)
    ss_ref[...] += jnp.sum(x * x, axis=0, keepdims=True)
    xb = x.astype(BF16)
    for j in range(g):
        o_ref[j] = _dot(g_ref[j], xb[j * n1:(j + 1) * n1])


def _filter_s2_kernel(sr_ref, si_ref, f2_ref, ss_ref, o_ref, *, g, oc):
    n2 = sr_ref.shape[0]
    scale = lax.rsqrt(ss_ref[...] + 1e-6)
    s = jnp.concatenate([ref[:, j, :] for j in range(g) for ref in (sr_ref, si_ref)], axis=0).astype(BF16)
    for j in range(g):
        o_ref[j] = (_dot(f2_ref[...], s[2 * j * n2:2 * (j + 1) * n2]) * scale).astype(o_ref.dtype)


def _filter_spectrum(two, tabs):
    n, oc = two.shape
    n1, n2 = tabs["n1"], tabs["n2"]
    g = SUB
    s1, ss = pl.pallas_call(
        functools.partial(_filter_s1_kernel, g=g, oc=oc),
        out_shape=(jax.ShapeDtypeStruct((n2, 2 * n1, oc), F32), jax.ShapeDtypeStruct((1, oc), F32)),
        grid=(n2 // g,),
        in_specs=[pl.BlockSpec((n1, g, oc), lambda j: (0, j, 0)),
                  pl.BlockSpec((g, 2 * n1, n1), lambda j: (j, 0, 0))],
        out_specs=(pl.BlockSpec((g, 2 * n1, oc), lambda j: (j, 0, 0)),
                   pl.BlockSpec((1, oc), lambda j: (0, 0))),
        compiler_params=_params(("arbitrary",)),
        name="hy_filter_dft1",
    )(two.reshape(n1, n2, oc), tabs["g1f"])
    return pl.pallas_call(
        functools.partial(_filter_s2_kernel, g=g, oc=oc),
        out_shape=jax.ShapeDtypeStruct((n1, 2 * n2, oc), BF16),
        grid=(n1 // g,),
        in_specs=[pl.BlockSpec((n2, g, oc), lambda k: (0, k, 0)),
                  pl.BlockSpec((n2, g, oc), lambda k: (0, n1 // g + k, 0)),
                  pl.BlockSpec((2 * n2, 2 * n2), lambda k: (0, 0)),
                  pl.BlockSpec((1, oc), lambda k: (0, 0))],
        out_specs=pl.BlockSpec((g, 2 * n2, oc), lambda k: (k, 0, 0)),
        compiler_params=_params(("parallel",)),
        name="hy_filter_dft2",
    )(s1, s1, tabs["f2"], ss)


def _conv_s1_kernel(u_ref, g_ref, o_ref, t_ref, *, g, n1h):
    for j0 in range(0, g, DFT_GROUP):
        js = range(j0, j0 + DFT_GROUP)
        rs = [_dot(g_ref[j], jnp.concatenate([u_ref[0, 0, j * n1h:(j + 1) * n1h, :],
                                               u_ref[0, 1, j * n1h:(j + 1) * n1h, :]], axis=0)) for j in js]
        for j, r in zip(js, rs):
            t_ref[:, j, :] = r
    o_ref[0] = t_ref[...].astype(o_ref.dtype)


def _conv_s23_kernel(sr_ref, si_ref, f2_ref, f3_ref, h_ref, o_ref, t_ref, *, g, n2):
    for j0 in range(0, g, DFT_GROUP):
        js = range(j0, j0 + DFT_GROUP)
        xs = [_dot(f2_ref[...], jnp.concatenate([sr_ref[0, j], si_ref[0, j]], axis=0)) for j in js]
        ys = []
        for j, x in zip(js, xs):
            xr, xi = x[:n2], x[n2:]
            hr, hi = h_ref[j, :n2, :].astype(F32), h_ref[j, n2:, :].astype(F32)
            ys.append(jnp.concatenate([xr * hr - xi * hi, xr * hi + xi * hr], axis=0).astype(BF16))
        rs = [_dot(f3_ref[...], y) for y in ys]
        for j, r in zip(js, rs):
            t_ref[:, j, :] = r
    o_ref[0] = t_ref[...].astype(o_ref.dtype)


def _conv_s4_kernel(rr_ref, ri_ref, g4_ref, u_ref, gate_ref, skip_ref, *rest, g, n1h, norm):
    if norm:
        ng_ref, avg_ref, o_ref = rest
    else:
        (o_ref,) = rest
    skip = skip_ref[...]
    ys = [_dot(g4_ref[j], jnp.concatenate([rr_ref[0, j], ri_ref[0, j]], axis=0)) for j in range(g)]
    zs = [gate_ref[0, q, j * n1h:(j + 1) * n1h, :].astype(F32)
          * (ys[j][q * n1h:(q + 1) * n1h] + u_ref[0, q, j * n1h:(j + 1) * n1h, :].astype(F32) * skip)
          for j in range(g) for q in range(2)]
    if norm:
        zz = jnp.concatenate([(z * z).astype(BF16) for z in zs], axis=0)
        ms = _dot(zz, avg_ref[...])
        zs = [z * lax.rsqrt(ms[i * n1h:(i + 1) * n1h] + RMS_EPS) * ng_ref[...] for i, z in enumerate(zs)]
    for j in range(g):
        for q in range(2):
            if norm:
                o_ref[0, q, :, j, :] = zs[2 * j + q]
            else:
                o_ref[0, q, j * n1h:(j + 1) * n1h, :] = zs[2 * j + q].astype(o_ref.dtype)


def _long_conv_gate(u_arr, u_col, gate_arr, gate_col, h_spec, h_col, skip, tabs, c, norm_g=None):
    b, t, _ = u_arr.shape
    n1, n2, n1h = tabs["n1"], tabs["n2"], tabs["n1h"]
    g = SUB16
    npair = b // 2
    u4 = u_arr.reshape(npair, 2, t, u_arr.shape[2])
    gate4 = gate_arr.reshape(npair, 2, t, gate_arr.shape[2])
    gt = SUB16
    s1 = pl.pallas_call(
        functools.partial(_conv_s1_kernel, g=gt, n1h=n1h),
        out_shape=jax.ShapeDtypeStruct((npair, 2 * n1, n2, c), BF16),
        grid=(n2 // gt, npair),
        in_specs=[pl.BlockSpec((1, 2, gt * n1h, c), lambda j, p: (p, 0, j, u_col)),
                  pl.BlockSpec((gt, 2 * n1, n1), lambda j, p: (j, 0, 0))],
        out_specs=pl.BlockSpec((1, 2 * n1, gt, c), lambda j, p: (p, 0, j, 0)),
        scratch_shapes=[pltpu.VMEM((2 * n1, gt, c), F32)],
        compiler_params=_params(("parallel", "parallel")),
        name="hy_conv_dft1",
    )(u4, tabs["g1"])
    r = pl.pallas_call(
        functools.partial(_conv_s23_kernel, g=gt, n2=n2),
        out_shape=jax.ShapeDtypeStruct((npair, 2 * n2, n1, c), BF16),
        grid=(n1 // gt, npair),
        in_specs=[pl.BlockSpec((1, gt, n2, c), lambda k, p: (p, k, 0, 0)),
                  pl.BlockSpec((1, gt, n2, c), lambda k, p: (p, n1 // gt + k, 0, 0)),
                  pl.BlockSpec((2 * n2, 2 * n2), lambda k, p: (0, 0)),
                  pl.BlockSpec((2 * n2, 2 * n2), lambda k, p: (0, 0)),
                  pl.BlockSpec((gt, 2 * n2, c), lambda k, p: (k, 0, h_col))],
        out_specs=pl.BlockSpec((1, 2 * n2, gt, c), lambda k, p: (p, 0, k, 0)),
        scratch_shapes=[pltpu.VMEM((2 * n2, gt, c), F32)],
        compiler_params=_params(("parallel", "parallel")),
        name="hy_conv_dft23",
    )(s1, s1, tabs["f2"], tabs["f3"], h_spec)
    norm = norm_g is not None
    in_specs = [pl.BlockSpec((1, g, n1, c), lambda p, j: (p, j, 0, 0)),
                pl.BlockSpec((1, g, n1, c), lambda p, j: (p, n2 // g + j, 0, 0)),
                pl.BlockSpec((g, n1, 2 * n1), lambda p, j: (j, 0, 0)),
                pl.BlockSpec((1, 2, g * n1h, c), lambda p, j: (p, 0, j, u_col)),
                pl.BlockSpec((1, 2, g * n1h, c), lambda p, j: (p, 0, j, gate_col)),
                pl.BlockSpec((1, c), lambda p, j: (0, 0))]
    args = [r, r, tabs["g4"], u4, gate4, skip.reshape(1, c)]
    if norm:
        ch = jnp.arange(c) // HEAD
        avg = ((ch[:, None] == ch[None, :]).astype(F32) / HEAD).astype(BF16)
        in_specs += [pl.BlockSpec((1, c), lambda p, j: (0, 0)), pl.BlockSpec((c, c), lambda p, j: (0, 0))]
        args += [norm_g.reshape(1, c), avg]
        out_shape = jax.ShapeDtypeStruct((npair, 2, n1h, n2, c), F32)
        out_spec = pl.BlockSpec((1, 2, n1h, g, c), lambda p, j: (p, 0, 0, j, 0))
    else:
        out_shape = jax.ShapeDtypeStruct((npair, 2, t, c), BF16)
        out_spec = pl.BlockSpec((1, 2, g * n1h, c), lambda p, j: (p, 0, j, 0))
    out = pl.pallas_call(
        functools.partial(_conv_s4_kernel, g=g, n1h=n1h, norm=norm),
        out_shape=out_shape,
        grid=(npair, n2 // g),
        in_specs=in_specs,
        out_specs=out_spec,
        compiler_params=_params(("parallel", "parallel")),
        name="hy_conv_dft4",
    )(*args)
    return out.reshape(b, t, c)


def _hyena_branch(p, conv_w, conv_b, fw1, fb1, fw2, fb2, fw3, fb3, ffreq, fwout, skip, norm_g, tabs):
    b, t, _ = p.shape
    c = norm_g.shape[0]
    n1, n2, n1h = tabs["n1"], tabs["n2"], tabs["n1h"]
    u = _shortconv(p, conv_w, conv_b, n1h, n2)
    two = _filter_taps(t, fw1, fb1, fw2, fb2, fw3, fb3, ffreq, fwout, c)
    hspec = _filter_spectrum(two, tabs)
    z = _long_conv_gate(u, 2, u, 0, hspec, 0, skip[0], tabs, c)
    return _long_conv_gate(z, 0, u, 1, hspec, 1, skip[1], tabs, c, norm_g=norm_g)


def _rw_prep_kernel(*refs, has_vres, c):
    if has_vres:
        (p_ref, pp_ref, pn_ref, mu_ref, w0_ref, w2_ref, a0_ref, a2_ref, g2_ref, kk_ref, ka_ref, rk_ref,
         sum_ref, vf_ref, v0_ref, v2_ref,
         r_o, v_o, kk_o, lwf_o, lwb_o, kf_o, kb_o, af_o, ab_o, g_o, bon_o) = refs
    else:
        (p_ref, pp_ref, pn_ref, mu_ref, w0_ref, w2_ref, a0_ref, a2_ref, g2_ref, kk_ref, ka_ref, rk_ref,
         sum_ref,
         r_o, v_o, kk_o, lwf_o, lwb_o, kf_o, kb_o, af_o, ab_o, g_o, bon_o) = refs
    i = pl.program_id(1)
    last = pl.num_programs(1) - 1
    nsh = mu_ref.shape[1]
    p = p_ref[0, :, :nsh].astype(F32)
    tt = p.shape[0]
    rows = lax.broadcasted_iota(jnp.int32, (tt, 1), 0)
    prev_row = jnp.where(i == 0, 0.0, pp_ref[0, SUB16 - 1:SUB16, :nsh].astype(F32))
    next_row = jnp.where(i == last, 0.0, pn_ref[0, 0:1, :nsh].astype(F32))
    prev = jnp.where(rows == 0, prev_row, pltpu.roll(p, 1, 0))
    nxt = jnp.where(rows == tt - 1, next_row, pltpu.roll(p, tt - 1, 0))
    mu0, mu1 = mu_ref[0:1, :], mu_ref[1:2, :]
    pf = p * (1.0 - mu0 - mu1) + mu0 * prev + mu1 * nxt
    r = pf[:, :c]
    k = pf[:, c:2 * c]
    v = pf[:, 2 * c:3 * c]
    lw = 3 * c
    nd = w2_ref.shape[0]
    na = a2_ref.shape[0]
    wd = pf[:, lw:lw + nd]
    ad = pf[:, lw + nd:lw + nd + na]
    gd = pf[:, lw + nd + na:]
    if has_vres:
        lora = _dot(p_ref[0, :, nsh:], v2_ref[...])
        v = v + (vf_ref[0].astype(F32) - v) * jax.nn.sigmoid(v0_ref[...] + lora)
    g = _dot(jax.nn.sigmoid(gd).astype(BF16), g2_ref[...])
    kk = k * kk_ref[...]
    kk = kk * lax.rsqrt(jnp.maximum(_dot((kk * kk).astype(BF16), sum_ref[...]), 1e-24))
    wl = w0_ref[...] + _dot(jnp.tanh(wd).astype(BF16), w2_ref[...])
    logw = -math.exp(-0.5) * jax.nn.sigmoid(wl)
    a = jax.nn.sigmoid(a0_ref[...] + _dot(ad.astype(BF16), a2_ref[...]))
    ka = ka_ref[...]
    rk = rk_ref[...]
    k_d = [k * (1.0 + (a[:, d * c:(d + 1) * c] - 1.0) * ka) for d in range(2)]
    bon = _dot((r * (k_d[0] + k_d[1]) * rk).astype(BF16), sum_ref[...]) * v
    r_o[0] = r.astype(BF16)
    v_o[0] = v.astype(BF16)
    kk_o[0] = kk.astype(BF16)
    lwf_o[0] = logw[:, :c]
    lwb_o[0] = logw[:, c:]
    kf_o[0] = k_d[0].astype(BF16)
    kb_o[0] = k_d[1].astype(BF16)
    af_o[0] = a[:, :c].astype(BF16)
    ab_o[0] = a[:, c:].astype(BF16)
    g_o[0] = g.astype(BF16)
    bon_o[0] = bon.astype(BF16)


def _blockdiag2(m):
    k, c = m.shape[1], m.shape[2]
    z = jnp.zeros((k, c), m.dtype)
    return jnp.concatenate([jnp.concatenate([m[0], z], axis=1), jnp.concatenate([z, m[1]], axis=1)], axis=0)


def _rw_prep(p, v_first, shift, w0, w2, a0, a2, g2, k_k, k_a, r_k, vres, c, tt=1024):
    b, t, pw = p.shape
    tt = min(tt, t)
    has_vres = vres is not None
    ch = jnp.arange(c) // HEAD
    summ = (ch[:, None] == ch[None, :]).astype(BF16)
    row = lambda a: a.reshape(1, -1)
    args = [p, p, p, shift, row(w0), _blockdiag2(w2).astype(BF16), row(a0), _blockdiag2(a2).astype(BF16),
            g2.astype(BF16), row(k_k), row(k_a), row(r_k), summ]
    full = lambda a: pl.BlockSpec(a.shape, lambda i, j: (0,) * a.ndim)
    nhb = t // SUB16
    in_specs = [pl.BlockSpec((1, tt, pw), lambda i, j: (i, j, 0)),
                pl.BlockSpec((1, SUB16, pw), lambda i, j: (i, jnp.maximum(j * (tt // SUB16) - 1, 0), 0)),
                pl.BlockSpec((1, SUB16, pw), lambda i, j: (i, jnp.minimum((j + 1) * (tt // SUB16), nhb - 1), 0))]
    in_specs += [full(a) for a in args[3:]]
    if has_vres:
        v0, _, v2 = vres
        v2p = jnp.pad(v2, ((0, pw - shift.shape[1] - v2.shape[0]), (0, 0))).astype(BF16)
        extra = [v_first, row(v0), v2p]
        in_specs += [pl.BlockSpec((1, tt, c), lambda i, j: (i, j, 0))] + [full(a) for a in extra[1:]]
        args += extra
    outs = tuple(jax.ShapeDtypeStruct((b, t, c), F32 if i in (3, 4) else BF16) for i in range(11))
    return pl.pallas_call(
        functools.partial(_rw_prep_kernel, has_vres=has_vres, c=c),
        out_shape=outs,
        grid=(b, t // tt),
        in_specs=in_specs,
        out_specs=tuple(pl.BlockSpec((1, tt, c), lambda i, j: (i, j, 0)) for _ in range(11)),
        compiler_params=_params(("parallel", "parallel")),
        name="rw_prep",
    )(*args)


def _scan_chunks(s0, r, k, v, kk, a, lw, rev, bdm):
    L = CHUNK
    n = len(r)
    each = lambda f, *ls: [f(*xs) for xs in zip(*ls)]
    ti = lax.broadcasted_iota(jnp.int32, (L, L), 0)
    si = lax.broadcasted_iota(jnp.int32, (L, L), 1)
    tt = lax.broadcasted_iota(jnp.int32, (L, GW), 0)
    ss = lax.broadcasted_iota(jnp.int32, (L, GW), 1) % L
    eye = (ss == tt).astype(F32)
    tri = [((si >= ti) if q else (si <= ti)).astype(F32).astype(BF16) for q in rev]
    strict = [(ss > tt) if q else (ss < tt) for q in rev]
    incl = [(ss >= tt) if q else (ss <= tt) for q in rev]
    bdf = bdm.astype(F32)

    def bd(x):
        xb = x.astype(BF16)
        return jnp.concatenate([xb] * HEADS_PER_GROUP, axis=0) * bdm

    def bd_t(x):
        xt = x.T.astype(BF16)
        return jnp.concatenate([xt] * HEADS_PER_GROUP, axis=1) * bdm

    def rcmul(x, y):
        return _dot(x.astype(BF16), bd(y))

    parts = each(_split3, lw)
    cum = [_dot(t, p[0]) + _dot(t, p[1]) + _dot(t, p[2]) for t, p in zip(tri, parts)]
    tot = [c[0:1, :] if q else c[L - 1:L, :] for c, q in zip(cum, rev)]
    p_in = each(jnp.exp, cum)
    p_inv = each(lambda c: jnp.exp(-c), cum)
    p_ex = each(lambda c, w: jnp.exp(c - w), cum, lw)
    p_rem = each(lambda t, c: jnp.exp(t - c), tot, cum)
    beta = each(lambda x, y: x * y, kk, a)
    a_t = each(lambda x, p: -x * p, kk, p_ex)
    r_t = each(lambda x, p: x * p, r, p_in)
    k_t = each(lambda x, p: x * p, k, p_inv)
    b_t = each(lambda x, p: x * p, beta, p_inv)
    k_h = each(lambda x, p: x * p, k, p_rem)
    b_h = each(lambda x, p: x * p, beta, p_rem)

    lhs = each(lambda x, y: jnp.concatenate([x, y], axis=0).astype(BF16), a_t, r_t)
    akk = each(lambda x, y: _dot(x, bd_t(y)), lhs, k_t)
    abb = each(lambda x, y: _dot(x, bd_t(y)), lhs, b_t)
    a_ak = each(lambda m, x: jnp.where(m, x[:L], 0.0), strict, akk)
    a_rk = each(lambda m, x: jnp.where(m, x[L:], 0.0), incl, akk)
    a_ab = each(lambda m, x: jnp.where(m, x[:L], 0.0), strict, abb)
    a_rb = each(lambda m, x: jnp.where(m, x[L:], 0.0), incl, abb)

    same = lambda size: (tt // size) == (ss // size)
    d = each(lambda x: jnp.where(same(INV_BASE), x, 0.0), a_ab)
    tinv = each(lambda x: eye + x, d)
    pw = each(rcmul, d, d)
    size = 4
    while size < INV_BASE:
        both = each(lambda p, t: rcmul(jnp.concatenate([p, t], axis=0), p), pw, tinv)
        pw = [x[:L] for x in both]
        tinv = each(lambda t, x: t + x[L:], tinv, both)
        size *= 2
    tinv = each(lambda t, p: t + rcmul(t, p), tinv, pw)
    size = INV_BASE
    while size < L // 2:
        size *= 2
        couple = same(size) & ~same(size // 2)
        e = each(lambda x: jnp.where(couple, x, 0.0), a_ab)
        tinv = each(lambda t, y: t + rcmul(rcmul(t, y), t), tinv, e)
    couple = ~same(L // 2)
    e = each(lambda x: jnp.where(couple, x, 0.0), a_ab)
    g1 = each(lambda y, c, t: rcmul(jnp.concatenate([y, c], axis=0), t), e, a_rb, tinv)
    g2 = each(lambda t, g: rcmul(jnp.concatenate([t, g[L:]], axis=0), g[:L]), tinv, g1)
    ct = each(lambda g, h: g[L:] + h[L:], g1, g2)
    tinv = each(lambda t, h: t + h[:L], tinv, g2)

    av = each(lambda x, y, z: rcmul(jnp.concatenate([x, y], axis=0), z), a_ak, a_rk, v)
    wu2 = each(lambda t, c, x, y: _dot(jnp.concatenate([t, c], axis=0).astype(BF16),
                                       jnp.concatenate([bd(x), bd(y[:L])], axis=1)), tinv, ct, a_t, av)
    wu = [x[:L] for x in wu2]
    rbwu = [x[L:] for x in wu2]
    o = each(lambda x, w, s, y: _dot((x + w[:, :GW]).astype(BF16), bd_t(s)) + y[L:] + w[:, GW:],
             r_t, rbwu, s0, av)

    m_bd = each(lambda w, x: _dot_tn(w[:, :GW].astype(BF16), x.astype(BF16)) * bdf, wu, b_h)
    n_full = each(lambda x, w, y, z: _dot_tn(jnp.concatenate([x, w[:, GW:]], axis=0).astype(BF16),
                                             jnp.concatenate([y, z], axis=0).astype(BF16)) * bdf,
                  v, wu, k_h, b_h)
    s1 = []
    for i in range(n):
        n_rc = n_full[i][0:HEAD]
        for h in range(1, HEADS_PER_GROUP):
            n_rc = n_rc + n_full[i][h * HEAD:(h + 1) * HEAD]
        s1.append(_dot(s0[i].astype(BF16), m_bd[i].astype(BF16)) + s0[i] * jnp.exp(tot[i]) + n_rc)
    return s1, o


def _scan_kernel(rf_ref, kf_ref, vf_ref, kkf_ref, af_ref, lwf_ref,
                 rb_ref, kb_ref, vb_ref, kkb_ref, ab_ref, lwb_ref, of_ref, ob_ref, s_ref, *, ng, nb):
    @pl.when(pl.program_id(1) == 0)
    def _():
        s_ref[...] = jnp.zeros_like(s_ref)

    gi = lax.broadcasted_iota(jnp.int32, (GW, GW), 0) // HEAD
    gj = lax.broadcasted_iota(jnp.int32, (GW, GW), 1) // HEAD
    bdm = (gi == gj).astype(F32).astype(BF16)
    dirs = ((rf_ref, kf_ref, vf_ref, kkf_ref, af_ref, lwf_ref), (rb_ref, kb_ref, vb_ref, kkb_ref, ab_ref, lwb_ref))
    chains = [(i, d, h) for i in range(nb) for d in range(2) for h in range(ng)]
    ins = [[dirs[d][q][i, :, h * GW:(h + 1) * GW].astype(F32) for i, d, h in chains] for q in range(6)]
    s1, o = _scan_chunks([s_ref[i, d, h] for i, d, h in chains], *ins, [d == 1 for _, d, _ in chains], bdm)
    for (i, d, h), s_new, o_new in zip(chains, s1, o):
        (of_ref, ob_ref)[d][i, :, h * GW:(h + 1) * GW] = o_new
        s_ref[i, d, h] = s_new


SCAN_BATCH_ROWS = 4


def _wkv_scan(r, v, kk, kf, af, lwf, kb, ab, lwb):
    b, t, c = r.shape
    nc = t // CHUNK
    ng = c // GW
    nb = math.gcd(b, SCAN_BATCH_ROWS)
    fspec = pl.BlockSpec((nb, CHUNK, c), lambda i, j: (i, j, 0))
    bspec = pl.BlockSpec((nb, CHUNK, c), lambda i, j: (i, nc - 1 - j, 0))
    return pl.pallas_call(
        functools.partial(_scan_kernel, ng=ng, nb=nb),
        out_shape=(jax.ShapeDtypeStruct((b, t, c), F32), jax.ShapeDtypeStruct((b, t, c), F32)),
        grid=(b // nb, nc),
        in_specs=[fspec] * 6 + [bspec] * 6,
        out_specs=(fspec, bspec),
        scratch_shapes=[pltpu.VMEM((nb, 2, ng, HEAD, GW), F32)],
        compiler_params=_params(("parallel", "arbitrary")),
        name="rw_scan",
    )(r, kf, v, kk, af, lwf, r, kb, v, kk, ab, lwb)


def _out_proj_kernel(yh_ref, sf_ref, sb_ref, bon_ref, g_ref, lg_ref, lb_ref, avg_ref, wh_ref, wr_ref, r_ref,
                     o_ref):
    s = sf_ref[...] + sb_ref[...]
    mean = _dot3(s, avg_ref[...])
    d = s - mean
    var = _dot((d * d).astype(BF16), avg_ref[...])
    y = d * lax.rsqrt(var + GN_EPS) * lg_ref[...] + lb_ref[...] + bon_ref[...].astype(F32)
    yr = (y * g_ref[...].astype(F32)).astype(BF16)
    o_ref[...] = _dot(yh_ref[...].astype(BF16), wh_ref[...]) + _dot(yr, wr_ref[...]) + r_ref[...]


def _out_proj(yh, sf, sb, bon, g, lnx_g, lnx_b, w, res, tm=1024):
    m, ch = yh.shape
    cr = sf.shape[1]
    d = w.shape[1]
    tm = min(tm, m)
    head = jnp.arange(cr) // HEAD
    avg = ((head[:, None] == head[None, :]).astype(F32) / HEAD).astype(BF16)
    row = lambda c: pl.BlockSpec((tm, c), lambda i: (i, 0))
    return pl.pallas_call(
        _out_proj_kernel,
        out_shape=jax.ShapeDtypeStruct((m, d), F32),
        grid=(m // tm,),
        in_specs=[row(ch), row(cr), row(cr), row(cr), row(cr), _const_spec((1, cr)), _const_spec((1, cr)),
                  _const_spec((cr, cr)), _const_spec((ch, d)), _const_spec((cr, d)), row(d)],
        out_specs=row(d),
        compiler_params=_params(("parallel",)),
        name="out_proj",
    )(yh, sf, sb, bon, g, lnx_g.reshape(1, cr), lnx_b.reshape(1, cr), avg, w[:ch], w[ch:], res)


def _rwkv_branch(p, v_first, shift, w0, w2, a0, a2, g2, k_k, k_a, r_k, vres, c):
    r, v, kk, lwf, lwb, kf, kb, af, ab, g, bon = _rw_prep(
        p, v_first, shift, w0, w2, a0, a2, g2, k_k, k_a, r_k, vres, c)
    sf, sb = _wkv_scan(r, v, kk, kf, af, lwf, kb, ab, lwb)
    return sf, sb, bon, g, (v if vres is None else v_first)


def _forward(x, norm1_g, w_in, hy_conv_w, hy_conv_b, hy_f_w1, hy_f_b1, hy_f_w2, hy_f_b2, hy_f_w3, hy_f_b3,
             hy_f_freq, hy_f_wout, hy_skip, hy_norm_g, rw_shift, rw_w0, rw_w2, rw_a0, rw_a2, rw_g2, rw_k_k,
             rw_k_a, rw_r_k, rw_lnx_g, rw_lnx_b, rw_v0, rw_v1, rw_v2, w_out, norm2_g, mlp_w1, mlp_w2,
             final_g):
    b, t, d = x.shape
    depth = w_in.shape[0]
    c_hy = hy_norm_g.shape[1]
    c_rw = rw_lnx_g.shape[1]
    hy_proj = hy_conv_b.shape[1]
    assert b % 2 == 0 and t % CHUNK == 0 and CHUNK == HEAD
    tabs = _dft_tables(t)
    assert tabs["n2"] % SUB16 == 0 and tabs["n1"] % SUB16 == 0 and tabs["n1h"] % SUB == 0
    v_first = None
    for l in range(depth):
        w_l = w_in[l].astype(BF16)
        w_rw = w_l[:, hy_proj:]
        vres = None if l == 0 else (rw_v0[l - 1], rw_v1[l - 1], rw_v2[l - 1])
        if vres is not None:
            v1 = vres[1].astype(BF16)
            w_rw = jnp.concatenate([w_rw, jnp.pad(v1, ((0, 0), (0, -v1.shape[1] % 128)))], axis=1)
        p_hy, p_rw = _in_proj(x, norm1_g[l], w_l[:, :hy_proj], w_rw, tabs["n1h"], tabs["n2"])
        y_hy = _hyena_branch(p_hy, hy_conv_w[l], hy_conv_b[l], hy_f_w1[l], hy_f_b1[l], hy_f_w2[l], hy_f_b2[l],
                             hy_f_w3[l], hy_f_b3[l], hy_f_freq[l], hy_f_wout[l], hy_skip[l], hy_norm_g[l], tabs)
        sf, sb, bon, g, v_first = _rwkv_branch(p_rw, v_first, rw_shift[l], rw_w0[l], rw_w2[l], rw_a0[l], rw_a2[l],
                                               rw_g2[l], rw_k_k[l], rw_k_a[l], rw_r_k[l], vres, c_rw)
        flat = lambda a: a.reshape(b * t, a.shape[-1])
        x = _out_proj(flat(y_hy), flat(sf), flat(sb), flat(bon), flat(g), rw_lnx_g[l], rw_lnx_b[l],
                      w_out[l].astype(BF16), flat(x))
        x = _mlp(x, norm2_g[l], mlp_w1[l].astype(BF16), mlp_w2[l].astype(BF16),
                 final_g=final_g if l == depth - 1 else None).reshape(b, t, d)
    return x


def kernel(x, norm1_g, w_in, hy_conv_w, hy_conv_b, hy_f_w1, hy_f_b1, hy_f_w2, hy_f_b2, hy_f_w3, hy_f_b3,
           hy_f_freq, hy_f_wout, hy_skip, hy_norm_g, rw_shift, rw_w0, rw_w2, rw_a0, rw_a2, rw_g2, rw_k_k,
           rw_k_a, rw_r_k, rw_lnx_g, rw_lnx_b, rw_v0, rw_v1, rw_v2, w_out, norm2_g, mlp_w1, mlp_w2, final_g):
    return _forward(x, norm1_g, w_in, hy_conv_w, hy_conv_b, hy_f_w1, hy_f_b1, hy_f_w2, hy_f_b2, hy_f_w3,
                    hy_f_b3, hy_f_freq, hy_f_wout, hy_skip, hy_norm_g, rw_shift, rw_w0, rw_w2, rw_a0, rw_a2,
                    rw_g2, rw_k_k, rw_k_a, rw_r_k, rw_lnx_g, rw_lnx_b, rw_v0, rw_v1, rw_v2, w_out, norm2_g,
                    mlp_w1, mlp_w2, final_g)
```
